```python
import jax, jax.numpy as jnp
from jax import lax
import numpy as np

D_MODEL = 1024
BATCH = 8
SEQ = 4096
DEPTH = 1

CHUNK = 64
N_META = 16
Q_BLOCK = 128
D_CONV = 512
CONV_WIDTH = 31
N_HEADS = 8
QK_NOPE = 64
QK_ROPE = 32
V_HEAD = 64
D_ATTN = N_HEADS * V_HEAD
Q_LORA = 384
KV_LORA = 256
ROPE_THETA = 10000.0
D_MIX = D_CONV + D_ATTN
D_IN = 2 * D_CONV + Q_LORA + KV_LORA + QK_ROPE
D_FF = 2816
FFN_CONV_WIDTH = 3
EPS = 1e-6
NEG = -1e30

kernel_name = 'hymba_conformer_mla_convffn_block'


def rms_norm(x, g):
    xf = x.astype(jnp.float32)
    y = xf * lax.rsqrt(jnp.mean(xf * xf, axis=-1, keepdims=True) + EPS)
    return (y * g.astype(jnp.float32)).astype(x.dtype)


def layer_norm(x, g, b):
    xf = x.astype(jnp.float32)
    mu = jnp.mean(xf, axis=-1, keepdims=True)
    var = jnp.mean(jnp.square(xf - mu), axis=-1, keepdims=True)
    y = (xf - mu) * lax.rsqrt(var + EPS)
    return (y * g.astype(jnp.float32) + b.astype(jnp.float32)).astype(x.dtype)


def causal_depthwise_conv(x, w, b):
    k = w.shape[0]
    y = lax.conv_general_dilated(
        x, w[:, None, :].astype(x.dtype), window_strides=(1,), padding=[(k - 1, 0)],
        dimension_numbers=('NWC', 'WIO', 'NWC'), feature_group_count=x.shape[-1])
    return y + b.astype(x.dtype)


def rope(x, cos, sin):
    half = x.shape[-1] // 2
    x1, x2 = x[..., :half], x[..., half:]
    return jnp.concatenate([x1 * cos - x2 * sin, x2 * cos + x1 * sin], axis=-1)


def block_causal_attention(q, k, v, chunk_id):
    b, l, h, dqk = q.shape
    dv = v.shape[-1]
    nblk = l // Q_BLOCK
    qb = q.reshape(b, nblk, Q_BLOCK, h, dqk).transpose(1, 0, 2, 3, 4)
    cb = chunk_id.reshape(nblk, Q_BLOCK)
    scale = dqk ** -0.5

    def one_block(args):
        qi, ci = args
        s = jnp.einsum('bqhd,bkhd->bhqk', qi, k, preferred_element_type=jnp.float32) * scale
        visible = ci[:, None] >= chunk_id[None, :]
        s = jnp.where(visible[None, None], s, NEG)
        p = jax.nn.softmax(s, axis=-1)
        return jnp.einsum('bhqk,bkhd->bqhd', p.astype(v.dtype), v)

    o = lax.map(one_block, (qb, cb))
    return o.transpose(1, 0, 2, 3, 4).reshape(b, l, h * dv)


def hybrid_layer(h, cos, sin, chunk_id, mix_norm_g, w_in, q_norm_g, w_uq, kv_norm_g, w_ukv,
                 conv_w, conv_b, conv_ln_g, conv_ln_b, conv_out_g, attn_out_g, w_out,
                 ffn_norm_g, w_ffn_up, ffn_conv_w, ffn_conv_b, w_ffn_down):
    b, l, _ = h.shape
    n = rms_norm(h, mix_norm_g)
    z = n @ w_in.astype(h.dtype)
    a, gate, c_q, c_kv, k_r = jnp.split(
        z, [D_CONV, 2 * D_CONV, 2 * D_CONV + Q_LORA, 2 * D_CONV + Q_LORA + KV_LORA], axis=-1)

    u = a * jax.nn.sigmoid(gate)
    u = causal_depthwise_conv(u, conv_w, conv_b)
    u = jax.nn.silu(layer_norm(u, conv_ln_g, conv_ln_b))

    q = (rms_norm(c_q, q_norm_g) @ w_uq.astype(h.dtype)).reshape(b, l, N_HEADS, QK_NOPE + QK_ROPE)
    kv = (rms_norm(c_kv, kv_norm_g) @ w_ukv.astype(h.dtype)).reshape(b, l, N_HEADS, QK_NOPE + V_HEAD)
    q_nope, q_rot = q[..., :QK_NOPE], q[..., QK_NOPE:]
    k_nope, v = kv[..., :QK_NOPE], kv[..., QK_NOPE:]
    q_rot = rope(q_rot, cos[:, None, :], sin[:, None, :])
    k_rot = rope(k_r, cos, sin)
    qf = jnp.concatenate([q_nope, q_rot], axis=-1)
    kf = jnp.concatenate(
        [k_nope, jnp.broadcast_to(k_rot[:, :, None, :], (b, l, N_HEADS, QK_ROPE))], axis=-1)
    o = block_causal_attention(qf, kf, v, chunk_id)

    mix = jnp.concatenate([rms_norm(u, conv_out_g), rms_norm(o, attn_out_g)], axis=-1)
    h = h + mix @ w_out.astype(h.dtype)

    n2 = rms_norm(h, ffn_norm_g)
    up = causal_depthwise_conv(n2 @ w_ffn_up.astype(h.dtype), ffn_conv_w, ffn_conv_b)
    g, val = up[..., :D_FF], up[..., D_FF:]
    return h + (jax.nn.silu(g) * val) @ w_ffn_down.astype(h.dtype)


def _fwd_setup_inputs(seed: int = 0) -> dict:
    key = jax.random.key(seed)
    ks = jax.random.split(key, 24)
    f32 = jnp.float32

    def nrm(k, shape, scale):
        return jax.random.normal(k, shape, f32) * scale

    def gain(k, shape):
        return 1.0 + 0.02 * jax.random.normal(k, shape, f32)

    L = DEPTH
    return {
        'x': jax.random.normal(ks[0], (BATCH, SEQ, D_MODEL), f32),
        'meta_tokens': nrm(ks[1], (N_META, D_MODEL), 1.0),
        'mix_norm_g': gain(ks[2], (L, D_MODEL)),
        'w_in': nrm(ks[3], (L, D_MODEL, D_IN), D_MODEL ** -0.5),
        'q_norm_g': gain(ks[4], (L, Q_LORA)),
        'w_uq': nrm(ks[5], (L, Q_LORA, N_HEADS * (QK_NOPE + QK_ROPE)), Q_LORA ** -0.5),
        'kv_norm_g': gain(ks[6], (L, KV_LORA)),
        'w_ukv': nrm(ks[7], (L, KV_LORA, N_HEADS * (QK_NOPE + V_HEAD)), KV_LORA ** -0.5),
        'conv_w': nrm(ks[8], (L, CONV_WIDTH, D_CONV), CONV_WIDTH ** -0.5),
        'conv_b': nrm(ks[9], (L, D_CONV), 0.02),
        'conv_ln_g': gain(ks[10], (L, D_CONV)),
        'conv_ln_b': nrm(ks[11], (L, D_CONV), 0.02),
        'conv_out_g': gain(ks[12], (L, D_CONV)),
        'attn_out_g': gain(ks[13], (L, D_ATTN)),
        'w_out': nrm(ks[14], (L, D_MIX, D_MODEL), D_MIX ** -0.5),
        'ffn_norm_g': gain(ks[15], (L, D_MODEL)),
        'w_ffn_up': nrm(ks[16], (L, D_MODEL, 2 * D_FF), D_MODEL ** -0.5),
        'ffn_conv_w': nrm(ks[17], (L, FFN_CONV_WIDTH, 2 * D_FF), FFN_CONV_WIDTH ** -0.5),
        'ffn_conv_b': nrm(ks[18], (L, 2 * D_FF), 0.02),
        'w_ffn_down': nrm(ks[19], (L, D_FF, D_MODEL), D_FF ** -0.5),
        'final_norm_g': gain(ks[20], (D_MODEL,)),
    }


def _fwd_reference(x, meta_tokens, mix_norm_g, w_in, q_norm_g, w_uq, kv_norm_g, w_ukv,
              conv_w, conv_b, conv_ln_g, conv_ln_b, conv_out_g, attn_out_g, w_out,
              ffn_norm_g, w_ffn_up, ffn_conv_w, ffn_conv_b, w_ffn_down, final_norm_g):
    b, s, d = x.shape
    l_real = N_META + s
    l_pad = ((l_real + Q_BLOCK - 1) // Q_BLOCK) * Q_BLOCK
    meta = jnp.broadcast_to(meta_tokens[None].astype(x.dtype), (b, N_META, d))
    pad = jnp.zeros((b, l_pad - l_real, d), x.dtype)
    h = jnp.concatenate([meta, x, pad], axis=1)

    pos = jnp.arange(l_pad, dtype=jnp.int32)
    chunk_id = jnp.where(pos < N_META, 0, 1 + (pos - N_META) // CHUNK).astype(jnp.int32)
    inv_freq = 1.0 / (ROPE_THETA ** (jnp.arange(0, QK_ROPE, 2, dtype=jnp.float32) / QK_ROPE))
    ang = pos.astype(jnp.float32)[:, None] * inv_freq[None, :]
    cos = jnp.cos(ang).astype(x.dtype)
    sin = jnp.sin(ang).astype(x.dtype)

    for i in range(DEPTH):
        h = hybrid_layer(h, cos, sin, chunk_id, mix_norm_g[i], w_in[i], q_norm_g[i], w_uq[i],
                         kv_norm_g[i], w_ukv[i], conv_w[i], conv_b[i], conv_ln_g[i], conv_ln_b[i],
                         conv_out_g[i], attn_out_g[i], w_out[i], ffn_norm_g[i], w_ffn_up[i],
                         ffn_conv_w[i], ffn_conv_b[i], w_ffn_down[i])

    h = rms_norm(h, final_norm_g)
    return h[:, N_META:N_META + s]


import jax as _jax
import jax.numpy as _jnp

TWIN_FORMAT = 'train_step'
FWD_PARAMS = ['x', 'meta_tokens', 'mix_norm_g', 'w_in', 'q_norm_g', 'w_uq', 'kv_norm_g', 'w_ukv', 'conv_w', 'conv_b', 'conv_ln_g', 'conv_ln_b', 'conv_out_g', 'attn_out_g', 'w_out', 'ffn_norm_g', 'w_ffn_up', 'ffn_conv_w', 'ffn_conv_b', 'w_ffn_down', 'final_norm_g']
TWIN_WEIGHTS = ['meta_tokens', 'mix_norm_g', 'w_in', 'q_norm_g', 'w_uq', 'kv_norm_g', 'w_ukv', 'conv_w', 'conv_b', 'conv_ln_g', 'conv_ln_b', 'conv_out_g', 'attn_out_g', 'w_out', 'ffn_norm_g', 'w_ffn_up', 'ffn_conv_w', 'ffn_conv_b', 'w_ffn_down', 'final_norm_g']
TWIN_DIFF_INPUT = 'x'
TWIN_INPUTS = ['x', 'meta_tokens', 'mix_norm_g', 'w_in', 'q_norm_g', 'w_uq', 'kv_norm_g', 'w_ukv', 'conv_w', 'conv_b', 'conv_ln_g', 'conv_ln_b', 'conv_out_g', 'attn_out_g', 'w_out', 'ffn_norm_g', 'w_ffn_up', 'ffn_conv_w', 'ffn_conv_b', 'w_ffn_down', 'final_norm_g', 'loss_target', 'm_meta_tokens', 'm_mix_norm_g', 'm_w_in', 'm_q_norm_g', 'm_w_uq', 'm_kv_norm_g', 'm_w_ukv', 'm_conv_w', 'm_conv_b', 'm_conv_ln_g', 'm_conv_ln_b', 'm_conv_out_g', 'm_attn_out_g', 'm_w_out', 'm_ffn_norm_g', 'm_w_ffn_up', 'm_ffn_conv_w', 'm_ffn_conv_b', 'm_w_ffn_down', 'm_final_norm_g', 'v_meta_tokens', 'v_mix_norm_g', 'v_w_in', 'v_q_norm_g', 'v_w_uq', 'v_kv_norm_g', 'v_w_ukv', 'v_conv_w', 'v_conv_b', 'v_conv_ln_g', 'v_conv_ln_b', 'v_conv_out_g', 'v_attn_out_g', 'v_w_out', 'v_ffn_norm_g', 'v_w_ffn_up', 'v_ffn_conv_w', 'v_ffn_conv_b', 'v_w_ffn_down', 'v_final_norm_g']
TWIN_OUTPUTS = ['loss', 'grad_x', 'grad_meta_tokens', 'grad_mix_norm_g', 'grad_w_in', 'grad_q_norm_g', 'grad_w_uq', 'grad_kv_norm_g', 'grad_w_ukv', 'grad_conv_w', 'grad_conv_b', 'grad_conv_ln_g', 'grad_conv_ln_b', 'grad_conv_out_g', 'grad_attn_out_g', 'grad_w_out', 'grad_ffn_norm_g', 'grad_w_ffn_up', 'grad_ffn_conv_w', 'grad_ffn_conv_b', 'grad_w_ffn_down', 'grad_final_norm_g', 'delta_meta_tokens', 'delta_mix_norm_g', 'delta_w_in', 'delta_q_norm_g', 'delta_w_uq', 'delta_kv_norm_g', 'delta_w_ukv', 'delta_conv_w', 'delta_conv_b', 'delta_conv_ln_g', 'delta_conv_ln_b', 'delta_conv_out_g', 'delta_attn_out_g', 'delta_w_out', 'delta_ffn_norm_g', 'delta_w_ffn_up', 'delta_ffn_conv_w', 'delta_ffn_conv_b', 'delta_w_ffn_down', 'delta_final_norm_g', 'new_m_meta_tokens', 'new_m_mix_norm_g', 'new_m_w_in', 'new_m_q_norm_g', 'new_m_w_uq', 'new_m_kv_norm_g', 'new_m_w_ukv', 'new_m_conv_w', 'new_m_conv_b', 'new_m_conv_ln_g', 'new_m_conv_ln_b', 'new_m_conv_out_g', 'new_m_attn_out_g', 'new_m_w_out', 'new_m_ffn_norm_g', 'new_m_w_ffn_up', 'new_m_ffn_conv_w', 'new_m_ffn_conv_b', 'new_m_w_ffn_down', 'new_m_final_norm_g', 'new_v_meta_tokens', 'new_v_mix_norm_g', 'new_v_w_in', 'new_v_q_norm_g', 'new_v_w_uq', 'new_v_kv_norm_g', 'new_v_w_ukv', 'new_v_conv_w', 'new_v_conv_b', 'new_v_conv_ln_g', 'new_v_conv_ln_b', 'new_v_conv_out_g', 'new_v_attn_out_g', 'new_v_w_out', 'new_v_ffn_norm_g', 'new_v_w_ffn_up', 'new_v_ffn_conv_w', 'new_v_ffn_conv_b', 'new_v_w_ffn_down', 'new_v_final_norm_g']
TWIN_LEAF_KINDS = {'loss': 'loss', 'grad_x': 'grad_x', 'grad_meta_tokens': 'grad_w', 'grad_mix_norm_g': 'grad_w', 'grad_w_in': 'grad_w', 'grad_q_norm_g': 'grad_w', 'grad_w_uq': 'grad_w', 'grad_kv_norm_g': 'grad_w', 'grad_w_ukv': 'grad_w', 'grad_conv_w': 'grad_w', 'grad_conv_b': 'grad_w', 'grad_conv_ln_g': 'grad_w', 'grad_conv_ln_b': 'grad_w', 'grad_conv_out_g': 'grad_w', 'grad_attn_out_g': 'grad_w', 'grad_w_out': 'grad_w', 'grad_ffn_norm_g': 'grad_w', 'grad_w_ffn_up': 'grad_w', 'grad_ffn_conv_w': 'grad_w', 'grad_ffn_conv_b': 'grad_w', 'grad_w_ffn_down': 'grad_w', 'grad_final_norm_g': 'grad_w', 'delta_meta_tokens': 'delta_w', 'delta_mix_norm_g': 'delta_w', 'delta_w_in': 'delta_w', 'delta_q_norm_g': 'delta_w', 'delta_w_uq': 'delta_w', 'delta_kv_norm_g': 'delta_w', 'delta_w_ukv': 'delta_w', 'delta_conv_w': 'delta_w', 'delta_conv_b': 'delta_w', 'delta_conv_ln_g': 'delta_w', 'delta_conv_ln_b': 'delta_w', 'delta_conv_out_g': 'delta_w', 'delta_attn_out_g': 'delta_w', 'delta_w_out': 'delta_w', 'delta_ffn_norm_g': 'delta_w', 'delta_w_ffn_up': 'delta_w', 'delta_ffn_conv_w': 'delta_w', 'delta_ffn_conv_b': 'delta_w', 'delta_w_ffn_down': 'delta_w', 'delta_final_norm_g': 'delta_w', 'new_m_meta_tokens': 'new_m', 'new_m_mix_norm_g': 'new_m', 'new_m_w_in': 'new_m', 'new_m_q_norm_g': 'new_m', 'new_m_w_uq': 'new_m', 'new_m_kv_norm_g': 'new_m', 'new_m_w_ukv': 'new_m', 'new_m_conv_w': 'new_m', 'new_m_conv_b': 'new_m', 'new_m_conv_ln_g': 'new_m', 'new_m_conv_ln_b': 'new_m', 'new_m_conv_out_g': 'new_m', 'new_m_attn_out_g': 'new_m', 'new_m_w_out': 'new_m', 'new_m_ffn_norm_g': 'new_m', 'new_m_w_ffn_up': 'new_m', 'new_m_ffn_conv_w': 'new_m', 'new_m_ffn_conv_b': 'new_m', 'new_m_w_ffn_down': 'new_m', 'new_m_final_norm_g': 'new_m', 'new_v_meta_tokens': 'new_v', 'new_v_mix_norm_g': 'new_v', 'new_v_w_in': 'new_v', 'new_v_q_norm_g': 'new_v', 'new_v_w_uq': 'new_v', 'new_v_kv_norm_g': 'new_v', 'new_v_w_ukv': 'new_v', 'new_v_conv_w': 'new_v', 'new_v_conv_b': 'new_v', 'new_v_conv_ln_g': 'new_v', 'new_v_conv_ln_b': 'new_v', 'new_v_conv_out_g': 'new_v', 'new_v_attn_out_g': 'new_v', 'new_v_w_out': 'new_v', 'new_v_ffn_norm_g': 'new_v', 'new_v_w_ffn_up': 'new_v', 'new_v_ffn_conv_w': 'new_v', 'new_v_ffn_conv_b': 'new_v', 'new_v_w_ffn_down': 'new_v', 'new_v_final_norm_g': 'new_v'}


def _forward(args):
    return _fwd_reference(*[args[k] for k in FWD_PARAMS])


def _output_shape():
    def fwd():
        inp = _fwd_setup_inputs(0)
        return _fwd_reference(*[inp[k] for k in FWD_PARAMS])
    out = _jax.eval_shape(fwd)
    return out.shape, out.dtype

N_MICROBATCH = 1
ADAM_LR = 0.001
ADAM_B1 = 0.9
ADAM_B2 = 0.999
ADAM_EPS = 1e-08
ADAM_WD = 0.01
ADAM_STEP = 10
PER_EXAMPLE_BATCH_AXIS = {'x': 0, 'loss_target': 0}
SHARED_INPUTS = []
_WEIGHT_DTYPES = {'meta_tokens': _jnp.float32, 'mix_norm_g': _jnp.float32, 'w_in': _jnp.float32, 'q_norm_g': _jnp.float32, 'w_uq': _jnp.float32, 'kv_norm_g': _jnp.float32, 'w_ukv': _jnp.float32, 'conv_w': _jnp.float32, 'conv_b': _jnp.float32, 'conv_ln_g': _jnp.float32, 'conv_ln_b': _jnp.float32, 'conv_out_g': _jnp.float32, 'attn_out_g': _jnp.float32, 'w_out': _jnp.float32, 'ffn_norm_g': _jnp.float32, 'w_ffn_up': _jnp.float32, 'ffn_conv_w': _jnp.float32, 'ffn_conv_b': _jnp.float32, 'w_ffn_down': _jnp.float32, 'final_norm_g': _jnp.float32}
MOMENT_SCALE = {'meta_tokens': 1.221558e-02, 'mix_norm_g': 1.879607e-01, 'w_in': 1.457090e-01, 'q_norm_g': 1.673937e-01, 'w_uq': 1.117250e-01, 'kv_norm_g': 2.937765e-01, 'w_ukv': 1.342945e-01, 'conv_w': 1.427767e-01, 'conv_b': 3.481158e-01, 'conv_ln_g': 2.004934e-01, 'conv_ln_b': 1.879708e-01, 'conv_out_g': 1.662678e-01, 'attn_out_g': 1.427623e-01, 'w_out': 1.393659e-01, 'ffn_norm_g': 1.025388e-01, 'w_ffn_up': 4.316502e-02, 'ffn_conv_w': 4.346030e-02, 'ffn_conv_b': 4.511582e-02, 'w_ffn_down': 7.071206e-02, 'final_norm_g': 3.209620e+01}


def _to_microbatches(a, axis):
    t = _jnp.moveaxis(a, axis, 0)
    t = t.reshape((N_MICROBATCH, t.shape[0] // N_MICROBATCH) + t.shape[1:])
    return _jnp.moveaxis(t, 1, axis + 1)


def setup_inputs(seed: int = 0) -> dict:
    inp = _fwd_setup_inputs(seed)
    key = _jax.random.fold_in(_jax.random.key(seed), 7919)
    shape, _ = _output_shape()
    out = dict(inp)
    out["loss_target"] = _jax.random.normal(_jax.random.fold_in(key, 0), shape, _jnp.float32)
    for i, name in enumerate(TWIN_WEIGHTS):
        w = inp[name].astype(_jnp.float32)
        if MOMENT_SCALE is None:
            s = _jnp.sqrt(_jnp.mean(_jnp.square(w)) + 1e-30)
        else:
            s = MOMENT_SCALE[name]
        km, kv = _jax.random.split(_jax.random.fold_in(key, i + 1))
        out[name] = w
        out["m_" + name] = s * _jax.random.normal(km, w.shape, _jnp.float32)
        out["v_" + name] = (s * s) * _jax.random.uniform(kv, w.shape, _jnp.float32, 0.5, 1.5)
    if N_MICROBATCH > 1:
        for name, axis in PER_EXAMPLE_BATCH_AXIS.items():
            out[name] = _to_microbatches(out[name], axis)
    return {'x': out['x'], 'meta_tokens': out['meta_tokens'], 'mix_norm_g': out['mix_norm_g'], 'w_in': out['w_in'], 'q_norm_g': out['q_norm_g'], 'w_uq': out['w_uq'], 'kv_norm_g': out['kv_norm_g'], 'w_ukv': out['w_ukv'], 'conv_w': out['conv_w'], 'conv_b': out['conv_b'], 'conv_ln_g': out['conv_ln_g'], 'conv_ln_b': out['conv_ln_b'], 'conv_out_g': out['conv_out_g'], 'attn_out_g': out['attn_out_g'], 'w_out': out['w_out'], 'ffn_norm_g': out['ffn_norm_g'], 'w_ffn_up': out['w_ffn_up'], 'ffn_conv_w': out['ffn_conv_w'], 'ffn_conv_b': out['ffn_conv_b'], 'w_ffn_down': out['w_ffn_down'], 'final_norm_g': out['final_norm_g'], 'loss_target': out['loss_target'], 'm_meta_tokens': out['m_meta_tokens'], 'm_mix_norm_g': out['m_mix_norm_g'], 'm_w_in': out['m_w_in'], 'm_q_norm_g': out['m_q_norm_g'], 'm_w_uq': out['m_w_uq'], 'm_kv_norm_g': out['m_kv_norm_g'], 'm_w_ukv': out['m_w_ukv'], 'm_conv_w': out['m_conv_w'], 'm_conv_b': out['m_conv_b'], 'm_conv_ln_g': out['m_conv_ln_g'], 'm_conv_ln_b': out['m_conv_ln_b'], 'm_conv_out_g': out['m_conv_out_g'], 'm_attn_out_g': out['m_attn_out_g'], 'm_w_out': out['m_w_out'], 'm_ffn_norm_g': out['m_ffn_norm_g'], 'm_w_ffn_up': out['m_w_ffn_up'], 'm_ffn_conv_w': out['m_ffn_conv_w'], 'm_ffn_conv_b': out['m_ffn_conv_b'], 'm_w_ffn_down': out['m_w_ffn_down'], 'm_final_norm_g': out['m_final_norm_g'], 'v_meta_tokens': out['v_meta_tokens'], 'v_mix_norm_g': out['v_mix_norm_g'], 'v_w_in': out['v_w_in'], 'v_q_norm_g': out['v_q_norm_g'], 'v_w_uq': out['v_w_uq'], 'v_kv_norm_g': out['v_kv_norm_g'], 'v_w_ukv': out['v_w_ukv'], 'v_conv_w': out['v_conv_w'], 'v_conv_b': out['v_conv_b'], 'v_conv_ln_g': out['v_conv_ln_g'], 'v_conv_ln_b': out['v_conv_ln_b'], 'v_conv_out_g': out['v_conv_out_g'], 'v_attn_out_g': out['v_attn_out_g'], 'v_w_out': out['v_w_out'], 'v_ffn_norm_g': out['v_ffn_norm_g'], 'v_w_ffn_up': out['v_w_ffn_up'], 'v_ffn_conv_w': out['v_ffn_conv_w'], 'v_ffn_conv_b': out['v_ffn_conv_b'], 'v_w_ffn_down': out['v_w_ffn_down'], 'v_final_norm_g': out['v_final_norm_g']}


def _loss(weights, diff, rest, loss_target):
    with _jax.named_scope("forward"):
        args = {**rest, TWIN_DIFF_INPUT: diff, **{k: w.astype(_WEIGHT_DTYPES[k]) for k, w in weights.items()}}
        y = _forward(args)
    with _jax.named_scope("loss_head"):
        err = _jnp.square(y.astype(_jnp.float32) - loss_target)
        return 0.5 * _jnp.sum(_jnp.mean(err, axis=-1)) if err.ndim else 0.5 * err


def _adamw(w, g, m, v):
    m = ADAM_B1 * m + (1.0 - ADAM_B1) * g
    v = ADAM_B2 * v + (1.0 - ADAM_B2) * _jnp.square(g)
    m_hat = m / (1.0 - ADAM_B1 ** ADAM_STEP)
    v_hat = v / (1.0 - ADAM_B2 ** ADAM_STEP)
    delta = -ADAM_LR * (m_hat / (_jnp.sqrt(v_hat) + ADAM_EPS) + ADAM_WD * w)
    return delta, m, v


def reference(x, meta_tokens, mix_norm_g, w_in, q_norm_g, w_uq, kv_norm_g, w_ukv, conv_w, conv_b, conv_ln_g, conv_ln_b, conv_out_g, attn_out_g, w_out, ffn_norm_g, w_ffn_up, ffn_conv_w, ffn_conv_b, w_ffn_down, final_norm_g, loss_target, m_meta_tokens, m_mix_norm_g, m_w_in, m_q_norm_g, m_w_uq, m_kv_norm_g, m_w_ukv, m_conv_w, m_conv_b, m_conv_ln_g, m_conv_ln_b, m_conv_out_g, m_attn_out_g, m_w_out, m_ffn_norm_g, m_w_ffn_up, m_ffn_conv_w, m_ffn_conv_b, m_w_ffn_down, m_final_norm_g, v_meta_tokens, v_mix_norm_g, v_w_in, v_q_norm_g, v_w_uq, v_kv_norm_g, v_w_ukv, v_conv_w, v_conv_b, v_conv_ln_g, v_conv_ln_b, v_conv_out_g, v_attn_out_g, v_w_out, v_ffn_norm_g, v_w_ffn_up, v_ffn_conv_w, v_ffn_conv_b, v_w_ffn_down, v_final_norm_g):
    given = dict(x=x, meta_tokens=meta_tokens, mix_norm_g=mix_norm_g, w_in=w_in, q_norm_g=q_norm_g, w_uq=w_uq, kv_norm_g=kv_norm_g, w_ukv=w_ukv, conv_w=conv_w, conv_b=conv_b, conv_ln_g=conv_ln_g, conv_ln_b=conv_ln_b, conv_out_g=conv_out_g, attn_out_g=attn_out_g, w_out=w_out, ffn_norm_g=ffn_norm_g, w_ffn_up=w_ffn_up, ffn_conv_w=ffn_conv_w, ffn_conv_b=ffn_conv_b, w_ffn_down=w_ffn_down, final_norm_g=final_norm_g, loss_target=loss_target, m_meta_tokens=m_meta_tokens, m_mix_norm_g=m_mix_norm_g, m_w_in=m_w_in, m_q_norm_g=m_q_norm_g, m_w_uq=m_w_uq, m_kv_norm_g=m_kv_norm_g, m_w_ukv=m_w_ukv, m_conv_w=m_conv_w, m_conv_b=m_conv_b, m_conv_ln_g=m_conv_ln_g, m_conv_ln_b=m_conv_ln_b, m_conv_out_g=m_conv_out_g, m_attn_out_g=m_attn_out_g, m_w_out=m_w_out, m_ffn_norm_g=m_ffn_norm_g, m_w_ffn_up=m_w_ffn_up, m_ffn_conv_w=m_ffn_conv_w, m_ffn_conv_b=m_ffn_conv_b, m_w_ffn_down=m_w_ffn_down, m_final_norm_g=m_final_norm_g, v_meta_tokens=v_meta_tokens, v_mix_norm_g=v_mix_norm_g, v_w_in=v_w_in, v_q_norm_g=v_q_norm_g, v_w_uq=v_w_uq, v_kv_norm_g=v_kv_norm_g, v_w_ukv=v_w_ukv, v_conv_w=v_conv_w, v_conv_b=v_conv_b, v_conv_ln_g=v_conv_ln_g, v_conv_ln_b=v_conv_ln_b, v_conv_out_g=v_conv_out_g, v_attn_out_g=v_attn_out_g, v_w_out=v_w_out, v_ffn_norm_g=v_ffn_norm_g, v_w_ffn_up=v_w_ffn_up, v_ffn_conv_w=v_ffn_conv_w, v_ffn_conv_b=v_ffn_conv_b, v_w_ffn_down=v_w_ffn_down, v_final_norm_g=v_final_norm_g)
    weights = {n: given[n] for n in TWIN_WEIGHTS}
    shared = {n: given[n] for n in SHARED_INPUTS}
    per_example = {n: given[n] for n in ['x']}
    grad_fn = _jax.value_and_grad(_loss, argnums=(0, 1))

    def one_microbatch(ex, loss_target):
        ex = dict(ex)
        diff = ex.pop(TWIN_DIFF_INPUT)
        return grad_fn(weights, diff, {**shared, **ex}, loss_target)

    if N_MICROBATCH == 1:
        loss, (grad_w, grad_x) = one_microbatch(per_example, given["loss_target"])
    else:
        def body(carry, xs):
            loss_sum, grad_sum = carry
            l_k, (gw_k, gx_k) = one_microbatch(xs[0], xs[1])
            with _jax.named_scope("update"):
                return (loss_sum + l_k, _jax.tree.map(_jnp.add, grad_sum, gw_k)), gx_k

        init = (_jnp.zeros((), _jnp.float32), _jax.tree.map(_jnp.zeros_like, weights))
        (loss, grad_w), grad_x = _jax.lax.scan(body, init, (per_example, given["loss_target"]))
    with _jax.named_scope("update"):
        delta_w, new_m, new_v = {}, {}, {}
        for n in TWIN_WEIGHTS:
            delta_w[n], new_m[n], new_v[n] = _adamw(weights[n], grad_w[n], given["m_" + n], given["v_" + n])
    return (loss, grad_x, *[grad_w[n] for n in TWIN_WEIGHTS], *[delta_w[n] for n in TWIN_WEIGHTS],
            *[new_m[n] for n in TWIN_WEIGHTS], *[new_v[n] for n in TWIN_WEIGHTS])
```

```python
import functools

import jax
import jax.numpy as jnp
from jax import lax
from jax.experimental import pallas as pl
from jax.experimental.pallas import tpu as pltpu

F32 = jnp.float32
BF16 = jnp.bfloat16

D_MODEL = 1024
D_CONV = 512
CONV_WIDTH = 31
N_HEADS = 8
QK_NOPE = 64
QK_ROPE = 32
V_HEAD = 64
Q_LORA = 384
KV_LORA = 256
D_FF = 2816
FFN_CONV_WIDTH = 3
CHUNK_SHIFT = 6
N_META = 16
ROPE_THETA = 10000.0
EPS = 1e-6
NEG = -1e30
ADAM_LR = 0.001
ADAM_B1 = 0.9
ADAM_B2 = 0.999
ADAM_EPS = 1e-08
ADAM_WD = 0.01
ADAM_STEP = 10

LANES = 128
HB = LANES
D_HEADS = N_HEADS * HB
TM = 256
DEAD = TM - N_META
D_AG = 2 * D_CONV
D_ZP = D_AG + Q_LORA + KV_LORA + HB
D_MIX = D_CONV + D_HEADS
SCALE = (QK_NOPE + QK_ROPE) ** -0.5
HALO_CONV = 32
HALO_FFN = 16
FF_CHUNK = 256
VMEM_LIMIT = 56 * 1024 * 1024
N_CHIPS = 4
MESH = pl.DeviceIdType.MESH


def _params(n_grid):
    return pltpu.CompilerParams(dimension_semantics=("arbitrary",) * n_grid, vmem_limit_bytes=VMEM_LIMIT)


def _rows(tm, c, off=0):
    return pl.BlockSpec((tm, c), lambda i: (i, off))


def _full(shape):
    return pl.BlockSpec(shape, lambda i: (0,) * len(shape))


def _prev(hb, c, tm, off=0):
    return pl.BlockSpec((hb, c), lambda i: (jnp.maximum(i * (tm // hb) - 1, 0), off))


def _next(hb, c, tm, nblk, off=0):
    return pl.BlockSpec((hb, c), lambda i: (jnp.minimum((i + 1) * (tm // hb), nblk - 1), off))


def _sds(shape, dtype):
    return jax.ShapeDtypeStruct(shape, dtype)


def _rms_r(x, n):
    return lax.rsqrt(jnp.sum(x * x, -1, keepdims=True) * (1.0 / n) + EPS)


def _rms_bwd(dy, x, r, g, n):
    gd = dy * g
    dx = r * gd - x * (r * r * r) * (jnp.sum(x * gd, -1, keepdims=True) * (1.0 / n))
    return dx, jnp.sum(dy * x * r, 0, keepdims=True)


def _dot(a, b, dims):
    return lax.dot_general(a, b, (dims, ((), ())), preferred_element_type=F32)


NN = ((1,), (0,))
NT = ((1,), (1,))
TN = ((0,), (0,))


def _rope(x, c, s1, s2):
    n = x.shape[-1]
    return x * c + pltpu.roll(x, n - QK_ROPE // 2, 1) * s1 + pltpu.roll(x, QK_ROPE // 2, 1) * s2


def _rope_bwd(g, c, s1, s2):
    n = g.shape[-1]
    return g * c + pltpu.roll(g * s1, QK_ROPE // 2, 1) + pltpu.roll(g * s2, n - QK_ROPE // 2, 1)


def _row_ids(tm, cols=1):
    return pl.program_id(0) * tm + lax.broadcasted_iota(jnp.int32, (tm, cols), 0)


def _mm(a, b, dims, out_dtype, tm, tn, tk, name):
    if dims == TN:
        (kk, m), (_, n) = a.shape, b.shape
        a_spec = pl.BlockSpec((tk, tm), lambda i, j, k: (k, i))
    else:
        m, kk = a.shape
        a_spec = pl.BlockSpec((tm, tk), lambda i, j, k: (i, k))
    if dims == NT:
        n = b.shape[0]
        b_spec = pl.BlockSpec((tn, tk), lambda i, j, k: (j, k))
    else:
        n = b.shape[1]
        b_spec = pl.BlockSpec((tk, tn), lambda i, j, k: (k, j))
    assert m % tm == 0 and n % tn == 0 and kk % tk == 0, (name, a.shape, b.shape, tm, tn, tk)
    nk = kk // tk

    def body(a_ref, b_ref, o_ref, acc_ref):
        k = pl.program_id(2)

        @pl.when(k == 0)
        def _():
            acc_ref[...] = jnp.zeros_like(acc_ref)

        acc_ref[...] += _dot(a_ref[...].astype(BF16), b_ref[...].astype(BF16), dims)

        @pl.when(k == nk - 1)
        def _():
            o_ref[...] = acc_ref[...].astype(out_dtype)

    return pl.pallas_call(
        body, name=name, grid=(m // tm, n // tn, nk), in_specs=[a_spec, b_spec],
        out_specs=pl.BlockSpec((tm, tn), lambda i, j, k: (i, j)), out_shape=_sds((m, n), out_dtype),
        scratch_shapes=[pltpu.VMEM((tm, tn), F32)], compiler_params=_params(3))(a, b)


def _fwd_in(h0, gmix, win, gq, wuq, gkv, wukv, rc, rs1, rs2):
    L = h0.shape[0]

    def body(h_ref, gmix_ref, win_ref, gq_ref, wuq_ref, gkv_ref, wukv_ref, c_ref, s1_ref, s2_ref,
             n_ref, zag_ref, u0_ref, cq_ref, ckv_ref, qn_ref, kvn_ref, q_ref, kv_ref, kr_ref):
        h = h_ref[...]
        n = (h * _rms_r(h, D_MODEL) * gmix_ref[...]).astype(BF16)
        n_ref[...] = n
        z = _dot(n, win_ref[...], NN)
        a, gate = z[:, :D_CONV], z[:, D_CONV:D_AG]
        zag_ref[...] = z[:, :D_AG].astype(BF16)
        u0_ref[...] = a * jax.nn.sigmoid(gate)
        cq = z[:, D_AG:D_AG + Q_LORA]
        ckv = z[:, D_AG + Q_LORA:D_AG + Q_LORA + KV_LORA]
        krp = z[:, D_AG + Q_LORA + KV_LORA:]
        cq_ref[...] = cq
        ckv_ref[...] = ckv
        qn = (cq * _rms_r(cq, Q_LORA) * gq_ref[...]).astype(BF16)
        qn_ref[...] = qn
        kvn = (ckv * _rms_r(ckv, KV_LORA) * gkv_ref[...]).astype(BF16)
        kvn_ref[...] = kvn
        c, s1, s2 = c_ref[...], s1_ref[...], s2_ref[...]
        q = _dot(qn, wuq_ref[...], NN)
        q = _rope(q, jnp.tile(c, (1, N_HEADS)), jnp.tile(s1, (1, N_HEADS)), jnp.tile(s2, (1, N_HEADS)))
        q_ref[...] = q.astype(BF16)
        kv_ref[...] = _dot(kvn, wukv_ref[...], NN).astype(BF16)
        kr_ref[...] = _rope(krp, c, s1, s2).astype(BF16)

    outs = [(D_MODEL, BF16), (D_AG, BF16), (D_CONV, F32), (Q_LORA, F32), (KV_LORA, F32), (Q_LORA, BF16),
            (KV_LORA, BF16), (D_HEADS, BF16), (D_HEADS, BF16), (HB, BF16)]
    return pl.pallas_call(
        body, name="fwd_in", grid=(L // TM,),
        in_specs=[_rows(TM, D_MODEL), _full(gmix.shape), _full(win.shape), _full(gq.shape), _full(wuq.shape),
                  _full(gkv.shape), _full(wukv.shape), _rows(TM, HB), _rows(TM, HB), _rows(TM, HB)],
        out_specs=[_rows(TM, c) for c, _ in outs], out_shape=[_sds((L, c), d) for c, d in outs],
        compiler_params=_params(1))(h0, gmix, win, gq, wuq, gkv, wukv, rc, rs1, rs2)


def _conv_taps(xx, w_ref, halo, tm, flip):
    kw = w_ref.shape[0]
    rows = xx.shape[0]
    acc = None
    for k in range(kw):
        d = kw - 1 - k
        shifted = xx if d == 0 else pltpu.roll(xx, (rows - d) if flip else d, 0)
        term = w_ref[k:k + 1, :] * (shifted[:tm] if flip else shifted[halo:])
        acc = term if acc is None else acc + term
    return acc


def _ln_silu(u1, lg, lb):
    mu = jnp.mean(u1, -1, keepdims=True)
    xc = u1 - mu
    rs = lax.rsqrt(jnp.mean(xc * xc, -1, keepdims=True) + EPS)
    xh = xc * rs
    u2 = xh * lg + lb
    sg = jax.nn.sigmoid(u2)
    return rs, xh, u2, sg, u2 * sg


def _fwd_conv(u0, cw, cb, lg, lb, og):
    L = u0.shape[0]

    def body(u0_ref, u0p_ref, cw_ref, cb_ref, lg_ref, lb_ref, og_ref, u1_ref, mixa_ref):
        halo = jnp.where(pl.program_id(0) > 0, u0p_ref[...], 0.0)
        xx = jnp.concatenate([halo, u0_ref[...]], 0)
        u1 = _conv_taps(xx, cw_ref, HALO_CONV, TM, False) + cb_ref[...]
        u1_ref[...] = u1
        u = _ln_silu(u1, lg_ref[...], lb_ref[...])[4]
        mixa_ref[...] = (u * _rms_r(u, D_CONV) * og_ref[...]).astype(BF16)

    return pl.pallas_call(
        body, name="fwd_conv", grid=(L // TM,),
        in_specs=[_rows(TM, D_CONV), _prev(HALO_CONV, D_CONV, TM), _full(cw.shape), _full(cb.shape),
                  _full(lg.shape), _full(lb.shape), _full(og.shape)],
        out_specs=[_rows(TM, D_CONV), _rows(TM, D_CONV)],
        out_shape=[_sds((L, D_CONV), F32), _sds((L, D_CONV), BF16)],
        compiler_params=_params(1))(u0, u0, cw, cb, lg, lb, og)


def _visible(i, j, t):
    row = i * t + lax.broadcasted_iota(jnp.int32, (t, t), 0)
    col = j * t + lax.broadcasted_iota(jnp.int32, (t, t), 1)
    return (lax.shift_right_logical(col, CHUNK_SHIFT) <= lax.shift_right_logical(row, CHUNK_SHIFT)) & (col >= DEAD)


def _attn_fwd(q, kv, kr):
    L = q.shape[0]
    t = TM
    nq = L // t

    def body(q_ref, kv_ref, kr_ref, o_ref, lse_ref):
        i = pl.program_id(1)
        qi = q_ref[...]
        lane = lax.broadcasted_iota(jnp.int32, (t, HB), 1)

        def step(j, carry):
            m, l, acc = carry
            kvj = kv_ref[pl.ds(pl.multiple_of(j * t, t), t), :]
            krj = kr_ref[pl.ds(pl.multiple_of(j * t, t), t), :]
            kk = jnp.where(lane < QK_NOPE, kvj, krj)
            s = jnp.where(_visible(i, j, t), _dot(qi, kk, NT) * SCALE, NEG)
            m_new = jnp.maximum(m, jnp.max(s, -1, keepdims=True))
            alpha = jnp.exp(m - m_new)
            p = jnp.exp(s - m_new)
            l = alpha * l + jnp.sum(p, -1, keepdims=True)
            acc = alpha * acc + _dot(p.astype(BF16), kvj, NN)
            return m_new, l, acc

        init = (jnp.full((t, 1), NEG, F32), jnp.zeros((t, 1), F32), jnp.zeros((t, HB), F32))
        m, l, acc = lax.fori_loop(0, i + 1, step, init)
        o_ref[...] = jnp.where(lane >= QK_NOPE, acc / l, 0.0).astype(BF16)
        lse_ref[...] = m + jnp.log(l)

    return pl.pallas_call(
        body, name="attn_fwd", grid=(N_HEADS, nq),
        in_specs=[pl.BlockSpec((t, HB), lambda h, i: (i, h)), pl.BlockSpec((L, HB), lambda h, i: (0, h)),
                  pl.BlockSpec((L, HB), lambda h, i: (0, 0))],
        out_specs=[pl.BlockSpec((t, HB), lambda h, i: (i, h)), pl.BlockSpec((None, t, 1), lambda h, i: (h, i, 0))],
        out_shape=[_sds((L, D_HEADS), BF16), _sds((N_HEADS, L, 1), F32)],
        compiler_params=_params(2))(q, kv, kr)


def _fwd_mix(h0, mixa, o, ga, wout, gffn):
    L = h0.shape[0]

    def body(h0_ref, mixa_ref, o_ref, ga_ref, wout_ref, gffn_ref, mix_ref, h1_ref, n2_ref):
        of = o_ref[...].astype(F32)
        mixb = (of * _rms_r(of, N_HEADS * V_HEAD) * ga_ref[...]).astype(BF16)
        mix = jnp.concatenate([mixa_ref[...], mixb], 1)
        mix_ref[...] = mix
        mo = jnp.where(_row_ids(TM) >= DEAD, _dot(mix, wout_ref[...], NN), 0.0)
        h1 = h0_ref[...] + mo
        h1_ref[...] = h1
        n2_ref[...] = (h1 * _rms_r(h1, D_MODEL) * gffn_ref[...]).astype(BF16)

    return pl.pallas_call(
        body, name="fwd_mix", grid=(L // TM,),
        in_specs=[_rows(TM, D_MODEL), _rows(TM, D_CONV), _rows(TM, D_HEADS), _full(ga.shape), _full(wout.shape),
                  _full(gffn.shape)],
        out_specs=[_rows(TM, D_MIX), _rows(TM, D_MODEL), _rows(TM, D_MODEL)],
        out_shape=[_sds((L, D_MIX), BF16), _sds((L, D_MODEL), F32), _sds((L, D_MODEL), BF16)],
        compiler_params=_params(1))(h0, mixa, o, ga, wout, gffn)


def _ffn_act_chunk(c, upg_ref, upv_ref, hg, hv, fcw_ref, fcb_ref):
    cs = slice(c * FF_CHUNK, (c + 1) * FF_CHUNK)
    out = []
    for part, (up_ref, halo) in enumerate(((upg_ref, hg), (upv_ref, hv))):
        xx = jnp.concatenate([halo[:, cs], up_ref[:, cs].astype(F32)], 0)
        ws = slice(part * D_FF + c * FF_CHUNK, part * D_FF + (c + 1) * FF_CHUNK)
        y = (fcw_ref[0:1, ws] * pltpu.roll(xx, 2, 0)[HALO_FFN:] + fcw_ref[1:2, ws] * pltpu.roll(xx, 1, 0)[HALO_FFN:]
             + fcw_ref[2:3, ws] * xx[HALO_FFN:] + fcb_ref[:, ws])
        out.append(y)
    return out


def _ffn_in_specs(L):
    return [_rows(TM, D_FF, 0), _rows(TM, D_FF, 1), _prev(HALO_FFN, D_FF, TM, 0), _prev(HALO_FFN, D_FF, TM, 1)]


def _ffn_halos(hg_ref, hv_ref):
    first = pl.program_id(0) == 0
    return (jnp.where(first, 0.0, hg_ref[...].astype(F32)), jnp.where(first, 0.0, hv_ref[...].astype(F32)))


def _fwd_ffn_loss(up0, fcw, fcb, wdown, h1, target, gfin):
    L = h1.shape[0]

    def body(upg_ref, upv_ref, hg_ref, hv_ref, fcw_ref, fcb_ref, wd_ref, h1_ref, t_ref, gf_ref,
             dh2_ref, loss_ref, dgf_ref, act_ref):
        i = pl.program_id(0)
        hg, hv = _ffn_halos(hg_ref, hv_ref)
        for c in range(D_FF // FF_CHUNK):
            g, val = _ffn_act_chunk(c, upg_ref, upv_ref, hg, hv, fcw_ref, fcb_ref)
            act_ref[:, c * FF_CHUNK:(c + 1) * FF_CHUNK] = (g * jax.nn.sigmoid(g) * val).astype(BF16)
        h2 = h1_ref[...] + _dot(act_ref[...], wd_ref[...], NN)
        r = _rms_r(h2, D_MODEL)
        gf = gf_ref[...]
        err = jnp.where(i > 0, h2 * r * gf - t_ref[...], 0.0)
        dy = err * (1.0 / D_MODEL)
        dh2, dgf = _rms_bwd(dy, h2, r, gf, D_MODEL)
        dh2_ref[...] = dh2

        @pl.when(i == 0)
        def _():
            loss_ref[...] = jnp.zeros_like(loss_ref)
            dgf_ref[...] = jnp.zeros_like(dgf_ref)

        loss_ref[...] += jnp.sum(err * err) * (0.5 / D_MODEL)
        dgf_ref[...] += dgf

    return pl.pallas_call(
        body, name="fwd_ffn_loss", grid=(L // TM,),
        in_specs=_ffn_in_specs(L) + [_full(fcw.shape), _full(fcb.shape), _full(wdown.shape), _rows(TM, D_MODEL),
                                     pl.BlockSpec((TM, D_MODEL), lambda i: (jnp.maximum(i - 1, 0), 0)),
                                     _full(gfin.shape)],
        out_specs=[_rows(TM, D_MODEL), _full((1, LANES)), _full((1, D_MODEL))],
        out_shape=[_sds((L, D_MODEL), F32), _sds((1, LANES), F32), _sds((1, D_MODEL), F32)],
        scratch_shapes=[pltpu.VMEM((TM, D_FF), BF16)],
        compiler_params=_params(1))(up0, up0, up0, up0, fcw, fcb, wdown, h1, target, gfin)


def _bwd_ffn_act(dh2, up0, fcw, fcb, wdown):
    L = dh2.shape[0]

    def body(dh2_ref, upg_ref, upv_ref, hg_ref, hv_ref, fcw_ref, fcb_ref, wd_ref, dup_ref, act_ref, dfcb_ref):
        hg, hv = _ffn_halos(hg_ref, hv_ref)
        da = _dot(dh2_ref[...].astype(BF16), wd_ref[...], NT)

        @pl.when(pl.program_id(0) == 0)
        def _():
            dfcb_ref[...] = jnp.zeros_like(dfcb_ref)

        for c in range(D_FF // FF_CHUNK):
            cs = slice(c * FF_CHUNK, (c + 1) * FF_CHUNK)
            vs = slice(D_FF + c * FF_CHUNK, D_FF + (c + 1) * FF_CHUNK)
            g, val = _ffn_act_chunk(c, upg_ref, upv_ref, hg, hv, fcw_ref, fcb_ref)
            sg = jax.nn.sigmoid(g)
            si = g * sg
            dac = da[:, cs]
            act_ref[:, cs] = (si * val).astype(BF16)
            dg = dac * val * (sg * (1.0 + g * (1.0 - sg)))
            dv = dac * si
            dup_ref[:, cs] = dg.astype(BF16)
            dup_ref[:, vs] = dv.astype(BF16)
            dfcb_ref[:, cs] += jnp.sum(dg, 0, keepdims=True)
            dfcb_ref[:, vs] += jnp.sum(dv, 0, keepdims=True)

    return pl.pallas_call(
        body, name="bwd_ffn_act", grid=(L // TM,),
        in_specs=[_rows(TM, D_MODEL)] + _ffn_in_specs(L) + [_full(fcw.shape), _full(fcb.shape), _full(wdown.shape)],
        out_specs=[_rows(TM, 2 * D_FF), _rows(TM, D_FF), _full((1, 2 * D_FF))],
        out_shape=[_sds((L, 2 * D_FF), BF16), _sds((L, D_FF), BF16), _sds((1, 2 * D_FF), F32)],
        compiler_params=_params(1))(dh2, up0, up0, up0, up0, fcw, fcb, wdown)


def _bwd_ffn_conv(dup, up0, fcw):
    L, C = dup.shape
    tc = C // N_CHIPS
    nt = L // TM
    nhb = L // HALO_FFN

    def body(dy_ref, dyn_ref, x_ref, xp_ref, w_ref, dx_ref, dw_ref):
        i = pl.program_id(1)
        yy = jnp.concatenate([dy_ref[...].astype(F32), jnp.where(i < nt - 1, dyn_ref[...].astype(F32), 0.0)], 0)
        dx_ref[...] = _conv_taps(yy, w_ref, HALO_FFN, TM, True).astype(BF16)
        xx = jnp.concatenate([jnp.where(i > 0, xp_ref[...].astype(F32), 0.0), x_ref[...].astype(F32)], 0)
        dy = yy[:TM]

        @pl.when(i == 0)
        def _():
            dw_ref[...] = jnp.zeros_like(dw_ref)

        for k in range(FFN_CONV_WIDTH):
            d = FFN_CONV_WIDTH - 1 - k
            xs = xx if d == 0 else pltpu.roll(xx, d, 0)
            dw_ref[k:k + 1, :] += jnp.sum(dy * xs[HALO_FFN:], 0, keepdims=True)

    col = lambda j, i: (i, j)
    return pl.pallas_call(
        body, name="bwd_ffn_conv", grid=(N_CHIPS, nt),
        in_specs=[pl.BlockSpec((TM, tc), col),
                  pl.BlockSpec((HALO_FFN, tc), lambda j, i: (jnp.minimum((i + 1) * (TM // HALO_FFN), nhb - 1), j)),
                  pl.BlockSpec((TM, tc), col),
                  pl.BlockSpec((HALO_FFN, tc), lambda j, i: (jnp.maximum(i * (TM // HALO_FFN) - 1, 0), j)),
                  pl.BlockSpec((FFN_CONV_WIDTH, tc), lambda j, i: (0, j))],
        out_specs=[pl.BlockSpec((TM, tc), col), pl.BlockSpec((FFN_CONV_WIDTH, tc), lambda j, i: (0, j))],
        out_shape=[_sds((L, C), BF16), _sds((FFN_CONV_WIDTH, C), F32)],
        compiler_params=_params(2))(dup, dup, up0, up0, fcw)


def _bwd_mix(dh2, dn2, h1, gffn, wout, o, ga, u1, lg, lb, og):
    L = h1.shape[0]

    def body(dh2_ref, dn2_ref, h1_ref, gffn_ref, wout_ref, o_ref, ga_ref, u1_ref, lg_ref, lb_ref, og_ref,
             dh1_ref, dh1m_ref, do_ref, du1_ref, dgffn_ref, dga_ref, dog_ref, dlg_ref, dlb_ref):
        h1 = h1_ref[...]
        dn2x, dgffn = _rms_bwd(dn2_ref[...], h1, _rms_r(h1, D_MODEL), gffn_ref[...], D_MODEL)
        dh1 = dh2_ref[...] + dn2x
        dh1_ref[...] = dh1
        dh1m = jnp.where(_row_ids(TM) >= DEAD, dh1, 0.0).astype(BF16)
        dh1m_ref[...] = dh1m
        dmix = _dot(dh1m, wout_ref[...], NT)
        dma, dmb = dmix[:, :D_CONV], dmix[:, D_CONV:]
        of = o_ref[...].astype(F32)
        do, dga = _rms_bwd(dmb, of, _rms_r(of, N_HEADS * V_HEAD), ga_ref[...], N_HEADS * V_HEAD)
        do_ref[...] = do.astype(BF16)
        lg = lg_ref[...]
        rs, xh, u2, sg, u = _ln_silu(u1_ref[...], lg, lb_ref[...])
        du, dog = _rms_bwd(dma, u, _rms_r(u, D_CONV), og_ref[...], D_CONV)
        du2 = du * (sg * (1.0 + u2 * (1.0 - sg)))
        dxh = du2 * lg
        du1_ref[...] = rs * (dxh - jnp.mean(dxh, -1, keepdims=True) - xh * jnp.mean(dxh * xh, -1, keepdims=True))

        @pl.when(pl.program_id(0) == 0)
        def _():
            for ref in (dgffn_ref, dga_ref, dog_ref, dlg_ref, dlb_ref):
                ref[...] = jnp.zeros_like(ref)

        dgffn_ref[...] += dgffn
        dga_ref[...] += dga
        dog_ref[...] += dog
        dlg_ref[...] += jnp.sum(du2 * xh, 0, keepdims=True)
        dlb_ref[...] += jnp.sum(du2, 0, keepdims=True)

    return pl.pallas_call(
        body, name="bwd_mix", grid=(L // TM,),
        in_specs=[_rows(TM, D_MODEL), _rows(TM, D_MODEL), _rows(TM, D_MODEL), _full(gffn.shape), _full(wout.shape),
                  _rows(TM, D_HEADS), _full(ga.shape), _rows(TM, D_CONV), _full(lg.shape), _full(lb.shape),
                  _full(og.shape)],
        out_specs=[_rows(TM, D_MODEL), _rows(TM, D_MODEL), _rows(TM, D_HEADS), _rows(TM, D_CONV),
                   _full((1, D_MODEL)), _full((1, D_HEADS)), _full((1, D_CONV)), _full((1, D_CONV)),
                   _full((1, D_CONV))],
        out_shape=[_sds((L, D_MODEL), F32), _sds((L, D_MODEL), BF16), _sds((L, D_HEADS), BF16),
                   _sds((L, D_CONV), F32), _sds((1, D_MODEL), F32), _sds((1, D_HEADS), F32), _sds((1, D_CONV), F32),
                   _sds((1, D_CONV), F32), _sds((1, D_CONV), F32)],
        compiler_params=_params(1))(dh2, dn2, h1, gffn, wout, o, ga, u1, lg, lb, og)


def _attn_bwd(q, kv, kr, o, do, lse):
    L = q.shape[0]
    t = TM
    nt = L // t

    def body(q_ref, kv_ref, kr_ref, o_ref, do_ref, lse_ref, dq_ref, dkv_ref, dkr_ref, dq_acc):
        h, j = pl.program_id(0), pl.program_id(1)
        lane = lax.broadcasted_iota(jnp.int32, (t, HB), 1)

        @pl.when(j == 0)
        def _():
            dq_acc[...] = jnp.zeros_like(dq_acc)

        @pl.when((j == 0) & (h == 0))
        def _():
            dkr_ref[...] = jnp.zeros_like(dkr_ref)

        rows_j = pl.ds(pl.multiple_of(j * t, t), t)
        kvj = kv_ref[rows_j, :]
        kk = jnp.where(lane < QK_NOPE, kvj, kr_ref[rows_j, :])

        def step(i, carry):
            dk, dv = carry
            rows_i = pl.ds(pl.multiple_of(i * t, t), t)
            qi, doi = q_ref[rows_i, :], do_ref[rows_i, :]
            s = jnp.where(_visible(i, j, t), _dot(qi, kk, NT) * SCALE, NEG)
            p = jnp.exp(s - lse_ref[rows_i, :])
            dv = dv + _dot(p.astype(BF16), doi, TN)
            delta = jnp.sum(doi.astype(F32) * o_ref[rows_i, :].astype(F32), -1, keepdims=True)
            ds = (p * (_dot(doi, kvj, NT) - delta) * SCALE).astype(BF16)
            dk = dk + _dot(ds, qi, TN)
            dq_acc[rows_i, :] += _dot(ds, kk, NN)
            return dk, dv

        dk, dv = lax.fori_loop(j, nt, step, (jnp.zeros((t, HB), F32), jnp.zeros((t, HB), F32)))
        dkv_ref[...] = jnp.where(lane < QK_NOPE, dk, dv).astype(BF16)
        dkr_ref[rows_j, :] += jnp.where(lane >= QK_NOPE, dk, 0.0)

        @pl.when(j == nt - 1)
        def _():
            dq_ref[...] = dq_acc[...].astype(BF16)

    head = pl.BlockSpec((L, HB), lambda h, j: (0, h))
    return pl.pallas_call(
        body, name="attn_bwd", grid=(N_HEADS, nt),
        in_specs=[head, head, pl.BlockSpec((L, HB), lambda h, j: (0, 0)), head, head,
                  pl.BlockSpec((None, L, 1), lambda h, j: (h, 0, 0))],
        out_specs=[head, pl.BlockSpec((t, HB), lambda h, j: (j, h)), pl.BlockSpec((L, HB), lambda h, j: (0, 0))],
        out_shape=[_sds((L, D_HEADS), BF16), _sds((L, D_HEADS), BF16), _sds((L, HB), F32)],
        scratch_shapes=[pltpu.VMEM((L, HB), F32)],
        compiler_params=_params(2))(q, kv, kr, o, do, lse)


def _bwd_conv(du1, u0, cw, zag):
    L = du1.shape[0]
    nt = L // TM

    def body(dy_ref, dyn_ref, x_ref, xp_ref, cw_ref, zag_ref, dzag_ref, dcw_ref, dcb_ref):
        i = pl.program_id(0)
        dy = dy_ref[...]
        yy = jnp.concatenate([dy, jnp.where(i < nt - 1, dyn_ref[...], 0.0)], 0)
        du0 = _conv_taps(yy, cw_ref, HALO_CONV, TM, True)
        zag = zag_ref[...].astype(F32)
        a, sg = zag[:, :D_CONV], jax.nn.sigmoid(zag[:, D_CONV:])
        dzag_ref[...] = jnp.concatenate([du0 * sg, du0 * a * sg * (1.0 - sg)], 1).astype(BF16)
        xx = jnp.concatenate([jnp.where(i > 0, xp_ref[...], 0.0), x_ref[...]], 0)

        @pl.when(i == 0)
        def _():
            dcw_ref[...] = jnp.zeros_like(dcw_ref)
            dcb_ref[...] = jnp.zeros_like(dcb_ref)

        for k in range(CONV_WIDTH):
            d = CONV_WIDTH - 1 - k
            xs = xx if d == 0 else pltpu.roll(xx, d, 0)
            dcw_ref[k:k + 1, :] += jnp.sum(dy * xs[HALO_CONV:], 0, keepdims=True)
        dcb_ref[...] += jnp.sum(dy, 0, keepdims=True)

    return pl.pallas_call(
        body, name="bwd_conv", grid=(nt,),
        in_specs=[_rows(TM, D_CONV), _next(HALO_CONV, D_CONV, TM, L // HALO_CONV), _rows(TM, D_CONV),
                  _prev(HALO_CONV, D_CONV, TM), _full(cw.shape), _rows(TM, D_AG)],
        out_specs=[_rows(TM, D_AG), _full(cw.shape), _full((1, D_CONV))],
        out_shape=[_sds((L, D_AG), BF16), _sds(cw.shape, F32), _sds((1, D_CONV), F32)],
        compiler_params=_params(1))(du1, du1, u0, u0, cw, zag)


def _bwd_in(dzag, dq, dkv, dkr, cq, ckv, gq, gkv, wuq, wukv, win, rc, rs1, rs2, h0, gmix, dh1):
    L = h0.shape[0]

    def body(dzag_ref, dq_ref, dkv_ref, dkr_ref, cq_ref, ckv_ref, gq_ref, gkv_ref, wuq_ref, wukv_ref, win_ref,
             c_ref, s1_ref, s2_ref, h0_ref, gmix_ref, dh1_ref,
             dz_ref, dqr_ref, gx_ref, dfirst_ref, dgq_ref, dgkv_ref, dgmix_ref):
        i = pl.program_id(0)
        c, s1, s2 = c_ref[...], s1_ref[...], s2_ref[...]
        dqr = _rope_bwd(dq_ref[...].astype(F32), jnp.tile(c, (1, N_HEADS)), jnp.tile(s1, (1, N_HEADS)),
                        jnp.tile(s2, (1, N_HEADS))).astype(BF16)
        dqr_ref[...] = dqr
        cq, ckv = cq_ref[...], ckv_ref[...]
        dcq, dgq = _rms_bwd(_dot(dqr, wuq_ref[...], NT), cq, _rms_r(cq, Q_LORA), gq_ref[...], Q_LORA)
        dckv, dgkv = _rms_bwd(_dot(dkv_ref[...], wukv_ref[...], NT), ckv, _rms_r(ckv, KV_LORA), gkv_ref[...], KV_LORA)
        dkrp = _rope_bwd(dkr_ref[...], c, s1, s2)
        dz = jnp.concatenate([dzag_ref[...], dcq.astype(BF16), dckv.astype(BF16), dkrp.astype(BF16)], 1)
        dz_ref[...] = dz
        h0 = h0_ref[...]
        dnx, dgmix = _rms_bwd(_dot(dz, win_ref[...], NT), h0, _rms_r(h0, D_MODEL), gmix_ref[...], D_MODEL)
        dh0 = dh1_ref[...] + dnx

        @pl.when(i == 0)
        def _():
            dfirst_ref[...] = dh0
            for ref in (dgq_ref, dgkv_ref, dgmix_ref):
                ref[...] = jnp.zeros_like(ref)

        @pl.when(i > 0)
        def _():
            gx_ref[...] = dh0

        dgq_ref[...] += dgq
        dgkv_ref[...] += dgkv
        dgmix_ref[...] += dgmix

    return pl.pallas_call(
        body, name="bwd_in", grid=(L // TM,),
        in_specs=[_rows(TM, D_AG), _rows(TM, D_HEADS), _rows(TM, D_HEADS), _rows(TM, HB), _rows(TM, Q_LORA),
                  _rows(TM, KV_LORA), _full(gq.shape), _full(gkv.shape), _full(wuq.shape), _full(wukv.shape),
                  _full(win.shape), _rows(TM, HB), _rows(TM, HB), _rows(TM, HB), _rows(TM, D_MODEL),
                  _full(gmix.shape), _rows(TM, D_MODEL)],
        out_specs=[_rows(TM, D_ZP), _rows(TM, D_HEADS),
                   pl.BlockSpec((TM, D_MODEL), lambda i: (jnp.maximum(i - 1, 0), 0)), _full((TM, D_MODEL)),
                   _full((1, Q_LORA)), _full((1, KV_LORA)), _full((1, D_MODEL))],
        out_shape=[_sds((L, D_ZP), BF16), _sds((L, D_HEADS), BF16), _sds((L - TM, D_MODEL), F32),
                   _sds((TM, D_MODEL), F32), _sds((1, Q_LORA), F32), _sds((1, KV_LORA), F32), _sds((1, D_MODEL), F32)],
        compiler_params=_params(1))(dzag, dq, dkv, dkr, cq, ckv, gq, gkv, wuq, wukv, win, rc, rs1, rs2, h0, gmix, dh1)


def _mesh_pos():
    return lax.axis_index("x"), lax.axis_index("y"), lax.axis_index("c")


def _all_gather(shards):
    n = len(shards)
    halves = [s.shape[0] // 2 for s in shards]

    def body(*refs):
        ins, outs = refs[:n], refs[n:2 * n]
        send_sems, recv_sems, local_sems = refs[2 * n:]
        x, y, c = _mesh_pos()
        chips = [(1 - x, y), (x, 1 - y), (1 - x, 1 - y)]
        started = []
        for a in range(n):
            m = halves[a]
            out = outs[a]

            def block(px, py, pc, out=out):
                return out.at[4 * px + 2 * py + pc]

            for hf in range(2):
                cp = pltpu.make_async_copy(ins[a].at[pl.ds(hf * m, m)], block(x, y, hf), local_sems.at[a, hf])
                cp.start()
                started.append(cp.wait)
            mine = ins[a].at[pl.ds(pl.multiple_of(c * m, 16), m)]
            for k, chip in enumerate(chips):
                cp = pltpu.make_async_remote_copy(mine, block(x, y, c), send_sems.at[a, k], recv_sems.at[a, k],
                                                  device_id=(*chip, c), device_id_type=MESH)
                cp.start()
                started.append(cp.wait_send)
        for a in range(n):
            out = outs[a]
            for k, chip in enumerate(chips):
                got = out.at[4 * chip[0] + 2 * chip[1] + c]
                pltpu.make_async_remote_copy(got, got, send_sems.at[a, k], recv_sems.at[a, k],
                                             device_id=(*chip, c), device_id_type=MESH).wait_recv()
                cp = pltpu.make_async_remote_copy(got, got, send_sems.at[a, 3 + k], recv_sems.at[a, 3 + k],
                                                  device_id=(x, y, 1 - c), device_id_type=MESH)
                cp.start()
                started.append(cp.wait_send)
        for a in range(n):
            out = outs[a]
            for k, chip in enumerate(chips):
                got = out.at[4 * chip[0] + 2 * chip[1] + 1 - c]
                pltpu.make_async_remote_copy(got, got, send_sems.at[a, 3 + k], recv_sems.at[a, 3 + k],
                                             device_id=(x, y, 1 - c), device_id_type=MESH).wait_recv()
        for wait in started:
            wait()

    any_spec = pl.BlockSpec(memory_space=pl.ANY)
    outs = pl.pallas_call(
        body, name="all_gather_weights", in_specs=[any_spec] * n, out_specs=[any_spec] * n,
        out_shape=[_sds((8, m) + s.shape[1:], s.dtype) for s, m in zip(shards, halves)],
        scratch_shapes=[pltpu.SemaphoreType.DMA((n, 6)), pltpu.SemaphoreType.DMA((n, 6)),
                        pltpu.SemaphoreType.DMA((n, 2))])(*shards)
    return [o.reshape((N_CHIPS, s.shape[0]) + s.shape[1:]) for o, s in zip(outs, shards)]


def _sibling_exchange(parts):
    n = len(parts)

    def body(*refs):
        ins, mine, theirs = refs[:n], refs[n:2 * n], refs[2 * n:3 * n]
        send_sems, recv_sems, local_sems = refs[3 * n:]
        x, y, c = _mesh_pos()
        waits = []
        for a in range(n):
            lc = pltpu.make_async_copy(ins[a].at[c], mine[a], local_sems.at[a])
            lc.start()
            rc = pltpu.make_async_remote_copy(ins[a].at[1 - c], theirs[a], send_sems.at[a], recv_sems.at[a],
                                              device_id=(x, y, 1 - c), device_id_type=MESH)
            rc.start()
            waits += [lc.wait, rc.wait]
        for wait in waits:
            wait()

    any_spec = pl.BlockSpec(memory_space=pl.ANY)
    shapes = [_sds(p.shape[1:], p.dtype) for p in parts]
    outs = pl.pallas_call(
        body, name="grad_sibling_exchange", in_specs=[any_spec] * n, out_specs=[any_spec] * (2 * n),
        out_shape=shapes + shapes,
        scratch_shapes=[pltpu.SemaphoreType.DMA((n,)), pltpu.SemaphoreType.DMA((n,)),
                        pltpu.SemaphoreType.DMA((n,))])(*parts)
    return outs[:n], outs[n:]


def _chip_exchange(parts):
    n = len(parts)

    def body(*refs):
        ins, outs = refs[:n], refs[n:2 * n]
        send_sems, recv_sems, local_sems = refs[2 * n:]
        x, y, c = _mesh_pos()
        me = 2 * x + y
        chips = [(1 - x, y), (x, 1 - y), (1 - x, 1 - y)]
        waits = []
        for a in range(n):
            lc = pltpu.make_async_copy(ins[a].at[me], outs[a].at[me], local_sems.at[a])
            lc.start()
            waits.append(lc.wait)
            for k, chip in enumerate(chips):
                rc = pltpu.make_async_remote_copy(ins[a].at[2 * chip[0] + chip[1]], outs[a].at[me],
                                                  send_sems.at[a, k], recv_sems.at[a, k],
                                                  device_id=(*chip, c), device_id_type=MESH)
                rc.start()
                waits.append(rc.wait_send)
        for a in range(n):
            for k, chip in enumerate(chips):
                src = outs[a].at[2 * chip[0] + chip[1]]
                pltpu.make_async_remote_copy(src, src, send_sems.at[a, k], recv_sems.at[a, k],
                                             device_id=(*chip, c), device_id_type=MESH).wait_recv()
        for wait in waits:
            wait()

    any_spec = pl.BlockSpec(memory_space=pl.ANY)
    return pl.pallas_call(
        body, name="grad_chip_exchange", in_specs=[any_spec] * n, out_specs=[any_spec] * n,
        out_shape=[_sds(p.shape, p.dtype) for p in parts],
        scratch_shapes=[pltpu.SemaphoreType.DMA((n, 3)), pltpu.SemaphoreType.DMA((n, 3)),
                        pltpu.SemaphoreType.DMA((n,))])(*parts)


def _sibling_gather(parts):
    n = len(parts)

    def body(*refs):
        ins, outs = refs[:n], refs[n:2 * n]
        send_sems, recv_sems, local_sems = refs[2 * n:]
        x, y, c = _mesh_pos()
        waits = []
        for a in range(n):
            lc = pltpu.make_async_copy(ins[a], outs[a].at[c], local_sems.at[a])
            lc.start()
            rc = pltpu.make_async_remote_copy(ins[a], outs[a].at[c], send_sems.at[a], recv_sems.at[a],
                                              device_id=(x, y, 1 - c), device_id_type=MESH)
            rc.start()
            waits += [lc.wait, rc.wait_send]
        for a in range(n):
            got = outs[a].at[1 - c]
            pltpu.make_async_remote_copy(got, got, send_sems.at[a], recv_sems.at[a],
                                         device_id=(x, y, 1 - c), device_id_type=MESH).wait_recv()
        for wait in waits:
            wait()

    any_spec = pl.BlockSpec(memory_space=pl.ANY)
    return pl.pallas_call(
        body, name="grad_sibling_gather", in_specs=[any_spec] * n, out_specs=[any_spec] * n,
        out_shape=[_sds((2,) + p.shape, p.dtype) for p in parts],
        scratch_shapes=[pltpu.SemaphoreType.DMA((n,)), pltpu.SemaphoreType.DMA((n,)),
                        pltpu.SemaphoreType.DMA((n,))])(*parts)


def _row_tile(rows, row_bytes, align, budget=1 << 20):
    best = None
    for t in range(align, rows + 1, align):
        if rows % t == 0 and t * row_bytes <= budget:
            best = t
    return best or rows


def _add_pair(a, b, out_dtype):
    shape = a.shape
    a2, b2 = a.reshape(-1, LANES), b.reshape(-1, LANES)
    tr = _row_tile(a2.shape[0], LANES * 4, 16)

    def body(a_ref, b_ref, o_ref):
        o_ref[...] = (a_ref[...].astype(F32) + b_ref[...].astype(F32)).astype(out_dtype)

    out = pl.pallas_call(
        body, name="grad_add_pair", grid=(a2.shape[0] // tr,), in_specs=[_rows(tr, LANES)] * 2,
        out_specs=_rows(tr, LANES), out_shape=_sds(a2.shape, out_dtype), compiler_params=_params(1))(a2, b2)
    return out.reshape(shape)


def _add_chips(p):
    nh = p.shape[1]
    tr = _row_tile(nh, LANES * 4 * N_CHIPS, 16)

    def body(p_ref, o_ref):
        acc = p_ref[0].astype(F32)
        for q in range(1, N_CHIPS):
            acc = acc + p_ref[q].astype(F32)
        o_ref[...] = acc

    return pl.pallas_call(
        body, name="grad_add_chips", grid=(nh // tr,),
        in_specs=[pl.BlockSpec((N_CHIPS, tr, LANES), lambda i: (0, i, 0))], out_specs=_rows(tr, LANES),
        out_shape=_sds((nh, LANES), F32), compiler_params=_params(1))(p)


def _reduce_scatter(packs):
    mine, theirs = _sibling_exchange(packs)
    pair = [_add_pair(m, t, m.dtype) for m, t in zip(mine, theirs)]
    by_src = _chip_exchange(pair)
    half = [_add_chips(p) for p in by_src]
    both = _sibling_gather(half)
    return [b.reshape(-1, LANES) for b in both]


def _adamw_math(w, g, m, v):
    m = ADAM_B1 * m + (1.0 - ADAM_B1) * g
    v = ADAM_B2 * v + (1.0 - ADAM_B2) * (g * g)
    m_hat = m / (1.0 - ADAM_B1 ** ADAM_STEP)
    v_hat = v / (1.0 - ADAM_B2 ** ADAM_STEP)
    return -ADAM_LR * (m_hat / (jnp.sqrt(v_hat) + ADAM_EPS) + ADAM_WD * w), m, v


def _adamw_big(w, g, m, v, name):
    r, c = w.shape
    tr = _row_tile(r, c * 4, 8, budget=1 << 19)

    def body(w_ref, g_ref, m_ref, v_ref, d_ref, mo_ref, vo_ref):
        d_ref[...], mo_ref[...], vo_ref[...] = _adamw_math(w_ref[...], g_ref[...], m_ref[...], v_ref[...])

    return pl.pallas_call(
        body, name=name, grid=(r // tr,), in_specs=[_rows(tr, c)] * 4, out_specs=[_rows(tr, c)] * 3,
        out_shape=[_sds((r, c), F32)] * 3, compiler_params=_params(1))(w, g, m, v)


def _adamw_small(ws, gs, ms, vs):
    n = len(ws)

    def body(*refs):
        for a in range(n):
            w_ref, g_ref, m_ref, v_ref = (refs[k * n + a] for k in range(4))
            d, m, v = _adamw_math(w_ref[...], g_ref[...], m_ref[...], v_ref[...])
            refs[4 * n + a][...] = d
            refs[5 * n + a][...] = m
            refs[6 * n + a][...] = v

    vm = pl.BlockSpec(memory_space=pltpu.VMEM)
    outs = pl.pallas_call(
        body, name="adamw_small", in_specs=[vm] * (4 * n), out_specs=[vm] * (3 * n),
        out_shape=[_sds(w.shape, F32) for w in ws] * 3)(*ws, *gs, *ms, *vs)
    return outs[:n], outs[n:2 * n], outs[2 * n:]


BIG = ("w_in", "w_uq", "w_ukv", "w_out", "w_ffn_up", "w_ffn_down")
SMALL_SHARDED = ("conv_w", "ffn_conv_w", "meta_tokens")
REPLICATED = ("mix_norm_g", "q_norm_g", "kv_norm_g", "conv_b", "conv_ln_g", "conv_ln_b", "conv_out_g", "attn_out_g",
              "ffn_norm_g", "ffn_conv_b", "final_norm_g")
WEIGHTS = ("meta_tokens", "mix_norm_g", "w_in", "q_norm_g", "w_uq", "kv_norm_g", "w_ukv", "conv_w", "conv_b",
           "conv_ln_g", "conv_ln_b", "conv_out_g", "attn_out_g", "w_out", "ffn_norm_g", "w_ffn_up", "ffn_conv_w",
           "ffn_conv_b", "w_ffn_down", "final_norm_g")


def _lane_rows(a):
    return a.reshape(N_CHIPS, -1, LANES)


def _col_shards(a):
    k = a.shape[0]
    return a.reshape(k, N_CHIPS, -1).transpose(1, 0, 2)


def _from_col_shards(a):
    return a.transpose(1, 0, 2).reshape(a.shape[1], -1)


def _pad_rows_to(a, rows):
    return jnp.pad(a, ((0, 0), (0, rows - a.shape[1]), (0, 0)))


def _rope_tables(L):
    pos = (jnp.arange(L, dtype=jnp.int32) - DEAD).astype(F32)
    inv_freq = 1.0 / (ROPE_THETA ** (jnp.arange(0, QK_ROPE, 2, dtype=F32) / QK_ROPE))
    ang = pos[:, None] * inv_freq[None, :]
    cos, sin = jnp.cos(ang), jnp.sin(ang)
    half = QK_ROPE // 2
    z = lambda n: jnp.zeros((L, n), F32)
    rc = jnp.concatenate([jnp.ones((L, QK_NOPE), F32), cos, cos, z(HB - QK_NOPE - QK_ROPE)], 1)
    rs1 = jnp.concatenate([z(QK_NOPE), -sin, z(HB - QK_NOPE - half)], 1)
    rs2 = jnp.concatenate([z(QK_NOPE + half), sin, z(HB - QK_NOPE - QK_ROPE)], 1)
    return rc, rs1, rs2


def _pad_heads(g):
    return jnp.pad(g.reshape(N_HEADS, V_HEAD), ((0, 0), (HB - V_HEAD, 0))).reshape(1, D_HEADS)


def _unpad_heads(g):
    return g.reshape(N_HEADS, HB)[:, HB - V_HEAD:].reshape(1, N_HEADS * V_HEAD)


def _local_step(x, target, w):
    S = x.shape[0]
    L = TM + S
    tl = L // 4
    d_qk = QK_NOPE + QK_ROPE
    win_n = w["w_in"]
    kr0 = D_AG + Q_LORA + KV_LORA
    win = jnp.concatenate([win_n[:, :kr0], jnp.zeros((D_MODEL, QK_NOPE), BF16), win_n[:, kr0:],
                           jnp.zeros((D_MODEL, HB - d_qk), BF16)], 1)
    wuq = jnp.pad(w["w_uq"].reshape(Q_LORA, N_HEADS, d_qk), ((0, 0), (0, 0), (0, HB - d_qk))).reshape(Q_LORA, D_HEADS)
    wukv = w["w_ukv"]
    wout_n = w["w_out"]
    wout = jnp.concatenate([wout_n[:D_CONV], jnp.pad(wout_n[D_CONV:].reshape(N_HEADS, V_HEAD, D_MODEL),
                                                     ((0, 0), (HB - V_HEAD, 0), (0, 0))).reshape(D_HEADS, D_MODEL)], 0)
    wup, wdown = w["w_ffn_up"], w["w_ffn_down"]
    ga = _pad_heads(w["attn_out_g"])
    gfin = w["final_norm_g"].reshape(1, D_MODEL)
    rc, rs1, rs2 = _rope_tables(L)
    h0 = jnp.concatenate([jnp.zeros((DEAD, D_MODEL), F32), w["meta_tokens"], x], 0)

    n, zag, u0, cq, ckv, qn, kvn, q, kv, kr = _fwd_in(h0, w["mix_norm_g"], win, w["q_norm_g"], wuq, w["kv_norm_g"],
                                                       wukv, rc, rs1, rs2)
    u1, mixa = _fwd_conv(u0, w["conv_w"], w["conv_b"], w["conv_ln_g"], w["conv_ln_b"], w["conv_out_g"])
    o, lse = _attn_fwd(q, kv, kr)
    mix, h1, n2 = _fwd_mix(h0, mixa, o, ga, wout, w["ffn_norm_g"])
    up0 = _mm(n2, wup, NN, BF16, tl, 2 * D_FF // N_CHIPS, D_MODEL, "ffn_up")
    dh2, loss, g_fin = _fwd_ffn_loss(up0, w["ffn_conv_w"], w["ffn_conv_b"], wdown, h1, target, gfin)

    dup, act, g_fcb = _bwd_ffn_act(dh2, up0, w["ffn_conv_w"], w["ffn_conv_b"], wdown)
    dup0, g_fcw = _bwd_ffn_conv(dup, up0, w["ffn_conv_w"])
    dn2 = _mm(dup0, wup, NT, F32, tl, D_MODEL, 2 * D_FF // N_CHIPS, "ffn_up_dx")
    g_wup = _mm(n2, dup0, TN, F32, D_MODEL, 2 * D_FF // N_CHIPS, tl, "ffn_up_dw")
    g_wdown = _mm(act, dh2, TN, F32, D_FF // 2, D_MODEL, tl, "ffn_down_dw")
    dh1, dh1m, do, du1, g_gffn, g_ga, g_og, g_lg, g_lb = _bwd_mix(
        dh2, dn2, h1, w["ffn_norm_g"], wout, o, ga, u1, w["conv_ln_g"], w["conv_ln_b"], w["conv_out_g"])
    g_wout = _mm(mix, dh1m, TN, F32, D_MIX // 2, D_MODEL, tl, "out_dw")
    dq, dkv, dkr = _attn_bwd(q, kv, kr, o, do, lse)
    dzag, g_cw, g_cb = _bwd_conv(du1, u0, w["conv_w"], zag)
    dz, dqr, gx, dfirst, g_gq, g_gkv, g_gmix = _bwd_in(dzag, dq, dkv, dkr, cq, ckv, w["q_norm_g"], w["kv_norm_g"],
                                                      wuq, wukv, win, rc, rs1, rs2, h0, w["mix_norm_g"], dh1)
    g_win = _mm(n, dz, TN, F32, D_MODEL, D_ZP // 2, tl, "in_dw")
    g_wuq = _mm(qn, dqr, TN, F32, Q_LORA, D_HEADS, tl, "uq_dw")
    g_wukv = _mm(kvn, dkv, TN, F32, KV_LORA, D_HEADS, tl, "ukv_dw")

    grads = {
        "w_in": jnp.concatenate([g_win[:, :kr0], g_win[:, kr0 + QK_NOPE:kr0 + d_qk]], 1),
        "w_uq": g_wuq.reshape(Q_LORA, N_HEADS, HB)[:, :, :d_qk].reshape(Q_LORA, N_HEADS * d_qk),
        "w_ukv": g_wukv,
        "w_out": jnp.concatenate([g_wout[:D_CONV], g_wout[D_CONV:].reshape(N_HEADS, HB, D_MODEL)[:, HB - V_HEAD:]
                                  .reshape(N_HEADS * V_HEAD, D_MODEL)], 0),
        "w_ffn_up": g_wup, "w_ffn_down": g_wdown, "conv_w": g_cw, "ffn_conv_w": g_fcw,
        "meta_tokens": dfirst[DEAD:], "mix_norm_g": g_gmix, "q_norm_g": g_gq, "kv_norm_g": g_gkv, "conv_b": g_cb,
        "conv_ln_g": g_lg, "conv_ln_b": g_lb, "conv_out_g": g_og, "attn_out_g": _unpad_heads(g_ga),
        "ffn_norm_g": g_gffn, "ffn_conv_b": g_fcb, "final_norm_g": g_fin,
    }
    return loss, gx, grads


ROW_SHARDED = ("w_out", "w_ffn_down")


def _owner_pieces(name, g):
    if name in ROW_SHARDED:
        return _lane_rows(g.reshape(N_CHIPS, -1))
    return _lane_rows(_col_shards(g))


def _split_halves(p):
    return p.reshape(N_CHIPS, 2, p.shape[1] // 2, LANES).transpose(1, 0, 2, 3)


def kernel(x, meta_tokens, mix_norm_g, w_in, q_norm_g, w_uq, kv_norm_g, w_ukv, conv_w, conv_b, conv_ln_g, conv_ln_b, conv_out_g, attn_out_g, w_out, ffn_norm_g, w_ffn_up, ffn_conv_w, ffn_conv_b, w_ffn_down, final_norm_g, loss_target, m_meta_tokens, m_mix_norm_g, m_w_in, m_q_norm_g, m_w_uq, m_kv_norm_g, m_w_ukv, m_conv_w, m_conv_b, m_conv_ln_g, m_conv_ln_b, m_conv_out_g, m_attn_out_g, m_w_out, m_ffn_norm_g, m_w_ffn_up, m_ffn_conv_w, m_ffn_conv_b, m_w_ffn_down, m_final_norm_g, v_meta_tokens, v_mix_norm_g, v_w_in, v_q_norm_g, v_w_uq, v_kv_norm_g, v_w_ukv, v_conv_w, v_conv_b, v_conv_ln_g, v_conv_ln_b, v_conv_out_g, v_attn_out_g, v_w_out, v_ffn_norm_g, v_w_ffn_up, v_ffn_conv_w, v_ffn_conv_b, v_w_ffn_down, v_final_norm_g):
    local = dict(meta_tokens=meta_tokens, mix_norm_g=mix_norm_g, w_in=w_in[0], q_norm_g=q_norm_g, w_uq=w_uq[0],
                 kv_norm_g=kv_norm_g, w_ukv=w_ukv[0], conv_w=conv_w[0], conv_b=conv_b, conv_ln_g=conv_ln_g,
                 conv_ln_b=conv_ln_b, conv_out_g=conv_out_g, attn_out_g=attn_out_g, w_out=w_out[0],
                 ffn_norm_g=ffn_norm_g, w_ffn_up=w_ffn_up[0], ffn_conv_w=ffn_conv_w[0], ffn_conv_b=ffn_conv_b,
                 w_ffn_down=w_ffn_down[0], final_norm_g=final_norm_g.reshape(1, D_MODEL))
    ms = dict(zip(WEIGHTS, (m_meta_tokens, m_mix_norm_g, m_w_in, m_q_norm_g, m_w_uq, m_kv_norm_g, m_w_ukv, m_conv_w,
                            m_conv_b, m_conv_ln_g, m_conv_ln_b, m_conv_out_g, m_attn_out_g, m_w_out, m_ffn_norm_g,
                            m_w_ffn_up, m_ffn_conv_w, m_ffn_conv_b, m_w_ffn_down, m_final_norm_g)))
    vs = dict(zip(WEIGHTS, (v_meta_tokens, v_mix_norm_g, v_w_in, v_q_norm_g, v_w_uq, v_kv_norm_g, v_w_ukv, v_conv_w,
                            v_conv_b, v_conv_ln_g, v_conv_ln_b, v_conv_out_g, v_attn_out_g, v_w_out, v_ffn_norm_g,
                            v_w_ffn_up, v_ffn_conv_w, v_ffn_conv_b, v_w_ffn_down, v_final_norm_g)))

    small_flat = jnp.concatenate([local[k].reshape(-1) for k in SMALL_SHARDED]).reshape(-1, LANES)
    gathered = _all_gather([local[k].astype(BF16) for k in BIG] + [small_flat])
    full = dict(local)
    for name, g in zip(BIG, gathered[:len(BIG)]):
        full[name] = g.reshape(-1, g.shape[-1]) if name in ROW_SHARDED else _from_col_shards(g)
    small = gathered[-1].reshape(N_CHIPS, -1)
    at = 0
    for name in SMALL_SHARDED:
        r, c = local[name].shape
        full[name] = _from_col_shards(small[:, at:at + r * c].reshape(N_CHIPS, r, c))
        at += r * c

    loss_row, grad_x, grads = _local_step(x[0], loss_target[0], full)

    big_pack = _split_halves(jnp.concatenate([_owner_pieces(k, grads[k]) for k in BIG], 1).astype(BF16))
    rep = jnp.concatenate([grads[k].reshape(-1) for k in REPLICATED] + [loss_row.reshape(-1)]).reshape(1, -1, LANES)
    small_pieces = [_owner_pieces(k, grads[k]) for k in SMALL_SHARDED] + [jnp.broadcast_to(rep, (N_CHIPS,) + rep.shape[1:])]
    small_rows = sum(p.shape[1] for p in small_pieces)
    small_pack = _split_halves(_pad_rows_to(jnp.concatenate(small_pieces, 1), -(-small_rows // 16) * 16))
    big_tot, small_tot = _reduce_scatter([big_pack, small_pack])

    total = {}
    at = 0
    for name in BIG:
        r, c = local[name].shape
        total[name] = big_tot[at:at + r * c // LANES].reshape(r, c)
        at += r * c // LANES
    flat = small_tot.reshape(-1)
    at = 0
    for name in SMALL_SHARDED + REPLICATED:
        shape = local[name].shape
        size = shape[0] * shape[1]
        total[name] = flat[at:at + size].reshape(shape)
        at += -(-size // LANES) * LANES if name in SMALL_SHARDED else size
    loss = flat[at]

    delta, new_m, new_v = {}, {}, {}
    shape2 = lambda a, name: a.reshape(local[name].shape)
    for name in BIG:
        delta[name], new_m[name], new_v[name] = _adamw_big(local[name], total[name], shape2(ms[name], name),
                                                          shape2(vs[name], name), "adamw_" + name)
    rest = SMALL_SHARDED + REPLICATED
    ds, nms, nvs = _adamw_small([local[k] for k in rest], [total[k] for k in rest],
                                [shape2(ms[k], k) for k in rest], [shape2(vs[k], k) for k in rest])
    for k, d, nm, nv in zip(rest, ds, nms, nvs):
        delta[k], new_m[k], new_v[k] = d, nm, nv

    out_shape = dict(zip(WEIGHTS, (meta_tokens, mix_norm_g, w_in, q_norm_g, w_uq, kv_norm_g, w_ukv, conv_w, conv_b,
                                   conv_ln_g, conv_ln_b, conv_out_g, attn_out_g, w_out, ffn_norm_g, w_ffn_up,
                                   ffn_conv_w, ffn_conv_b, w_ffn_down, final_norm_g)))
    outs = [loss, grad_x[None]]
    for group in (total, delta, new_m, new_v):
        outs += [group[k].reshape(out_shape[k].shape) for k in WEIGHTS]
    return tuple(outs)
```

```python
import functools

import jax
import jax.numpy as jnp
from jax import lax
from jax.experimental import pallas as pl
from jax.experimental.pallas import tpu as pltpu

F32 = jnp.float32
BF16 = jnp.bfloat16

D_MODEL = 1024
D_CONV = 512
CONV_WIDTH = 31
N_HEADS = 8
QK_NOPE = 64
QK_ROPE = 32
V_HEAD = 64
Q_LORA = 384
KV_LORA = 256
D_FF = 2816
FFN_CONV_WIDTH = 3
CHUNK_SHIFT = 6
N_META = 16
ROPE_THETA = 10000.0
EPS = 1e-6
NEG = -1e30
ADAM_LR = 0.001
ADAM_B1 = 0.9
ADAM_B2 = 0.999
ADAM_EPS = 1e-08
ADAM_WD = 0.01
ADAM_STEP = 10

LANES = 128
HB = LANES
D_HEADS = N_HEADS * HB
TM = 256
DEAD = TM - N_META
D_AG = 2 * D_CONV
D_ZP = D_AG + Q_LORA + KV_LORA + HB
D_MIX = D_CONV + D_HEADS
SCALE = (QK_NOPE + QK_ROPE) ** -0.5
HALO_CONV = 32
HALO_FFN = 16
FF_CHUNK = 256
FF_TILE = D_FF // 2
VMEM_LIMIT = 56 * 1024 * 1024
N_CHIPS = 4
BULK_COPY_BYTES = 1 << 20
BULK_COPY_SPLITS = (12, 8, 6, 4, 3, 2)
HEAD_GROUP = 4
N_GROUPS = N_HEADS // HEAD_GROUP
MESH =pl.DeviceIdType.MESH


def _params(n_grid):
    return pltpu.CompilerParams(dimension_semantics=("arbitrary",) * n_grid, vmem_limit_bytes=VMEM_LIMIT)


def _rows(tm, c, off=0):
    return pl.BlockSpec((tm, c), lambda i: (i, off))


def _full(shape):
    return pl.BlockSpec(shape, lambda i: (0,) * len(shape))


def _prev(hb, c, tm, off=0):
    return pl.BlockSpec((hb, c), lambda i: (jnp.maximum(i * (tm // hb) - 1, 0), off))


def _next(hb, c, tm, nblk, off=0):
    return pl.BlockSpec((hb, c), lambda i: (jnp.minimum((i + 1) * (tm // hb), nblk - 1), off))


def _sds(shape, dtype):
    return jax.ShapeDtypeStruct(shape, dtype)


def _rms_r(x, n):
    return lax.rsqrt(jnp.sum(x * x, -1, keepdims=True) * (1.0 / n) + EPS)


def _rms_bwd(dy, x, r, g, n):
    gd = dy * g
    dx = r * gd - x * (r * r * r) * (jnp.sum(x * gd, -1, keepdims=True) * (1.0 / n))
    return dx, jnp.sum(dy * x * r, 0, keepdims=True)


def _dot(a, b, dims):
    return lax.dot_general(a, b, (dims, ((), ())), preferred_element_type=F32)


NN = ((1,), (0,))
NT = ((1,), (1,))
TN = ((0,), (0,))


def _rope(x, c, s1, s2):
    n = x.shape[-1]
    return x * c + pltpu.roll(x, n - QK_ROPE // 2, 1) * s1 + pltpu.roll(x, QK_ROPE // 2, 1) * s2


def _rope_bwd(g, c, s1, s2):
    n = g.shape[-1]
    return g * c + pltpu.roll(g * s1, QK_ROPE // 2, 1) + pltpu.roll(g * s2, n - QK_ROPE // 2, 1)


def _row_ids(tm, cols=1):
    return pl.program_id(0) * tm + lax.broadcasted_iota(jnp.int32, (tm, cols), 0)


def _mm(a, b, dims, out_dtype, tm, tn, tk, name):
    if dims == TN:
        (kk, m), (_, n) = a.shape, b.shape
        a_spec = pl.BlockSpec((tk, tm), lambda i, j, k: (k, i))
    else:
        m, kk = a.shape
        a_spec = pl.BlockSpec((tm, tk), lambda i, j, k: (i, k))
    if dims == NT:
        n = b.shape[0]
        b_spec = pl.BlockSpec((tn, tk), lambda i, j, k: (j, k))
    else:
        n = b.shape[1]
        b_spec = pl.BlockSpec((tk, tn), lambda i, j, k: (k, j))
    assert m % tm == 0 and n % tn == 0 and kk % tk == 0, (name, a.shape, b.shape, tm, tn, tk)
    nk = kk // tk

    def body(a_ref, b_ref, o_ref, acc_ref):
        k = pl.program_id(2)

        @pl.when(k == 0)
        def _():
            acc_ref[...] = jnp.zeros_like(acc_ref)

        acc_ref[...] += _dot(a_ref[...].astype(BF16), b_ref[...].astype(BF16), dims)

        @pl.when(k == nk - 1)
        def _():
            o_ref[...] = acc_ref[...].astype(out_dtype)

    return pl.pallas_call(
        body, name=name, grid=(m // tm, n // tn, nk), in_specs=[a_spec, b_spec],
        out_specs=pl.BlockSpec((tm, tn), lambda i, j, k: (i, j)), out_shape=_sds((m, n), out_dtype),
        scratch_shapes=[pltpu.VMEM((tm, tn), F32)], compiler_params=_params(3))(a, b)


def _fwd_in(h0, gmix, win, gq, wuq, gkv, wukv, rc, rs1, rs2):
    L = h0.shape[0]

    def body(h_ref, gmix_ref, win_ref, gq_ref, wuq_ref, gkv_ref, wukv_ref, c_ref, s1_ref, s2_ref,
             n_ref, zag_ref, u0_ref, cq_ref, ckv_ref, qn_ref, kvn_ref, q_ref, kv_ref, kr_ref):
        h = h_ref[...]
        n = (h * _rms_r(h, D_MODEL) * gmix_ref[...]).astype(BF16)
        n_ref[...] = n
        z = _dot(n, win_ref[...], NN)
        a, gate = z[:, :D_CONV], z[:, D_CONV:D_AG]
        zag_ref[...] = z[:, :D_AG].astype(BF16)
        u0_ref[...] = a * jax.nn.sigmoid(gate)
        cq = z[:, D_AG:D_AG + Q_LORA]
        ckv = z[:, D_AG + Q_LORA:D_AG + Q_LORA + KV_LORA]
        krp = z[:, D_AG + Q_LORA + KV_LORA:]
        cq_ref[...] = cq
        ckv_ref[...] = ckv
        qn = (cq * _rms_r(cq, Q_LORA) * gq_ref[...]).astype(BF16)
        qn_ref[...] = qn
        kvn = (ckv * _rms_r(ckv, KV_LORA) * gkv_ref[...]).astype(BF16)
        kvn_ref[...] = kvn
        c, s1, s2 = c_ref[...], s1_ref[...], s2_ref[...]
        q = _dot(qn, wuq_ref[...], NN)
        q = _rope(q, jnp.tile(c, (1, N_HEADS)), jnp.tile(s1, (1, N_HEADS)), jnp.tile(s2, (1, N_HEADS)))
        q_ref[...] = q.astype(BF16)
        kv_ref[...] = _dot(kvn, wukv_ref[...], NN).astype(BF16)
        kr_ref[...] = _rope(krp, c, s1, s2).astype(BF16)

    outs = [(D_MODEL, BF16), (D_AG, BF16), (D_CONV, F32), (Q_LORA, F32), (KV_LORA, F32), (Q_LORA, BF16),
            (KV_LORA, BF16), (D_HEADS, BF16), (D_HEADS, BF16), (HB, BF16)]
    return pl.pallas_call(
        body, name="fwd_in", grid=(L // TM,),
        in_specs=[_rows(TM, D_MODEL), _full(gmix.shape), _full(win.shape), _full(gq.shape), _full(wuq.shape),
                  _full(gkv.shape), _full(wukv.shape), _rows(TM, HB), _rows(TM, HB), _rows(TM, HB)],
        out_specs=[_rows(TM, c) for c, _ in outs], out_shape=[_sds((L, c), d) for c, d in outs],
        compiler_params=_params(1))(h0, gmix, win, gq, wuq, gkv, wukv, rc, rs1, rs2)


def _conv_taps(xx, w_ref, halo, tm, flip):
    kw = w_ref.shape[0]
    rows = xx.shape[0]
    acc = None
    for k in range(kw):
        d = kw - 1 - k
        shifted = xx if d == 0 else pltpu.roll(xx, (rows - d) if flip else d, 0)
        term = w_ref[k:k + 1, :] * (shifted[:tm] if flip else shifted[halo:])
        acc = term if acc is None else acc + term
    return acc


def _ln_silu(u1, lg, lb):
    mu = jnp.mean(u1, -1, keepdims=True)
    xc = u1 - mu
    rs = lax.rsqrt(jnp.mean(xc * xc, -1, keepdims=True) + EPS)
    xh = xc * rs
    u2 = xh * lg + lb
    sg = jax.nn.sigmoid(u2)
    return rs, xh, u2, sg, u2 * sg


def _fwd_conv(u0, cw, cb, lg, lb, og):
    L = u0.shape[0]

    def body(u0_ref, u0p_ref, cw_ref, cb_ref, lg_ref, lb_ref, og_ref, u1_ref, mixa_ref):
        halo = jnp.where(pl.program_id(0) > 0, u0p_ref[...], 0.0)
        xx = jnp.concatenate([halo, u0_ref[...]], 0)
        u1 = _conv_taps(xx, cw_ref, HALO_CONV, TM, False) + cb_ref[...]
        u1_ref[...] = u1
        u = _ln_silu(u1, lg_ref[...], lb_ref[...])[4]
        mixa_ref[...] = (u * _rms_r(u, D_CONV) * og_ref[...]).astype(BF16)

    return pl.pallas_call(
        body, name="fwd_conv", grid=(L // TM,),
        in_specs=[_rows(TM, D_CONV), _prev(HALO_CONV, D_CONV, TM), _full(cw.shape), _full(cb.shape),
                  _full(lg.shape), _full(lb.shape), _full(og.shape)],
        out_specs=[_rows(TM, D_CONV), _rows(TM, D_CONV)],
        out_shape=[_sds((L, D_CONV), F32), _sds((L, D_CONV), BF16)],
        compiler_params=_params(1))(u0, u0, cw, cb, lg, lb, og)


def _visible(i, j, t):
    row = i * t + lax.broadcasted_iota(jnp.int32, (t, t), 0)
    col = j * t + lax.broadcasted_iota(jnp.int32, (t, t), 1)
    return (lax.shift_right_logical(col, CHUNK_SHIFT) <= lax.shift_right_logical(row, CHUNK_SHIFT)) & (col >= DEAD)


def _stat_lane(h):
    return (h // HEAD_GROUP) * LANES + h % HEAD_GROUP


def _scatter_stats(cols, t):
    lane = lax.broadcasted_iota(jnp.int32, (t, N_GROUPS * LANES), 1)
    out = jnp.zeros((t, N_GROUPS * LANES), F32)
    for h, col in enumerate(cols):
        out = jnp.where(lane == _stat_lane(h), col, out)
    return out


def _resident(shape, index_map):
    return pl.BlockSpec(shape, index_map, pipeline_mode=pl.Buffered(1))


def _attn_fwd(q, kv, kr):
    L = q.shape[0]
    t = TM
    nq = L // t
    rep = t // HB

    def body(q_ref, kv_ref, kr_ref, o_ref, lse_ref, m_scr, l_scr, acc_scr):
        i = pl.program_id(0)
        lane = lax.broadcasted_iota(jnp.int32, (t, HB), 1)
        m_scr[...] = jnp.full_like(m_scr, NEG)
        l_scr[...] = jnp.zeros_like(l_scr)
        acc_scr[...] = jnp.zeros_like(acc_scr)

        def tile(j, vis):
            rows = pl.ds(pl.multiple_of(j * t, t), t)
            krj = kr_ref[rows, :]
            for h in range(N_HEADS):
                hc = slice(h * HB, (h + 1) * HB)
                kvj = kv_ref[rows, hc]
                s = _dot(q_ref[:, hc], jnp.where(lane < QK_NOPE, kvj, krj), NT) * SCALE
                if vis is not None:
                    s = jnp.where(vis, s, NEG)
                m_prev = m_scr[h]
                m_new = jnp.maximum(m_prev, jnp.max(s, -1, keepdims=True))
                alpha = jnp.exp(m_prev - m_new)
                p = jnp.exp(s - jnp.concatenate([m_new] * rep, 1))
                l_scr[h] = alpha * l_scr[h] + jnp.sum(p, -1, keepdims=True)
                acc_scr[:, hc] = alpha * acc_scr[:, hc] + _dot(p.astype(BF16), kvj, NN)
                m_scr[h] = m_new

        tile(i, _visible(i, i, t))

        @pl.when(i > 0)
        def _():
            tile(0, _visible(i, 0, t))

        def unmasked(j, carry):
            tile(j, None)
            return carry

        lax.fori_loop(1, i, unmasked, 0)
        lses = []
        for h in range(N_HEADS):
            hc = slice(h * HB, (h + 1) * HB)
            l = l_scr[h]
            o_ref[:, hc] = jnp.where(lane >= QK_NOPE, acc_scr[:, hc] / l, 0.0).astype(BF16)
            lses.append((m_scr[h] + jnp.log(l))[:, :1])
        lse_ref[...] = _scatter_stats(lses, t)

    return pl.pallas_call(
        body, name="attn_fwd", grid=(nq,),
        in_specs=[_rows(t, D_HEADS), _resident((L, D_HEADS), lambda i: (0, 0)), _resident((L, HB), lambda i: (0, 0))],
        out_specs=[_rows(t, D_HEADS), _rows(t, N_GROUPS * LANES)],
        out_shape=[_sds((L, D_HEADS), BF16), _sds((L, N_GROUPS * LANES), F32)],
        scratch_shapes=[pltpu.VMEM((N_HEADS, t, HB), F32), pltpu.VMEM((N_HEADS, t, HB), F32),
                        pltpu.VMEM((t, D_HEADS), F32)],
        compiler_params=_params(1))(q, kv, kr)


def _fwd_mix(h0, mixa, o, ga, wout, gffn):
    L = h0.shape[0]

    def body(h0_ref, mixa_ref, o_ref, ga_ref, wout_ref, gffn_ref, mix_ref, h1_ref, n2_ref):
        of = o_ref[...].astype(F32)
        mixb = (of * _rms_r(of, N_HEADS * V_HEAD) * ga_ref[...]).astype(BF16)
        mix = jnp.concatenate([mixa_ref[...], mixb], 1)
        mix_ref[...] = mix
        mo = jnp.where(_row_ids(TM) >= DEAD, _dot(mix, wout_ref[...], NN), 0.0)
        h1 = h0_ref[...] + mo
        h1_ref[...] = h1
        n2_ref[...] = (h1 * _rms_r(h1, D_MODEL) * gffn_ref[...]).astype(BF16)

    return pl.pallas_call(
        body, name="fwd_mix", grid=(L // TM,),
        in_specs=[_rows(TM, D_MODEL), _rows(TM, D_CONV), _rows(TM, D_HEADS), _full(ga.shape), _full(wout.shape),
                  _full(gffn.shape)],
        out_specs=[_rows(TM, D_MIX), _rows(TM, D_MODEL), _rows(TM, D_MODEL)],
        out_shape=[_sds((L, D_MIX), BF16), _sds((L, D_MODEL), F32), _sds((L, D_MODEL), BF16)],
        compiler_params=_params(1))(h0, mixa, o, ga, wout, gffn)


def _ffn_act_chunk(c, upg_ref, upv_ref, hg, hv, fcw_ref, fcb_ref):
    cs = slice(c * FF_CHUNK, (c + 1) * FF_CHUNK)
    out = []
    for part, (up_ref, halo) in enumerate(((upg_ref, hg), (upv_ref, hv))):
        xx = jnp.concatenate([halo[:, cs], up_ref[:, cs].astype(F32)], 0)
        ws = slice(part * D_FF + c * FF_CHUNK, part * D_FF + (c + 1) * FF_CHUNK)
        y = (fcw_ref[0:1, ws] * pltpu.roll(xx, 2, 0)[HALO_FFN:] + fcw_ref[1:2, ws] * pltpu.roll(xx, 1, 0)[HALO_FFN:]
             + fcw_ref[2:3, ws] * xx[HALO_FFN:] + fcb_ref[:, ws])
        out.append(y)
    return out


def _ffn_in_specs(L):
    return [_rows(TM, D_FF, 0), _rows(TM, D_FF, 1), _prev(HALO_FFN, D_FF, TM, 0), _prev(HALO_FFN, D_FF, TM, 1)]


def _ffn_halos(hg_ref, hv_ref):
    first = pl.program_id(0) == 0
    return (jnp.where(first, 0.0, hg_ref[...].astype(F32)), jnp.where(first, 0.0, hv_ref[...].astype(F32)))


def _fwd_ffn_loss(up0, fcw, fcb, wdown, h1, target, gfin):
    L = h1.shape[0]

    def body(upg_ref, upv_ref, hg_ref, hv_ref, fcw_ref, fcb_ref, wd_ref, h1_ref, t_ref, gf_ref,
             dh2_ref, loss_ref, dgf_ref, act_ref):
        i = pl.program_id(0)
        hg, hv = _ffn_halos(hg_ref, hv_ref)
        for c in range(D_FF // FF_CHUNK):
            g, val = _ffn_act_chunk(c, upg_ref, upv_ref, hg, hv, fcw_ref, fcb_ref)
            act_ref[:, c * FF_CHUNK:(c + 1) * FF_CHUNK] = (g * jax.nn.sigmoid(g) * val).astype(BF16)
        h2 = h1_ref[...] + _dot(act_ref[...], wd_ref[...], NN)
        r = _rms_r(h2, D_MODEL)
        gf = gf_ref[...]
        err = jnp.where(i > 0, h2 * r * gf - t_ref[...], 0.0)
        dy = err * (1.0 / D_MODEL)
        dh2, dgf = _rms_bwd(dy, h2, r, gf, D_MODEL)
        dh2_ref[...] = dh2

        @pl.when(i == 0)
        def _():
            loss_ref[...] = jnp.zeros_like(loss_ref)
            dgf_ref[...] = jnp.zeros_like(dgf_ref)

        loss_ref[...] += jnp.sum(err * err) * (0.5 / D_MODEL)
        dgf_ref[...] += dgf

    return pl.pallas_call(
        body, name="fwd_ffn_loss", grid=(L // TM,),
        in_specs=_ffn_in_specs(L) + [_full(fcw.shape), _full(fcb.shape), _full(wdown.shape), _rows(TM, D_MODEL),
                                     pl.BlockSpec((TM, D_MODEL), lambda i: (jnp.maximum(i - 1, 0), 0)),
                                     _full(gfin.shape)],
        out_specs=[_rows(TM, D_MODEL), _full((1, LANES)), _full((1, D_MODEL))],
        out_shape=[_sds((L, D_MODEL), F32), _sds((1, LANES), F32), _sds((1, D_MODEL), F32)],
        scratch_shapes=[pltpu.VMEM((TM, D_FF), BF16)],
        compiler_params=_params(1))(up0, up0, up0, up0, fcw, fcb, wdown, h1, target, gfin)


def _bwd_ffn_act(dh2, up0, fcw, fcb, wdown):
    L = dh2.shape[0]

    def body(dh2_ref, upg_ref, upv_ref, hg_ref, hv_ref, fcw_ref, fcb_ref, wd_ref, dup_ref, act_ref, dfcb_ref):
        hg, hv = _ffn_halos(hg_ref, hv_ref)
        da = _dot(dh2_ref[...].astype(BF16), wd_ref[...], NT)

        @pl.when(pl.program_id(0) == 0)
        def _():
            dfcb_ref[...] = jnp.zeros_like(dfcb_ref)

        for c in range(D_FF // FF_CHUNK):
            cs = slice(c * FF_CHUNK, (c + 1) * FF_CHUNK)
            vs = slice(D_FF + c * FF_CHUNK, D_FF + (c + 1) * FF_CHUNK)
            g, val = _ffn_act_chunk(c, upg_ref, upv_ref, hg, hv, fcw_ref, fcb_ref)
            sg = jax.nn.sigmoid(g)
            si = g * sg
            dac = da[:, cs]
            act_ref[:, cs] = (si * val).astype(BF16)
            dg = dac * val * (sg * (1.0 + g * (1.0 - sg)))
            dv = dac * si
            dup_ref[:, cs] = dg.astype(BF16)
            dup_ref[:, vs] = dv.astype(BF16)
            dfcb_ref[:, cs] += jnp.sum(dg, 0, keepdims=True)
            dfcb_ref[:, vs] += jnp.sum(dv, 0, keepdims=True)

    return pl.pallas_call(
        body, name="bwd_ffn_act", grid=(L // TM,),
        in_specs=[_rows(TM, D_MODEL)] + _ffn_in_specs(L) + [_full(fcw.shape), _full(fcb.shape), _full(wdown.shape)],
        out_specs=[_rows(TM, 2 * D_FF), _rows(TM, D_FF), _full((1, 2 * D_FF))],
        out_shape=[_sds((L, 2 * D_FF), BF16), _sds((L, D_FF), BF16), _sds((1, 2 * D_FF), F32)],
        compiler_params=_params(1))(dh2, up0, up0, up0, up0, fcw, fcb, wdown)


def _bwd_ffn_conv(dup, up0, fcw):
    L, C = dup.shape
    tc = FF_TILE
    nt = L // TM
    nhb = L // HALO_FFN

    def body(dy_ref, dyn_ref, x_ref, xp_ref, w_ref, dx_ref, dw_ref):
        i = pl.program_id(1)
        yy = jnp.concatenate([dy_ref[...].astype(F32), jnp.where(i < nt - 1, dyn_ref[...].astype(F32), 0.0)], 0)
        dx_ref[...] = _conv_taps(yy, w_ref, HALO_FFN, TM, True).astype(BF16)
        xx = jnp.concatenate([jnp.where(i > 0, xp_ref[...].astype(F32), 0.0), x_ref[...].astype(F32)], 0)
        dy = yy[:TM]

        @pl.when(i == 0)
        def _():
            dw_ref[...] = jnp.zeros_like(dw_ref)

        for k in range(FFN_CONV_WIDTH):
            d = FFN_CONV_WIDTH - 1 - k
            xs = xx if d == 0 else pltpu.roll(xx, d, 0)
            dw_ref[k:k + 1, :] += jnp.sum(dy * xs[HALO_FFN:], 0, keepdims=True)

    col = lambda j, i: (i, j)
    return pl.pallas_call(
        body, name="bwd_ffn_conv", grid=(C // tc, nt),
        in_specs=[pl.BlockSpec((TM, tc), col),
                  pl.BlockSpec((HALO_FFN, tc), lambda j, i: (jnp.minimum((i + 1) * (TM // HALO_FFN), nhb - 1), j)),
                  pl.BlockSpec((TM, tc), col),
                  pl.BlockSpec((HALO_FFN, tc), lambda j, i: (jnp.maximum(i * (TM // HALO_FFN) - 1, 0), j)),
                  pl.BlockSpec((FFN_CONV_WIDTH, tc), lambda j, i: (0, j))],
        out_specs=[pl.BlockSpec((TM, tc), col), pl.BlockSpec((FFN_CONV_WIDTH, tc), lambda j, i: (0, j))],
        out_shape=[_sds((L, C), BF16), _sds((FFN_CONV_WIDTH, C), F32)],
        compiler_params=_params(2))(dup, dup, up0, up0, fcw)


def _bwd_mix(dh2, dn2, h1, gffn, wout, o, ga, u1, lg, lb, og):
    L = h1.shape[0]

    def body(dh2_ref, dn2_ref, h1_ref, gffn_ref, wout_ref, o_ref, ga_ref, u1_ref, lg_ref, lb_ref, og_ref,
             dh1_ref, dh1m_ref, do_ref, delta_ref, du1_ref, dgffn_ref, dga_ref, dog_ref, dlg_ref, dlb_ref):
        h1 = h1_ref[...]
        dn2x, dgffn = _rms_bwd(dn2_ref[...], h1, _rms_r(h1, D_MODEL), gffn_ref[...], D_MODEL)
        dh1 = dh2_ref[...] + dn2x
        dh1_ref[...] = dh1
        dh1m = jnp.where(_row_ids(TM) >= DEAD, dh1, 0.0).astype(BF16)
        dh1m_ref[...] = dh1m
        dmix = _dot(dh1m, wout_ref[...], NT)
        dma, dmb = dmix[:, :D_CONV], dmix[:, D_CONV:]
        of = o_ref[...].astype(F32)
        do, dga = _rms_bwd(dmb, of, _rms_r(of, N_HEADS * V_HEAD), ga_ref[...], N_HEADS * V_HEAD)
        do_ref[...] = do.astype(BF16)
        prod = do * of
        delta_ref[...] = _scatter_stats(
            [jnp.sum(prod[:, h * HB:(h + 1) * HB], -1, keepdims=True) for h in range(N_HEADS)], TM)
        lg = lg_ref[...]
        rs, xh, u2, sg, u = _ln_silu(u1_ref[...], lg, lb_ref[...])
        du, dog = _rms_bwd(dma, u, _rms_r(u, D_CONV), og_ref[...], D_CONV)
        du2 = du * (sg * (1.0 + u2 * (1.0 - sg)))
        dxh = du2 * lg
        du1_ref[...] = rs * (dxh - jnp.mean(dxh, -1, keepdims=True) - xh * jnp.mean(dxh * xh, -1, keepdims=True))

        @pl.when(pl.program_id(0) == 0)
        def _():
            for ref in (dgffn_ref, dga_ref, dog_ref, dlg_ref, dlb_ref):
                ref[...] = jnp.zeros_like(ref)

        dgffn_ref[...] += dgffn
        dga_ref[...] += dga
        dog_ref[...] += dog
        dlg_ref[...] += jnp.sum(du2 * xh, 0, keepdims=True)
        dlb_ref[...] += jnp.sum(du2, 0, keepdims=True)

    return pl.pallas_call(
        body, name="bwd_mix", grid=(L // TM,),
        in_specs=[_rows(TM, D_MODEL), _rows(TM, D_MODEL), _rows(TM, D_MODEL), _full(gffn.shape), _full(wout.shape),
                  _rows(TM, D_HEADS), _full(ga.shape), _rows(TM, D_CONV), _full(lg.shape), _full(lb.shape),
                  _full(og.shape)],
        out_specs=[_rows(TM, D_MODEL), _rows(TM, D_MODEL), _rows(TM, D_HEADS), _rows(TM, N_GROUPS * LANES),
                   _rows(TM, D_CONV), _full((1, D_MODEL)), _full((1, D_HEADS)), _full((1, D_CONV)),
                   _full((1, D_CONV)), _full((1, D_CONV))],
        out_shape=[_sds((L, D_MODEL), F32), _sds((L, D_MODEL), BF16), _sds((L, D_HEADS), BF16),
                   _sds((L, N_GROUPS * LANES), F32),
                   _sds((L, D_CONV), F32), _sds((1, D_MODEL), F32), _sds((1, D_HEADS), F32), _sds((1, D_CONV), F32),
                   _sds((1, D_CONV), F32), _sds((1, D_CONV), F32)],
        compiler_params=_params(1))(dh2, dn2, h1, gffn, wout, o, ga, u1, lg, lb, og)


def _attn_bwd(q, kv, kr, do, lse, delta):
    L = q.shape[0]
    t = TM
    nt = L // t
    gw = HEAD_GROUP * HB

    def body(q_ref, kv_ref, kr_ref, do_ref, lse_ref, delta_ref, dq_ref, dkv_ref, dkr_ref, dq_acc, dk_acc, dv_acc):
        g, j = pl.program_id(0), pl.program_id(1)
        lane = lax.broadcasted_iota(jnp.int32, (t, HB), 1)

        @pl.when(j == 0)
        def _():
            dq_acc[...] = jnp.zeros_like(dq_acc)

        @pl.when((j == 0) & (g == 0))
        def _():
            dkr_ref[...] = jnp.zeros_like(dkr_ref)

        dk_acc[...] = jnp.zeros_like(dk_acc)
        dv_acc[...] = jnp.zeros_like(dv_acc)
        krj = kr_ref[...]

        def tile(i, vis):
            rows = pl.ds(pl.multiple_of(i * t, t), t)
            lse_i, delta_i = lse_ref[rows, :], delta_ref[rows, :]
            for h in range(HEAD_GROUP):
                hc = slice(h * HB, (h + 1) * HB)
                kvj = kv_ref[:, hc]
                kk = jnp.where(lane < QK_NOPE, kvj, krj)
                qi, doi = q_ref[rows, hc], do_ref[rows, hc]
                s = _dot(qi, kk, NT) * SCALE
                if vis is not None:
                    s = jnp.where(vis, s, NEG)
                p = jnp.exp(s - lse_i[:, h:h + 1])
                dv_acc[:, hc] += _dot(p.astype(BF16), doi, TN)
                ds = (p * (_dot(doi, kvj, NT) - delta_i[:, h:h + 1]) * SCALE).astype(BF16)
                dk_acc[:, hc] += _dot(ds, qi, TN)
                dq_acc[rows, hc] += _dot(ds, kk, NN)

        tile(j, _visible(j, j, t))

        @pl.when(j == 0)
        def _():
            def masked(i, carry):
                tile(i, _visible(i, 0, t))
                return carry

            lax.fori_loop(1, nt, masked, 0)

        @pl.when(j > 0)
        def _():
            def unmasked(i, carry):
                tile(i, None)
                return carry

            lax.fori_loop(j + 1, nt, unmasked, 0)

        dkr = jnp.zeros((t, HB), F32)
        for h in range(HEAD_GROUP):
            hc = slice(h * HB, (h + 1) * HB)
            dk = dk_acc[:, hc]
            dkv_ref[:, hc] = jnp.where(lane < QK_NOPE, dk, dv_acc[:, hc]).astype(BF16)
            dkr = dkr + jnp.where(lane >= QK_NOPE, dk, 0.0)
        dkr_ref[pl.ds(pl.multiple_of(j * t, t), t), :] += dkr

        @pl.when(j == nt - 1)
        def _():
            dq_ref[...] = dq_acc[...].astype(BF16)

    group = lambda g, j: (0, g)
    return pl.pallas_call(
        body, name="attn_bwd", grid=(N_GROUPS, nt),
        in_specs=[_resident((L, gw), group), pl.BlockSpec((t, gw), lambda g, j: (j, g)),
                  pl.BlockSpec((t, HB), lambda g, j: (j, 0)), _resident((L, gw), group),
                  _resident((L, LANES), group), _resident((L, LANES), group)],
        out_specs=[pl.BlockSpec((L, gw), group), pl.BlockSpec((t, gw), lambda g, j: (j, g)),
                   pl.BlockSpec((L, HB), lambda g, j: (0, 0))],
        out_shape=[_sds((L, D_HEADS), BF16), _sds((L, D_HEADS), BF16), _sds((L, HB), F32)],
        scratch_shapes=[pltpu.VMEM((L, gw), F32), pltpu.VMEM((t, gw), F32), pltpu.VMEM((t, gw), F32)],
        compiler_params=_params(2))(q, kv, kr, do, lse, delta)


def _bwd_conv(du1, u0, cw, zag):
    L = du1.shape[0]
    nt = L // TM

    def body(dy_ref, dyn_ref, x_ref, xp_ref, cw_ref, zag_ref, dzag_ref, dcw_ref, dcb_ref):
        i = pl.program_id(0)
        dy = dy_ref[...]
        yy = jnp.concatenate([dy, jnp.where(i < nt - 1, dyn_ref[...], 0.0)], 0)
        du0 = _conv_taps(yy, cw_ref, HALO_CONV, TM, True)
        zag = zag_ref[...].astype(F32)
        a, sg = zag[:, :D_CONV], jax.nn.sigmoid(zag[:, D_CONV:])
        dzag_ref[...] = jnp.concatenate([du0 * sg, du0 * a * sg * (1.0 - sg)], 1).astype(BF16)
        xx = jnp.concatenate([jnp.where(i > 0, xp_ref[...], 0.0), x_ref[...]], 0)

        @pl.when(i == 0)
        def _():
            dcw_ref[...] = jnp.zeros_like(dcw_ref)
            dcb_ref[...] = jnp.zeros_like(dcb_ref)

        for k in range(CONV_WIDTH):
            d = CONV_WIDTH - 1 - k
            xs = xx if d == 0 else pltpu.roll(xx, d, 0)
            dcw_ref[k:k + 1, :] += jnp.sum(dy * xs[HALO_CONV:], 0, keepdims=True)
        dcb_ref[...] += jnp.sum(dy, 0, keepdims=True)

    return pl.pallas_call(
        body, name="bwd_conv", grid=(nt,),
        in_specs=[_rows(TM, D_CONV), _next(HALO_CONV, D_CONV, TM, L // HALO_CONV), _rows(TM, D_CONV),
                  _prev(HALO_CONV, D_CONV, TM), _full(cw.shape), _rows(TM, D_AG)],
        out_specs=[_rows(TM, D_AG), _full(cw.shape), _full((1, D_CONV))],
        out_shape=[_sds((L, D_AG), BF16), _sds(cw.shape, F32), _sds((1, D_CONV), F32)],
        compiler_params=_params(1))(du1, du1, u0, u0, cw, zag)


def _bwd_in(dzag, dq, dkv, dkr, cq, ckv, gq, gkv, wuq, wukv, win, rc, rs1, rs2, h0, gmix, dh1):
    L = h0.shape[0]

    def body(dzag_ref, dq_ref, dkv_ref, dkr_ref, cq_ref, ckv_ref, gq_ref, gkv_ref, wuq_ref, wukv_ref, win_ref,
             c_ref, s1_ref, s2_ref, h0_ref, gmix_ref, dh1_ref,
             dz_ref, dqr_ref, gx_ref, dfirst_ref, dgq_ref, dgkv_ref, dgmix_ref):
        i = pl.program_id(0)
        c, s1, s2 = c_ref[...], s1_ref[...], s2_ref[...]
        dqr = _rope_bwd(dq_ref[...].astype(F32), jnp.tile(c, (1, N_HEADS)), jnp.tile(s1, (1, N_HEADS)),
                        jnp.tile(s2, (1, N_HEADS))).astype(BF16)
        dqr_ref[...] = dqr
        cq, ckv = cq_ref[...], ckv_ref[...]
        dcq, dgq = _rms_bwd(_dot(dqr, wuq_ref[...], NT), cq, _rms_r(cq, Q_LORA), gq_ref[...], Q_LORA)
        dckv, dgkv = _rms_bwd(_dot(dkv_ref[...], wukv_ref[...], NT), ckv, _rms_r(ckv, KV_LORA), gkv_ref[...], KV_LORA)
        dkrp = _rope_bwd(dkr_ref[...], c, s1, s2)
        dz = jnp.concatenate([dzag_ref[...], dcq.astype(BF16), dckv.astype(BF16), dkrp.astype(BF16)], 1)
        dz_ref[...] = dz
        h0 = h0_ref[...]
        dnx, dgmix = _rms_bwd(_dot(dz, win_ref[...], NT), h0, _rms_r(h0, D_MODEL), gmix_ref[...], D_MODEL)
        dh0 = dh1_ref[...] + dnx

        @pl.when(i == 0)
        def _():
            dfirst_ref[...] = dh0
            for ref in (dgq_ref, dgkv_ref, dgmix_ref):
                ref[...] = jnp.zeros_like(ref)

        @pl.when(i > 0)
        def _():
            gx_ref[...] = dh0

        dgq_ref[...] += dgq
        dgkv_ref[...] += dgkv
        dgmix_ref[...] += dgmix

    return pl.pallas_call(
        body, name="bwd_in", grid=(L // TM,),
        in_specs=[_rows(TM, D_AG), _rows(TM, D_HEADS), _rows(TM, D_HEADS), _rows(TM, HB), _rows(TM, Q_LORA),
                  _rows(TM, KV_LORA), _full(gq.shape), _full(gkv.shape), _full(wuq.shape), _full(wukv.shape),
                  _full(win.shape), _rows(TM, HB), _rows(TM, HB), _rows(TM, HB), _rows(TM, D_MODEL),
                  _full(gmix.shape), _rows(TM, D_MODEL)],
        out_specs=[_rows(TM, D_ZP), _rows(TM, D_HEADS),
                   pl.BlockSpec((TM, D_MODEL), lambda i: (jnp.maximum(i - 1, 0), 0)), _full((TM, D_MODEL)),
                   _full((1, Q_LORA)), _full((1, KV_LORA)), _full((1, D_MODEL))],
        out_shape=[_sds((L, D_ZP), BF16), _sds((L, D_HEADS), BF16), _sds((L - TM, D_MODEL), F32),
                   _sds((TM, D_MODEL), F32), _sds((1, Q_LORA), F32), _sds((1, KV_LORA), F32), _sds((1, D_MODEL), F32)],
        compiler_params=_params(1))(dzag, dq, dkv, dkr, cq, ckv, gq, gkv, wuq, wukv, win, rc, rs1, rs2, h0, gmix, dh1)


def _mesh_pos():
    return lax.axis_index("x"), lax.axis_index("y"), lax.axis_index("c")


def _all_gather(shards):
    n = len(shards)
    halves = [s.shape[0] // 2 for s in shards]

    def body(*refs):
        ins, outs = refs[:n], refs[n:2 * n]
        send_sems, recv_sems, local_sems = refs[2 * n:]
        x, y, c = _mesh_pos()
        chips = [(1 - x, y), (x, 1 - y), (1 - x, 1 - y)]
        started = []
        for a in range(n):
            m = halves[a]
            out = outs[a]

            def block(px, py, pc, out=out):
                return out.at[4 * px + 2 * py + pc]

            for hf in range(2):
                cp = pltpu.make_async_copy(ins[a].at[pl.ds(hf * m, m)], block(x, y, hf), local_sems.at[a, hf])
                cp.start()
                started.append(cp.wait)
            mine = ins[a].at[pl.ds(pl.multiple_of(c * m, 16), m)]
            for k, chip in enumerate(chips):
                cp = pltpu.make_async_remote_copy(mine, block(x, y, c), send_sems.at[a, k], recv_sems.at[a, k],
                                                  device_id=(*chip, c), device_id_type=MESH)
                cp.start()
                started.append(cp.wait_send)
        for a in range(n):
            out = outs[a]
            for k, chip in enumerate(chips):
                got = out.at[4 * chip[0] + 2 * chip[1] + c]
                pltpu.make_async_remote_copy(got, got, send_sems.at[a, k], recv_sems.at[a, k],
                                             device_id=(*chip, c), device_id_type=MESH).wait_recv()
                cp = pltpu.make_async_remote_copy(got, got, send_sems.at[a, 3 + k], recv_sems.at[a, 3 + k],
                                                  device_id=(x, y, 1 - c), device_id_type=MESH)
                cp.start()
                started.append(cp.wait_send)
        for a in range(n):
            out = outs[a]
            for k, chip in enumerate(chips):
                got = out.at[4 * chip[0] + 2 * chip[1] + 1 - c]
                pltpu.make_async_remote_copy(got, got, send_sems.at[a, 3 + k], recv_sems.at[a, 3 + k],
                                             device_id=(x, y, 1 - c), device_id_type=MESH).wait_recv()
        for wait in started:
            wait()

    any_spec = pl.BlockSpec(memory_space=pl.ANY)
    outs = pl.pallas_call(
        body, name="all_gather_weights", in_specs=[any_spec] * n, out_specs=[any_spec] * n,
        out_shape=[_sds((8, m) + s.shape[1:], s.dtype) for s, m in zip(shards, halves)],
        scratch_shapes=[pltpu.SemaphoreType.DMA((n, 6)), pltpu.SemaphoreType.DMA((n, 6)),
                        pltpu.SemaphoreType.DMA((n, 2))])(*shards)
    return [o.reshape((N_CHIPS, s.shape[0]) + s.shape[1:]) for o, s in zip(outs, shards)]


def _row_chunks(rows, row_bytes):
    if rows * row_bytes >= BULK_COPY_BYTES:
        for k in BULK_COPY_SPLITS:
            if rows % (k * 16) == 0:
                return [(r * (rows // k), rows // k) for r in range(k)]
    return [(0, rows)]


def _sibling_exchange(parts):
    n = len(parts)
    pieces = [[(q, r0, rn) for q in range(N_CHIPS)
               for r0, rn in _row_chunks(p.shape[2], LANES * p.dtype.itemsize)] for p in parts]
    total = sum(len(pc) for pc in pieces)

    def body(*refs):
        ins, mine, theirs = refs[:n], refs[n:2 * n], refs[2 * n:3 * n]
        send_sems, recv_sems, local_sems = refs[3 * n:]
        x, y, c = _mesh_pos()
        waits = []
        k = 0
        for a in range(n):
            for q, r0, rn in pieces[a]:
                rows = pl.ds(r0, rn)
                lc = pltpu.make_async_copy(ins[a].at[c, q, rows], mine[a].at[q, rows], local_sems.at[k])
                lc.start()
                rc = pltpu.make_async_remote_copy(ins[a].at[1 - c, q, rows], theirs[a].at[q, rows], send_sems.at[k],
                                                  recv_sems.at[k], device_id=(x, y, 1 - c), device_id_type=MESH)
                rc.start()
                waits += [lc.wait, rc.wait]
                k += 1
        for wait in waits:
            wait()

    any_spec = pl.BlockSpec(memory_space=pl.ANY)
    shapes = [_sds(p.shape[1:], p.dtype) for p in parts]
    outs = pl.pallas_call(
        body, name="grad_sibling_exchange", in_specs=[any_spec] * n, out_specs=[any_spec] * (2 * n),
        out_shape=shapes + shapes,
        scratch_shapes=[pltpu.SemaphoreType.DMA((total,)), pltpu.SemaphoreType.DMA((total,)),
                        pltpu.SemaphoreType.DMA((total,))])(*parts)
    return outs[:n], outs[n:]


def _chip_exchange(parts):
    n = len(parts)

    def body(*refs):
        ins, outs = refs[:n], refs[n:2 * n]
        send_sems, recv_sems, local_sems = refs[2 * n:]
        x, y, c = _mesh_pos()
        me = 2 * x + y
        chips = [(1 - x, y), (x, 1 - y), (1 - x, 1 - y)]
        waits = []
        for a in range(n):
            lc = pltpu.make_async_copy(ins[a].at[me], outs[a].at[me], local_sems.at[a])
            lc.start()
            waits.append(lc.wait)
            for k, chip in enumerate(chips):
                rc = pltpu.make_async_remote_copy(ins[a].at[2 * chip[0] + chip[1]], outs[a].at[me],
                                                  send_sems.at[a, k], recv_sems.at[a, k],
                                                  device_id=(*chip, c), device_id_type=MESH)
                rc.start()
                waits.append(rc.wait_send)
        for a in range(n):
            for k, chip in enumerate(chips):
                src = outs[a].at[2 * chip[0] + chip[1]]
                pltpu.make_async_remote_copy(src, src, send_sems.at[a, k], recv_sems.at[a, k],
                                             device_id=(*chip, c), device_id_type=MESH).wait_recv()
        for wait in waits:
            wait()

    any_spec = pl.BlockSpec(memory_space=pl.ANY)
    return pl.pallas_call(
        body, name="grad_chip_exchange", in_specs=[any_spec] * n, out_specs=[any_spec] * n,
        out_shape=[_sds(p.shape, p.dtype) for p in parts],
        scratch_shapes=[pltpu.SemaphoreType.DMA((n, 3)), pltpu.SemaphoreType.DMA((n, 3)),
                        pltpu.SemaphoreType.DMA((n,))])(*parts)


def _sibling_gather(parts):
    n = len(parts)
    pieces = [_row_chunks(p.shape[0], LANES * p.dtype.itemsize) for p in parts]
    total = sum(len(pc) for pc in pieces)

    def body(*refs):
        ins, outs = refs[:n], refs[n:2 * n]
        send_sems, recv_sems, local_sems = refs[2 * n:]
        x, y, c = _mesh_pos()
        waits = []
        k = 0
        for a in range(n):
            for r0, rn in pieces[a]:
                rows = pl.ds(r0, rn)
                lc = pltpu.make_async_copy(ins[a].at[rows], outs[a].at[c, rows], local_sems.at[k])
                lc.start()
                rc = pltpu.make_async_remote_copy(ins[a].at[rows], outs[a].at[c, rows], send_sems.at[k],
                                                  recv_sems.at[k], device_id=(x, y, 1 - c), device_id_type=MESH)
                rc.start()
                waits += [lc.wait, rc.wait_send]
                k += 1
        k = 0
        for a in range(n):
            for r0, rn in pieces[a]:
                got = outs[a].at[1 - c, pl.ds(r0, rn)]
                pltpu.make_async_remote_copy(got, got, send_sems.at[k], recv_sems.at[k],
                                             device_id=(x, y, 1 - c), device_id_type=MESH).wait_recv()
                k += 1
        for wait in waits:
            wait()

    any_spec = pl.BlockSpec(memory_space=pl.ANY)
    return pl.pallas_call(
        body, name="grad_sibling_gather", in_specs=[any_spec] * n, out_specs=[any_spec] * n,
        out_shape=[_sds((2,) + p.shape, p.dtype) for p in parts],
        scratch_shapes=[pltpu.SemaphoreType.DMA((total,)), pltpu.SemaphoreType.DMA((total,)),
                        pltpu.SemaphoreType.DMA((total,))])(*parts)


def _row_tile(rows, row_bytes, align, budget=1 << 20):
    best = None
    for t in range(align, rows + 1, align):
        if rows % t == 0 and t * row_bytes <= budget:
            best = t
    return best or rows


def _add_pair(a, b, out_dtype):
    shape = a.shape
    a2, b2 = a.reshape(-1, LANES), b.reshape(-1, LANES)
    tr = _row_tile(a2.shape[0], LANES * 4, 16)

    def body(a_ref, b_ref, o_ref):
        o_ref[...] = (a_ref[...].astype(F32) + b_ref[...].astype(F32)).astype(out_dtype)

    out = pl.pallas_call(
        body, name="grad_add_pair", grid=(a2.shape[0] // tr,), in_specs=[_rows(tr, LANES)] * 2,
        out_specs=_rows(tr, LANES), out_shape=_sds(a2.shape, out_dtype), compiler_params=_params(1))(a2, b2)
    return out.reshape(shape)


def _add_chips(p):
    nh = p.shape[1]
    tr = _row_tile(nh, LANES * 4 * N_CHIPS, 16)

    def body(p_ref, o_ref):
        acc = p_ref[0].astype(F32)
        for q in range(1, N_CHIPS):
            acc = acc + p_ref[q].astype(F32)
        o_ref[...] = acc

    return pl.pallas_call(
        body, name="grad_add_chips", grid=(nh // tr,),
        in_specs=[pl.BlockSpec((N_CHIPS, tr, LANES), lambda i: (0, i, 0))], out_specs=_rows(tr, LANES),
        out_shape=_sds((nh, LANES), F32), compiler_params=_params(1))(p)


def _reduce_scatter(packs):
    mine, theirs = _sibling_exchange(packs)
    pair = [_add_pair(m, t, m.dtype) for m, t in zip(mine, theirs)]
    by_src = _chip_exchange(pair)
    half = [_add_chips(p) for p in by_src]
    both = _sibling_gather(half)
    return [b.reshape(-1, LANES) for b in both]


def _adamw_math(w, g, m, v):
    m = ADAM_B1 * m + (1.0 - ADAM_B1) * g
    v = ADAM_B2 * v + (1.0 - ADAM_B2) * (g * g)
    m_hat = m / (1.0 - ADAM_B1 ** ADAM_STEP)
    v_hat = v / (1.0 - ADAM_B2 ** ADAM_STEP)
    return -ADAM_LR * (m_hat / (jnp.sqrt(v_hat) + ADAM_EPS) + ADAM_WD * w), m, v


def _adamw_big(w, g, m, v, name):
    r, c = w.shape
    tr = _row_tile(r, c * 4, 8, budget=1 << 19)

    def body(w_ref, g_ref, m_ref, v_ref, d_ref, mo_ref, vo_ref):
        d_ref[...], mo_ref[...], vo_ref[...] = _adamw_math(w_ref[...], g_ref[...], m_ref[...], v_ref[...])

    return pl.pallas_call(
        body, name=name, grid=(r // tr,), in_specs=[_rows(tr, c)] * 4, out_specs=[_rows(tr, c)] * 3,
        out_shape=[_sds((r, c), F32)] * 3, compiler_params=_params(1))(w, g, m, v)


def _adamw_small(ws, gs, ms, vs):
    n = len(ws)

    def body(*refs):
        for a in range(n):
            w_ref, g_ref, m_ref, v_ref = (refs[k * n + a] for k in range(4))
            d, m, v = _adamw_math(w_ref[...], g_ref[...], m_ref[...], v_ref[...])
            refs[4 * n + a][...] = d
            refs[5 * n + a][...] = m
            refs[6 * n + a][...] = v

    vm = pl.BlockSpec(memory_space=pltpu.VMEM)
    outs = pl.pallas_call(
        body, name="adamw_small", in_specs=[vm] * (4 * n), out_specs=[vm] * (3 * n),
        out_shape=[_sds(w.shape, F32) for w in ws] * 3)(*ws, *gs, *ms, *vs)
    return outs[:n], outs[n:2 * n], outs[2 * n:]


BIG = ("w_in", "w_uq", "w_ukv", "w_out", "w_ffn_up", "w_ffn_down")
SMALL_SHARDED = ("conv_w", "ffn_conv_w", "meta_tokens")
REPLICATED = ("mix_norm_g", "q_norm_g", "kv_norm_g", "conv_b", "conv_ln_g", "conv_ln_b", "conv_out_g", "attn_out_g",
              "ffn_norm_g", "ffn_conv_b", "final_norm_g")
WEIGHTS = ("meta_tokens", "mix_norm_g", "w_in", "q_norm_g", "w_uq", "kv_norm_g", "w_ukv", "conv_w", "conv_b",
           "conv_ln_g", "conv_ln_b", "conv_out_g", "attn_out_g", "w_out", "ffn_norm_g", "w_ffn_up", "ffn_conv_w",
           "ffn_conv_b", "w_ffn_down", "final_norm_g")


def _lane_rows(a):
    return a.reshape(N_CHIPS, -1, LANES)


def _col_shards(a):
    k = a.shape[0]
    return a.reshape(k, N_CHIPS, -1).transpose(1, 0, 2)


def _from_col_shards(a):
    return a.transpose(1, 0, 2).reshape(a.shape[1], -1)


def _pad_rows_to(a, rows):
    return jnp.pad(a, ((0, 0), (0, rows - a.shape[1]), (0, 0)))


def _rope_tables(L):
    pos = (jnp.arange(L, dtype=jnp.int32) - DEAD).astype(F32)
    inv_freq = 1.0 / (ROPE_THETA ** (jnp.arange(0, QK_ROPE, 2, dtype=F32) / QK_ROPE))
    ang = pos[:, None] * inv_freq[None, :]
    cos, sin = jnp.cos(ang), jnp.sin(ang)
    half = QK_ROPE // 2
    z = lambda n: jnp.zeros((L, n), F32)
    rc = jnp.concatenate([jnp.ones((L, QK_NOPE), F32), cos, cos, z(HB - QK_NOPE - QK_ROPE)], 1)
    rs1 = jnp.concatenate([z(QK_NOPE), -sin, z(HB - QK_NOPE - half)], 1)
    rs2 = jnp.concatenate([z(QK_NOPE + half), sin, z(HB - QK_NOPE - QK_ROPE)], 1)
    return rc, rs1, rs2


def _pad_heads(g):
    return jnp.pad(g.reshape(N_HEADS, V_HEAD), ((0, 0), (HB - V_HEAD, 0))).reshape(1, D_HEADS)


def _unpad_heads(g):
    return g.reshape(N_HEADS, HB)[:, HB - V_HEAD:].reshape(1, N_HEADS * V_HEAD)


def _local_step(x, target, w):
    S = x.shape[0]
    L = TM + S
    tl = L // 4
    d_qk = QK_NOPE + QK_ROPE
    win_n = w["w_in"]
    kr0 = D_AG + Q_LORA + KV_LORA
    win = jnp.concatenate([win_n[:, :kr0], jnp.zeros((D_MODEL, QK_NOPE), BF16), win_n[:, kr0:],
                           jnp.zeros((D_MODEL, HB - d_qk), BF16)], 1)
    wuq = jnp.pad(w["w_uq"].reshape(Q_LORA, N_HEADS, d_qk), ((0, 0), (0, 0), (0, HB - d_qk))).reshape(Q_LORA, D_HEADS)
    wukv = w["w_ukv"]
    wout_n = w["w_out"]
    wout = jnp.concatenate([wout_n[:D_CONV], jnp.pad(wout_n[D_CONV:].reshape(N_HEADS, V_HEAD, D_MODEL),
                                                     ((0, 0), (HB - V_HEAD, 0), (0, 0))).reshape(D_HEADS, D_MODEL)], 0)
    wup, wdown = w["w_ffn_up"], w["w_ffn_down"]
    ga = _pad_heads(w["attn_out_g"])
    gfin = w["final_norm_g"].reshape(1, D_MODEL)
    rc, rs1, rs2 = _rope_tables(L)
    h0 = jnp.concatenate([jnp.zeros((DEAD, D_MODEL), F32), w["meta_tokens"], x], 0)

    n, zag, u0, cq, ckv, qn, kvn, q, kv, kr = _fwd_in(h0, w["mix_norm_g"], win, w["q_norm_g"], wuq, w["kv_norm_g"],
                                                       wukv, rc, rs1, rs2)
    u1, mixa = _fwd_conv(u0, w["conv_w"], w["conv_b"], w["conv_ln_g"], w["conv_ln_b"], w["conv_out_g"])
    o, lse = _attn_fwd(q, kv, kr)
    mix, h1, n2 = _fwd_mix(h0, mixa, o, ga, wout, w["ffn_norm_g"])
    up0 = _mm(n2, wup, NN, BF16, tl, FF_TILE, D_MODEL, "ffn_up")
    dh2, loss, g_fin = _fwd_ffn_loss(up0, w["ffn_conv_w"], w["ffn_conv_b"], wdown, h1, target, gfin)

    dup, act, g_fcb = _bwd_ffn_act(dh2, up0, w["ffn_conv_w"], w["ffn_conv_b"], wdown)
    dup0, g_fcw = _bwd_ffn_conv(dup, up0, w["ffn_conv_w"])
    dn2 = _mm(dup0, wup, NT, F32, tl, D_MODEL, FF_TILE, "ffn_up_dx")
    g_wup = _mm(n2, dup0, TN, F32, D_MODEL, FF_TILE, tl, "ffn_up_dw")
    g_wdown = _mm(act, dh2, TN, F32, D_FF // 2, D_MODEL, tl, "ffn_down_dw")
    dh1, dh1m, do, delta, du1, g_gffn, g_ga, g_og, g_lg, g_lb = _bwd_mix(
        dh2, dn2, h1, w["ffn_norm_g"], wout, o, ga, u1, w["conv_ln_g"], w["conv_ln_b"], w["conv_out_g"])
    g_wout = _mm(mix, dh1m, TN, F32, D_MIX // 2, D_MODEL, tl, "out_dw")
    dq, dkv, dkr = _attn_bwd(q, kv, kr, do, lse, delta)
    dzag, g_cw, g_cb = _bwd_conv(du1, u0, w["conv_w"], zag)
    dz, dqr, gx, dfirst, g_gq, g_gkv, g_gmix = _bwd_in(dzag, dq, dkv, dkr, cq, ckv, w["q_norm_g"], w["kv_norm_g"],
                                                      wuq, wukv, win, rc, rs1, rs2, h0, w["mix_norm_g"], dh1)
    g_win = _mm(n, dz, TN, F32, D_MODEL, D_ZP // 2, tl, "in_dw")
    g_wuq = _mm(qn, dqr, TN, F32, Q_LORA, D_HEADS, tl, "uq_dw")
    g_wukv = _mm(kvn, dkv, TN, F32, KV_LORA, D_HEADS, tl, "ukv_dw")

    grads = {
        "w_in": jnp.concatenate([g_win[:, :kr0], g_win[:, kr0 + QK_NOPE:kr0 + d_qk]], 1),
        "w_uq": g_wuq.reshape(Q_LORA, N_HEADS, HB)[:, :, :d_qk].reshape(Q_LORA, N_HEADS * d_qk),
        "w_ukv": g_wukv,
        "w_out": jnp.concatenate([g_wout[:D_CONV], g_wout[D_CONV:].reshape(N_HEADS, HB, D_MODEL)[:, HB - V_HEAD:]
                                  .reshape(N_HEADS * V_HEAD, D_MODEL)], 0),
        "w_ffn_up": g_wup, "w_ffn_down": g_wdown, "conv_w": g_cw, "ffn_conv_w": g_fcw,
        "meta_tokens": dfirst[DEAD:], "mix_norm_g": g_gmix, "q_norm_g": g_gq, "kv_norm_g": g_gkv, "conv_b": g_cb,
        "conv_ln_g": g_lg, "conv_ln_b": g_lb, "conv_out_g": g_og, "attn_out_g": _unpad_heads(g_ga),
        "ffn_norm_g": g_gffn, "ffn_conv_b": g_fcb, "final_norm_g": g_fin,
    }
    return loss, gx, grads


ROW_SHARDED = ("w_out", "w_ffn_down")


def _owner_pieces(name, g):
    if name in ROW_SHARDED:
        return _lane_rows(g.reshape(N_CHIPS, -1))
    return _lane_rows(_col_shards(g))


def _split_halves(p):
    return p.reshape(N_CHIPS, 2, p.shape[1] // 2, LANES).transpose(1, 0, 2, 3)


def kernel(x, meta_tokens, mix_norm_g, w_in, q_norm_g, w_uq, kv_norm_g, w_ukv, conv_w, conv_b, conv_ln_g, conv_ln_b, conv_out_g, attn_out_g, w_out, ffn_norm_g, w_ffn_up, ffn_conv_w, ffn_conv_b, w_ffn_down, final_norm_g, loss_target, m_meta_tokens, m_mix_norm_g, m_w_in, m_q_norm_g, m_w_uq, m_kv_norm_g, m_w_ukv, m_conv_w, m_conv_b, m_conv_ln_g, m_conv_ln_b, m_conv_out_g, m_attn_out_g, m_w_out, m_ffn_norm_g, m_w_ffn_up, m_ffn_conv_w, m_ffn_conv_b, m_w_ffn_down, m_final_norm_g, v_meta_tokens, v_mix_norm_g, v_w_in, v_q_norm_g, v_w_uq, v_kv_norm_g, v_w_ukv, v_conv_w, v_conv_b, v_conv_ln_g, v_conv_ln_b, v_conv_out_g, v_attn_out_g, v_w_out, v_ffn_norm_g, v_w_ffn_up, v_ffn_conv_w, v_ffn_conv_b, v_w_ffn_down, v_final_norm_g):
    local = dict(meta_tokens=meta_tokens, mix_norm_g=mix_norm_g, w_in=w_in[0], q_norm_g=q_norm_g, w_uq=w_uq[0],
                 kv_norm_g=kv_norm_g, w_ukv=w_ukv[0], conv_w=conv_w[0], conv_b=conv_b, conv_ln_g=conv_ln_g,
                 conv_ln_b=conv_ln_b, conv_out_g=conv_out_g, attn_out_g=attn_out_g, w_out=w_out[0],
                 ffn_norm_g=ffn_norm_g, w_ffn_up=w_ffn_up[0], ffn_conv_w=ffn_conv_w[0], ffn_conv_b=ffn_conv_b,
                 w_ffn_down=w_ffn_down[0], final_norm_g=final_norm_g.reshape(1, D_MODEL))
    ms = dict(zip(WEIGHTS, (m_meta_tokens, m_mix_norm_g, m_w_in, m_q_norm_g, m_w_uq, m_kv_norm_g, m_w_ukv, m_conv_w,
                            m_conv_b, m_conv_ln_g, m_conv_ln_b, m_conv_out_g, m_attn_out_g, m_w_out, m_ffn_norm_g,
                            m_w_ffn_up, m_ffn_conv_w, m_ffn_conv_b, m_w_ffn_down, m_final_norm_g)))
    vs = dict(zip(WEIGHTS, (v_meta_tokens, v_mix_norm_g, v_w_in, v_q_norm_g, v_w_uq, v_kv_norm_g, v_w_ukv, v_conv_w,
                            v_conv_b, v_conv_ln_g, v_conv_ln_b, v_conv_out_g, v_attn_out_g, v_w_out, v_ffn_norm_g,
                            v_w_ffn_up, v_ffn_conv_w, v_ffn_conv_b, v_w_ffn_down, v_final_norm_g)))

    small_flat = jnp.concatenate([local[k].reshape(-1) for k in SMALL_SHARDED]).reshape(-1, LANES)
    gathered = _all_gather([local[k].astype(BF16) for k in BIG] + [small_flat])
    full = dict(local)
    for name, g in zip(BIG, gathered[:len(BIG)]):
        full[name] = g.reshape(-1, g.shape[-1]) if name in ROW_SHARDED else _from_col_shards(g)
    small = gathered[-1].reshape(N_CHIPS, -1)
    at = 0
    for name in SMALL_SHARDED:
        r, c = local[name].shape
        full[name] = _from_col_shards(small[:, at:at + r * c].reshape(N_CHIPS, r, c))
        at += r * c

    loss_row, grad_x, grads = _local_step(x[0], loss_target[0], full)

    big_pack = _split_halves(jnp.concatenate([_owner_pieces(k, grads[k]) for k in BIG], 1).astype(BF16))
    rep = jnp.concatenate([grads[k].reshape(-1) for k in REPLICATED] + [loss_row.reshape(-1)]).reshape(1, -1, LANES)
    small_pieces = [_owner_pieces(k, grads[k]) for k in SMALL_SHARDED] + [jnp.broadcast_to(rep, (N_CHIPS,) + rep.shape[1:])]
    small_rows = sum(p.shape[1] for p in small_pieces)
    small_pack = _split_halves(_pad_rows_to(jnp.concatenate(small_pieces, 1), -(-small_rows // 16) * 16))
    big_tot, small_tot = _reduce_scatter([big_pack, small_pack])

    total = {}
    at = 0
    for name in BIG:
        r, c = local[name].shape
        total[name] = big_tot[at:at + r * c // LANES].reshape(r, c)
        at += r * c // LANES
    flat = small_tot.reshape(-1)
    at = 0
    for name in SMALL_SHARDED + REPLICATED:
        shape = local[name].shape
        size = shape[0] * shape[1]
        total[name] = flat[at:at + size].reshape(shape)
        at += -(-size // LANES) * LANES if name in SMALL_SHARDED else size
    loss = flat[at]

    delta, new_m, new_v = {}, {}, {}
    shape2 = lambda a, name: a.reshape(local[name].shape)
    for name in BIG:
        delta[name], new_m[name], new_v[name] = _adamw_big(local[name], total[name], shape2(ms[name], name),
                                                          shape2(vs[name], name), "adamw_" + name)
    rest = SMALL_SHARDED + REPLICATED
    ds, nms, nvs = _adamw_small([local[k] for k in rest], [total[k] for k in rest],
                                [shape2(ms[k], k) for k in rest], [shape2(vs[k], k) for k in rest])
    for k, d, nm, nv in zip(rest, ds, nms, nvs):
        delta[k], new_m[k], new_v[k] = d, nm, nv

    out_shape = dict(zip(WEIGHTS, (meta_tokens, mix_norm_g, w_in, q_norm_g, w_uq, kv_norm_g, w_ukv, conv_w, conv_b,
                                   conv_ln_g, conv_ln_b, conv_out_g, attn_out_g, w_out, ffn_norm_g, w_ffn_up,
                                   ffn_conv_w, ffn_conv_b, w_ffn_down, final_norm_g)))
    outs = [loss, grad_x[None]]
    for group in (total, delta, new_m, new_v):
        outs += [group[k].reshape(out_shape[k].shape) for k in WEIGHTS]
    return tuple(outs)
```

```python
import functools

import jax
import jax.numpy as jnp
from jax import lax
from jax.experimental import pallas as pl
from jax.experimental.pallas import tpu as pltpu

F32 = jnp.float32
BF16 = jnp.bfloat16

D_MODEL = 1024
D_CONV = 512
CONV_WIDTH = 31
N_HEADS = 8
QK_NOPE = 64
QK_ROPE = 32
V_HEAD = 64
Q_LORA = 384
KV_LORA = 256
D_FF = 2816
FFN_CONV_WIDTH = 3
CHUNK_SHIFT = 6
N_META = 16
ROPE_THETA = 10000.0
EPS = 1e-6
NEG = -1e30
ADAM_LR = 0.001
ADAM_B1 = 0.9
ADAM_B2 = 0.999
ADAM_EPS = 1e-08
ADAM_WD = 0.01
ADAM_STEP = 10

LANES = 128
HB = LANES
D_HEADS = N_HEADS * HB
TM = 256
DEAD = TM - N_META
D_AG = 2 * D_CONV
D_ZP = D_AG + Q_LORA + KV_LORA + HB
D_MIX = D_CONV + D_HEADS
SCALE = (QK_NOPE + QK_ROPE) ** -0.5
HALO_CONV = 32
HALO_FFN = 16
FF_CHUNK = 256
FF_TILE = D_FF // 2
VMEM_LIMIT = 56 * 1024 * 1024
N_CHIPS = 4
HEAD_GROUP = 4
N_GROUPS = N_HEADS // HEAD_GROUP
MESH =pl.DeviceIdType.MESH


def _params(n_grid):
    return pltpu.CompilerParams(dimension_semantics=("arbitrary",) * n_grid, vmem_limit_bytes=VMEM_LIMIT)


def _rows(tm, c, off=0):
    return pl.BlockSpec((tm, c), lambda i: (i, off))


def _full(shape):
    return pl.BlockSpec(shape, lambda i: (0,) * len(shape))


def _prev(hb, c, tm, off=0):
    return pl.BlockSpec((hb, c), lambda i: (jnp.maximum(i * (tm // hb) - 1, 0), off))


def _next(hb, c, tm, nblk, off=0):
    return pl.BlockSpec((hb, c), lambda i: (jnp.minimum((i + 1) * (tm // hb), nblk - 1), off))


def _sds(shape, dtype):
    return jax.ShapeDtypeStruct(shape, dtype)


def _rms_r(x, n):
    return lax.rsqrt(jnp.sum(x * x, -1, keepdims=True) * (1.0 / n) + EPS)


def _rms_bwd(dy, x, r, g, n):
    gd = dy * g
    dx = r * gd - x * (r * r * r) * (jnp.sum(x * gd, -1, keepdims=True) * (1.0 / n))
    return dx, jnp.sum(dy * x * r, 0, keepdims=True)


def _dot(a, b, dims):
    return lax.dot_general(a, b, (dims, ((), ())), preferred_element_type=F32)


NN = ((1,), (0,))
NT = ((1,), (1,))
TN = ((0,), (0,))


def _rope(x, c, s1, s2):
    n = x.shape[-1]
    return x * c + pltpu.roll(x, n - QK_ROPE // 2, 1) * s1 + pltpu.roll(x, QK_ROPE // 2, 1) * s2


def _rope_bwd(g, c, s1, s2):
    n = g.shape[-1]
    return g * c + pltpu.roll(g * s1, QK_ROPE // 2, 1) + pltpu.roll(g * s2, n - QK_ROPE // 2, 1)


def _row_ids(tm, cols=1):
    return pl.program_id(0) * tm + lax.broadcasted_iota(jnp.int32, (tm, cols), 0)


def _mm(a, b, dims, out_dtype, tm, tn, tk, name):
    if dims == TN:
        (kk, m), (_, n) = a.shape, b.shape
        a_spec = pl.BlockSpec((tk, tm), lambda i, j, k: (k, i))
    else:
        m, kk = a.shape
        a_spec = pl.BlockSpec((tm, tk), lambda i, j, k: (i, k))
    if dims == NT:
        n = b.shape[0]
        b_spec = pl.BlockSpec((tn, tk), lambda i, j, k: (j, k))
    else:
        n = b.shape[1]
        b_spec = pl.BlockSpec((tk, tn), lambda i, j, k: (k, j))
    assert m % tm == 0 and n % tn == 0 and kk % tk == 0, (name, a.shape, b.shape, tm, tn, tk)
    nk = kk // tk

    def body(a_ref, b_ref, o_ref, acc_ref):
        k = pl.program_id(2)

        @pl.when(k == 0)
        def _():
            acc_ref[...] = jnp.zeros_like(acc_ref)

        acc_ref[...] += _dot(a_ref[...].astype(BF16), b_ref[...].astype(BF16), dims)

        @pl.when(k == nk - 1)
        def _():
            o_ref[...] = acc_ref[...].astype(out_dtype)

    return pl.pallas_call(
        body, name=name, grid=(m // tm, n // tn, nk), in_specs=[a_spec, b_spec],
        out_specs=pl.BlockSpec((tm, tn), lambda i, j, k: (i, j)), out_shape=_sds((m, n), out_dtype),
        scratch_shapes=[pltpu.VMEM((tm, tn), F32)], compiler_params=_params(3))(a, b)


def _fwd_in(h0, gmix, win, gq, wuq, gkv, wukv, rc, rs1, rs2):
    L = h0.shape[0]

    def body(h_ref, gmix_ref, win_ref, gq_ref, wuq_ref, gkv_ref, wukv_ref, c_ref, s1_ref, s2_ref,
             n_ref, zag_ref, u0_ref, cq_ref, ckv_ref, qn_ref, kvn_ref, q_ref, kv_ref, kr_ref):
        h = h_ref[...]
        n = (h * _rms_r(h, D_MODEL) * gmix_ref[...]).astype(BF16)
        n_ref[...] = n
        z = _dot(n, win_ref[...], NN)
        a, gate = z[:, :D_CONV], z[:, D_CONV:D_AG]
        zag_ref[...] = z[:, :D_AG].astype(BF16)
        u0_ref[...] = a * jax.nn.sigmoid(gate)
        cq = z[:, D_AG:D_AG + Q_LORA]
        ckv = z[:, D_AG + Q_LORA:D_AG + Q_LORA + KV_LORA]
        krp = z[:, D_AG + Q_LORA + KV_LORA:]
        cq_ref[...] = cq
        ckv_ref[...] = ckv
        qn = (cq * _rms_r(cq, Q_LORA) * gq_ref[...]).astype(BF16)
        qn_ref[...] = qn
        kvn = (ckv * _rms_r(ckv, KV_LORA) * gkv_ref[...]).astype(BF16)
        kvn_ref[...] = kvn
        c, s1, s2 = c_ref[...], s1_ref[...], s2_ref[...]
        q = _dot(qn, wuq_ref[...], NN)
        q = _rope(q, jnp.tile(c, (1, N_HEADS)), jnp.tile(s1, (1, N_HEADS)), jnp.tile(s2, (1, N_HEADS)))
        q_ref[...] = q.astype(BF16)
        kv_ref[...] = _dot(kvn, wukv_ref[...], NN).astype(BF16)
        kr_ref[...] = _rope(krp, c, s1, s2).astype(BF16)

    outs = [(D_MODEL, BF16), (D_AG, BF16), (D_CONV, F32), (Q_LORA, F32), (KV_LORA, F32), (Q_LORA, BF16),
            (KV_LORA, BF16), (D_HEADS, BF16), (D_HEADS, BF16), (HB, BF16)]
    return pl.pallas_call(
        body, name="fwd_in", grid=(L // TM,),
        in_specs=[_rows(TM, D_MODEL), _full(gmix.shape), _full(win.shape), _full(gq.shape), _full(wuq.shape),
                  _full(gkv.shape), _full(wukv.shape), _rows(TM, HB), _rows(TM, HB), _rows(TM, HB)],
        out_specs=[_rows(TM, c) for c, _ in outs], out_shape=[_sds((L, c), d) for c, d in outs],
        compiler_params=_params(1))(h0, gmix, win, gq, wuq, gkv, wukv, rc, rs1, rs2)


def _conv_taps(xx, w_ref, halo, tm, flip):
    kw = w_ref.shape[0]
    rows = xx.shape[0]
    acc = None
    for k in range(kw):
        d = kw - 1 - k
        shifted = xx if d == 0 else pltpu.roll(xx, (rows - d) if flip else d, 0)
        term = w_ref[k:k + 1, :] * (shifted[:tm] if flip else shifted[halo:])
        acc = term if acc is None else acc + term
    return acc


def _ln_silu(u1, lg, lb):
    mu = jnp.mean(u1, -1, keepdims=True)
    xc = u1 - mu
    rs = lax.rsqrt(jnp.mean(xc * xc, -1, keepdims=True) + EPS)
    xh = xc * rs
    u2 = xh * lg + lb
    sg = jax.nn.sigmoid(u2)
    return rs, xh, u2, sg, u2 * sg


def _fwd_conv(u0, cw, cb, lg, lb, og):
    L = u0.shape[0]

    def body(u0_ref, u0p_ref, cw_ref, cb_ref, lg_ref, lb_ref, og_ref, u1_ref, mixa_ref):
        halo = jnp.where(pl.program_id(0) > 0, u0p_ref[...], 0.0)
        xx = jnp.concatenate([halo, u0_ref[...]], 0)
        u1 = _conv_taps(xx, cw_ref, HALO_CONV, TM, False) + cb_ref[...]
        u1_ref[...] = u1
        u = _ln_silu(u1, lg_ref[...], lb_ref[...])[4]
        mixa_ref[...] = (u * _rms_r(u, D_CONV) * og_ref[...]).astype(BF16)

    return pl.pallas_call(
        body, name="fwd_conv", grid=(L // TM,),
        in_specs=[_rows(TM, D_CONV), _prev(HALO_CONV, D_CONV, TM), _full(cw.shape), _full(cb.shape),
                  _full(lg.shape), _full(lb.shape), _full(og.shape)],
        out_specs=[_rows(TM, D_CONV), _rows(TM, D_CONV)],
        out_shape=[_sds((L, D_CONV), F32), _sds((L, D_CONV), BF16)],
        compiler_params=_params(1))(u0, u0, cw, cb, lg, lb, og)


def _visible(i, j, t):
    row = i * t + lax.broadcasted_iota(jnp.int32, (t, t), 0)
    col = j * t + lax.broadcasted_iota(jnp.int32, (t, t), 1)
    return (lax.shift_right_logical(col, CHUNK_SHIFT) <= lax.shift_right_logical(row, CHUNK_SHIFT)) & (col >= DEAD)


def _stat_lane(h):
    return (h // HEAD_GROUP) * LANES + h % HEAD_GROUP


def _scatter_stats(cols, t):
    lane = lax.broadcasted_iota(jnp.int32, (t, N_GROUPS * LANES), 1)
    out = jnp.zeros((t, N_GROUPS * LANES), F32)
    for h, col in enumerate(cols):
        out = jnp.where(lane == _stat_lane(h), col, out)
    return out


def _resident(shape, index_map):
    return pl.BlockSpec(shape, index_map, pipeline_mode=pl.Buffered(1))


def _attn_fwd(q, kv, kr):
    L = q.shape[0]
    t = TM
    nq = L // t
    rep = t // HB

    def body(q_ref, kv_ref, kr_ref, o_ref, lse_ref, m_scr, l_scr, acc_scr):
        i = pl.program_id(0)
        lane = lax.broadcasted_iota(jnp.int32, (t, HB), 1)
        m_scr[...] = jnp.full_like(m_scr, NEG)
        l_scr[...] = jnp.zeros_like(l_scr)
        acc_scr[...] = jnp.zeros_like(acc_scr)

        def tile(j, vis):
            rows = pl.ds(pl.multiple_of(j * t, t), t)
            krj = kr_ref[rows, :]
            heads = range(N_HEADS)
            cols = [slice(h * HB, (h + 1) * HB) for h in heads]
            kvj = [kv_ref[rows, hc] for hc in cols]
            s = [_dot(q_ref[:, cols[h]], jnp.where(lane < QK_NOPE, kvj[h], krj), NT) for h in heads]
            p, alpha = [], []
            for h in heads:
                sh = s[h] * SCALE
                if vis is not None:
                    sh = jnp.where(vis, sh, NEG)
                m_prev = m_scr[h]
                m_new = jnp.maximum(m_prev, jnp.max(sh, -1, keepdims=True))
                a = jnp.exp(m_prev - m_new)
                ph = jnp.exp(sh - jnp.concatenate([m_new] * rep, 1))
                l_scr[h] = a * l_scr[h] + jnp.sum(ph, -1, keepdims=True)
                m_scr[h] = m_new
                p.append(ph.astype(BF16))
                alpha.append(a)
            for h in heads:
                acc_scr[:, cols[h]] = alpha[h] * acc_scr[:, cols[h]] + _dot(p[h], kvj[h], NN)

        tile(i, _visible(i, i, t))

        @pl.when(i > 0)
        def _():
            tile(0, _visible(i, 0, t))

        def unmasked(j, carry):
            tile(j, None)
            return carry

        lax.fori_loop(1, i, unmasked, 0)
        lses = []
        for h in range(N_HEADS):
            hc = slice(h * HB, (h + 1) * HB)
            l = l_scr[h]
            o_ref[:, hc] = jnp.where(lane >= QK_NOPE, acc_scr[:, hc] / l, 0.0).astype(BF16)
            lses.append((m_scr[h] + jnp.log(l))[:, :1])
        lse_ref[...] = _scatter_stats(lses, t)

    return pl.pallas_call(
        body, name="attn_fwd", grid=(nq,),
        in_specs=[_rows(t, D_HEADS), _resident((L, D_HEADS), lambda i: (0, 0)), _resident((L, HB), lambda i: (0, 0))],
        out_specs=[_rows(t, D_HEADS), _rows(t, N_GROUPS * LANES)],
        out_shape=[_sds((L, D_HEADS), BF16), _sds((L, N_GROUPS * LANES), F32)],
        scratch_shapes=[pltpu.VMEM((N_HEADS, t, HB), F32), pltpu.VMEM((N_HEADS, t, HB), F32),
                        pltpu.VMEM((t, D_HEADS), F32)],
        compiler_params=_params(1))(q, kv, kr)


def _fwd_mix(h0, mixa, o, ga, wout, gffn):
    L = h0.shape[0]

    def body(h0_ref, mixa_ref, o_ref, ga_ref, wout_ref, gffn_ref, mix_ref, h1_ref, n2_ref):
        of = o_ref[...].astype(F32)
        mixb = (of * _rms_r(of, N_HEADS * V_HEAD) * ga_ref[...]).astype(BF16)
        mix = jnp.concatenate([mixa_ref[...], mixb], 1)
        mix_ref[...] = mix
        mo = jnp.where(_row_ids(TM) >= DEAD, _dot(mix, wout_ref[...], NN), 0.0)
        h1 = h0_ref[...] + mo
        h1_ref[...] = h1
        n2_ref[...] = (h1 * _rms_r(h1, D_MODEL) * gffn_ref[...]).astype(BF16)

    return pl.pallas_call(
        body, name="fwd_mix", grid=(L // TM,),
        in_specs=[_rows(TM, D_MODEL), _rows(TM, D_CONV), _rows(TM, D_HEADS), _full(ga.shape), _full(wout.shape),
                  _full(gffn.shape)],
        out_specs=[_rows(TM, D_MIX), _rows(TM, D_MODEL), _rows(TM, D_MODEL)],
        out_shape=[_sds((L, D_MIX), BF16), _sds((L, D_MODEL), F32), _sds((L, D_MODEL), BF16)],
        compiler_params=_params(1))(h0, mixa, o, ga, wout, gffn)


def _ffn_act_chunk(c, upg_ref, upv_ref, hg, hv, fcw_ref, fcb_ref):
    cs = slice(c * FF_CHUNK, (c + 1) * FF_CHUNK)
    out = []
    for part, (up_ref, halo) in enumerate(((upg_ref, hg), (upv_ref, hv))):
        xx = jnp.concatenate([halo[:, cs], up_ref[:, cs].astype(F32)], 0)
        ws = slice(part * D_FF + c * FF_CHUNK, part * D_FF + (c + 1) * FF_CHUNK)
        y = (fcw_ref[0:1, ws] * pltpu.roll(xx, 2, 0)[HALO_FFN:] + fcw_ref[1:2, ws] * pltpu.roll(xx, 1, 0)[HALO_FFN:]
             + fcw_ref[2:3, ws] * xx[HALO_FFN:] + fcb_ref[:, ws])
        out.append(y)
    return out


def _ffn_in_specs(L):
    return [_rows(TM, D_FF, 0), _rows(TM, D_FF, 1), _prev(HALO_FFN, D_FF, TM, 0), _prev(HALO_FFN, D_FF, TM, 1)]


def _ffn_halos(hg_ref, hv_ref):
    first = pl.program_id(0) == 0
    return (jnp.where(first, 0.0, hg_ref[...].astype(F32)), jnp.where(first, 0.0, hv_ref[...].astype(F32)))


def _fwd_ffn_loss(up0, fcw, fcb, wdown, h1, target, gfin):
    L = h1.shape[0]

    def body(upg_ref, upv_ref, hg_ref, hv_ref, fcw_ref, fcb_ref, wd_ref, h1_ref, t_ref, gf_ref,
             dh2_ref, loss_ref, dgf_ref, act_ref):
        i = pl.program_id(0)
        hg, hv = _ffn_halos(hg_ref, hv_ref)
        for c in range(D_FF // FF_CHUNK):
            g, val = _ffn_act_chunk(c, upg_ref, upv_ref, hg, hv, fcw_ref, fcb_ref)
            act_ref[:, c * FF_CHUNK:(c + 1) * FF_CHUNK] = (g * jax.nn.sigmoid(g) * val).astype(BF16)
        h2 = h1_ref[...] + _dot(act_ref[...], wd_ref[...], NN)
        r = _rms_r(h2, D_MODEL)
        gf = gf_ref[...]
        err = jnp.where(i > 0, h2 * r * gf - t_ref[...], 0.0)
        dy = err * (1.0 / D_MODEL)
        dh2, dgf = _rms_bwd(dy, h2, r, gf, D_MODEL)
        dh2_ref[...] = dh2

        @pl.when(i == 0)
        def _():
            loss_ref[...] = jnp.zeros_like(loss_ref)
            dgf_ref[...] = jnp.zeros_like(dgf_ref)

        loss_ref[...] += jnp.sum(err * err) * (0.5 / D_MODEL)
        dgf_ref[...] += dgf

    return pl.pallas_call(
        body, name="fwd_ffn_loss", grid=(L // TM,),
        in_specs=_ffn_in_specs(L) + [_full(fcw.shape), _full(fcb.shape), _full(wdown.shape), _rows(TM, D_MODEL),
                                     pl.BlockSpec((TM, D_MODEL), lambda i: (jnp.maximum(i - 1, 0), 0)),
                                     _full(gfin.shape)],
        out_specs=[_rows(TM, D_MODEL), _full((1, LANES)), _full((1, D_MODEL))],
        out_shape=[_sds((L, D_MODEL), F32), _sds((1, LANES), F32), _sds((1, D_MODEL), F32)],
        scratch_shapes=[pltpu.VMEM((TM, D_FF), BF16)],
        compiler_params=_params(1))(up0, up0, up0, up0, fcw, fcb, wdown, h1, target, gfin)


def _bwd_ffn_act(dh2, up0, fcw, fcb, wdown):
    L = dh2.shape[0]

    def body(dh2_ref, upg_ref, upv_ref, hg_ref, hv_ref, fcw_ref, fcb_ref, wd_ref, dup_ref, act_ref, dfcb_ref):
        hg, hv = _ffn_halos(hg_ref, hv_ref)
        da = _dot(dh2_ref[...].astype(BF16), wd_ref[...], NT)

        @pl.when(pl.program_id(0) == 0)
        def _():
            dfcb_ref[...] = jnp.zeros_like(dfcb_ref)

        for c in range(D_FF // FF_CHUNK):
            cs = slice(c * FF_CHUNK, (c + 1) * FF_CHUNK)
            vs = slice(D_FF + c * FF_CHUNK, D_FF + (c + 1) * FF_CHUNK)
            g, val = _ffn_act_chunk(c, upg_ref, upv_ref, hg, hv, fcw_ref, fcb_ref)
            sg = jax.nn.sigmoid(g)
            si = g * sg
            dac = da[:, cs]
            act_ref[:, cs] = (si * val).astype(BF16)
            dg = dac * val * (sg * (1.0 + g * (1.0 - sg)))
            dv = dac * si
            dup_ref[:, cs] = dg.astype(BF16)
            dup_ref[:, vs] = dv.astype(BF16)
            dfcb_ref[:, cs] += jnp.sum(dg, 0, keepdims=True)
            dfcb_ref[:, vs] += jnp.sum(dv, 0, keepdims=True)

    return pl.pallas_call(
        body, name="bwd_ffn_act", grid=(L // TM,),
        in_specs=[_rows(TM, D_MODEL)] + _ffn_in_specs(L) + [_full(fcw.shape), _full(fcb.shape), _full(wdown.shape)],
        out_specs=[_rows(TM, 2 * D_FF), _rows(TM, D_FF), _full((1, 2 * D_FF))],
        out_shape=[_sds((L, 2 * D_FF), BF16), _sds((L, D_FF), BF16), _sds((1, 2 * D_FF), F32)],
        compiler_params=_params(1))(dh2, up0, up0, up0, up0, fcw, fcb, wdown)


def _bwd_ffn_conv(dup, up0, fcw):
    L, C = dup.shape
    tc = FF_TILE
    nt = L // TM
    nhb = L // HALO_FFN

    def body(dy_ref, dyn_ref, x_ref, xp_ref, w_ref, dx_ref, dw_ref):
        i = pl.program_id(1)
        yy = jnp.concatenate([dy_ref[...].astype(F32), jnp.where(i < nt - 1, dyn_ref[...].astype(F32), 0.0)], 0)
        dx_ref[...] = _conv_taps(yy, w_ref, HALO_FFN, TM, True).astype(BF16)
        xx = jnp.concatenate([jnp.where(i > 0, xp_ref[...].astype(F32), 0.0), x_ref[...].astype(F32)], 0)
        dy = yy[:TM]

        @pl.when(i == 0)
        def _():
            dw_ref[...] = jnp.zeros_like(dw_ref)

        for k in range(FFN_CONV_WIDTH):
            d = FFN_CONV_WIDTH - 1 - k
            xs = xx if d == 0 else pltpu.roll(xx, d, 0)
            dw_ref[k:k + 1, :] += jnp.sum(dy * xs[HALO_FFN:], 0, keepdims=True)

    col = lambda j, i: (i, j)
    return pl.pallas_call(
        body, name="bwd_ffn_conv", grid=(C // tc, nt),
        in_specs=[pl.BlockSpec((TM, tc), col),
                  pl.BlockSpec((HALO_FFN, tc), lambda j, i: (jnp.minimum((i + 1) * (TM // HALO_FFN), nhb - 1), j)),
                  pl.BlockSpec((TM, tc), col),
                  pl.BlockSpec((HALO_FFN, tc), lambda j, i: (jnp.maximum(i * (TM // HALO_FFN) - 1, 0), j)),
                  pl.BlockSpec((FFN_CONV_WIDTH, tc), lambda j, i: (0, j))],
        out_specs=[pl.BlockSpec((TM, tc), col), pl.BlockSpec((FFN_CONV_WIDTH, tc), lambda j, i: (0, j))],
        out_shape=[_sds((L, C), BF16), _sds((FFN_CONV_WIDTH, C), F32)],
        compiler_params=_params(2))(dup, dup, up0, up0, fcw)


def _bwd_mix(dh2, dn2, h1, gffn, wout, o, ga, u1, lg, lb, og):
    L = h1.shape[0]

    def body(dh2_ref, dn2_ref, h1_ref, gffn_ref, wout_ref, o_ref, ga_ref, u1_ref, lg_ref, lb_ref, og_ref,
             dh1_ref, dh1m_ref, do_ref, delta_ref, du1_ref, dgffn_ref, dga_ref, dog_ref, dlg_ref, dlb_ref):
        h1 = h1_ref[...]
        dn2x, dgffn = _rms_bwd(dn2_ref[...], h1, _rms_r(h1, D_MODEL), gffn_ref[...], D_MODEL)
        dh1 = dh2_ref[...] + dn2x
        dh1_ref[...] = dh1
        dh1m = jnp.where(_row_ids(TM) >= DEAD, dh1, 0.0).astype(BF16)
        dh1m_ref[...] = dh1m
        dmix = _dot(dh1m, wout_ref[...], NT)
        dma, dmb = dmix[:, :D_CONV], dmix[:, D_CONV:]
        of = o_ref[...].astype(F32)
        do, dga = _rms_bwd(dmb, of, _rms_r(of, N_HEADS * V_HEAD), ga_ref[...], N_HEADS * V_HEAD)
        do_ref[...] = do.astype(BF16)
        prod = do * of
        delta_ref[...] = _scatter_stats(
            [jnp.sum(prod[:, h * HB:(h + 1) * HB], -1, keepdims=True) for h in range(N_HEADS)], TM)
        lg = lg_ref[...]
        rs, xh, u2, sg, u = _ln_silu(u1_ref[...], lg, lb_ref[...])
        du, dog = _rms_bwd(dma, u, _rms_r(u, D_CONV), og_ref[...], D_CONV)
        du2 = du * (sg * (1.0 + u2 * (1.0 - sg)))
        dxh = du2 * lg
        du1_ref[...] = rs * (dxh - jnp.mean(dxh, -1, keepdims=True) - xh * jnp.mean(dxh * xh, -1, keepdims=True))

        @pl.when(pl.program_id(0) == 0)
        def _():
            for ref in (dgffn_ref, dga_ref, dog_ref, dlg_ref, dlb_ref):
                ref[...] = jnp.zeros_like(ref)

        dgffn_ref[...] += dgffn
        dga_ref[...] += dga
        dog_ref[...] += dog
        dlg_ref[...] += jnp.sum(du2 * xh, 0, keepdims=True)
        dlb_ref[...] += jnp.sum(du2, 0, keepdims=True)

    return pl.pallas_call(
        body, name="bwd_mix", grid=(L // TM,),
        in_specs=[_rows(TM, D_MODEL), _rows(TM, D_MODEL), _rows(TM, D_MODEL), _full(gffn.shape), _full(wout.shape),
                  _rows(TM, D_HEADS), _full(ga.shape), _rows(TM, D_CONV), _full(lg.shape), _full(lb.shape),
                  _full(og.shape)],
        out_specs=[_rows(TM, D_MODEL), _rows(TM, D_MODEL), _rows(TM, D_HEADS), _rows(TM, N_GROUPS * LANES),
                   _rows(TM, D_CONV), _full((1, D_MODEL)), _full((1, D_HEADS)), _full((1, D_CONV)),
                   _full((1, D_CONV)), _full((1, D_CONV))],
        out_shape=[_sds((L, D_MODEL), F32), _sds((L, D_MODEL), BF16), _sds((L, D_HEADS), BF16),
                   _sds((L, N_GROUPS * LANES), F32),
                   _sds((L, D_CONV), F32), _sds((1, D_MODEL), F32), _sds((1, D_HEADS), F32), _sds((1, D_CONV), F32),
                   _sds((1, D_CONV), F32), _sds((1, D_CONV), F32)],
        compiler_params=_params(1))(dh2, dn2, h1, gffn, wout, o, ga, u1, lg, lb, og)


def _attn_bwd(q, kv, kr, do, lse, delta):
    L = q.shape[0]
    t = TM
    nt = L // t
    gw = HEAD_GROUP * HB

    def body(q_ref, kv_ref, kr_ref, do_ref, lse_ref, delta_ref, dq_ref, dkv_ref, dkr_ref, dq_acc, dk_acc, dv_acc):
        g, j = pl.program_id(0), pl.program_id(1)
        lane = lax.broadcasted_iota(jnp.int32, (t, HB), 1)

        @pl.when(j == 0)
        def _():
            dq_acc[...] = jnp.zeros_like(dq_acc)

        @pl.when((j == 0) & (g == 0))
        def _():
            dkr_ref[...] = jnp.zeros_like(dkr_ref)

        dk_acc[...] = jnp.zeros_like(dk_acc)
        dv_acc[...] = jnp.zeros_like(dv_acc)
        krj = kr_ref[...]

        def tile(i, vis):
            rows = pl.ds(pl.multiple_of(i * t, t), t)
            lse_i, delta_i = lse_ref[rows, :], delta_ref[rows, :]
            heads = range(HEAD_GROUP)
            cols = [slice(h * HB, (h + 1) * HB) for h in heads]
            kvj = [kv_ref[:, hc] for hc in cols]
            kk = [jnp.where(lane < QK_NOPE, kv, krj) for kv in kvj]
            qi = [q_ref[rows, hc] for hc in cols]
            doi = [do_ref[rows, hc] for hc in cols]
            s = [_dot(qi[h], kk[h], NT) for h in heads]
            dp = [_dot(doi[h], kvj[h], NT) for h in heads]
            p = []
            for h in heads:
                sh = s[h] * SCALE
                if vis is not None:
                    sh = jnp.where(vis, sh, NEG)
                p.append(jnp.exp(sh - lse_i[:, h:h + 1]))
            for h in heads:
                dv_acc[:, cols[h]] += _dot(p[h].astype(BF16), doi[h], TN)
            ds = [(p[h] * (dp[h] - delta_i[:, h:h + 1]) * SCALE).astype(BF16) for h in heads]
            for h in heads:
                dk_acc[:, cols[h]] += _dot(ds[h], qi[h], TN)
            for h in heads:
                dq_acc[rows, cols[h]] += _dot(ds[h], kk[h], NN)

        tile(j, _visible(j, j, t))

        @pl.when(j == 0)
        def _():
            def masked(i, carry):
                tile(i, _visible(i, 0, t))
                return carry

            lax.fori_loop(1, nt, masked, 0)

        @pl.when(j > 0)
        def _():
            def unmasked(i, carry):
                tile(i, None)
                return carry

            lax.fori_loop(j + 1, nt, unmasked, 0)

        dkr = jnp.zeros((t, HB), F32)
        for h in range(HEAD_GROUP):
            hc = slice(h * HB, (h + 1) * HB)
            dk = dk_acc[:, hc]
            dkv_ref[:, hc] = jnp.where(lane < QK_NOPE, dk, dv_acc[:, hc]).astype(BF16)
            dkr = dkr + jnp.where(lane >= QK_NOPE, dk, 0.0)
        dkr_ref[pl.ds(pl.multiple_of(j * t, t), t), :] += dkr

        @pl.when(j == nt - 1)
        def _():
            dq_ref[...] = dq_acc[...].astype(BF16)

    group = lambda g, j: (0, g)
    return pl.pallas_call(
        body, name="attn_bwd", grid=(N_GROUPS, nt),
        in_specs=[_resident((L, gw), group), pl.BlockSpec((t, gw), lambda g, j: (j, g)),
                  pl.BlockSpec((t, HB), lambda g, j: (j, 0)), _resident((L, gw), group),
                  _resident((L, LANES), group), _resident((L, LANES), group)],
        out_specs=[pl.BlockSpec((L, gw), group), pl.BlockSpec((t, gw), lambda g, j: (j, g)),
                   pl.BlockSpec((L, HB), lambda g, j: (0, 0))],
        out_shape=[_sds((L, D_HEADS), BF16), _sds((L, D_HEADS), BF16), _sds((L, HB), F32)],
        scratch_shapes=[pltpu.VMEM((L, gw), F32), pltpu.VMEM((t, gw), F32), pltpu.VMEM((t, gw), F32)],
        compiler_params=_params(2))(q, kv, kr, do, lse, delta)


def _bwd_conv(du1, u0, cw, zag):
    L = du1.shape[0]
    nt = L // TM

    def body(dy_ref, dyn_ref, x_ref, xp_ref, cw_ref, zag_ref, dzag_ref, dcw_ref, dcb_ref):
        i = pl.program_id(0)
        dy = dy_ref[...]
        yy = jnp.concatenate([dy, jnp.where(i < nt - 1, dyn_ref[...], 0.0)], 0)
        du0 = _conv_taps(yy, cw_ref, HALO_CONV, TM, True)
        zag = zag_ref[...].astype(F32)
        a, sg = zag[:, :D_CONV], jax.nn.sigmoid(zag[:, D_CONV:])
        dzag_ref[...] = jnp.concatenate([du0 * sg, du0 * a * sg * (1.0 - sg)], 1).astype(BF16)
        xx = jnp.concatenate([jnp.where(i > 0, xp_ref[...], 0.0), x_ref[...]], 0)

        @pl.when(i == 0)
        def _():
            dcw_ref[...] = jnp.zeros_like(dcw_ref)
            dcb_ref[...] = jnp.zeros_like(dcb_ref)

        for k in range(CONV_WIDTH):
            d = CONV_WIDTH - 1 - k
            xs = xx if d == 0 else pltpu.roll(xx, d, 0)
            dcw_ref[k:k + 1, :] += jnp.sum(dy * xs[HALO_CONV:], 0, keepdims=True)
        dcb_ref[...] += jnp.sum(dy, 0, keepdims=True)

    return pl.pallas_call(
        body, name="bwd_conv", grid=(nt,),
        in_specs=[_rows(TM, D_CONV), _next(HALO_CONV, D_CONV, TM, L // HALO_CONV), _rows(TM, D_CONV),
                  _prev(HALO_CONV, D_CONV, TM), _full(cw.shape), _rows(TM, D_AG)],
        out_specs=[_rows(TM, D_AG), _full(cw.shape), _full((1, D_CONV))],
        out_shape=[_sds((L, D_AG), BF16), _sds(cw.shape, F32), _sds((1, D_CONV), F32)],
        compiler_params=_params(1))(du1, du1, u0, u0, cw, zag)


def _bwd_in(dzag, dq, dkv, dkr, cq, ckv, gq, gkv, wuq, wukv, win, rc, rs1, rs2, h0, gmix, dh1):
    L = h0.shape[0]

    def body(dzag_ref, dq_ref, dkv_ref, dkr_ref, cq_ref, ckv_ref, gq_ref, gkv_ref, wuq_ref, wukv_ref, win_ref,
             c_ref, s1_ref, s2_ref, h0_ref, gmix_ref, dh1_ref,
             dz_ref, dqr_ref, gx_ref, dfirst_ref, dgq_ref, dgkv_ref, dgmix_ref):
        i = pl.program_id(0)
        c, s1, s2 = c_ref[...], s1_ref[...], s2_ref[...]
        dqr = _rope_bwd(dq_ref[...].astype(F32), jnp.tile(c, (1, N_HEADS)), jnp.tile(s1, (1, N_HEADS)),
                        jnp.tile(s2, (1, N_HEADS))).astype(BF16)
        dqr_ref[...] = dqr
        cq, ckv = cq_ref[...], ckv_ref[...]
        dcq, dgq = _rms_bwd(_dot(dqr, wuq_ref[...], NT), cq, _rms_r(cq, Q_LORA), gq_ref[...], Q_LORA)
        dckv, dgkv = _rms_bwd(_dot(dkv_ref[...], wukv_ref[...], NT), ckv, _rms_r(ckv, KV_LORA), gkv_ref[...], KV_LORA)
        dkrp = _rope_bwd(dkr_ref[...], c, s1, s2)
        dz = jnp.concatenate([dzag_ref[...], dcq.astype(BF16), dckv.astype(BF16), dkrp.astype(BF16)], 1)
        dz_ref[...] = dz
        h0 = h0_ref[...]
        dnx, dgmix = _rms_bwd(_dot(dz, win_ref[...], NT), h0, _rms_r(h0, D_MODEL), gmix_ref[...], D_MODEL)
        dh0 = dh1_ref[...] + dnx

        @pl.when(i == 0)
        def _():
            dfirst_ref[...] = dh0
            for ref in (dgq_ref, dgkv_ref, dgmix_ref):
                ref[...] = jnp.zeros_like(ref)

        @pl.when(i > 0)
        def _():
            gx_ref[...] = dh0

        dgq_ref[...] += dgq
        dgkv_ref[...] += dgkv
        dgmix_ref[...] += dgmix

    return pl.pallas_call(
        body, name="bwd_in", grid=(L // TM,),
        in_specs=[_rows(TM, D_AG), _rows(TM, D_HEADS), _rows(TM, D_HEADS), _rows(TM, HB), _rows(TM, Q_LORA),
                  _rows(TM, KV_LORA), _full(gq.shape), _full(gkv.shape), _full(wuq.shape), _full(wukv.shape),
                  _full(win.shape), _rows(TM, HB), _rows(TM, HB), _rows(TM, HB), _rows(TM, D_MODEL),
                  _full(gmix.shape), _rows(TM, D_MODEL)],
        out_specs=[_rows(TM, D_ZP), _rows(TM, D_HEADS),
                   pl.BlockSpec((TM, D_MODEL), lambda i: (jnp.maximum(i - 1, 0), 0)), _full((TM, D_MODEL)),
                   _full((1, Q_LORA)), _full((1, KV_LORA)), _full((1, D_MODEL))],
        out_shape=[_sds((L, D_ZP), BF16), _sds((L, D_HEADS), BF16), _sds((L - TM, D_MODEL), F32),
                   _sds((TM, D_MODEL), F32), _sds((1, Q_LORA), F32), _sds((1, KV_LORA), F32), _sds((1, D_MODEL), F32)],
        compiler_params=_params(1))(dzag, dq, dkv, dkr, cq, ckv, gq, gkv, wuq, wukv, win, rc, rs1, rs2, h0, gmix, dh1)


def _mesh_pos():
    return lax.axis_index("x"), lax.axis_index("y"), lax.axis_index("c")


def _all_gather(shards):
    n = len(shards)
    halves = [s.shape[0] // 2 for s in shards]

    def body(*refs):
        ins, outs = refs[:n], refs[n:2 * n]
        send_sems, recv_sems = refs[2 * n:]
        x, y, c = _mesh_pos()
        chips = [(1 - x, y), (x, 1 - y), (1 - x, 1 - y)]
        started = []
        for a in range(n):
            m = halves[a]
            mine = ins[a].at[pl.ds(pl.multiple_of(c * m, 16), m)]
            for k, chip in enumerate(chips):
                cp = pltpu.make_async_remote_copy(mine, outs[a].at[4 * x + 2 * y + c], send_sems.at[a, k],
                                                  recv_sems.at[a, k], device_id=(*chip, c), device_id_type=MESH)
                cp.start()
                started.append(cp.wait_send)
        for a in range(n):
            out = outs[a]
            for k, chip in enumerate(chips):
                got = out.at[4 * chip[0] + 2 * chip[1] + c]
                pltpu.make_async_remote_copy(got, got, send_sems.at[a, k], recv_sems.at[a, k],
                                             device_id=(*chip, c), device_id_type=MESH).wait_recv()
                cp = pltpu.make_async_remote_copy(got, got, send_sems.at[a, 3 + k], recv_sems.at[a, 3 + k],
                                                  device_id=(x, y, 1 - c), device_id_type=MESH)
                cp.start()
                started.append(cp.wait_send)
        for a in range(n):
            out = outs[a]
            for k, chip in enumerate(chips):
                got = out.at[4 * chip[0] + 2 * chip[1] + 1 - c]
                pltpu.make_async_remote_copy(got, got, send_sems.at[a, 3 + k], recv_sems.at[a, 3 + k],
                                             device_id=(x, y, 1 - c), device_id_type=MESH).wait_recv()
        for wait in started:
            wait()

    any_spec = pl.BlockSpec(memory_space=pl.ANY)
    outs = pl.pallas_call(
        body, name="all_gather_weights", in_specs=[any_spec] * n, out_specs=[any_spec] * n,
        out_shape=[_sds((8, m) + s.shape[1:], s.dtype) for s, m in zip(shards, halves)],
        scratch_shapes=[pltpu.SemaphoreType.DMA((n, 6)), pltpu.SemaphoreType.DMA((n, 6))])(*shards)
    return [_own_slot(o.reshape((N_CHIPS, s.shape[0]) + s.shape[1:]), s[None]) for o, s in zip(outs, shards)]


def _own_slot(by_chip, own):
    me = 2 * lax.axis_index("x") + lax.axis_index("y")
    chip = lax.broadcasted_iota(jnp.int32, (N_CHIPS,) + (1,) * (by_chip.ndim - 1), 0)
    return jnp.where(chip == me, own, by_chip)


def _sibling_exchange(parts):
    n = len(parts)

    def body(*refs):
        ins, theirs = refs[:n], refs[n:2 * n]
        send_sems, recv_sems = refs[2 * n:]
        x, y, c = _mesh_pos()
        copies = [pltpu.make_async_remote_copy(ins[a].at[1 - c], theirs[a], send_sems.at[a], recv_sems.at[a],
                                               device_id=(x, y, 1 - c), device_id_type=MESH) for a in range(n)]
        for cp in copies:
            cp.start()
        for cp in copies:
            cp.wait()

    any_spec = pl.BlockSpec(memory_space=pl.ANY)
    return pl.pallas_call(
        body, name="grad_sibling_exchange", in_specs=[any_spec] * n, out_specs=[any_spec] * n,
        out_shape=[_sds(p.shape[1:], p.dtype) for p in parts],
        scratch_shapes=[pltpu.SemaphoreType.DMA((n,)), pltpu.SemaphoreType.DMA((n,))])(*parts)


def _chip_exchange(parts):
    n = len(parts)

    def body(*refs):
        ins, outs = refs[:n], refs[n:2 * n]
        send_sems, recv_sems = refs[2 * n:]
        x, y, c = _mesh_pos()
        me = 2 * x + y
        chips = [(1 - x, y), (x, 1 - y), (1 - x, 1 - y)]
        waits = []
        for a in range(n):
            for k, chip in enumerate(chips):
                rc = pltpu.make_async_remote_copy(ins[a].at[2 * chip[0] + chip[1]], outs[a].at[me],
                                                  send_sems.at[a, k], recv_sems.at[a, k],
                                                  device_id=(*chip, c), device_id_type=MESH)
                rc.start()
                waits.append(rc.wait_send)
        for a in range(n):
            for k, chip in enumerate(chips):
                src = outs[a].at[2 * chip[0] + chip[1]]
                pltpu.make_async_remote_copy(src, src, send_sems.at[a, k], recv_sems.at[a, k],
                                             device_id=(*chip, c), device_id_type=MESH).wait_recv()
        for wait in waits:
            wait()

    any_spec = pl.BlockSpec(memory_space=pl.ANY)
    return pl.pallas_call(
        body, name="grad_chip_exchange", in_specs=[any_spec] * n, out_specs=[any_spec] * n,
        out_shape=[_sds(p.shape, p.dtype) for p in parts],
        scratch_shapes=[pltpu.SemaphoreType.DMA((n, 3)), pltpu.SemaphoreType.DMA((n, 3))])(*parts)


def _sibling_gather(parts):
    n = len(parts)

    def body(*refs):
        ins, outs = refs[:n], refs[n:2 * n]
        send_sems, recv_sems = refs[2 * n:]
        x, y, c = _mesh_pos()
        copies = [pltpu.make_async_remote_copy(ins[a], outs[a], send_sems.at[a], recv_sems.at[a],
                                               device_id=(x, y, 1 - c), device_id_type=MESH) for a in range(n)]
        for cp in copies:
            cp.start()
        for cp in copies:
            cp.wait()

    any_spec = pl.BlockSpec(memory_space=pl.ANY)
    return pl.pallas_call(
        body, name="grad_sibling_gather", in_specs=[any_spec] * n, out_specs=[any_spec] * n,
        out_shape=[_sds(p.shape, p.dtype) for p in parts],
        scratch_shapes=[pltpu.SemaphoreType.DMA((n,)), pltpu.SemaphoreType.DMA((n,))])(*parts)


def _row_tile(rows, row_bytes, align, budget=1 << 20):
    best = None
    for t in range(align, rows + 1, align):
        if rows % t == 0 and t * row_bytes <= budget:
            best = t
    return best or rows


def _add_pair(pack, theirs, c):
    shape = theirs.shape
    a3, b2 = pack.reshape(2, -1, LANES), theirs.reshape(-1, LANES)
    rows = b2.shape[0]
    tr = _row_tile(rows, LANES * 4, 16)

    def body(c_ref, a_ref, b_ref, o_ref):
        o_ref[...] = (a_ref[...].astype(F32) + b_ref[...].astype(F32)).astype(o_ref.dtype)

    grid_spec = pltpu.PrefetchScalarGridSpec(
        num_scalar_prefetch=1, grid=(rows // tr,),
        in_specs=[pl.BlockSpec((None, tr, LANES), lambda i, c_ref: (c_ref[0], i, 0)),
                  pl.BlockSpec((tr, LANES), lambda i, c_ref: (i, 0))],
        out_specs=pl.BlockSpec((tr, LANES), lambda i, c_ref: (i, 0)))
    out = pl.pallas_call(
        body, name="grad_add_pair", grid_spec=grid_spec, out_shape=_sds(b2.shape, pack.dtype),
        compiler_params=_params(1))(jnp.reshape(c, (1,)).astype(jnp.int32), a3, b2)
    return out.reshape(shape)


def _add_chips(p):
    nh = p.shape[1]
    tr = _row_tile(nh, LANES * 4 * N_CHIPS, 16)

    def body(p_ref, o_ref):
        acc = p_ref[0].astype(F32)
        for q in range(1, N_CHIPS):
            acc = acc + p_ref[q].astype(F32)
        o_ref[...] = acc

    return pl.pallas_call(
        body, name="grad_add_chips", grid=(nh // tr,),
        in_specs=[pl.BlockSpec((N_CHIPS, tr, LANES), lambda i: (0, i, 0))], out_specs=_rows(tr, LANES),
        out_shape=_sds((nh, LANES), F32), compiler_params=_params(1))(p)


def _reduce_scatter(packs):
    x, y, c = _mesh_pos()
    theirs = _sibling_exchange(packs)
    pair = [_add_pair(p, t, c) for p, t in zip(packs, theirs)]
    by_src = [_own_slot(got, lax.dynamic_index_in_dim(p, 2 * x + y, 0)) for got, p in zip(_chip_exchange(pair), pair)]
    half = [_add_chips(p) for p in by_src]
    other = _sibling_gather(half)
    return [jnp.where(c == 0, jnp.concatenate([h, o], 0), jnp.concatenate([o, h], 0)) for h, o in zip(half, other)]


def _adamw_math(w, g, m, v):
    m = ADAM_B1 * m + (1.0 - ADAM_B1) * g
    v = ADAM_B2 * v + (1.0 - ADAM_B2) * (g * g)
    m_hat = m / (1.0 - ADAM_B1 ** ADAM_STEP)
    v_hat = v / (1.0 - ADAM_B2 ** ADAM_STEP)
    return -ADAM_LR * (m_hat / (jnp.sqrt(v_hat) + ADAM_EPS) + ADAM_WD * w), m, v


def _adamw_big(w, g, m, v, name):
    r, c = w.shape
    tr = _row_tile(r, c * 4, 8, budget=1 << 19)

    def body(w_ref, g_ref, m_ref, v_ref, d_ref, mo_ref, vo_ref):
        d_ref[...], mo_ref[...], vo_ref[...] = _adamw_math(w_ref[...], g_ref[...], m_ref[...], v_ref[...])

    return pl.pallas_call(
        body, name=name, grid=(r // tr,), in_specs=[_rows(tr, c)] * 4, out_specs=[_rows(tr, c)] * 3,
        out_shape=[_sds((r, c), F32)] * 3, compiler_params=_params(1))(w, g, m, v)


def _adamw_small(ws, gs, ms, vs):
    n = len(ws)

    def body(*refs):
        for a in range(n):
            w_ref, g_ref, m_ref, v_ref = (refs[k * n + a] for k in range(4))
            d, m, v = _adamw_math(w_ref[...], g_ref[...], m_ref[...], v_ref[...])
            refs[4 * n + a][...] = d
            refs[5 * n + a][...] = m
            refs[6 * n + a][...] = v

    vm = pl.BlockSpec(memory_space=pltpu.VMEM)
    outs = pl.pallas_call(
        body, name="adamw_small", in_specs=[vm] * (4 * n), out_specs=[vm] * (3 * n),
        out_shape=[_sds(w.shape, F32) for w in ws] * 3)(*ws, *gs, *ms, *vs)
    return outs[:n], outs[n:2 * n], outs[2 * n:]


BIG = ("w_in", "w_uq", "w_ukv", "w_out", "w_ffn_up", "w_ffn_down")
SMALL_SHARDED = ("conv_w", "ffn_conv_w", "meta_tokens")
REPLICATED = ("mix_norm_g", "q_norm_g", "kv_norm_g", "conv_b", "conv_ln_g", "conv_ln_b", "conv_out_g", "attn_out_g",
              "ffn_norm_g", "ffn_conv_b", "final_norm_g")
WEIGHTS = ("meta_tokens", "mix_norm_g", "w_in", "q_norm_g", "w_uq", "kv_norm_g", "w_ukv", "conv_w", "conv_b",
           "conv_ln_g", "conv_ln_b", "conv_out_g", "attn_out_g", "w_out", "ffn_norm_g", "w_ffn_up", "ffn_conv_w",
           "ffn_conv_b", "w_ffn_down", "final_norm_g")


def _lane_rows(a):
    return a.reshape(N_CHIPS, -1, LANES)


def _col_shards(a):
    k = a.shape[0]
    return a.reshape(k, N_CHIPS, -1).transpose(1, 0, 2)


def _from_col_shards(a):
    return a.transpose(1, 0, 2).reshape(a.shape[1], -1)


def _pad_rows_to(a, rows):
    return jnp.pad(a, ((0, 0), (0, rows - a.shape[1]), (0, 0)))


def _rope_tables(L):
    pos = (jnp.arange(L, dtype=jnp.int32) - DEAD).astype(F32)
    inv_freq = 1.0 / (ROPE_THETA ** (jnp.arange(0, QK_ROPE, 2, dtype=F32) / QK_ROPE))
    ang = pos[:, None] * inv_freq[None, :]
    cos, sin = jnp.cos(ang), jnp.sin(ang)
    half = QK_ROPE // 2
    z = lambda n: jnp.zeros((L, n), F32)
    rc = jnp.concatenate([jnp.ones((L, QK_NOPE), F32), cos, cos, z(HB - QK_NOPE - QK_ROPE)], 1)
    rs1 = jnp.concatenate([z(QK_NOPE), -sin, z(HB - QK_NOPE - half)], 1)
    rs2 = jnp.concatenate([z(QK_NOPE + half), sin, z(HB - QK_NOPE - QK_ROPE)], 1)
    return rc, rs1, rs2


def _pad_heads(g):
    return jnp.pad(g.reshape(N_HEADS, V_HEAD), ((0, 0), (HB - V_HEAD, 0))).reshape(1, D_HEADS)


def _unpad_heads(g):
    return g.reshape(N_HEADS, HB)[:, HB - V_HEAD:].reshape(1, N_HEADS * V_HEAD)


def _local_step(x, target, w):
    S = x.shape[0]
    L = TM + S
    tl = L // 4
    d_qk = QK_NOPE + QK_ROPE
    win_n = w["w_in"]
    kr0 = D_AG + Q_LORA + KV_LORA
    win = jnp.concatenate([win_n[:, :kr0], jnp.zeros((D_MODEL, QK_NOPE), BF16), win_n[:, kr0:],
                           jnp.zeros((D_MODEL, HB - d_qk), BF16)], 1)
    wuq = jnp.pad(w["w_uq"].reshape(Q_LORA, N_HEADS, d_qk), ((0, 0), (0, 0), (0, HB - d_qk))).reshape(Q_LORA, D_HEADS)
    wukv = w["w_ukv"]
    wout_n = w["w_out"]
    wout = jnp.concatenate([wout_n[:D_CONV], jnp.pad(wout_n[D_CONV:].reshape(N_HEADS, V_HEAD, D_MODEL),
                                                     ((0, 0), (HB - V_HEAD, 0), (0, 0))).reshape(D_HEADS, D_MODEL)], 0)
    wup, wdown = w["w_ffn_up"], w["w_ffn_down"]
    ga = _pad_heads(w["attn_out_g"])
    gfin = w["final_norm_g"].reshape(1, D_MODEL)
    rc, rs1, rs2 = _rope_tables(L)
    h0 = jnp.concatenate([jnp.zeros((DEAD, D_MODEL), F32), w["meta_tokens"], x], 0)

    n, zag, u0, cq, ckv, qn, kvn, q, kv, kr = _fwd_in(h0, w["mix_norm_g"], win, w["q_norm_g"], wuq, w["kv_norm_g"],
                                                       wukv, rc, rs1, rs2)
    u1, mixa = _fwd_conv(u0, w["conv_w"], w["conv_b"], w["conv_ln_g"], w["conv_ln_b"], w["conv_out_g"])
    o, lse = _attn_fwd(q, kv, kr)
    mix, h1, n2 = _fwd_mix(h0, mixa, o, ga, wout, w["ffn_norm_g"])
    up0 = _mm(n2, wup, NN, BF16, tl, FF_TILE, D_MODEL, "ffn_up")
    dh2, loss, g_fin = _fwd_ffn_loss(up0, w["ffn_conv_w"], w["ffn_conv_b"], wdown, h1, target, gfin)

    dup, act, g_fcb = _bwd_ffn_act(dh2, up0, w["ffn_conv_w"], w["ffn_conv_b"], wdown)
    dup0, g_fcw = _bwd_ffn_conv(dup, up0, w["ffn_conv_w"])
    dn2 = _mm(dup0, wup, NT, F32, tl, D_MODEL, FF_TILE, "ffn_up_dx")
    g_wup = _mm(n2, dup0, TN, F32, D_MODEL, FF_TILE, tl, "ffn_up_dw")
    g_wdown = _mm(act, dh2, TN, F32, D_FF // 2, D_MODEL, tl, "ffn_down_dw")
    dh1, dh1m, do, delta, du1, g_gffn, g_ga, g_og, g_lg, g_lb = _bwd_mix(
        dh2, dn2, h1, w["ffn_norm_g"], wout, o, ga, u1, w["conv_ln_g"], w["conv_ln_b"], w["conv_out_g"])
    g_wout = _mm(mix, dh1m, TN, F32, D_MIX // 2, D_MODEL, tl, "out_dw")
    dq, dkv, dkr = _attn_bwd(q, kv, kr, do, lse, delta)
    dzag, g_cw, g_cb = _bwd_conv(du1, u0, w["conv_w"], zag)
    dz, dqr, gx, dfirst, g_gq, g_gkv, g_gmix = _bwd_in(dzag, dq, dkv, dkr, cq, ckv, w["q_norm_g"], w["kv_norm_g"],
                                                      wuq, wukv, win, rc, rs1, rs2, h0, w["mix_norm_g"], dh1)
    g_win = _mm(n, dz, TN, F32, D_MODEL, D_ZP // 2, tl, "in_dw")
    g_wuq = _mm(qn, dqr, TN, F32, Q_LORA, D_HEADS, tl, "uq_dw")
    g_wukv = _mm(kvn, dkv, TN, F32, KV_LORA, D_HEADS, tl, "ukv_dw")

    grads = {
        "w_in": jnp.concatenate([g_win[:, :kr0], g_win[:, kr0 + QK_NOPE:kr0 + d_qk]], 1),
        "w_uq": g_wuq.reshape(Q_LORA, N_HEADS, HB)[:, :, :d_qk].reshape(Q_LORA, N_HEADS * d_qk),
        "w_ukv": g_wukv,
        "w_out": jnp.concatenate([g_wout[:D_CONV], g_wout[D_CONV:].reshape(N_HEADS, HB, D_MODEL)[:, HB - V_HEAD:]
                                  .reshape(N_HEADS * V_HEAD, D_MODEL)], 0),
        "w_ffn_up": g_wup, "w_ffn_down": g_wdown, "conv_w": g_cw, "ffn_conv_w": g_fcw,
        "meta_tokens": dfirst[DEAD:], "mix_norm_g": g_gmix, "q_norm_g": g_gq, "kv_norm_g": g_gkv, "conv_b": g_cb,
        "conv_ln_g": g_lg, "conv_ln_b": g_lb, "conv_out_g": g_og, "attn_out_g": _unpad_heads(g_ga),
        "ffn_norm_g": g_gffn, "ffn_conv_b": g_fcb, "final_norm_g": g_fin,
    }
    return loss, gx, grads


ROW_SHARDED = ("w_out", "w_ffn_down")


def _owner_pieces(name, g):
    if name in ROW_SHARDED:
        return _lane_rows(g.reshape(N_CHIPS, -1))
    return _lane_rows(_col_shards(g))


def _split_halves(p):
    return p.reshape(N_CHIPS, 2, p.shape[1] // 2, LANES).transpose(1, 0, 2, 3)


def kernel(x, meta_tokens, mix_norm_g, w_in, q_norm_g, w_uq, kv_norm_g, w_ukv, conv_w, conv_b, conv_ln_g, conv_ln_b, conv_out_g, attn_out_g, w_out, ffn_norm_g, w_ffn_up, ffn_conv_w, ffn_conv_b, w_ffn_down, final_norm_g, loss_target, m_meta_tokens, m_mix_norm_g, m_w_in, m_q_norm_g, m_w_uq, m_kv_norm_g, m_w_ukv, m_conv_w, m_conv_b, m_conv_ln_g, m_conv_ln_b, m_conv_out_g, m_attn_out_g, m_w_out, m_ffn_norm_g, m_w_ffn_up, m_ffn_conv_w, m_ffn_conv_b, m_w_ffn_down, m_final_norm_g, v_meta_tokens, v_mix_norm_g, v_w_in, v_q_norm_g, v_w_uq, v_kv_norm_g, v_w_ukv, v_conv_w, v_conv_b, v_conv_ln_g, v_conv_ln_b, v_conv_out_g, v_attn_out_g, v_w_out, v_ffn_norm_g, v_w_ffn_up, v_ffn_conv_w, v_ffn_conv_b, v_w_ffn_down, v_final_norm_g):
    local = dict(meta_tokens=meta_tokens, mix_norm_g=mix_norm_g, w_in=w_in[0], q_norm_g=q_norm_g, w_uq=w_uq[0],
                 kv_norm_g=kv_norm_g, w_ukv=w_ukv[0], conv_w=conv_w[0], conv_b=conv_b, conv_ln_g=conv_ln_g,
                 conv_ln_b=conv_ln_b, conv_out_g=conv_out_g, attn_out_g=attn_out_g, w_out=w_out[0],
                 ffn_norm_g=ffn_norm_g, w_ffn_up=w_ffn_up[0], ffn_conv_w=ffn_conv_w[0], ffn_conv_b=ffn_conv_b,
                 w_ffn_down=w_ffn_down[0], final_norm_g=final_norm_g.reshape(1, D_MODEL))
    ms = dict(zip(WEIGHTS, (m_meta_tokens, m_mix_norm_g, m_w_in, m_q_norm_g, m_w_uq, m_kv_norm_g, m_w_ukv, m_conv_w,
                            m_conv_b, m_conv_ln_g, m_conv_ln_b, m_conv_out_g, m_attn_out_g, m_w_out, m_ffn_norm_g,
                            m_w_ffn_up, m_ffn_conv_w, m_ffn_conv_b, m_w_ffn_down, m_final_norm_g)))
    vs = dict(zip(WEIGHTS, (v_meta_tokens, v_mix_norm_g, v_w_in, v_q_norm_g, v_w_uq, v_kv_norm_g, v_w_ukv, v_conv_w,
                            v_conv_b, v_conv_ln_g, v_conv_ln_b, v_conv_out_g, v_attn_out_g, v_w_out, v_ffn_norm_g,
                            v_w_ffn_up, v_ffn_conv_w, v_ffn_conv_b, v_w_ffn_down, v_final_norm_g)))

    small_flat = jnp.concatenate([local[k].reshape(-1) for k in SMALL_SHARDED]).reshape(-1, LANES)
    gathered = _all_gather([local[k].astype(BF16) for k in BIG] + [small_flat])
    full = dict(local)
    for name, g in zip(BIG, gathered[:len(BIG)]):
        full[name] = g.reshape(-1, g.shape[-1]) if name in ROW_SHARDED else _from_col_shards(g)
    small = gathered[-1].reshape(N_CHIPS, -1)
    at = 0
    for name in SMALL_SHARDED:
        r, c = local[name].shape
        full[name] = _from_col_shards(small[:, at:at + r * c].reshape(N_CHIPS, r, c))
        at += r * c

    loss_row, grad_x, grads = _local_step(x[0], loss_target[0], full)

    big_pack = _split_halves(jnp.concatenate([_owner_pieces(k, grads[k]) for k in BIG], 1).astype(BF16))
    rep = jnp.concatenate([grads[k].reshape(-1) for k in REPLICATED] + [loss_row.reshape(-1)]).reshape(1, -1, LANES)
    small_pieces = [_owner_pieces(k, grads[k]) for k in SMALL_SHARDED] + [jnp.broadcast_to(rep, (N_CHIPS,) + rep.shape[1:])]
    small_rows = sum(p.shape[1] for p in small_pieces)
    small_pack = _split_halves(_pad_rows_to(jnp.concatenate(small_pieces, 1), -(-small_rows // 16) * 16))
    big_tot, small_tot = _reduce_scatter([big_pack, small_pack])

    total = {}
    at = 0
    for name in BIG:
        r, c = local[name].shape
        total[name] = big_tot[at:at + r * c // LANES].reshape(r, c)
        at += r * c // LANES
    flat = small_tot.reshape(-1)
    at = 0
    for name in SMALL_SHARDED + REPLICATED:
        shape = local[name].shape
        size = shape[0] * shape[1]
        total[name] = flat[at:at + size].reshape(shape)
        at += -(-size // LANES) * LANES if name in SMALL_SHARDED else size
    loss = flat[at]

    delta, new_m, new_v = {}, {}, {}
    shape2 = lambda a, name: a.reshape(local[name].shape)
    for name in BIG:
        delta[name], new_m[name], new_v[name] = _adamw_big(local[name], total[name], shape2(ms[name], name),
                                                          shape2(vs[name], name), "adamw_" + name)
    rest = SMALL_SHARDED + REPLICATED
    ds, nms, nvs = _adamw_small([local[k] for k in rest], [total[k] for k in rest],
                                [shape2(ms[k], k) for k in rest], [shape2(vs[k], k) for k in rest])
    for k, d, nm, nv in zip(rest, ds, nms, nvs):
        delta[k], new_m[k], new_v[k] = d, nm, nv

    out_shape = dict(zip(WEIGHTS, (meta_tokens, mix_norm_g, w_in, q_norm_g, w_uq, kv_norm_g, w_ukv, conv_w, conv_b,
                                   conv_ln_g, conv_ln_b, conv_out_g, attn_out_g, w_out, ffn_norm_g, w_ffn_up,
                                   ffn_conv_w, ffn_conv_b, w_ffn_down, final_norm_g)))
    outs = [loss, grad_x[None]]
    for group in (total, delta, new_m, new_v):
        outs += [group[k].reshape(out_shape[k].shape) for k in WEIGHTS]
    return tuple(outs)
```

```python
import functools

import jax
import jax.numpy as jnp
from jax import lax
from jax.experimental import pallas as pl
from jax.experimental.pallas import tpu as pltpu

F32 = jnp.float32
BF16 = jnp.bfloat16

D_MODEL = 1024
D_CONV = 512
CONV_WIDTH = 31
N_HEADS = 8
QK_NOPE = 64
QK_ROPE = 32
V_HEAD = 64
Q_LORA = 384
KV_LORA = 256
D_FF = 2816
FFN_CONV_WIDTH = 3
CHUNK_SHIFT = 6
N_META = 16
ROPE_THETA = 10000.0
EPS = 1e-6
NEG = -1e30
ADAM_LR = 0.001
ADAM_B1 = 0.9
ADAM_B2 = 0.999
ADAM_EPS = 1e-08
ADAM_WD = 0.01
ADAM_STEP = 10

LANES = 128
SUBLANES = 8
HB = LANES
D_HEADS = N_HEADS * HB
TM = 256
DEAD = TM - N_META
D_AG = 2 * D_CONV
D_ZP = D_AG + Q_LORA + KV_LORA + HB
D_MIX = D_CONV + D_HEADS
LN2 = 0.6931471805599453
Q_SCALE = (QK_NOPE + QK_ROPE) ** -0.5 / LN2
HALO_CONV = 32
HALO_FFN = 16
FF_CHUNK = 256
FF_TILE = D_FF // 2
VMEM_LIMIT = 56 * 1024 * 1024
N_CHIPS = 4
HEAD_GROUP = 4
N_GROUPS = N_HEADS // HEAD_GROUP
MESH =pl.DeviceIdType.MESH


def _params(n_grid):
    return pltpu.CompilerParams(dimension_semantics=("arbitrary",) * n_grid, vmem_limit_bytes=VMEM_LIMIT)


def _rows(tm, c, off=0):
    return pl.BlockSpec((tm, c), lambda i: (i, off))


def _full(shape):
    return pl.BlockSpec(shape, lambda i: (0,) * len(shape))


def _prev(hb, c, tm, off=0):
    return pl.BlockSpec((hb, c), lambda i: (jnp.maximum(i * (tm // hb) - 1, 0), off))


def _next(hb, c, tm, nblk, off=0):
    return pl.BlockSpec((hb, c), lambda i: (jnp.minimum((i + 1) * (tm // hb), nblk - 1), off))


def _sds(shape, dtype):
    return jax.ShapeDtypeStruct(shape, dtype)


def _rms_r(x, n):
    return lax.rsqrt(jnp.sum(x * x, -1, keepdims=True) * (1.0 / n) + EPS)


def _rms_bwd(dy, x, r, g, n):
    gd = dy * g
    dx = r * gd - x * (r * r * r) * (jnp.sum(x * gd, -1, keepdims=True) * (1.0 / n))
    return dx, jnp.sum(dy * x * r, 0, keepdims=True)


def _dot(a, b, dims):
    return lax.dot_general(a, b, (dims, ((), ())), preferred_element_type=F32)


NN = ((1,), (0,))
NT = ((1,), (1,))
TN = ((0,), (0,))


def _rope(x, c, s1, s2):
    n = x.shape[-1]
    return x * c + pltpu.roll(x, n - QK_ROPE // 2, 1) * s1 + pltpu.roll(x, QK_ROPE // 2, 1) * s2


def _rope_bwd(g, c, s1, s2):
    n = g.shape[-1]
    return g * c + pltpu.roll(g * s1, QK_ROPE // 2, 1) + pltpu.roll(g * s2, n - QK_ROPE // 2, 1)


def _row_ids(tm, cols=1):
    return pl.program_id(0) * tm + lax.broadcasted_iota(jnp.int32, (tm, cols), 0)


def _mm(a, b, dims, out_dtype, tm, tn, tk, name, by_col_tile=False):
    if dims == TN:
        (kk, m), (_, n) = a.shape, b.shape
        a_spec = pl.BlockSpec((tk, tm), lambda i, j, k: (k, i))
    else:
        m, kk = a.shape
        a_spec = pl.BlockSpec((tm, tk), lambda i, j, k: (i, k))
    if dims == NT:
        n = b.shape[0]
        b_spec = pl.BlockSpec((tn, tk), lambda i, j, k: (j, k))
    else:
        n = b.shape[1]
        b_spec = pl.BlockSpec((tk, tn), lambda i, j, k: (k, j))
    assert m % tm == 0 and n % tn == 0 and kk % tk == 0, (name, a.shape, b.shape, tm, tn, tk)
    nk = kk // tk

    def body(a_ref, b_ref, o_ref, acc_ref):
        k = pl.program_id(2)

        @pl.when(k == 0)
        def _():
            acc_ref[...] = jnp.zeros_like(acc_ref)

        acc_ref[...] += _dot(a_ref[...].astype(BF16), b_ref[...].astype(BF16), dims)

        @pl.when(k == nk - 1)
        def _():
            o_ref[...] = acc_ref[...].astype(out_dtype)

    if by_col_tile:
        out_spec, out_shape = pl.BlockSpec((None, tm, tn), lambda i, j, k: (j, i, 0)), (n // tn, m, tn)
    else:
        out_spec, out_shape = pl.BlockSpec((tm, tn), lambda i, j, k: (i, j)), (m, n)
    return pl.pallas_call(
        body, name=name, grid=(m // tm, n // tn, nk), in_specs=[a_spec, b_spec], out_specs=out_spec,
        out_shape=_sds(out_shape, out_dtype), scratch_shapes=[pltpu.VMEM((tm, tn), F32)],
        compiler_params=_params(3))(a, b)


def _fwd_in(h0, gmix, win, gq, wuq, gkv, wukv, rc, rs1, rs2):
    L = h0.shape[0]

    def body(h_ref, gmix_ref, win_ref, gq_ref, wuq_ref, gkv_ref, wukv_ref, c_ref, s1_ref, s2_ref,
             n_ref, zag_ref, u0_ref, cq_ref, ckv_ref, qn_ref, kvn_ref, q_ref, kv_ref, kr_ref, kvt_ref):
        h = h_ref[...]
        n = (h * _rms_r(h, D_MODEL) * gmix_ref[...]).astype(BF16)
        n_ref[...] = n
        z = _dot(n, win_ref[...], NN)
        a, gate = z[:, :D_CONV], z[:, D_CONV:D_AG]
        zag_ref[...] = z[:, :D_AG].astype(BF16)
        u0_ref[...] = a * jax.nn.sigmoid(gate)
        cq = z[:, D_AG:D_AG + Q_LORA]
        ckv = z[:, D_AG + Q_LORA:D_AG + Q_LORA + KV_LORA]
        krp = z[:, D_AG + Q_LORA + KV_LORA:]
        cq_ref[...] = cq
        ckv_ref[...] = ckv
        qn = (cq * _rms_r(cq, Q_LORA) * gq_ref[...]).astype(BF16)
        qn_ref[...] = qn
        kvn = (ckv * _rms_r(ckv, KV_LORA) * gkv_ref[...]).astype(BF16)
        kvn_ref[...] = kvn
        c, s1, s2 = c_ref[...], s1_ref[...], s2_ref[...]
        q = _dot(qn, wuq_ref[...], NN)
        q = _rope(q, jnp.tile(c, (1, N_HEADS)), jnp.tile(s1, (1, N_HEADS)), jnp.tile(s2, (1, N_HEADS)))
        q_ref[...] = (q * Q_SCALE).astype(BF16)
        kv = _dot(kvn, wukv_ref[...], NN)
        kv_ref[...] = kv.astype(BF16)
        kvt_ref[...] = kv.T.astype(BF16)
        kr_ref[...] = _rope(krp, c, s1, s2).astype(BF16)

    outs = [(D_MODEL, BF16), (D_AG, BF16), (D_CONV, F32), (Q_LORA, F32), (KV_LORA, F32), (Q_LORA, BF16),
            (KV_LORA, BF16), (D_HEADS, BF16), (D_HEADS, BF16), (HB, BF16)]
    return pl.pallas_call(
        body, name="fwd_in", grid=(L // TM,),
        in_specs=[_rows(TM, D_MODEL), _full(gmix.shape), _full(win.shape), _full(gq.shape), _full(wuq.shape),
                  _full(gkv.shape), _full(wukv.shape), _rows(TM, HB), _rows(TM, HB), _rows(TM, HB)],
        out_specs=[_rows(TM, c) for c, _ in outs] + [pl.BlockSpec((D_HEADS, TM), lambda i: (0, i))],
        out_shape=[_sds((L, c), d) for c, d in outs] + [_sds((D_HEADS, L), BF16)],
        compiler_params=_params(1))(h0, gmix, win, gq, wuq, gkv, wukv, rc, rs1, rs2)


def _conv_taps(xx, w_ref, halo, tm, flip):
    kw = w_ref.shape[0]
    acc, rolls = None, {}
    for k in range(kw):
        term = w_ref[k:k + 1, :] * _shifted_rows(xx, kw - 1 - k, halo, tm, flip, rolls)
        acc = term if acc is None else acc + term
    return acc


def _shifted_rows(xx, d, halo, tm, flip, rolls):
    a, b = divmod(d, SUBLANES)
    if b not in rolls:
        rolls[b] = xx if b == 0 else pltpu.roll(xx, (xx.shape[0] - b) if flip else b, 0)
    start = SUBLANES * a if flip else halo - SUBLANES * a
    return rolls[b][start:start + tm]


def _ln_silu(u1, lg, lb):
    mu = jnp.mean(u1, -1, keepdims=True)
    xc = u1 - mu
    rs = lax.rsqrt(jnp.mean(xc * xc, -1, keepdims=True) + EPS)
    xh = xc * rs
    u2 = xh * lg + lb
    sg = jax.nn.sigmoid(u2)
    return rs, xh, u2, sg, u2 * sg


def _fwd_conv(u0, cw, cb, lg, lb, og):
    L = u0.shape[0]

    def body(u0_ref, u0p_ref, cw_ref, cb_ref, lg_ref, lb_ref, og_ref, u1_ref, mixa_ref):
        halo = jnp.where(pl.program_id(0) > 0, u0p_ref[...], 0.0)
        xx = jnp.concatenate([halo, u0_ref[...]], 0)
        u1 = _conv_taps(xx, cw_ref, HALO_CONV, TM, False) + cb_ref[...]
        u1_ref[...] = u1
        u = _ln_silu(u1, lg_ref[...], lb_ref[...])[4]
        mixa_ref[...] = (u * _rms_r(u, D_CONV) * og_ref[...]).astype(BF16)

    return pl.pallas_call(
        body, name="fwd_conv", grid=(L // TM,),
        in_specs=[_rows(TM, D_CONV), _prev(HALO_CONV, D_CONV, TM), _full(cw.shape), _full(cb.shape),
                  _full(lg.shape), _full(lb.shape), _full(og.shape)],
        out_specs=[_rows(TM, D_CONV), _rows(TM, D_CONV)],
        out_shape=[_sds((L, D_CONV), F32), _sds((L, D_CONV), BF16)],
        compiler_params=_params(1))(u0, u0, cw, cb, lg, lb, og)


def _visible(i, j, t):
    row = i * t + lax.broadcasted_iota(jnp.int32, (t, t), 0)
    col = j * t + lax.broadcasted_iota(jnp.int32, (t, t), 1)
    return (lax.shift_right_logical(col, CHUNK_SHIFT) <= lax.shift_right_logical(row, CHUNK_SHIFT)) & (col >= DEAD)


def _visible_t(i, j, t):
    key = j * t + lax.broadcasted_iota(jnp.int32, (t, t), 0)
    query = i * t + lax.broadcasted_iota(jnp.int32, (t, t), 1)
    return (lax.shift_right_logical(key, CHUNK_SHIFT) <= lax.shift_right_logical(query, CHUNK_SHIFT)) & (key >= DEAD)


def _stat_lane(h):
    return (h // HEAD_GROUP) * LANES + h % HEAD_GROUP


def _scatter_stats(cols, t):
    lane = lax.broadcasted_iota(jnp.int32, (t, N_GROUPS * LANES), 1)
    out = jnp.zeros((t, N_GROUPS * LANES), F32)
    for h, col in enumerate(cols):
        out = jnp.where(lane == _stat_lane(h), col, out)
    return out


def _resident(shape, index_map):
    return pl.BlockSpec(shape, index_map, pipeline_mode=pl.Buffered(1))


def _attn_fwd(q, kv, kvt, kr, shards=()):
    L = q.shape[0]
    t = TM
    nq = L // t
    n = len(shards)
    pass_step = (3 * nq) // 4

    def body(q_ref, kv_ref, kvt_ref, kr_ref, *refs):
        gather_refs = refs[:n] + refs[n + 2:2 * n + 2] + refs[2 * n + 6:]
        o_ref, lse_ref = refs[n:n + 2]
        qt_scr, m_scr, l_scr, acc_scr = refs[2 * n + 2:2 * n + 6]
        i = pl.program_id(0)
        if n:
            @pl.when(i == 0)
            def _():
                for cp in _gather_copies(gather_refs[:n], gather_refs[n:2 * n], *gather_refs[2 * n:])[0]:
                    cp.start()

        lane = lax.broadcasted_iota(jnp.int32, (t, HB), 1)
        heads = range(N_HEADS)
        cols = [slice(h * HB, (h + 1) * HB) for h in heads]
        for h in heads:
            qt_scr[cols[h], :] = q_ref[:, cols[h]].astype(F32).T.astype(BF16)
        m_scr[...] = jnp.full_like(m_scr, NEG)
        l_scr[...] = jnp.zeros_like(l_scr)
        acc_scr[...] = jnp.zeros_like(acc_scr)

        def tile(j, vis):
            keys = pl.ds(pl.multiple_of(j * t, t), t)
            krj = kr_ref[keys, :]
            s = [_dot(jnp.where(lane < QK_NOPE, kv_ref[keys, cols[h]], krj), qt_scr[cols[h], :], NN) for h in heads]
            p, alpha = [], []
            for h in heads:
                sh = s[h] if vis is None else jnp.where(vis, s[h], NEG)
                m_prev = m_scr[h:h + 1, :]
                m_new = jnp.maximum(m_prev, jnp.max(sh, 0, keepdims=True))
                a = jnp.exp2(m_prev - m_new)
                ph = jnp.exp2(sh - m_new)
                l_scr[h:h + 1, :] = a * l_scr[h:h + 1, :] + jnp.sum(ph, 0, keepdims=True)
                m_scr[h:h + 1, :] = m_new
                p.append(ph.astype(BF16))
                alpha.append(a)
            for h in heads:
                acc_scr[cols[h], :] = alpha[h] * acc_scr[cols[h], :] + _dot(kvt_ref[cols[h], keys], p[h], NN)

        tile(i, _visible_t(i, i, t))

        @pl.when(i > 0)
        def _():
            tile(0, _visible_t(i, 0, t))

        def unmasked(j, carry):
            tile(j, None)
            return carry

        lax.fori_loop(1, i, unmasked, 0)
        lse_ref[...] = jnp.zeros_like(lse_ref)
        for h in heads:
            l = l_scr[h:h + 1, :]
            o_ref[:, cols[h]] = jnp.where(lane >= QK_NOPE, (acc_scr[cols[h], :] / l).T, 0.0).astype(BF16)
            lse_ref[h // HEAD_GROUP, h % HEAD_GROUP:h % HEAD_GROUP + 1, :] = m_scr[h:h + 1, :] + jnp.log2(l)
        if n:
            @pl.when(i == pass_step)
            def _():
                _, arrivals, forwards, _ = _gather_copies(gather_refs[:n], gather_refs[n:2 * n], *gather_refs[2 * n:])
                for landed, onward in zip(arrivals, forwards):
                    landed.wait_recv()
                    onward.start()

            @pl.when(i == nq - 1)
            def _():
                sends, _, forwards, finals = _gather_copies(gather_refs[:n], gather_refs[n:2 * n], *gather_refs[2 * n:])
                for cp in finals:
                    cp.wait_recv()
                for cp in sends + forwards:
                    cp.wait_send()

    any_spec = pl.BlockSpec(memory_space=pl.ANY)
    outs = pl.pallas_call(
        body, name="attn_fwd", grid=(nq,),
        in_specs=[_rows(t, D_HEADS), _resident((L, D_HEADS), lambda i: (0, 0)),
                  _resident((D_HEADS, L), lambda i: (0, 0)), _resident((L, HB), lambda i: (0, 0))] + [any_spec] * n,
        out_specs=[_rows(t, D_HEADS), pl.BlockSpec((N_GROUPS, SUBLANES, t), lambda i: (0, 0, i))] + [any_spec] * n,
        out_shape=[_sds((L, D_HEADS), BF16), _sds((N_GROUPS, SUBLANES, L), F32)] + _gather_out_shapes(shards),
        scratch_shapes=[pltpu.VMEM((D_HEADS, t), BF16), pltpu.VMEM((N_HEADS, t), F32), pltpu.VMEM((N_HEADS, t), F32),
                        pltpu.VMEM((D_HEADS, t), F32)] + (_gather_semaphores(n) if n else []),
        compiler_params=_params(1))(q, kv, kvt, kr, *shards)
    return outs[0], outs[1], _gathered(outs[2:], shards)


def _fwd_mix(h0, mixa, o, ga, wout, gffn):
    L = h0.shape[0]

    def body(h0_ref, mixa_ref, o_ref, ga_ref, wout_ref, gffn_ref, mix_ref, h1_ref, n2_ref):
        of = o_ref[...].astype(F32)
        mixb = (of * _rms_r(of, N_HEADS * V_HEAD) * ga_ref[...]).astype(BF16)
        mix = jnp.concatenate([mixa_ref[...], mixb], 1)
        mix_ref[...] = mix
        mo = jnp.where(_row_ids(TM) >= DEAD, _dot(mix, wout_ref[...], NN), 0.0)
        h1 = h0_ref[...] + mo
        h1_ref[...] = h1
        n2_ref[...] = (h1 * _rms_r(h1, D_MODEL) * gffn_ref[...]).astype(BF16)

    return pl.pallas_call(
        body, name="fwd_mix", grid=(L // TM,),
        in_specs=[_rows(TM, D_MODEL), _rows(TM, D_CONV), _rows(TM, D_HEADS), _full(ga.shape), _full(wout.shape),
                  _full(gffn.shape)],
        out_specs=[_rows(TM, D_MIX), _rows(TM, D_MODEL), _rows(TM, D_MODEL)],
        out_shape=[_sds((L, D_MIX), BF16), _sds((L, D_MODEL), F32), _sds((L, D_MODEL), BF16)],
        compiler_params=_params(1))(h0, mixa, o, ga, wout, gffn)


def _ffn_act_chunk(c, upg_ref, upv_ref, hg, hv, fcw_ref, fcb_ref):
    cs = slice(c * FF_CHUNK, (c + 1) * FF_CHUNK)
    out = []
    for part, (up_ref, halo) in enumerate(((upg_ref, hg), (upv_ref, hv))):
        xx = jnp.concatenate([halo[:, cs], up_ref[:, cs].astype(F32)], 0)
        ws = slice(part * D_FF + c * FF_CHUNK, part * D_FF + (c + 1) * FF_CHUNK)
        y = (fcw_ref[0:1, ws] * pltpu.roll(xx, 2, 0)[HALO_FFN:] + fcw_ref[1:2, ws] * pltpu.roll(xx, 1, 0)[HALO_FFN:]
             + fcw_ref[2:3, ws] * xx[HALO_FFN:] + fcb_ref[:, ws])
        out.append(y)
    return out


def _ffn_in_specs(L):
    return [_rows(TM, D_FF, 0), _rows(TM, D_FF, 1), _prev(HALO_FFN, D_FF, TM, 0), _prev(HALO_FFN, D_FF, TM, 1)]


def _ffn_halos(hg_ref, hv_ref):
    first = pl.program_id(0) == 0
    return (jnp.where(first, 0.0, hg_ref[...].astype(F32)), jnp.where(first, 0.0, hv_ref[...].astype(F32)))


def _fwd_ffn_loss(up0, fcw, fcb, wdown, h1, target, gfin):
    L = h1.shape[0]

    def body(upg_ref, upv_ref, hg_ref, hv_ref, fcw_ref, fcb_ref, wd_ref, h1_ref, t_ref, gf_ref,
             dh2_ref, loss_ref, dgf_ref, up_ref, act_ref):
        i = pl.program_id(0)
        hg, hv = _ffn_halos(hg_ref, hv_ref)
        for c in range(D_FF // FF_CHUNK):
            cs = slice(c * FF_CHUNK, (c + 1) * FF_CHUNK)
            g, val = _ffn_act_chunk(c, upg_ref, upv_ref, hg, hv, fcw_ref, fcb_ref)
            up_ref[:, cs] = g.astype(BF16)
            up_ref[:, D_FF + c * FF_CHUNK:D_FF + (c + 1) * FF_CHUNK] = val.astype(BF16)
            act_ref[:, cs] = (g * jax.nn.sigmoid(g) * val).astype(BF16)
        h2 = h1_ref[...] + _dot(act_ref[...], wd_ref[...], NN)
        r = _rms_r(h2, D_MODEL)
        gf = gf_ref[...]
        err = jnp.where(i > 0, h2 * r * gf - t_ref[...], 0.0)
        dy = err * (1.0 / D_MODEL)
        dh2, dgf = _rms_bwd(dy, h2, r, gf, D_MODEL)
        dh2_ref[...] = dh2

        @pl.when(i == 0)
        def _():
            loss_ref[...] = jnp.zeros_like(loss_ref)
            dgf_ref[...] = jnp.zeros_like(dgf_ref)

        loss_ref[...] += jnp.sum(err * err) * (0.5 / D_MODEL)
        dgf_ref[...] += dgf

    return pl.pallas_call(
        body, name="fwd_ffn_loss", grid=(L // TM,),
        in_specs=_ffn_in_specs(L) + [_full(fcw.shape), _full(fcb.shape), _full(wdown.shape), _rows(TM, D_MODEL),
                                     pl.BlockSpec((TM, D_MODEL), lambda i: (jnp.maximum(i - 1, 0), 0)),
                                     _full(gfin.shape)],
        out_specs=[_rows(TM, D_MODEL), _full((1, LANES)), _full((1, D_MODEL)), _rows(TM, 2 * D_FF), _rows(TM, D_FF)],
        out_shape=[_sds((L, D_MODEL), F32), _sds((1, LANES), F32), _sds((1, D_MODEL), F32),
                   _sds((L, 2 * D_FF), BF16), _sds((L, D_FF), BF16)],
        compiler_params=_params(1))(up0, up0, up0, up0, fcw, fcb, wdown, h1, target, gfin)


def _bwd_ffn_act(dh2, up, wdown):
    L = dh2.shape[0]

    def body(dh2_ref, upg_ref, upv_ref, wd_ref, dup_ref, dfcb_ref):
        da = _dot(dh2_ref[...].astype(BF16), wd_ref[...], NT)

        @pl.when(pl.program_id(0) == 0)
        def _():
            dfcb_ref[...] = jnp.zeros_like(dfcb_ref)

        for c in range(D_FF // FF_CHUNK):
            cs = slice(c * FF_CHUNK, (c + 1) * FF_CHUNK)
            vs = slice(D_FF + c * FF_CHUNK, D_FF + (c + 1) * FF_CHUNK)
            g, val = upg_ref[:, cs].astype(F32), upv_ref[:, cs].astype(F32)
            sg = jax.nn.sigmoid(g)
            si = g * sg
            dac = da[:, cs]
            dg = dac * val * (sg * (1.0 + g * (1.0 - sg)))
            dv = dac * si
            dup_ref[:, cs] = dg.astype(BF16)
            dup_ref[:, vs] = dv.astype(BF16)
            dfcb_ref[:, cs] += jnp.sum(dg, 0, keepdims=True)
            dfcb_ref[:, vs] += jnp.sum(dv, 0, keepdims=True)

    return pl.pallas_call(
        body, name="bwd_ffn_act", grid=(L // TM,),
        in_specs=[_rows(TM, D_MODEL), _rows(TM, D_FF, 0), _rows(TM, D_FF, 1), _full(wdown.shape)],
        out_specs=[_rows(TM, 2 * D_FF), _full((1, 2 * D_FF))],
        out_shape=[_sds((L, 2 * D_FF), BF16), _sds((1, 2 * D_FF), F32)],
        compiler_params=_params(1))(dh2, up, up, wdown)


def _bwd_ffn_conv(dup, up0, fcw):
    L, C = dup.shape
    tc = FF_TILE
    nt = L // TM
    nhb = L // HALO_FFN

    def body(dy_ref, dyn_ref, x_ref, xp_ref, w_ref, dx_ref, dw_ref):
        i = pl.program_id(1)
        yy = jnp.concatenate([dy_ref[...].astype(F32), jnp.where(i < nt - 1, dyn_ref[...].astype(F32), 0.0)], 0)
        dx_ref[...] = _conv_taps(yy, w_ref, HALO_FFN, TM, True).astype(BF16)
        xx = jnp.concatenate([jnp.where(i > 0, xp_ref[...].astype(F32), 0.0), x_ref[...].astype(F32)], 0)
        dy = yy[:TM]

        @pl.when(i == 0)
        def _():
            dw_ref[...] = jnp.zeros_like(dw_ref)

        rolls = {}
        for k in range(FFN_CONV_WIDTH):
            xs = _shifted_rows(xx, FFN_CONV_WIDTH - 1 - k, HALO_FFN, TM, False, rolls)
            dw_ref[k:k + 1, :] += jnp.sum(dy * xs, 0, keepdims=True)

    col = lambda j, i: (i, j)
    return pl.pallas_call(
        body, name="bwd_ffn_conv", grid=(C // tc, nt),
        in_specs=[pl.BlockSpec((TM, tc), col),
                  pl.BlockSpec((HALO_FFN, tc), lambda j, i: (jnp.minimum((i + 1) * (TM // HALO_FFN), nhb - 1), j)),
                  pl.BlockSpec((TM, tc), col),
                  pl.BlockSpec((HALO_FFN, tc), lambda j, i: (jnp.maximum(i * (TM // HALO_FFN) - 1, 0), j)),
                  pl.BlockSpec((FFN_CONV_WIDTH, tc), lambda j, i: (0, j))],
        out_specs=[pl.BlockSpec((TM, tc), col), pl.BlockSpec((FFN_CONV_WIDTH, tc), lambda j, i: (0, j))],
        out_shape=[_sds((L, C), BF16), _sds((FFN_CONV_WIDTH, C), F32)],
        compiler_params=_params(2))(dup, dup, up0, up0, fcw)


def _bwd_mix(dh2, dn2, h1, gffn, wout, o, ga, u1, lg, lb, og):
    L = h1.shape[0]

    def body(dh2_ref, dn2_ref, h1_ref, gffn_ref, wout_ref, o_ref, ga_ref, u1_ref, lg_ref, lb_ref, og_ref,
             dh1_ref, dh1m_ref, do_ref, delta_ref, du1_ref, dgffn_ref, dga_ref, dog_ref, dlg_ref, dlb_ref):
        h1 = h1_ref[...]
        dn2x, dgffn = _rms_bwd(dn2_ref[...], h1, _rms_r(h1, D_MODEL), gffn_ref[...], D_MODEL)
        dh1 = dh2_ref[...] + dn2x
        dh1_ref[...] = dh1
        dh1m = jnp.where(_row_ids(TM) >= DEAD, dh1, 0.0).astype(BF16)
        dh1m_ref[...] = dh1m
        dmix = _dot(dh1m, wout_ref[...], NT)
        dma, dmb = dmix[:, :D_CONV], dmix[:, D_CONV:]
        of = o_ref[...].astype(F32)
        do, dga = _rms_bwd(dmb, of, _rms_r(of, N_HEADS * V_HEAD), ga_ref[...], N_HEADS * V_HEAD)
        do_ref[...] = do.astype(BF16)
        prod = do * of
        by_lane = _scatter_stats([jnp.sum(prod[:, h * HB:(h + 1) * HB], -1, keepdims=True) for h in range(N_HEADS)], TM)
        by_row = by_lane.T
        for grp in range(N_GROUPS):
            delta_ref[grp] = by_row[grp * LANES:grp * LANES + SUBLANES, :]
        lg = lg_ref[...]
        rs, xh, u2, sg, u = _ln_silu(u1_ref[...], lg, lb_ref[...])
        du, dog = _rms_bwd(dma, u, _rms_r(u, D_CONV), og_ref[...], D_CONV)
        du2 = du * (sg * (1.0 + u2 * (1.0 - sg)))
        dxh = du2 * lg
        du1_ref[...] = rs * (dxh - jnp.mean(dxh, -1, keepdims=True) - xh * jnp.mean(dxh * xh, -1, keepdims=True))

        @pl.when(pl.program_id(0) == 0)
        def _():
            for ref in (dgffn_ref, dga_ref, dog_ref, dlg_ref, dlb_ref):
                ref[...] = jnp.zeros_like(ref)

        dgffn_ref[...] += dgffn
        dga_ref[...] += dga
        dog_ref[...] += dog
        dlg_ref[...] += jnp.sum(du2 * xh, 0, keepdims=True)
        dlb_ref[...] += jnp.sum(du2, 0, keepdims=True)

    return pl.pallas_call(
        body, name="bwd_mix", grid=(L // TM,),
        in_specs=[_rows(TM, D_MODEL), _rows(TM, D_MODEL), _rows(TM, D_MODEL), _full(gffn.shape), _full(wout.shape),
                  _rows(TM, D_HEADS), _full(ga.shape), _rows(TM, D_CONV), _full(lg.shape), _full(lb.shape),
                  _full(og.shape)],
        out_specs=[_rows(TM, D_MODEL), _rows(TM, D_MODEL), _rows(TM, D_HEADS),
                   pl.BlockSpec((N_GROUPS, SUBLANES, TM), lambda i: (0, 0, i)),
                   _rows(TM, D_CONV), _full((1, D_MODEL)), _full((1, D_HEADS)), _full((1, D_CONV)),
                   _full((1, D_CONV)), _full((1, D_CONV))],
        out_shape=[_sds((L, D_MODEL), F32), _sds((L, D_MODEL), BF16), _sds((L, D_HEADS), BF16),
                   _sds((N_GROUPS, SUBLANES, L), F32),
                   _sds((L, D_CONV), F32), _sds((1, D_MODEL), F32), _sds((1, D_HEADS), F32), _sds((1, D_CONV), F32),
                   _sds((1, D_CONV), F32), _sds((1, D_CONV), F32)],
        compiler_params=_params(1))(dh2, dn2, h1, gffn, wout, o, ga, u1, lg, lb, og)


def _attn_bwd(q, kv, kr, do, lse, delta, parts=()):
    L = q.shape[0]
    t = TM
    nt = L // t
    gw = HEAD_GROUP * HB
    n = len(parts)

    def body(q_ref, kv_ref, kr_ref, do_ref, lse_ref, delta_ref, *refs):
        dq_ref, dkv_ref, dkr_ref = refs[n:n + 3]
        dqt_acc, dk_acc, dv_acc, kkt_scr = refs[2 * n + 3:2 * n + 7]
        exchange_refs = (refs[:n], refs[n + 3:2 * n + 3]) + refs[2 * n + 7:]
        g, j = pl.program_id(0), pl.program_id(1)
        if n:
            @pl.when((g == 0) & (j == 0))
            def _():
                for cp in _chip_copies(*exchange_refs)[0]:
                    cp.start()

        lane = lax.broadcasted_iota(jnp.int32, (t, HB), 1)

        @pl.when(j == 0)
        def _():
            dqt_acc[...] = jnp.zeros_like(dqt_acc)

        @pl.when((j == 0) & (g == 0))
        def _():
            dkr_ref[...] = jnp.zeros_like(dkr_ref)

        dk_acc[...] = jnp.zeros_like(dk_acc)
        dv_acc[...] = jnp.zeros_like(dv_acc)
        krj = kr_ref[...]
        heads = range(HEAD_GROUP)
        cols = [slice(h * HB, (h + 1) * HB) for h in heads]
        for hc in cols:
            kkt_scr[hc, :] = jnp.where(lane < QK_NOPE, kv_ref[:, hc], krj).astype(F32).T.astype(BF16)

        def tile(i, vis):
            qs = pl.ds(pl.multiple_of(i * t, t), t)
            kvj = [kv_ref[:, hc] for hc in cols]
            qi = [q_ref[qs, hc] for hc in cols]
            doi = [do_ref[qs, hc] for hc in cols]
            s = [_dot(jnp.where(lane < QK_NOPE, kvj[h], krj), qi[h].astype(F32).T.astype(BF16), NN) for h in heads]
            dp = [_dot(kvj[h], doi[h].astype(F32).T.astype(BF16), NN) for h in heads]
            p = []
            for h in heads:
                sh = s[h] if vis is None else jnp.where(vis, s[h], NEG)
                p.append(jnp.exp2(sh - lse_ref[h:h + 1, qs]))
            for h in heads:
                dv_acc[:, cols[h]] += _dot(p[h].astype(BF16), doi[h], NN)
            ds = [(p[h] * (dp[h] - delta_ref[h:h + 1, qs]) * LN2).astype(BF16) for h in heads]
            for h in heads:
                dk_acc[:, cols[h]] += _dot(ds[h], qi[h], NN)
            for h in heads:
                dqt_acc[cols[h], qs] += _dot(kkt_scr[cols[h], :], ds[h], NN)

        tile(j, _visible_t(j, j, t))

        @pl.when(j == 0)
        def _():
            def masked(i, carry):
                tile(i, _visible_t(i, 0, t))
                return carry

            lax.fori_loop(1, nt, masked, 0)

        @pl.when(j > 0)
        def _():
            def unmasked(i, carry):
                tile(i, None)
                return carry

            lax.fori_loop(j + 1, nt, unmasked, 0)

        dkr = jnp.zeros((t, HB), F32)
        for h in range(HEAD_GROUP):
            hc = slice(h * HB, (h + 1) * HB)
            dk = dk_acc[:, hc]
            dkv_ref[:, hc] = jnp.where(lane < QK_NOPE, dk, dv_acc[:, hc]).astype(BF16)
            dkr = dkr + jnp.where(lane >= QK_NOPE, dk, 0.0)
        dkr_ref[pl.ds(pl.multiple_of(j * t, t), t), :] += dkr

        @pl.when(j == nt - 1)
        def _():
            def untranspose(i, carry):
                qs = pl.ds(pl.multiple_of(i * t, t), t)
                dq_ref[qs, :] = (dqt_acc[:, qs].T * Q_SCALE).astype(BF16)
                return carry

            lax.fori_loop(0, nt, untranspose, 0)

        if n:
            @pl.when((g == N_GROUPS - 1) & (j == nt - 1))
            def _():
                sends, arrivals = _chip_copies(*exchange_refs)
                for cp in arrivals:
                    cp.wait_recv()
                for cp in sends:
                    cp.wait_send()

    group = lambda g, j: (0, g)
    stats = _resident((None, SUBLANES, L), lambda g, j: (g, 0, 0))
    any_spec = pl.BlockSpec(memory_space=pl.ANY)
    outs = pl.pallas_call(
        body, name="attn_bwd", grid=(N_GROUPS, nt),
        in_specs=[_resident((L, gw), group), pl.BlockSpec((t, gw), lambda g, j: (j, g)),
                  pl.BlockSpec((t, HB), lambda g, j: (j, 0)), _resident((L, gw), group), stats, stats]
        + [any_spec] * n,
        out_specs=[pl.BlockSpec((L, gw), group), pl.BlockSpec((t, gw), lambda g, j: (j, g)),
                   pl.BlockSpec((L, HB), lambda g, j: (0, 0))] + [any_spec] * n,
        out_shape=[_sds((L, D_HEADS), BF16), _sds((L, D_HEADS), BF16), _sds((L, HB), F32)]
        + [_sds(p.shape, p.dtype) for p in parts],
        scratch_shapes=[pltpu.VMEM((gw, L), F32), pltpu.VMEM((t, gw), F32), pltpu.VMEM((t, gw), F32),
                        pltpu.VMEM((gw, t), BF16)] + (_chip_semaphores(n) if n else []),
        compiler_params=_params(2))(q, kv, kr, do, lse, delta, *parts)
    return outs[0], outs[1], outs[2], list(outs[3:])


def _bwd_conv(du1, u0, cw, zag):
    L = du1.shape[0]
    nt = L // TM

    def body(dy_ref, dyn_ref, x_ref, xp_ref, cw_ref, zag_ref, dzag_ref, dcw_ref, dcb_ref):
        i = pl.program_id(0)
        dy = dy_ref[...]
        yy = jnp.concatenate([dy, jnp.where(i < nt - 1, dyn_ref[...], 0.0)], 0)
        du0 = _conv_taps(yy, cw_ref, HALO_CONV, TM, True)
        zag = zag_ref[...].astype(F32)
        a, sg = zag[:, :D_CONV], jax.nn.sigmoid(zag[:, D_CONV:])
        dzag_ref[...] = jnp.concatenate([du0 * sg, du0 * a * sg * (1.0 - sg)], 1).astype(BF16)
        xx = jnp.concatenate([jnp.where(i > 0, xp_ref[...], 0.0), x_ref[...]], 0)

        @pl.when(i == 0)
        def _():
            dcw_ref[...] = jnp.zeros_like(dcw_ref)
            dcb_ref[...] = jnp.zeros_like(dcb_ref)

        rolls = {}
        for k in range(CONV_WIDTH):
            xs = _shifted_rows(xx, CONV_WIDTH - 1 - k, HALO_CONV, TM, False, rolls)
            dcw_ref[k:k + 1, :] += jnp.sum(dy * xs, 0, keepdims=True)
        dcb_ref[...] += jnp.sum(dy, 0, keepdims=True)

    return pl.pallas_call(
        body, name="bwd_conv", grid=(nt,),
        in_specs=[_rows(TM, D_CONV), _next(HALO_CONV, D_CONV, TM, L // HALO_CONV), _rows(TM, D_CONV),
                  _prev(HALO_CONV, D_CONV, TM), _full(cw.shape), _rows(TM, D_AG)],
        out_specs=[_rows(TM, D_AG), _full(cw.shape), _full((1, D_CONV))],
        out_shape=[_sds((L, D_AG), BF16), _sds(cw.shape, F32), _sds((1, D_CONV), F32)],
        compiler_params=_params(1))(du1, du1, u0, u0, cw, zag)


def _bwd_in(dzag, dq, dkv, dkr, cq, ckv, gq, gkv, wuq, wukv, win, rc, rs1, rs2, h0, gmix, dh1):
    L = h0.shape[0]

    def body(dzag_ref, dq_ref, dkv_ref, dkr_ref, cq_ref, ckv_ref, gq_ref, gkv_ref, wuq_ref, wukv_ref, win_ref,
             c_ref, s1_ref, s2_ref, h0_ref, gmix_ref, dh1_ref,
             dz_ref, dqr_ref, gx_ref, dfirst_ref, dgq_ref, dgkv_ref, dgmix_ref):
        i = pl.program_id(0)
        c, s1, s2 = c_ref[...], s1_ref[...], s2_ref[...]
        dqr = _rope_bwd(dq_ref[...].astype(F32), jnp.tile(c, (1, N_HEADS)), jnp.tile(s1, (1, N_HEADS)),
                        jnp.tile(s2, (1, N_HEADS))).astype(BF16)
        dqr_ref[...] = dqr
        cq, ckv = cq_ref[...], ckv_ref[...]
        dcq, dgq = _rms_bwd(_dot(dqr, wuq_ref[...], NT), cq, _rms_r(cq, Q_LORA), gq_ref[...], Q_LORA)
        dckv, dgkv = _rms_bwd(_dot(dkv_ref[...], wukv_ref[...], NT), ckv, _rms_r(ckv, KV_LORA), gkv_ref[...], KV_LORA)
        dkrp = _rope_bwd(dkr_ref[...], c, s1, s2)
        dz = jnp.concatenate([dzag_ref[...], dcq.astype(BF16), dckv.astype(BF16), dkrp.astype(BF16)], 1)
        dz_ref[...] = dz
        h0 = h0_ref[...]
        dnx, dgmix = _rms_bwd(_dot(dz, win_ref[...], NT), h0, _rms_r(h0, D_MODEL), gmix_ref[...], D_MODEL)
        dh0 = dh1_ref[...] + dnx

        @pl.when(i == 0)
        def _():
            dfirst_ref[...] = dh0
            for ref in (dgq_ref, dgkv_ref, dgmix_ref):
                ref[...] = jnp.zeros_like(ref)

        @pl.when(i > 0)
        def _():
            gx_ref[...] = dh0

        dgq_ref[...] += dgq
        dgkv_ref[...] += dgkv
        dgmix_ref[...] += dgmix

    return pl.pallas_call(
        body, name="bwd_in", grid=(L // TM,),
        in_specs=[_rows(TM, D_AG), _rows(TM, D_HEADS), _rows(TM, D_HEADS), _rows(TM, HB), _rows(TM, Q_LORA),
                  _rows(TM, KV_LORA), _full(gq.shape), _full(gkv.shape), _full(wuq.shape), _full(wukv.shape),
                  _full(win.shape), _rows(TM, HB), _rows(TM, HB), _rows(TM, HB), _rows(TM, D_MODEL),
                  _full(gmix.shape), _rows(TM, D_MODEL)],
        out_specs=[_rows(TM, D_ZP), _rows(TM, D_HEADS),
                   pl.BlockSpec((TM, D_MODEL), lambda i: (jnp.maximum(i - 1, 0), 0)), _full((TM, D_MODEL)),
                   _full((1, Q_LORA)), _full((1, KV_LORA)), _full((1, D_MODEL))],
        out_shape=[_sds((L, D_ZP), BF16), _sds((L, D_HEADS), BF16), _sds((L - TM, D_MODEL), F32),
                   _sds((TM, D_MODEL), F32), _sds((1, Q_LORA), F32), _sds((1, KV_LORA), F32), _sds((1, D_MODEL), F32)],
        compiler_params=_params(1))(dzag, dq, dkv, dkr, cq, ckv, gq, gkv, wuq, wukv, win, rc, rs1, rs2, h0, gmix, dh1)


def _mesh_pos():
    return lax.axis_index("x"), lax.axis_index("y"), lax.axis_index("c")


def _all_gather(shards):
    n = len(shards)

    def body(*refs):
        sends, arrivals, forwards, finals = _gather_copies(refs[:n], refs[n:2 * n], *refs[2 * n:])
        for cp in sends:
            cp.start()
        for landed, onward in zip(arrivals, forwards):
            landed.wait_recv()
            onward.start()
        for cp in finals:
            cp.wait_recv()
        for cp in sends + forwards:
            cp.wait_send()

    any_spec = pl.BlockSpec(memory_space=pl.ANY)
    outs = pl.pallas_call(
        body, name="all_gather_weights", in_specs=[any_spec] * n, out_specs=[any_spec] * n,
        out_shape=_gather_out_shapes(shards), scratch_shapes=_gather_semaphores(n))(*shards)
    return _gathered(outs, shards)


def _gather_out_shapes(shards):
    return [_sds((2 * N_CHIPS, s.shape[0] // 2) + s.shape[1:], s.dtype) for s in shards]


def _gather_semaphores(n):
    return [pltpu.SemaphoreType.DMA((n, 6)), pltpu.SemaphoreType.DMA((n, 6))]


def _gathered(outs, shards):
    return [_own_slot(o.reshape((N_CHIPS, s.shape[0]) + s.shape[1:]), s[None]) for o, s in zip(outs, shards)]


def _gather_copies(ins, outs, send_sems, recv_sems):
    x, y, c = _mesh_pos()
    chips = [(1 - x, y), (x, 1 - y), (1 - x, 1 - y)]
    sends, arrivals, forwards, finals = [], [], [], []

    def copy(src, dst, a, k, to):
        return pltpu.make_async_remote_copy(src, dst, send_sems.at[a, k], recv_sems.at[a, k], device_id=to,
                                            device_id_type=MESH)

    for a, (src, out) in enumerate(zip(ins, outs)):
        m = out.shape[1]
        mine = src.at[pl.ds(pl.multiple_of(c * m, 16), m)]
        for k, chip in enumerate(chips):
            slot = 4 * chip[0] + 2 * chip[1]
            sends.append(copy(mine, out.at[4 * x + 2 * y + c], a, k, (*chip, c)))
            arrivals.append(copy(out.at[slot + c], out.at[slot + c], a, k, (*chip, c)))
            forwards.append(copy(out.at[slot + c], out.at[slot + c], a, 3 + k, (x, y, 1 - c)))
            finals.append(copy(out.at[slot + 1 - c], out.at[slot + 1 - c], a, 3 + k, (x, y, 1 - c)))
    return sends, arrivals, forwards, finals


def _own_slot(by_chip, own):
    me = 2 * lax.axis_index("x") + lax.axis_index("y")
    chip = lax.broadcasted_iota(jnp.int32, (N_CHIPS,) + (1,) * (by_chip.ndim - 1), 0)
    return jnp.where(chip == me, own, by_chip)


def _sibling_exchange(parts, name):
    n = len(parts)

    def body(*refs):
        ins, theirs = refs[:n], refs[n:2 * n]
        send_sems, recv_sems = refs[2 * n:]
        x, y, c = _mesh_pos()
        copies = []
        for a in range(n):
            h = parts[a].shape[1] // 2
            rows = pl.ds(pl.multiple_of((1 - c) * h, 16), h)
            copies += [pltpu.make_async_remote_copy(ins[a].at[q, rows], theirs[a].at[q], send_sems.at[a, q],
                                                    recv_sems.at[a, q], device_id=(x, y, 1 - c), device_id_type=MESH)
                       for q in range(N_CHIPS)]
        for cp in copies:
            cp.start()
        for cp in copies:
            cp.wait()

    any_spec = pl.BlockSpec(memory_space=pl.ANY)
    return pl.pallas_call(
        body, name=name, in_specs=[any_spec] * n, out_specs=[any_spec] * n,
        out_shape=[_sds((N_CHIPS, p.shape[1] // 2, p.shape[2]), p.dtype) for p in parts],
        scratch_shapes=[pltpu.SemaphoreType.DMA((n, N_CHIPS)), pltpu.SemaphoreType.DMA((n, N_CHIPS))])(*parts)


def _chip_exchange(parts, name):
    n = len(parts)

    def body(*refs):
        sends, arrivals = _chip_copies(refs[:n], refs[n:2 * n], *refs[2 * n:])
        for cp in sends:
            cp.start()
        for cp in arrivals:
            cp.wait_recv()
        for cp in sends:
            cp.wait_send()

    any_spec = pl.BlockSpec(memory_space=pl.ANY)
    return pl.pallas_call(
        body, name=name, in_specs=[any_spec] * n, out_specs=[any_spec] * n,
        out_shape=[_sds(p.shape, p.dtype) for p in parts], scratch_shapes=_chip_semaphores(n))(*parts)


def _chip_semaphores(n):
    return [pltpu.SemaphoreType.DMA((n, 3)), pltpu.SemaphoreType.DMA((n, 3))]


def _chip_copies(ins, outs, send_sems, recv_sems):
    x, y, c = _mesh_pos()
    me = 2 * x + y
    sends, arrivals = [], []
    for a, (src, out) in enumerate(zip(ins, outs)):
        for k, chip in enumerate([(1 - x, y), (x, 1 - y), (1 - x, 1 - y)]):
            slot = 2 * chip[0] + chip[1]
            sems = dict(send_sem=send_sems.at[a, k], recv_sem=recv_sems.at[a, k], device_id=(*chip, c),
                        device_id_type=MESH)
            sends.append(pltpu.make_async_remote_copy(src.at[slot], out.at[me], **sems))
            arrivals.append(pltpu.make_async_remote_copy(out.at[slot], out.at[slot], **sems))
    return sends, arrivals


def _sibling_gather(parts, name):
    n = len(parts)

    def body(*refs):
        ins, outs = refs[:n], refs[n:2 * n]
        send_sems, recv_sems = refs[2 * n:]
        x, y, c = _mesh_pos()
        copies = [pltpu.make_async_remote_copy(ins[a], outs[a], send_sems.at[a], recv_sems.at[a],
                                               device_id=(x, y, 1 - c), device_id_type=MESH) for a in range(n)]
        for cp in copies:
            cp.start()
        for cp in copies:
            cp.wait()

    any_spec = pl.BlockSpec(memory_space=pl.ANY)
    return pl.pallas_call(
        body, name=name, in_specs=[any_spec] * n, out_specs=[any_spec] * n,
        out_shape=[_sds(p.shape, p.dtype) for p in parts],
        scratch_shapes=[pltpu.SemaphoreType.DMA((n,)), pltpu.SemaphoreType.DMA((n,))])(*parts)


def _row_tile(rows, row_bytes, align, budget=1 << 20):
    best = None
    for t in range(align, rows + 1, align):
        if rows % t == 0 and t * row_bytes <= budget:
            best = t
    return best or rows


def _scalar(v):
    return jnp.reshape(v, (1,)).astype(jnp.int32)


def _add_pair(part, theirs, c, name):
    _, h, cols = theirs.shape
    tr = _row_tile(h, cols * 4, 16)
    nb = h // tr

    def body(c_ref, a_ref, b_ref, o_ref):
        o_ref[...] = (a_ref[...].astype(F32) + b_ref[...].astype(F32)).astype(o_ref.dtype)

    half = pl.BlockSpec((None, tr, cols), lambda q, i, c_ref: (q, i, 0))
    grid_spec = pltpu.PrefetchScalarGridSpec(
        num_scalar_prefetch=1, grid=(N_CHIPS, nb),
        in_specs=[pl.BlockSpec((None, tr, cols), lambda q, i, c_ref: (q, c_ref[0] * nb + i, 0)), half],
        out_specs=half)
    return pl.pallas_call(body, name=name, grid_spec=grid_spec, out_shape=_sds(theirs.shape, part.dtype),
                          compiler_params=_params(2))(_scalar(c), part, theirs)


def _add_chips(got, own, me, name):
    _, h, cols = got.shape
    tr = _row_tile(h, cols * 4 * N_CHIPS, 16, budget=1 << 21)

    def body(me_ref, got_ref, own_ref, o_ref):
        acc = None
        for q in range(N_CHIPS):
            term = jnp.where(me_ref[0] == q, own_ref[q], got_ref[q]).astype(F32)
            acc = term if acc is None else acc + term
        o_ref[...] = acc

    by_chip = pl.BlockSpec((N_CHIPS, tr, cols), lambda i, me_ref: (0, i, 0))
    grid_spec = pltpu.PrefetchScalarGridSpec(
        num_scalar_prefetch=1, grid=(h // tr,), in_specs=[by_chip, by_chip],
        out_specs=pl.BlockSpec((tr, cols), lambda i, me_ref: (i, 0)))
    return pl.pallas_call(body, name=name, grid_spec=grid_spec, out_shape=_sds((h, cols), F32),
                          compiler_params=_params(1))(_scalar(me), got, own)


def _reduce_begin(parts, tag):
    c = lax.axis_index("c")
    theirs = _sibling_exchange(parts, f"grad_sibling_exchange_{tag}")
    return [_add_pair(p, t, c, f"grad_add_pair_{tag}_{a}") for a, (p, t) in enumerate(zip(parts, theirs))]


def _reduce_end(pair, got, tag):
    x, y, c = _mesh_pos()
    half = [_add_chips(g, p, 2 * x + y, f"grad_add_chips_{tag}_{a}") for a, (g, p) in enumerate(zip(got, pair))]
    other = _sibling_gather(half, f"grad_sibling_gather_{tag}")
    return [jnp.where(c == 0, jnp.concatenate([h, o], 0), jnp.concatenate([o, h], 0)) for h, o in zip(half, other)]


def _adamw_math(w, g, m, v):
    m = ADAM_B1 * m + (1.0 - ADAM_B1) * g
    v = ADAM_B2 * v + (1.0 - ADAM_B2) * (g * g)
    m_hat = m / (1.0 - ADAM_B1 ** ADAM_STEP)
    v_hat = v / (1.0 - ADAM_B2 ** ADAM_STEP)
    return -ADAM_LR * (m_hat / (jnp.sqrt(v_hat) + ADAM_EPS) + ADAM_WD * w), m, v


def _adamw_big(w, g, m, v, name):
    r, c = w.shape
    tr = _row_tile(r, c * 4, 8, budget=1 << 19)

    def body(w_ref, g_ref, m_ref, v_ref, d_ref, mo_ref, vo_ref):
        d_ref[...], mo_ref[...], vo_ref[...] = _adamw_math(w_ref[...], g_ref[...], m_ref[...], v_ref[...])

    return pl.pallas_call(
        body, name=name, grid=(r // tr,), in_specs=[_rows(tr, c)] * 4, out_specs=[_rows(tr, c)] * 3,
        out_shape=[_sds((r, c), F32)] * 3, compiler_params=_params(1))(w, g, m, v)


def _adamw_small(ws, gs, ms, vs):
    n = len(ws)

    def body(*refs):
        for a in range(n):
            w_ref, g_ref, m_ref, v_ref = (refs[k * n + a] for k in range(4))
            d, m, v = _adamw_math(w_ref[...], g_ref[...], m_ref[...], v_ref[...])
            refs[4 * n + a][...] = d
            refs[5 * n + a][...] = m
            refs[6 * n + a][...] = v

    vm = pl.BlockSpec(memory_space=pltpu.VMEM)
    outs = pl.pallas_call(
        body, name="adamw_small", in_specs=[vm] * (4 * n), out_specs=[vm] * (3 * n),
        out_shape=[_sds(w.shape, F32) for w in ws] * 3)(*ws, *gs, *ms, *vs)
    return outs[:n], outs[n:2 * n], outs[2 * n:]


BIG = ("w_in", "w_uq", "w_ukv", "w_out", "w_ffn_up", "w_ffn_down")
SMALL_SHARDED = ("conv_w", "ffn_conv_w", "meta_tokens")
REPLICATED = ("mix_norm_g", "q_norm_g", "kv_norm_g", "conv_b", "conv_ln_g", "conv_ln_b", "conv_out_g", "attn_out_g",
              "ffn_norm_g", "ffn_conv_b", "final_norm_g")
WEIGHTS = ("meta_tokens", "mix_norm_g", "w_in", "q_norm_g", "w_uq", "kv_norm_g", "w_ukv", "conv_w", "conv_b",
           "conv_ln_g", "conv_ln_b", "conv_out_g", "attn_out_g", "w_out", "ffn_norm_g", "w_ffn_up", "ffn_conv_w",
           "ffn_conv_b", "w_ffn_down", "final_norm_g")


def _lane_rows(a):
    return a.reshape(N_CHIPS, -1, LANES)


def _col_shards(a):
    k = a.shape[0]
    return a.reshape(k, N_CHIPS, -1).transpose(1, 0, 2)


def _from_col_shards(a):
    return a.transpose(1, 0, 2).reshape(a.shape[1], -1)


def _pad_rows_to(a, rows):
    return jnp.pad(a, ((0, 0), (0, rows - a.shape[1]), (0, 0)))


def _rope_tables(L):
    pos = (jnp.arange(L, dtype=jnp.int32) - DEAD).astype(F32)
    inv_freq = 1.0 / (ROPE_THETA ** (jnp.arange(0, QK_ROPE, 2, dtype=F32) / QK_ROPE))
    ang = pos[:, None] * inv_freq[None, :]
    cos, sin = jnp.cos(ang), jnp.sin(ang)
    half = QK_ROPE // 2
    z = lambda n: jnp.zeros((L, n), F32)
    rc = jnp.concatenate([jnp.ones((L, QK_NOPE), F32), cos, cos, z(HB - QK_NOPE - QK_ROPE)], 1)
    rs1 = jnp.concatenate([z(QK_NOPE), -sin, z(HB - QK_NOPE - half)], 1)
    rs2 = jnp.concatenate([z(QK_NOPE + half), sin, z(HB - QK_NOPE - QK_ROPE)], 1)
    return rc, rs1, rs2


def _pad_heads(g):
    return jnp.pad(g.reshape(N_HEADS, V_HEAD), ((0, 0), (HB - V_HEAD, 0))).reshape(1, D_HEADS)


def _unpad_heads(g):
    return g.reshape(N_HEADS, HB)[:, HB - V_HEAD:].reshape(1, N_HEADS * V_HEAD)


def _local_step(x, target, w, late_shards=None, reduce_ffn=False):
    S = x.shape[0]
    L = TM + S
    tl = L // 4
    d_qk = QK_NOPE + QK_ROPE
    win_n = w["w_in"]
    kr0 = D_AG + Q_LORA + KV_LORA
    win = jnp.concatenate([win_n[:, :kr0], jnp.zeros((D_MODEL, QK_NOPE), BF16), win_n[:, kr0:],
                           jnp.zeros((D_MODEL, HB - d_qk), BF16)], 1)
    wuq = jnp.pad(w["w_uq"].reshape(Q_LORA, N_HEADS, d_qk), ((0, 0), (0, 0), (0, HB - d_qk))).reshape(Q_LORA, D_HEADS)
    wukv = w["w_ukv"]
    ga = _pad_heads(w["attn_out_g"])
    gfin = w["final_norm_g"].reshape(1, D_MODEL)
    rc, rs1, rs2 = _rope_tables(L)
    h0 = jnp.concatenate([jnp.zeros((DEAD, D_MODEL), F32), w["meta_tokens"], x], 0)

    n, zag, u0, cq, ckv, qn, kvn, q, kv, kr, kvt = _fwd_in(h0, w["mix_norm_g"], win, w["q_norm_g"], wuq,
                                                            w["kv_norm_g"], wukv, rc, rs1, rs2)
    u1, mixa = _fwd_conv(u0, w["conv_w"], w["conv_b"], w["conv_ln_g"], w["conv_ln_b"], w["conv_out_g"])
    if late_shards is None:
        o, lse, _ = _attn_fwd(q, kv, kvt, kr)
        wout_n, wup, wdown = (w[k] for k in LATE)
    else:
        o, lse, late = _attn_fwd(q, kv, kvt, kr, [late_shards[k] for k in LATE])
        wout_n, wup, wdown = (_full_weight(k, g) for k, g in zip(LATE, late))
    wout = jnp.concatenate([wout_n[:D_CONV], jnp.pad(wout_n[D_CONV:].reshape(N_HEADS, V_HEAD, D_MODEL),
                                                     ((0, 0), (HB - V_HEAD, 0), (0, 0))).reshape(D_HEADS, D_MODEL)], 0)
    mix, h1, n2 = _fwd_mix(h0, mixa, o, ga, wout, w["ffn_norm_g"])
    up0 = _mm(n2, wup, NN, BF16, tl, FF_TILE, D_MODEL, "ffn_up")
    dh2, loss, g_fin, up, act = _fwd_ffn_loss(up0, w["ffn_conv_w"], w["ffn_conv_b"], wdown, h1, target, gfin)

    dup, g_fcb = _bwd_ffn_act(dh2, up, wdown)
    dup0, g_fcw = _bwd_ffn_conv(dup, up0, w["ffn_conv_w"])
    dn2 = _mm(dup0, wup, NT, F32, tl, D_MODEL, FF_TILE, "ffn_up_dx")
    g_wup = _mm(n2, dup0, TN, BF16, D_MODEL, FF_TILE, tl, "ffn_up_dw", by_col_tile=True)
    g_wdown = _mm(act, dh2, TN, BF16, D_FF // 2, D_MODEL, tl, "ffn_down_dw").reshape(N_CHIPS, -1, D_MODEL)
    ffn_pair = _reduce_begin([g_wup, g_wdown], "ffn") if reduce_ffn else None
    dh1, dh1m, do, delta, du1, g_gffn, g_ga, g_og, g_lg, g_lb = _bwd_mix(
        dh2, dn2, h1, w["ffn_norm_g"], wout, o, ga, u1, w["conv_ln_g"], w["conv_ln_b"], w["conv_out_g"])
    g_wout = _mm(mix, dh1m, TN, BF16, D_MIX // 2, D_MODEL, tl, "out_dw")
    dq, dkv, dkr, ffn_got = _attn_bwd(q, kv, kr, do, lse, delta, ffn_pair if reduce_ffn else ())
    if reduce_ffn:
        g_wup, g_wdown = _reduce_end(ffn_pair, ffn_got, "ffn")
    dzag, g_cw, g_cb = _bwd_conv(du1, u0, w["conv_w"], zag)
    dz, dqr, gx, dfirst, g_gq, g_gkv, g_gmix = _bwd_in(dzag, dq, dkv, dkr, cq, ckv, w["q_norm_g"], w["kv_norm_g"],
                                                      wuq, wukv, win, rc, rs1, rs2, h0, w["mix_norm_g"], dh1)
    g_win = _mm(n, dz, TN, BF16, D_MODEL, D_ZP // 2, tl, "in_dw")
    g_wuq = _mm(qn, dqr, TN, BF16, Q_LORA, D_HEADS, tl, "uq_dw")
    g_wukv = _mm(kvn, dkv, TN, BF16, KV_LORA, D_HEADS // N_CHIPS, tl, "ukv_dw", by_col_tile=True)

    grads = {
        "w_in": _col_shards(jnp.concatenate([g_win[:, :kr0], g_win[:, kr0 + QK_NOPE:kr0 + d_qk]], 1)),
        "w_uq": _col_shards(g_wuq.reshape(Q_LORA, N_HEADS, HB)[:, :, :d_qk].reshape(Q_LORA, N_HEADS * d_qk)),
        "w_ukv": g_wukv,
        "w_out": jnp.concatenate([g_wout[:D_CONV], g_wout[D_CONV:].reshape(N_HEADS, HB, D_MODEL)[:, HB - V_HEAD:]
                                  .reshape(N_HEADS * V_HEAD, D_MODEL)], 0).reshape(N_CHIPS, -1, D_MODEL),
        "w_ffn_up": g_wup, "w_ffn_down": g_wdown, "conv_w": g_cw, "ffn_conv_w": g_fcw,
        "meta_tokens": dfirst[DEAD:], "mix_norm_g": g_gmix, "q_norm_g": g_gq, "kv_norm_g": g_gkv, "conv_b": g_cb,
        "conv_ln_g": g_lg, "conv_ln_b": g_lb, "conv_out_g": g_og, "attn_out_g": _unpad_heads(g_ga),
        "ffn_norm_g": g_gffn, "ffn_conv_b": g_fcb, "final_norm_g": g_fin,
    }
    return loss, gx, grads


ROW_SHARDED = ("w_out", "w_ffn_down")


FFN = ("w_ffn_up", "w_ffn_down")
LATE = ("w_out", "w_ffn_up", "w_ffn_down")


def _full_weight(name, by_chip):
    return by_chip.reshape(-1, by_chip.shape[-1]) if name in ROW_SHARDED else _from_col_shards(by_chip)


def kernel(x, meta_tokens, mix_norm_g, w_in, q_norm_g, w_uq, kv_norm_g, w_ukv, conv_w, conv_b, conv_ln_g, conv_ln_b, conv_out_g, attn_out_g, w_out, ffn_norm_g, w_ffn_up, ffn_conv_w, ffn_conv_b, w_ffn_down, final_norm_g, loss_target, m_meta_tokens, m_mix_norm_g, m_w_in, m_q_norm_g, m_w_uq, m_kv_norm_g, m_w_ukv, m_conv_w, m_conv_b, m_conv_ln_g, m_conv_ln_b, m_conv_out_g, m_attn_out_g, m_w_out, m_ffn_norm_g, m_w_ffn_up, m_ffn_conv_w, m_ffn_conv_b, m_w_ffn_down, m_final_norm_g, v_meta_tokens, v_mix_norm_g, v_w_in, v_q_norm_g, v_w_uq, v_kv_norm_g, v_w_ukv, v_conv_w, v_conv_b, v_conv_ln_g, v_conv_ln_b, v_conv_out_g, v_attn_out_g, v_w_out, v_ffn_norm_g, v_w_ffn_up, v_ffn_conv_w, v_ffn_conv_b, v_w_ffn_down, v_final_norm_g):
    local = dict(meta_tokens=meta_tokens, mix_norm_g=mix_norm_g, w_in=w_in[0], q_norm_g=q_norm_g, w_uq=w_uq[0],
                 kv_norm_g=kv_norm_g, w_ukv=w_ukv[0], conv_w=conv_w[0], conv_b=conv_b, conv_ln_g=conv_ln_g,
                 conv_ln_b=conv_ln_b, conv_out_g=conv_out_g, attn_out_g=attn_out_g, w_out=w_out[0],
                 ffn_norm_g=ffn_norm_g, w_ffn_up=w_ffn_up[0], ffn_conv_w=ffn_conv_w[0], ffn_conv_b=ffn_conv_b,
                 w_ffn_down=w_ffn_down[0], final_norm_g=final_norm_g.reshape(1, D_MODEL))
    ms = dict(zip(WEIGHTS, (m_meta_tokens, m_mix_norm_g, m_w_in, m_q_norm_g, m_w_uq, m_kv_norm_g, m_w_ukv, m_conv_w,
                            m_conv_b, m_conv_ln_g, m_conv_ln_b, m_conv_out_g, m_attn_out_g, m_w_out, m_ffn_norm_g,
                            m_w_ffn_up, m_ffn_conv_w, m_ffn_conv_b, m_w_ffn_down, m_final_norm_g)))
    vs = dict(zip(WEIGHTS, (v_meta_tokens, v_mix_norm_g, v_w_in, v_q_norm_g, v_w_uq, v_kv_norm_g, v_w_ukv, v_conv_w,
                            v_conv_b, v_conv_ln_g, v_conv_ln_b, v_conv_out_g, v_attn_out_g, v_w_out, v_ffn_norm_g,
                            v_w_ffn_up, v_ffn_conv_w, v_ffn_conv_b, v_w_ffn_down, v_final_norm_g)))

    small_flat = jnp.concatenate([local[k].reshape(-1) for k in SMALL_SHARDED]).reshape(-1, LANES)
    early = [k for k in BIG if k not in LATE]
    gathered = _all_gather([local[k].astype(BF16) for k in early] + [small_flat])
    full = {k: v for k, v in local.items() if k not in LATE}
    for name, g in zip(early, gathered[:len(early)]):
        full[name] = _full_weight(name, g)
    small = gathered[-1].reshape(N_CHIPS, -1)
    at = 0
    for name in SMALL_SHARDED:
        r, c = local[name].shape
        full[name] = _from_col_shards(small[:, at:at + r * c].reshape(N_CHIPS, r, c))
        at += r * c

    loss_row, grad_x, grads = _local_step(x[0], loss_target[0], full, {k: local[k].astype(BF16) for k in LATE},
                                          reduce_ffn=True)

    rest_big = [k for k in BIG if k not in FFN]
    rep = jnp.concatenate([grads[k].reshape(-1) for k in REPLICATED] + [loss_row.reshape(-1)]).reshape(1, -1, LANES)
    small_pieces = [_lane_rows(_col_shards(grads[k])) for k in SMALL_SHARDED]
    small_pieces.append(jnp.broadcast_to(rep, (N_CHIPS,) + rep.shape[1:]))
    small_rows = sum(p.shape[1] for p in small_pieces)
    small_pack = _pad_rows_to(jnp.concatenate(small_pieces, 1), -(-small_rows // 32) * 32)
    pair = _reduce_begin([grads[k] for k in rest_big] + [small_pack], "rest")
    *rest_tot, small_tot = _reduce_end(pair, _chip_exchange(pair, "grad_chip_exchange_rest"), "rest")

    total = {k: grads[k] for k in FFN}
    total.update(zip(rest_big, rest_tot))
    flat = small_tot.reshape(-1)
    at = 0
    for name in SMALL_SHARDED + REPLICATED:
        shape = local[name].shape
        size = shape[0] * shape[1]
        total[name] = flat[at:at + size].reshape(shape)
        at += -(-size // LANES) * LANES if name in SMALL_SHARDED else size
    loss = flat[at]

    delta, new_m, new_v = {}, {}, {}
    shape2 = lambda a, name: a.reshape(local[name].shape)
    for name in BIG:
        delta[name], new_m[name], new_v[name] = _adamw_big(local[name], total[name], shape2(ms[name], name),
                                                          shape2(vs[name], name), "adamw_" + name)
    rest = SMALL_SHARDED + REPLICATED
    ds, nms, nvs = _adamw_small([local[k] for k in rest], [total[k] for k in rest],
                                [shape2(ms[k], k) for k in rest], [shape2(vs[k], k) for k in rest])
    for k, d, nm, nv in zip(rest, ds, nms, nvs):
        delta[k], new_m[k], new_v[k] = d, nm, nv

    out_shape = dict(zip(WEIGHTS, (meta_tokens, mix_norm_g, w_in, q_norm_g, w_uq, kv_norm_g, w_ukv, conv_w, conv_b,
                                   conv_ln_g, conv_ln_b, conv_out_g, attn_out_g, w_out, ffn_norm_g, w_ffn_up,
                                   ffn_conv_w, ffn_conv_b, w_ffn_down, final_norm_g)))
    outs = [loss, grad_x[None]]
    for group in (total, delta, new_m, new_v):
        outs += [group[k].reshape(out_shape[k].shape) for k in WEIGHTS]
    return tuple(outs)
```

```python
import functools

import jax
import jax.numpy as jnp
from jax import lax
from jax.experimental import pallas as pl
from jax.experimental.pallas import tpu as pltpu

F32 = jnp.float32
BF16 = jnp.bfloat16

D_MODEL = 1024
D_CONV = 512
CONV_WIDTH = 31
N_HEADS = 8
QK_NOPE = 64
QK_ROPE = 32
V_HEAD = 64
Q_LORA = 384
KV_LORA = 256
D_FF = 2816
FFN_CONV_WIDTH = 3
CHUNK_SHIFT = 6
N_META = 16
ROPE_THETA = 10000.0
EPS = 1e-6
NEG = -1e30
ADAM_LR = 0.001
ADAM_B1 = 0.9
ADAM_B2 = 0.999
ADAM_EPS = 1e-08
ADAM_WD = 0.01
ADAM_STEP = 10

LANES = 128
SUBLANES = 8
HB = LANES
D_HEADS = N_HEADS * HB
TM = 256
DEAD = TM - N_META
D_AG = 2 * D_CONV
D_ZP = D_AG + Q_LORA + KV_LORA + HB
D_MIX = D_CONV + D_HEADS
LN2 = 0.6931471805599453
Q_SCALE = (QK_NOPE + QK_ROPE) ** -0.5 / LN2
HALO_CONV = 32
HALO_FFN = 16
FF_CHUNK = 256
FF_MXU_CHUNK = 256
FF_TILE = D_FF // 2
VMEM_LIMIT = 56 * 1024 * 1024
N_CHIPS = 4
HEAD_GROUP = 4
N_GROUPS = N_HEADS // HEAD_GROUP
MESH =pl.DeviceIdType.MESH


def _params(n_grid):
    return pltpu.CompilerParams(dimension_semantics=("arbitrary",) * n_grid, vmem_limit_bytes=VMEM_LIMIT)


def _rows(tm, c, off=0):
    return pl.BlockSpec((tm, c), lambda i: (i, off))


def _full(shape):
    return pl.BlockSpec(shape, lambda i: (0,) * len(shape))


def _prev(hb, c, tm, off=0):
    return pl.BlockSpec((hb, c), lambda i: (jnp.maximum(i * (tm // hb) - 1, 0), off))


def _next(hb, c, tm, nblk, off=0):
    return pl.BlockSpec((hb, c), lambda i: (jnp.minimum((i + 1) * (tm // hb), nblk - 1), off))


def _sds(shape, dtype):
    return jax.ShapeDtypeStruct(shape, dtype)


def _rms_r(x, n):
    return lax.rsqrt(jnp.sum(x * x, -1, keepdims=True) * (1.0 / n) + EPS)


def _rms_bwd(dy, x, r, g, n):
    gd = dy * g
    dx = r * gd - x * (r * r * r) * (jnp.sum(x * gd, -1, keepdims=True) * (1.0 / n))
    return dx, jnp.sum(dy * x * r, 0, keepdims=True)


def _dot(a, b, dims):
    return lax.dot_general(a, b, (dims, ((), ())), preferred_element_type=F32)


NN = ((1,), (0,))
NT = ((1,), (1,))
TN = ((0,), (0,))


def _rope(x, c, s1, s2):
    n = x.shape[-1]
    return x * c + pltpu.roll(x, n - QK_ROPE // 2, 1) * s1 + pltpu.roll(x, QK_ROPE // 2, 1) * s2


def _rope_bwd(g, c, s1, s2):
    n = g.shape[-1]
    return g * c + pltpu.roll(g * s1, QK_ROPE // 2, 1) + pltpu.roll(g * s2, n - QK_ROPE // 2, 1)


def _row_ids(tm, cols=1):
    return pl.program_id(0) * tm + lax.broadcasted_iota(jnp.int32, (tm, cols), 0)


def _mm(a, b, dims, out_dtype, tm, tn, tk, name, by_col_tile=False, b_slabs=False):
    if dims == TN:
        (kk, m), (_, n) = a.shape, b.shape
        a_spec = pl.BlockSpec((tk, tm), lambda i, j, k: (k, i))
    else:
        m, kk = a.shape
        a_spec = pl.BlockSpec((tm, tk), lambda i, j, k: (i, k))
    if dims == NT and b_slabs:
        n = b.shape[1]
        assert b.shape[2] == tk and kk == b.shape[0] * tk, (name, b.shape)
        b_spec = pl.BlockSpec((None, tn, tk), lambda i, j, k: (k, j, 0))
    elif dims == NT:
        n = b.shape[0]
        b_spec = pl.BlockSpec((tn, tk), lambda i, j, k: (j, k))
    elif b_slabs:
        n = b.shape[0] * b.shape[2]
        assert b.shape[2] == tn and kk == b.shape[1], (name, b.shape)
        b_spec = pl.BlockSpec((None, tk, tn), lambda i, j, k: (j, k, 0))
    else:
        n = b.shape[1]
        b_spec = pl.BlockSpec((tk, tn), lambda i, j, k: (k, j))
    assert m % tm == 0 and n % tn == 0 and kk % tk == 0, (name, a.shape, b.shape, tm, tn, tk)
    nk = kk // tk

    def body(a_ref, b_ref, o_ref, acc_ref):
        k = pl.program_id(2)

        @pl.when(k == 0)
        def _():
            acc_ref[...] = jnp.zeros_like(acc_ref)

        acc_ref[...] += _dot(a_ref[...].astype(BF16), b_ref[...].astype(BF16), dims)

        @pl.when(k == nk - 1)
        def _():
            o_ref[...] = acc_ref[...].astype(out_dtype)

    if by_col_tile:
        out_spec, out_shape = pl.BlockSpec((None, tm, tn), lambda i, j, k: (j, i, 0)), (n // tn, m, tn)
    else:
        out_spec, out_shape = pl.BlockSpec((tm, tn), lambda i, j, k: (i, j)), (m, n)
    return pl.pallas_call(
        body, name=name, grid=(m // tm, n // tn, nk), in_specs=[a_spec, b_spec], out_specs=out_spec,
        out_shape=_sds(out_shape, out_dtype), scratch_shapes=[pltpu.VMEM((tm, tn), F32)],
        compiler_params=_params(3))(a, b)


def _fwd_in(h0, gmix, win, gq, wuq, gkv, wukv, rc, rs1, rs2):
    L = h0.shape[0]

    def body(h_ref, gmix_ref, win_ref, gq_ref, wuq_ref, gkv_ref, wukv_ref, c_ref, s1_ref, s2_ref,
             n_ref, zag_ref, u0_ref, cq_ref, ckv_ref, qn_ref, kvn_ref, q_ref, kv_ref, kr_ref, kvt_ref):
        h = h_ref[...]
        n = (h * _rms_r(h, D_MODEL) * gmix_ref[...]).astype(BF16)
        n_ref[...] = n
        z = _dot(n, win_ref[...], NN)
        a, gate = z[:, :D_CONV], z[:, D_CONV:D_AG]
        zag_ref[...] = z[:, :D_AG].astype(BF16)
        u0_ref[...] = a * jax.nn.sigmoid(gate)
        cq = z[:, D_AG:D_AG + Q_LORA]
        ckv = z[:, D_AG + Q_LORA:D_AG + Q_LORA + KV_LORA]
        krp = z[:, D_AG + Q_LORA + KV_LORA:]
        cq_ref[...] = cq
        ckv_ref[...] = ckv
        qn = (cq * _rms_r(cq, Q_LORA) * gq_ref[...]).astype(BF16)
        qn_ref[...] = qn
        kvn = (ckv * _rms_r(ckv, KV_LORA) * gkv_ref[...]).astype(BF16)
        kvn_ref[...] = kvn
        c, s1, s2 = c_ref[...], s1_ref[...], s2_ref[...]
        q = _dot(qn, wuq_ref[...], NN)
        q = _rope(q, jnp.tile(c, (1, N_HEADS)), jnp.tile(s1, (1, N_HEADS)), jnp.tile(s2, (1, N_HEADS)))
        q_ref[...] = (q * Q_SCALE).astype(BF16)
        kv = _dot(kvn, wukv_ref[...], NN)
        kv_ref[...] = kv.astype(BF16)
        kvt_ref[...] = kv.T.astype(BF16)
        kr_ref[...] = _rope(krp, c, s1, s2).astype(BF16)

    outs = [(D_MODEL, BF16), (D_AG, BF16), (D_CONV, F32), (Q_LORA, F32), (KV_LORA, F32), (Q_LORA, BF16),
            (KV_LORA, BF16), (D_HEADS, BF16), (D_HEADS, BF16), (HB, BF16)]
    return pl.pallas_call(
        body, name="fwd_in", grid=(L // TM,),
        in_specs=[_rows(TM, D_MODEL), _full(gmix.shape), _full(win.shape), _full(gq.shape), _full(wuq.shape),
                  _full(gkv.shape), _full(wukv.shape), _rows(TM, HB), _rows(TM, HB), _rows(TM, HB)],
        out_specs=[_rows(TM, c) for c, _ in outs] + [pl.BlockSpec((D_HEADS, TM), lambda i: (0, i))],
        out_shape=[_sds((L, c), d) for c, d in outs] + [_sds((D_HEADS, L), BF16)],
        compiler_params=_params(1))(h0, gmix, win, gq, wuq, gkv, wukv, rc, rs1, rs2)


def _conv_taps(xx, w_ref, halo, tm, flip):
    kw = w_ref.shape[0]
    acc, rolls = None, {}
    for k in range(kw):
        term = w_ref[k:k + 1, :] * _shifted_rows(xx, kw - 1 - k, halo, tm, flip, rolls)
        acc = term if acc is None else acc + term
    return acc


def _shifted_rows(xx, d, halo, tm, flip, rolls):
    a, b = divmod(d, SUBLANES)
    if b not in rolls:
        rolls[b] = xx if b == 0 else pltpu.roll(xx, (xx.shape[0] - b) if flip else b, 0)
    start = SUBLANES * a if flip else halo - SUBLANES * a
    return rolls[b][start:start + tm]


def _ln_silu(u1, lg, lb):
    mu = jnp.mean(u1, -1, keepdims=True)
    xc = u1 - mu
    rs = lax.rsqrt(jnp.mean(xc * xc, -1, keepdims=True) + EPS)
    xh = xc * rs
    u2 = xh * lg + lb
    sg = jax.nn.sigmoid(u2)
    return rs, xh, u2, sg, u2 * sg


def _fwd_conv(u0, cw, cb, lg, lb, og):
    L = u0.shape[0]

    def body(u0_ref, u0p_ref, cw_ref, cb_ref, lg_ref, lb_ref, og_ref, u1_ref, mixa_ref):
        halo = jnp.where(pl.program_id(0) > 0, u0p_ref[...], 0.0)
        xx = jnp.concatenate([halo, u0_ref[...]], 0)
        u1 = _conv_taps(xx, cw_ref, HALO_CONV, TM, False) + cb_ref[...]
        u1_ref[...] = u1
        u = _ln_silu(u1, lg_ref[...], lb_ref[...])[4]
        mixa_ref[...] = (u * _rms_r(u, D_CONV) * og_ref[...]).astype(BF16)

    return pl.pallas_call(
        body, name="fwd_conv", grid=(L // TM,),
        in_specs=[_rows(TM, D_CONV), _prev(HALO_CONV, D_CONV, TM), _full(cw.shape), _full(cb.shape),
                  _full(lg.shape), _full(lb.shape), _full(og.shape)],
        out_specs=[_rows(TM, D_CONV), _rows(TM, D_CONV)],
        out_shape=[_sds((L, D_CONV), F32), _sds((L, D_CONV), BF16)],
        compiler_params=_params(1))(u0, u0, cw, cb, lg, lb, og)


def _visible(i, j, t):
    row = i * t + lax.broadcasted_iota(jnp.int32, (t, t), 0)
    col = j * t + lax.broadcasted_iota(jnp.int32, (t, t), 1)
    return (lax.shift_right_logical(col, CHUNK_SHIFT) <= lax.shift_right_logical(row, CHUNK_SHIFT)) & (col >= DEAD)


def _visible_t(i, j, t):
    key = j * t + lax.broadcasted_iota(jnp.int32, (t, t), 0)
    query = i * t + lax.broadcasted_iota(jnp.int32, (t, t), 1)
    return (lax.shift_right_logical(key, CHUNK_SHIFT) <= lax.shift_right_logical(query, CHUNK_SHIFT)) & (key >= DEAD)


def _stat_lane(h):
    return (h // HEAD_GROUP) * LANES + h % HEAD_GROUP


def _scatter_stats(cols, t):
    lane = lax.broadcasted_iota(jnp.int32, (t, N_GROUPS * LANES), 1)
    out = jnp.zeros((t, N_GROUPS * LANES), F32)
    for h, col in enumerate(cols):
        out = jnp.where(lane == _stat_lane(h), col, out)
    return out


def _resident(shape, index_map):
    return pl.BlockSpec(shape, index_map, pipeline_mode=pl.Buffered(1))


def _attn_fwd(q, kv, kvt, kr, shards=()):
    L = q.shape[0]
    t = TM
    nq = L // t
    n = len(shards)
    pass_step = (3 * nq) // 4

    def body(q_ref, kv_ref, kvt_ref, kr_ref, *refs):
        gather_refs = refs[:n] + refs[n + 2:2 * n + 2] + refs[2 * n + 6:]
        o_ref, lse_ref = refs[n:n + 2]
        qt_scr, m_scr, l_scr, acc_scr = refs[2 * n + 2:2 * n + 6]
        i = pl.program_id(0)
        if n:
            @pl.when(i == 0)
            def _():
                for cp in _gather_copies(gather_refs[:n], gather_refs[n:2 * n], *gather_refs[2 * n:])[0]:
                    cp.start()

        lane = lax.broadcasted_iota(jnp.int32, (t, HB), 1)
        heads = range(N_HEADS)
        cols = [slice(h * HB, (h + 1) * HB) for h in heads]
        for h in heads:
            qt_scr[cols[h], :] = q_ref[:, cols[h]].astype(F32).T.astype(BF16)
        m_scr[...] = jnp.full_like(m_scr, NEG)
        l_scr[...] = jnp.zeros_like(l_scr)
        acc_scr[...] = jnp.zeros_like(acc_scr)

        def tile(j, vis):
            keys = pl.ds(pl.multiple_of(j * t, t), t)
            krj = kr_ref[keys, :]
            s = [_dot(jnp.where(lane < QK_NOPE, kv_ref[keys, cols[h]], krj), qt_scr[cols[h], :], NN) for h in heads]
            p, alpha = [], []
            for h in heads:
                sh = s[h] if vis is None else jnp.where(vis, s[h], NEG)
                m_prev = m_scr[h:h + 1, :]
                m_new = jnp.maximum(m_prev, jnp.max(sh, 0, keepdims=True))
                a = jnp.exp2(m_prev - m_new)
                ph = jnp.exp2(sh - m_new)
                l_scr[h:h + 1, :] = a * l_scr[h:h + 1, :] + jnp.sum(ph, 0, keepdims=True)
                m_scr[h:h + 1, :] = m_new
                p.append(ph.astype(BF16))
                alpha.append(a)
            for h in heads:
                acc_scr[cols[h], :] = alpha[h] * acc_scr[cols[h], :] + _dot(kvt_ref[cols[h], keys], p[h], NN)

        tile(i, _visible_t(i, i, t))

        @pl.when(i > 0)
        def _():
            tile(0, _visible_t(i, 0, t))

        def unmasked(j, carry):
            tile(j, None)
            return carry

        lax.fori_loop(1, i, unmasked, 0)
        lse_ref[...] = jnp.zeros_like(lse_ref)
        for h in heads:
            l = l_scr[h:h + 1, :]
            o_ref[:, cols[h]] = jnp.where(lane >= QK_NOPE, (acc_scr[cols[h], :] / l).T, 0.0).astype(BF16)
            lse_ref[h // HEAD_GROUP, h % HEAD_GROUP:h % HEAD_GROUP + 1, :] = m_scr[h:h + 1, :] + jnp.log2(l)
        if n:
            @pl.when(i == pass_step)
            def _():
                _, arrivals, forwards, _ = _gather_copies(gather_refs[:n], gather_refs[n:2 * n], *gather_refs[2 * n:])
                for landed, onward in zip(arrivals, forwards):
                    landed.wait_recv()
                    onward.start()

            @pl.when(i == nq - 1)
            def _():
                sends, _, forwards, finals = _gather_copies(gather_refs[:n], gather_refs[n:2 * n], *gather_refs[2 * n:])
                for cp in finals:
                    cp.wait_recv()
                for cp in sends + forwards:
                    cp.wait_send()

    any_spec = pl.BlockSpec(memory_space=pl.ANY)
    outs = pl.pallas_call(
        body, name="attn_fwd", grid=(nq,),
        in_specs=[_rows(t, D_HEADS), _resident((L, D_HEADS), lambda i: (0, 0)),
                  _resident((D_HEADS, L), lambda i: (0, 0)), _resident((L, HB), lambda i: (0, 0))] + [any_spec] * n,
        out_specs=[_rows(t, D_HEADS), pl.BlockSpec((N_GROUPS, SUBLANES, t), lambda i: (0, 0, i))] + [any_spec] * n,
        out_shape=[_sds((L, D_HEADS), BF16), _sds((N_GROUPS, SUBLANES, L), F32)] + _gather_out_shapes(shards),
        scratch_shapes=[pltpu.VMEM((D_HEADS, t), BF16), pltpu.VMEM((N_HEADS, t), F32), pltpu.VMEM((N_HEADS, t), F32),
                        pltpu.VMEM((D_HEADS, t), F32)] + (_gather_semaphores(n) if n else []),
        compiler_params=_params(1))(q, kv, kvt, kr, *shards)
    return outs[0], outs[1], _gathered(outs[2:], shards)


def _fwd_mix(h0, mixa, o, ga, wout, gffn):
    L = h0.shape[0]

    def body(h0_ref, mixa_ref, o_ref, ga_ref, wout_ref, gffn_ref, mix_ref, h1_ref, n2_ref):
        of = o_ref[...].astype(F32)
        mixb = (of * _rms_r(of, N_HEADS * V_HEAD) * ga_ref[...]).astype(BF16)
        mix = jnp.concatenate([mixa_ref[...], mixb], 1)
        mix_ref[...] = mix
        mo = jnp.where(_row_ids(TM) >= DEAD, _dot(mix, wout_ref[...], NN), 0.0)
        h1 = h0_ref[...] + mo
        h1_ref[...] = h1
        n2_ref[...] = (h1 * _rms_r(h1, D_MODEL) * gffn_ref[...]).astype(BF16)

    return pl.pallas_call(
        body, name="fwd_mix", grid=(L // TM,),
        in_specs=[_rows(TM, D_MODEL), _rows(TM, D_CONV), _rows(TM, D_HEADS), _full(ga.shape), _full(wout.shape),
                  _full(gffn.shape)],
        out_specs=[_rows(TM, D_MIX), _rows(TM, D_MODEL), _rows(TM, D_MODEL)],
        out_shape=[_sds((L, D_MIX), BF16), _sds((L, D_MODEL), F32), _sds((L, D_MODEL), BF16)],
        compiler_params=_params(1))(h0, mixa, o, ga, wout, gffn)


def _ffn_act_chunk(c, upg_ref, upv_ref, hg, hv, fcw_ref, fcb_ref):
    cs = slice(c * FF_CHUNK, (c + 1) * FF_CHUNK)
    out = []
    for part, (up_ref, halo) in enumerate(((upg_ref, hg), (upv_ref, hv))):
        xx = jnp.concatenate([halo[:, cs], up_ref[:, cs].astype(F32)], 0)
        ws = slice(part * D_FF + c * FF_CHUNK, part * D_FF + (c + 1) * FF_CHUNK)
        y = (fcw_ref[0:1, ws] * pltpu.roll(xx, 2, 0)[HALO_FFN:] + fcw_ref[1:2, ws] * pltpu.roll(xx, 1, 0)[HALO_FFN:]
             + fcw_ref[2:3, ws] * xx[HALO_FFN:] + fcb_ref[:, ws])
        out.append(y)
    return out


def _ffn_in_specs(L):
    return [_rows(TM, D_FF, 0), _rows(TM, D_FF, 1), _prev(HALO_FFN, D_FF, TM, 0), _prev(HALO_FFN, D_FF, TM, 1)]


def _ffn_halos(hg_ref, hv_ref):
    first = pl.program_id(0) == 0
    return (jnp.where(first, 0.0, hg_ref[...].astype(F32)), jnp.where(first, 0.0, hv_ref[...].astype(F32)))


def _fwd_ffn_loss(up0, fcw, fcb, wdown, h1, target, gfin):
    L = h1.shape[0]

    def body(upg_ref, upv_ref, hg_ref, hv_ref, fcw_ref, fcb_ref, wd_ref, h1_ref, t_ref, gf_ref,
             dh2_ref, loss_ref, dgf_ref, up_ref, act_ref):
        i = pl.program_id(0)
        hg, hv = _ffn_halos(hg_ref, hv_ref)
        for c in range(D_FF // FF_CHUNK):
            cs = slice(c * FF_CHUNK, (c + 1) * FF_CHUNK)
            g, val = _ffn_act_chunk(c, upg_ref, upv_ref, hg, hv, fcw_ref, fcb_ref)
            up_ref[:, cs] = g.astype(BF16)
            up_ref[:, D_FF + c * FF_CHUNK:D_FF + (c + 1) * FF_CHUNK] = val.astype(BF16)
            act_ref[:, cs] = (g * jax.nn.sigmoid(g) * val).astype(BF16)
        h2 = h1_ref[...] + _dot(act_ref[...], wd_ref[...], NN)
        r = _rms_r(h2, D_MODEL)
        gf = gf_ref[...]
        err = jnp.where(i > 0, h2 * r * gf - t_ref[...], 0.0)
        dy = err * (1.0 / D_MODEL)
        dh2, dgf = _rms_bwd(dy, h2, r, gf, D_MODEL)
        dh2_ref[...] = dh2

        @pl.when(i == 0)
        def _():
            loss_ref[...] = jnp.zeros_like(loss_ref)
            dgf_ref[...] = jnp.zeros_like(dgf_ref)

        loss_ref[...] += jnp.sum(err * err) * (0.5 / D_MODEL)
        dgf_ref[...] += dgf

    return pl.pallas_call(
        body, name="fwd_ffn_loss", grid=(L // TM,),
        in_specs=_ffn_in_specs(L) + [_full(fcw.shape), _full(fcb.shape), _full(wdown.shape), _rows(TM, D_MODEL),
                                     pl.BlockSpec((TM, D_MODEL), lambda i: (jnp.maximum(i - 1, 0), 0)),
                                     _full(gfin.shape)],
        out_specs=[_rows(TM, D_MODEL), _full((1, LANES)), _full((1, D_MODEL)), _rows(TM, 2 * D_FF), _rows(TM, D_FF)],
        out_shape=[_sds((L, D_MODEL), F32), _sds((1, LANES), F32), _sds((1, D_MODEL), F32),
                   _sds((L, 2 * D_FF), BF16), _sds((L, D_FF), BF16)],
        compiler_params=_params(1))(up0, up0, up0, up0, fcw, fcb, wdown, h1, target, gfin)


def _bwd_ffn_act(dh2, up, wdown):
    L = dh2.shape[0]

    def body(dh2_ref, upg_ref, upv_ref, wd_ref, dup_ref, dfcb_ref):
        da = _dot(dh2_ref[...].astype(BF16), wd_ref[...], NT)

        @pl.when(pl.program_id(0) == 0)
        def _():
            dfcb_ref[...] = jnp.zeros_like(dfcb_ref)

        for c in range(D_FF // FF_CHUNK):
            cs = slice(c * FF_CHUNK, (c + 1) * FF_CHUNK)
            vs = slice(D_FF + c * FF_CHUNK, D_FF + (c + 1) * FF_CHUNK)
            g, val = upg_ref[:, cs].astype(F32), upv_ref[:, cs].astype(F32)
            sg = jax.nn.sigmoid(g)
            si = g * sg
            dac = da[:, cs]
            dg = dac * val * (sg * (1.0 + g * (1.0 - sg)))
            dv = dac * si
            dup_ref[:, cs] = dg.astype(BF16)
            dup_ref[:, vs] = dv.astype(BF16)
            dfcb_ref[:, cs] += jnp.sum(dg, 0, keepdims=True)
            dfcb_ref[:, vs] += jnp.sum(dv, 0, keepdims=True)

    return pl.pallas_call(
        body, name="bwd_ffn_act", grid=(L // TM,),
        in_specs=[_rows(TM, D_MODEL), _rows(TM, D_FF, 0), _rows(TM, D_FF, 1), _full(wdown.shape)],
        out_specs=[_rows(TM, 2 * D_FF), _full((1, 2 * D_FF))],
        out_shape=[_sds((L, 2 * D_FF), BF16), _sds((1, 2 * D_FF), F32)],
        compiler_params=_params(1))(dh2, up, up, wdown)


def _bwd_ffn_conv_up(dup, up0, fcw, wup, tl):
    L, C = dup.shape
    tc = FF_TILE
    nt = L // tl
    nhb = L // HALO_FFN
    chunks = [(c0, min(FF_MXU_CHUNK, tc - c0)) for c0 in range(0, tc, FF_MXU_CHUNK)]

    def body(dy_ref, dyn_ref, x_ref, xp_ref, w_ref, wup_ref, dn2_ref, dx_ref, dw_ref, acc_ref):
        i, k = pl.program_id(0), pl.program_id(1)

        @pl.when(k == 0)
        def _():
            acc_ref[...] = jnp.zeros_like(acc_ref)

        last, first = i == nt - 1, i == 0
        for c0, cw in chunks:
            cs = slice(c0, c0 + cw)
            yy = jnp.concatenate([dy_ref[:, cs].astype(F32), jnp.where(last, 0.0, dyn_ref[:, cs].astype(F32))], 0)
            w = w_ref[:, cs]
            dx = (w[0:1] * pltpu.roll(yy, tl + HALO_FFN - 2, 0)[:tl] + w[1:2] * pltpu.roll(yy, tl + HALO_FFN - 1, 0)[:tl]
                  + w[2:3] * yy[:tl]).astype(BF16)
            dx_ref[:, cs] = dx
            acc_ref[...] += _dot(dx, wup_ref[:, cs], NT)
            xx = jnp.concatenate([jnp.where(first, 0.0, xp_ref[:, cs].astype(F32)), x_ref[:, cs].astype(F32)], 0)
            dy = yy[:tl]
            dw_ref[0:1, cs] = jnp.sum(dy * pltpu.roll(xx, 2, 0)[HALO_FFN:], 0, keepdims=True)
            dw_ref[1:2, cs] = jnp.sum(dy * pltpu.roll(xx, 1, 0)[HALO_FFN:], 0, keepdims=True)
            dw_ref[2:3, cs] = jnp.sum(dy * xx[HALO_FFN:], 0, keepdims=True)

        @pl.when(k == C // tc - 1)
        def _():
            dn2_ref[...] = acc_ref[...]

    tile = pl.BlockSpec((tl, tc), lambda i, k: (i, k))
    per = tl // HALO_FFN
    return pl.pallas_call(
        body, name="bwd_ffn_conv_up", grid=(nt, C // tc),
        in_specs=[tile, pl.BlockSpec((HALO_FFN, tc), lambda i, k: (jnp.minimum((i + 1) * per, nhb - 1), k)),
                  tile, pl.BlockSpec((HALO_FFN, tc), lambda i, k: (jnp.maximum(i * per - 1, 0), k)),
                  pl.BlockSpec((FFN_CONV_WIDTH, tc), lambda i, k: (0, k)),
                  pl.BlockSpec((None, D_MODEL, tc), lambda i, k: (k, 0, 0))],
        out_specs=[pl.BlockSpec((tl, D_MODEL), lambda i, k: (i, 0)), tile,
                   pl.BlockSpec((None, FFN_CONV_WIDTH, tc), lambda i, k: (i, 0, k))],
        out_shape=[_sds((L, D_MODEL), F32), _sds((L, C), BF16), _sds((nt, FFN_CONV_WIDTH, C), F32)],
        scratch_shapes=[pltpu.VMEM((tl, D_MODEL), F32)],
        compiler_params=_params(2))(dup, dup, up0, up0, fcw, wup)


def _bwd_mix(dh2, dn2, h1, gffn, wout, o, ga, u1, lg, lb, og):
    L = h1.shape[0]

    def body(dh2_ref, dn2_ref, h1_ref, gffn_ref, wout_ref, o_ref, ga_ref, u1_ref, lg_ref, lb_ref, og_ref,
             dh1_ref, dh1m_ref, do_ref, delta_ref, du1_ref, dgffn_ref, dga_ref, dog_ref, dlg_ref, dlb_ref):
        h1 = h1_ref[...]
        dn2x, dgffn = _rms_bwd(dn2_ref[...], h1, _rms_r(h1, D_MODEL), gffn_ref[...], D_MODEL)
        dh1 = dh2_ref[...] + dn2x
        dh1_ref[...] = dh1
        dh1m = jnp.where(_row_ids(TM) >= DEAD, dh1, 0.0).astype(BF16)
        dh1m_ref[...] = dh1m
        dmix = _dot(dh1m, wout_ref[...], NT)
        dma, dmb = dmix[:, :D_CONV], dmix[:, D_CONV:]
        of = o_ref[...].astype(F32)
        do, dga = _rms_bwd(dmb, of, _rms_r(of, N_HEADS * V_HEAD), ga_ref[...], N_HEADS * V_HEAD)
        do_ref[...] = do.astype(BF16)
        prod = do * of
        by_lane = _scatter_stats([jnp.sum(prod[:, h * HB:(h + 1) * HB], -1, keepdims=True) for h in range(N_HEADS)], TM)
        by_row = by_lane.T
        for grp in range(N_GROUPS):
            delta_ref[grp] = by_row[grp * LANES:grp * LANES + SUBLANES, :]
        lg = lg_ref[...]
        rs, xh, u2, sg, u = _ln_silu(u1_ref[...], lg, lb_ref[...])
        du, dog = _rms_bwd(dma, u, _rms_r(u, D_CONV), og_ref[...], D_CONV)
        du2 = du * (sg * (1.0 + u2 * (1.0 - sg)))
        dxh = du2 * lg
        du1_ref[...] = rs * (dxh - jnp.mean(dxh, -1, keepdims=True) - xh * jnp.mean(dxh * xh, -1, keepdims=True))

        @pl.when(pl.program_id(0) == 0)
        def _():
            for ref in (dgffn_ref, dga_ref, dog_ref, dlg_ref, dlb_ref):
                ref[...] = jnp.zeros_like(ref)

        dgffn_ref[...] += dgffn
        dga_ref[...] += dga
        dog_ref[...] += dog
        dlg_ref[...] += jnp.sum(du2 * xh, 0, keepdims=True)
        dlb_ref[...] += jnp.sum(du2, 0, keepdims=True)

    return pl.pallas_call(
        body, name="bwd_mix", grid=(L // TM,),
        in_specs=[_rows(TM, D_MODEL), _rows(TM, D_MODEL), _rows(TM, D_MODEL), _full(gffn.shape), _full(wout.shape),
                  _rows(TM, D_HEADS), _full(ga.shape), _rows(TM, D_CONV), _full(lg.shape), _full(lb.shape),
                  _full(og.shape)],
        out_specs=[_rows(TM, D_MODEL), _rows(TM, D_MODEL), _rows(TM, D_HEADS),
                   pl.BlockSpec((N_GROUPS, SUBLANES, TM), lambda i: (0, 0, i)),
                   _rows(TM, D_CONV), _full((1, D_MODEL)), _full((1, D_HEADS)), _full((1, D_CONV)),
                   _full((1, D_CONV)), _full((1, D_CONV))],
        out_shape=[_sds((L, D_MODEL), F32), _sds((L, D_MODEL), BF16), _sds((L, D_HEADS), BF16),
                   _sds((N_GROUPS, SUBLANES, L), F32),
                   _sds((L, D_CONV), F32), _sds((1, D_MODEL), F32), _sds((1, D_HEADS), F32), _sds((1, D_CONV), F32),
                   _sds((1, D_CONV), F32), _sds((1, D_CONV), F32)],
        compiler_params=_params(1))(dh2, dn2, h1, gffn, wout, o, ga, u1, lg, lb, og)


def _attn_bwd(q, kv, kr, do, lse, delta, parts=()):
    L = q.shape[0]
    t = TM
    nt = L // t
    gw = HEAD_GROUP * HB
    n = len(parts)

    def body(q_ref, kv_ref, kr_ref, do_ref, lse_ref, delta_ref, *refs):
        dq_ref, dkv_ref, dkr_ref = refs[n:n + 3]
        dqt_acc, dk_acc, dv_acc, kkt_scr = refs[2 * n + 3:2 * n + 7]
        exchange_refs = (refs[:n], refs[n + 3:2 * n + 3]) + refs[2 * n + 7:]
        g, j = pl.program_id(0), pl.program_id(1)
        if n:
            @pl.when((g == 0) & (j == 0))
            def _():
                for cp in _chip_copies(*exchange_refs)[0]:
                    cp.start()

        lane = lax.broadcasted_iota(jnp.int32, (t, HB), 1)

        @pl.when(j == 0)
        def _():
            dqt_acc[...] = jnp.zeros_like(dqt_acc)

        @pl.when((j == 0) & (g == 0))
        def _():
            dkr_ref[...] = jnp.zeros_like(dkr_ref)

        dk_acc[...] = jnp.zeros_like(dk_acc)
        dv_acc[...] = jnp.zeros_like(dv_acc)
        krj = kr_ref[...]
        heads = range(HEAD_GROUP)
        cols = [slice(h * HB, (h + 1) * HB) for h in heads]
        for hc in cols:
            kkt_scr[hc, :] = jnp.where(lane < QK_NOPE, kv_ref[:, hc], krj).astype(F32).T.astype(BF16)

        def tile(i, vis):
            qs = pl.ds(pl.multiple_of(i * t, t), t)
            kvj = [kv_ref[:, hc] for hc in cols]
            qi = [q_ref[qs, hc] for hc in cols]
            doi = [do_ref[qs, hc] for hc in cols]
            s = [_dot(jnp.where(lane < QK_NOPE, kvj[h], krj), qi[h].astype(F32).T.astype(BF16), NN) for h in heads]
            dp = [_dot(kvj[h], doi[h].astype(F32).T.astype(BF16), NN) for h in heads]
            p = []
            for h in heads:
                sh = s[h] if vis is None else jnp.where(vis, s[h], NEG)
                p.append(jnp.exp2(sh - lse_ref[h:h + 1, qs]))
            for h in heads:
                dv_acc[:, cols[h]] += _dot(p[h].astype(BF16), doi[h], NN)
            ds = [(p[h] * (dp[h] - delta_ref[h:h + 1, qs]) * LN2).astype(BF16) for h in heads]
            for h in heads:
                dk_acc[:, cols[h]] += _dot(ds[h], qi[h], NN)
            for h in heads:
                dqt_acc[cols[h], qs] += _dot(kkt_scr[cols[h], :], ds[h], NN)

        tile(j, _visible_t(j, j, t))

        @pl.when(j == 0)
        def _():
            def masked(i, carry):
                tile(i, _visible_t(i, 0, t))
                return carry

            lax.fori_loop(1, nt, masked, 0)

        @pl.when(j > 0)
        def _():
            def unmasked(i, carry):
                tile(i, None)
                return carry

            lax.fori_loop(j + 1, nt, unmasked, 0)

        dkr = jnp.zeros((t, HB), F32)
        for h in range(HEAD_GROUP):
            hc = slice(h * HB, (h + 1) * HB)
            dk = dk_acc[:, hc]
            dkv_ref[:, hc] = jnp.where(lane < QK_NOPE, dk, dv_acc[:, hc]).astype(BF16)
            dkr = dkr + jnp.where(lane >= QK_NOPE, dk, 0.0)
        dkr_ref[pl.ds(pl.multiple_of(j * t, t), t), :] += dkr

        @pl.when(j == nt - 1)
        def _():
            def untranspose(i, carry):
                qs = pl.ds(pl.multiple_of(i * t, t), t)
                dq_ref[qs, :] = (dqt_acc[:, qs].T * Q_SCALE).astype(BF16)
                return carry

            lax.fori_loop(0, nt, untranspose, 0)

        if n:
            @pl.when((g == N_GROUPS - 1) & (j == nt - 1))
            def _():
                sends, arrivals = _chip_copies(*exchange_refs)
                for cp in arrivals:
                    cp.wait_recv()
                for cp in sends:
                    cp.wait_send()

    group = lambda g, j: (0, g)
    stats = _resident((None, SUBLANES, L), lambda g, j: (g, 0, 0))
    any_spec = pl.BlockSpec(memory_space=pl.ANY)
    outs = pl.pallas_call(
        body, name="attn_bwd", grid=(N_GROUPS, nt),
        in_specs=[_resident((L, gw), group), pl.BlockSpec((t, gw), lambda g, j: (j, g)),
                  pl.BlockSpec((t, HB), lambda g, j: (j, 0)), _resident((L, gw), group), stats, stats]
        + [any_spec] * n,
        out_specs=[pl.BlockSpec((L, gw), group), pl.BlockSpec((t, gw), lambda g, j: (j, g)),
                   pl.BlockSpec((L, HB), lambda g, j: (0, 0))] + [any_spec] * n,
        out_shape=[_sds((L, D_HEADS), BF16), _sds((L, D_HEADS), BF16), _sds((L, HB), F32)]
        + [_sds(p.shape, p.dtype) for p in parts],
        scratch_shapes=[pltpu.VMEM((gw, L), F32), pltpu.VMEM((t, gw), F32), pltpu.VMEM((t, gw), F32),
                        pltpu.VMEM((gw, t), BF16)] + (_chip_semaphores(n) if n else []),
        compiler_params=_params(2))(q, kv, kr, do, lse, delta, *parts)
    return outs[0], outs[1], outs[2], list(outs[3:])


def _bwd_conv(du1, u0, cw, zag):
    L = du1.shape[0]
    nt = L // TM

    def body(dy_ref, dyn_ref, x_ref, xp_ref, cw_ref, zag_ref, dzag_ref, dcw_ref, dcb_ref):
        i = pl.program_id(0)
        dy = dy_ref[...]
        yy = jnp.concatenate([dy, jnp.where(i < nt - 1, dyn_ref[...], 0.0)], 0)
        du0 = _conv_taps(yy, cw_ref, HALO_CONV, TM, True)
        zag = zag_ref[...].astype(F32)
        a, sg = zag[:, :D_CONV], jax.nn.sigmoid(zag[:, D_CONV:])
        dzag_ref[...] = jnp.concatenate([du0 * sg, du0 * a * sg * (1.0 - sg)], 1).astype(BF16)
        xx = jnp.concatenate([jnp.where(i > 0, xp_ref[...], 0.0), x_ref[...]], 0)

        @pl.when(i == 0)
        def _():
            dcw_ref[...] = jnp.zeros_like(dcw_ref)
            dcb_ref[...] = jnp.zeros_like(dcb_ref)

        rolls = {}
        for k in range(CONV_WIDTH):
            xs = _shifted_rows(xx, CONV_WIDTH - 1 - k, HALO_CONV, TM, False, rolls)
            dcw_ref[k:k + 1, :] += jnp.sum(dy * xs, 0, keepdims=True)
        dcb_ref[...] += jnp.sum(dy, 0, keepdims=True)

    return pl.pallas_call(
        body, name="bwd_conv", grid=(nt,),
        in_specs=[_rows(TM, D_CONV), _next(HALO_CONV, D_CONV, TM, L // HALO_CONV), _rows(TM, D_CONV),
                  _prev(HALO_CONV, D_CONV, TM), _full(cw.shape), _rows(TM, D_AG)],
        out_specs=[_rows(TM, D_AG), _full(cw.shape), _full((1, D_CONV))],
        out_shape=[_sds((L, D_AG), BF16), _sds(cw.shape, F32), _sds((1, D_CONV), F32)],
        compiler_params=_params(1))(du1, du1, u0, u0, cw, zag)


def _bwd_in(dzag, dq, dkv, dkr, cq, ckv, gq, gkv, wuq, wukv, win, rc, rs1, rs2, h0, gmix, dh1):
    L = h0.shape[0]

    def body(dzag_ref, dq_ref, dkv_ref, dkr_ref, cq_ref, ckv_ref, gq_ref, gkv_ref, wuq_ref, wukv_ref, win_ref,
             c_ref, s1_ref, s2_ref, h0_ref, gmix_ref, dh1_ref,
             dz_ref, dqr_ref, gx_ref, dfirst_ref, dgq_ref, dgkv_ref, dgmix_ref):
        i = pl.program_id(0)
        c, s1, s2 = c_ref[...], s1_ref[...], s2_ref[...]
        dqr = _rope_bwd(dq_ref[...].astype(F32), jnp.tile(c, (1, N_HEADS)), jnp.tile(s1, (1, N_HEADS)),
                        jnp.tile(s2, (1, N_HEADS))).astype(BF16)
        dqr_ref[...] = dqr
        cq, ckv = cq_ref[...], ckv_ref[...]
        dcq, dgq = _rms_bwd(_dot(dqr, wuq_ref[...], NT), cq, _rms_r(cq, Q_LORA), gq_ref[...], Q_LORA)
        dckv, dgkv = _rms_bwd(_dot(dkv_ref[...], wukv_ref[...], NT), ckv, _rms_r(ckv, KV_LORA), gkv_ref[...], KV_LORA)
        dkrp = _rope_bwd(dkr_ref[...], c, s1, s2)
        dz = jnp.concatenate([dzag_ref[...], dcq.astype(BF16), dckv.astype(BF16), dkrp.astype(BF16)], 1)
        dz_ref[...] = dz
        h0 = h0_ref[...]
        dnx, dgmix = _rms_bwd(_dot(dz, win_ref[...], NT), h0, _rms_r(h0, D_MODEL), gmix_ref[...], D_MODEL)
        dh0 = dh1_ref[...] + dnx

        @pl.when(i == 0)
        def _():
            dfirst_ref[...] = dh0
            for ref in (dgq_ref, dgkv_ref, dgmix_ref):
                ref[...] = jnp.zeros_like(ref)

        @pl.when(i > 0)
        def _():
            gx_ref[...] = dh0

        dgq_ref[...] += dgq
        dgkv_ref[...] += dgkv
        dgmix_ref[...] += dgmix

    return pl.pallas_call(
        body, name="bwd_in", grid=(L // TM,),
        in_specs=[_rows(TM, D_AG), _rows(TM, D_HEADS), _rows(TM, D_HEADS), _rows(TM, HB), _rows(TM, Q_LORA),
                  _rows(TM, KV_LORA), _full(gq.shape), _full(gkv.shape), _full(wuq.shape), _full(wukv.shape),
                  _full(win.shape), _rows(TM, HB), _rows(TM, HB), _rows(TM, HB), _rows(TM, D_MODEL),
                  _full(gmix.shape), _rows(TM, D_MODEL)],
        out_specs=[_rows(TM, D_ZP), _rows(TM, D_HEADS),
                   pl.BlockSpec((TM, D_MODEL), lambda i: (jnp.maximum(i - 1, 0), 0)), _full((TM, D_MODEL)),
                   _full((1, Q_LORA)), _full((1, KV_LORA)), _full((1, D_MODEL))],
        out_shape=[_sds((L, D_ZP), BF16), _sds((L, D_HEADS), BF16), _sds((L - TM, D_MODEL), F32),
                   _sds((TM, D_MODEL), F32), _sds((1, Q_LORA), F32), _sds((1, KV_LORA), F32), _sds((1, D_MODEL), F32)],
        compiler_params=_params(1))(dzag, dq, dkv, dkr, cq, ckv, gq, gkv, wuq, wukv, win, rc, rs1, rs2, h0, gmix, dh1)


def _mesh_pos():
    return lax.axis_index("x"), lax.axis_index("y"), lax.axis_index("c")


def _all_gather(shards):
    n = len(shards)

    def body(*refs):
        sends, arrivals, forwards, finals = _gather_copies(refs[:n], refs[n:2 * n], *refs[2 * n:])
        for cp in sends:
            cp.start()
        for landed, onward in zip(arrivals, forwards):
            landed.wait_recv()
            onward.start()
        for cp in finals:
            cp.wait_recv()
        for cp in sends + forwards:
            cp.wait_send()

    any_spec = pl.BlockSpec(memory_space=pl.ANY)
    outs = pl.pallas_call(
        body, name="all_gather_weights", in_specs=[any_spec] * n, out_specs=[any_spec] * n,
        out_shape=_gather_out_shapes(shards), scratch_shapes=_gather_semaphores(n))(*shards)
    return _gathered(outs, shards)


def _gather_out_shapes(shards):
    return [_sds((2 * N_CHIPS, s.shape[0] // 2) + s.shape[1:], s.dtype) for s in shards]


def _gather_semaphores(n):
    return [pltpu.SemaphoreType.DMA((n, 8)), pltpu.SemaphoreType.DMA((n, 8))]


def _gathered(outs, shards):
    return [o.reshape((N_CHIPS, s.shape[0]) + s.shape[1:]) for o, s in zip(outs, shards)]


def _gather_copies(ins, outs, send_sems, recv_sems):
    x, y, c = _mesh_pos()
    chips = [(1 - x, y), (x, 1 - y), (1 - x, 1 - y)]
    sends, arrivals, forwards, finals = [], [], [], []

    def copy(src, dst, a, k, to):
        return pltpu.make_async_remote_copy(src, dst, send_sems.at[a, k], recv_sems.at[a, k], device_id=to,
                                            device_id_type=MESH)

    for a, (src, out) in enumerate(zip(ins, outs)):
        m = out.shape[1]
        mine = src.at[pl.ds(pl.multiple_of(c * m, 16), m)]
        for hf in range(2):
            own = out.at[4 * x + 2 * y + hf]
            sends.append(copy(src.at[pl.ds(hf * m, m)], own, a, 6 + hf, (x, y, 1 - c)))
            finals.append(copy(own, own, a, 6 + hf, (x, y, 1 - c)))
        for k, chip in enumerate(chips):
            slot = 4 * chip[0] + 2 * chip[1]
            sends.append(copy(mine, out.at[4 * x + 2 * y + c], a, k, (*chip, c)))
            arrivals.append(copy(out.at[slot + c], out.at[slot + c], a, k, (*chip, c)))
            forwards.append(copy(out.at[slot + c], out.at[slot + c], a, 3 + k, (x, y, 1 - c)))
            finals.append(copy(out.at[slot + 1 - c], out.at[slot + 1 - c], a, 3 + k, (x, y, 1 - c)))
    return sends, arrivals, forwards, finals


def _sibling_exchange(parts, name):
    n = len(parts)

    def body(*refs):
        ins, theirs = refs[:n], refs[n:2 * n]
        send_sems, recv_sems = refs[2 * n:]
        x, y, c = _mesh_pos()
        copies = []
        for a in range(n):
            h = parts[a].shape[1] // 2
            rows = pl.ds(pl.multiple_of((1 - c) * h, 16), h)
            copies += [pltpu.make_async_remote_copy(ins[a].at[q, rows], theirs[a].at[q], send_sems.at[a, q],
                                                    recv_sems.at[a, q], device_id=(x, y, 1 - c), device_id_type=MESH)
                       for q in range(N_CHIPS)]
        for cp in copies:
            cp.start()
        for cp in copies:
            cp.wait()

    any_spec = pl.BlockSpec(memory_space=pl.ANY)
    return pl.pallas_call(
        body, name=name, in_specs=[any_spec] * n, out_specs=[any_spec] * n,
        out_shape=[_sds((N_CHIPS, p.shape[1] // 2, p.shape[2]), p.dtype) for p in parts],
        scratch_shapes=[pltpu.SemaphoreType.DMA((n, N_CHIPS)), pltpu.SemaphoreType.DMA((n, N_CHIPS))])(*parts)


def _chip_exchange(parts, name):
    n = len(parts)

    def body(*refs):
        sends, arrivals = _chip_copies(refs[:n], refs[n:2 * n], *refs[2 * n:])
        for cp in sends:
            cp.start()
        for cp in arrivals:
            cp.wait_recv()
        for cp in sends:
            cp.wait_send()

    any_spec = pl.BlockSpec(memory_space=pl.ANY)
    return pl.pallas_call(
        body, name=name, in_specs=[any_spec] * n, out_specs=[any_spec] * n,
        out_shape=[_sds(p.shape, p.dtype) for p in parts], scratch_shapes=_chip_semaphores(n))(*parts)


def _chip_semaphores(n):
    return [pltpu.SemaphoreType.DMA((n, 3)), pltpu.SemaphoreType.DMA((n, 3))]


def _chip_copies(ins, outs, send_sems, recv_sems):
    x, y, c = _mesh_pos()
    me = 2 * x + y
    sends, arrivals = [], []
    for a, (src, out) in enumerate(zip(ins, outs)):
        for k, chip in enumerate([(1 - x, y), (x, 1 - y), (1 - x, 1 - y)]):
            slot = 2 * chip[0] + chip[1]
            sems = dict(send_sem=send_sems.at[a, k], recv_sem=recv_sems.at[a, k], device_id=(*chip, c),
                        device_id_type=MESH)
            sends.append(pltpu.make_async_remote_copy(src.at[slot], out.at[me], **sems))
            arrivals.append(pltpu.make_async_remote_copy(out.at[slot], out.at[slot], **sems))
    return sends, arrivals


def _sibling_gather(parts, name):
    n = len(parts)

    def body(*refs):
        ins, outs = refs[:n], refs[n:2 * n]
        send_sems, recv_sems = refs[2 * n:]
        x, y, c = _mesh_pos()
        copies = [pltpu.make_async_remote_copy(ins[a], outs[a], send_sems.at[a], recv_sems.at[a],
                                               device_id=(x, y, 1 - c), device_id_type=MESH) for a in range(n)]
        for cp in copies:
            cp.start()
        for cp in copies:
            cp.wait()

    any_spec = pl.BlockSpec(memory_space=pl.ANY)
    return pl.pallas_call(
        body, name=name, in_specs=[any_spec] * n, out_specs=[any_spec] * n,
        out_shape=[_sds(p.shape, p.dtype) for p in parts],
        scratch_shapes=[pltpu.SemaphoreType.DMA((n,)), pltpu.SemaphoreType.DMA((n,))])(*parts)


def _row_tile(rows, row_bytes, align, budget=1 << 20):
    best = None
    for t in range(align, rows + 1, align):
        if rows % t == 0 and t * row_bytes <= budget:
            best = t
    return best or rows


def _scalar(v):
    return jnp.reshape(v, (1,)).astype(jnp.int32)


def _add_pair(part, theirs, c, name):
    _, h, cols = theirs.shape
    tr = _row_tile(h, cols * 4, 16)
    nb = h // tr

    def body(c_ref, a_ref, b_ref, o_ref):
        o_ref[...] = (a_ref[...].astype(F32) + b_ref[...].astype(F32)).astype(o_ref.dtype)

    half = pl.BlockSpec((None, tr, cols), lambda q, i, c_ref: (q, i, 0))
    grid_spec = pltpu.PrefetchScalarGridSpec(
        num_scalar_prefetch=1, grid=(N_CHIPS, nb),
        in_specs=[pl.BlockSpec((None, tr, cols), lambda q, i, c_ref: (q, c_ref[0] * nb + i, 0)), half],
        out_specs=half)
    return pl.pallas_call(body, name=name, grid_spec=grid_spec, out_shape=_sds(theirs.shape, part.dtype),
                          compiler_params=_params(2))(_scalar(c), part, theirs)


def _add_chips(got, own, me, name):
    _, h, cols = got.shape
    tr = _row_tile(h, cols * 4 * N_CHIPS, 16, budget=1 << 21)

    def body(me_ref, got_ref, own_ref, o_ref):
        acc = None
        for q in range(N_CHIPS):
            term = jnp.where(me_ref[0] == q, own_ref[q], got_ref[q]).astype(F32)
            acc = term if acc is None else acc + term
        o_ref[...] = acc

    by_chip = pl.BlockSpec((N_CHIPS, tr, cols), lambda i, me_ref: (0, i, 0))
    grid_spec = pltpu.PrefetchScalarGridSpec(
        num_scalar_prefetch=1, grid=(h // tr,), in_specs=[by_chip, by_chip],
        out_specs=pl.BlockSpec((tr, cols), lambda i, me_ref: (i, 0)))
    return pl.pallas_call(body, name=name, grid_spec=grid_spec, out_shape=_sds((h, cols), F32),
                          compiler_params=_params(1))(_scalar(me), got, own)


def _reduce_begin(parts, tag):
    c = lax.axis_index("c")
    theirs = _sibling_exchange(parts, f"grad_sibling_exchange_{tag}")
    return [_add_pair(p, t, c, f"grad_add_pair_{tag}_{a}") for a, (p, t) in enumerate(zip(parts, theirs))]


def _reduce_end(pair, got, tag):
    x, y, c = _mesh_pos()
    half = [_add_chips(g, p, 2 * x + y, f"grad_add_chips_{tag}_{a}") for a, (g, p) in enumerate(zip(got, pair))]
    other = _sibling_gather(half, f"grad_sibling_gather_{tag}")
    return [jnp.where(c == 0, jnp.concatenate([h, o], 0), jnp.concatenate([o, h], 0)) for h, o in zip(half, other)]


def _adamw_math(w, g, m, v):
    m = ADAM_B1 * m + (1.0 - ADAM_B1) * g
    v = ADAM_B2 * v + (1.0 - ADAM_B2) * (g * g)
    m_hat = m / (1.0 - ADAM_B1 ** ADAM_STEP)
    v_hat = v / (1.0 - ADAM_B2 ** ADAM_STEP)
    return -ADAM_LR * (m_hat / (jnp.sqrt(v_hat) + ADAM_EPS) + ADAM_WD * w), m, v


def _adamw_big(w, g, m, v, name):
    r, c = w.shape
    tr = _row_tile(r, c * 4, 8, budget=1 << 19)

    def body(w_ref, g_ref, m_ref, v_ref, d_ref, mo_ref, vo_ref):
        d_ref[...], mo_ref[...], vo_ref[...] = _adamw_math(w_ref[...], g_ref[...], m_ref[...], v_ref[...])

    return pl.pallas_call(
        body, name=name, grid=(r // tr,), in_specs=[_rows(tr, c)] * 4, out_specs=[_rows(tr, c)] * 3,
        out_shape=[_sds((r, c), F32)] * 3, compiler_params=_params(1))(w, g, m, v)


def _adamw_small(ws, gs, ms, vs):
    n = len(ws)

    def body(*refs):
        for a in range(n):
            w_ref, g_ref, m_ref, v_ref = (refs[k * n + a] for k in range(4))
            d, m, v = _adamw_math(w_ref[...], g_ref[...], m_ref[...], v_ref[...])
            refs[4 * n + a][...] = d
            refs[5 * n + a][...] = m
            refs[6 * n + a][...] = v

    vm = pl.BlockSpec(memory_space=pltpu.VMEM)
    outs = pl.pallas_call(
        body, name="adamw_small", in_specs=[vm] * (4 * n), out_specs=[vm] * (3 * n),
        out_shape=[_sds(w.shape, F32) for w in ws] * 3)(*ws, *gs, *ms, *vs)
    return outs[:n], outs[n:2 * n], outs[2 * n:]


BIG = ("w_in", "w_uq", "w_ukv", "w_out", "w_ffn_up", "w_ffn_down")
SMALL_SHARDED = ("conv_w", "ffn_conv_w", "meta_tokens")
REPLICATED = ("mix_norm_g", "q_norm_g", "kv_norm_g", "conv_b", "conv_ln_g", "conv_ln_b", "conv_out_g", "attn_out_g",
              "ffn_norm_g", "ffn_conv_b", "final_norm_g")
WEIGHTS = ("meta_tokens", "mix_norm_g", "w_in", "q_norm_g", "w_uq", "kv_norm_g", "w_ukv", "conv_w", "conv_b",
           "conv_ln_g", "conv_ln_b", "conv_out_g", "attn_out_g", "w_out", "ffn_norm_g", "w_ffn_up", "ffn_conv_w",
           "ffn_conv_b", "w_ffn_down", "final_norm_g")


def _lane_rows(a):
    return a.reshape(N_CHIPS, -1, LANES)


def _col_shards(a):
    k = a.shape[0]
    return a.reshape(k, N_CHIPS, -1).transpose(1, 0, 2)


def _from_col_shards(a):
    return a.transpose(1, 0, 2).reshape(a.shape[1], -1)


def _pad_rows_to(a, rows):
    return jnp.pad(a, ((0, 0), (0, rows - a.shape[1]), (0, 0)))


def _rope_tables(L):
    pos = (jnp.arange(L, dtype=jnp.int32) - DEAD).astype(F32)
    inv_freq = 1.0 / (ROPE_THETA ** (jnp.arange(0, QK_ROPE, 2, dtype=F32) / QK_ROPE))
    ang = pos[:, None] * inv_freq[None, :]
    cos, sin = jnp.cos(ang), jnp.sin(ang)
    half = QK_ROPE // 2
    z = lambda n: jnp.zeros((L, n), F32)
    rc = jnp.concatenate([jnp.ones((L, QK_NOPE), F32), cos, cos, z(HB - QK_NOPE - QK_ROPE)], 1)
    rs1 = jnp.concatenate([z(QK_NOPE), -sin, z(HB - QK_NOPE - half)], 1)
    rs2 = jnp.concatenate([z(QK_NOPE + half), sin, z(HB - QK_NOPE - QK_ROPE)], 1)
    return rc, rs1, rs2


def _pad_heads(g):
    return jnp.pad(g.reshape(N_HEADS, V_HEAD), ((0, 0), (HB - V_HEAD, 0))).reshape(1, D_HEADS)


def _unpad_heads(g):
    return g.reshape(N_HEADS, HB)[:, HB - V_HEAD:].reshape(1, N_HEADS * V_HEAD)


def _local_step(x, target, w, late_shards=None, reduce_ffn=False):
    S = x.shape[0]
    L = TM + S
    tl = L // 4
    d_qk = QK_NOPE + QK_ROPE
    win_n = w["w_in"]
    kr0 = D_AG + Q_LORA + KV_LORA
    win = jnp.concatenate([win_n[:, :kr0], jnp.zeros((D_MODEL, QK_NOPE), BF16), win_n[:, kr0:],
                           jnp.zeros((D_MODEL, HB - d_qk), BF16)], 1)
    wuq = jnp.pad(w["w_uq"].reshape(Q_LORA, N_HEADS, d_qk), ((0, 0), (0, 0), (0, HB - d_qk))).reshape(Q_LORA, D_HEADS)
    wukv = w["w_ukv"]
    ga = _pad_heads(w["attn_out_g"])
    gfin = w["final_norm_g"].reshape(1, D_MODEL)
    rc, rs1, rs2 = _rope_tables(L)
    h0 = jnp.concatenate([jnp.zeros((DEAD, D_MODEL), F32), w["meta_tokens"], x], 0)

    n, zag, u0, cq, ckv, qn, kvn, q, kv, kr, kvt = _fwd_in(h0, w["mix_norm_g"], win, w["q_norm_g"], wuq,
                                                            w["kv_norm_g"], wukv, rc, rs1, rs2)
    u1, mixa = _fwd_conv(u0, w["conv_w"], w["conv_b"], w["conv_ln_g"], w["conv_ln_b"], w["conv_out_g"])
    if late_shards is None:
        o, lse, _ = _attn_fwd(q, kv, kvt, kr)
        wout_n, wup, wdown = w["w_out"], _col_shards(w["w_ffn_up"]), w["w_ffn_down"]
    else:
        o, lse, late = _attn_fwd(q, kv, kvt, kr, [late_shards[k] for k in LATE])
        wout_n, wup, wdown = _full_weight("w_out", late[0]), late[1], _full_weight("w_ffn_down", late[2])
    wout = jnp.concatenate([wout_n[:D_CONV], jnp.pad(wout_n[D_CONV:].reshape(N_HEADS, V_HEAD, D_MODEL),
                                                     ((0, 0), (HB - V_HEAD, 0), (0, 0))).reshape(D_HEADS, D_MODEL)], 0)
    mix, h1, n2 = _fwd_mix(h0, mixa, o, ga, wout, w["ffn_norm_g"])
    up0 = _mm(n2, wup, NN, BF16, tl, FF_TILE, D_MODEL, "ffn_up", b_slabs=True)
    dh2, loss, g_fin, up, act = _fwd_ffn_loss(up0, w["ffn_conv_w"], w["ffn_conv_b"], wdown, h1, target, gfin)

    dup, g_fcb = _bwd_ffn_act(dh2, up, wdown)
    dn2, dup0, g_fcw_tiles = _bwd_ffn_conv_up(dup, up0, w["ffn_conv_w"], wup, tl)
    g_fcw = jnp.sum(g_fcw_tiles, 0)
    g_wup = _mm(n2, dup0, TN, BF16, D_MODEL, FF_TILE, tl, "ffn_up_dw", by_col_tile=True)
    g_wdown = _mm(act, dh2, TN, BF16, D_FF // 2, D_MODEL, tl, "ffn_down_dw").reshape(N_CHIPS, -1, D_MODEL)
    ffn_pair = _reduce_begin([g_wup, g_wdown], "ffn") if reduce_ffn else None
    dh1, dh1m, do, delta, du1, g_gffn, g_ga, g_og, g_lg, g_lb = _bwd_mix(
        dh2, dn2, h1, w["ffn_norm_g"], wout, o, ga, u1, w["conv_ln_g"], w["conv_ln_b"], w["conv_out_g"])
    g_wout = _mm(mix, dh1m, TN, BF16, D_MIX // 2, D_MODEL, tl, "out_dw")
    dq, dkv, dkr, ffn_got = _attn_bwd(q, kv, kr, do, lse, delta, ffn_pair if reduce_ffn else ())
    if reduce_ffn:
        g_wup, g_wdown = _reduce_end(ffn_pair, ffn_got, "ffn")
    dzag, g_cw, g_cb = _bwd_conv(du1, u0, w["conv_w"], zag)
    dz, dqr, gx, dfirst, g_gq, g_gkv, g_gmix = _bwd_in(dzag, dq, dkv, dkr, cq, ckv, w["q_norm_g"], w["kv_norm_g"],
                                                      wuq, wukv, win, rc, rs1, rs2, h0, w["mix_norm_g"], dh1)
    g_win = _mm(n, dz, TN, BF16, D_MODEL, D_ZP // 2, tl, "in_dw")
    g_wuq = _mm(qn, dqr, TN, BF16, Q_LORA, D_HEADS, tl, "uq_dw")
    g_wukv = _mm(kvn, dkv, TN, BF16, KV_LORA, D_HEADS // N_CHIPS, tl, "ukv_dw", by_col_tile=True)

    grads = {
        "w_in": _col_shards(jnp.concatenate([g_win[:, :kr0], g_win[:, kr0 + QK_NOPE:kr0 + d_qk]], 1)),
        "w_uq": _col_shards(g_wuq.reshape(Q_LORA, N_HEADS, HB)[:, :, :d_qk].reshape(Q_LORA, N_HEADS * d_qk)),
        "w_ukv": g_wukv,
        "w_out": jnp.concatenate([g_wout[:D_CONV], g_wout[D_CONV:].reshape(N_HEADS, HB, D_MODEL)[:, HB - V_HEAD:]
                                  .reshape(N_HEADS * V_HEAD, D_MODEL)], 0).reshape(N_CHIPS, -1, D_MODEL),
        "w_ffn_up": g_wup, "w_ffn_down": g_wdown, "conv_w": g_cw, "ffn_conv_w": g_fcw,
        "meta_tokens": dfirst[DEAD:], "mix_norm_g": g_gmix, "q_norm_g": g_gq, "kv_norm_g": g_gkv, "conv_b": g_cb,
        "conv_ln_g": g_lg, "conv_ln_b": g_lb, "conv_out_g": g_og, "attn_out_g": _unpad_heads(g_ga),
        "ffn_norm_g": g_gffn, "ffn_conv_b": g_fcb, "final_norm_g": g_fin,
    }
    return loss, gx, grads


ROW_SHARDED = ("w_out", "w_ffn_down")


FFN = ("w_ffn_up", "w_ffn_down")
LATE = ("w_out", "w_ffn_up", "w_ffn_down")


def _full_weight(name, by_chip):
    return by_chip.reshape(-1, by_chip.shape[-1]) if name in ROW_SHARDED else _from_col_shards(by_chip)


def kernel(x, meta_tokens, mix_norm_g, w_in, q_norm_g, w_uq, kv_norm_g, w_ukv, conv_w, conv_b, conv_ln_g, conv_ln_b, conv_out_g, attn_out_g, w_out, ffn_norm_g, w_ffn_up, ffn_conv_w, ffn_conv_b, w_ffn_down, final_norm_g, loss_target, m_meta_tokens, m_mix_norm_g, m_w_in, m_q_norm_g, m_w_uq, m_kv_norm_g, m_w_ukv, m_conv_w, m_conv_b, m_conv_ln_g, m_conv_ln_b, m_conv_out_g, m_attn_out_g, m_w_out, m_ffn_norm_g, m_w_ffn_up, m_ffn_conv_w, m_ffn_conv_b, m_w_ffn_down, m_final_norm_g, v_meta_tokens, v_mix_norm_g, v_w_in, v_q_norm_g, v_w_uq, v_kv_norm_g, v_w_ukv, v_conv_w, v_conv_b, v_conv_ln_g, v_conv_ln_b, v_conv_out_g, v_attn_out_g, v_w_out, v_ffn_norm_g, v_w_ffn_up, v_ffn_conv_w, v_ffn_conv_b, v_w_ffn_down, v_final_norm_g):
    local = dict(meta_tokens=meta_tokens, mix_norm_g=mix_norm_g, w_in=w_in[0], q_norm_g=q_norm_g, w_uq=w_uq[0],
                 kv_norm_g=kv_norm_g, w_ukv=w_ukv[0], conv_w=conv_w[0], conv_b=conv_b, conv_ln_g=conv_ln_g,
                 conv_ln_b=conv_ln_b, conv_out_g=conv_out_g, attn_out_g=attn_out_g, w_out=w_out[0],
                 ffn_norm_g=ffn_norm_g, w_ffn_up=w_ffn_up[0], ffn_conv_w=ffn_conv_w[0], ffn_conv_b=ffn_conv_b,
                 w_ffn_down=w_ffn_down[0], final_norm_g=final_norm_g.reshape(1, D_MODEL))
    ms = dict(zip(WEIGHTS, (m_meta_tokens, m_mix_norm_g, m_w_in, m_q_norm_g, m_w_uq, m_kv_norm_g, m_w_ukv, m_conv_w,
                            m_conv_b, m_conv_ln_g, m_conv_ln_b, m_conv_out_g, m_attn_out_g, m_w_out, m_ffn_norm_g,
                            m_w_ffn_up, m_ffn_conv_w, m_ffn_conv_b, m_w_ffn_down, m_final_norm_g)))
    vs = dict(zip(WEIGHTS, (v_meta_tokens, v_mix_norm_g, v_w_in, v_q_norm_g, v_w_uq, v_kv_norm_g, v_w_ukv, v_conv_w,
                            v_conv_b, v_conv_ln_g, v_conv_ln_b, v_conv_out_g, v_attn_out_g, v_w_out, v_ffn_norm_g,
                            v_w_ffn_up, v_ffn_conv_w, v_ffn_conv_b, v_w_ffn_down, v_final_norm_g)))

    small_flat = jnp.concatenate([local[k].reshape(-1) for k in SMALL_SHARDED]).reshape(-1, LANES)
    early = [k for k in BIG if k not in LATE]
    gathered = _all_gather([local[k].astype(BF16) for k in early] + [small_flat])
    full = {k: v for k, v in local.items() if k not in LATE}
    for name, g in zip(early, gathered[:len(early)]):
        full[name] = _full_weight(name, g)
    small = gathered[-1].reshape(N_CHIPS, -1)
    at = 0
    for name in SMALL_SHARDED:
        r, c = local[name].shape
        full[name] = _from_col_shards(small[:, at:at + r * c].reshape(N_CHIPS, r, c))
        at += r * c

    loss_row, grad_x, grads = _local_step(x[0], loss_target[0], full, {k: local[k].astype(BF16) for k in LATE},
                                          reduce_ffn=True)

    rest_big = [k for k in BIG if k not in FFN]
    rep = jnp.concatenate([grads[k].reshape(-1) for k in REPLICATED] + [loss_row.reshape(-1)]).reshape(1, -1, LANES)
    small_pieces = [_lane_rows(_col_shards(grads[k])) for k in SMALL_SHARDED]
    small_pieces.append(jnp.broadcast_to(rep, (N_CHIPS,) + rep.shape[1:]))
    small_rows = sum(p.shape[1] for p in small_pieces)
    small_pack = _pad_rows_to(jnp.concatenate(small_pieces, 1), -(-small_rows // 32) * 32)
    pair = _reduce_begin([grads[k] for k in rest_big] + [small_pack], "rest")
    *rest_tot, small_tot = _reduce_end(pair, _chip_exchange(pair, "grad_chip_exchange_rest"), "rest")

    total = {k: grads[k] for k in FFN}
    total.update(zip(rest_big, rest_tot))
    flat = small_tot.reshape(-1)
    at = 0
    for name in SMALL_SHARDED + REPLICATED:
        shape = local[name].shape
        size = shape[0] * shape[1]
        total[name] = flat[at:at + size].reshape(shape)
        at += -(-size // LANES) * LANES if name in SMALL_SHARDED else size
    loss = flat[at]

    delta, new_m, new_v = {}, {}, {}
    shape2 = lambda a, name: a.reshape(local[name].shape)
    for name in BIG:
        delta[name], new_m[name], new_v[name] = _adamw_big(local[name], total[name], shape2(ms[name], name),
                                                          shape2(vs[name], name), "adamw_" + name)
    rest = SMALL_SHARDED + REPLICATED
    ds, nms, nvs = _adamw_small([local[k] for k in rest], [total[k] for k in rest],
                                [shape2(ms[k], k) for k in rest], [shape2(vs[k], k) for k in rest])
    for k, d, nm, nv in zip(rest, ds, nms, nvs):
        delta[k], new_m[k], new_v[k] = d, nm, nv

    out_shape = dict(zip(WEIGHTS, (meta_tokens, mix_norm_g, w_in, q_norm_g, w_uq, kv_norm_g, w_ukv, conv_w, conv_b,
                                   conv_ln_g, conv_ln_b, conv_out_g, attn_out_g, w_out, ffn_norm_g, w_ffn_up,
                                   ffn_conv_w, ffn_conv_b, w_ffn_down, final_norm_g)))
    outs = [loss, grad_x[None]]
    for group in (total, delta, new_m, new_v):
        outs += [group[k].reshape(out_shape[k].shape) for k in WEIGHTS]
    return tuple(outs)
```

```python
import functools

import jax
import jax.numpy as jnp
from jax import lax
from jax.experimental import pallas as pl
from jax.experimental.pallas import tpu as pltpu

F32 = jnp.float32
BF16 = jnp.bfloat16

D_MODEL = 1024
D_CONV = 512
CONV_WIDTH = 31
N_HEADS = 8
QK_NOPE = 64
QK_ROPE = 32
V_HEAD = 64
Q_LORA = 384
KV_LORA = 256
D_FF = 2816
FFN_CONV_WIDTH = 3
CHUNK_SHIFT = 6
N_META = 16
ROPE_THETA = 10000.0
EPS = 1e-6
NEG = -1e30
ADAM_LR = 0.001
ADAM_B1 = 0.9
ADAM_B2 = 0.999
ADAM_EPS = 1e-08
ADAM_WD = 0.01
ADAM_STEP = 10

LANES = 128
SUBLANES = 8
HB = LANES
D_HEADS = N_HEADS * HB
TM = 256
DEAD = TM - N_META
D_AG = 2 * D_CONV
D_ZP = D_AG + Q_LORA + KV_LORA + HB
D_MIX = D_CONV + D_HEADS
LN2 = 0.6931471805599453
Q_SCALE = (QK_NOPE + QK_ROPE) ** -0.5 / LN2
HALO_CONV = 32
HALO_FFN = 16
FF_CHUNK = 256
FF_MXU_CHUNK = 256
FF_TILE = D_FF // 2
VMEM_LIMIT = 56 * 1024 * 1024
N_CHIPS = 4
HEAD_GROUP = 4
N_GROUPS = N_HEADS // HEAD_GROUP
MESH =pl.DeviceIdType.MESH


def _params(n_grid):
    return pltpu.CompilerParams(dimension_semantics=("arbitrary",) * n_grid, vmem_limit_bytes=VMEM_LIMIT)


def _rows(tm, c, off=0):
    return pl.BlockSpec((tm, c), lambda i: (i, off))


def _full(shape):
    return pl.BlockSpec(shape, lambda i: (0,) * len(shape))


def _prev(hb, c, tm, off=0):
    return pl.BlockSpec((hb, c), lambda i: (jnp.maximum(i * (tm // hb) - 1, 0), off))


def _next(hb, c, tm, nblk, off=0):
    return pl.BlockSpec((hb, c), lambda i: (jnp.minimum((i + 1) * (tm // hb), nblk - 1), off))


def _sds(shape, dtype):
    return jax.ShapeDtypeStruct(shape, dtype)


def _rms_r(x, n):
    return lax.rsqrt(jnp.sum(x * x, -1, keepdims=True) * (1.0 / n) + EPS)


def _rms_bwd(dy, x, r, g, n):
    gd = dy * g
    dx = r * gd - x * (r * r * r) * (jnp.sum(x * gd, -1, keepdims=True) * (1.0 / n))
    return dx, jnp.sum(dy * x * r, 0, keepdims=True)


def _dot(a, b, dims):
    return lax.dot_general(a, b, (dims, ((), ())), preferred_element_type=F32)


NN = ((1,), (0,))
NT = ((1,), (1,))
TN = ((0,), (0,))


def _rope(x, c, s1, s2):
    n = x.shape[-1]
    return x * c + pltpu.roll(x, n - QK_ROPE // 2, 1) * s1 + pltpu.roll(x, QK_ROPE // 2, 1) * s2


def _rope_bwd(g, c, s1, s2):
    n = g.shape[-1]
    return g * c + pltpu.roll(g * s1, QK_ROPE // 2, 1) + pltpu.roll(g * s2, n - QK_ROPE // 2, 1)


def _row_ids(tm, cols=1):
    return pl.program_id(0) * tm + lax.broadcasted_iota(jnp.int32, (tm, cols), 0)


def _mm(a, b, dims, out_dtype, tm, tn, tk, name, by_col_tile=False, b_slabs=False):
    if dims == TN:
        (kk, m), (_, n) = a.shape, b.shape
        a_spec = pl.BlockSpec((tk, tm), lambda i, j, k: (k, i))
    else:
        m, kk = a.shape
        a_spec = pl.BlockSpec((tm, tk), lambda i, j, k: (i, k))
    if dims == NT and b_slabs:
        n = b.shape[1]
        assert b.shape[2] == tk and kk == b.shape[0] * tk, (name, b.shape)
        b_spec = pl.BlockSpec((None, tn, tk), lambda i, j, k: (k, j, 0))
    elif dims == NT:
        n = b.shape[0]
        b_spec = pl.BlockSpec((tn, tk), lambda i, j, k: (j, k))
    elif b_slabs:
        n = b.shape[0] * b.shape[2]
        assert b.shape[2] == tn and kk == b.shape[1], (name, b.shape)
        b_spec = pl.BlockSpec((None, tk, tn), lambda i, j, k: (j, k, 0))
    else:
        n = b.shape[1]
        b_spec = pl.BlockSpec((tk, tn), lambda i, j, k: (k, j))
    assert m % tm == 0 and n % tn == 0 and kk % tk == 0, (name, a.shape, b.shape, tm, tn, tk)
    nk = kk // tk

    def body(a_ref, b_ref, o_ref, acc_ref):
        k = pl.program_id(2)

        @pl.when(k == 0)
        def _():
            acc_ref[...] = jnp.zeros_like(acc_ref)

        acc_ref[...] += _dot(a_ref[...].astype(BF16), b_ref[...].astype(BF16), dims)

        @pl.when(k == nk - 1)
        def _():
            o_ref[...] = acc_ref[...].astype(out_dtype)

    if by_col_tile:
        out_spec, out_shape = pl.BlockSpec((None, tm, tn), lambda i, j, k: (j, i, 0)), (n // tn, m, tn)
    else:
        out_spec, out_shape = pl.BlockSpec((tm, tn), lambda i, j, k: (i, j)), (m, n)
    return pl.pallas_call(
        body, name=name, grid=(m // tm, n // tn, nk), in_specs=[a_spec, b_spec], out_specs=out_spec,
        out_shape=_sds(out_shape, out_dtype), scratch_shapes=[pltpu.VMEM((tm, tn), F32)],
        compiler_params=_params(3))(a, b)


def _fwd_in(x, head, gmix, win, gq, wuq, gkv, wukv, rc, rs1, rs2):
    L = TM + x.shape[0]

    def body(x_ref, head_ref, gmix_ref, win_ref, gq_ref, wuq_ref, gkv_ref, wukv_ref, c_ref, s1_ref, s2_ref,
             h0_ref, n_ref, zag_ref, u0_ref, cq_ref, ckv_ref, qn_ref, kvn_ref, q_ref, kv_ref, kr_ref, kvt_ref):
        h = jnp.where(pl.program_id(0) == 0, head_ref[...], x_ref[...])
        h0_ref[...] = h
        n = (h * _rms_r(h, D_MODEL) * gmix_ref[...]).astype(BF16)
        n_ref[...] = n
        z = _dot(n, win_ref[...], NN)
        a, gate = z[:, :D_CONV], z[:, D_CONV:D_AG]
        zag_ref[...] = z[:, :D_AG].astype(BF16)
        u0_ref[...] = a * jax.nn.sigmoid(gate)
        cq = z[:, D_AG:D_AG + Q_LORA]
        ckv = z[:, D_AG + Q_LORA:D_AG + Q_LORA + KV_LORA]
        krp = z[:, D_AG + Q_LORA + KV_LORA:]
        cq_ref[...] = cq
        ckv_ref[...] = ckv
        qn = (cq * _rms_r(cq, Q_LORA) * gq_ref[...]).astype(BF16)
        qn_ref[...] = qn
        kvn = (ckv * _rms_r(ckv, KV_LORA) * gkv_ref[...]).astype(BF16)
        kvn_ref[...] = kvn
        c, s1, s2 = c_ref[...], s1_ref[...], s2_ref[...]
        q = _dot(qn, wuq_ref[...], NN)
        q = _rope(q, jnp.tile(c, (1, N_HEADS)), jnp.tile(s1, (1, N_HEADS)), jnp.tile(s2, (1, N_HEADS)))
        q_ref[...] = (q * Q_SCALE).astype(BF16)
        kv = _dot(kvn, wukv_ref[...], NN)
        kv_ref[...] = kv.astype(BF16)
        kvt_ref[...] = kv.T.astype(BF16)
        kr_ref[...] = _rope(krp, c, s1, s2).astype(BF16)

    outs = [(D_MODEL, F32), (D_MODEL, BF16), (D_AG, BF16), (D_CONV, F32), (Q_LORA, F32), (KV_LORA, F32), (Q_LORA, BF16),
            (KV_LORA, BF16), (D_HEADS, BF16), (D_HEADS, BF16), (HB, BF16)]
    return pl.pallas_call(
        body, name="fwd_in", grid=(L // TM,),
        in_specs=[pl.BlockSpec((TM, D_MODEL), lambda i: (jnp.maximum(i - 1, 0), 0)), _full(head.shape),
                  _full(gmix.shape), _full(win.shape), _full(gq.shape), _full(wuq.shape),
                  _full(gkv.shape), _full(wukv.shape), _rows(TM, HB), _rows(TM, HB), _rows(TM, HB)],
        out_specs=[_rows(TM, c) for c, _ in outs] + [pl.BlockSpec((D_HEADS, TM), lambda i: (0, i))],
        out_shape=[_sds((L, c), d) for c, d in outs] + [_sds((D_HEADS, L), BF16)],
        compiler_params=_params(1))(x, head, gmix, win, gq, wuq, gkv, wukv, rc, rs1, rs2)


def _conv_taps(xx, w_ref, halo, tm, flip):
    kw = w_ref.shape[0]
    acc, rolls = None, {}
    for k in range(kw):
        term = w_ref[k:k + 1, :] * _shifted_rows(xx, kw - 1 - k, halo, tm, flip, rolls)
        acc = term if acc is None else acc + term
    return acc


def _shifted_rows(xx, d, halo, tm, flip, rolls):
    a, b = divmod(d, SUBLANES)
    if b not in rolls:
        rolls[b] = xx if b == 0 else pltpu.roll(xx, (xx.shape[0] - b) if flip else b, 0)
    start = SUBLANES * a if flip else halo - SUBLANES * a
    return rolls[b][start:start + tm]


def _ln_silu(u1, lg, lb):
    mu = jnp.mean(u1, -1, keepdims=True)
    xc = u1 - mu
    rs = lax.rsqrt(jnp.mean(xc * xc, -1, keepdims=True) + EPS)
    xh = xc * rs
    u2 = xh * lg + lb
    sg = jax.nn.sigmoid(u2)
    return rs, xh, u2, sg, u2 * sg


def _fwd_conv(u0, cw, cb, lg, lb, og):
    L = u0.shape[0]

    def body(u0_ref, u0p_ref, cw_ref, cb_ref, lg_ref, lb_ref, og_ref, u1_ref, mixa_ref):
        halo = jnp.where(pl.program_id(0) > 0, u0p_ref[...], 0.0)
        xx = jnp.concatenate([halo, u0_ref[...]], 0)
        u1 = _conv_taps(xx, cw_ref, HALO_CONV, TM, False) + cb_ref[...]
        u1_ref[...] = u1
        u = _ln_silu(u1, lg_ref[...], lb_ref[...])[4]
        mixa_ref[...] = (u * _rms_r(u, D_CONV) * og_ref[...]).astype(BF16)

    return pl.pallas_call(
        body, name="fwd_conv", grid=(L // TM,),
        in_specs=[_rows(TM, D_CONV), _prev(HALO_CONV, D_CONV, TM), _full(cw.shape), _full(cb.shape),
                  _full(lg.shape), _full(lb.shape), _full(og.shape)],
        out_specs=[_rows(TM, D_CONV), _rows(TM, D_CONV)],
        out_shape=[_sds((L, D_CONV), F32), _sds((L, D_CONV), BF16)],
        compiler_params=_params(1))(u0, u0, cw, cb, lg, lb, og)


def _visible(i, j, t):
    row = i * t + lax.broadcasted_iota(jnp.int32, (t, t), 0)
    col = j * t + lax.broadcasted_iota(jnp.int32, (t, t), 1)
    return (lax.shift_right_logical(col, CHUNK_SHIFT) <= lax.shift_right_logical(row, CHUNK_SHIFT)) & (col >= DEAD)


def _visible_t(i, j, t):
    key = j * t + lax.broadcasted_iota(jnp.int32, (t, t), 0)
    query = i * t + lax.broadcasted_iota(jnp.int32, (t, t), 1)
    return (lax.shift_right_logical(key, CHUNK_SHIFT) <= lax.shift_right_logical(query, CHUNK_SHIFT)) & (key >= DEAD)


def _stat_lane(h):
    return (h // HEAD_GROUP) * LANES + h % HEAD_GROUP


def _scatter_stats(cols, t):
    lane = lax.broadcasted_iota(jnp.int32, (t, N_GROUPS * LANES), 1)
    out = jnp.zeros((t, N_GROUPS * LANES), F32)
    for h, col in enumerate(cols):
        out = jnp.where(lane == _stat_lane(h), col, out)
    return out


def _resident(shape, index_map):
    return pl.BlockSpec(shape, index_map, pipeline_mode=pl.Buffered(1))


def _attn_fwd(q, kv, kvt, kr, shards=()):
    L = q.shape[0]
    t = TM
    nq = L // t
    n = len(shards)
    pass_step = (5 * nq) // 6

    def body(q_ref, kv_ref, kvt_ref, kr_ref, *refs):
        gather_refs = refs[:n] + refs[n + 2:2 * n + 2] + refs[2 * n + 6:]
        o_ref, lse_ref = refs[n:n + 2]
        qt_scr, m_scr, l_scr, acc_scr = refs[2 * n + 2:2 * n + 6]
        i = pl.program_id(0)
        if n:
            @pl.when(i == 0)
            def _():
                for cp in _gather_copies(gather_refs[:n], gather_refs[n:2 * n], *gather_refs[2 * n:])[0]:
                    cp.start()

        lane = lax.broadcasted_iota(jnp.int32, (t, HB), 1)
        heads = range(N_HEADS)
        cols = [slice(h * HB, (h + 1) * HB) for h in heads]
        for h in heads:
            qt_scr[cols[h], :] = q_ref[:, cols[h]].astype(F32).T.astype(BF16)
        m_scr[...] = jnp.full_like(m_scr, NEG)
        l_scr[...] = jnp.zeros_like(l_scr)
        acc_scr[...] = jnp.zeros_like(acc_scr)

        def tile(keys, vis, whole=True):
            krj = kr_ref[keys, :]
            kvj = [kv_ref[keys, cols[h]] for h in heads]
            lane_k = lane[:krj.shape[0]]
            s = [_dot(jnp.where(lane_k < QK_NOPE, kvj[h], krj), qt_scr[cols[h], :], NN) for h in heads]
            p, alpha = [], []
            for h in heads:
                sh = s[h] if vis is None else jnp.where(vis, s[h], NEG)
                m_prev = m_scr[h:h + 1, :]
                m_new = jnp.maximum(m_prev, jnp.max(sh, 0, keepdims=True))
                a = jnp.exp2(m_prev - m_new)
                ph = jnp.exp2(sh - m_new)
                l_scr[h:h + 1, :] = a * l_scr[h:h + 1, :] + jnp.sum(ph, 0, keepdims=True)
                m_scr[h:h + 1, :] = m_new
                p.append(ph.astype(BF16))
                alpha.append(a)
            for h in heads:
                pv = _dot(kvt_ref[cols[h], keys], p[h], NN) if whole else _dot(kvj[h], p[h], TN)
                acc_scr[cols[h], :] = alpha[h] * acc_scr[cols[h], :] + pv

        tile(pl.ds(pl.multiple_of(i * t, t), t), _visible_t(i, i, t))

        @pl.when(i > 0)
        def _():
            tile(pl.ds(DEAD, N_META), None, whole=False)

        def unmasked(j, carry):
            tile(pl.ds(pl.multiple_of(j * t, t), t), None)
            return carry

        lax.fori_loop(1, i, unmasked, 0)
        lse_ref[...] = jnp.zeros_like(lse_ref)
        for h in heads:
            l = l_scr[h:h + 1, :]
            o_ref[:, cols[h]] = jnp.where(lane >= QK_NOPE, (acc_scr[cols[h], :] / l).T, 0.0).astype(BF16)
            lse_ref[h // HEAD_GROUP, h % HEAD_GROUP:h % HEAD_GROUP + 1, :] = m_scr[h:h + 1, :] + jnp.log2(l)
        if n:
            @pl.when(i == pass_step)
            def _():
                _, arrivals, forwards, _ = _gather_copies(gather_refs[:n], gather_refs[n:2 * n], *gather_refs[2 * n:])
                for landed, onward in zip(arrivals, forwards):
                    landed.wait_recv()
                    onward.start()

            @pl.when(i == nq - 1)
            def _():
                sends, _, forwards, finals = _gather_copies(gather_refs[:n], gather_refs[n:2 * n], *gather_refs[2 * n:])
                for cp in finals:
                    cp.wait_recv()
                for cp in sends + forwards:
                    cp.wait_send()

    any_spec = pl.BlockSpec(memory_space=pl.ANY)
    outs = pl.pallas_call(
        body, name="attn_fwd", grid=(nq,),
        in_specs=[_rows(t, D_HEADS), _resident((L, D_HEADS), lambda i: (0, 0)),
                  _resident((D_HEADS, L), lambda i: (0, 0)), _resident((L, HB), lambda i: (0, 0))] + [any_spec] * n,
        out_specs=[_rows(t, D_HEADS), pl.BlockSpec((N_GROUPS, SUBLANES, t), lambda i: (0, 0, i))] + [any_spec] * n,
        out_shape=[_sds((L, D_HEADS), BF16), _sds((N_GROUPS, SUBLANES, L), F32)] + _gather_out_shapes(shards),
        scratch_shapes=[pltpu.VMEM((D_HEADS, t), BF16), pltpu.VMEM((N_HEADS, t), F32), pltpu.VMEM((N_HEADS, t), F32),
                        pltpu.VMEM((D_HEADS, t), F32)] + (_gather_semaphores(n) if n else []),
        compiler_params=_params(1))(q, kv, kvt, kr, *shards)
    return outs[0], outs[1], _gathered(outs[2:], shards)


def _fwd_mix(h0, mixa, o, ga, wout, gffn):
    L = h0.shape[0]

    def body(h0_ref, mixa_ref, o_ref, ga_ref, wout_ref, gffn_ref, mix_ref, h1_ref, n2_ref):
        of = o_ref[...].astype(F32)
        mixb = (of * _rms_r(of, N_HEADS * V_HEAD) * ga_ref[...]).astype(BF16)
        mix = jnp.concatenate([mixa_ref[...], mixb], 1)
        mix_ref[...] = mix
        mo = jnp.where(_row_ids(TM) >= DEAD, _dot(mix, wout_ref[...], NN), 0.0)
        h1 = h0_ref[...] + mo
        h1_ref[...] = h1
        n2_ref[...] = (h1 * _rms_r(h1, D_MODEL) * gffn_ref[...]).astype(BF16)

    return pl.pallas_call(
        body, name="fwd_mix", grid=(L // TM,),
        in_specs=[_rows(TM, D_MODEL), _rows(TM, D_CONV), _rows(TM, D_HEADS), _full(ga.shape), _full(wout.shape),
                  _full(gffn.shape)],
        out_specs=[_rows(TM, D_MIX), _rows(TM, D_MODEL), _rows(TM, D_MODEL)],
        out_shape=[_sds((L, D_MIX), BF16), _sds((L, D_MODEL), F32), _sds((L, D_MODEL), BF16)],
        compiler_params=_params(1))(h0, mixa, o, ga, wout, gffn)


def _ffn_act_chunk(c, upg_ref, upv_ref, hg, hv, fcw_ref, fcb_ref):
    cs = slice(c * FF_CHUNK, (c + 1) * FF_CHUNK)
    out = []
    for part, (up_ref, halo) in enumerate(((upg_ref, hg), (upv_ref, hv))):
        xx = jnp.concatenate([halo[:, cs], up_ref[:, cs].astype(F32)], 0)
        ws = slice(part * D_FF + c * FF_CHUNK, part * D_FF + (c + 1) * FF_CHUNK)
        y = (fcw_ref[0:1, ws] * pltpu.roll(xx, 2, 0)[HALO_FFN:] + fcw_ref[1:2, ws] * pltpu.roll(xx, 1, 0)[HALO_FFN:]
             + fcw_ref[2:3, ws] * xx[HALO_FFN:] + fcb_ref[:, ws])
        out.append(y)
    return out


def _ffn_in_specs(L):
    return [_rows(TM, D_FF, 0), _rows(TM, D_FF, 1), _prev(HALO_FFN, D_FF, TM, 0), _prev(HALO_FFN, D_FF, TM, 1)]


def _ffn_halos(hg_ref, hv_ref):
    first = pl.program_id(0) == 0
    return (jnp.where(first, 0.0, hg_ref[...].astype(F32)), jnp.where(first, 0.0, hv_ref[...].astype(F32)))


def _fwd_ffn_loss(up0, fcw, fcb, wdown, h1, target, gfin):
    L = h1.shape[0]

    def body(upg_ref, upv_ref, hg_ref, hv_ref, fcw_ref, fcb_ref, wd_ref, h1_ref, t_ref, gf_ref,
             dh2_ref, loss_ref, dgf_ref, up_ref, act_ref):
        i = pl.program_id(0)
        hg, hv = _ffn_halos(hg_ref, hv_ref)
        for c in range(D_FF // FF_CHUNK):
            cs = slice(c * FF_CHUNK, (c + 1) * FF_CHUNK)
            g, val = _ffn_act_chunk(c, upg_ref, upv_ref, hg, hv, fcw_ref, fcb_ref)
            up_ref[:, cs] = g.astype(BF16)
            up_ref[:, D_FF + c * FF_CHUNK:D_FF + (c + 1) * FF_CHUNK] = val.astype(BF16)
            act_ref[:, cs] = (g * jax.nn.sigmoid(g) * val).astype(BF16)
        h2 = h1_ref[...] + _dot(act_ref[...], wd_ref[...], NN)
        r = _rms_r(h2, D_MODEL)
        gf = gf_ref[...]
        err = jnp.where(i > 0, h2 * r * gf - t_ref[...], 0.0)
        dy = err * (1.0 / D_MODEL)
        dh2, dgf = _rms_bwd(dy, h2, r, gf, D_MODEL)
        dh2_ref[...] = dh2

        @pl.when(i == 0)
        def _():
            loss_ref[...] = jnp.zeros_like(loss_ref)
            dgf_ref[...] = jnp.zeros_like(dgf_ref)

        loss_ref[...] += jnp.sum(err * err) * (0.5 / D_MODEL)
        dgf_ref[...] += dgf

    return pl.pallas_call(
        body, name="fwd_ffn_loss", grid=(L // TM,),
        in_specs=_ffn_in_specs(L) + [_full(fcw.shape), _full(fcb.shape), _full(wdown.shape), _rows(TM, D_MODEL),
                                     pl.BlockSpec((TM, D_MODEL), lambda i: (jnp.maximum(i - 1, 0), 0)),
                                     _full(gfin.shape)],
        out_specs=[_rows(TM, D_MODEL), _full((1, LANES)), _full((1, D_MODEL)), _rows(TM, 2 * D_FF), _rows(TM, D_FF)],
        out_shape=[_sds((L, D_MODEL), F32), _sds((1, LANES), F32), _sds((1, D_MODEL), F32),
                   _sds((L, 2 * D_FF), BF16), _sds((L, D_FF), BF16)],
        compiler_params=_params(1))(up0, up0, up0, up0, fcw, fcb, wdown, h1, target, gfin)


def _bwd_ffn_act(dh2, up, wdown):
    L = dh2.shape[0]

    def body(dh2_ref, upg_ref, upv_ref, wd_ref, dup_ref, dfcb_ref):
        da = _dot(dh2_ref[...].astype(BF16), wd_ref[...], NT)

        @pl.when(pl.program_id(0) == 0)
        def _():
            dfcb_ref[...] = jnp.zeros_like(dfcb_ref)

        for c in range(D_FF // FF_CHUNK):
            cs = slice(c * FF_CHUNK, (c + 1) * FF_CHUNK)
            vs = slice(D_FF + c * FF_CHUNK, D_FF + (c + 1) * FF_CHUNK)
            g, val = upg_ref[:, cs].astype(F32), upv_ref[:, cs].astype(F32)
            sg = jax.nn.sigmoid(g)
            si = g * sg
            dac = da[:, cs]
            dg = dac * val * (sg * (1.0 + g * (1.0 - sg)))
            dv = dac * si
            dup_ref[:, cs] = dg.astype(BF16)
            dup_ref[:, vs] = dv.astype(BF16)
            dfcb_ref[:, cs] += jnp.sum(dg, 0, keepdims=True)
            dfcb_ref[:, vs] += jnp.sum(dv, 0, keepdims=True)

    return pl.pallas_call(
        body, name="bwd_ffn_act", grid=(L // TM,),
        in_specs=[_rows(TM, D_MODEL), _rows(TM, D_FF, 0), _rows(TM, D_FF, 1), _full(wdown.shape)],
        out_specs=[_rows(TM, 2 * D_FF), _full((1, 2 * D_FF))],
        out_shape=[_sds((L, 2 * D_FF), BF16), _sds((1, 2 * D_FF), F32)],
        compiler_params=_params(1))(dh2, up, up, wdown)


def _bwd_ffn_conv_up(dup, up0, fcw, wup, tl):
    L, C = dup.shape
    tc = FF_TILE
    nt = L // tl
    nhb = L // HALO_FFN
    chunks = [(c0, min(FF_MXU_CHUNK, tc - c0)) for c0 in range(0, tc, FF_MXU_CHUNK)]

    def body(dy_ref, dyn_ref, x_ref, xp_ref, w_ref, wup_ref, dn2_ref, dx_ref, dw_ref, acc_ref):
        i, k = pl.program_id(0), pl.program_id(1)

        @pl.when(k == 0)
        def _():
            acc_ref[...] = jnp.zeros_like(acc_ref)

        last, first = i == nt - 1, i == 0
        for c0, cw in chunks:
            cs = slice(c0, c0 + cw)
            yy = jnp.concatenate([dy_ref[:, cs].astype(F32), jnp.where(last, 0.0, dyn_ref[:, cs].astype(F32))], 0)
            w = w_ref[:, cs]
            dx = (w[0:1] * pltpu.roll(yy, tl + HALO_FFN - 2, 0)[:tl] + w[1:2] * pltpu.roll(yy, tl + HALO_FFN - 1, 0)[:tl]
                  + w[2:3] * yy[:tl]).astype(BF16)
            dx_ref[:, cs] = dx
            acc_ref[...] += _dot(dx, wup_ref[:, cs], NT)
            xx = jnp.concatenate([jnp.where(first, 0.0, xp_ref[:, cs].astype(F32)), x_ref[:, cs].astype(F32)], 0)
            dy = yy[:tl]
            dw_ref[0:1, cs] = jnp.sum(dy * pltpu.roll(xx, 2, 0)[HALO_FFN:], 0, keepdims=True)
            dw_ref[1:2, cs] = jnp.sum(dy * pltpu.roll(xx, 1, 0)[HALO_FFN:], 0, keepdims=True)
            dw_ref[2:3, cs] = jnp.sum(dy * xx[HALO_FFN:], 0, keepdims=True)

        @pl.when(k == C // tc - 1)
        def _():
            dn2_ref[...] = acc_ref[...]

    tile = pl.BlockSpec((tl, tc), lambda i, k: (i, k))
    per = tl // HALO_FFN
    return pl.pallas_call(
        body, name="bwd_ffn_conv_up", grid=(nt, C // tc),
        in_specs=[tile, pl.BlockSpec((HALO_FFN, tc), lambda i, k: (jnp.minimum((i + 1) * per, nhb - 1), k)),
                  tile, pl.BlockSpec((HALO_FFN, tc), lambda i, k: (jnp.maximum(i * per - 1, 0), k)),
                  pl.BlockSpec((FFN_CONV_WIDTH, tc), lambda i, k: (0, k)),
                  pl.BlockSpec((None, D_MODEL, tc), lambda i, k: (k, 0, 0))],
        out_specs=[pl.BlockSpec((tl, D_MODEL), lambda i, k: (i, 0)), tile,
                   pl.BlockSpec((None, FFN_CONV_WIDTH, tc), lambda i, k: (i, 0, k))],
        out_shape=[_sds((L, D_MODEL), F32), _sds((L, C), BF16), _sds((nt, FFN_CONV_WIDTH, C), F32)],
        scratch_shapes=[pltpu.VMEM((tl, D_MODEL), F32)],
        compiler_params=_params(2))(dup, dup, up0, up0, fcw, wup)


def _bwd_mix(dh2, dn2, h1, gffn, wout, o, ga, u1, lg, lb, og):
    L = h1.shape[0]

    def body(dh2_ref, dn2_ref, h1_ref, gffn_ref, wout_ref, o_ref, ga_ref, u1_ref, lg_ref, lb_ref, og_ref,
             dh1_ref, dh1m_ref, do_ref, delta_ref, du1_ref, dgffn_ref, dga_ref, dog_ref, dlg_ref, dlb_ref):
        h1 = h1_ref[...]
        dn2x, dgffn = _rms_bwd(dn2_ref[...], h1, _rms_r(h1, D_MODEL), gffn_ref[...], D_MODEL)
        dh1 = dh2_ref[...] + dn2x
        dh1_ref[...] = dh1
        dh1m = jnp.where(_row_ids(TM) >= DEAD, dh1, 0.0).astype(BF16)
        dh1m_ref[...] = dh1m
        dmix = _dot(dh1m, wout_ref[...], NT)
        dma, dmb = dmix[:, :D_CONV], dmix[:, D_CONV:]
        of = o_ref[...].astype(F32)
        do, dga = _rms_bwd(dmb, of, _rms_r(of, N_HEADS * V_HEAD), ga_ref[...], N_HEADS * V_HEAD)
        do_ref[...] = do.astype(BF16)
        prod = do * of
        by_lane = _scatter_stats([jnp.sum(prod[:, h * HB:(h + 1) * HB], -1, keepdims=True) for h in range(N_HEADS)], TM)
        by_row = by_lane.T
        for grp in range(N_GROUPS):
            delta_ref[grp] = by_row[grp * LANES:grp * LANES + SUBLANES, :]
        lg = lg_ref[...]
        rs, xh, u2, sg, u = _ln_silu(u1_ref[...], lg, lb_ref[...])
        du, dog = _rms_bwd(dma, u, _rms_r(u, D_CONV), og_ref[...], D_CONV)
        du2 = du * (sg * (1.0 + u2 * (1.0 - sg)))
        dxh = du2 * lg
        du1_ref[...] = rs * (dxh - jnp.mean(dxh, -1, keepdims=True) - xh * jnp.mean(dxh * xh, -1, keepdims=True))

        @pl.when(pl.program_id(0) == 0)
        def _():
            for ref in (dgffn_ref, dga_ref, dog_ref, dlg_ref, dlb_ref):
                ref[...] = jnp.zeros_like(ref)

        dgffn_ref[...] += dgffn
        dga_ref[...] += dga
        dog_ref[...] += dog
        dlg_ref[...] += jnp.sum(du2 * xh, 0, keepdims=True)
        dlb_ref[...] += jnp.sum(du2, 0, keepdims=True)

    return pl.pallas_call(
        body, name="bwd_mix", grid=(L // TM,),
        in_specs=[_rows(TM, D_MODEL), _rows(TM, D_MODEL), _rows(TM, D_MODEL), _full(gffn.shape), _full(wout.shape),
                  _rows(TM, D_HEADS), _full(ga.shape), _rows(TM, D_CONV), _full(lg.shape), _full(lb.shape),
                  _full(og.shape)],
        out_specs=[_rows(TM, D_MODEL), _rows(TM, D_MODEL), _rows(TM, D_HEADS),
                   pl.BlockSpec((N_GROUPS, SUBLANES, TM), lambda i: (0, 0, i)),
                   _rows(TM, D_CONV), _full((1, D_MODEL)), _full((1, D_HEADS)), _full((1, D_CONV)),
                   _full((1, D_CONV)), _full((1, D_CONV))],
        out_shape=[_sds((L, D_MODEL), F32), _sds((L, D_MODEL), BF16), _sds((L, D_HEADS), BF16),
                   _sds((N_GROUPS, SUBLANES, L), F32),
                   _sds((L, D_CONV), F32), _sds((1, D_MODEL), F32), _sds((1, D_HEADS), F32), _sds((1, D_CONV), F32),
                   _sds((1, D_CONV), F32), _sds((1, D_CONV), F32)],
        compiler_params=_params(1))(dh2, dn2, h1, gffn, wout, o, ga, u1, lg, lb, og)


def _attn_bwd(q, kv, kr, do, lse, delta, parts=()):
    L = q.shape[0]
    t = TM
    nt = L // t
    gw = HEAD_GROUP * HB
    n = len(parts)

    def body(q_ref, kv_ref, kr_ref, do_ref, lse_ref, delta_ref, *refs):
        dq_ref, dkv_ref, dkr_ref = refs[n:n + 3]
        dqt_acc, dk_acc, dv_acc, kkt_scr = refs[2 * n + 3:2 * n + 7]
        exchange_refs = (refs[:n], refs[n + 3:2 * n + 3]) + refs[2 * n + 7:]
        g, j = pl.program_id(0), pl.program_id(1)
        if n:
            @pl.when((g == 0) & (j == 0))
            def _():
                for cp in _chip_copies(*exchange_refs)[0]:
                    cp.start()

        lane = lax.broadcasted_iota(jnp.int32, (t, HB), 1)

        @pl.when(j == 0)
        def _():
            dqt_acc[...] = jnp.zeros_like(dqt_acc)

        @pl.when((j == 0) & (g == 0))
        def _():
            dkr_ref[...] = jnp.zeros_like(dkr_ref)

        dk_acc[...] = jnp.zeros_like(dk_acc)
        dv_acc[...] = jnp.zeros_like(dv_acc)
        krj = kr_ref[...]
        heads = range(HEAD_GROUP)
        cols = [slice(h * HB, (h + 1) * HB) for h in heads]
        for hc in cols:
            kkt_scr[hc, :] = jnp.where(lane < QK_NOPE, kv_ref[:, hc], krj).astype(F32).T.astype(BF16)

        def tile(i, vis, whole=True):
            qs = pl.ds(pl.multiple_of(i * t, t), t)
            keys = slice(None) if whole else slice(DEAD, t)
            kvj = [kv_ref[keys, hc] for hc in cols]
            lane_k = lane[:kvj[0].shape[0]]
            kk = [jnp.where(lane_k < QK_NOPE, kvj[h], krj[keys]) for h in heads]
            qi = [q_ref[qs, hc] for hc in cols]
            doi = [do_ref[qs, hc] for hc in cols]
            s = [_dot(kk[h], qi[h].astype(F32).T.astype(BF16), NN) for h in heads]
            dp = [_dot(kvj[h], doi[h].astype(F32).T.astype(BF16), NN) for h in heads]
            p = []
            for h in heads:
                sh = s[h] if vis is None else jnp.where(vis, s[h], NEG)
                p.append(jnp.exp2(sh - lse_ref[h:h + 1, qs]))
            for h in heads:
                dv_acc[keys, cols[h]] += _dot(p[h].astype(BF16), doi[h], NN)
            ds = [(p[h] * (dp[h] - delta_ref[h:h + 1, qs]) * LN2).astype(BF16) for h in heads]
            for h in heads:
                dk_acc[keys, cols[h]] += _dot(ds[h], qi[h], NN)
            for h in heads:
                dqt = _dot(kkt_scr[cols[h], :], ds[h], NN) if whole else _dot(kk[h], ds[h], TN)
                dqt_acc[cols[h], qs] += dqt

        @pl.when(j == 0)
        def _():
            tile(0, _visible_t(0, 0, t)[DEAD:], whole=False)

            def meta_keys(i, carry):
                tile(i, None, whole=False)
                return carry

            lax.fori_loop(1, nt, meta_keys, 0)

        @pl.when(j > 0)
        def _():
            tile(j, _visible_t(j, j, t))

            def unmasked(i, carry):
                tile(i, None)
                return carry

            lax.fori_loop(j + 1, nt, unmasked, 0)

        dkr = jnp.zeros((t, HB), F32)
        for h in range(HEAD_GROUP):
            hc = slice(h * HB, (h + 1) * HB)
            dk = dk_acc[:, hc]
            dkv_ref[:, hc] = jnp.where(lane < QK_NOPE, dk, dv_acc[:, hc]).astype(BF16)
            dkr = dkr + jnp.where(lane >= QK_NOPE, dk, 0.0)
        dkr_ref[pl.ds(pl.multiple_of(j * t, t), t), :] += dkr

        @pl.when(j == nt - 1)
        def _():
            def untranspose(i, carry):
                qs = pl.ds(pl.multiple_of(i * t, t), t)
                dq_ref[qs, :] = (dqt_acc[:, qs].T * Q_SCALE).astype(BF16)
                return carry

            lax.fori_loop(0, nt, untranspose, 0)

        if n:
            @pl.when((g == N_GROUPS - 1) & (j == nt - 1))
            def _():
                sends, arrivals = _chip_copies(*exchange_refs)
                for cp in arrivals:
                    cp.wait_recv()
                for cp in sends:
                    cp.wait_send()

    group = lambda g, j: (0, g)
    stats = _resident((None, SUBLANES, L), lambda g, j: (g, 0, 0))
    any_spec = pl.BlockSpec(memory_space=pl.ANY)
    outs = pl.pallas_call(
        body, name="attn_bwd", grid=(N_GROUPS, nt),
        in_specs=[_resident((L, gw), group), pl.BlockSpec((t, gw), lambda g, j: (j, g)),
                  pl.BlockSpec((t, HB), lambda g, j: (j, 0)), _resident((L, gw), group), stats, stats]
        + [any_spec] * n,
        out_specs=[pl.BlockSpec((L, gw), group), pl.BlockSpec((t, gw), lambda g, j: (j, g)),
                   pl.BlockSpec((L, HB), lambda g, j: (0, 0))] + [any_spec] * n,
        out_shape=[_sds((L, D_HEADS), BF16), _sds((L, D_HEADS), BF16), _sds((L, HB), F32)]
        + [_sds(p.shape, p.dtype) for p in parts],
        scratch_shapes=[pltpu.VMEM((gw, L), F32), pltpu.VMEM((t, gw), F32), pltpu.VMEM((t, gw), F32),
                        pltpu.VMEM((gw, t), BF16)] + (_chip_semaphores(n) if n else []),
        compiler_params=_params(2))(q, kv, kr, do, lse, delta, *parts)
    return outs[0], outs[1], outs[2], list(outs[3:])


def _bwd_conv(du1, u0, cw, zag):
    L = du1.shape[0]
    nt = L // TM

    def body(dy_ref, dyn_ref, x_ref, xp_ref, cw_ref, zag_ref, dzag_ref, dcw_ref, dcb_ref):
        i = pl.program_id(0)
        dy = dy_ref[...]
        yy = jnp.concatenate([dy, jnp.where(i < nt - 1, dyn_ref[...], 0.0)], 0)
        du0 = _conv_taps(yy, cw_ref, HALO_CONV, TM, True)
        zag = zag_ref[...].astype(F32)
        a, sg = zag[:, :D_CONV], jax.nn.sigmoid(zag[:, D_CONV:])
        dzag_ref[...] = jnp.concatenate([du0 * sg, du0 * a * sg * (1.0 - sg)], 1).astype(BF16)
        xx = jnp.concatenate([jnp.where(i > 0, xp_ref[...], 0.0), x_ref[...]], 0)

        @pl.when(i == 0)
        def _():
            dcw_ref[...] = jnp.zeros_like(dcw_ref)
            dcb_ref[...] = jnp.zeros_like(dcb_ref)

        rolls = {}
        for k in range(CONV_WIDTH):
            xs = _shifted_rows(xx, CONV_WIDTH - 1 - k, HALO_CONV, TM, False, rolls)
            dcw_ref[k:k + 1, :] += jnp.sum(dy * xs, 0, keepdims=True)
        dcb_ref[...] += jnp.sum(dy, 0, keepdims=True)

    return pl.pallas_call(
        body, name="bwd_conv", grid=(nt,),
        in_specs=[_rows(TM, D_CONV), _next(HALO_CONV, D_CONV, TM, L // HALO_CONV), _rows(TM, D_CONV),
                  _prev(HALO_CONV, D_CONV, TM), _full(cw.shape), _rows(TM, D_AG)],
        out_specs=[_rows(TM, D_AG), _full(cw.shape), _full((1, D_CONV))],
        out_shape=[_sds((L, D_AG), BF16), _sds(cw.shape, F32), _sds((1, D_CONV), F32)],
        compiler_params=_params(1))(du1, du1, u0, u0, cw, zag)


def _bwd_in(dzag, dq, dkv, dkr, cq, ckv, gq, gkv, wuq, wukv, win, rc, rs1, rs2, h0, gmix, dh1):
    L = h0.shape[0]

    def body(dzag_ref, dq_ref, dkv_ref, dkr_ref, cq_ref, ckv_ref, gq_ref, gkv_ref, wuq_ref, wukv_ref, win_ref,
             c_ref, s1_ref, s2_ref, h0_ref, gmix_ref, dh1_ref,
             dz_ref, dqr_ref, gx_ref, dfirst_ref, dgq_ref, dgkv_ref, dgmix_ref):
        i = pl.program_id(0)
        c, s1, s2 = c_ref[...], s1_ref[...], s2_ref[...]
        dqr = _rope_bwd(dq_ref[...].astype(F32), jnp.tile(c, (1, N_HEADS)), jnp.tile(s1, (1, N_HEADS)),
                        jnp.tile(s2, (1, N_HEADS))).astype(BF16)
        dqr_ref[...] = dqr
        cq, ckv = cq_ref[...], ckv_ref[...]
        dcq, dgq = _rms_bwd(_dot(dqr, wuq_ref[...], NT), cq, _rms_r(cq, Q_LORA), gq_ref[...], Q_LORA)
        dckv, dgkv = _rms_bwd(_dot(dkv_ref[...], wukv_ref[...], NT), ckv, _rms_r(ckv, KV_LORA), gkv_ref[...], KV_LORA)
        dkrp = _rope_bwd(dkr_ref[...], c, s1, s2)
        dz = jnp.concatenate([dzag_ref[...], dcq.astype(BF16), dckv.astype(BF16), dkrp.astype(BF16)], 1)
        dz_ref[...] = dz
        h0 = h0_ref[...]
        dnx, dgmix = _rms_bwd(_dot(dz, win_ref[...], NT), h0, _rms_r(h0, D_MODEL), gmix_ref[...], D_MODEL)
        dh0 = dh1_ref[...] + dnx

        @pl.when(i == 0)
        def _():
            dfirst_ref[...] = dh0
            for ref in (dgq_ref, dgkv_ref, dgmix_ref):
                ref[...] = jnp.zeros_like(ref)

        @pl.when(i > 0)
        def _():
            gx_ref[...] = dh0

        dgq_ref[...] += dgq
        dgkv_ref[...] += dgkv
        dgmix_ref[...] += dgmix

    return pl.pallas_call(
        body, name="bwd_in", grid=(L // TM,),
        in_specs=[_rows(TM, D_AG), _rows(TM, D_HEADS), _rows(TM, D_HEADS), _rows(TM, HB), _rows(TM, Q_LORA),
                  _rows(TM, KV_LORA), _full(gq.shape), _full(gkv.shape), _full(wuq.shape), _full(wukv.shape),
                  _full(win.shape), _rows(TM, HB), _rows(TM, HB), _rows(TM, HB), _rows(TM, D_MODEL),
                  _full(gmix.shape), _rows(TM, D_MODEL)],
        out_specs=[_rows(TM, D_ZP), _rows(TM, D_HEADS),
                   pl.BlockSpec((TM, D_MODEL), lambda i: (jnp.maximum(i - 1, 0), 0)), _full((TM, D_MODEL)),
                   _full((1, Q_LORA)), _full((1, KV_LORA)), _full((1, D_MODEL))],
        out_shape=[_sds((L, D_ZP), BF16), _sds((L, D_HEADS), BF16), _sds((L - TM, D_MODEL), F32),
                   _sds((TM, D_MODEL), F32), _sds((1, Q_LORA), F32), _sds((1, KV_LORA), F32), _sds((1, D_MODEL), F32)],
        compiler_params=_params(1))(dzag, dq, dkv, dkr, cq, ckv, gq, gkv, wuq, wukv, win, rc, rs1, rs2, h0, gmix, dh1)


def _mesh_pos():
    return lax.axis_index("x"), lax.axis_index("y"), lax.axis_index("c")


def _all_gather(shards):
    n = len(shards)

    def body(*refs):
        sends, arrivals, forwards, finals = _gather_copies(refs[:n], refs[n:2 * n], *refs[2 * n:])
        for cp in sends:
            cp.start()
        for landed, onward in zip(arrivals, forwards):
            landed.wait_recv()
            onward.start()
        for cp in finals:
            cp.wait_recv()
        for cp in sends + forwards:
            cp.wait_send()

    any_spec = pl.BlockSpec(memory_space=pl.ANY)
    outs = pl.pallas_call(
        body, name="all_gather_weights", in_specs=[any_spec] * n, out_specs=[any_spec] * n,
        out_shape=_gather_out_shapes(shards), scratch_shapes=_gather_semaphores(n))(*shards)
    return _gathered(outs, shards)


def _gather_out_shapes(shards):
    return [_sds((2 * N_CHIPS, s.shape[0] // 2) + s.shape[1:], s.dtype) for s in shards]


def _gather_semaphores(n):
    return [pltpu.SemaphoreType.DMA((n, 8)), pltpu.SemaphoreType.DMA((n, 8))]


def _gathered(outs, shards):
    return [o.reshape((N_CHIPS, s.shape[0]) + s.shape[1:]) for o, s in zip(outs, shards)]


def _gather_copies(ins, outs, send_sems, recv_sems):
    x, y, c = _mesh_pos()
    chips = [(1 - x, y), (x, 1 - y), (1 - x, 1 - y)]
    sends, arrivals, forwards, finals = [], [], [], []

    def copy(src, dst, a, k, to):
        return pltpu.make_async_remote_copy(src, dst, send_sems.at[a, k], recv_sems.at[a, k], device_id=to,
                                            device_id_type=MESH)

    for a, (src, out) in enumerate(zip(ins, outs)):
        m = out.shape[1]
        mine = src.at[pl.ds(pl.multiple_of(c * m, 16), m)]
        for hf in range(2):
            own = out.at[4 * x + 2 * y + hf]
            sends.append(copy(src.at[pl.ds(hf * m, m)], own, a, 6 + hf, (x, y, 1 - c)))
            finals.append(copy(own, own, a, 6 + hf, (x, y, 1 - c)))
        for k, chip in enumerate(chips):
            slot = 4 * chip[0] + 2 * chip[1]
            sends.append(copy(mine, out.at[4 * x + 2 * y + c], a, k, (*chip, c)))
            arrivals.append(copy(out.at[slot + c], out.at[slot + c], a, k, (*chip, c)))
            forwards.append(copy(out.at[slot + c], out.at[slot + c], a, 3 + k, (x, y, 1 - c)))
            finals.append(copy(out.at[slot + 1 - c], out.at[slot + 1 - c], a, 3 + k, (x, y, 1 - c)))
    return sends, arrivals, forwards, finals


def _sibling_exchange(parts, name):
    n = len(parts)

    def body(*refs):
        ins, theirs = refs[:n], refs[n:2 * n]
        send_sems, recv_sems = refs[2 * n:]
        x, y, c = _mesh_pos()
        copies = []
        for a in range(n):
            h = parts[a].shape[1] // 2
            rows = pl.ds(pl.multiple_of((1 - c) * h, 16), h)
            copies += [pltpu.make_async_remote_copy(ins[a].at[q, rows], theirs[a].at[q], send_sems.at[a, q],
                                                    recv_sems.at[a, q], device_id=(x, y, 1 - c), device_id_type=MESH)
                       for q in range(N_CHIPS)]
        for cp in copies:
            cp.start()
        for cp in copies:
            cp.wait()

    any_spec = pl.BlockSpec(memory_space=pl.ANY)
    return pl.pallas_call(
        body, name=name, in_specs=[any_spec] * n, out_specs=[any_spec] * n,
        out_shape=[_sds((N_CHIPS, p.shape[1] // 2, p.shape[2]), p.dtype) for p in parts],
        scratch_shapes=[pltpu.SemaphoreType.DMA((n, N_CHIPS)), pltpu.SemaphoreType.DMA((n, N_CHIPS))])(*parts)


def _chip_exchange(parts, name):
    n = len(parts)

    def body(*refs):
        sends, arrivals = _chip_copies(refs[:n], refs[n:2 * n], *refs[2 * n:])
        for cp in sends:
            cp.start()
        for cp in arrivals:
            cp.wait_recv()
        for cp in sends:
            cp.wait_send()

    any_spec = pl.BlockSpec(memory_space=pl.ANY)
    return pl.pallas_call(
        body, name=name, in_specs=[any_spec] * n, out_specs=[any_spec] * n,
        out_shape=[_sds(p.shape, p.dtype) for p in parts], scratch_shapes=_chip_semaphores(n))(*parts)


def _chip_semaphores(n):
    return [pltpu.SemaphoreType.DMA((n, 3)), pltpu.SemaphoreType.DMA((n, 3))]


def _chip_copies(ins, outs, send_sems, recv_sems):
    x, y, c = _mesh_pos()
    me = 2 * x + y
    sends, arrivals = [], []
    for a, (src, out) in enumerate(zip(ins, outs)):
        for k, chip in enumerate([(1 - x, y), (x, 1 - y), (1 - x, 1 - y)]):
            slot = 2 * chip[0] + chip[1]
            sems = dict(send_sem=send_sems.at[a, k], recv_sem=recv_sems.at[a, k], device_id=(*chip, c),
                        device_id_type=MESH)
            sends.append(pltpu.make_async_remote_copy(src.at[slot], out.at[me], **sems))
            arrivals.append(pltpu.make_async_remote_copy(out.at[slot], out.at[slot], **sems))
    return sends, arrivals


def _sibling_gather(parts, name):
    n = len(parts)

    def body(*refs):
        ins, outs = refs[:n], refs[n:2 * n]
        send_sems, recv_sems = refs[2 * n:]
        x, y, c = _mesh_pos()
        copies = [pltpu.make_async_remote_copy(ins[a], outs[a], send_sems.at[a], recv_sems.at[a],
                                               device_id=(x, y, 1 - c), device_id_type=MESH) for a in range(n)]
        for cp in copies:
            cp.start()
        for cp in copies:
            cp.wait()

    any_spec = pl.BlockSpec(memory_space=pl.ANY)
    return pl.pallas_call(
        body, name=name, in_specs=[any_spec] * n, out_specs=[any_spec] * n,
        out_shape=[_sds(p.shape, p.dtype) for p in parts],
        scratch_shapes=[pltpu.SemaphoreType.DMA((n,)), pltpu.SemaphoreType.DMA((n,))])(*parts)


def _row_tile(rows, row_bytes, align, budget=1 << 20):
    best = None
    for t in range(align, rows + 1, align):
        if rows % t == 0 and t * row_bytes <= budget:
            best = t
    return best or rows


def _scalar(v):
    return jnp.reshape(v, (1,)).astype(jnp.int32)


def _add_pair(part, theirs, c, name):
    _, h, cols = theirs.shape
    tr = _row_tile(h, cols * 4, 16)
    nb = h // tr

    def body(c_ref, a_ref, b_ref, o_ref):
        o_ref[...] = (a_ref[...].astype(F32) + b_ref[...].astype(F32)).astype(o_ref.dtype)

    half = pl.BlockSpec((None, tr, cols), lambda q, i, c_ref: (q, i, 0))
    grid_spec = pltpu.PrefetchScalarGridSpec(
        num_scalar_prefetch=1, grid=(N_CHIPS, nb),
        in_specs=[pl.BlockSpec((None, tr, cols), lambda q, i, c_ref: (q, c_ref[0] * nb + i, 0)), half],
        out_specs=half)
    return pl.pallas_call(body, name=name, grid_spec=grid_spec, out_shape=_sds(theirs.shape, part.dtype),
                          compiler_params=_params(2))(_scalar(c), part, theirs)


def _add_chips(got, own, me, name):
    _, h, cols = got.shape
    tr = _row_tile(h, cols * 4 * N_CHIPS, 16, budget=1 << 21)

    def body(me_ref, got_ref, own_ref, o_ref):
        acc = None
        for q in range(N_CHIPS):
            term = jnp.where(me_ref[0] == q, own_ref[q], got_ref[q]).astype(F32)
            acc = term if acc is None else acc + term
        o_ref[...] = acc

    by_chip = pl.BlockSpec((N_CHIPS, tr, cols), lambda i, me_ref: (0, i, 0))
    grid_spec = pltpu.PrefetchScalarGridSpec(
        num_scalar_prefetch=1, grid=(h // tr,), in_specs=[by_chip, by_chip],
        out_specs=pl.BlockSpec((tr, cols), lambda i, me_ref: (i, 0)))
    return pl.pallas_call(body, name=name, grid_spec=grid_spec, out_shape=_sds((h, cols), F32),
                          compiler_params=_params(1))(_scalar(me), got, own)


def _reduce_begin(parts, tag):
    c = lax.axis_index("c")
    theirs = _sibling_exchange(parts, f"grad_sibling_exchange_{tag}")
    return [_add_pair(p, t, c, f"grad_add_pair_{tag}_{a}") for a, (p, t) in enumerate(zip(parts, theirs))]


def _reduce_end(pair, got, tag):
    x, y, c = _mesh_pos()
    half = [_add_chips(g, p, 2 * x + y, f"grad_add_chips_{tag}_{a}") for a, (g, p) in enumerate(zip(got, pair))]
    other = _sibling_gather(half, f"grad_sibling_gather_{tag}")
    return [jnp.where(c == 0, jnp.concatenate([h, o], 0), jnp.concatenate([o, h], 0)) for h, o in zip(half, other)]


def _adamw_math(w, g, m, v):
    m = ADAM_B1 * m + (1.0 - ADAM_B1) * g
    v = ADAM_B2 * v + (1.0 - ADAM_B2) * (g * g)
    m_hat = m / (1.0 - ADAM_B1 ** ADAM_STEP)
    v_hat = v / (1.0 - ADAM_B2 ** ADAM_STEP)
    return -ADAM_LR * (m_hat / (jnp.sqrt(v_hat) + ADAM_EPS) + ADAM_WD * w), m, v


def _adamw_big(w, g, m, v, name):
    r, c = w.shape
    tr = _row_tile(r, c * 4, 8, budget=1 << 19)

    def body(w_ref, g_ref, m_ref, v_ref, d_ref, mo_ref, vo_ref):
        d_ref[...], mo_ref[...], vo_ref[...] = _adamw_math(w_ref[...], g_ref[...], m_ref[...], v_ref[...])

    return pl.pallas_call(
        body, name=name, grid=(r // tr,), in_specs=[_rows(tr, c)] * 4, out_specs=[_rows(tr, c)] * 3,
        out_shape=[_sds((r, c), F32)] * 3, compiler_params=_params(1))(w, g, m, v)


def _adamw_small(ws, gs, ms, vs):
    n = len(ws)

    def body(*refs):
        for a in range(n):
            w_ref, g_ref, m_ref, v_ref = (refs[k * n + a] for k in range(4))
            d, m, v = _adamw_math(w_ref[...], g_ref[...], m_ref[...], v_ref[...])
            refs[4 * n + a][...] = d
            refs[5 * n + a][...] = m
            refs[6 * n + a][...] = v

    vm = pl.BlockSpec(memory_space=pltpu.VMEM)
    outs = pl.pallas_call(
        body, name="adamw_small", in_specs=[vm] * (4 * n), out_specs=[vm] * (3 * n),
        out_shape=[_sds(w.shape, F32) for w in ws] * 3)(*ws, *gs, *ms, *vs)
    return outs[:n], outs[n:2 * n], outs[2 * n:]


BIG = ("w_in", "w_uq", "w_ukv", "w_out", "w_ffn_up", "w_ffn_down")
SMALL_SHARDED = ("conv_w", "ffn_conv_w", "meta_tokens")
REPLICATED = ("mix_norm_g", "q_norm_g", "kv_norm_g", "conv_b", "conv_ln_g", "conv_ln_b", "conv_out_g", "attn_out_g",
              "ffn_norm_g", "ffn_conv_b", "final_norm_g")
WEIGHTS = ("meta_tokens", "mix_norm_g", "w_in", "q_norm_g", "w_uq", "kv_norm_g", "w_ukv", "conv_w", "conv_b",
           "conv_ln_g", "conv_ln_b", "conv_out_g", "attn_out_g", "w_out", "ffn_norm_g", "w_ffn_up", "ffn_conv_w",
           "ffn_conv_b", "w_ffn_down", "final_norm_g")


def _lane_rows(a):
    return a.reshape(N_CHIPS, -1, LANES)


def _col_shards(a):
    k = a.shape[0]
    return a.reshape(k, N_CHIPS, -1).transpose(1, 0, 2)


def _from_col_shards(a):
    return a.transpose(1, 0, 2).reshape(a.shape[1], -1)


def _pad_rows_to(a, rows):
    return jnp.pad(a, ((0, 0), (0, rows - a.shape[1]), (0, 0)))


def _rope_tables(L):
    pos = (jnp.arange(L, dtype=jnp.int32) - DEAD).astype(F32)
    inv_freq = 1.0 / (ROPE_THETA ** (jnp.arange(0, QK_ROPE, 2, dtype=F32) / QK_ROPE))
    ang = pos[:, None] * inv_freq[None, :]
    cos, sin = jnp.cos(ang), jnp.sin(ang)
    half = QK_ROPE // 2
    z = lambda n: jnp.zeros((L, n), F32)
    rc = jnp.concatenate([jnp.ones((L, QK_NOPE), F32), cos, cos, z(HB - QK_NOPE - QK_ROPE)], 1)
    rs1 = jnp.concatenate([z(QK_NOPE), -sin, z(HB - QK_NOPE - half)], 1)
    rs2 = jnp.concatenate([z(QK_NOPE + half), sin, z(HB - QK_NOPE - QK_ROPE)], 1)
    return rc, rs1, rs2


def _pad_heads(g):
    return jnp.pad(g.reshape(N_HEADS, V_HEAD), ((0, 0), (HB - V_HEAD, 0))).reshape(1, D_HEADS)


def _unpad_heads(g):
    return g.reshape(N_HEADS, HB)[:, HB - V_HEAD:].reshape(1, N_HEADS * V_HEAD)


def _local_step(x, target, w, late_shards=None, reduce_first=False):
    S = x.shape[0]
    L = TM + S
    tl = L // 4
    d_qk = QK_NOPE + QK_ROPE
    win_n = w["w_in"]
    kr0 = D_AG + Q_LORA + KV_LORA
    win = jnp.concatenate([win_n[:, :kr0], jnp.zeros((D_MODEL, QK_NOPE), BF16), win_n[:, kr0:],
                           jnp.zeros((D_MODEL, HB - d_qk), BF16)], 1)
    wuq = jnp.pad(w["w_uq"].reshape(Q_LORA, N_HEADS, d_qk), ((0, 0), (0, 0), (0, HB - d_qk))).reshape(Q_LORA, D_HEADS)
    wukv = w["w_ukv"]
    ga = _pad_heads(w["attn_out_g"])
    gfin = w["final_norm_g"].reshape(1, D_MODEL)
    rc, rs1, rs2 = _rope_tables(L)
    head = jnp.concatenate([jnp.zeros((DEAD, D_MODEL), F32), w["meta_tokens"]], 0)

    h0, n, zag, u0, cq, ckv, qn, kvn, q, kv, kr, kvt = _fwd_in(x, head, w["mix_norm_g"], win, w["q_norm_g"], wuq,
                                                                w["kv_norm_g"], wukv, rc, rs1, rs2)
    u1, mixa = _fwd_conv(u0, w["conv_w"], w["conv_b"], w["conv_ln_g"], w["conv_ln_b"], w["conv_out_g"])
    if late_shards is None:
        o, lse, _ = _attn_fwd(q, kv, kvt, kr)
        wout_n, wup, wdown = w["w_out"], _col_shards(w["w_ffn_up"]), w["w_ffn_down"]
    else:
        o, lse, late = _attn_fwd(q, kv, kvt, kr, [late_shards[k] for k in LATE])
        wout_n, wup, wdown = _full_weight("w_out", late[0]), late[1], _full_weight("w_ffn_down", late[2])
    wout = jnp.concatenate([wout_n[:D_CONV], jnp.pad(wout_n[D_CONV:].reshape(N_HEADS, V_HEAD, D_MODEL),
                                                     ((0, 0), (HB - V_HEAD, 0), (0, 0))).reshape(D_HEADS, D_MODEL)], 0)
    mix, h1, n2 = _fwd_mix(h0, mixa, o, ga, wout, w["ffn_norm_g"])
    up0 = _mm(n2, wup, NN, BF16, tl, FF_TILE, D_MODEL, "ffn_up", b_slabs=True)
    dh2, loss, g_fin, up, act = _fwd_ffn_loss(up0, w["ffn_conv_w"], w["ffn_conv_b"], wdown, h1, target, gfin)

    dup, g_fcb = _bwd_ffn_act(dh2, up, wdown)
    dn2, dup0, g_fcw_tiles = _bwd_ffn_conv_up(dup, up0, w["ffn_conv_w"], wup, tl)
    g_fcw = jnp.sum(g_fcw_tiles, 0)
    g_wup = _mm(n2, dup0, TN, BF16, D_MODEL, FF_TILE, tl, "ffn_up_dw", by_col_tile=True)
    g_wdown = _mm(act, dh2, TN, BF16, D_FF // 2, D_MODEL, tl, "ffn_down_dw").reshape(N_CHIPS, -1, D_MODEL)
    dh1, dh1m, do, delta, du1, g_gffn, g_ga, g_og, g_lg, g_lb = _bwd_mix(
        dh2, dn2, h1, w["ffn_norm_g"], wout, o, ga, u1, w["conv_ln_g"], w["conv_ln_b"], w["conv_out_g"])
    g_wout = _mm(mix, dh1m, TN, BF16, D_MIX // 2, D_MODEL, tl, "out_dw")
    g_wout = jnp.concatenate([g_wout[:D_CONV], g_wout[D_CONV:].reshape(N_HEADS, HB, D_MODEL)[:, HB - V_HEAD:]
                              .reshape(N_HEADS * V_HEAD, D_MODEL)], 0).reshape(N_CHIPS, -1, D_MODEL)
    pair = _reduce_begin([g_wup, g_wdown, g_wout], "first") if reduce_first else ()
    dq, dkv, dkr, got = _attn_bwd(q, kv, kr, do, lse, delta, pair)
    if reduce_first:
        g_wup, g_wdown, g_wout = _reduce_end(pair, got, "first")
    dzag, g_cw, g_cb = _bwd_conv(du1, u0, w["conv_w"], zag)
    dz, dqr, gx, dfirst, g_gq, g_gkv, g_gmix = _bwd_in(dzag, dq, dkv, dkr, cq, ckv, w["q_norm_g"], w["kv_norm_g"],
                                                      wuq, wukv, win, rc, rs1, rs2, h0, w["mix_norm_g"], dh1)
    g_win = _mm(n, dz, TN, BF16, D_MODEL, D_ZP // 2, tl, "in_dw")
    g_wuq = _mm(qn, dqr, TN, BF16, Q_LORA, D_HEADS, tl, "uq_dw")
    g_wukv = _mm(kvn, dkv, TN, BF16, KV_LORA, D_HEADS // N_CHIPS, tl, "ukv_dw", by_col_tile=True)

    grads = {
        "w_in": _col_shards(jnp.concatenate([g_win[:, :kr0], g_win[:, kr0 + QK_NOPE:kr0 + d_qk]], 1)),
        "w_uq": _col_shards(g_wuq.reshape(Q_LORA, N_HEADS, HB)[:, :, :d_qk].reshape(Q_LORA, N_HEADS * d_qk)),
        "w_ukv": g_wukv,
        "w_out": g_wout,
        "w_ffn_up": g_wup, "w_ffn_down": g_wdown, "conv_w": g_cw, "ffn_conv_w": g_fcw,
        "meta_tokens": dfirst[DEAD:], "mix_norm_g": g_gmix, "q_norm_g": g_gq, "kv_norm_g": g_gkv, "conv_b": g_cb,
        "conv_ln_g": g_lg, "conv_ln_b": g_lb, "conv_out_g": g_og, "attn_out_g": _unpad_heads(g_ga),
        "ffn_norm_g": g_gffn, "ffn_conv_b": g_fcb, "final_norm_g": g_fin,
    }
    return loss, gx, grads


ROW_SHARDED = ("w_out", "w_ffn_down")


REDUCED_FIRST = ("w_ffn_up", "w_ffn_down", "w_out")
LATE = ("w_out", "w_ffn_up", "w_ffn_down")


def _full_weight(name, by_chip):
    return by_chip.reshape(-1, by_chip.shape[-1]) if name in ROW_SHARDED else _from_col_shards(by_chip)


def kernel(x, meta_tokens, mix_norm_g, w_in, q_norm_g, w_uq, kv_norm_g, w_ukv, conv_w, conv_b, conv_ln_g, conv_ln_b, conv_out_g, attn_out_g, w_out, ffn_norm_g, w_ffn_up, ffn_conv_w, ffn_conv_b, w_ffn_down, final_norm_g, loss_target, m_meta_tokens, m_mix_norm_g, m_w_in, m_q_norm_g, m_w_uq, m_kv_norm_g, m_w_ukv, m_conv_w, m_conv_b, m_conv_ln_g, m_conv_ln_b, m_conv_out_g, m_attn_out_g, m_w_out, m_ffn_norm_g, m_w_ffn_up, m_ffn_conv_w, m_ffn_conv_b, m_w_ffn_down, m_final_norm_g, v_meta_tokens, v_mix_norm_g, v_w_in, v_q_norm_g, v_w_uq, v_kv_norm_g, v_w_ukv, v_conv_w, v_conv_b, v_conv_ln_g, v_conv_ln_b, v_conv_out_g, v_attn_out_g, v_w_out, v_ffn_norm_g, v_w_ffn_up, v_ffn_conv_w, v_ffn_conv_b, v_w_ffn_down, v_final_norm_g):
    local = dict(meta_tokens=meta_tokens, mix_norm_g=mix_norm_g, w_in=w_in[0], q_norm_g=q_norm_g, w_uq=w_uq[0],
                 kv_norm_g=kv_norm_g, w_ukv=w_ukv[0], conv_w=conv_w[0], conv_b=conv_b, conv_ln_g=conv_ln_g,
                 conv_ln_b=conv_ln_b, conv_out_g=conv_out_g, attn_out_g=attn_out_g, w_out=w_out[0],
                 ffn_norm_g=ffn_norm_g, w_ffn_up=w_ffn_up[0], ffn_conv_w=ffn_conv_w[0], ffn_conv_b=ffn_conv_b,
                 w_ffn_down=w_ffn_down[0], final_norm_g=final_norm_g.reshape(1, D_MODEL))
    ms = dict(zip(WEIGHTS, (m_meta_tokens, m_mix_norm_g, m_w_in, m_q_norm_g, m_w_uq, m_kv_norm_g, m_w_ukv, m_conv_w,
                            m_conv_b, m_conv_ln_g, m_conv_ln_b, m_conv_out_g, m_attn_out_g, m_w_out, m_ffn_norm_g,
                            m_w_ffn_up, m_ffn_conv_w, m_ffn_conv_b, m_w_ffn_down, m_final_norm_g)))
    vs = dict(zip(WEIGHTS, (v_meta_tokens, v_mix_norm_g, v_w_in, v_q_norm_g, v_w_uq, v_kv_norm_g, v_w_ukv, v_conv_w,
                            v_conv_b, v_conv_ln_g, v_conv_ln_b, v_conv_out_g, v_attn_out_g, v_w_out, v_ffn_norm_g,
                            v_w_ffn_up, v_ffn_conv_w, v_ffn_conv_b, v_w_ffn_down, v_final_norm_g)))

    small_flat = jnp.concatenate([local[k].reshape(-1) for k in SMALL_SHARDED]).reshape(-1, LANES)
    early = [k for k in BIG if k not in LATE]
    gathered = _all_gather([local[k].astype(BF16) for k in early] + [small_flat])
    full = {k: v for k, v in local.items() if k not in LATE}
    for name, g in zip(early, gathered[:len(early)]):
        full[name] = _full_weight(name, g)
    small = gathered[-1].reshape(N_CHIPS, -1)
    at = 0
    for name in SMALL_SHARDED:
        r, c = local[name].shape
        full[name] = _from_col_shards(small[:, at:at + r * c].reshape(N_CHIPS, r, c))
        at += r * c

    loss_row, grad_x, grads = _local_step(x[0], loss_target[0], full, {k: local[k].astype(BF16) for k in LATE},
                                          reduce_first=True)

    rest_big = [k for k in BIG if k not in REDUCED_FIRST]
    rep = jnp.concatenate([grads[k].reshape(-1) for k in REPLICATED] + [loss_row.reshape(-1)]).reshape(1, -1, LANES)
    small_pieces = [_lane_rows(_col_shards(grads[k])) for k in SMALL_SHARDED]
    small_pieces.append(jnp.broadcast_to(rep, (N_CHIPS,) + rep.shape[1:]))
    small_rows = sum(p.shape[1] for p in small_pieces)
    small_pack = _pad_rows_to(jnp.concatenate(small_pieces, 1), -(-small_rows // 32) * 32)
    pair = _reduce_begin([grads[k] for k in rest_big] + [small_pack], "rest")
    *rest_tot, small_tot = _reduce_end(pair, _chip_exchange(pair, "grad_chip_exchange_rest"), "rest")

    total = {k: grads[k] for k in REDUCED_FIRST}
    total.update(zip(rest_big, rest_tot))
    flat = small_tot.reshape(-1)
    at = 0
    for name in SMALL_SHARDED + REPLICATED:
        shape = local[name].shape
        size = shape[0] * shape[1]
        total[name] = flat[at:at + size].reshape(shape)
        at += -(-size // LANES) * LANES if name in SMALL_SHARDED else size
    loss = flat[at]

    delta, new_m, new_v = {}, {}, {}
    shape2 = lambda a, name: a.reshape(local[name].shape)
    for name in BIG:
        delta[name], new_m[name], new_v[name] = _adamw_big(local[name], total[name], shape2(ms[name], name),
                                                          shape2(vs[name], name), "adamw_" + name)
    rest = SMALL_SHARDED + REPLICATED
    ds, nms, nvs = _adamw_small([local[k] for k in rest], [total[k] for k in rest],
                                [shape2(ms[k], k) for k in rest], [shape2(vs[k], k) for k in rest])
    for k, d, nm, nv in zip(rest, ds, nms, nvs):
        delta[k], new_m[k], new_v[k] = d, nm, nv

    out_shape = dict(zip(WEIGHTS, (meta_tokens, mix_norm_g, w_in, q_norm_g, w_uq, kv_norm_g, w_ukv, conv_w, conv_b,
                                   conv_ln_g, conv_ln_b, conv_out_g, attn_out_g, w_out, ffn_norm_g, w_ffn_up,
                                   ffn_conv_w, ffn_conv_b, w_ffn_down, final_norm_g)))
    outs = [loss, grad_x[None]]
    for group in (total, delta, new_m, new_v):
        outs += [group[k].reshape(out_shape[k].shape) for k in WEIGHTS]
    return tuple(outs)
```

```python
import functools

import jax
import jax.numpy as jnp
from jax import lax
from jax.experimental import pallas as pl
from jax.experimental.pallas import tpu as pltpu

F32 = jnp.float32
BF16 = jnp.bfloat16

D_MODEL = 1024
D_CONV = 512
CONV_WIDTH = 31
N_HEADS = 8
QK_NOPE = 64
QK_ROPE = 32
V_HEAD = 64
Q_LORA = 384
KV_LORA = 256
D_FF = 2816
FFN_CONV_WIDTH = 3
CHUNK_SHIFT = 6
N_META = 16
ROPE_THETA = 10000.0
EPS = 1e-6
NEG = -1e30
ADAM_LR = 0.001
ADAM_B1 = 0.9
ADAM_B2 = 0.999
ADAM_EPS = 1e-08
ADAM_WD = 0.01
ADAM_STEP = 10

LANES = 128
SUBLANES = 8
HB = LANES
D_HEADS = N_HEADS * HB
TM = 256
DEAD = TM - N_META
D_AG = 2 * D_CONV
D_ZP = D_AG + Q_LORA + KV_LORA + HB
D_MIX = D_CONV + D_HEADS
LN2 = 0.6931471805599453
Q_SCALE = (QK_NOPE + QK_ROPE) ** -0.5 / LN2
HALO_CONV = 32
HALO_FFN = 16
FF_CHUNK = 256
FF_MXU_CHUNK = 256
FF_TILE = D_FF // 2
VMEM_LIMIT = 56 * 1024 * 1024
ADAMW_MAX_STEPS = 32
N_CHIPS = 4
HEAD_GROUP = 4
N_GROUPS = N_HEADS // HEAD_GROUP
MESH =pl.DeviceIdType.MESH


def _params(n_grid):
    return pltpu.CompilerParams(dimension_semantics=("arbitrary",) * n_grid, vmem_limit_bytes=VMEM_LIMIT)


def _rows(tm, c, off=0):
    return pl.BlockSpec((tm, c), lambda i: (i, off))


def _full(shape):
    return pl.BlockSpec(shape, lambda i: (0,) * len(shape))


def _prev(hb, c, tm, off=0):
    return pl.BlockSpec((hb, c), lambda i: (jnp.maximum(i * (tm // hb) - 1, 0), off))


def _next(hb, c, tm, nblk, off=0):
    return pl.BlockSpec((hb, c), lambda i: (jnp.minimum((i + 1) * (tm // hb), nblk - 1), off))


def _sds(shape, dtype):
    return jax.ShapeDtypeStruct(shape, dtype)


def _rms_r(x, n):
    return lax.rsqrt(jnp.sum(x * x, -1, keepdims=True) * (1.0 / n) + EPS)


def _rms_bwd(dy, x, r, g, n):
    gd = dy * g
    dx = r * gd - x * (r * r * r) * (jnp.sum(x * gd, -1, keepdims=True) * (1.0 / n))
    return dx, jnp.sum(dy * x * r, 0, keepdims=True)


def _dot(a, b, dims):
    return lax.dot_general(a, b, (dims, ((), ())), preferred_element_type=F32)


NN = ((1,), (0,))
NT = ((1,), (1,))
TN = ((0,), (0,))


def _rope(x, c, s1, s2):
    n = x.shape[-1]
    return x * c + pltpu.roll(x, n - QK_ROPE // 2, 1) * s1 + pltpu.roll(x, QK_ROPE // 2, 1) * s2


def _rope_bwd(g, c, s1, s2):
    n = g.shape[-1]
    return g * c + pltpu.roll(g * s1, QK_ROPE // 2, 1) + pltpu.roll(g * s2, n - QK_ROPE // 2, 1)


def _row_ids(tm, cols=1):
    return pl.program_id(0) * tm + lax.broadcasted_iota(jnp.int32, (tm, cols), 0)


def _mm(a, b, dims, out_dtype, tm, tn, tk, name, by_col_tile=False, b_slabs=False):
    if dims == TN:
        (kk, m), (_, n) = a.shape, b.shape
        a_spec = pl.BlockSpec((tk, tm), lambda i, j, k: (k, i))
    else:
        m, kk = a.shape
        a_spec = pl.BlockSpec((tm, tk), lambda i, j, k: (i, k))
    if dims == NT and b_slabs:
        n = b.shape[1]
        assert b.shape[2] == tk and kk == b.shape[0] * tk, (name, b.shape)
        b_spec = pl.BlockSpec((None, tn, tk), lambda i, j, k: (k, j, 0))
    elif dims == NT:
        n = b.shape[0]
        b_spec = pl.BlockSpec((tn, tk), lambda i, j, k: (j, k))
    elif b_slabs:
        n = b.shape[0] * b.shape[2]
        assert b.shape[2] == tn and kk == b.shape[1], (name, b.shape)
        b_spec = pl.BlockSpec((None, tk, tn), lambda i, j, k: (j, k, 0))
    else:
        n = b.shape[1]
        b_spec = pl.BlockSpec((tk, tn), lambda i, j, k: (k, j))
    assert m % tm == 0 and n % tn == 0 and kk % tk == 0, (name, a.shape, b.shape, tm, tn, tk)
    nk = kk // tk

    def body(a_ref, b_ref, o_ref, acc_ref):
        k = pl.program_id(2)

        @pl.when(k == 0)
        def _():
            acc_ref[...] = jnp.zeros_like(acc_ref)

        acc_ref[...] += _dot(a_ref[...].astype(BF16), b_ref[...].astype(BF16), dims)

        @pl.when(k == nk - 1)
        def _():
            o_ref[...] = acc_ref[...].astype(out_dtype)

    if by_col_tile:
        out_spec, out_shape = pl.BlockSpec((None, tm, tn), lambda i, j, k: (j, i, 0)), (n // tn, m, tn)
    else:
        out_spec, out_shape = pl.BlockSpec((tm, tn), lambda i, j, k: (i, j)), (m, n)
    return pl.pallas_call(
        body, name=name, grid=(m // tm, n // tn, nk), in_specs=[a_spec, b_spec], out_specs=out_spec,
        out_shape=_sds(out_shape, out_dtype), scratch_shapes=[pltpu.VMEM((tm, tn), F32)],
        compiler_params=_params(3))(a, b)


def _fwd_in(x, head, gmix, win, gq, wuq, gkv, wukv, rc, rs1, rs2):
    L = TM + x.shape[0]

    def body(x_ref, head_ref, gmix_ref, win_ref, gq_ref, wuq_ref, gkv_ref, wukv_ref, c_ref, s1_ref, s2_ref,
             h0_ref, n_ref, zag_ref, u0_ref, cq_ref, ckv_ref, qn_ref, kvn_ref, q_ref, kv_ref, kr_ref, kvt_ref):
        h = jnp.where(pl.program_id(0) == 0, head_ref[...], x_ref[...])
        h0_ref[...] = h
        n = (h * _rms_r(h, D_MODEL) * gmix_ref[...]).astype(BF16)
        n_ref[...] = n
        z = _dot(n, win_ref[...], NN)
        a, gate = z[:, :D_CONV], z[:, D_CONV:D_AG]
        zag_ref[...] = z[:, :D_AG].astype(BF16)
        u0_ref[...] = a * jax.nn.sigmoid(gate)
        cq = z[:, D_AG:D_AG + Q_LORA]
        ckv = z[:, D_AG + Q_LORA:D_AG + Q_LORA + KV_LORA]
        krp = z[:, D_AG + Q_LORA + KV_LORA:]
        cq_ref[...] = cq
        ckv_ref[...] = ckv
        qn = (cq * _rms_r(cq, Q_LORA) * gq_ref[...]).astype(BF16)
        qn_ref[...] = qn
        kvn = (ckv * _rms_r(ckv, KV_LORA) * gkv_ref[...]).astype(BF16)
        kvn_ref[...] = kvn
        c, s1, s2 = c_ref[...], s1_ref[...], s2_ref[...]
        q = _dot(qn, wuq_ref[...], NN)
        q = _rope(q, jnp.tile(c, (1, N_HEADS)), jnp.tile(s1, (1, N_HEADS)), jnp.tile(s2, (1, N_HEADS)))
        q_ref[...] = (q * Q_SCALE).astype(BF16)
        kv = _dot(kvn, wukv_ref[...], NN)
        kv_ref[...] = kv.astype(BF16)
        kvt_ref[...] = kv.T.astype(BF16)
        kr_ref[...] = _rope(krp, c, s1, s2).astype(BF16)

    outs = [(D_MODEL, F32), (D_MODEL, BF16), (D_AG, BF16), (D_CONV, F32), (Q_LORA, F32), (KV_LORA, F32), (Q_LORA, BF16),
            (KV_LORA, BF16), (D_HEADS, BF16), (D_HEADS, BF16), (HB, BF16)]
    return pl.pallas_call(
        body, name="fwd_in", grid=(L // TM,),
        in_specs=[pl.BlockSpec((TM, D_MODEL), lambda i: (jnp.maximum(i - 1, 0), 0)), _full(head.shape),
                  _full(gmix.shape), _full(win.shape), _full(gq.shape), _full(wuq.shape),
                  _full(gkv.shape), _full(wukv.shape), _rows(TM, HB), _rows(TM, HB), _rows(TM, HB)],
        out_specs=[_rows(TM, c) for c, _ in outs] + [pl.BlockSpec((D_HEADS, TM), lambda i: (0, i))],
        out_shape=[_sds((L, c), d) for c, d in outs] + [_sds((D_HEADS, L), BF16)],
        compiler_params=_params(1))(x, head, gmix, win, gq, wuq, gkv, wukv, rc, rs1, rs2)


def _conv_taps(xx, w_ref, halo, tm, flip):
    kw = w_ref.shape[0]
    acc, rolls = None, {}
    for k in range(kw):
        term = w_ref[k:k + 1, :] * _shifted_rows(xx, kw - 1 - k, halo, tm, flip, rolls)
        acc = term if acc is None else acc + term
    return acc


def _shifted_rows(xx, d, halo, tm, flip, rolls):
    a, b = divmod(d, SUBLANES)
    if b not in rolls:
        rolls[b] = xx if b == 0 else pltpu.roll(xx, (xx.shape[0] - b) if flip else b, 0)
    start = SUBLANES * a if flip else halo - SUBLANES * a
    return rolls[b][start:start + tm]


def _ln_silu(u1, lg, lb):
    mu = jnp.mean(u1, -1, keepdims=True)
    xc = u1 - mu
    rs = lax.rsqrt(jnp.mean(xc * xc, -1, keepdims=True) + EPS)
    xh = xc * rs
    u2 = xh * lg + lb
    sg = jax.nn.sigmoid(u2)
    return rs, xh, u2, sg, u2 * sg


def _fwd_conv(u0, cw, cb, lg, lb, og):
    L = u0.shape[0]

    def body(u0_ref, u0p_ref, cw_ref, cb_ref, lg_ref, lb_ref, og_ref, u1_ref, mixa_ref):
        halo = jnp.where(pl.program_id(0) > 0, u0p_ref[...], 0.0)
        xx = jnp.concatenate([halo, u0_ref[...]], 0)
        u1 = _conv_taps(xx, cw_ref, HALO_CONV, TM, False) + cb_ref[...]
        u1_ref[...] = u1
        u = _ln_silu(u1, lg_ref[...], lb_ref[...])[4]
        mixa_ref[...] = (u * _rms_r(u, D_CONV) * og_ref[...]).astype(BF16)

    return pl.pallas_call(
        body, name="fwd_conv", grid=(L // TM,),
        in_specs=[_rows(TM, D_CONV), _prev(HALO_CONV, D_CONV, TM), _full(cw.shape), _full(cb.shape),
                  _full(lg.shape), _full(lb.shape), _full(og.shape)],
        out_specs=[_rows(TM, D_CONV), _rows(TM, D_CONV)],
        out_shape=[_sds((L, D_CONV), F32), _sds((L, D_CONV), BF16)],
        compiler_params=_params(1))(u0, u0, cw, cb, lg, lb, og)


def _visible(i, j, t):
    row = i * t + lax.broadcasted_iota(jnp.int32, (t, t), 0)
    col = j * t + lax.broadcasted_iota(jnp.int32, (t, t), 1)
    return (lax.shift_right_logical(col, CHUNK_SHIFT) <= lax.shift_right_logical(row, CHUNK_SHIFT)) & (col >= DEAD)


def _visible_t(i, j, t):
    key = j * t + lax.broadcasted_iota(jnp.int32, (t, t), 0)
    query = i * t + lax.broadcasted_iota(jnp.int32, (t, t), 1)
    return (lax.shift_right_logical(key, CHUNK_SHIFT) <= lax.shift_right_logical(query, CHUNK_SHIFT)) & (key >= DEAD)


def _stat_lane(h):
    return (h // HEAD_GROUP) * LANES + h % HEAD_GROUP


def _scatter_stats(cols, t):
    lane = lax.broadcasted_iota(jnp.int32, (t, N_GROUPS * LANES), 1)
    out = jnp.zeros((t, N_GROUPS * LANES), F32)
    for h, col in enumerate(cols):
        out = jnp.where(lane == _stat_lane(h), col, out)
    return out


def _resident(shape, index_map):
    return pl.BlockSpec(shape, index_map, pipeline_mode=pl.Buffered(1))


def _attn_fwd(q, kv, kvt, kr, shards=()):
    L = q.shape[0]
    t = TM
    nq = L // t
    n = len(shards)
    pass_step = (5 * nq) // 6

    def body(q_ref, kv_ref, kvt_ref, kr_ref, *refs):
        gather_refs = refs[:n] + refs[n + 2:2 * n + 2] + refs[2 * n + 6:]
        o_ref, lse_ref = refs[n:n + 2]
        qt_scr, m_scr, l_scr, acc_scr = refs[2 * n + 2:2 * n + 6]
        i = pl.program_id(0)
        if n:
            @pl.when(i == 0)
            def _():
                for cp in _gather_copies(gather_refs[:n], gather_refs[n:2 * n], *gather_refs[2 * n:])[0]:
                    cp.start()

        lane = lax.broadcasted_iota(jnp.int32, (t, HB), 1)
        heads = range(N_HEADS)
        cols = [slice(h * HB, (h + 1) * HB) for h in heads]
        for h in heads:
            qt_scr[cols[h], :] = q_ref[:, cols[h]].astype(F32).T.astype(BF16)
        m_scr[...] = jnp.full_like(m_scr, NEG)
        l_scr[...] = jnp.zeros_like(l_scr)
        acc_scr[...] = jnp.zeros_like(acc_scr)

        def tile(keys, vis, whole=True):
            krj = kr_ref[keys, :]
            kvj = [kv_ref[keys, cols[h]] for h in heads]
            lane_k = lane[:krj.shape[0]]
            s = [_dot(jnp.where(lane_k < QK_NOPE, kvj[h], krj), qt_scr[cols[h], :], NN) for h in heads]
            p, alpha = [], []
            for h in heads:
                sh = s[h] if vis is None else jnp.where(vis, s[h], NEG)
                m_prev = m_scr[h:h + 1, :]
                m_new = jnp.maximum(m_prev, jnp.max(sh, 0, keepdims=True))
                a = jnp.exp2(m_prev - m_new)
                ph = jnp.exp2(sh - m_new)
                l_scr[h:h + 1, :] = a * l_scr[h:h + 1, :] + jnp.sum(ph, 0, keepdims=True)
                m_scr[h:h + 1, :] = m_new
                p.append(ph.astype(BF16))
                alpha.append(a)
            for h in heads:
                pv = _dot(kvt_ref[cols[h], keys], p[h], NN) if whole else _dot(kvj[h], p[h], TN)
                acc_scr[cols[h], :] = alpha[h] * acc_scr[cols[h], :] + pv

        tile(pl.ds(pl.multiple_of(i * t, t), t), _visible_t(i, i, t))

        @pl.when(i > 0)
        def _():
            tile(pl.ds(DEAD, N_META), None, whole=False)

        def unmasked(j, carry):
            tile(pl.ds(pl.multiple_of(j * t, t), t), None)
            return carry

        lax.fori_loop(1, i, unmasked, 0)
        lse_ref[...] = jnp.zeros_like(lse_ref)
        for h in heads:
            l = l_scr[h:h + 1, :]
            o_ref[:, cols[h]] = jnp.where(lane >= QK_NOPE, (acc_scr[cols[h], :] / l).T, 0.0).astype(BF16)
            lse_ref[h // HEAD_GROUP, h % HEAD_GROUP:h % HEAD_GROUP + 1, :] = m_scr[h:h + 1, :] + jnp.log2(l)
        if n:
            @pl.when(i == pass_step)
            def _():
                _, arrivals, forwards, _ = _gather_copies(gather_refs[:n], gather_refs[n:2 * n], *gather_refs[2 * n:])
                for landed, onward in zip(arrivals, forwards):
                    landed.wait_recv()
                    onward.start()

            @pl.when(i == nq - 1)
            def _():
                sends, _, forwards, finals = _gather_copies(gather_refs[:n], gather_refs[n:2 * n], *gather_refs[2 * n:])
                for cp in finals:
                    cp.wait_recv()
                for cp in sends + forwards:
                    cp.wait_send()

    any_spec = pl.BlockSpec(memory_space=pl.ANY)
    outs = pl.pallas_call(
        body, name="attn_fwd", grid=(nq,),
        in_specs=[_rows(t, D_HEADS), _resident((L, D_HEADS), lambda i: (0, 0)),
                  _resident((D_HEADS, L), lambda i: (0, 0)), _resident((L, HB), lambda i: (0, 0))] + [any_spec] * n,
        out_specs=[_rows(t, D_HEADS), pl.BlockSpec((N_GROUPS, SUBLANES, t), lambda i: (0, 0, i))] + [any_spec] * n,
        out_shape=[_sds((L, D_HEADS), BF16), _sds((N_GROUPS, SUBLANES, L), F32)] + _gather_out_shapes(shards),
        scratch_shapes=[pltpu.VMEM((D_HEADS, t), BF16), pltpu.VMEM((N_HEADS, t), F32), pltpu.VMEM((N_HEADS, t), F32),
                        pltpu.VMEM((D_HEADS, t), F32)] + (_gather_semaphores(n) if n else []),
        compiler_params=_params(1))(q, kv, kvt, kr, *shards)
    return outs[0], outs[1], _gathered(outs[2:], shards)


def _fwd_mix(h0, mixa, o, ga, wout, gffn):
    L = h0.shape[0]

    def body(h0_ref, mixa_ref, o_ref, ga_ref, wout_ref, gffn_ref, mix_ref, h1_ref, n2_ref):
        of = o_ref[...].astype(F32)
        mixb = (of * _rms_r(of, N_HEADS * V_HEAD) * ga_ref[...]).astype(BF16)
        mix = jnp.concatenate([mixa_ref[...], mixb], 1)
        mix_ref[...] = mix
        mo = jnp.where(_row_ids(TM) >= DEAD, _dot(mix, wout_ref[...], NN), 0.0)
        h1 = h0_ref[...] + mo
        h1_ref[...] = h1
        n2_ref[...] = (h1 * _rms_r(h1, D_MODEL) * gffn_ref[...]).astype(BF16)

    return pl.pallas_call(
        body, name="fwd_mix", grid=(L // TM,),
        in_specs=[_rows(TM, D_MODEL), _rows(TM, D_CONV), _rows(TM, D_HEADS), _full(ga.shape), _full(wout.shape),
                  _full(gffn.shape)],
        out_specs=[_rows(TM, D_MIX), _rows(TM, D_MODEL), _rows(TM, D_MODEL)],
        out_shape=[_sds((L, D_MIX), BF16), _sds((L, D_MODEL), F32), _sds((L, D_MODEL), BF16)],
        compiler_params=_params(1))(h0, mixa, o, ga, wout, gffn)


def _ffn_act_chunk(c, upg_ref, upv_ref, hg, hv, fcw_ref, fcb_ref):
    cs = slice(c * FF_CHUNK, (c + 1) * FF_CHUNK)
    out = []
    for part, (up_ref, halo) in enumerate(((upg_ref, hg), (upv_ref, hv))):
        xx = jnp.concatenate([halo[:, cs], up_ref[:, cs].astype(F32)], 0)
        ws = slice(part * D_FF + c * FF_CHUNK, part * D_FF + (c + 1) * FF_CHUNK)
        y = (fcw_ref[0:1, ws] * pltpu.roll(xx, 2, 0)[HALO_FFN:] + fcw_ref[1:2, ws] * pltpu.roll(xx, 1, 0)[HALO_FFN:]
             + fcw_ref[2:3, ws] * xx[HALO_FFN:] + fcb_ref[:, ws])
        out.append(y)
    return out


def _ffn_in_specs(L):
    return [_rows(TM, D_FF, 0), _rows(TM, D_FF, 1), _prev(HALO_FFN, D_FF, TM, 0), _prev(HALO_FFN, D_FF, TM, 1)]


def _ffn_halos(hg_ref, hv_ref):
    first = pl.program_id(0) == 0
    return (jnp.where(first, 0.0, hg_ref[...].astype(F32)), jnp.where(first, 0.0, hv_ref[...].astype(F32)))


def _fwd_ffn_loss(up0, fcw, fcb, wdown, h1, target, gfin):
    L = h1.shape[0]

    def body(upg_ref, upv_ref, hg_ref, hv_ref, fcw_ref, fcb_ref, wd_ref, h1_ref, t_ref, gf_ref,
             dh2_ref, loss_ref, dgf_ref, up_ref, act_ref):
        i = pl.program_id(0)
        hg, hv = _ffn_halos(hg_ref, hv_ref)
        for c in range(D_FF // FF_CHUNK):
            cs = slice(c * FF_CHUNK, (c + 1) * FF_CHUNK)
            g, val = _ffn_act_chunk(c, upg_ref, upv_ref, hg, hv, fcw_ref, fcb_ref)
            up_ref[:, cs] = g.astype(BF16)
            up_ref[:, D_FF + c * FF_CHUNK:D_FF + (c + 1) * FF_CHUNK] = val.astype(BF16)
            act_ref[:, cs] = (g * jax.nn.sigmoid(g) * val).astype(BF16)
        h2 = h1_ref[...] + _dot(act_ref[...], wd_ref[...], NN)
        r = _rms_r(h2, D_MODEL)
        gf = gf_ref[...]
        err = jnp.where(i > 0, h2 * r * gf - t_ref[...], 0.0)
        dy = err * (1.0 / D_MODEL)
        dh2, dgf = _rms_bwd(dy, h2, r, gf, D_MODEL)
        dh2_ref[...] = dh2

        @pl.when(i == 0)
        def _():
            loss_ref[...] = jnp.zeros_like(loss_ref)
            dgf_ref[...] = jnp.zeros_like(dgf_ref)

        loss_ref[...] += jnp.sum(err * err) * (0.5 / D_MODEL)
        dgf_ref[...] += dgf

    return pl.pallas_call(
        body, name="fwd_ffn_loss", grid=(L // TM,),
        in_specs=_ffn_in_specs(L) + [_full(fcw.shape), _full(fcb.shape), _full(wdown.shape), _rows(TM, D_MODEL),
                                     pl.BlockSpec((TM, D_MODEL), lambda i: (jnp.maximum(i - 1, 0), 0)),
                                     _full(gfin.shape)],
        out_specs=[_rows(TM, D_MODEL), _full((1, LANES)), _full((1, D_MODEL)), _rows(TM, 2 * D_FF), _rows(TM, D_FF)],
        out_shape=[_sds((L, D_MODEL), F32), _sds((1, LANES), F32), _sds((1, D_MODEL), F32),
                   _sds((L, 2 * D_FF), BF16), _sds((L, D_FF), BF16)],
        compiler_params=_params(1))(up0, up0, up0, up0, fcw, fcb, wdown, h1, target, gfin)


def _bwd_ffn_act(dh2, up, wdown):
    L = dh2.shape[0]

    def body(dh2_ref, upg_ref, upv_ref, wd_ref, dup_ref, dfcb_ref):
        da = _dot(dh2_ref[...].astype(BF16), wd_ref[...], NT)

        @pl.when(pl.program_id(0) == 0)
        def _():
            dfcb_ref[...] = jnp.zeros_like(dfcb_ref)

        for c in range(D_FF // FF_CHUNK):
            cs = slice(c * FF_CHUNK, (c + 1) * FF_CHUNK)
            vs = slice(D_FF + c * FF_CHUNK, D_FF + (c + 1) * FF_CHUNK)
            g, val = upg_ref[:, cs].astype(F32), upv_ref[:, cs].astype(F32)
            sg = jax.nn.sigmoid(g)
            si = g * sg
            dac = da[:, cs]
            dg = dac * val * (sg * (1.0 + g * (1.0 - sg)))
            dv = dac * si
            dup_ref[:, cs] = dg.astype(BF16)
            dup_ref[:, vs] = dv.astype(BF16)
            dfcb_ref[:, cs] += jnp.sum(dg, 0, keepdims=True)
            dfcb_ref[:, vs] += jnp.sum(dv, 0, keepdims=True)

    return pl.pallas_call(
        body, name="bwd_ffn_act", grid=(L // TM,),
        in_specs=[_rows(TM, D_MODEL), _rows(TM, D_FF, 0), _rows(TM, D_FF, 1), _full(wdown.shape)],
        out_specs=[_rows(TM, 2 * D_FF), _full((1, 2 * D_FF))],
        out_shape=[_sds((L, 2 * D_FF), BF16), _sds((1, 2 * D_FF), F32)],
        compiler_params=_params(1))(dh2, up, up, wdown)


def _bwd_ffn_conv_up(dup, up0, fcw, wup, tl):
    L, C = dup.shape
    tc = FF_TILE
    nt = L // tl
    nhb = L // HALO_FFN
    chunks = [(c0, min(FF_MXU_CHUNK, tc - c0)) for c0 in range(0, tc, FF_MXU_CHUNK)]

    def body(dy_ref, dyn_ref, x_ref, xp_ref, w_ref, wup_ref, dn2_ref, dx_ref, dw_ref, acc_ref):
        i, k = pl.program_id(0), pl.program_id(1)

        @pl.when(k == 0)
        def _():
            acc_ref[...] = jnp.zeros_like(acc_ref)

        last, first = i == nt - 1, i == 0
        for c0, cw in chunks:
            cs = slice(c0, c0 + cw)
            yy = jnp.concatenate([dy_ref[:, cs].astype(F32), jnp.where(last, 0.0, dyn_ref[:, cs].astype(F32))], 0)
            w = w_ref[:, cs]
            dx = (w[0:1] * pltpu.roll(yy, tl + HALO_FFN - 2, 0)[:tl] + w[1:2] * pltpu.roll(yy, tl + HALO_FFN - 1, 0)[:tl]
                  + w[2:3] * yy[:tl]).astype(BF16)
            dx_ref[:, cs] = dx
            acc_ref[...] += _dot(dx, wup_ref[:, cs], NT)
            xx = jnp.concatenate([jnp.where(first, 0.0, xp_ref[:, cs].astype(F32)), x_ref[:, cs].astype(F32)], 0)
            dy = yy[:tl]
            dw_ref[0:1, cs] = jnp.sum(dy * pltpu.roll(xx, 2, 0)[HALO_FFN:], 0, keepdims=True)
            dw_ref[1:2, cs] = jnp.sum(dy * pltpu.roll(xx, 1, 0)[HALO_FFN:], 0, keepdims=True)
            dw_ref[2:3, cs] = jnp.sum(dy * xx[HALO_FFN:], 0, keepdims=True)

        @pl.when(k == C // tc - 1)
        def _():
            dn2_ref[...] = acc_ref[...]

    tile = pl.BlockSpec((tl, tc), lambda i, k: (i, k))
    per = tl // HALO_FFN
    return pl.pallas_call(
        body, name="bwd_ffn_conv_up", grid=(nt, C // tc),
        in_specs=[tile, pl.BlockSpec((HALO_FFN, tc), lambda i, k: (jnp.minimum((i + 1) * per, nhb - 1), k)),
                  tile, pl.BlockSpec((HALO_FFN, tc), lambda i, k: (jnp.maximum(i * per - 1, 0), k)),
                  pl.BlockSpec((FFN_CONV_WIDTH, tc), lambda i, k: (0, k)),
                  pl.BlockSpec((None, D_MODEL, tc), lambda i, k: (k, 0, 0))],
        out_specs=[pl.BlockSpec((tl, D_MODEL), lambda i, k: (i, 0)), tile,
                   pl.BlockSpec((None, FFN_CONV_WIDTH, tc), lambda i, k: (i, 0, k))],
        out_shape=[_sds((L, D_MODEL), F32), _sds((L, C), BF16), _sds((nt, FFN_CONV_WIDTH, C), F32)],
        scratch_shapes=[pltpu.VMEM((tl, D_MODEL), F32)],
        compiler_params=_params(2))(dup, dup, up0, up0, fcw, wup)


def _bwd_mix(dh2, dn2, h1, gffn, wout, o, ga, u1, lg, lb, og):
    L = h1.shape[0]

    def body(dh2_ref, dn2_ref, h1_ref, gffn_ref, wout_ref, o_ref, ga_ref, u1_ref, lg_ref, lb_ref, og_ref,
             dh1_ref, dh1m_ref, do_ref, delta_ref, du1_ref, dgffn_ref, dga_ref, dog_ref, dlg_ref, dlb_ref):
        h1 = h1_ref[...]
        dn2x, dgffn = _rms_bwd(dn2_ref[...], h1, _rms_r(h1, D_MODEL), gffn_ref[...], D_MODEL)
        dh1 = dh2_ref[...] + dn2x
        dh1_ref[...] = dh1
        dh1m = jnp.where(_row_ids(TM) >= DEAD, dh1, 0.0).astype(BF16)
        dh1m_ref[...] = dh1m
        dmix = _dot(dh1m, wout_ref[...], NT)
        dma, dmb = dmix[:, :D_CONV], dmix[:, D_CONV:]
        of = o_ref[...].astype(F32)
        do, dga = _rms_bwd(dmb, of, _rms_r(of, N_HEADS * V_HEAD), ga_ref[...], N_HEADS * V_HEAD)
        do_ref[...] = do.astype(BF16)
        prod = do * of
        by_lane = _scatter_stats([jnp.sum(prod[:, h * HB:(h + 1) * HB], -1, keepdims=True) for h in range(N_HEADS)], TM)
        by_row = by_lane.T
        for grp in range(N_GROUPS):
            delta_ref[grp] = by_row[grp * LANES:grp * LANES + SUBLANES, :]
        lg = lg_ref[...]
        rs, xh, u2, sg, u = _ln_silu(u1_ref[...], lg, lb_ref[...])
        du, dog = _rms_bwd(dma, u, _rms_r(u, D_CONV), og_ref[...], D_CONV)
        du2 = du * (sg * (1.0 + u2 * (1.0 - sg)))
        dxh = du2 * lg
        du1_ref[...] = rs * (dxh - jnp.mean(dxh, -1, keepdims=True) - xh * jnp.mean(dxh * xh, -1, keepdims=True))

        @pl.when(pl.program_id(0) == 0)
        def _():
            for ref in (dgffn_ref, dga_ref, dog_ref, dlg_ref, dlb_ref):
                ref[...] = jnp.zeros_like(ref)

        dgffn_ref[...] += dgffn
        dga_ref[...] += dga
        dog_ref[...] += dog
        dlg_ref[...] += jnp.sum(du2 * xh, 0, keepdims=True)
        dlb_ref[...] += jnp.sum(du2, 0, keepdims=True)

    return pl.pallas_call(
        body, name="bwd_mix", grid=(L // TM,),
        in_specs=[_rows(TM, D_MODEL), _rows(TM, D_MODEL), _rows(TM, D_MODEL), _full(gffn.shape), _full(wout.shape),
                  _rows(TM, D_HEADS), _full(ga.shape), _rows(TM, D_CONV), _full(lg.shape), _full(lb.shape),
                  _full(og.shape)],
        out_specs=[_rows(TM, D_MODEL), _rows(TM, D_MODEL), _rows(TM, D_HEADS),
                   pl.BlockSpec((N_GROUPS, SUBLANES, TM), lambda i: (0, 0, i)),
                   _rows(TM, D_CONV), _full((1, D_MODEL)), _full((1, D_HEADS)), _full((1, D_CONV)),
                   _full((1, D_CONV)), _full((1, D_CONV))],
        out_shape=[_sds((L, D_MODEL), F32), _sds((L, D_MODEL), BF16), _sds((L, D_HEADS), BF16),
                   _sds((N_GROUPS, SUBLANES, L), F32),
                   _sds((L, D_CONV), F32), _sds((1, D_MODEL), F32), _sds((1, D_HEADS), F32), _sds((1, D_CONV), F32),
                   _sds((1, D_CONV), F32), _sds((1, D_CONV), F32)],
        compiler_params=_params(1))(dh2, dn2, h1, gffn, wout, o, ga, u1, lg, lb, og)


def _attn_bwd(q, kv, kr, do, lse, delta, parts=()):
    L = q.shape[0]
    t = TM
    nt = L // t
    gw = HEAD_GROUP * HB
    n = len(parts)

    def body(q_ref, kv_ref, kr_ref, do_ref, lse_ref, delta_ref, *refs):
        dq_ref, dkv_ref, dkr_ref = refs[n:n + 3]
        dqt_acc, dk_acc, dv_acc, kkt_scr = refs[2 * n + 3:2 * n + 7]
        exchange_refs = (refs[:n], refs[n + 3:2 * n + 3]) + refs[2 * n + 7:]
        g, j = pl.program_id(0), pl.program_id(1)
        if n:
            @pl.when((g == 0) & (j == 0))
            def _():
                for cp in _chip_copies(*exchange_refs)[0]:
                    cp.start()

        lane = lax.broadcasted_iota(jnp.int32, (t, HB), 1)

        @pl.when(j == 0)
        def _():
            dqt_acc[...] = jnp.zeros_like(dqt_acc)

        @pl.when((j == 0) & (g == 0))
        def _():
            dkr_ref[...] = jnp.zeros_like(dkr_ref)

        dk_acc[...] = jnp.zeros_like(dk_acc)
        dv_acc[...] = jnp.zeros_like(dv_acc)
        krj = kr_ref[...]
        heads = range(HEAD_GROUP)
        cols = [slice(h * HB, (h + 1) * HB) for h in heads]
        for hc in cols:
            kkt_scr[hc, :] = jnp.where(lane < QK_NOPE, kv_ref[:, hc], krj).astype(F32).T.astype(BF16)

        def tile(i, vis, whole=True):
            qs = pl.ds(pl.multiple_of(i * t, t), t)
            keys = slice(None) if whole else slice(DEAD, t)
            kvj = [kv_ref[keys, hc] for hc in cols]
            lane_k = lane[:kvj[0].shape[0]]
            kk = [jnp.where(lane_k < QK_NOPE, kvj[h], krj[keys]) for h in heads]
            qi = [q_ref[qs, hc] for hc in cols]
            doi = [do_ref[qs, hc] for hc in cols]
            s = [_dot(kk[h], qi[h].astype(F32).T.astype(BF16), NN) for h in heads]
            dp = [_dot(kvj[h], doi[h].astype(F32).T.astype(BF16), NN) for h in heads]
            p = []
            for h in heads:
                sh = s[h] if vis is None else jnp.where(vis, s[h], NEG)
                p.append(jnp.exp2(sh - lse_ref[h:h + 1, qs]))
            for h in heads:
                dv_acc[keys, cols[h]] += _dot(p[h].astype(BF16), doi[h], NN)
            ds = [(p[h] * (dp[h] - delta_ref[h:h + 1, qs]) * LN2).astype(BF16) for h in heads]
            for h in heads:
                dk_acc[keys, cols[h]] += _dot(ds[h], qi[h], NN)
            for h in heads:
                dqt = _dot(kkt_scr[cols[h], :], ds[h], NN) if whole else _dot(kk[h], ds[h], TN)
                dqt_acc[cols[h], qs] += dqt

        @pl.when(j == 0)
        def _():
            tile(0, _visible_t(0, 0, t)[DEAD:], whole=False)

            def meta_keys(i, carry):
                tile(i, None, whole=False)
                return carry

            lax.fori_loop(1, nt, meta_keys, 0)

        @pl.when(j > 0)
        def _():
            tile(j, _visible_t(j, j, t))

            def unmasked(i, carry):
                tile(i, None)
                return carry

            lax.fori_loop(j + 1, nt, unmasked, 0)

        dkr = jnp.zeros((t, HB), F32)
        for h in range(HEAD_GROUP):
            hc = slice(h * HB, (h + 1) * HB)
            dk = dk_acc[:, hc]
            dkv_ref[:, hc] = jnp.where(lane < QK_NOPE, dk, dv_acc[:, hc]).astype(BF16)
            dkr = dkr + jnp.where(lane >= QK_NOPE, dk, 0.0)
        dkr_ref[pl.ds(pl.multiple_of(j * t, t), t), :] += dkr

        @pl.when(j == nt - 1)
        def _():
            def untranspose(i, carry):
                qs = pl.ds(pl.multiple_of(i * t, t), t)
                dq_ref[qs, :] = (dqt_acc[:, qs].T * Q_SCALE).astype(BF16)
                return carry

            lax.fori_loop(0, nt, untranspose, 0)

        if n:
            @pl.when((g == N_GROUPS - 1) & (j == nt - 1))
            def _():
                sends, arrivals = _chip_copies(*exchange_refs)
                for cp in arrivals:
                    cp.wait_recv()
                for cp in sends:
                    cp.wait_send()

    group = lambda g, j: (0, g)
    stats = _resident((None, SUBLANES, L), lambda g, j: (g, 0, 0))
    any_spec = pl.BlockSpec(memory_space=pl.ANY)
    outs = pl.pallas_call(
        body, name="attn_bwd", grid=(N_GROUPS, nt),
        in_specs=[_resident((L, gw), group), pl.BlockSpec((t, gw), lambda g, j: (j, g)),
                  pl.BlockSpec((t, HB), lambda g, j: (j, 0)), _resident((L, gw), group), stats, stats]
        + [any_spec] * n,
        out_specs=[pl.BlockSpec((L, gw), group), pl.BlockSpec((t, gw), lambda g, j: (j, g)),
                   pl.BlockSpec((L, HB), lambda g, j: (0, 0))] + [any_spec] * n,
        out_shape=[_sds((L, D_HEADS), BF16), _sds((L, D_HEADS), BF16), _sds((L, HB), F32)]
        + [_sds(p.shape, p.dtype) for p in parts],
        scratch_shapes=[pltpu.VMEM((gw, L), F32), pltpu.VMEM((t, gw), F32), pltpu.VMEM((t, gw), F32),
                        pltpu.VMEM((gw, t), BF16)] + (_chip_semaphores(n) if n else []),
        compiler_params=_params(2))(q, kv, kr, do, lse, delta, *parts)
    return outs[0], outs[1], outs[2], list(outs[3:])


def _bwd_conv(du1, u0, cw, zag):
    L = du1.shape[0]
    nt = L // TM

    def body(dy_ref, dyn_ref, x_ref, xp_ref, cw_ref, zag_ref, dzag_ref, dcw_ref, dcb_ref):
        i = pl.program_id(0)
        dy = dy_ref[...]
        yy = jnp.concatenate([dy, jnp.where(i < nt - 1, dyn_ref[...], 0.0)], 0)
        du0 = _conv_taps(yy, cw_ref, HALO_CONV, TM, True)
        zag = zag_ref[...].astype(F32)
        a, sg = zag[:, :D_CONV], jax.nn.sigmoid(zag[:, D_CONV:])
        dzag_ref[...] = jnp.concatenate([du0 * sg, du0 * a * sg * (1.0 - sg)], 1).astype(BF16)
        xx = jnp.concatenate([jnp.where(i > 0, xp_ref[...], 0.0), x_ref[...]], 0)

        @pl.when(i == 0)
        def _():
            dcw_ref[...] = jnp.zeros_like(dcw_ref)
            dcb_ref[...] = jnp.zeros_like(dcb_ref)

        rolls = {}
        for k in range(CONV_WIDTH):
            xs = _shifted_rows(xx, CONV_WIDTH - 1 - k, HALO_CONV, TM, False, rolls)
            dcw_ref[k:k + 1, :] += jnp.sum(dy * xs, 0, keepdims=True)
        dcb_ref[...] += jnp.sum(dy, 0, keepdims=True)

    return pl.pallas_call(
        body, name="bwd_conv", grid=(nt,),
        in_specs=[_rows(TM, D_CONV), _next(HALO_CONV, D_CONV, TM, L // HALO_CONV), _rows(TM, D_CONV),
                  _prev(HALO_CONV, D_CONV, TM), _full(cw.shape), _rows(TM, D_AG)],
        out_specs=[_rows(TM, D_AG), _full(cw.shape), _full((1, D_CONV))],
        out_shape=[_sds((L, D_AG), BF16), _sds(cw.shape, F32), _sds((1, D_CONV), F32)],
        compiler_params=_params(1))(du1, du1, u0, u0, cw, zag)


def _bwd_in(dzag, dq, dkv, dkr, cq, ckv, gq, gkv, wuq, wukv, win, rc, rs1, rs2, h0, gmix, dh1):
    L = h0.shape[0]

    def body(dzag_ref, dq_ref, dkv_ref, dkr_ref, cq_ref, ckv_ref, gq_ref, gkv_ref, wuq_ref, wukv_ref, win_ref,
             c_ref, s1_ref, s2_ref, h0_ref, gmix_ref, dh1_ref,
             dz_ref, dqr_ref, gx_ref, dfirst_ref, dgq_ref, dgkv_ref, dgmix_ref):
        i = pl.program_id(0)
        c, s1, s2 = c_ref[...], s1_ref[...], s2_ref[...]
        dqr = _rope_bwd(dq_ref[...].astype(F32), jnp.tile(c, (1, N_HEADS)), jnp.tile(s1, (1, N_HEADS)),
                        jnp.tile(s2, (1, N_HEADS))).astype(BF16)
        dqr_ref[...] = dqr
        cq, ckv = cq_ref[...], ckv_ref[...]
        dcq, dgq = _rms_bwd(_dot(dqr, wuq_ref[...], NT), cq, _rms_r(cq, Q_LORA), gq_ref[...], Q_LORA)
        dckv, dgkv = _rms_bwd(_dot(dkv_ref[...], wukv_ref[...], NT), ckv, _rms_r(ckv, KV_LORA), gkv_ref[...], KV_LORA)
        dkrp = _rope_bwd(dkr_ref[...], c, s1, s2)
        dz = jnp.concatenate([dzag_ref[...], dcq.astype(BF16), dckv.astype(BF16), dkrp.astype(BF16)], 1)
        dz_ref[...] = dz
        h0 = h0_ref[...]
        dnx, dgmix = _rms_bwd(_dot(dz, win_ref[...], NT), h0, _rms_r(h0, D_MODEL), gmix_ref[...], D_MODEL)
        dh0 = dh1_ref[...] + dnx

        @pl.when(i == 0)
        def _():
            dfirst_ref[...] = dh0
            for ref in (dgq_ref, dgkv_ref, dgmix_ref):
                ref[...] = jnp.zeros_like(ref)

        @pl.when(i > 0)
        def _():
            gx_ref[...] = dh0

        dgq_ref[...] += dgq
        dgkv_ref[...] += dgkv
        dgmix_ref[...] += dgmix

    return pl.pallas_call(
        body, name="bwd_in", grid=(L // TM,),
        in_specs=[_rows(TM, D_AG), _rows(TM, D_HEADS), _rows(TM, D_HEADS), _rows(TM, HB), _rows(TM, Q_LORA),
                  _rows(TM, KV_LORA), _full(gq.shape), _full(gkv.shape), _full(wuq.shape), _full(wukv.shape),
                  _full(win.shape), _rows(TM, HB), _rows(TM, HB), _rows(TM, HB), _rows(TM, D_MODEL),
                  _full(gmix.shape), _rows(TM, D_MODEL)],
        out_specs=[_rows(TM, D_ZP), _rows(TM, D_HEADS),
                   pl.BlockSpec((TM, D_MODEL), lambda i: (jnp.maximum(i - 1, 0), 0)), _full((TM, D_MODEL)),
                   _full((1, Q_LORA)), _full((1, KV_LORA)), _full((1, D_MODEL))],
        out_shape=[_sds((L, D_ZP), BF16), _sds((L, D_HEADS), BF16), _sds((L - TM, D_MODEL), F32),
                   _sds((TM, D_MODEL), F32), _sds((1, Q_LORA), F32), _sds((1, KV_LORA), F32), _sds((1, D_MODEL), F32)],
        compiler_params=_params(1))(dzag, dq, dkv, dkr, cq, ckv, gq, gkv, wuq, wukv, win, rc, rs1, rs2, h0, gmix, dh1)


def _mesh_pos():
    return lax.axis_index("x"), lax.axis_index("y"), lax.axis_index("c")


def _all_gather(shards):
    n = len(shards)

    def body(*refs):
        sends, arrivals, forwards, finals = _gather_copies(refs[:n], refs[n:2 * n], *refs[2 * n:])
        for cp in sends:
            cp.start()
        for landed, onward in zip(arrivals, forwards):
            landed.wait_recv()
            onward.start()
        for cp in finals:
            cp.wait_recv()
        for cp in sends + forwards:
            cp.wait_send()

    any_spec = pl.BlockSpec(memory_space=pl.ANY)
    outs = pl.pallas_call(
        body, name="all_gather_weights", in_specs=[any_spec] * n, out_specs=[any_spec] * n,
        out_shape=_gather_out_shapes(shards), scratch_shapes=_gather_semaphores(n))(*shards)
    return _gathered(outs, shards)


def _gather_out_shapes(shards):
    return [_sds((2 * N_CHIPS, s.shape[0] // 2) + s.shape[1:], s.dtype) for s in shards]


def _gather_semaphores(n):
    return [pltpu.SemaphoreType.DMA((n, 8)), pltpu.SemaphoreType.DMA((n, 8))]


def _gathered(outs, shards):
    return [o.reshape((N_CHIPS, s.shape[0]) + s.shape[1:]) for o, s in zip(outs, shards)]


def _gather_copies(ins, outs, send_sems, recv_sems):
    x, y, c = _mesh_pos()
    chips = [(1 - x, y), (x, 1 - y), (1 - x, 1 - y)]
    sends, arrivals, forwards, finals = [], [], [], []

    def copy(src, dst, a, k, to):
        return pltpu.make_async_remote_copy(src, dst, send_sems.at[a, k], recv_sems.at[a, k], device_id=to,
                                            device_id_type=MESH)

    for a, (src, out) in enumerate(zip(ins, outs)):
        m = out.shape[1]
        mine = src.at[pl.ds(pl.multiple_of(c * m, 16), m)]
        for hf in range(2):
            own = out.at[4 * x + 2 * y + hf]
            sends.append(copy(src.at[pl.ds(hf * m, m)], own, a, 6 + hf, (x, y, 1 - c)))
            finals.append(copy(own, own, a, 6 + hf, (x, y, 1 - c)))
        for k, chip in enumerate(chips):
            slot = 4 * chip[0] + 2 * chip[1]
            sends.append(copy(mine, out.at[4 * x + 2 * y + c], a, k, (*chip, c)))
            arrivals.append(copy(out.at[slot + c], out.at[slot + c], a, k, (*chip, c)))
            forwards.append(copy(out.at[slot + c], out.at[slot + c], a, 3 + k, (x, y, 1 - c)))
            finals.append(copy(out.at[slot + 1 - c], out.at[slot + 1 - c], a, 3 + k, (x, y, 1 - c)))
    return sends, arrivals, forwards, finals


def _sibling_exchange(parts, name):
    n = len(parts)

    def body(*refs):
        ins, theirs = refs[:n], refs[n:2 * n]
        send_sems, recv_sems = refs[2 * n:]
        x, y, c = _mesh_pos()
        copies = []
        for a in range(n):
            h = parts[a].shape[1] // 2
            rows = pl.ds(pl.multiple_of((1 - c) * h, 16), h)
            copies += [pltpu.make_async_remote_copy(ins[a].at[q, rows], theirs[a].at[q], send_sems.at[a, q],
                                                    recv_sems.at[a, q], device_id=(x, y, 1 - c), device_id_type=MESH)
                       for q in range(N_CHIPS)]
        for cp in copies:
            cp.start()
        for cp in copies:
            cp.wait()

    any_spec = pl.BlockSpec(memory_space=pl.ANY)
    return pl.pallas_call(
        body, name=name, in_specs=[any_spec] * n, out_specs=[any_spec] * n,
        out_shape=[_sds((N_CHIPS, p.shape[1] // 2, p.shape[2]), p.dtype) for p in parts],
        scratch_shapes=[pltpu.SemaphoreType.DMA((n, N_CHIPS)), pltpu.SemaphoreType.DMA((n, N_CHIPS))])(*parts)


def _chip_exchange(parts, name):
    n = len(parts)

    def body(*refs):
        sends, arrivals = _chip_copies(refs[:n], refs[n:2 * n], *refs[2 * n:])
        for cp in sends:
            cp.start()
        for cp in arrivals:
            cp.wait_recv()
        for cp in sends:
            cp.wait_send()

    any_spec = pl.BlockSpec(memory_space=pl.ANY)
    return pl.pallas_call(
        body, name=name, in_specs=[any_spec] * n, out_specs=[any_spec] * n,
        out_shape=[_sds(p.shape, p.dtype) for p in parts], scratch_shapes=_chip_semaphores(n))(*parts)


def _chip_semaphores(n):
    return [pltpu.SemaphoreType.DMA((n, 3)), pltpu.SemaphoreType.DMA((n, 3))]


def _chip_copies(ins, outs, send_sems, recv_sems):
    x, y, c = _mesh_pos()
    me = 2 * x + y
    sends, arrivals = [], []
    for a, (src, out) in enumerate(zip(ins, outs)):
        for k, chip in enumerate([(1 - x, y), (x, 1 - y), (1 - x, 1 - y)]):
            slot = 2 * chip[0] + chip[1]
            sems = dict(send_sem=send_sems.at[a, k], recv_sem=recv_sems.at[a, k], device_id=(*chip, c),
                        device_id_type=MESH)
            sends.append(pltpu.make_async_remote_copy(src.at[slot], out.at[me], **sems))
            arrivals.append(pltpu.make_async_remote_copy(out.at[slot], out.at[slot], **sems))
    return sends, arrivals


def _sibling_gather(parts, name):
    n = len(parts)

    def body(*refs):
        ins, outs = refs[:n], refs[n:2 * n]
        send_sems, recv_sems = refs[2 * n:]
        x, y, c = _mesh_pos()
        copies = [pltpu.make_async_remote_copy(ins[a].at[c], outs[a].at[c], send_sems.at[a], recv_sems.at[a],
                                               device_id=(x, y, 1 - c), device_id_type=MESH) for a in range(n)]
        for cp in copies:
            cp.start()
        for cp in copies:
            cp.wait()

    any_spec = pl.BlockSpec(memory_space=pl.ANY)
    return pl.pallas_call(
        body, name=name, in_specs=[any_spec] * n, out_specs=[any_spec] * n,
        out_shape=[_sds(p.shape, p.dtype) for p in parts], input_output_aliases={a: a for a in range(n)},
        scratch_shapes=[pltpu.SemaphoreType.DMA((n,)), pltpu.SemaphoreType.DMA((n,))])(*parts)


def _row_tile(rows, row_bytes, align, budget=1 << 20):
    best = None
    for t in range(align, rows + 1, align):
        if rows % t == 0 and t * row_bytes <= budget:
            best = t
    return best or rows


def _scalar(v):
    return jnp.reshape(v, (1,)).astype(jnp.int32)


def _add_pair(part, theirs, c, name):
    _, h, cols = theirs.shape
    tr = _row_tile(h, cols * 4, 16)
    nb = h // tr

    def body(c_ref, a_ref, b_ref, o_ref):
        o_ref[...] = (a_ref[...].astype(F32) + b_ref[...].astype(F32)).astype(o_ref.dtype)

    half = pl.BlockSpec((None, tr, cols), lambda q, i, c_ref: (q, i, 0))
    grid_spec = pltpu.PrefetchScalarGridSpec(
        num_scalar_prefetch=1, grid=(N_CHIPS, nb),
        in_specs=[pl.BlockSpec((None, tr, cols), lambda q, i, c_ref: (q, c_ref[0] * nb + i, 0)), half],
        out_specs=half)
    return pl.pallas_call(body, name=name, grid_spec=grid_spec, out_shape=_sds(theirs.shape, part.dtype),
                          compiler_params=_params(2))(_scalar(c), part, theirs)


def _add_chips(got, own, me, c, name):
    _, h, cols = got.shape
    tr = _row_tile(h, cols * 4 * N_CHIPS, 16, budget=1 << 21)

    def body(pos_ref, got_ref, own_ref, o_ref):
        acc = None
        for q in range(N_CHIPS):
            term = jnp.where(pos_ref[0] == q, own_ref[q], got_ref[q]).astype(F32)
            acc = term if acc is None else acc + term
        o_ref[...] = acc

    by_chip = pl.BlockSpec((N_CHIPS, tr, cols), lambda i, pos_ref: (0, i, 0))
    grid_spec = pltpu.PrefetchScalarGridSpec(
        num_scalar_prefetch=1, grid=(h // tr,), in_specs=[by_chip, by_chip],
        out_specs=pl.BlockSpec((None, tr, cols), lambda i, pos_ref: (pos_ref[1], i, 0)))
    return pl.pallas_call(body, name=name, grid_spec=grid_spec, out_shape=_sds((2, h, cols), F32),
                          compiler_params=_params(1))(jnp.stack([me, c]).astype(jnp.int32), got, own)


def _reduce_begin(parts, tag):
    c = lax.axis_index("c")
    theirs = _sibling_exchange(parts, f"grad_sibling_exchange_{tag}")
    return [_add_pair(p, t, c, f"grad_add_pair_{tag}_{a}") for a, (p, t) in enumerate(zip(parts, theirs))]


def _reduce_end(pair, got, tag):
    x, y, c = _mesh_pos()
    half = [_add_chips(g, p, 2 * x + y, c, f"grad_add_chips_{tag}_{a}") for a, (g, p) in enumerate(zip(got, pair))]
    both = _sibling_gather(half, f"grad_sibling_gather_{tag}")
    return [b.reshape(-1, b.shape[-1]) for b in both]


def _adamw_math(w, g, m, v):
    m = ADAM_B1 * m + (1.0 - ADAM_B1) * g
    v = ADAM_B2 * v + (1.0 - ADAM_B2) * (g * g)
    m_hat = m / (1.0 - ADAM_B1 ** ADAM_STEP)
    v_hat = v / (1.0 - ADAM_B2 ** ADAM_STEP)
    return -ADAM_LR * (m_hat / (jnp.sqrt(v_hat) + ADAM_EPS) + ADAM_WD * w), m, v


def _adamw_big(w, g, m, v, name):
    r, c = w.shape
    tr = _row_tile(r, c * 4, 8, budget=1 << 19)
    if r // tr > ADAMW_MAX_STEPS:
        tr = r

    def body(w_ref, g_ref, m_ref, v_ref, go_ref, d_ref, mo_ref, vo_ref):
        g = g_ref[...]
        go_ref[...] = g
        d_ref[...], mo_ref[...], vo_ref[...] = _adamw_math(w_ref[...], g, m_ref[...], v_ref[...])

    return pl.pallas_call(
        body, name=name, grid=(r // tr,), in_specs=[_rows(tr, c)] * 4, out_specs=[_rows(tr, c)] * 4,
        out_shape=[_sds((r, c), F32)] * 4, compiler_params=_params(1))(w, g, m, v)


def _adamw_small(ws, gs, ms, vs):
    n = len(ws)

    def body(*refs):
        for a in range(n):
            w_ref, g_ref, m_ref, v_ref = (refs[k * n + a] for k in range(4))
            d, m, v = _adamw_math(w_ref[...], g_ref[...], m_ref[...], v_ref[...])
            refs[4 * n + a][...] = d
            refs[5 * n + a][...] = m
            refs[6 * n + a][...] = v

    vm = pl.BlockSpec(memory_space=pltpu.VMEM)
    outs = pl.pallas_call(
        body, name="adamw_small", in_specs=[vm] * (4 * n), out_specs=[vm] * (3 * n),
        out_shape=[_sds(w.shape, F32) for w in ws] * 3)(*ws, *gs, *ms, *vs)
    return outs[:n], outs[n:2 * n], outs[2 * n:]


BIG = ("w_in", "w_uq", "w_ukv", "w_out", "w_ffn_up", "w_ffn_down")
SMALL_SHARDED = ("conv_w", "ffn_conv_w", "meta_tokens")
REPLICATED = ("mix_norm_g", "q_norm_g", "kv_norm_g", "conv_b", "conv_ln_g", "conv_ln_b", "conv_out_g", "attn_out_g",
              "ffn_norm_g", "ffn_conv_b", "final_norm_g")
WEIGHTS = ("meta_tokens", "mix_norm_g", "w_in", "q_norm_g", "w_uq", "kv_norm_g", "w_ukv", "conv_w", "conv_b",
           "conv_ln_g", "conv_ln_b", "conv_out_g", "attn_out_g", "w_out", "ffn_norm_g", "w_ffn_up", "ffn_conv_w",
           "ffn_conv_b", "w_ffn_down", "final_norm_g")


def _lane_rows(a):
    return a.reshape(N_CHIPS, -1, LANES)


def _col_shards(a):
    k = a.shape[0]
    return a.reshape(k, N_CHIPS, -1).transpose(1, 0, 2)


def _from_col_shards(a):
    return a.transpose(1, 0, 2).reshape(a.shape[1], -1)


def _pad_rows_to(a, rows):
    return jnp.pad(a, ((0, 0), (0, rows - a.shape[1]), (0, 0)))


def _rope_tables(L):
    pos = (jnp.arange(L, dtype=jnp.int32) - DEAD).astype(F32)
    inv_freq = 1.0 / (ROPE_THETA ** (jnp.arange(0, QK_ROPE, 2, dtype=F32) / QK_ROPE))
    ang = pos[:, None] * inv_freq[None, :]
    cos, sin = jnp.cos(ang), jnp.sin(ang)
    half = QK_ROPE // 2
    z = lambda n: jnp.zeros((L, n), F32)
    rc = jnp.concatenate([jnp.ones((L, QK_NOPE), F32), cos, cos, z(HB - QK_NOPE - QK_ROPE)], 1)
    rs1 = jnp.concatenate([z(QK_NOPE), -sin, z(HB - QK_NOPE - half)], 1)
    rs2 = jnp.concatenate([z(QK_NOPE + half), sin, z(HB - QK_NOPE - QK_ROPE)], 1)
    return rc, rs1, rs2


def _pad_heads(g):
    return jnp.pad(g.reshape(N_HEADS, V_HEAD), ((0, 0), (HB - V_HEAD, 0))).reshape(1, D_HEADS)


def _unpad_heads(g):
    return g.reshape(N_HEADS, HB)[:, HB - V_HEAD:].reshape(1, N_HEADS * V_HEAD)


def _local_step(x, target, w, late_shards=None, reduce_first=False):
    S = x.shape[0]
    L = TM + S
    tl = L // 4
    d_qk = QK_NOPE + QK_ROPE
    win_n = w["w_in"]
    kr0 = D_AG + Q_LORA + KV_LORA
    win = jnp.concatenate([win_n[:, :kr0], jnp.zeros((D_MODEL, QK_NOPE), BF16), win_n[:, kr0:],
                           jnp.zeros((D_MODEL, HB - d_qk), BF16)], 1)
    wuq = jnp.pad(w["w_uq"].reshape(Q_LORA, N_HEADS, d_qk), ((0, 0), (0, 0), (0, HB - d_qk))).reshape(Q_LORA, D_HEADS)
    wukv = w["w_ukv"]
    ga = _pad_heads(w["attn_out_g"])
    gfin = w["final_norm_g"].reshape(1, D_MODEL)
    rc, rs1, rs2 = _rope_tables(L)
    head = jnp.concatenate([jnp.zeros((DEAD, D_MODEL), F32), w["meta_tokens"]], 0)

    h0, n, zag, u0, cq, ckv, qn, kvn, q, kv, kr, kvt = _fwd_in(x, head, w["mix_norm_g"], win, w["q_norm_g"], wuq,
                                                                w["kv_norm_g"], wukv, rc, rs1, rs2)
    u1, mixa = _fwd_conv(u0, w["conv_w"], w["conv_b"], w["conv_ln_g"], w["conv_ln_b"], w["conv_out_g"])
    if late_shards is None:
        o, lse, _ = _attn_fwd(q, kv, kvt, kr)
        wout_n, wup, wdown = w["w_out"], _col_shards(w["w_ffn_up"]), w["w_ffn_down"]
    else:
        o, lse, late = _attn_fwd(q, kv, kvt, kr, [late_shards[k] for k in LATE])
        wout_n, wup, wdown = _full_weight("w_out", late[0]), late[1], _full_weight("w_ffn_down", late[2])
    wout = jnp.concatenate([wout_n[:D_CONV], jnp.pad(wout_n[D_CONV:].reshape(N_HEADS, V_HEAD, D_MODEL),
                                                     ((0, 0), (HB - V_HEAD, 0), (0, 0))).reshape(D_HEADS, D_MODEL)], 0)
    mix, h1, n2 = _fwd_mix(h0, mixa, o, ga, wout, w["ffn_norm_g"])
    up0 = _mm(n2, wup, NN, BF16, tl, FF_TILE, D_MODEL, "ffn_up", b_slabs=True)
    dh2, loss, g_fin, up, act = _fwd_ffn_loss(up0, w["ffn_conv_w"], w["ffn_conv_b"], wdown, h1, target, gfin)

    dup, g_fcb = _bwd_ffn_act(dh2, up, wdown)
    dn2, dup0, g_fcw_tiles = _bwd_ffn_conv_up(dup, up0, w["ffn_conv_w"], wup, tl)
    g_fcw = jnp.sum(g_fcw_tiles, 0)
    g_wup = _mm(n2, dup0, TN, BF16, D_MODEL, FF_TILE, tl, "ffn_up_dw", by_col_tile=True)
    g_wdown = _mm(act, dh2, TN, BF16, D_FF // 2, D_MODEL, tl, "ffn_down_dw").reshape(N_CHIPS, -1, D_MODEL)
    dh1, dh1m, do, delta, du1, g_gffn, g_ga, g_og, g_lg, g_lb = _bwd_mix(
        dh2, dn2, h1, w["ffn_norm_g"], wout, o, ga, u1, w["conv_ln_g"], w["conv_ln_b"], w["conv_out_g"])
    g_wout = _mm(mix, dh1m, TN, BF16, D_MIX // 2, D_MODEL, tl, "out_dw")
    g_wout = jnp.concatenate([g_wout[:D_CONV], g_wout[D_CONV:].reshape(N_HEADS, HB, D_MODEL)[:, HB - V_HEAD:]
                              .reshape(N_HEADS * V_HEAD, D_MODEL)], 0).reshape(N_CHIPS, -1, D_MODEL)
    pair = _reduce_begin([g_wup, g_wdown, g_wout], "first") if reduce_first else ()
    dq, dkv, dkr, got = _attn_bwd(q, kv, kr, do, lse, delta, pair)
    if reduce_first:
        g_wup, g_wdown, g_wout = _reduce_end(pair, got, "first")
    dzag, g_cw, g_cb = _bwd_conv(du1, u0, w["conv_w"], zag)
    dz, dqr, gx, dfirst, g_gq, g_gkv, g_gmix = _bwd_in(dzag, dq, dkv, dkr, cq, ckv, w["q_norm_g"], w["kv_norm_g"],
                                                      wuq, wukv, win, rc, rs1, rs2, h0, w["mix_norm_g"], dh1)
    g_win = _mm(n, dz, TN, BF16, D_MODEL, D_ZP // 2, tl, "in_dw")
    g_wuq = _mm(qn, dqr, TN, BF16, Q_LORA, D_HEADS, tl, "uq_dw")
    g_wukv = _mm(kvn, dkv, TN, BF16, KV_LORA, D_HEADS // N_CHIPS, tl, "ukv_dw", by_col_tile=True)

    grads = {
        "w_in": _col_shards(jnp.concatenate([g_win[:, :kr0], g_win[:, kr0 + QK_NOPE:kr0 + d_qk]], 1)),
        "w_uq": _col_shards(g_wuq.reshape(Q_LORA, N_HEADS, HB)[:, :, :d_qk].reshape(Q_LORA, N_HEADS * d_qk)),
        "w_ukv": g_wukv,
        "w_out": g_wout,
        "w_ffn_up": g_wup, "w_ffn_down": g_wdown, "conv_w": g_cw, "ffn_conv_w": g_fcw,
        "meta_tokens": dfirst[DEAD:], "mix_norm_g": g_gmix, "q_norm_g": g_gq, "kv_norm_g": g_gkv, "conv_b": g_cb,
        "conv_ln_g": g_lg, "conv_ln_b": g_lb, "conv_out_g": g_og, "attn_out_g": _unpad_heads(g_ga),
        "ffn_norm_g": g_gffn, "ffn_conv_b": g_fcb, "final_norm_g": g_fin,
    }
    return loss, gx, grads


ROW_SHARDED = ("w_out", "w_ffn_down")


TRANSPOSED = ("w_in", "w_uq")
REDUCED_FIRST = ("w_ffn_up", "w_ffn_down", "w_out")
LATE = ("w_out", "w_ffn_up", "w_ffn_down")


def _full_weight(name, by_chip):
    return by_chip.reshape(-1, by_chip.shape[-1]) if name in ROW_SHARDED else _from_col_shards(by_chip)


def kernel(x, meta_tokens, mix_norm_g, w_in, q_norm_g, w_uq, kv_norm_g, w_ukv, conv_w, conv_b, conv_ln_g, conv_ln_b, conv_out_g, attn_out_g, w_out, ffn_norm_g, w_ffn_up, ffn_conv_w, ffn_conv_b, w_ffn_down, final_norm_g, loss_target, m_meta_tokens, m_mix_norm_g, m_w_in, m_q_norm_g, m_w_uq, m_kv_norm_g, m_w_ukv, m_conv_w, m_conv_b, m_conv_ln_g, m_conv_ln_b, m_conv_out_g, m_attn_out_g, m_w_out, m_ffn_norm_g, m_w_ffn_up, m_ffn_conv_w, m_ffn_conv_b, m_w_ffn_down, m_final_norm_g, v_meta_tokens, v_mix_norm_g, v_w_in, v_q_norm_g, v_w_uq, v_kv_norm_g, v_w_ukv, v_conv_w, v_conv_b, v_conv_ln_g, v_conv_ln_b, v_conv_out_g, v_attn_out_g, v_w_out, v_ffn_norm_g, v_w_ffn_up, v_ffn_conv_w, v_ffn_conv_b, v_w_ffn_down, v_final_norm_g):
    local = dict(meta_tokens=meta_tokens, mix_norm_g=mix_norm_g, w_in=w_in[0], q_norm_g=q_norm_g, w_uq=w_uq[0],
                 kv_norm_g=kv_norm_g, w_ukv=w_ukv[0], conv_w=conv_w[0], conv_b=conv_b, conv_ln_g=conv_ln_g,
                 conv_ln_b=conv_ln_b, conv_out_g=conv_out_g, attn_out_g=attn_out_g, w_out=w_out[0],
                 ffn_norm_g=ffn_norm_g, w_ffn_up=w_ffn_up[0], ffn_conv_w=ffn_conv_w[0], ffn_conv_b=ffn_conv_b,
                 w_ffn_down=w_ffn_down[0], final_norm_g=final_norm_g.reshape(1, D_MODEL))
    ms = dict(zip(WEIGHTS, (m_meta_tokens, m_mix_norm_g, m_w_in, m_q_norm_g, m_w_uq, m_kv_norm_g, m_w_ukv, m_conv_w,
                            m_conv_b, m_conv_ln_g, m_conv_ln_b, m_conv_out_g, m_attn_out_g, m_w_out, m_ffn_norm_g,
                            m_w_ffn_up, m_ffn_conv_w, m_ffn_conv_b, m_w_ffn_down, m_final_norm_g)))
    vs = dict(zip(WEIGHTS, (v_meta_tokens, v_mix_norm_g, v_w_in, v_q_norm_g, v_w_uq, v_kv_norm_g, v_w_ukv, v_conv_w,
                            v_conv_b, v_conv_ln_g, v_conv_ln_b, v_conv_out_g, v_attn_out_g, v_w_out, v_ffn_norm_g,
                            v_w_ffn_up, v_ffn_conv_w, v_ffn_conv_b, v_w_ffn_down, v_final_norm_g)))

    small_flat = jnp.concatenate([local[k].reshape(-1) for k in SMALL_SHARDED]).reshape(-1, LANES)
    early = [k for k in BIG if k not in LATE]
    gathered = _all_gather([local[k].astype(BF16) for k in early] + [small_flat])
    full = {k: v for k, v in local.items() if k not in LATE}
    for name, g in zip(early, gathered[:len(early)]):
        full[name] = _full_weight(name, g)
    small = gathered[-1].reshape(N_CHIPS, -1)
    at = 0
    for name in SMALL_SHARDED:
        r, c = local[name].shape
        full[name] = _from_col_shards(small[:, at:at + r * c].reshape(N_CHIPS, r, c))
        at += r * c

    loss_row, grad_x, grads = _local_step(x[0], loss_target[0], full, {k: local[k].astype(BF16) for k in LATE},
                                          reduce_first=True)

    rest_big = [k for k in BIG if k not in REDUCED_FIRST]
    rep = jnp.concatenate([grads[k].reshape(-1) for k in REPLICATED] + [loss_row.reshape(-1)]).reshape(1, -1, LANES)
    small_pieces = [_lane_rows(_col_shards(grads[k])) for k in SMALL_SHARDED]
    small_pieces.append(jnp.broadcast_to(rep, (N_CHIPS,) + rep.shape[1:]))
    small_rows = sum(p.shape[1] for p in small_pieces)
    small_pack = _pad_rows_to(jnp.concatenate(small_pieces, 1), -(-small_rows // 32) * 32)
    pair = _reduce_begin([grads[k] for k in rest_big] + [small_pack], "rest")
    *rest_tot, small_tot = _reduce_end(pair, _chip_exchange(pair, "grad_chip_exchange_rest"), "rest")

    total = {k: grads[k] for k in REDUCED_FIRST}
    total.update(zip(rest_big, rest_tot))
    flat = small_tot.reshape(-1)
    at = 0
    for name in SMALL_SHARDED + REPLICATED:
        shape = local[name].shape
        size = shape[0] * shape[1]
        total[name] = flat[at:at + size].reshape(shape)
        at += -(-size // LANES) * LANES if name in SMALL_SHARDED else size
    loss = flat[at]

    delta, new_m, new_v = {}, {}, {}
    shape2 = lambda a, name: a.reshape(local[name].shape)
    turn = lambda a, name: a.T if name in TRANSPOSED else a
    for name in BIG:
        outs = _adamw_big(turn(local[name], name), turn(total[name], name), turn(shape2(ms[name], name), name),
                          turn(shape2(vs[name], name), name), "adamw_" + name)
        total[name], delta[name], new_m[name], new_v[name] = (turn(o, name) for o in outs)
    rest = SMALL_SHARDED + REPLICATED
    ds, nms, nvs = _adamw_small([local[k] for k in rest], [total[k] for k in rest],
                                [shape2(ms[k], k) for k in rest], [shape2(vs[k], k) for k in rest])
    for k, d, nm, nv in zip(rest, ds, nms, nvs):
        delta[k], new_m[k], new_v[k] = d, nm, nv

    out_shape = dict(zip(WEIGHTS, (meta_tokens, mix_norm_g, w_in, q_norm_g, w_uq, kv_norm_g, w_ukv, conv_w, conv_b,
                                   conv_ln_g, conv_ln_b, conv_out_g, attn_out_g, w_out, ffn_norm_g, w_ffn_up,
                                   ffn_conv_w, ffn_conv_b, w_ffn_down, final_norm_g)))
    outs = [loss, grad_x[None]]
    for group in (total, delta, new_m, new_v):
        outs += [group[k].reshape(out_shape[k].shape) for k in WEIGHTS]
    return tuple(outs)
```

```python
import functools

import jax
import jax.numpy as jnp
from jax import lax
from jax.experimental import pallas as pl
from jax.experimental.pallas import tpu as pltpu

F32 = jnp.float32
BF16 = jnp.bfloat16

D_MODEL = 1024
D_CONV = 512
CONV_WIDTH = 31
N_HEADS = 8
QK_NOPE = 64
QK_ROPE = 32
V_HEAD = 64
Q_LORA = 384
KV_LORA = 256
D_FF = 2816
FFN_CONV_WIDTH = 3
CHUNK_SHIFT = 6
N_META = 16
ROPE_THETA = 10000.0
EPS = 1e-6
NEG = -1e30
ADAM_LR = 0.001
ADAM_B1 = 0.9
ADAM_B2 = 0.999
ADAM_EPS = 1e-08
ADAM_WD = 0.01
ADAM_STEP = 10

LANES = 128
SUBLANES = 8
HB = LANES
D_HEADS = N_HEADS * HB
TM = 256
DEAD = TM - N_META
D_AG = 2 * D_CONV
D_ZP = D_AG + Q_LORA + KV_LORA + HB
D_MIX = D_CONV + D_HEADS
LN2 = 0.6931471805599453
Q_SCALE = (QK_NOPE + QK_ROPE) ** -0.5 / LN2
HALO_CONV = 32
HALO_FFN = 16
FF_CHUNK = 256
FF_MXU_CHUNK = 256
FF_TILE = D_FF // 2
VMEM_LIMIT = 56 * 1024 * 1024
ADAMW_MAX_STEPS = 32
N_CHIPS = 4
HEAD_GROUP = 4
N_GROUPS = N_HEADS // HEAD_GROUP
MESH =pl.DeviceIdType.MESH


def _params(n_grid):
    return pltpu.CompilerParams(dimension_semantics=("arbitrary",) * n_grid, vmem_limit_bytes=VMEM_LIMIT)


def _rows(tm, c, off=0):
    return pl.BlockSpec((tm, c), lambda i: (i, off))


def _full(shape):
    return pl.BlockSpec(shape, lambda i: (0,) * len(shape))


def _prev(hb, c, tm, off=0):
    return pl.BlockSpec((hb, c), lambda i: (jnp.maximum(i * (tm // hb) - 1, 0), off))


def _next(hb, c, tm, nblk, off=0):
    return pl.BlockSpec((hb, c), lambda i: (jnp.minimum((i + 1) * (tm // hb), nblk - 1), off))


def _sds(shape, dtype):
    return jax.ShapeDtypeStruct(shape, dtype)


def _rms_r(x, n):
    return lax.rsqrt(jnp.sum(x * x, -1, keepdims=True) * (1.0 / n) + EPS)


def _rms_bwd(dy, x, r, g, n):
    gd = dy * g
    dx = r * gd - x * (r * r * r) * (jnp.sum(x * gd, -1, keepdims=True) * (1.0 / n))
    return dx, jnp.sum(dy * x * r, 0, keepdims=True)


def _dot(a, b, dims):
    return lax.dot_general(a, b, (dims, ((), ())), preferred_element_type=F32)


NN = ((1,), (0,))
NT = ((1,), (1,))
TN = ((0,), (0,))


def _rope(x, c, s1, s2):
    n = x.shape[-1]
    return x * c + pltpu.roll(x, n - QK_ROPE // 2, 1) * s1 + pltpu.roll(x, QK_ROPE // 2, 1) * s2


def _rope_bwd(g, c, s1, s2):
    n = g.shape[-1]
    return g * c + pltpu.roll(g * s1, QK_ROPE // 2, 1) + pltpu.roll(g * s2, n - QK_ROPE // 2, 1)


def _row_ids(tm, cols=1):
    return pl.program_id(0) * tm + lax.broadcasted_iota(jnp.int32, (tm, cols), 0)


def _mm(a, b, dims, out_dtype, tm, tn, tk, name, by_col_tile=False, b_slabs=False):
    if dims == TN:
        (kk, m), (_, n) = a.shape, b.shape
        a_spec = pl.BlockSpec((tk, tm), lambda i, j, k: (k, i))
    else:
        m, kk = a.shape
        a_spec = pl.BlockSpec((tm, tk), lambda i, j, k: (i, k))
    if dims == NT and b_slabs:
        n = b.shape[1]
        assert b.shape[2] == tk and kk == b.shape[0] * tk, (name, b.shape)
        b_spec = pl.BlockSpec((None, tn, tk), lambda i, j, k: (k, j, 0))
    elif dims == NT:
        n = b.shape[0]
        b_spec = pl.BlockSpec((tn, tk), lambda i, j, k: (j, k))
    elif b_slabs:
        n = b.shape[0] * b.shape[2]
        assert b.shape[2] == tn and kk == b.shape[1], (name, b.shape)
        b_spec = pl.BlockSpec((None, tk, tn), lambda i, j, k: (j, k, 0))
    else:
        n = b.shape[1]
        b_spec = pl.BlockSpec((tk, tn), lambda i, j, k: (k, j))
    assert m % tm == 0 and n % tn == 0 and kk % tk == 0, (name, a.shape, b.shape, tm, tn, tk)
    nk = kk // tk

    def body(a_ref, b_ref, o_ref, acc_ref):
        k = pl.program_id(2)

        @pl.when(k == 0)
        def _():
            acc_ref[...] = jnp.zeros_like(acc_ref)

        acc_ref[...] += _dot(a_ref[...].astype(BF16), b_ref[...].astype(BF16), dims)

        @pl.when(k == nk - 1)
        def _():
            o_ref[...] = acc_ref[...].astype(out_dtype)

    if by_col_tile:
        out_spec, out_shape = pl.BlockSpec((None, tm, tn), lambda i, j, k: (j, i, 0)), (n // tn, m, tn)
    else:
        out_spec, out_shape = pl.BlockSpec((tm, tn), lambda i, j, k: (i, j)), (m, n)
    return pl.pallas_call(
        body, name=name, grid=(m // tm, n // tn, nk), in_specs=[a_spec, b_spec], out_specs=out_spec,
        out_shape=_sds(out_shape, out_dtype), scratch_shapes=[pltpu.VMEM((tm, tn), F32)],
        compiler_params=_params(3))(a, b)


def _fwd_in(x, head, gmix, win, gq, wuq, gkv, wukv, rc, rs1, rs2):
    L = TM + x.shape[0]

    def body(x_ref, head_ref, gmix_ref, win_ref, gq_ref, wuq_ref, gkv_ref, wukv_ref, c_ref, s1_ref, s2_ref,
             h0_ref, n_ref, zag_ref, u0_ref, cq_ref, ckv_ref, qn_ref, kvn_ref, q_ref, kv_ref, kr_ref, kvt_ref):
        h = jnp.where(pl.program_id(0) == 0, head_ref[...], x_ref[...])
        h0_ref[...] = h
        n = (h * _rms_r(h, D_MODEL) * gmix_ref[...]).astype(BF16)
        n_ref[...] = n
        z = _dot(n, win_ref[...], NN)
        a, gate = z[:, :D_CONV], z[:, D_CONV:D_AG]
        zag_ref[...] = z[:, :D_AG].astype(BF16)
        u0_ref[...] = a * jax.nn.sigmoid(gate)
        cq = z[:, D_AG:D_AG + Q_LORA]
        ckv = z[:, D_AG + Q_LORA:D_AG + Q_LORA + KV_LORA]
        krp = z[:, D_AG + Q_LORA + KV_LORA:]
        cq_ref[...] = cq
        ckv_ref[...] = ckv
        qn = (cq * _rms_r(cq, Q_LORA) * gq_ref[...]).astype(BF16)
        qn_ref[...] = qn
        kvn = (ckv * _rms_r(ckv, KV_LORA) * gkv_ref[...]).astype(BF16)
        kvn_ref[...] = kvn
        c, s1, s2 = c_ref[...], s1_ref[...], s2_ref[...]
        q = _dot(qn, wuq_ref[...], NN)
        q = _rope(q, jnp.tile(c, (1, N_HEADS)), jnp.tile(s1, (1, N_HEADS)), jnp.tile(s2, (1, N_HEADS)))
        q_ref[...] = (q * Q_SCALE).astype(BF16)
        kv = _dot(kvn, wukv_ref[...], NN)
        kv_ref[...] = kv.astype(BF16)
        kvt_ref[...] = kv.T.astype(BF16)
        kr_ref[...] = _rope(krp, c, s1, s2).astype(BF16)

    outs = [(D_MODEL, F32), (D_MODEL, BF16), (D_AG, BF16), (D_CONV, F32), (Q_LORA, F32), (KV_LORA, F32), (Q_LORA, BF16),
            (KV_LORA, BF16), (D_HEADS, BF16), (D_HEADS, BF16), (HB, BF16)]
    return pl.pallas_call(
        body, name="fwd_in", grid=(L // TM,),
        in_specs=[pl.BlockSpec((TM, D_MODEL), lambda i: (jnp.maximum(i - 1, 0), 0)), _full(head.shape),
                  _full(gmix.shape), _full(win.shape), _full(gq.shape), _full(wuq.shape),
                  _full(gkv.shape), _full(wukv.shape), _rows(TM, HB), _rows(TM, HB), _rows(TM, HB)],
        out_specs=[_rows(TM, c) for c, _ in outs] + [pl.BlockSpec((D_HEADS, TM), lambda i: (0, i))],
        out_shape=[_sds((L, c), d) for c, d in outs] + [_sds((D_HEADS, L), BF16)],
        compiler_params=_params(1))(x, head, gmix, win, gq, wuq, gkv, wukv, rc, rs1, rs2)


def _conv_taps(xx, w_ref, halo, tm, flip):
    kw = w_ref.shape[0]
    acc, rolls = None, {}
    for k in range(kw):
        term = w_ref[k:k + 1, :] * _shifted_rows(xx, kw - 1 - k, halo, tm, flip, rolls)
        acc = term if acc is None else acc + term
    return acc


def _shifted_rows(xx, d, halo, tm, flip, rolls):
    a, b = divmod(d, SUBLANES)
    if b not in rolls:
        rolls[b] = xx if b == 0 else pltpu.roll(xx, (xx.shape[0] - b) if flip else b, 0)
    start = SUBLANES * a if flip else halo - SUBLANES * a
    return rolls[b][start:start + tm]


def _ln_silu(u1, lg, lb):
    mu = jnp.mean(u1, -1, keepdims=True)
    xc = u1 - mu
    rs = lax.rsqrt(jnp.mean(xc * xc, -1, keepdims=True) + EPS)
    xh = xc * rs
    u2 = xh * lg + lb
    sg = jax.nn.sigmoid(u2)
    return rs, xh, u2, sg, u2 * sg


def _fwd_conv(u0, cw, cb, lg, lb, og):
    L = u0.shape[0]

    def body(u0_ref, u0p_ref, cw_ref, cb_ref, lg_ref, lb_ref, og_ref, u1_ref, mixa_ref):
        halo = jnp.where(pl.program_id(0) > 0, u0p_ref[...], 0.0)
        xx = jnp.concatenate([halo, u0_ref[...]], 0)
        u1 = _conv_taps(xx, cw_ref, HALO_CONV, TM, False) + cb_ref[...]
        u1_ref[...] = u1
        u = _ln_silu(u1, lg_ref[...], lb_ref[...])[4]
        mixa_ref[...] = (u * _rms_r(u, D_CONV) * og_ref[...]).astype(BF16)

    return pl.pallas_call(
        body, name="fwd_conv", grid=(L // TM,),
        in_specs=[_rows(TM, D_CONV), _prev(HALO_CONV, D_CONV, TM), _full(cw.shape), _full(cb.shape),
                  _full(lg.shape), _full(lb.shape), _full(og.shape)],
        out_specs=[_rows(TM, D_CONV), _rows(TM, D_CONV)],
        out_shape=[_sds((L, D_CONV), F32), _sds((L, D_CONV), BF16)],
        compiler_params=_params(1))(u0, u0, cw, cb, lg, lb, og)


def _visible(i, j, t):
    row = i * t + lax.broadcasted_iota(jnp.int32, (t, t), 0)
    col = j * t + lax.broadcasted_iota(jnp.int32, (t, t), 1)
    return (lax.shift_right_logical(col, CHUNK_SHIFT) <= lax.shift_right_logical(row, CHUNK_SHIFT)) & (col >= DEAD)


def _visible_t(i, j, t):
    key = j * t + lax.broadcasted_iota(jnp.int32, (t, t), 0)
    query = i * t + lax.broadcasted_iota(jnp.int32, (t, t), 1)
    return (lax.shift_right_logical(key, CHUNK_SHIFT) <= lax.shift_right_logical(query, CHUNK_SHIFT)) & (key >= DEAD)


def _stat_lane(h):
    return (h // HEAD_GROUP) * LANES + h % HEAD_GROUP


def _scatter_stats(cols, t):
    lane = lax.broadcasted_iota(jnp.int32, (t, N_GROUPS * LANES), 1)
    out = jnp.zeros((t, N_GROUPS * LANES), F32)
    for h, col in enumerate(cols):
        out = jnp.where(lane == _stat_lane(h), col, out)
    return out


def _resident(shape, index_map):
    return pl.BlockSpec(shape, index_map, pipeline_mode=pl.Buffered(1))


def _attn_fwd(q, kv, kvt, kr, shards=()):
    L = q.shape[0]
    t = TM
    nq = L // t
    n = len(shards)
    pass_step = (5 * nq) // 6

    def body(q_ref, kv_ref, kvt_ref, kr_ref, *refs):
        gather_refs = refs[:n] + refs[n + 2:2 * n + 2] + refs[2 * n + 6:]
        o_ref, lse_ref = refs[n:n + 2]
        qt_scr, m_scr, l_scr, acc_scr = refs[2 * n + 2:2 * n + 6]
        i = pl.program_id(0)
        if n:
            @pl.when(i == 0)
            def _():
                for cp in _gather_copies(gather_refs[:n], gather_refs[n:2 * n], *gather_refs[2 * n:])[0]:
                    cp().start()

        lane = lax.broadcasted_iota(jnp.int32, (t, HB), 1)
        heads = range(N_HEADS)
        cols = [slice(h * HB, (h + 1) * HB) for h in heads]
        for h in heads:
            qt_scr[cols[h], :] = q_ref[:, cols[h]].astype(F32).T.astype(BF16)
        m_scr[...] = jnp.full_like(m_scr, NEG)
        l_scr[...] = jnp.zeros_like(l_scr)
        acc_scr[...] = jnp.zeros_like(acc_scr)

        def tile(keys, vis, whole=True):
            krj = kr_ref[keys, :]
            kvj = [kv_ref[keys, cols[h]] for h in heads]
            lane_k = lane[:krj.shape[0]]
            s = [_dot(jnp.where(lane_k < QK_NOPE, kvj[h], krj), qt_scr[cols[h], :], NN) for h in heads]
            p, alpha = [], []
            for h in heads:
                sh = s[h] if vis is None else jnp.where(vis, s[h], NEG)
                m_prev = m_scr[h:h + 1, :]
                m_new = jnp.maximum(m_prev, jnp.max(sh, 0, keepdims=True))
                a = jnp.exp2(m_prev - m_new)
                ph = jnp.exp2(sh - m_new)
                l_scr[h:h + 1, :] = a * l_scr[h:h + 1, :] + jnp.sum(ph, 0, keepdims=True)
                m_scr[h:h + 1, :] = m_new
                p.append(ph.astype(BF16))
                alpha.append(a)
            for h in heads:
                pv = _dot(kvt_ref[cols[h], keys], p[h], NN) if whole else _dot(kvj[h], p[h], TN)
                acc_scr[cols[h], :] = alpha[h] * acc_scr[cols[h], :] + pv

        tile(pl.ds(pl.multiple_of(i * t, t), t), _visible_t(i, i, t))

        @pl.when(i > 0)
        def _():
            tile(pl.ds(DEAD, N_META), None, whole=False)

        def unmasked(j, carry):
            tile(pl.ds(pl.multiple_of(j * t, t), t), None)
            return carry

        lax.fori_loop(1, i, unmasked, 0)
        lse_ref[...] = jnp.zeros_like(lse_ref)
        for h in heads:
            l = l_scr[h:h + 1, :]
            o_ref[:, cols[h]] = jnp.where(lane >= QK_NOPE, (acc_scr[cols[h], :] / l).T, 0.0).astype(BF16)
            lse_ref[h // HEAD_GROUP, h % HEAD_GROUP:h % HEAD_GROUP + 1, :] = m_scr[h:h + 1, :] + jnp.log2(l)
        if n:
            @pl.when(i == pass_step)
            def _():
                _, arrivals, forwards, _ = _gather_copies(gather_refs[:n], gather_refs[n:2 * n], *gather_refs[2 * n:])
                for landed, onward in zip(arrivals, forwards):
                    landed().wait_recv()
                    onward().start()

            @pl.when(i == nq - 1)
            def _():
                sends, _, forwards, finals = _gather_copies(gather_refs[:n], gather_refs[n:2 * n], *gather_refs[2 * n:])
                for cp in finals:
                    cp().wait_recv()
                for cp in sends + forwards:
                    cp().wait_send()

    any_spec = pl.BlockSpec(memory_space=pl.ANY)
    outs = pl.pallas_call(
        body, name="attn_fwd", grid=(nq,),
        in_specs=[_rows(t, D_HEADS), _resident((L, D_HEADS), lambda i: (0, 0)),
                  _resident((D_HEADS, L), lambda i: (0, 0)), _resident((L, HB), lambda i: (0, 0))] + [any_spec] * n,
        out_specs=[_rows(t, D_HEADS), pl.BlockSpec((N_GROUPS, SUBLANES, t), lambda i: (0, 0, i))] + [any_spec] * n,
        out_shape=[_sds((L, D_HEADS), BF16), _sds((N_GROUPS, SUBLANES, L), F32)] + _gather_out_shapes(shards),
        scratch_shapes=[pltpu.VMEM((D_HEADS, t), BF16), pltpu.VMEM((N_HEADS, t), F32), pltpu.VMEM((N_HEADS, t), F32),
                        pltpu.VMEM((D_HEADS, t), F32)] + (_gather_semaphores(n) if n else []),
        compiler_params=_params(1))(q, kv, kvt, kr, *shards)
    return outs[0], outs[1], _gathered(outs[2:], shards)


def _fwd_mix(h0, mixa, o, ga, wout, gffn):
    L = h0.shape[0]

    def body(h0_ref, mixa_ref, o_ref, ga_ref, wout_ref, gffn_ref, mix_ref, h1_ref, n2_ref):
        of = o_ref[...].astype(F32)
        mixb = (of * _rms_r(of, N_HEADS * V_HEAD) * ga_ref[...]).astype(BF16)
        mix = jnp.concatenate([mixa_ref[...], mixb], 1)
        mix_ref[...] = mix
        mo = jnp.where(_row_ids(TM) >= DEAD, _dot(mix, wout_ref[...], NN), 0.0)
        h1 = h0_ref[...] + mo
        h1_ref[...] = h1
        n2_ref[...] = (h1 * _rms_r(h1, D_MODEL) * gffn_ref[...]).astype(BF16)

    return pl.pallas_call(
        body, name="fwd_mix", grid=(L // TM,),
        in_specs=[_rows(TM, D_MODEL), _rows(TM, D_CONV), _rows(TM, D_HEADS), _full(ga.shape), _full(wout.shape),
                  _full(gffn.shape)],
        out_specs=[_rows(TM, D_MIX), _rows(TM, D_MODEL), _rows(TM, D_MODEL)],
        out_shape=[_sds((L, D_MIX), BF16), _sds((L, D_MODEL), F32), _sds((L, D_MODEL), BF16)],
        compiler_params=_params(1))(h0, mixa, o, ga, wout, gffn)


def _ffn_act_chunk(c, upg_ref, upv_ref, hg, hv, fcw_ref, fcb_ref):
    cs = slice(c * FF_CHUNK, (c + 1) * FF_CHUNK)
    out = []
    for part, (up_ref, halo) in enumerate(((upg_ref, hg), (upv_ref, hv))):
        xx = jnp.concatenate([halo[:, cs], up_ref[:, cs].astype(F32)], 0)
        ws = slice(part * D_FF + c * FF_CHUNK, part * D_FF + (c + 1) * FF_CHUNK)
        y = (fcw_ref[0:1, ws] * pltpu.roll(xx, 2, 0)[HALO_FFN:] + fcw_ref[1:2, ws] * pltpu.roll(xx, 1, 0)[HALO_FFN:]
             + fcw_ref[2:3, ws] * xx[HALO_FFN:] + fcb_ref[:, ws])
        out.append(y)
    return out


def _ffn_in_specs(L):
    return [_rows(TM, D_FF, 0), _rows(TM, D_FF, 1), _prev(HALO_FFN, D_FF, TM, 0), _prev(HALO_FFN, D_FF, TM, 1)]


def _ffn_halos(hg_ref, hv_ref):
    first = pl.program_id(0) == 0
    return (jnp.where(first, 0.0, hg_ref[...].astype(F32)), jnp.where(first, 0.0, hv_ref[...].astype(F32)))


def _fwd_ffn_loss(up0, fcw, fcb, wdown, h1, target, gfin):
    L = h1.shape[0]

    def body(upg_ref, upv_ref, hg_ref, hv_ref, fcw_ref, fcb_ref, wd_ref, h1_ref, t_ref, gf_ref,
             dh2_ref, loss_ref, dgf_ref, up_ref, act_ref):
        i = pl.program_id(0)
        hg, hv = _ffn_halos(hg_ref, hv_ref)
        for c in range(D_FF // FF_CHUNK):
            cs = slice(c * FF_CHUNK, (c + 1) * FF_CHUNK)
            g, val = _ffn_act_chunk(c, upg_ref, upv_ref, hg, hv, fcw_ref, fcb_ref)
            up_ref[:, cs] = g.astype(BF16)
            up_ref[:, D_FF + c * FF_CHUNK:D_FF + (c + 1) * FF_CHUNK] = val.astype(BF16)
            act_ref[:, cs] = (g * jax.nn.sigmoid(g) * val).astype(BF16)
        h2 = h1_ref[...] + _dot(act_ref[...], wd_ref[...], NN)
        r = _rms_r(h2, D_MODEL)
        gf = gf_ref[...]
        err = jnp.where(i > 0, h2 * r * gf - t_ref[...], 0.0)
        dy = err * (1.0 / D_MODEL)
        dh2, dgf = _rms_bwd(dy, h2, r, gf, D_MODEL)
        dh2_ref[...] = dh2

        @pl.when(i == 0)
        def _():
            loss_ref[...] = jnp.zeros_like(loss_ref)
            dgf_ref[...] = jnp.zeros_like(dgf_ref)

        loss_ref[...] += jnp.sum(err * err) * (0.5 / D_MODEL)
        dgf_ref[...] += dgf

    return pl.pallas_call(
        body, name="fwd_ffn_loss", grid=(L // TM,),
        in_specs=_ffn_in_specs(L) + [_full(fcw.shape), _full(fcb.shape), _full(wdown.shape), _rows(TM, D_MODEL),
                                     pl.BlockSpec((TM, D_MODEL), lambda i: (jnp.maximum(i - 1, 0), 0)),
                                     _full(gfin.shape)],
        out_specs=[_rows(TM, D_MODEL), _full((1, LANES)), _full((1, D_MODEL)), _rows(TM, 2 * D_FF), _rows(TM, D_FF)],
        out_shape=[_sds((L, D_MODEL), F32), _sds((1, LANES), F32), _sds((1, D_MODEL), F32),
                   _sds((L, 2 * D_FF), BF16), _sds((L, D_FF), BF16)],
        compiler_params=_params(1))(up0, up0, up0, up0, fcw, fcb, wdown, h1, target, gfin)


def _bwd_ffn_act(dh2, up, wdown):
    L = dh2.shape[0]

    def body(dh2_ref, upg_ref, upv_ref, wd_ref, dup_ref, dfcb_ref):
        da = _dot(dh2_ref[...].astype(BF16), wd_ref[...], NT)

        @pl.when(pl.program_id(0) == 0)
        def _():
            dfcb_ref[...] = jnp.zeros_like(dfcb_ref)

        for c in range(D_FF // FF_CHUNK):
            cs = slice(c * FF_CHUNK, (c + 1) * FF_CHUNK)
            vs = slice(D_FF + c * FF_CHUNK, D_FF + (c + 1) * FF_CHUNK)
            g, val = upg_ref[:, cs].astype(F32), upv_ref[:, cs].astype(F32)
            sg = jax.nn.sigmoid(g)
            si = g * sg
            dac = da[:, cs]
            dg = dac * val * (sg * (1.0 + g * (1.0 - sg)))
            dv = dac * si
            dup_ref[:, cs] = dg.astype(BF16)
            dup_ref[:, vs] = dv.astype(BF16)
            dfcb_ref[:, cs] += jnp.sum(dg, 0, keepdims=True)
            dfcb_ref[:, vs] += jnp.sum(dv, 0, keepdims=True)

    return pl.pallas_call(
        body, name="bwd_ffn_act", grid=(L // TM,),
        in_specs=[_rows(TM, D_MODEL), _rows(TM, D_FF, 0), _rows(TM, D_FF, 1), _full(wdown.shape)],
        out_specs=[_rows(TM, 2 * D_FF), _full((1, 2 * D_FF))],
        out_shape=[_sds((L, 2 * D_FF), BF16), _sds((1, 2 * D_FF), F32)],
        compiler_params=_params(1))(dh2, up, up, wdown)


def _bwd_ffn_conv_up(dup, up0, fcw, wup, tl):
    L, C = dup.shape
    tc = FF_TILE
    nt = L // tl
    nhb = L // HALO_FFN
    chunks = [(c0, min(FF_MXU_CHUNK, tc - c0)) for c0 in range(0, tc, FF_MXU_CHUNK)]

    def body(dy_ref, dyn_ref, x_ref, xp_ref, w_ref, wup_ref, dn2_ref, dx_ref, dw_ref, acc_ref):
        i, k = pl.program_id(0), pl.program_id(1)

        @pl.when(k == 0)
        def _():
            acc_ref[...] = jnp.zeros_like(acc_ref)

        last, first = i == nt - 1, i == 0
        for c0, cw in chunks:
            cs = slice(c0, c0 + cw)
            yy = jnp.concatenate([dy_ref[:, cs].astype(F32), jnp.where(last, 0.0, dyn_ref[:, cs].astype(F32))], 0)
            w = w_ref[:, cs]
            dx = (w[0:1] * pltpu.roll(yy, tl + HALO_FFN - 2, 0)[:tl] + w[1:2] * pltpu.roll(yy, tl + HALO_FFN - 1, 0)[:tl]
                  + w[2:3] * yy[:tl]).astype(BF16)
            dx_ref[:, cs] = dx
            acc_ref[...] += _dot(dx, wup_ref[:, cs], NT)
            xx = jnp.concatenate([jnp.where(first, 0.0, xp_ref[:, cs].astype(F32)), x_ref[:, cs].astype(F32)], 0)
            dy = yy[:tl]
            dw_ref[0:1, cs] = jnp.sum(dy * pltpu.roll(xx, 2, 0)[HALO_FFN:], 0, keepdims=True)
            dw_ref[1:2, cs] = jnp.sum(dy * pltpu.roll(xx, 1, 0)[HALO_FFN:], 0, keepdims=True)
            dw_ref[2:3, cs] = jnp.sum(dy * xx[HALO_FFN:], 0, keepdims=True)

        @pl.when(k == C // tc - 1)
        def _():
            dn2_ref[...] = acc_ref[...]

    tile = pl.BlockSpec((tl, tc), lambda i, k: (i, k))
    per = tl // HALO_FFN
    return pl.pallas_call(
        body, name="bwd_ffn_conv_up", grid=(nt, C // tc),
        in_specs=[tile, pl.BlockSpec((HALO_FFN, tc), lambda i, k: (jnp.minimum((i + 1) * per, nhb - 1), k)),
                  tile, pl.BlockSpec((HALO_FFN, tc), lambda i, k: (jnp.maximum(i * per - 1, 0), k)),
                  pl.BlockSpec((FFN_CONV_WIDTH, tc), lambda i, k: (0, k)),
                  pl.BlockSpec((None, D_MODEL, tc), lambda i, k: (k, 0, 0))],
        out_specs=[pl.BlockSpec((tl, D_MODEL), lambda i, k: (i, 0)), tile,
                   pl.BlockSpec((None, FFN_CONV_WIDTH, tc), lambda i, k: (i, 0, k))],
        out_shape=[_sds((L, D_MODEL), F32), _sds((L, C), BF16), _sds((nt, FFN_CONV_WIDTH, C), F32)],
        scratch_shapes=[pltpu.VMEM((tl, D_MODEL), F32)],
        compiler_params=_params(2))(dup, dup, up0, up0, fcw, wup)


def _bwd_mix(dh2, dn2, h1, gffn, wout, o, ga, u1, lg, lb, og):
    L = h1.shape[0]

    def body(dh2_ref, dn2_ref, h1_ref, gffn_ref, wout_ref, o_ref, ga_ref, u1_ref, lg_ref, lb_ref, og_ref,
             dh1_ref, dh1m_ref, do_ref, delta_ref, du1_ref, dgffn_ref, dga_ref, dog_ref, dlg_ref, dlb_ref):
        h1 = h1_ref[...]
        dn2x, dgffn = _rms_bwd(dn2_ref[...], h1, _rms_r(h1, D_MODEL), gffn_ref[...], D_MODEL)
        dh1 = dh2_ref[...] + dn2x
        dh1_ref[...] = dh1
        dh1m = jnp.where(_row_ids(TM) >= DEAD, dh1, 0.0).astype(BF16)
        dh1m_ref[...] = dh1m
        dmix = _dot(dh1m, wout_ref[...], NT)
        dma, dmb = dmix[:, :D_CONV], dmix[:, D_CONV:]
        of = o_ref[...].astype(F32)
        do, dga = _rms_bwd(dmb, of, _rms_r(of, N_HEADS * V_HEAD), ga_ref[...], N_HEADS * V_HEAD)
        do_ref[...] = do.astype(BF16)
        prod = do * of
        by_lane = _scatter_stats([jnp.sum(prod[:, h * HB:(h + 1) * HB], -1, keepdims=True) for h in range(N_HEADS)], TM)
        by_row = by_lane.T
        for grp in range(N_GROUPS):
            delta_ref[grp] = by_row[grp * LANES:grp * LANES + SUBLANES, :]
        lg = lg_ref[...]
        rs, xh, u2, sg, u = _ln_silu(u1_ref[...], lg, lb_ref[...])
        du, dog = _rms_bwd(dma, u, _rms_r(u, D_CONV), og_ref[...], D_CONV)
        du2 = du * (sg * (1.0 + u2 * (1.0 - sg)))
        dxh = du2 * lg
        du1_ref[...] = rs * (dxh - jnp.mean(dxh, -1, keepdims=True) - xh * jnp.mean(dxh * xh, -1, keepdims=True))

        @pl.when(pl.program_id(0) == 0)
        def _():
            for ref in (dgffn_ref, dga_ref, dog_ref, dlg_ref, dlb_ref):
                ref[...] = jnp.zeros_like(ref)

        dgffn_ref[...] += dgffn
        dga_ref[...] += dga
        dog_ref[...] += dog
        dlg_ref[...] += jnp.sum(du2 * xh, 0, keepdims=True)
        dlb_ref[...] += jnp.sum(du2, 0, keepdims=True)

    return pl.pallas_call(
        body, name="bwd_mix", grid=(L // TM,),
        in_specs=[_rows(TM, D_MODEL), _rows(TM, D_MODEL), _rows(TM, D_MODEL), _full(gffn.shape), _full(wout.shape),
                  _rows(TM, D_HEADS), _full(ga.shape), _rows(TM, D_CONV), _full(lg.shape), _full(lb.shape),
                  _full(og.shape)],
        out_specs=[_rows(TM, D_MODEL), _rows(TM, D_MODEL), _rows(TM, D_HEADS),
                   pl.BlockSpec((N_GROUPS, SUBLANES, TM), lambda i: (0, 0, i)),
                   _rows(TM, D_CONV), _full((1, D_MODEL)), _full((1, D_HEADS)), _full((1, D_CONV)),
                   _full((1, D_CONV)), _full((1, D_CONV))],
        out_shape=[_sds((L, D_MODEL), F32), _sds((L, D_MODEL), BF16), _sds((L, D_HEADS), BF16),
                   _sds((N_GROUPS, SUBLANES, L), F32),
                   _sds((L, D_CONV), F32), _sds((1, D_MODEL), F32), _sds((1, D_HEADS), F32), _sds((1, D_CONV), F32),
                   _sds((1, D_CONV), F32), _sds((1, D_CONV), F32)],
        compiler_params=_params(1))(dh2, dn2, h1, gffn, wout, o, ga, u1, lg, lb, og)


def _attn_bwd(q, kv, kr, do, lse, delta, parts=()):
    L = q.shape[0]
    t = TM
    nt = L // t
    gw = HEAD_GROUP * HB
    n = len(parts)

    def body(q_ref, kv_ref, kr_ref, do_ref, lse_ref, delta_ref, *refs):
        dq_ref, dkv_ref, dkr_ref = refs[n:n + 3]
        dqt_acc, dk_acc, dv_acc, kkt_scr = refs[2 * n + 3:2 * n + 7]
        exchange_refs = (refs[:n], refs[n + 3:2 * n + 3]) + refs[2 * n + 7:]
        g, j = pl.program_id(0), pl.program_id(1)
        if n:
            @pl.when((g == 0) & (j == 0))
            def _():
                for cp in _chip_copies(*exchange_refs)[0]:
                    cp().start()

        lane = lax.broadcasted_iota(jnp.int32, (t, HB), 1)

        @pl.when(j == 0)
        def _():
            dqt_acc[...] = jnp.zeros_like(dqt_acc)

        @pl.when((j == 0) & (g == 0))
        def _():
            dkr_ref[...] = jnp.zeros_like(dkr_ref)

        dk_acc[...] = jnp.zeros_like(dk_acc)
        dv_acc[...] = jnp.zeros_like(dv_acc)
        krj = kr_ref[...]
        heads = range(HEAD_GROUP)
        cols = [slice(h * HB, (h + 1) * HB) for h in heads]
        for hc in cols:
            kkt_scr[hc, :] = jnp.where(lane < QK_NOPE, kv_ref[:, hc], krj).astype(F32).T.astype(BF16)

        def tile(i, vis, whole=True):
            qs = pl.ds(pl.multiple_of(i * t, t), t)
            keys = slice(None) if whole else slice(DEAD, t)
            kvj = [kv_ref[keys, hc] for hc in cols]
            lane_k = lane[:kvj[0].shape[0]]
            kk = [jnp.where(lane_k < QK_NOPE, kvj[h], krj[keys]) for h in heads]
            qi = [q_ref[qs, hc] for hc in cols]
            doi = [do_ref[qs, hc] for hc in cols]
            s = [_dot(kk[h], qi[h].astype(F32).T.astype(BF16), NN) for h in heads]
            dp = [_dot(kvj[h], doi[h].astype(F32).T.astype(BF16), NN) for h in heads]
            p = []
            for h in heads:
                sh = s[h] if vis is None else jnp.where(vis, s[h], NEG)
                p.append(jnp.exp2(sh - lse_ref[h:h + 1, qs]))
            for h in heads:
                dv_acc[keys, cols[h]] += _dot(p[h].astype(BF16), doi[h], NN)
            ds = [(p[h] * (dp[h] - delta_ref[h:h + 1, qs]) * LN2).astype(BF16) for h in heads]
            for h in heads:
                dk_acc[keys, cols[h]] += _dot(ds[h], qi[h], NN)
            for h in heads:
                dqt = _dot(kkt_scr[cols[h], :], ds[h], NN) if whole else _dot(kk[h], ds[h], TN)
                dqt_acc[cols[h], qs] += dqt

        @pl.when(j == 0)
        def _():
            tile(0, _visible_t(0, 0, t)[DEAD:], whole=False)

            def meta_keys(i, carry):
                tile(i, None, whole=False)
                return carry

            lax.fori_loop(1, nt, meta_keys, 0)

        @pl.when(j > 0)
        def _():
            tile(j, _visible_t(j, j, t))

            def unmasked(i, carry):
                tile(i, None)
                return carry

            lax.fori_loop(j + 1, nt, unmasked, 0)

        dkr = jnp.zeros((t, HB), F32)
        for h in range(HEAD_GROUP):
            hc = slice(h * HB, (h + 1) * HB)
            dk = dk_acc[:, hc]
            dkv_ref[:, hc] = jnp.where(lane < QK_NOPE, dk, dv_acc[:, hc]).astype(BF16)
            dkr = dkr + jnp.where(lane >= QK_NOPE, dk, 0.0)
        dkr_ref[pl.ds(pl.multiple_of(j * t, t), t), :] += dkr

        @pl.when(j == nt - 1)
        def _():
            def untranspose(i, carry):
                qs = pl.ds(pl.multiple_of(i * t, t), t)
                dq_ref[qs, :] = (dqt_acc[:, qs].T * Q_SCALE).astype(BF16)
                return carry

            lax.fori_loop(0, nt, untranspose, 0)

        if n:
            @pl.when((g == N_GROUPS - 1) & (j == nt - 1))
            def _():
                sends, arrivals = _chip_copies(*exchange_refs)
                for cp in arrivals:
                    cp().wait_recv()
                for cp in sends:
                    cp().wait_send()

    group = lambda g, j: (0, g)
    stats = _resident((None, SUBLANES, L), lambda g, j: (g, 0, 0))
    any_spec = pl.BlockSpec(memory_space=pl.ANY)
    outs = pl.pallas_call(
        body, name="attn_bwd", grid=(N_GROUPS, nt),
        in_specs=[_resident((L, gw), group), pl.BlockSpec((t, gw), lambda g, j: (j, g)),
                  pl.BlockSpec((t, HB), lambda g, j: (j, 0)), _resident((L, gw), group), stats, stats]
        + [any_spec] * n,
        out_specs=[pl.BlockSpec((L, gw), group), pl.BlockSpec((t, gw), lambda g, j: (j, g)),
                   pl.BlockSpec((L, HB), lambda g, j: (0, 0))] + [any_spec] * n,
        out_shape=[_sds((L, D_HEADS), BF16), _sds((L, D_HEADS), BF16), _sds((L, HB), F32)]
        + [_sds(p.shape, p.dtype) for p in parts],
        scratch_shapes=[pltpu.VMEM((gw, L), F32), pltpu.VMEM((t, gw), F32), pltpu.VMEM((t, gw), F32),
                        pltpu.VMEM((gw, t), BF16)] + (_chip_semaphores(n) if n else []),
        compiler_params=_params(2))(q, kv, kr, do, lse, delta, *parts)
    return outs[0], outs[1], outs[2], list(outs[3:])


def _bwd_conv(du1, u0, cw, zag):
    L = du1.shape[0]
    nt = L // TM

    def body(dy_ref, dyn_ref, x_ref, xp_ref, cw_ref, zag_ref, dzag_ref, dcw_ref, dcb_ref):
        i = pl.program_id(0)
        dy = dy_ref[...]
        yy = jnp.concatenate([dy, jnp.where(i < nt - 1, dyn_ref[...], 0.0)], 0)
        du0 = _conv_taps(yy, cw_ref, HALO_CONV, TM, True)
        zag = zag_ref[...].astype(F32)
        a, sg = zag[:, :D_CONV], jax.nn.sigmoid(zag[:, D_CONV:])
        dzag_ref[...] = jnp.concatenate([du0 * sg, du0 * a * sg * (1.0 - sg)], 1).astype(BF16)
        xx = jnp.concatenate([jnp.where(i > 0, xp_ref[...], 0.0), x_ref[...]], 0)

        @pl.when(i == 0)
        def _():
            dcw_ref[...] = jnp.zeros_like(dcw_ref)
            dcb_ref[...] = jnp.zeros_like(dcb_ref)

        rolls = {}
        for k in range(CONV_WIDTH):
            xs = _shifted_rows(xx, CONV_WIDTH - 1 - k, HALO_CONV, TM, False, rolls)
            dcw_ref[k:k + 1, :] += jnp.sum(dy * xs, 0, keepdims=True)
        dcb_ref[...] += jnp.sum(dy, 0, keepdims=True)

    return pl.pallas_call(
        body, name="bwd_conv", grid=(nt,),
        in_specs=[_rows(TM, D_CONV), _next(HALO_CONV, D_CONV, TM, L // HALO_CONV), _rows(TM, D_CONV),
                  _prev(HALO_CONV, D_CONV, TM), _full(cw.shape), _rows(TM, D_AG)],
        out_specs=[_rows(TM, D_AG), _full(cw.shape), _full((1, D_CONV))],
        out_shape=[_sds((L, D_AG), BF16), _sds(cw.shape, F32), _sds((1, D_CONV), F32)],
        compiler_params=_params(1))(du1, du1, u0, u0, cw, zag)


def _bwd_in(dzag, dq, dkv, dkr, cq, ckv, gq, gkv, wuq, wukv, win, rc, rs1, rs2, h0, gmix, dh1):
    L = h0.shape[0]

    def body(dzag_ref, dq_ref, dkv_ref, dkr_ref, cq_ref, ckv_ref, gq_ref, gkv_ref, wuq_ref, wukv_ref, win_ref,
             c_ref, s1_ref, s2_ref, h0_ref, gmix_ref, dh1_ref,
             dz_ref, dqr_ref, gx_ref, dfirst_ref, dgq_ref, dgkv_ref, dgmix_ref):
        i = pl.program_id(0)
        c, s1, s2 = c_ref[...], s1_ref[...], s2_ref[...]
        dqr = _rope_bwd(dq_ref[...].astype(F32), jnp.tile(c, (1, N_HEADS)), jnp.tile(s1, (1, N_HEADS)),
                        jnp.tile(s2, (1, N_HEADS))).astype(BF16)
        dqr_ref[...] = dqr
        cq, ckv = cq_ref[...], ckv_ref[...]
        dcq, dgq = _rms_bwd(_dot(dqr, wuq_ref[...], NT), cq, _rms_r(cq, Q_LORA), gq_ref[...], Q_LORA)
        dckv, dgkv = _rms_bwd(_dot(dkv_ref[...], wukv_ref[...], NT), ckv, _rms_r(ckv, KV_LORA), gkv_ref[...], KV_LORA)
        dkrp = _rope_bwd(dkr_ref[...], c, s1, s2)
        dz = jnp.concatenate([dzag_ref[...], dcq.astype(BF16), dckv.astype(BF16), dkrp.astype(BF16)], 1)
        dz_ref[...] = dz
        h0 = h0_ref[...]
        dnx, dgmix = _rms_bwd(_dot(dz, win_ref[...], NT), h0, _rms_r(h0, D_MODEL), gmix_ref[...], D_MODEL)
        dh0 = dh1_ref[...] + dnx

        @pl.when(i == 0)
        def _():
            dfirst_ref[...] = dh0
            for ref in (dgq_ref, dgkv_ref, dgmix_ref):
                ref[...] = jnp.zeros_like(ref)

        @pl.when(i > 0)
        def _():
            gx_ref[...] = dh0

        dgq_ref[...] += dgq
        dgkv_ref[...] += dgkv
        dgmix_ref[...] += dgmix

    return pl.pallas_call(
        body, name="bwd_in", grid=(L // TM,),
        in_specs=[_rows(TM, D_AG), _rows(TM, D_HEADS), _rows(TM, D_HEADS), _rows(TM, HB), _rows(TM, Q_LORA),
                  _rows(TM, KV_LORA), _full(gq.shape), _full(gkv.shape), _full(wuq.shape), _full(wukv.shape),
                  _full(win.shape), _rows(TM, HB), _rows(TM, HB), _rows(TM, HB), _rows(TM, D_MODEL),
                  _full(gmix.shape), _rows(TM, D_MODEL)],
        out_specs=[_rows(TM, D_ZP), _rows(TM, D_HEADS),
                   pl.BlockSpec((TM, D_MODEL), lambda i: (jnp.maximum(i - 1, 0), 0)), _full((TM, D_MODEL)),
                   _full((1, Q_LORA)), _full((1, KV_LORA)), _full((1, D_MODEL))],
        out_shape=[_sds((L, D_ZP), BF16), _sds((L, D_HEADS), BF16), _sds((L - TM, D_MODEL), F32),
                   _sds((TM, D_MODEL), F32), _sds((1, Q_LORA), F32), _sds((1, KV_LORA), F32), _sds((1, D_MODEL), F32)],
        compiler_params=_params(1))(dzag, dq, dkv, dkr, cq, ckv, gq, gkv, wuq, wukv, win, rc, rs1, rs2, h0, gmix, dh1)


def _mesh_pos():
    return lax.axis_index("x"), lax.axis_index("y"), lax.axis_index("c")


def _remote_copy(src, dst, send_sem, recv_sem, to):
    return functools.partial(pltpu.make_async_remote_copy, src, dst, send_sem, recv_sem, device_id=to,
                             device_id_type=MESH)


def _all_gather(shards):
    n = len(shards)

    def body(*refs):
        sends, arrivals, forwards, finals = _gather_copies(refs[:n], refs[n:2 * n], *refs[2 * n:])
        for cp in sends:
            cp().start()
        for landed, onward in zip(arrivals, forwards):
            landed().wait_recv()
            onward().start()
        for cp in finals:
            cp().wait_recv()
        for cp in sends + forwards:
            cp().wait_send()

    any_spec = pl.BlockSpec(memory_space=pl.ANY)
    outs = pl.pallas_call(
        body, name="all_gather_weights", in_specs=[any_spec] * n, out_specs=[any_spec] * n,
        out_shape=_gather_out_shapes(shards), scratch_shapes=_gather_semaphores(n))(*shards)
    return _gathered(outs, shards)


def _gather_out_shapes(shards):
    return [_sds((2 * N_CHIPS, s.shape[0] // 2) + s.shape[1:], s.dtype) for s in shards]


def _gather_semaphores(n):
    return [pltpu.SemaphoreType.DMA((n, 8)), pltpu.SemaphoreType.DMA((n, 8))]


def _gathered(outs, shards):
    return [o.reshape((N_CHIPS, s.shape[0]) + s.shape[1:]) for o, s in zip(outs, shards)]


def _gather_copies(ins, outs, send_sems, recv_sems):
    x, y, c = _mesh_pos()
    chips = [(1 - x, y), (x, 1 - y), (1 - x, 1 - y)]
    sends, arrivals, forwards, finals = [], [], [], []

    def copy(src, dst, a, k, to):
        return _remote_copy(src, dst, send_sems.at[a, k], recv_sems.at[a, k], to)

    for a, (src, out) in enumerate(zip(ins, outs)):
        m = out.shape[1]
        mine = src.at[pl.ds(pl.multiple_of(c * m, 16), m)]
        for hf in range(2):
            own = out.at[4 * x + 2 * y + hf]
            sends.append(copy(src.at[pl.ds(hf * m, m)], own, a, 6 + hf, (x, y, 1 - c)))
            finals.append(copy(own, own, a, 6 + hf, (x, y, 1 - c)))
        for k, chip in enumerate(chips):
            slot = 4 * chip[0] + 2 * chip[1]
            sends.append(copy(mine, out.at[4 * x + 2 * y + c], a, k, (*chip, c)))
            arrivals.append(copy(out.at[slot + c], out.at[slot + c], a, k, (*chip, c)))
            forwards.append(copy(out.at[slot + c], out.at[slot + c], a, 3 + k, (x, y, 1 - c)))
            finals.append(copy(out.at[slot + 1 - c], out.at[slot + 1 - c], a, 3 + k, (x, y, 1 - c)))
    return sends, arrivals, forwards, finals


def _sibling_exchange(parts, name):
    n = len(parts)

    def body(*refs):
        ins, theirs = refs[:n], refs[n:2 * n]
        send_sems, recv_sems = refs[2 * n:]
        x, y, c = _mesh_pos()
        copies = []
        for a in range(n):
            h = parts[a].shape[1] // 2
            rows = pl.ds(pl.multiple_of((1 - c) * h, 16), h)
            copies += [_remote_copy(ins[a].at[q, rows], theirs[a].at[q], send_sems.at[a, q], recv_sems.at[a, q],
                                    (x, y, 1 - c)) for q in range(N_CHIPS)]
        for cp in copies:
            cp().start()
        for cp in copies:
            cp().wait()

    any_spec = pl.BlockSpec(memory_space=pl.ANY)
    return pl.pallas_call(
        body, name=name, in_specs=[any_spec] * n, out_specs=[any_spec] * n,
        out_shape=[_sds((N_CHIPS, p.shape[1] // 2, p.shape[2]), p.dtype) for p in parts],
        scratch_shapes=[pltpu.SemaphoreType.DMA((n, N_CHIPS)), pltpu.SemaphoreType.DMA((n, N_CHIPS))])(*parts)


def _chip_exchange(parts, name):
    n = len(parts)

    def body(*refs):
        sends, arrivals = _chip_copies(refs[:n], refs[n:2 * n], *refs[2 * n:])
        for cp in sends:
            cp().start()
        for cp in arrivals:
            cp().wait_recv()
        for cp in sends:
            cp().wait_send()

    any_spec = pl.BlockSpec(memory_space=pl.ANY)
    return pl.pallas_call(
        body, name=name, in_specs=[any_spec] * n, out_specs=[any_spec] * n,
        out_shape=[_sds(p.shape, p.dtype) for p in parts], scratch_shapes=_chip_semaphores(n))(*parts)


def _chip_semaphores(n):
    return [pltpu.SemaphoreType.DMA((n, 3)), pltpu.SemaphoreType.DMA((n, 3))]


def _chip_copies(ins, outs, send_sems, recv_sems):
    x, y, c = _mesh_pos()
    me = 2 * x + y
    sends, arrivals = [], []
    for a, (src, out) in enumerate(zip(ins, outs)):
        for k, chip in enumerate([(1 - x, y), (x, 1 - y), (1 - x, 1 - y)]):
            slot = 2 * chip[0] + chip[1]
            sems = (send_sems.at[a, k], recv_sems.at[a, k], (*chip, c))
            sends.append(_remote_copy(src.at[slot], out.at[me], *sems))
            arrivals.append(_remote_copy(out.at[slot], out.at[slot], *sems))
    return sends, arrivals


def _sibling_gather(parts, name):
    n = len(parts)

    def body(*refs):
        ins, outs = refs[:n], refs[n:2 * n]
        send_sems, recv_sems = refs[2 * n:]
        x, y, c = _mesh_pos()
        copies = [_remote_copy(ins[a].at[c], outs[a].at[c], send_sems.at[a], recv_sems.at[a], (x, y, 1 - c))
                  for a in range(n)]
        for cp in copies:
            cp().start()
        for cp in copies:
            cp().wait()

    any_spec = pl.BlockSpec(memory_space=pl.ANY)
    return pl.pallas_call(
        body, name=name, in_specs=[any_spec] * n, out_specs=[any_spec] * n,
        out_shape=[_sds(p.shape, p.dtype) for p in parts], input_output_aliases={a: a for a in range(n)},
        scratch_shapes=[pltpu.SemaphoreType.DMA((n,)), pltpu.SemaphoreType.DMA((n,))])(*parts)


def _row_tile(rows, row_bytes, align, budget=1 << 20):
    best = None
    for t in range(align, rows + 1, align):
        if rows % t == 0 and t * row_bytes <= budget:
            best = t
    return best or rows


def _scalar(v):
    return jnp.reshape(v, (1,)).astype(jnp.int32)


def _add_pair(part, theirs, c, name):
    _, h, cols = theirs.shape
    tr = _row_tile(h, cols * 4, 16)
    nb = h // tr

    def body(c_ref, a_ref, b_ref, o_ref):
        o_ref[...] = (a_ref[...].astype(F32) + b_ref[...].astype(F32)).astype(o_ref.dtype)

    half = pl.BlockSpec((None, tr, cols), lambda q, i, c_ref: (q, i, 0))
    grid_spec = pltpu.PrefetchScalarGridSpec(
        num_scalar_prefetch=1, grid=(N_CHIPS, nb),
        in_specs=[pl.BlockSpec((None, tr, cols), lambda q, i, c_ref: (q, c_ref[0] * nb + i, 0)), half],
        out_specs=half)
    return pl.pallas_call(body, name=name, grid_spec=grid_spec, out_shape=_sds(theirs.shape, part.dtype),
                          compiler_params=_params(2))(_scalar(c), part, theirs)


def _add_chips(got, own, me, c, name):
    _, h, cols = got.shape
    tr = _row_tile(h, cols * 4 * N_CHIPS, 16, budget=1 << 21)

    def body(pos_ref, got_ref, own_ref, o_ref):
        acc = None
        for q in range(N_CHIPS):
            term = jnp.where(pos_ref[0] == q, own_ref[q], got_ref[q]).astype(F32)
            acc = term if acc is None else acc + term
        o_ref[...] = acc

    by_chip = pl.BlockSpec((N_CHIPS, tr, cols), lambda i, pos_ref: (0, i, 0))
    grid_spec = pltpu.PrefetchScalarGridSpec(
        num_scalar_prefetch=1, grid=(h // tr,), in_specs=[by_chip, by_chip],
        out_specs=pl.BlockSpec((None, tr, cols), lambda i, pos_ref: (pos_ref[1], i, 0)))
    return pl.pallas_call(body, name=name, grid_spec=grid_spec, out_shape=_sds((2, h, cols), F32),
                          compiler_params=_params(1))(jnp.stack([me, c]).astype(jnp.int32), got, own)


def _reduce_begin(parts, tag):
    c = lax.axis_index("c")
    theirs = _sibling_exchange(parts, f"grad_sibling_exchange_{tag}")
    return [_add_pair(p, t, c, f"grad_add_pair_{tag}_{a}") for a, (p, t) in enumerate(zip(parts, theirs))]


def _reduce_end(pair, got, tag):
    x, y, c = _mesh_pos()
    half = [_add_chips(g, p, 2 * x + y, c, f"grad_add_chips_{tag}_{a}") for a, (g, p) in enumerate(zip(got, pair))]
    both = _sibling_gather(half, f"grad_sibling_gather_{tag}")
    return [b.reshape(-1, b.shape[-1]) for b in both]


def _adamw_math(w, g, m, v):
    m = ADAM_B1 * m + (1.0 - ADAM_B1) * g
    v = ADAM_B2 * v + (1.0 - ADAM_B2) * (g * g)
    m_hat = m / (1.0 - ADAM_B1 ** ADAM_STEP)
    v_hat = v / (1.0 - ADAM_B2 ** ADAM_STEP)
    return -ADAM_LR * (m_hat / (jnp.sqrt(v_hat) + ADAM_EPS) + ADAM_WD * w), m, v


def _adamw_big(w, g, m, v, name, parts=()):
    r, c = w.shape
    tr = _row_tile(r, c * 4, 8, budget=1 << 19)
    if r // tr > ADAMW_MAX_STEPS:
        tr = r
    n = len(parts)
    steps = r // tr

    def body(w_ref, g_ref, m_ref, v_ref, *refs):
        go_ref, d_ref, mo_ref, vo_ref = refs[n:n + 4]
        exchange_refs = (refs[:n], refs[n + 4:2 * n + 4]) + refs[2 * n + 4:]
        if n:
            @pl.when(pl.program_id(0) == 0)
            def _():
                for cp in _chip_copies(*exchange_refs)[0]:
                    cp().start()

        g = g_ref[...]
        go_ref[...] = g
        d_ref[...], mo_ref[...], vo_ref[...] = _adamw_math(w_ref[...], g, m_ref[...], v_ref[...])
        if n:
            @pl.when(pl.program_id(0) == steps - 1)
            def _():
                sends, arrivals = _chip_copies(*exchange_refs)
                for cp in arrivals:
                    cp().wait_recv()
                for cp in sends:
                    cp().wait_send()

    any_spec = pl.BlockSpec(memory_space=pl.ANY)
    outs = pl.pallas_call(
        body, name=name, grid=(steps,), in_specs=[_rows(tr, c)] * 4 + [any_spec] * n,
        out_specs=[_rows(tr, c)] * 4 + [any_spec] * n,
        out_shape=[_sds((r, c), F32)] * 4 + [_sds(p.shape, p.dtype) for p in parts],
        scratch_shapes=_chip_semaphores(n) if n else [], compiler_params=_params(1))(w, g, m, v, *parts)
    return outs[:4], list(outs[4:])


def _adamw_small(ws, gs, ms, vs):
    n = len(ws)

    def body(*refs):
        for a in range(n):
            w_ref, g_ref, m_ref, v_ref = (refs[k * n + a] for k in range(4))
            d, m, v = _adamw_math(w_ref[...], g_ref[...], m_ref[...], v_ref[...])
            refs[4 * n + a][...] = d
            refs[5 * n + a][...] = m
            refs[6 * n + a][...] = v

    vm = pl.BlockSpec(memory_space=pltpu.VMEM)
    outs = pl.pallas_call(
        body, name="adamw_small", in_specs=[vm] * (4 * n), out_specs=[vm] * (3 * n),
        out_shape=[_sds(w.shape, F32) for w in ws] * 3)(*ws, *gs, *ms, *vs)
    return outs[:n], outs[n:2 * n], outs[2 * n:]


BIG = ("w_in", "w_uq", "w_ukv", "w_out", "w_ffn_up", "w_ffn_down")
SMALL_SHARDED = ("conv_w", "ffn_conv_w", "meta_tokens")
REPLICATED = ("mix_norm_g", "q_norm_g", "kv_norm_g", "conv_b", "conv_ln_g", "conv_ln_b", "conv_out_g", "attn_out_g",
              "ffn_norm_g", "ffn_conv_b", "final_norm_g")
WEIGHTS = ("meta_tokens", "mix_norm_g", "w_in", "q_norm_g", "w_uq", "kv_norm_g", "w_ukv", "conv_w", "conv_b",
           "conv_ln_g", "conv_ln_b", "conv_out_g", "attn_out_g", "w_out", "ffn_norm_g", "w_ffn_up", "ffn_conv_w",
           "ffn_conv_b", "w_ffn_down", "final_norm_g")


def _lane_rows(a):
    return a.reshape(N_CHIPS, -1, LANES)


def _col_shards(a):
    k = a.shape[0]
    return a.reshape(k, N_CHIPS, -1).transpose(1, 0, 2)


def _from_col_shards(a):
    return a.transpose(1, 0, 2).reshape(a.shape[1], -1)


def _pad_rows_to(a, rows):
    return jnp.pad(a, ((0, 0), (0, rows - a.shape[1]), (0, 0)))


def _rope_tables(L):
    pos = (jnp.arange(L, dtype=jnp.int32) - DEAD).astype(F32)
    inv_freq = 1.0 / (ROPE_THETA ** (jnp.arange(0, QK_ROPE, 2, dtype=F32) / QK_ROPE))
    ang = pos[:, None] * inv_freq[None, :]
    cos, sin = jnp.cos(ang), jnp.sin(ang)
    half = QK_ROPE // 2
    z = lambda n: jnp.zeros((L, n), F32)
    rc = jnp.concatenate([jnp.ones((L, QK_NOPE), F32), cos, cos, z(HB - QK_NOPE - QK_ROPE)], 1)
    rs1 = jnp.concatenate([z(QK_NOPE), -sin, z(HB - QK_NOPE - half)], 1)
    rs2 = jnp.concatenate([z(QK_NOPE + half), sin, z(HB - QK_NOPE - QK_ROPE)], 1)
    return rc, rs1, rs2


def _pad_heads(g):
    return jnp.pad(g.reshape(N_HEADS, V_HEAD), ((0, 0), (HB - V_HEAD, 0))).reshape(1, D_HEADS)


def _unpad_heads(g):
    return g.reshape(N_HEADS, HB)[:, HB - V_HEAD:].reshape(1, N_HEADS * V_HEAD)


def _local_step(x, target, w, late_shards=None, reduce_first=False):
    S = x.shape[0]
    L = TM + S
    tl = L // 4
    d_qk = QK_NOPE + QK_ROPE
    win_n = w["w_in"]
    kr0 = D_AG + Q_LORA + KV_LORA
    win = jnp.concatenate([win_n[:, :kr0], jnp.zeros((D_MODEL, QK_NOPE), BF16), win_n[:, kr0:],
                           jnp.zeros((D_MODEL, HB - d_qk), BF16)], 1)
    wuq = jnp.pad(w["w_uq"].reshape(Q_LORA, N_HEADS, d_qk), ((0, 0), (0, 0), (0, HB - d_qk))).reshape(Q_LORA, D_HEADS)
    wukv = w["w_ukv"]
    ga = _pad_heads(w["attn_out_g"])
    gfin = w["final_norm_g"].reshape(1, D_MODEL)
    rc, rs1, rs2 = _rope_tables(L)
    head = jnp.concatenate([jnp.zeros((DEAD, D_MODEL), F32), w["meta_tokens"]], 0)

    h0, n, zag, u0, cq, ckv, qn, kvn, q, kv, kr, kvt = _fwd_in(x, head, w["mix_norm_g"], win, w["q_norm_g"], wuq,
                                                                w["kv_norm_g"], wukv, rc, rs1, rs2)
    u1, mixa = _fwd_conv(u0, w["conv_w"], w["conv_b"], w["conv_ln_g"], w["conv_ln_b"], w["conv_out_g"])
    if late_shards is None:
        o, lse, _ = _attn_fwd(q, kv, kvt, kr)
        wout_n, wup, wdown = w["w_out"], _col_shards(w["w_ffn_up"]), w["w_ffn_down"]
    else:
        o, lse, late = _attn_fwd(q, kv, kvt, kr, [late_shards[k] for k in LATE])
        wout_n, wup, wdown = _full_weight("w_out", late[0]), late[1], _full_weight("w_ffn_down", late[2])
    wout = jnp.concatenate([wout_n[:D_CONV], jnp.pad(wout_n[D_CONV:].reshape(N_HEADS, V_HEAD, D_MODEL),
                                                     ((0, 0), (HB - V_HEAD, 0), (0, 0))).reshape(D_HEADS, D_MODEL)], 0)
    mix, h1, n2 = _fwd_mix(h0, mixa, o, ga, wout, w["ffn_norm_g"])
    up0 = _mm(n2, wup, NN, BF16, tl, FF_TILE, D_MODEL, "ffn_up", b_slabs=True)
    dh2, loss, g_fin, up, act = _fwd_ffn_loss(up0, w["ffn_conv_w"], w["ffn_conv_b"], wdown, h1, target, gfin)

    dup, g_fcb = _bwd_ffn_act(dh2, up, wdown)
    dn2, dup0, g_fcw_tiles = _bwd_ffn_conv_up(dup, up0, w["ffn_conv_w"], wup, tl)
    g_fcw = jnp.sum(g_fcw_tiles, 0)
    g_wup = _mm(n2, dup0, TN, BF16, D_MODEL, FF_TILE, tl, "ffn_up_dw", by_col_tile=True)
    g_wdown = _mm(act, dh2, TN, BF16, D_FF // 2, D_MODEL, tl, "ffn_down_dw").reshape(N_CHIPS, -1, D_MODEL)
    dh1, dh1m, do, delta, du1, g_gffn, g_ga, g_og, g_lg, g_lb = _bwd_mix(
        dh2, dn2, h1, w["ffn_norm_g"], wout, o, ga, u1, w["conv_ln_g"], w["conv_ln_b"], w["conv_out_g"])
    g_wout = _mm(mix, dh1m, TN, BF16, D_MIX // 2, D_MODEL, tl, "out_dw")
    g_wout = jnp.concatenate([g_wout[:D_CONV], g_wout[D_CONV:].reshape(N_HEADS, HB, D_MODEL)[:, HB - V_HEAD:]
                              .reshape(N_HEADS * V_HEAD, D_MODEL)], 0).reshape(N_CHIPS, -1, D_MODEL)
    pair = _reduce_begin([g_wup, g_wdown, g_wout], "first") if reduce_first else ()
    dq, dkv, dkr, got = _attn_bwd(q, kv, kr, do, lse, delta, pair)
    if reduce_first:
        g_wup, g_wdown, g_wout = _reduce_end(pair, got, "first")
    dzag, g_cw, g_cb = _bwd_conv(du1, u0, w["conv_w"], zag)
    dz, dqr, gx, dfirst, g_gq, g_gkv, g_gmix = _bwd_in(dzag, dq, dkv, dkr, cq, ckv, w["q_norm_g"], w["kv_norm_g"],
                                                      wuq, wukv, win, rc, rs1, rs2, h0, w["mix_norm_g"], dh1)
    g_win = _mm(n, dz, TN, BF16, D_MODEL, D_ZP // 2, tl, "in_dw")
    g_wuq = _mm(qn, dqr, TN, BF16, Q_LORA, D_HEADS, tl, "uq_dw")
    g_wukv = _mm(kvn, dkv, TN, BF16, KV_LORA, D_HEADS // N_CHIPS, tl, "ukv_dw", by_col_tile=True)

    grads = {
        "w_in": _col_shards(jnp.concatenate([g_win[:, :kr0], g_win[:, kr0 + QK_NOPE:kr0 + d_qk]], 1)),
        "w_uq": _col_shards(g_wuq.reshape(Q_LORA, N_HEADS, HB)[:, :, :d_qk].reshape(Q_LORA, N_HEADS * d_qk)),
        "w_ukv": g_wukv,
        "w_out": g_wout,
        "w_ffn_up": g_wup, "w_ffn_down": g_wdown, "conv_w": g_cw, "ffn_conv_w": g_fcw,
        "meta_tokens": dfirst[DEAD:], "mix_norm_g": g_gmix, "q_norm_g": g_gq, "kv_norm_g": g_gkv, "conv_b": g_cb,
        "conv_ln_g": g_lg, "conv_ln_b": g_lb, "conv_out_g": g_og, "attn_out_g": _unpad_heads(g_ga),
        "ffn_norm_g": g_gffn, "ffn_conv_b": g_fcb, "final_norm_g": g_fin,
    }
    return loss, gx, grads


ROW_SHARDED = ("w_out", "w_ffn_down")


TRANSPOSED = ("w_in", "w_uq")
REDUCED_FIRST = ("w_ffn_up", "w_ffn_down", "w_out")
LATE = ("w_out", "w_ffn_up", "w_ffn_down")


def _full_weight(name, by_chip):
    return by_chip.reshape(-1, by_chip.shape[-1]) if name in ROW_SHARDED else _from_col_shards(by_chip)


def kernel(x, meta_tokens, mix_norm_g, w_in, q_norm_g, w_uq, kv_norm_g, w_ukv, conv_w, conv_b, conv_ln_g, conv_ln_b, conv_out_g, attn_out_g, w_out, ffn_norm_g, w_ffn_up, ffn_conv_w, ffn_conv_b, w_ffn_down, final_norm_g, loss_target, m_meta_tokens, m_mix_norm_g, m_w_in, m_q_norm_g, m_w_uq, m_kv_norm_g, m_w_ukv, m_conv_w, m_conv_b, m_conv_ln_g, m_conv_ln_b, m_conv_out_g, m_attn_out_g, m_w_out, m_ffn_norm_g, m_w_ffn_up, m_ffn_conv_w, m_ffn_conv_b, m_w_ffn_down, m_final_norm_g, v_meta_tokens, v_mix_norm_g, v_w_in, v_q_norm_g, v_w_uq, v_kv_norm_g, v_w_ukv, v_conv_w, v_conv_b, v_conv_ln_g, v_conv_ln_b, v_conv_out_g, v_attn_out_g, v_w_out, v_ffn_norm_g, v_w_ffn_up, v_ffn_conv_w, v_ffn_conv_b, v_w_ffn_down, v_final_norm_g):
    local = dict(meta_tokens=meta_tokens, mix_norm_g=mix_norm_g, w_in=w_in[0], q_norm_g=q_norm_g, w_uq=w_uq[0],
                 kv_norm_g=kv_norm_g, w_ukv=w_ukv[0], conv_w=conv_w[0], conv_b=conv_b, conv_ln_g=conv_ln_g,
                 conv_ln_b=conv_ln_b, conv_out_g=conv_out_g, attn_out_g=attn_out_g, w_out=w_out[0],
                 ffn_norm_g=ffn_norm_g, w_ffn_up=w_ffn_up[0], ffn_conv_w=ffn_conv_w[0], ffn_conv_b=ffn_conv_b,
                 w_ffn_down=w_ffn_down[0], final_norm_g=final_norm_g.reshape(1, D_MODEL))
    ms = dict(zip(WEIGHTS, (m_meta_tokens, m_mix_norm_g, m_w_in, m_q_norm_g, m_w_uq, m_kv_norm_g, m_w_ukv, m_conv_w,
                            m_conv_b, m_conv_ln_g, m_conv_ln_b, m_conv_out_g, m_attn_out_g, m_w_out, m_ffn_norm_g,
                            m_w_ffn_up, m_ffn_conv_w, m_ffn_conv_b, m_w_ffn_down, m_final_norm_g)))
    vs = dict(zip(WEIGHTS, (v_meta_tokens, v_mix_norm_g, v_w_in, v_q_norm_g, v_w_uq, v_kv_norm_g, v_w_ukv, v_conv_w,
                            v_conv_b, v_conv_ln_g, v_conv_ln_b, v_conv_out_g, v_attn_out_g, v_w_out, v_ffn_norm_g,
                            v_w_ffn_up, v_ffn_conv_w, v_ffn_conv_b, v_w_ffn_down, v_final_norm_g)))

    small_flat = jnp.concatenate([local[k].reshape(-1) for k in SMALL_SHARDED]).reshape(-1, LANES)
    early = [k for k in BIG if k not in LATE]
    gathered = _all_gather([local[k].astype(BF16) for k in early] + [small_flat])
    full = {k: v for k, v in local.items() if k not in LATE}
    for name, g in zip(early, gathered[:len(early)]):
        full[name] = _full_weight(name, g)
    small = gathered[-1].reshape(N_CHIPS, -1)
    at = 0
    for name in SMALL_SHARDED:
        r, c = local[name].shape
        full[name] = _from_col_shards(small[:, at:at + r * c].reshape(N_CHIPS, r, c))
        at += r * c

    loss_row, grad_x, grads = _local_step(x[0], loss_target[0], full, {k: local[k].astype(BF16) for k in LATE},
                                          reduce_first=True)

    rest_big = [k for k in BIG if k not in REDUCED_FIRST]
    rep = jnp.concatenate([grads[k].reshape(-1) for k in REPLICATED] + [loss_row.reshape(-1)]).reshape(1, -1, LANES)
    small_pieces = [_lane_rows(_col_shards(grads[k])) for k in SMALL_SHARDED]
    small_pieces.append(jnp.broadcast_to(rep, (N_CHIPS,) + rep.shape[1:]))
    small_rows = sum(p.shape[1] for p in small_pieces)
    small_pack = _pad_rows_to(jnp.concatenate(small_pieces, 1), -(-small_rows // 32) * 32)
    pair = _reduce_begin([grads[k] for k in rest_big] + [small_pack], "rest")

    total = {k: grads[k] for k in REDUCED_FIRST}
    delta, new_m, new_v = {}, {}, {}
    shape2 = lambda a, name: a.reshape(local[name].shape)
    turn = lambda a, name: a.T if name in TRANSPOSED else a

    def update(name, parts=()):
        outs, got = _adamw_big(turn(local[name], name), turn(total[name], name), turn(shape2(ms[name], name), name),
                               turn(shape2(vs[name], name), name), "adamw_" + name, parts)
        total[name], delta[name], new_m[name], new_v[name] = (turn(o, name) for o in outs)
        return got

    *rest_tot, small_tot = _reduce_end(pair, update(REDUCED_FIRST[0], pair), "rest")
    total.update(zip(rest_big, rest_tot))
    flat = small_tot.reshape(-1)
    at = 0
    for name in SMALL_SHARDED + REPLICATED:
        shape = local[name].shape
        size = shape[0] * shape[1]
        total[name] = flat[at:at + size].reshape(shape)
        at += -(-size // LANES) * LANES if name in SMALL_SHARDED else size
    loss = flat[at]

    for name in BIG:
        if name != REDUCED_FIRST[0]:
            update(name)
    rest = SMALL_SHARDED + REPLICATED
    ds, nms, nvs = _adamw_small([local[k] for k in rest], [total[k] for k in rest],
                                [shape2(ms[k], k) for k in rest], [shape2(vs[k], k) for k in rest])
    for k, d, nm, nv in zip(rest, ds, nms, nvs):
        delta[k], new_m[k], new_v[k] = d, nm, nv

    out_shape = dict(zip(WEIGHTS, (meta_tokens, mix_norm_g, w_in, q_norm_g, w_uq, kv_norm_g, w_ukv, conv_w, conv_b,
                                   conv_ln_g, conv_ln_b, conv_out_g, attn_out_g, w_out, ffn_norm_g, w_ffn_up,
                                   ffn_conv_w, ffn_conv_b, w_ffn_down, final_norm_g)))
    outs = [loss, grad_x[None]]
    for group in (total, delta, new_m, new_v):
        outs += [group[k].reshape(out_shape[k].shape) for k in WEIGHTS]
    return tuple(outs)
```

```python
import functools

import jax
import jax.numpy as jnp
from jax import lax
from jax.experimental import pallas as pl
from jax.experimental.pallas import tpu as pltpu

F32 = jnp.float32
BF16 = jnp.bfloat16

D_MODEL = 1024
D_CONV = 512
CONV_WIDTH = 31
N_HEADS = 8
QK_NOPE = 64
QK_ROPE = 32
V_HEAD = 64
Q_LORA = 384
KV_LORA = 256
D_FF = 2816
FFN_CONV_WIDTH = 3
CHUNK_SHIFT = 6
N_META = 16
ROPE_THETA = 10000.0
EPS = 1e-6
NEG = -1e30
ADAM_LR = 0.001
ADAM_B1 = 0.9
ADAM_B2 = 0.999
ADAM_EPS = 1e-08
ADAM_WD = 0.01
ADAM_STEP = 10

LANES = 128
SUBLANES = 8
HB = LANES
D_HEADS = N_HEADS * HB
TM = 256
DEAD = TM - N_META
D_AG = 2 * D_CONV
D_ZP = D_AG + Q_LORA + KV_LORA + HB
D_MIX = D_CONV + D_HEADS
LN2 = 0.6931471805599453
Q_SCALE = (QK_NOPE + QK_ROPE) ** -0.5 / LN2
HALO_CONV = 32
HALO_FFN = 16
FF_CHUNK = 256
FF_MXU_CHUNK = 256
FF_TILE = D_FF // 2
VMEM_LIMIT = 56 * 1024 * 1024
ADAMW_MAX_STEPS = 32
N_CHIPS = 4
HEAD_GROUP = 4
N_GROUPS = N_HEADS // HEAD_GROUP
MESH =pl.DeviceIdType.MESH


def _params(n_grid):
    return pltpu.CompilerParams(dimension_semantics=("arbitrary",) * n_grid, vmem_limit_bytes=VMEM_LIMIT)


def _rows(tm, c, off=0):
    return pl.BlockSpec((tm, c), lambda i: (i, off))


def _full(shape):
    return pl.BlockSpec(shape, lambda i: (0,) * len(shape))


def _prev(hb, c, tm, off=0):
    return pl.BlockSpec((hb, c), lambda i: (jnp.maximum(i * (tm // hb) - 1, 0), off))


def _next(hb, c, tm, nblk, off=0):
    return pl.BlockSpec((hb, c), lambda i: (jnp.minimum((i + 1) * (tm // hb), nblk - 1), off))


def _sds(shape, dtype):
    return jax.ShapeDtypeStruct(shape, dtype)


def _rms_r(x, n):
    return lax.rsqrt(jnp.sum(x * x, -1, keepdims=True) * (1.0 / n) + EPS)


def _rms_bwd(dy, x, r, g, n):
    gd = dy * g
    dx = r * gd - x * (r * r * r) * (jnp.sum(x * gd, -1, keepdims=True) * (1.0 / n))
    return dx, jnp.sum(dy * x * r, 0, keepdims=True)


def _dot(a, b, dims):
    return lax.dot_general(a, b, (dims, ((), ())), preferred_element_type=F32)


NN = ((1,), (0,))
NT = ((1,), (1,))
TN = ((0,), (0,))


def _rope(x, c, s1, s2):
    n = x.shape[-1]
    return x * c + pltpu.roll(x, n - QK_ROPE // 2, 1) * s1 + pltpu.roll(x, QK_ROPE // 2, 1) * s2


def _rope_bwd(g, c, s1, s2):
    n = g.shape[-1]
    return g * c + pltpu.roll(g * s1, QK_ROPE // 2, 1) + pltpu.roll(g * s2, n - QK_ROPE // 2, 1)


def _row_ids(tm, cols=1):
    return pl.program_id(0) * tm + lax.broadcasted_iota(jnp.int32, (tm, cols), 0)


def _mm(a, b, dims, out_dtype, tm, tn, tk, name, by_col_tile=False, b_slabs=False):
    if dims == TN:
        (kk, m), (_, n) = a.shape, b.shape
        a_spec = pl.BlockSpec((tk, tm), lambda i, j, k: (k, i))
    else:
        m, kk = a.shape
        a_spec = pl.BlockSpec((tm, tk), lambda i, j, k: (i, k))
    if dims == NT and b_slabs:
        n = b.shape[1]
        assert b.shape[2] == tk and kk == b.shape[0] * tk, (name, b.shape)
        b_spec = pl.BlockSpec((None, tn, tk), lambda i, j, k: (k, j, 0))
    elif dims == NT:
        n = b.shape[0]
        b_spec = pl.BlockSpec((tn, tk), lambda i, j, k: (j, k))
    elif b_slabs:
        n = b.shape[0] * b.shape[2]
        assert b.shape[2] == tn and kk == b.shape[1], (name, b.shape)
        b_spec = pl.BlockSpec((None, tk, tn), lambda i, j, k: (j, k, 0))
    else:
        n = b.shape[1]
        b_spec = pl.BlockSpec((tk, tn), lambda i, j, k: (k, j))
    assert m % tm == 0 and n % tn == 0 and kk % tk == 0, (name, a.shape, b.shape, tm, tn, tk)
    nk = kk // tk

    def body(a_ref, b_ref, o_ref, acc_ref):
        k = pl.program_id(2)

        @pl.when(k == 0)
        def _():
            acc_ref[...] = jnp.zeros_like(acc_ref)

        acc_ref[...] += _dot(a_ref[...].astype(BF16), b_ref[...].astype(BF16), dims)

        @pl.when(k == nk - 1)
        def _():
            o_ref[...] = acc_ref[...].astype(out_dtype)

    if by_col_tile:
        out_spec, out_shape = pl.BlockSpec((None, tm, tn), lambda i, j, k: (j, i, 0)), (n // tn, m, tn)
    else:
        out_spec, out_shape = pl.BlockSpec((tm, tn), lambda i, j, k: (i, j)), (m, n)
    return pl.pallas_call(
        body, name=name, grid=(m // tm, n // tn, nk), in_specs=[a_spec, b_spec], out_specs=out_spec,
        out_shape=_sds(out_shape, out_dtype), scratch_shapes=[pltpu.VMEM((tm, tn), F32)],
        compiler_params=_params(3))(a, b)


def _fwd_in(x, head, gmix, win, gq, wuq, gkv, wukv, rc, rs1, rs2):
    L = TM + x.shape[0]

    def body(x_ref, head_ref, gmix_ref, win_ref, gq_ref, wuq_ref, gkv_ref, wukv_ref, c_ref, s1_ref, s2_ref,
             h0_ref, n_ref, zag_ref, u0_ref, cq_ref, ckv_ref, qn_ref, kvn_ref, q_ref, kv_ref, kr_ref, kvt_ref):
        h = jnp.where(pl.program_id(0) == 0, head_ref[...], x_ref[...])
        h0_ref[...] = h
        n = (h * _rms_r(h, D_MODEL) * gmix_ref[...]).astype(BF16)
        n_ref[...] = n
        z = _dot(n, win_ref[...], NN)
        a, gate = z[:, :D_CONV], z[:, D_CONV:D_AG]
        zag_ref[...] = z[:, :D_AG].astype(BF16)
        u0_ref[...] = a * jax.nn.sigmoid(gate)
        cq = z[:, D_AG:D_AG + Q_LORA]
        ckv = z[:, D_AG + Q_LORA:D_AG + Q_LORA + KV_LORA]
        krp = z[:, D_AG + Q_LORA + KV_LORA:]
        cq_ref[...] = cq
        ckv_ref[...] = ckv
        qn = (cq * _rms_r(cq, Q_LORA) * gq_ref[...]).astype(BF16)
        qn_ref[...] = qn
        kvn = (ckv * _rms_r(ckv, KV_LORA) * gkv_ref[...]).astype(BF16)
        kvn_ref[...] = kvn
        c, s1, s2 = c_ref[...], s1_ref[...], s2_ref[...]
        q = _dot(qn, wuq_ref[...], NN)
        q = _rope(q, jnp.tile(c, (1, N_HEADS)), jnp.tile(s1, (1, N_HEADS)), jnp.tile(s2, (1, N_HEADS)))
        q_ref[...] = (q * Q_SCALE).astype(BF16)
        kv = _dot(kvn, wukv_ref[...], NN)
        kv_ref[...] = kv.astype(BF16)
        kvt_ref[...] = kv.T.astype(BF16)
        kr_ref[...] = _rope(krp, c, s1, s2).astype(BF16)

    outs = [(D_MODEL, F32), (D_MODEL, BF16), (D_AG, BF16), (D_CONV, F32), (Q_LORA, F32), (KV_LORA, F32), (Q_LORA, BF16),
            (KV_LORA, BF16), (D_HEADS, BF16), (D_HEADS, BF16), (HB, BF16)]
    return pl.pallas_call(
        body, name="fwd_in", grid=(L // TM,),
        in_specs=[pl.BlockSpec((TM, D_MODEL), lambda i: (jnp.maximum(i - 1, 0), 0)), _full(head.shape),
                  _full(gmix.shape), _full(win.shape), _full(gq.shape), _full(wuq.shape),
                  _full(gkv.shape), _full(wukv.shape), _rows(TM, HB), _rows(TM, HB), _rows(TM, HB)],
        out_specs=[_rows(TM, c) for c, _ in outs] + [pl.BlockSpec((D_HEADS, TM), lambda i: (0, i))],
        out_shape=[_sds((L, c), d) for c, d in outs] + [_sds((D_HEADS, L), BF16)],
        compiler_params=_params(1))(x, head, gmix, win, gq, wuq, gkv, wukv, rc, rs1, rs2)


def _conv_taps(xx, w_ref, halo, tm, flip):
    kw = w_ref.shape[0]
    acc, rolls = None, {}
    for k in range(kw):
        term = w_ref[k:k + 1, :] * _shifted_rows(xx, kw - 1 - k, halo, tm, flip, rolls)
        acc = term if acc is None else acc + term
    return acc


def _shifted_rows(xx, d, halo, tm, flip, rolls):
    a, b = divmod(d, SUBLANES)
    if b not in rolls:
        rolls[b] = xx if b == 0 else pltpu.roll(xx, (xx.shape[0] - b) if flip else b, 0)
    start = SUBLANES * a if flip else halo - SUBLANES * a
    return rolls[b][start:start + tm]


def _ln_silu(u1, lg, lb):
    mu = jnp.mean(u1, -1, keepdims=True)
    xc = u1 - mu
    rs = lax.rsqrt(jnp.mean(xc * xc, -1, keepdims=True) + EPS)
    xh = xc * rs
    u2 = xh * lg + lb
    sg = jax.nn.sigmoid(u2)
    return rs, xh, u2, sg, u2 * sg


def _fwd_conv(u0, cw, cb, lg, lb, og):
    L = u0.shape[0]

    def body(u0_ref, u0p_ref, cw_ref, cb_ref, lg_ref, lb_ref, og_ref, u1_ref, mixa_ref):
        halo = jnp.where(pl.program_id(0) > 0, u0p_ref[...], 0.0)
        xx = jnp.concatenate([halo, u0_ref[...]], 0)
        u1 = _conv_taps(xx, cw_ref, HALO_CONV, TM, False) + cb_ref[...]
        u1_ref[...] = u1
        u = _ln_silu(u1, lg_ref[...], lb_ref[...])[4]
        mixa_ref[...] = (u * _rms_r(u, D_CONV) * og_ref[...]).astype(BF16)

    return pl.pallas_call(
        body, name="fwd_conv", grid=(L // TM,),
        in_specs=[_rows(TM, D_CONV), _prev(HALO_CONV, D_CONV, TM), _full(cw.shape), _full(cb.shape),
                  _full(lg.shape), _full(lb.shape), _full(og.shape)],
        out_specs=[_rows(TM, D_CONV), _rows(TM, D_CONV)],
        out_shape=[_sds((L, D_CONV), F32), _sds((L, D_CONV), BF16)],
        compiler_params=_params(1))(u0, u0, cw, cb, lg, lb, og)


def _visible(i, j, t):
    row = i * t + lax.broadcasted_iota(jnp.int32, (t, t), 0)
    col = j * t + lax.broadcasted_iota(jnp.int32, (t, t), 1)
    return (lax.shift_right_logical(col, CHUNK_SHIFT) <= lax.shift_right_logical(row, CHUNK_SHIFT)) & (col >= DEAD)


def _visible_t(i, j, t):
    key = j * t + lax.broadcasted_iota(jnp.int32, (t, t), 0)
    query = i * t + lax.broadcasted_iota(jnp.int32, (t, t), 1)
    return (lax.shift_right_logical(key, CHUNK_SHIFT) <= lax.shift_right_logical(query, CHUNK_SHIFT)) & (key >= DEAD)


def _stat_lane(h):
    return (h // HEAD_GROUP) * LANES + h % HEAD_GROUP


def _scatter_stats(cols, t):
    lane = lax.broadcasted_iota(jnp.int32, (t, N_GROUPS * LANES), 1)
    out = jnp.zeros((t, N_GROUPS * LANES), F32)
    for h, col in enumerate(cols):
        out = jnp.where(lane == _stat_lane(h), col, out)
    return out


def _resident(shape, index_map):
    return pl.BlockSpec(shape, index_map, pipeline_mode=pl.Buffered(1))


def _attn_fwd(q, kv, kvt, kr, shards=()):
    L = q.shape[0]
    t = TM
    nq = L // t
    n = len(shards)
    pass_step = (5 * nq) // 6

    def body(q_ref, kv_ref, kvt_ref, kr_ref, *refs):
        gather_refs = refs[:n] + refs[n + 2:2 * n + 2] + refs[2 * n + 6:]
        o_ref, lse_ref = refs[n:n + 2]
        qt_scr, m_scr, l_scr, acc_scr = refs[2 * n + 2:2 * n + 6]
        i = pl.program_id(0)
        if n:
            @pl.when(i == 0)
            def _():
                for cp in _gather_copies(gather_refs[:n], gather_refs[n:2 * n], *gather_refs[2 * n:])[0]:
                    cp().start()

        lane = lax.broadcasted_iota(jnp.int32, (t, HB), 1)
        heads = range(N_HEADS)
        cols = [slice(h * HB, (h + 1) * HB) for h in heads]
        for h in heads:
            qt_scr[cols[h], :] = q_ref[:, cols[h]].astype(F32).T.astype(BF16)
        m_scr[...] = jnp.full_like(m_scr, NEG)
        l_scr[...] = jnp.zeros_like(l_scr)
        acc_scr[...] = jnp.zeros_like(acc_scr)

        def tile(keys, vis, whole=True):
            krj = kr_ref[keys, :]
            kvj = [kv_ref[keys, cols[h]] for h in heads]
            lane_k = lane[:krj.shape[0]]
            s = [_dot(jnp.where(lane_k < QK_NOPE, kvj[h], krj), qt_scr[cols[h], :], NN) for h in heads]
            p, alpha = [], []
            for h in heads:
                sh = s[h] if vis is None else jnp.where(vis, s[h], NEG)
                m_prev = m_scr[h:h + 1, :]
                m_new = jnp.maximum(m_prev, jnp.max(sh, 0, keepdims=True))
                a = jnp.exp2(m_prev - m_new)
                ph = jnp.exp2(sh - m_new)
                l_scr[h:h + 1, :] = a * l_scr[h:h + 1, :] + jnp.sum(ph, 0, keepdims=True)
                m_scr[h:h + 1, :] = m_new
                p.append(ph.astype(BF16))
                alpha.append(a)
            for h in heads:
                pv = _dot(kvt_ref[cols[h], keys], p[h], NN) if whole else _dot(kvj[h], p[h], TN)
                acc_scr[cols[h], :] = alpha[h] * acc_scr[cols[h], :] + pv

        tile(pl.ds(pl.multiple_of(i * t, t), t), _visible_t(i, i, t))

        @pl.when(i > 0)
        def _():
            tile(pl.ds(DEAD, N_META), None, whole=False)

        def unmasked(j, carry):
            tile(pl.ds(pl.multiple_of(j * t, t), t), None)
            return carry

        lax.fori_loop(1, i, unmasked, 0)
        lse_ref[...] = jnp.zeros_like(lse_ref)
        for h in heads:
            l = l_scr[h:h + 1, :]
            o_ref[:, cols[h]] = jnp.where(lane >= QK_NOPE, (acc_scr[cols[h], :] / l).T, 0.0).astype(BF16)
            lse_ref[h // HEAD_GROUP, h % HEAD_GROUP:h % HEAD_GROUP + 1, :] = m_scr[h:h + 1, :] + jnp.log2(l)
        if n:
            @pl.when(i == pass_step)
            def _():
                _, arrivals, forwards, _ = _gather_copies(gather_refs[:n], gather_refs[n:2 * n], *gather_refs[2 * n:])
                for landed, onward in zip(arrivals, forwards):
                    landed().wait_recv()
                    onward().start()

            @pl.when(i == nq - 1)
            def _():
                sends, _, forwards, finals = _gather_copies(gather_refs[:n], gather_refs[n:2 * n], *gather_refs[2 * n:])
                for cp in finals:
                    cp().wait_recv()
                for cp in sends + forwards:
                    cp().wait_send()

    any_spec = pl.BlockSpec(memory_space=pl.ANY)
    outs = pl.pallas_call(
        body, name="attn_fwd", grid=(nq,),
        in_specs=[_rows(t, D_HEADS), _resident((L, D_HEADS), lambda i: (0, 0)),
                  _resident((D_HEADS, L), lambda i: (0, 0)), _resident((L, HB), lambda i: (0, 0))] + [any_spec] * n,
        out_specs=[_rows(t, D_HEADS), pl.BlockSpec((N_GROUPS, SUBLANES, t), lambda i: (0, 0, i))] + [any_spec] * n,
        out_shape=[_sds((L, D_HEADS), BF16), _sds((N_GROUPS, SUBLANES, L), F32)] + _gather_out_shapes(shards),
        scratch_shapes=[pltpu.VMEM((D_HEADS, t), BF16), pltpu.VMEM((N_HEADS, t), F32), pltpu.VMEM((N_HEADS, t), F32),
                        pltpu.VMEM((D_HEADS, t), F32)] + (_gather_semaphores(n) if n else []),
        compiler_params=_params(1))(q, kv, kvt, kr, *shards)
    return outs[0], outs[1], _gathered(outs[2:], shards)


def _fwd_mix(h0, mixa, o, ga, wout, gffn):
    L = h0.shape[0]

    def body(h0_ref, mixa_ref, o_ref, ga_ref, wout_ref, gffn_ref, mix_ref, h1_ref, n2_ref):
        of = o_ref[...].astype(F32)
        mixb = (of * _rms_r(of, N_HEADS * V_HEAD) * ga_ref[...]).astype(BF16)
        mix = jnp.concatenate([mixa_ref[...], mixb], 1)
        mix_ref[...] = mix
        mo = jnp.where(_row_ids(TM) >= DEAD, _dot(mix, wout_ref[...], NN), 0.0)
        h1 = h0_ref[...] + mo
        h1_ref[...] = h1
        n2_ref[...] = (h1 * _rms_r(h1, D_MODEL) * gffn_ref[...]).astype(BF16)

    return pl.pallas_call(
        body, name="fwd_mix", grid=(L // TM,),
        in_specs=[_rows(TM, D_MODEL), _rows(TM, D_CONV), _rows(TM, D_HEADS), _full(ga.shape), _full(wout.shape),
                  _full(gffn.shape)],
        out_specs=[_rows(TM, D_MIX), _rows(TM, D_MODEL), _rows(TM, D_MODEL)],
        out_shape=[_sds((L, D_MIX), BF16), _sds((L, D_MODEL), F32), _sds((L, D_MODEL), BF16)],
        compiler_params=_params(1))(h0, mixa, o, ga, wout, gffn)


def _ffn_act_chunk(c, upg_ref, upv_ref, hg, hv, fcw_ref, fcb_ref):
    cs = slice(c * FF_CHUNK, (c + 1) * FF_CHUNK)
    out = []
    for part, (up_ref, halo) in enumerate(((upg_ref, hg), (upv_ref, hv))):
        xx = jnp.concatenate([halo[:, cs], up_ref[:, cs].astype(F32)], 0)
        ws = slice(part * D_FF + c * FF_CHUNK, part * D_FF + (c + 1) * FF_CHUNK)
        y = (fcw_ref[0:1, ws] * pltpu.roll(xx, 2, 0)[HALO_FFN:] + fcw_ref[1:2, ws] * pltpu.roll(xx, 1, 0)[HALO_FFN:]
             + fcw_ref[2:3, ws] * xx[HALO_FFN:] + fcb_ref[:, ws])
        out.append(y)
    return out


def _ffn_in_specs(L):
    return [_rows(TM, D_FF, 0), _rows(TM, D_FF, 1), _prev(HALO_FFN, D_FF, TM, 0), _prev(HALO_FFN, D_FF, TM, 1)]


def _ffn_halos(hg_ref, hv_ref):
    first = pl.program_id(0) == 0
    return (jnp.where(first, 0.0, hg_ref[...].astype(F32)), jnp.where(first, 0.0, hv_ref[...].astype(F32)))


def _fwd_ffn_loss(up0, fcw, fcb, wdown, h1, target, gfin):
    L = h1.shape[0]

    def body(upg_ref, upv_ref, hg_ref, hv_ref, fcw_ref, fcb_ref, wd_ref, h1_ref, t_ref, gf_ref,
             dh2_ref, loss_ref, dgf_ref, up_ref, act_ref):
        i = pl.program_id(0)
        hg, hv = _ffn_halos(hg_ref, hv_ref)
        for c in range(D_FF // FF_CHUNK):
            cs = slice(c * FF_CHUNK, (c + 1) * FF_CHUNK)
            g, val = _ffn_act_chunk(c, upg_ref, upv_ref, hg, hv, fcw_ref, fcb_ref)
            up_ref[:, cs] = g.astype(BF16)
            up_ref[:, D_FF + c * FF_CHUNK:D_FF + (c + 1) * FF_CHUNK] = val.astype(BF16)
            act_ref[:, cs] = (g * jax.nn.sigmoid(g) * val).astype(BF16)
        h2 = h1_ref[...] + _dot(act_ref[...], wd_ref[...], NN)
        r = _rms_r(h2, D_MODEL)
        gf = gf_ref[...]
        err = jnp.where(i > 0, h2 * r * gf - t_ref[...], 0.0)
        dy = err * (1.0 / D_MODEL)
        dh2, dgf = _rms_bwd(dy, h2, r, gf, D_MODEL)
        dh2_ref[...] = dh2

        @pl.when(i == 0)
        def _():
            loss_ref[...] = jnp.zeros_like(loss_ref)
            dgf_ref[...] = jnp.zeros_like(dgf_ref)

        loss_ref[...] += jnp.sum(err * err) * (0.5 / D_MODEL)
        dgf_ref[...] += dgf

    return pl.pallas_call(
        body, name="fwd_ffn_loss", grid=(L // TM,),
        in_specs=_ffn_in_specs(L) + [_full(fcw.shape), _full(fcb.shape), _full(wdown.shape), _rows(TM, D_MODEL),
                                     pl.BlockSpec((TM, D_MODEL), lambda i: (jnp.maximum(i - 1, 0), 0)),
                                     _full(gfin.shape)],
        out_specs=[_rows(TM, D_MODEL), _full((1, LANES)), _full((1, D_MODEL)), _rows(TM, 2 * D_FF), _rows(TM, D_FF)],
        out_shape=[_sds((L, D_MODEL), F32), _sds((1, LANES), F32), _sds((1, D_MODEL), F32),
                   _sds((L, 2 * D_FF), BF16), _sds((L, D_FF), BF16)],
        compiler_params=_params(1))(up0, up0, up0, up0, fcw, fcb, wdown, h1, target, gfin)


def _bwd_ffn_act(dh2, up, wdown):
    L = dh2.shape[0]

    def body(dh2_ref, upg_ref, upv_ref, wd_ref, dup_ref, dfcb_ref):
        da = _dot(dh2_ref[...].astype(BF16), wd_ref[...], NT)

        @pl.when(pl.program_id(0) == 0)
        def _():
            dfcb_ref[...] = jnp.zeros_like(dfcb_ref)

        for c in range(D_FF // FF_CHUNK):
            cs = slice(c * FF_CHUNK, (c + 1) * FF_CHUNK)
            vs = slice(D_FF + c * FF_CHUNK, D_FF + (c + 1) * FF_CHUNK)
            g, val = upg_ref[:, cs].astype(F32), upv_ref[:, cs].astype(F32)
            sg = jax.nn.sigmoid(g)
            si = g * sg
            dac = da[:, cs]
            dg = dac * val * (sg * (1.0 + g * (1.0 - sg)))
            dv = dac * si
            dup_ref[:, cs] = dg.astype(BF16)
            dup_ref[:, vs] = dv.astype(BF16)
            dfcb_ref[:, cs] += jnp.sum(dg, 0, keepdims=True)
            dfcb_ref[:, vs] += jnp.sum(dv, 0, keepdims=True)

    return pl.pallas_call(
        body, name="bwd_ffn_act", grid=(L // TM,),
        in_specs=[_rows(TM, D_MODEL), _rows(TM, D_FF, 0), _rows(TM, D_FF, 1), _full(wdown.shape)],
        out_specs=[_rows(TM, 2 * D_FF), _full((1, 2 * D_FF))],
        out_shape=[_sds((L, 2 * D_FF), BF16), _sds((1, 2 * D_FF), F32)],
        compiler_params=_params(1))(dh2, up, up, wdown)


def _bwd_ffn_conv_up(dup, up0, fcw, wup, tl):
    L, C = dup.shape
    tc = FF_TILE
    nt = L // tl
    nhb = L // HALO_FFN
    chunks = [(c0, min(FF_MXU_CHUNK, tc - c0)) for c0 in range(0, tc, FF_MXU_CHUNK)]

    def body(dy_ref, dyn_ref, x_ref, xp_ref, w_ref, wup_ref, dn2_ref, dx_ref, dw_ref, acc_ref):
        i, k = pl.program_id(0), pl.program_id(1)

        @pl.when(k == 0)
        def _():
            acc_ref[...] = jnp.zeros_like(acc_ref)

        last, first = i == nt - 1, i == 0
        for c0, cw in chunks:
            cs = slice(c0, c0 + cw)
            yy = jnp.concatenate([dy_ref[:, cs].astype(F32), jnp.where(last, 0.0, dyn_ref[:, cs].astype(F32))], 0)
            w = w_ref[:, cs]
            dx = (w[0:1] * pltpu.roll(yy, tl + HALO_FFN - 2, 0)[:tl] + w[1:2] * pltpu.roll(yy, tl + HALO_FFN - 1, 0)[:tl]
                  + w[2:3] * yy[:tl]).astype(BF16)
            dx_ref[:, cs] = dx
            acc_ref[...] += _dot(dx, wup_ref[:, cs], NT)
            xx = jnp.concatenate([jnp.where(first, 0.0, xp_ref[:, cs].astype(F32)), x_ref[:, cs].astype(F32)], 0)
            dy = yy[:tl]
            dw_ref[0:1, cs] = jnp.sum(dy * pltpu.roll(xx, 2, 0)[HALO_FFN:], 0, keepdims=True)
            dw_ref[1:2, cs] = jnp.sum(dy * pltpu.roll(xx, 1, 0)[HALO_FFN:], 0, keepdims=True)
            dw_ref[2:3, cs] = jnp.sum(dy * xx[HALO_FFN:], 0, keepdims=True)

        @pl.when(k == C // tc - 1)
        def _():
            dn2_ref[...] = acc_ref[...]

    tile = pl.BlockSpec((tl, tc), lambda i, k: (i, k))
    per = tl // HALO_FFN
    return pl.pallas_call(
        body, name="bwd_ffn_conv_up", grid=(nt, C // tc),
        in_specs=[tile, pl.BlockSpec((HALO_FFN, tc), lambda i, k: (jnp.minimum((i + 1) * per, nhb - 1), k)),
                  tile, pl.BlockSpec((HALO_FFN, tc), lambda i, k: (jnp.maximum(i * per - 1, 0), k)),
                  pl.BlockSpec((FFN_CONV_WIDTH, tc), lambda i, k: (0, k)),
                  pl.BlockSpec((None, D_MODEL, tc), lambda i, k: (k, 0, 0))],
        out_specs=[pl.BlockSpec((tl, D_MODEL), lambda i, k: (i, 0)), tile,
                   pl.BlockSpec((None, FFN_CONV_WIDTH, tc), lambda i, k: (i, 0, k))],
        out_shape=[_sds((L, D_MODEL), F32), _sds((L, C), BF16), _sds((nt, FFN_CONV_WIDTH, C), F32)],
        scratch_shapes=[pltpu.VMEM((tl, D_MODEL), F32)],
        compiler_params=_params(2))(dup, dup, up0, up0, fcw, wup)


def _carrying(core, n_in, n_out, n, copies_fn, steps):
    def body(*refs):
        exchange = (refs[n_in:n_in + n], refs[n_in + n + n_out:n_in + 2 * n + n_out]) + refs[n_in + 2 * n + n_out:]
        if n:
            @pl.when(pl.program_id(0) == 0)
            def _():
                for cp in copies_fn(*exchange):
                    cp().start()

        core(*refs[:n_in], *refs[n_in + n:n_in + n + n_out])
        if n:
            @pl.when(pl.program_id(0) == steps - 1)
            def _():
                for cp in copies_fn(*exchange):
                    cp().wait()

    return body


def _bwd_mix(dh2, dn2, h1, gffn, wout, o, ga, u1, lg, lb, og, parts=()):
    L = h1.shape[0]

    def body(dh2_ref, dn2_ref, h1_ref, gffn_ref, wout_ref, o_ref, ga_ref, u1_ref, lg_ref, lb_ref, og_ref,
             dh1_ref, dh1m_ref, do_ref, delta_ref, du1_ref, dgffn_ref, dga_ref, dog_ref, dlg_ref, dlb_ref):
        h1 = h1_ref[...]
        dn2x, dgffn = _rms_bwd(dn2_ref[...], h1, _rms_r(h1, D_MODEL), gffn_ref[...], D_MODEL)
        dh1 = dh2_ref[...] + dn2x
        dh1_ref[...] = dh1
        dh1m = jnp.where(_row_ids(TM) >= DEAD, dh1, 0.0).astype(BF16)
        dh1m_ref[...] = dh1m
        dmix = _dot(dh1m, wout_ref[...], NT)
        dma, dmb = dmix[:, :D_CONV], dmix[:, D_CONV:]
        of = o_ref[...].astype(F32)
        do, dga = _rms_bwd(dmb, of, _rms_r(of, N_HEADS * V_HEAD), ga_ref[...], N_HEADS * V_HEAD)
        do_ref[...] = do.astype(BF16)
        prod = do * of
        by_lane = _scatter_stats([jnp.sum(prod[:, h * HB:(h + 1) * HB], -1, keepdims=True) for h in range(N_HEADS)], TM)
        by_row = by_lane.T
        for grp in range(N_GROUPS):
            delta_ref[grp] = by_row[grp * LANES:grp * LANES + SUBLANES, :]
        lg = lg_ref[...]
        rs, xh, u2, sg, u = _ln_silu(u1_ref[...], lg, lb_ref[...])
        du, dog = _rms_bwd(dma, u, _rms_r(u, D_CONV), og_ref[...], D_CONV)
        du2 = du * (sg * (1.0 + u2 * (1.0 - sg)))
        dxh = du2 * lg
        du1_ref[...] = rs * (dxh - jnp.mean(dxh, -1, keepdims=True) - xh * jnp.mean(dxh * xh, -1, keepdims=True))

        @pl.when(pl.program_id(0) == 0)
        def _():
            for ref in (dgffn_ref, dga_ref, dog_ref, dlg_ref, dlb_ref):
                ref[...] = jnp.zeros_like(ref)

        dgffn_ref[...] += dgffn
        dga_ref[...] += dga
        dog_ref[...] += dog
        dlg_ref[...] += jnp.sum(du2 * xh, 0, keepdims=True)
        dlb_ref[...] += jnp.sum(du2, 0, keepdims=True)

    n = len(parts)
    any_spec = pl.BlockSpec(memory_space=pl.ANY)
    outs = pl.pallas_call(
        _carrying(body, 11, 10, n, _sibling_exchange_copies, L // TM), name="bwd_mix", grid=(L // TM,),
        in_specs=[_rows(TM, D_MODEL), _rows(TM, D_MODEL), _rows(TM, D_MODEL), _full(gffn.shape), _full(wout.shape),
                  _rows(TM, D_HEADS), _full(ga.shape), _rows(TM, D_CONV), _full(lg.shape), _full(lb.shape),
                  _full(og.shape)] + [any_spec] * n,
        out_specs=[_rows(TM, D_MODEL), _rows(TM, D_MODEL), _rows(TM, D_HEADS),
                   pl.BlockSpec((N_GROUPS, SUBLANES, TM), lambda i: (0, 0, i)),
                   _rows(TM, D_CONV), _full((1, D_MODEL)), _full((1, D_HEADS)), _full((1, D_CONV)),
                   _full((1, D_CONV)), _full((1, D_CONV))] + [any_spec] * n,
        out_shape=[_sds((L, D_MODEL), F32), _sds((L, D_MODEL), BF16), _sds((L, D_HEADS), BF16),
                   _sds((N_GROUPS, SUBLANES, L), F32),
                   _sds((L, D_CONV), F32), _sds((1, D_MODEL), F32), _sds((1, D_HEADS), F32), _sds((1, D_CONV), F32),
                   _sds((1, D_CONV), F32), _sds((1, D_CONV), F32)] + _sibling_exchange_shapes(parts),
        scratch_shapes=_sibling_exchange_semaphores(n) if n else [],
        compiler_params=_params(1))(dh2, dn2, h1, gffn, wout, o, ga, u1, lg, lb, og, *parts)
    return outs[:10], list(outs[10:])


def _attn_bwd(q, kv, kr, do, lse, delta, parts=()):
    L = q.shape[0]
    t = TM
    nt = L // t
    gw = HEAD_GROUP * HB
    n = len(parts)

    def body(q_ref, kv_ref, kr_ref, do_ref, lse_ref, delta_ref, *refs):
        dq_ref, dkv_ref, dkr_ref = refs[n:n + 3]
        dqt_acc, dk_acc, dv_acc, kkt_scr = refs[2 * n + 3:2 * n + 7]
        exchange_refs = (refs[:n], refs[n + 3:2 * n + 3]) + refs[2 * n + 7:]
        g, j = pl.program_id(0), pl.program_id(1)
        if n:
            @pl.when((g == 0) & (j == 0))
            def _():
                for cp in _chip_copies(*exchange_refs)[0]:
                    cp().start()

        lane = lax.broadcasted_iota(jnp.int32, (t, HB), 1)

        @pl.when(j == 0)
        def _():
            dqt_acc[...] = jnp.zeros_like(dqt_acc)

        @pl.when((j == 0) & (g == 0))
        def _():
            dkr_ref[...] = jnp.zeros_like(dkr_ref)

        dk_acc[...] = jnp.zeros_like(dk_acc)
        dv_acc[...] = jnp.zeros_like(dv_acc)
        krj = kr_ref[...]
        heads = range(HEAD_GROUP)
        cols = [slice(h * HB, (h + 1) * HB) for h in heads]
        for hc in cols:
            kkt_scr[hc, :] = jnp.where(lane < QK_NOPE, kv_ref[:, hc], krj).astype(F32).T.astype(BF16)

        def tile(i, vis, whole=True):
            qs = pl.ds(pl.multiple_of(i * t, t), t)
            keys = slice(None) if whole else slice(DEAD, t)
            kvj = [kv_ref[keys, hc] for hc in cols]
            lane_k = lane[:kvj[0].shape[0]]
            kk = [jnp.where(lane_k < QK_NOPE, kvj[h], krj[keys]) for h in heads]
            qi = [q_ref[qs, hc] for hc in cols]
            doi = [do_ref[qs, hc] for hc in cols]
            s = [_dot(kk[h], qi[h].astype(F32).T.astype(BF16), NN) for h in heads]
            dp = [_dot(kvj[h], doi[h].astype(F32).T.astype(BF16), NN) for h in heads]
            p = []
            for h in heads:
                sh = s[h] if vis is None else jnp.where(vis, s[h], NEG)
                p.append(jnp.exp2(sh - lse_ref[h:h + 1, qs]))
            for h in heads:
                dv_acc[keys, cols[h]] += _dot(p[h].astype(BF16), doi[h], NN)
            ds = [(p[h] * (dp[h] - delta_ref[h:h + 1, qs]) * LN2).astype(BF16) for h in heads]
            for h in heads:
                dk_acc[keys, cols[h]] += _dot(ds[h], qi[h], NN)
            for h in heads:
                dqt = _dot(kkt_scr[cols[h], :], ds[h], NN) if whole else _dot(kk[h], ds[h], TN)
                dqt_acc[cols[h], qs] += dqt

        @pl.when(j == 0)
        def _():
            tile(0, _visible_t(0, 0, t)[DEAD:], whole=False)

            def meta_keys(i, carry):
                tile(i, None, whole=False)
                return carry

            lax.fori_loop(1, nt, meta_keys, 0)

        @pl.when(j > 0)
        def _():
            tile(j, _visible_t(j, j, t))

            def unmasked(i, carry):
                tile(i, None)
                return carry

            lax.fori_loop(j + 1, nt, unmasked, 0)

        dkr = jnp.zeros((t, HB), F32)
        for h in range(HEAD_GROUP):
            hc = slice(h * HB, (h + 1) * HB)
            dk = dk_acc[:, hc]
            dkv_ref[:, hc] = jnp.where(lane < QK_NOPE, dk, dv_acc[:, hc]).astype(BF16)
            dkr = dkr + jnp.where(lane >= QK_NOPE, dk, 0.0)
        dkr_ref[pl.ds(pl.multiple_of(j * t, t), t), :] += dkr

        @pl.when(j == nt - 1)
        def _():
            def untranspose(i, carry):
                qs = pl.ds(pl.multiple_of(i * t, t), t)
                dq_ref[qs, :] = (dqt_acc[:, qs].T * Q_SCALE).astype(BF16)
                return carry

            lax.fori_loop(0, nt, untranspose, 0)

        if n:
            @pl.when((g == N_GROUPS - 1) & (j == nt - 1))
            def _():
                sends, arrivals = _chip_copies(*exchange_refs)
                for cp in arrivals:
                    cp().wait_recv()
                for cp in sends:
                    cp().wait_send()

    group = lambda g, j: (0, g)
    stats = _resident((None, SUBLANES, L), lambda g, j: (g, 0, 0))
    any_spec = pl.BlockSpec(memory_space=pl.ANY)
    outs = pl.pallas_call(
        body, name="attn_bwd", grid=(N_GROUPS, nt),
        in_specs=[_resident((L, gw), group), pl.BlockSpec((t, gw), lambda g, j: (j, g)),
                  pl.BlockSpec((t, HB), lambda g, j: (j, 0)), _resident((L, gw), group), stats, stats]
        + [any_spec] * n,
        out_specs=[pl.BlockSpec((L, gw), group), pl.BlockSpec((t, gw), lambda g, j: (j, g)),
                   pl.BlockSpec((L, HB), lambda g, j: (0, 0))] + [any_spec] * n,
        out_shape=[_sds((L, D_HEADS), BF16), _sds((L, D_HEADS), BF16), _sds((L, HB), F32)]
        + [_sds(p.shape, p.dtype) for p in parts],
        scratch_shapes=[pltpu.VMEM((gw, L), F32), pltpu.VMEM((t, gw), F32), pltpu.VMEM((t, gw), F32),
                        pltpu.VMEM((gw, t), BF16)] + (_chip_semaphores(n) if n else []),
        compiler_params=_params(2))(q, kv, kr, do, lse, delta, *parts)
    return outs[0], outs[1], outs[2], list(outs[3:])


def _bwd_conv(du1, u0, cw, zag, halves=()):
    L = du1.shape[0]
    nt = L // TM

    def body(dy_ref, dyn_ref, x_ref, xp_ref, cw_ref, zag_ref, dzag_ref, dcw_ref, dcb_ref):
        i = pl.program_id(0)
        dy = dy_ref[...]
        yy = jnp.concatenate([dy, jnp.where(i < nt - 1, dyn_ref[...], 0.0)], 0)
        du0 = _conv_taps(yy, cw_ref, HALO_CONV, TM, True)
        zag = zag_ref[...].astype(F32)
        a, sg = zag[:, :D_CONV], jax.nn.sigmoid(zag[:, D_CONV:])
        dzag_ref[...] = jnp.concatenate([du0 * sg, du0 * a * sg * (1.0 - sg)], 1).astype(BF16)
        xx = jnp.concatenate([jnp.where(i > 0, xp_ref[...], 0.0), x_ref[...]], 0)

        @pl.when(i == 0)
        def _():
            dcw_ref[...] = jnp.zeros_like(dcw_ref)
            dcb_ref[...] = jnp.zeros_like(dcb_ref)

        rolls = {}
        for k in range(CONV_WIDTH):
            xs = _shifted_rows(xx, CONV_WIDTH - 1 - k, HALO_CONV, TM, False, rolls)
            dcw_ref[k:k + 1, :] += jnp.sum(dy * xs, 0, keepdims=True)
        dcb_ref[...] += jnp.sum(dy, 0, keepdims=True)

    n = len(halves)
    any_spec = pl.BlockSpec(memory_space=pl.ANY)
    outs = pl.pallas_call(
        _carrying(body, 6, 3, n, _sibling_gather_copies, nt), name="bwd_conv", grid=(nt,),
        in_specs=[_rows(TM, D_CONV), _next(HALO_CONV, D_CONV, TM, L // HALO_CONV), _rows(TM, D_CONV),
                  _prev(HALO_CONV, D_CONV, TM), _full(cw.shape), _rows(TM, D_AG)] + [any_spec] * n,
        out_specs=[_rows(TM, D_AG), _full(cw.shape), _full((1, D_CONV))] + [any_spec] * n,
        out_shape=[_sds((L, D_AG), BF16), _sds(cw.shape, F32), _sds((1, D_CONV), F32)]
        + [_sds(p.shape, p.dtype) for p in halves],
        input_output_aliases={6 + a: 3 + a for a in range(n)},
        scratch_shapes=_sibling_gather_semaphores(n) if n else [],
        compiler_params=_params(1))(du1, du1, u0, u0, cw, zag, *halves)
    return outs[:3], list(outs[3:])


def _bwd_in(dzag, dq, dkv, dkr, cq, ckv, gq, gkv, wuq, wukv, win, rc, rs1, rs2, h0, gmix, dh1):
    L = h0.shape[0]

    def body(dzag_ref, dq_ref, dkv_ref, dkr_ref, cq_ref, ckv_ref, gq_ref, gkv_ref, wuq_ref, wukv_ref, win_ref,
             c_ref, s1_ref, s2_ref, h0_ref, gmix_ref, dh1_ref,
             dz_ref, dqr_ref, gx_ref, dfirst_ref, dgq_ref, dgkv_ref, dgmix_ref):
        i = pl.program_id(0)
        c, s1, s2 = c_ref[...], s1_ref[...], s2_ref[...]
        dqr = _rope_bwd(dq_ref[...].astype(F32), jnp.tile(c, (1, N_HEADS)), jnp.tile(s1, (1, N_HEADS)),
                        jnp.tile(s2, (1, N_HEADS))).astype(BF16)
        dqr_ref[...] = dqr
        cq, ckv = cq_ref[...], ckv_ref[...]
        dcq, dgq = _rms_bwd(_dot(dqr, wuq_ref[...], NT), cq, _rms_r(cq, Q_LORA), gq_ref[...], Q_LORA)
        dckv, dgkv = _rms_bwd(_dot(dkv_ref[...], wukv_ref[...], NT), ckv, _rms_r(ckv, KV_LORA), gkv_ref[...], KV_LORA)
        dkrp = _rope_bwd(dkr_ref[...], c, s1, s2)
        dz = jnp.concatenate([dzag_ref[...], dcq.astype(BF16), dckv.astype(BF16), dkrp.astype(BF16)], 1)
        dz_ref[...] = dz
        h0 = h0_ref[...]
        dnx, dgmix = _rms_bwd(_dot(dz, win_ref[...], NT), h0, _rms_r(h0, D_MODEL), gmix_ref[...], D_MODEL)
        dh0 = dh1_ref[...] + dnx

        @pl.when(i == 0)
        def _():
            dfirst_ref[...] = dh0
            for ref in (dgq_ref, dgkv_ref, dgmix_ref):
                ref[...] = jnp.zeros_like(ref)

        @pl.when(i > 0)
        def _():
            gx_ref[...] = dh0

        dgq_ref[...] += dgq
        dgkv_ref[...] += dgkv
        dgmix_ref[...] += dgmix

    return pl.pallas_call(
        body, name="bwd_in", grid=(L // TM,),
        in_specs=[_rows(TM, D_AG), _rows(TM, D_HEADS), _rows(TM, D_HEADS), _rows(TM, HB), _rows(TM, Q_LORA),
                  _rows(TM, KV_LORA), _full(gq.shape), _full(gkv.shape), _full(wuq.shape), _full(wukv.shape),
                  _full(win.shape), _rows(TM, HB), _rows(TM, HB), _rows(TM, HB), _rows(TM, D_MODEL),
                  _full(gmix.shape), _rows(TM, D_MODEL)],
        out_specs=[_rows(TM, D_ZP), _rows(TM, D_HEADS),
                   pl.BlockSpec((TM, D_MODEL), lambda i: (jnp.maximum(i - 1, 0), 0)), _full((TM, D_MODEL)),
                   _full((1, Q_LORA)), _full((1, KV_LORA)), _full((1, D_MODEL))],
        out_shape=[_sds((L, D_ZP), BF16), _sds((L, D_HEADS), BF16), _sds((L - TM, D_MODEL), F32),
                   _sds((TM, D_MODEL), F32), _sds((1, Q_LORA), F32), _sds((1, KV_LORA), F32), _sds((1, D_MODEL), F32)],
        compiler_params=_params(1))(dzag, dq, dkv, dkr, cq, ckv, gq, gkv, wuq, wukv, win, rc, rs1, rs2, h0, gmix, dh1)


def _mesh_pos():
    return lax.axis_index("x"), lax.axis_index("y"), lax.axis_index("c")


def _remote_copy(src, dst, send_sem, recv_sem, to):
    return functools.partial(pltpu.make_async_remote_copy, src, dst, send_sem, recv_sem, device_id=to,
                             device_id_type=MESH)


def _all_gather(shards):
    n = len(shards)

    def body(*refs):
        sends, arrivals, forwards, finals = _gather_copies(refs[:n], refs[n:2 * n], *refs[2 * n:])
        for cp in sends:
            cp().start()
        for landed, onward in zip(arrivals, forwards):
            landed().wait_recv()
            onward().start()
        for cp in finals:
            cp().wait_recv()
        for cp in sends + forwards:
            cp().wait_send()

    any_spec = pl.BlockSpec(memory_space=pl.ANY)
    outs = pl.pallas_call(
        body, name="all_gather_weights", in_specs=[any_spec] * n, out_specs=[any_spec] * n,
        out_shape=_gather_out_shapes(shards), scratch_shapes=_gather_semaphores(n))(*shards)
    return _gathered(outs, shards)


def _gather_out_shapes(shards):
    return [_sds((2 * N_CHIPS, s.shape[0] // 2) + s.shape[1:], s.dtype) for s in shards]


def _gather_semaphores(n):
    return [pltpu.SemaphoreType.DMA((n, 8)), pltpu.SemaphoreType.DMA((n, 8))]


def _gathered(outs, shards):
    return [o.reshape((N_CHIPS, s.shape[0]) + s.shape[1:]) for o, s in zip(outs, shards)]


def _gather_copies(ins, outs, send_sems, recv_sems):
    x, y, c = _mesh_pos()
    chips = [(1 - x, y), (x, 1 - y), (1 - x, 1 - y)]
    sends, arrivals, forwards, finals = [], [], [], []

    def copy(src, dst, a, k, to):
        return _remote_copy(src, dst, send_sems.at[a, k], recv_sems.at[a, k], to)

    for a, (src, out) in enumerate(zip(ins, outs)):
        m = out.shape[1]
        mine = src.at[pl.ds(pl.multiple_of(c * m, 16), m)]
        for hf in range(2):
            own = out.at[4 * x + 2 * y + hf]
            sends.append(copy(src.at[pl.ds(hf * m, m)], own, a, 6 + hf, (x, y, 1 - c)))
            finals.append(copy(own, own, a, 6 + hf, (x, y, 1 - c)))
        for k, chip in enumerate(chips):
            slot = 4 * chip[0] + 2 * chip[1]
            sends.append(copy(mine, out.at[4 * x + 2 * y + c], a, k, (*chip, c)))
            arrivals.append(copy(out.at[slot + c], out.at[slot + c], a, k, (*chip, c)))
            forwards.append(copy(out.at[slot + c], out.at[slot + c], a, 3 + k, (x, y, 1 - c)))
            finals.append(copy(out.at[slot + 1 - c], out.at[slot + 1 - c], a, 3 + k, (x, y, 1 - c)))
    return sends, arrivals, forwards, finals


def _sibling_exchange(parts, name):
    n = len(parts)

    def body(*refs):
        copies = _sibling_exchange_copies(refs[:n], refs[n:2 * n], *refs[2 * n:])
        for cp in copies:
            cp().start()
        for cp in copies:
            cp().wait()

    any_spec = pl.BlockSpec(memory_space=pl.ANY)
    return pl.pallas_call(
        body, name=name, in_specs=[any_spec] * n, out_specs=[any_spec] * n,
        out_shape=_sibling_exchange_shapes(parts), scratch_shapes=_sibling_exchange_semaphores(n))(*parts)


def _sibling_exchange_shapes(parts):
    return [_sds((N_CHIPS, p.shape[1] // 2, p.shape[2]), p.dtype) for p in parts]


def _sibling_exchange_semaphores(n):
    return [pltpu.SemaphoreType.DMA((n, N_CHIPS)), pltpu.SemaphoreType.DMA((n, N_CHIPS))]


def _sibling_exchange_copies(ins, theirs, send_sems, recv_sems):
    x, y, c = _mesh_pos()
    copies = []
    for a, (src, dst) in enumerate(zip(ins, theirs)):
        h = dst.shape[1]
        rows = pl.ds(pl.multiple_of((1 - c) * h, 16), h)
        copies += [_remote_copy(src.at[q, rows], dst.at[q], send_sems.at[a, q], recv_sems.at[a, q], (x, y, 1 - c))
                   for q in range(N_CHIPS)]
    return copies


def _chip_exchange(parts, name):
    n = len(parts)

    def body(*refs):
        sends, arrivals = _chip_copies(refs[:n], refs[n:2 * n], *refs[2 * n:])
        for cp in sends:
            cp().start()
        for cp in arrivals:
            cp().wait_recv()
        for cp in sends:
            cp().wait_send()

    any_spec = pl.BlockSpec(memory_space=pl.ANY)
    return pl.pallas_call(
        body, name=name, in_specs=[any_spec] * n, out_specs=[any_spec] * n,
        out_shape=[_sds(p.shape, p.dtype) for p in parts], scratch_shapes=_chip_semaphores(n))(*parts)


def _chip_semaphores(n):
    return [pltpu.SemaphoreType.DMA((n, 3)), pltpu.SemaphoreType.DMA((n, 3))]


def _chip_copies(ins, outs, send_sems, recv_sems):
    x, y, c = _mesh_pos()
    me = 2 * x + y
    sends, arrivals = [], []
    for a, (src, out) in enumerate(zip(ins, outs)):
        for k, chip in enumerate([(1 - x, y), (x, 1 - y), (1 - x, 1 - y)]):
            slot = 2 * chip[0] + chip[1]
            sems = (send_sems.at[a, k], recv_sems.at[a, k], (*chip, c))
            sends.append(_remote_copy(src.at[slot], out.at[me], *sems))
            arrivals.append(_remote_copy(out.at[slot], out.at[slot], *sems))
    return sends, arrivals


def _sibling_gather(parts, name):
    n = len(parts)

    def body(*refs):
        copies = _sibling_gather_copies(refs[:n], refs[n:2 * n], *refs[2 * n:])
        for cp in copies:
            cp().start()
        for cp in copies:
            cp().wait()

    any_spec = pl.BlockSpec(memory_space=pl.ANY)
    return pl.pallas_call(
        body, name=name, in_specs=[any_spec] * n, out_specs=[any_spec] * n,
        out_shape=[_sds(p.shape, p.dtype) for p in parts], input_output_aliases={a: a for a in range(n)},
        scratch_shapes=_sibling_gather_semaphores(n))(*parts)


def _sibling_gather_semaphores(n):
    return [pltpu.SemaphoreType.DMA((n,)), pltpu.SemaphoreType.DMA((n,))]


def _sibling_gather_copies(ins, outs, send_sems, recv_sems):
    x, y, c = _mesh_pos()
    return [_remote_copy(src.at[c], dst.at[c], send_sems.at[a], recv_sems.at[a], (x, y, 1 - c))
            for a, (src, dst) in enumerate(zip(ins, outs))]


def _row_tile(rows, row_bytes, align, budget=1 << 20):
    best = None
    for t in range(align, rows + 1, align):
        if rows % t == 0 and t * row_bytes <= budget:
            best = t
    return best or rows


def _scalar(v):
    return jnp.reshape(v, (1,)).astype(jnp.int32)


def _add_pair(part, theirs, c, name):
    _, h, cols = theirs.shape
    tr = _row_tile(h, cols * 4, 16)
    nb = h // tr

    def body(c_ref, a_ref, b_ref, o_ref):
        o_ref[...] = (a_ref[...].astype(F32) + b_ref[...].astype(F32)).astype(o_ref.dtype)

    half = pl.BlockSpec((None, tr, cols), lambda q, i, c_ref: (q, i, 0))
    grid_spec = pltpu.PrefetchScalarGridSpec(
        num_scalar_prefetch=1, grid=(N_CHIPS, nb),
        in_specs=[pl.BlockSpec((None, tr, cols), lambda q, i, c_ref: (q, c_ref[0] * nb + i, 0)), half],
        out_specs=half)
    return pl.pallas_call(body, name=name, grid_spec=grid_spec, out_shape=_sds(theirs.shape, part.dtype),
                          compiler_params=_params(2))(_scalar(c), part, theirs)


def _add_chips(got, own, me, c, name):
    _, h, cols = got.shape
    tr = _row_tile(h, cols * 4 * N_CHIPS, 16, budget=1 << 21)

    def body(pos_ref, got_ref, own_ref, o_ref):
        acc = None
        for q in range(N_CHIPS):
            term = jnp.where(pos_ref[0] == q, own_ref[q], got_ref[q]).astype(F32)
            acc = term if acc is None else acc + term
        o_ref[...] = acc

    by_chip = pl.BlockSpec((N_CHIPS, tr, cols), lambda i, pos_ref: (0, i, 0))
    grid_spec = pltpu.PrefetchScalarGridSpec(
        num_scalar_prefetch=1, grid=(h // tr,), in_specs=[by_chip, by_chip],
        out_specs=pl.BlockSpec((None, tr, cols), lambda i, pos_ref: (pos_ref[1], i, 0)))
    return pl.pallas_call(body, name=name, grid_spec=grid_spec, out_shape=_sds((2, h, cols), F32),
                          compiler_params=_params(1))(jnp.stack([me, c]).astype(jnp.int32), got, own)


def _add_pairs(parts, theirs, tag):
    c = lax.axis_index("c")
    return [_add_pair(p, t, c, f"grad_add_pair_{tag}_{a}") for a, (p, t) in enumerate(zip(parts, theirs))]


def _add_all_chips(pair, got, tag):
    x, y, c = _mesh_pos()
    return [_add_chips(g, p, 2 * x + y, c, f"grad_add_chips_{tag}_{a}") for a, (g, p) in enumerate(zip(got, pair))]


def _totals(both):
    return [b.reshape(-1, b.shape[-1]) for b in both]


def _reduce_begin(parts, tag):
    return _add_pairs(parts, _sibling_exchange(parts, f"grad_sibling_exchange_{tag}"), tag)


def _reduce_end(pair, got, tag):
    return _totals(_sibling_gather(_add_all_chips(pair, got, tag), f"grad_sibling_gather_{tag}"))


def _adamw_math(w, g, m, v):
    m = ADAM_B1 * m + (1.0 - ADAM_B1) * g
    v = ADAM_B2 * v + (1.0 - ADAM_B2) * (g * g)
    m_hat = m / (1.0 - ADAM_B1 ** ADAM_STEP)
    v_hat = v / (1.0 - ADAM_B2 ** ADAM_STEP)
    return -ADAM_LR * (m_hat / (jnp.sqrt(v_hat) + ADAM_EPS) + ADAM_WD * w), m, v


def _adamw_big(w, g, m, v, name, parts=()):
    r, c = w.shape
    tr = _row_tile(r, c * 4, 8, budget=1 << 19)
    if r // tr > ADAMW_MAX_STEPS:
        tr = r
    n = len(parts)
    steps = r // tr

    def body(w_ref, g_ref, m_ref, v_ref, *refs):
        go_ref, d_ref, mo_ref, vo_ref = refs[n:n + 4]
        exchange_refs = (refs[:n], refs[n + 4:2 * n + 4]) + refs[2 * n + 4:]
        if n:
            @pl.when(pl.program_id(0) == 0)
            def _():
                for cp in _chip_copies(*exchange_refs)[0]:
                    cp().start()

        g = g_ref[...]
        go_ref[...] = g
        d_ref[...], mo_ref[...], vo_ref[...] = _adamw_math(w_ref[...], g, m_ref[...], v_ref[...])
        if n:
            @pl.when(pl.program_id(0) == steps - 1)
            def _():
                sends, arrivals = _chip_copies(*exchange_refs)
                for cp in arrivals:
                    cp().wait_recv()
                for cp in sends:
                    cp().wait_send()

    any_spec = pl.BlockSpec(memory_space=pl.ANY)
    outs = pl.pallas_call(
        body, name=name, grid=(steps,), in_specs=[_rows(tr, c)] * 4 + [any_spec] * n,
        out_specs=[_rows(tr, c)] * 4 + [any_spec] * n,
        out_shape=[_sds((r, c), F32)] * 4 + [_sds(p.shape, p.dtype) for p in parts],
        scratch_shapes=_chip_semaphores(n) if n else [], compiler_params=_params(1))(w, g, m, v, *parts)
    return outs[:4], list(outs[4:])


def _adamw_small(ws, gs, ms, vs):
    n = len(ws)

    def body(*refs):
        for a in range(n):
            w_ref, g_ref, m_ref, v_ref = (refs[k * n + a] for k in range(4))
            d, m, v = _adamw_math(w_ref[...], g_ref[...], m_ref[...], v_ref[...])
            refs[4 * n + a][...] = d
            refs[5 * n + a][...] = m
            refs[6 * n + a][...] = v

    vm = pl.BlockSpec(memory_space=pltpu.VMEM)
    outs = pl.pallas_call(
        body, name="adamw_small", in_specs=[vm] * (4 * n), out_specs=[vm] * (3 * n),
        out_shape=[_sds(w.shape, F32) for w in ws] * 3)(*ws, *gs, *ms, *vs)
    return outs[:n], outs[n:2 * n], outs[2 * n:]


BIG = ("w_in", "w_uq", "w_ukv", "w_out", "w_ffn_up", "w_ffn_down")
SMALL_SHARDED = ("conv_w", "ffn_conv_w", "meta_tokens")
REPLICATED = ("mix_norm_g", "q_norm_g", "kv_norm_g", "conv_b", "conv_ln_g", "conv_ln_b", "conv_out_g", "attn_out_g",
              "ffn_norm_g", "ffn_conv_b", "final_norm_g")
WEIGHTS = ("meta_tokens", "mix_norm_g", "w_in", "q_norm_g", "w_uq", "kv_norm_g", "w_ukv", "conv_w", "conv_b",
           "conv_ln_g", "conv_ln_b", "conv_out_g", "attn_out_g", "w_out", "ffn_norm_g", "w_ffn_up", "ffn_conv_w",
           "ffn_conv_b", "w_ffn_down", "final_norm_g")


def _lane_rows(a):
    return a.reshape(N_CHIPS, -1, LANES)


def _col_shards(a):
    k = a.shape[0]
    return a.reshape(k, N_CHIPS, -1).transpose(1, 0, 2)


def _from_col_shards(a):
    return a.transpose(1, 0, 2).reshape(a.shape[1], -1)


def _pad_rows_to(a, rows):
    return jnp.pad(a, ((0, 0), (0, rows - a.shape[1]), (0, 0)))


def _rope_tables(L):
    pos = (jnp.arange(L, dtype=jnp.int32) - DEAD).astype(F32)
    inv_freq = 1.0 / (ROPE_THETA ** (jnp.arange(0, QK_ROPE, 2, dtype=F32) / QK_ROPE))
    ang = pos[:, None] * inv_freq[None, :]
    cos, sin = jnp.cos(ang), jnp.sin(ang)
    half = QK_ROPE // 2
    z = lambda n: jnp.zeros((L, n), F32)
    rc = jnp.concatenate([jnp.ones((L, QK_NOPE), F32), cos, cos, z(HB - QK_NOPE - QK_ROPE)], 1)
    rs1 = jnp.concatenate([z(QK_NOPE), -sin, z(HB - QK_NOPE - half)], 1)
    rs2 = jnp.concatenate([z(QK_NOPE + half), sin, z(HB - QK_NOPE - QK_ROPE)], 1)
    return rc, rs1, rs2


def _pad_heads(g):
    return jnp.pad(g.reshape(N_HEADS, V_HEAD), ((0, 0), (HB - V_HEAD, 0))).reshape(1, D_HEADS)


def _unpad_heads(g):
    return g.reshape(N_HEADS, HB)[:, HB - V_HEAD:].reshape(1, N_HEADS * V_HEAD)


def _local_step(x, target, w, late_shards=None, reduce_first=False):
    S = x.shape[0]
    L = TM + S
    tl = L // 4
    d_qk = QK_NOPE + QK_ROPE
    win_n = w["w_in"]
    kr0 = D_AG + Q_LORA + KV_LORA
    win = jnp.concatenate([win_n[:, :kr0], jnp.zeros((D_MODEL, QK_NOPE), BF16), win_n[:, kr0:],
                           jnp.zeros((D_MODEL, HB - d_qk), BF16)], 1)
    wuq = jnp.pad(w["w_uq"].reshape(Q_LORA, N_HEADS, d_qk), ((0, 0), (0, 0), (0, HB - d_qk))).reshape(Q_LORA, D_HEADS)
    wukv = w["w_ukv"]
    ga = _pad_heads(w["attn_out_g"])
    gfin = w["final_norm_g"].reshape(1, D_MODEL)
    rc, rs1, rs2 = _rope_tables(L)
    head = jnp.concatenate([jnp.zeros((DEAD, D_MODEL), F32), w["meta_tokens"]], 0)

    h0, n, zag, u0, cq, ckv, qn, kvn, q, kv, kr, kvt = _fwd_in(x, head, w["mix_norm_g"], win, w["q_norm_g"], wuq,
                                                                w["kv_norm_g"], wukv, rc, rs1, rs2)
    u1, mixa = _fwd_conv(u0, w["conv_w"], w["conv_b"], w["conv_ln_g"], w["conv_ln_b"], w["conv_out_g"])
    if late_shards is None:
        o, lse, _ = _attn_fwd(q, kv, kvt, kr)
        wout_n, wup, wdown = w["w_out"], _col_shards(w["w_ffn_up"]), w["w_ffn_down"]
    else:
        o, lse, late = _attn_fwd(q, kv, kvt, kr, [late_shards[k] for k in LATE])
        wout_n, wup, wdown = _full_weight("w_out", late[0]), late[1], _full_weight("w_ffn_down", late[2])
    wout = jnp.concatenate([wout_n[:D_CONV], jnp.pad(wout_n[D_CONV:].reshape(N_HEADS, V_HEAD, D_MODEL),
                                                     ((0, 0), (HB - V_HEAD, 0), (0, 0))).reshape(D_HEADS, D_MODEL)], 0)
    mix, h1, n2 = _fwd_mix(h0, mixa, o, ga, wout, w["ffn_norm_g"])
    up0 = _mm(n2, wup, NN, BF16, tl, FF_TILE, D_MODEL, "ffn_up", b_slabs=True)
    dh2, loss, g_fin, up, act = _fwd_ffn_loss(up0, w["ffn_conv_w"], w["ffn_conv_b"], wdown, h1, target, gfin)

    dup, g_fcb = _bwd_ffn_act(dh2, up, wdown)
    dn2, dup0, g_fcw_tiles = _bwd_ffn_conv_up(dup, up0, w["ffn_conv_w"], wup, tl)
    g_fcw = jnp.sum(g_fcw_tiles, 0)
    g_wup = _mm(n2, dup0, TN, BF16, D_MODEL, FF_TILE, tl, "ffn_up_dw", by_col_tile=True)
    g_wdown = _mm(act, dh2, TN, BF16, D_FF // 2, D_MODEL, tl, "ffn_down_dw").reshape(N_CHIPS, -1, D_MODEL)
    ffn = [g_wup, g_wdown] if reduce_first else []
    (dh1, dh1m, do, delta, du1, g_gffn, g_ga, g_og, g_lg, g_lb), theirs = _bwd_mix(
        dh2, dn2, h1, w["ffn_norm_g"], wout, o, ga, u1, w["conv_ln_g"], w["conv_ln_b"], w["conv_out_g"], ffn)
    g_wout = _mm(mix, dh1m, TN, BF16, D_MIX // 2, D_MODEL, tl, "out_dw")
    g_wout = jnp.concatenate([g_wout[:D_CONV], g_wout[D_CONV:].reshape(N_HEADS, HB, D_MODEL)[:, HB - V_HEAD:]
                              .reshape(N_HEADS * V_HEAD, D_MODEL)], 0).reshape(N_CHIPS, -1, D_MODEL)
    pair = ()
    if reduce_first:
        theirs += _sibling_exchange([g_wout], "grad_sibling_exchange_out")
        pair = _add_pairs(ffn + [g_wout], theirs, "first")
    dq, dkv, dkr, got = _attn_bwd(q, kv, kr, do, lse, delta, pair)
    halves = _add_all_chips(pair, got, "first") if reduce_first else ()
    (dzag, g_cw, g_cb), both = _bwd_conv(du1, u0, w["conv_w"], zag, halves)
    if reduce_first:
        g_wup, g_wdown, g_wout = _totals(both)
    dz, dqr, gx, dfirst, g_gq, g_gkv, g_gmix = _bwd_in(dzag, dq, dkv, dkr, cq, ckv, w["q_norm_g"], w["kv_norm_g"],
                                                      wuq, wukv, win, rc, rs1, rs2, h0, w["mix_norm_g"], dh1)
    g_win = _mm(n, dz, TN, BF16, D_MODEL, D_ZP // 2, tl, "in_dw")
    g_wuq = _mm(qn, dqr, TN, BF16, Q_LORA, D_HEADS, tl, "uq_dw")
    g_wukv = _mm(kvn, dkv, TN, BF16, KV_LORA, D_HEADS // N_CHIPS, tl, "ukv_dw", by_col_tile=True)

    grads = {
        "w_in": _col_shards(jnp.concatenate([g_win[:, :kr0], g_win[:, kr0 + QK_NOPE:kr0 + d_qk]], 1)),
        "w_uq": _col_shards(g_wuq.reshape(Q_LORA, N_HEADS, HB)[:, :, :d_qk].reshape(Q_LORA, N_HEADS * d_qk)),
        "w_ukv": g_wukv,
        "w_out": g_wout,
        "w_ffn_up": g_wup, "w_ffn_down": g_wdown, "conv_w": g_cw, "ffn_conv_w": g_fcw,
        "meta_tokens": dfirst[DEAD:], "mix_norm_g": g_gmix, "q_norm_g": g_gq, "kv_norm_g": g_gkv, "conv_b": g_cb,
        "conv_ln_g": g_lg, "conv_ln_b": g_lb, "conv_out_g": g_og, "attn_out_g": _unpad_heads(g_ga),
        "ffn_norm_g": g_gffn, "ffn_conv_b": g_fcb, "final_norm_g": g_fin,
    }
    return loss, gx, grads


ROW_SHARDED = ("w_out", "w_ffn_down")


TRANSPOSED = ("w_in", "w_uq")
REDUCED_FIRST = ("w_ffn_up", "w_ffn_down", "w_out")
LATE = ("w_out", "w_ffn_up", "w_ffn_down")


def _full_weight(name, by_chip):
    return by_chip.reshape(-1, by_chip.shape[-1]) if name in ROW_SHARDED else _from_col_shards(by_chip)


def kernel(x, meta_tokens, mix_norm_g, w_in, q_norm_g, w_uq, kv_norm_g, w_ukv, conv_w, conv_b, conv_ln_g, conv_ln_b, conv_out_g, attn_out_g, w_out, ffn_norm_g, w_ffn_up, ffn_conv_w, ffn_conv_b, w_ffn_down, final_norm_g, loss_target, m_meta_tokens, m_mix_norm_g, m_w_in, m_q_norm_g, m_w_uq, m_kv_norm_g, m_w_ukv, m_conv_w, m_conv_b, m_conv_ln_g, m_conv_ln_b, m_conv_out_g, m_attn_out_g, m_w_out, m_ffn_norm_g, m_w_ffn_up, m_ffn_conv_w, m_ffn_conv_b, m_w_ffn_down, m_final_norm_g, v_meta_tokens, v_mix_norm_g, v_w_in, v_q_norm_g, v_w_uq, v_kv_norm_g, v_w_ukv, v_conv_w, v_conv_b, v_conv_ln_g, v_conv_ln_b, v_conv_out_g, v_attn_out_g, v_w_out, v_ffn_norm_g, v_w_ffn_up, v_ffn_conv_w, v_ffn_conv_b, v_w_ffn_down, v_final_norm_g):
    local = dict(meta_tokens=meta_tokens, mix_norm_g=mix_norm_g, w_in=w_in[0], q_norm_g=q_norm_g, w_uq=w_uq[0],
                 kv_norm_g=kv_norm_g, w_ukv=w_ukv[0], conv_w=conv_w[0], conv_b=conv_b, conv_ln_g=conv_ln_g,
                 conv_ln_b=conv_ln_b, conv_out_g=conv_out_g, attn_out_g=attn_out_g, w_out=w_out[0],
                 ffn_norm_g=ffn_norm_g, w_ffn_up=w_ffn_up[0], ffn_conv_w=ffn_conv_w[0], ffn_conv_b=ffn_conv_b,
                 w_ffn_down=w_ffn_down[0], final_norm_g=final_norm_g.reshape(1, D_MODEL))
    ms = dict(zip(WEIGHTS, (m_meta_tokens, m_mix_norm_g, m_w_in, m_q_norm_g, m_w_uq, m_kv_norm_g, m_w_ukv, m_conv_w,
                            m_conv_b, m_conv_ln_g, m_conv_ln_b, m_conv_out_g, m_attn_out_g, m_w_out, m_ffn_norm_g,
                            m_w_ffn_up, m_ffn_conv_w, m_ffn_conv_b, m_w_ffn_down, m_final_norm_g)))
    vs = dict(zip(WEIGHTS, (v_meta_tokens, v_mix_norm_g, v_w_in, v_q_norm_g, v_w_uq, v_kv_norm_g, v_w_ukv, v_conv_w,
                            v_conv_b, v_conv_ln_g, v_conv_ln_b, v_conv_out_g, v_attn_out_g, v_w_out, v_ffn_norm_g,
                            v_w_ffn_up, v_ffn_conv_w, v_ffn_conv_b, v_w_ffn_down, v_final_norm_g)))

    small_flat = jnp.concatenate([local[k].reshape(-1) for k in SMALL_SHARDED]).reshape(-1, LANES)
    early = [k for k in BIG if k not in LATE]
    gathered = _all_gather([local[k].astype(BF16) for k in early] + [small_flat])
    full = {k: v for k, v in local.items() if k not in LATE}
    for name, g in zip(early, gathered[:len(early)]):
        full[name] = _full_weight(name, g)
    small = gathered[-1].reshape(N_CHIPS, -1)
    at = 0
    for name in SMALL_SHARDED:
        r, c = local[name].shape
        full[name] = _from_col_shards(small[:, at:at + r * c].reshape(N_CHIPS, r, c))
        at += r * c

    loss_row, grad_x, grads = _local_step(x[0], loss_target[0], full, {k: local[k].astype(BF16) for k in LATE},
                                          reduce_first=True)

    rest_big = [k for k in BIG if k not in REDUCED_FIRST]
    rep = jnp.concatenate([grads[k].reshape(-1) for k in REPLICATED] + [loss_row.reshape(-1)]).reshape(1, -1, LANES)
    small_pieces = [_lane_rows(_col_shards(grads[k])) for k in SMALL_SHARDED]
    small_pieces.append(jnp.broadcast_to(rep, (N_CHIPS,) + rep.shape[1:]))
    small_rows = sum(p.shape[1] for p in small_pieces)
    small_pack = _pad_rows_to(jnp.concatenate(small_pieces, 1), -(-small_rows // 32) * 32)
    pair = _reduce_begin([grads[k] for k in rest_big] + [small_pack], "rest")

    total = {k: grads[k] for k in REDUCED_FIRST}
    delta, new_m, new_v = {}, {}, {}
    shape2 = lambda a, name: a.reshape(local[name].shape)
    turn = lambda a, name: a.T if name in TRANSPOSED else a

    def update(name, parts=()):
        outs, got = _adamw_big(turn(local[name], name), turn(total[name], name), turn(shape2(ms[name], name), name),
                               turn(shape2(vs[name], name), name), "adamw_" + name, parts)
        total[name], delta[name], new_m[name], new_v[name] = (turn(o, name) for o in outs)
        return got

    *rest_tot, small_tot = _reduce_end(pair, update(REDUCED_FIRST[0], pair), "rest")
    total.update(zip(rest_big, rest_tot))
    flat = small_tot.reshape(-1)
    at = 0
    for name in SMALL_SHARDED + REPLICATED:
        shape = local[name].shape
        size = shape[0] * shape[1]
        total[name] = flat[at:at + size].reshape(shape)
        at += -(-size // LANES) * LANES if name in SMALL_SHARDED else size
    loss = flat[at]

    for name in BIG:
        if name != REDUCED_FIRST[0]:
            update(name)
    rest = SMALL_SHARDED + REPLICATED
    ds, nms, nvs = _adamw_small([local[k] for k in rest], [total[k] for k in rest],
                                [shape2(ms[k], k) for k in rest], [shape2(vs[k], k) for k in rest])
    for k, d, nm, nv in zip(rest, ds, nms, nvs):
        delta[k], new_m[k], new_v[k] = d, nm, nv

    out_shape = dict(zip(WEIGHTS, (meta_tokens, mix_norm_g, w_in, q_norm_g, w_uq, kv_norm_g, w_ukv, conv_w, conv_b,
                                   conv_ln_g, conv_ln_b, conv_out_g, attn_out_g, w_out, ffn_norm_g, w_ffn_up,
                                   ffn_conv_w, ffn_conv_b, w_ffn_down, final_norm_g)))
    outs = [loss, grad_x[None]]
    for group in (total, delta, new_m, new_v):
        outs += [group[k].reshape(out_shape[k].shape) for k in WEIGHTS]
    return tuple(outs)
```

```python
import functools

import jax
import jax.numpy as jnp
from jax import lax
from jax.experimental import pallas as pl
from jax.experimental.pallas import tpu as pltpu

F32 = jnp.float32
BF16 = jnp.bfloat16

D_MODEL = 1024
D_CONV = 512
CONV_WIDTH = 31
N_HEADS = 8
QK_NOPE = 64
QK_ROPE = 32
V_HEAD = 64
Q_LORA = 384
KV_LORA = 256
D_FF = 2816
FFN_CONV_WIDTH = 3
CHUNK_SHIFT = 6
N_META = 16
ROPE_THETA = 10000.0
EPS = 1e-6
NEG = -1e30
ADAM_LR = 0.001
ADAM_B1 = 0.9
ADAM_B2 = 0.999
ADAM_EPS = 1e-08
ADAM_WD = 0.01
ADAM_STEP = 10

LANES = 128
SUBLANES = 8
HB = LANES
D_HEADS = N_HEADS * HB
TM = 256
DEAD = TM - N_META
D_AG = 2 * D_CONV
D_ZP = D_AG + Q_LORA + KV_LORA + HB
D_MIX = D_CONV + D_HEADS
LN2 = 0.6931471805599453
Q_SCALE = (QK_NOPE + QK_ROPE) ** -0.5 / LN2
HALO_CONV = 32
HALO_FFN = 16
FF_CHUNK = 256
FF_MXU_CHUNK = 256
FF_TILE = D_FF // 2
VMEM_LIMIT = 56 * 1024 * 1024
ADAMW_MAX_STEPS = 32
N_CHIPS = 4
HEAD_GROUP = 4
N_GROUPS = N_HEADS // HEAD_GROUP
MESH =pl.DeviceIdType.MESH


def _params(n_grid):
    return pltpu.CompilerParams(dimension_semantics=("arbitrary",) * n_grid, vmem_limit_bytes=VMEM_LIMIT)


def _rows(tm, c, off=0):
    return pl.BlockSpec((tm, c), lambda i: (i, off))


def _full(shape):
    return pl.BlockSpec(shape, lambda i: (0,) * len(shape))


def _prev(hb, c, tm, off=0):
    return pl.BlockSpec((hb, c), lambda i: (jnp.maximum(i * (tm // hb) - 1, 0), off))


def _next(hb, c, tm, nblk, off=0):
    return pl.BlockSpec((hb, c), lambda i: (jnp.minimum((i + 1) * (tm // hb), nblk - 1), off))


def _sds(shape, dtype):
    return jax.ShapeDtypeStruct(shape, dtype)


def _rms_r(x, n):
    return lax.rsqrt(jnp.sum(x * x, -1, keepdims=True) * (1.0 / n) + EPS)


def _rms_bwd(dy, x, r, g, n):
    gd = dy * g
    dx = r * gd - x * (r * r * r) * (jnp.sum(x * gd, -1, keepdims=True) * (1.0 / n))
    return dx, jnp.sum(dy * x * r, 0, keepdims=True)


def _dot(a, b, dims):
    return lax.dot_general(a, b, (dims, ((), ())), preferred_element_type=F32)


NN = ((1,), (0,))
NT = ((1,), (1,))
TN = ((0,), (0,))


def _rope(x, c, s1, s2):
    n = x.shape[-1]
    return x * c + pltpu.roll(x, n - QK_ROPE // 2, 1) * s1 + pltpu.roll(x, QK_ROPE // 2, 1) * s2


def _rope_bwd(g, c, s1, s2):
    n = g.shape[-1]
    return g * c + pltpu.roll(g * s1, QK_ROPE // 2, 1) + pltpu.roll(g * s2, n - QK_ROPE // 2, 1)


def _row_ids(tm, cols=1):
    return pl.program_id(0) * tm + lax.broadcasted_iota(jnp.int32, (tm, cols), 0)


def _mm(a, b, dims, out_dtype, tm, tn, tk, name, by_col_tile=False, b_slabs=False):
    if dims == TN:
        (kk, m), (_, n) = a.shape, b.shape
        a_spec = pl.BlockSpec((tk, tm), lambda i, j, k: (k, i))
    else:
        m, kk = a.shape
        a_spec = pl.BlockSpec((tm, tk), lambda i, j, k: (i, k))
    if dims == NT and b_slabs:
        n = b.shape[1]
        assert b.shape[2] == tk and kk == b.shape[0] * tk, (name, b.shape)
        b_spec = pl.BlockSpec((None, tn, tk), lambda i, j, k: (k, j, 0))
    elif dims == NT:
        n = b.shape[0]
        b_spec = pl.BlockSpec((tn, tk), lambda i, j, k: (j, k))
    elif b_slabs:
        n = b.shape[0] * b.shape[2]
        assert b.shape[2] == tn and kk == b.shape[1], (name, b.shape)
        b_spec = pl.BlockSpec((None, tk, tn), lambda i, j, k: (j, k, 0))
    else:
        n = b.shape[1]
        b_spec = pl.BlockSpec((tk, tn), lambda i, j, k: (k, j))
    assert m % tm == 0 and n % tn == 0 and kk % tk == 0, (name, a.shape, b.shape, tm, tn, tk)
    nk = kk // tk

    def body(a_ref, b_ref, o_ref, acc_ref):
        k = pl.program_id(2)

        @pl.when(k == 0)
        def _():
            acc_ref[...] = jnp.zeros_like(acc_ref)

        acc_ref[...] += _dot(a_ref[...].astype(BF16), b_ref[...].astype(BF16), dims)

        @pl.when(k == nk - 1)
        def _():
            o_ref[...] = acc_ref[...].astype(out_dtype)

    if by_col_tile:
        out_spec, out_shape = pl.BlockSpec((None, tm, tn), lambda i, j, k: (j, i, 0)), (n // tn, m, tn)
    else:
        out_spec, out_shape = pl.BlockSpec((tm, tn), lambda i, j, k: (i, j)), (m, n)
    return pl.pallas_call(
        body, name=name, grid=(m // tm, n // tn, nk), in_specs=[a_spec, b_spec], out_specs=out_spec,
        out_shape=_sds(out_shape, out_dtype), scratch_shapes=[pltpu.VMEM((tm, tn), F32)],
        compiler_params=_params(3))(a, b)


def _fwd_in(x, head, gmix, win, gq, wuq, gkv, wukv, rc, rs1, rs2):
    L = TM + x.shape[0]

    def body(x_ref, head_ref, gmix_ref, win_ref, gq_ref, wuq_ref, gkv_ref, wukv_ref, c_ref, s1_ref, s2_ref,
             h0_ref, n_ref, zag_ref, u0_ref, cq_ref, ckv_ref, qn_ref, kvn_ref, q_ref, kv_ref, kr_ref, kvt_ref):
        h = jnp.where(pl.program_id(0) == 0, head_ref[...], x_ref[...])
        h0_ref[...] = h
        n = (h * _rms_r(h, D_MODEL) * gmix_ref[...]).astype(BF16)
        n_ref[...] = n
        z = _dot(n, win_ref[...], NN)
        a, gate = z[:, :D_CONV], z[:, D_CONV:D_AG]
        zag_ref[...] = z[:, :D_AG].astype(BF16)
        u0_ref[...] = a * jax.nn.sigmoid(gate)
        cq = z[:, D_AG:D_AG + Q_LORA]
        ckv = z[:, D_AG + Q_LORA:D_AG + Q_LORA + KV_LORA]
        krp = z[:, D_AG + Q_LORA + KV_LORA:]
        cq_ref[...] = cq
        ckv_ref[...] = ckv
        qn = (cq * _rms_r(cq, Q_LORA) * gq_ref[...]).astype(BF16)
        qn_ref[...] = qn
        kvn = (ckv * _rms_r(ckv, KV_LORA) * gkv_ref[...]).astype(BF16)
        kvn_ref[...] = kvn
        c, s1, s2 = c_ref[...], s1_ref[...], s2_ref[...]
        q = _dot(qn, wuq_ref[...], NN)
        q = _rope(q, jnp.tile(c, (1, N_HEADS)), jnp.tile(s1, (1, N_HEADS)), jnp.tile(s2, (1, N_HEADS)))
        q_ref[...] = (q * Q_SCALE).astype(BF16)
        kv = _dot(kvn, wukv_ref[...], NN)
        kv_ref[...] = kv.astype(BF16)
        kvt_ref[...] = kv.T.astype(BF16)
        kr_ref[...] = _rope(krp, c, s1, s2).astype(BF16)

    outs = [(D_MODEL, F32), (D_MODEL, BF16), (D_AG, BF16), (D_CONV, F32), (Q_LORA, F32), (KV_LORA, F32), (Q_LORA, BF16),
            (KV_LORA, BF16), (D_HEADS, BF16), (D_HEADS, BF16), (HB, BF16)]
    return pl.pallas_call(
        body, name="fwd_in", grid=(L // TM,),
        in_specs=[pl.BlockSpec((TM, D_MODEL), lambda i: (jnp.maximum(i - 1, 0), 0)), _full(head.shape),
                  _full(gmix.shape), _full(win.shape), _full(gq.shape), _full(wuq.shape),
                  _full(gkv.shape), _full(wukv.shape), _rows(TM, HB), _rows(TM, HB), _rows(TM, HB)],
        out_specs=[_rows(TM, c) for c, _ in outs] + [pl.BlockSpec((D_HEADS, TM), lambda i: (0, i))],
        out_shape=[_sds((L, c), d) for c, d in outs] + [_sds((D_HEADS, L), BF16)],
        compiler_params=_params(1))(x, head, gmix, win, gq, wuq, gkv, wukv, rc, rs1, rs2)


def _conv_taps(xx, w_ref, halo, tm, flip):
    kw = w_ref.shape[0]
    acc, rolls = None, {}
    for k in range(kw):
        term = w_ref[k:k + 1, :] * _shifted_rows(xx, kw - 1 - k, halo, tm, flip, rolls)
        acc = term if acc is None else acc + term
    return acc


def _shifted_rows(xx, d, halo, tm, flip, rolls):
    a, b = divmod(d, SUBLANES)
    if b not in rolls:
        rolls[b] = xx if b == 0 else pltpu.roll(xx, (xx.shape[0] - b) if flip else b, 0)
    start = SUBLANES * a if flip else halo - SUBLANES * a
    return rolls[b][start:start + tm]


def _ln_silu(u1, lg, lb):
    mu = jnp.mean(u1, -1, keepdims=True)
    xc = u1 - mu
    rs = lax.rsqrt(jnp.mean(xc * xc, -1, keepdims=True) + EPS)
    xh = xc * rs
    u2 = xh * lg + lb
    sg = jax.nn.sigmoid(u2)
    return rs, xh, u2, sg, u2 * sg


def _fwd_conv(u0, cw, cb, lg, lb, og):
    L = u0.shape[0]

    def body(u0_ref, u0p_ref, cw_ref, cb_ref, lg_ref, lb_ref, og_ref, u1_ref, mixa_ref):
        halo = jnp.where(pl.program_id(0) > 0, u0p_ref[...], 0.0)
        xx = jnp.concatenate([halo, u0_ref[...]], 0)
        u1 = _conv_taps(xx, cw_ref, HALO_CONV, TM, False) + cb_ref[...]
        u1_ref[...] = u1
        u = _ln_silu(u1, lg_ref[...], lb_ref[...])[4]
        mixa_ref[...] = (u * _rms_r(u, D_CONV) * og_ref[...]).astype(BF16)

    return pl.pallas_call(
        body, name="fwd_conv", grid=(L // TM,),
        in_specs=[_rows(TM, D_CONV), _prev(HALO_CONV, D_CONV, TM), _full(cw.shape), _full(cb.shape),
                  _full(lg.shape), _full(lb.shape), _full(og.shape)],
        out_specs=[_rows(TM, D_CONV), _rows(TM, D_CONV)],
        out_shape=[_sds((L, D_CONV), F32), _sds((L, D_CONV), BF16)],
        compiler_params=_params(1))(u0, u0, cw, cb, lg, lb, og)


def _visible(i, j, t):
    row = i * t + lax.broadcasted_iota(jnp.int32, (t, t), 0)
    col = j * t + lax.broadcasted_iota(jnp.int32, (t, t), 1)
    return (lax.shift_right_logical(col, CHUNK_SHIFT) <= lax.shift_right_logical(row, CHUNK_SHIFT)) & (col >= DEAD)


def _visible_t(i, j, t):
    key = j * t + lax.broadcasted_iota(jnp.int32, (t, t), 0)
    query = i * t + lax.broadcasted_iota(jnp.int32, (t, t), 1)
    return (lax.shift_right_logical(key, CHUNK_SHIFT) <= lax.shift_right_logical(query, CHUNK_SHIFT)) & (key >= DEAD)


def _stat_lane(h):
    return (h // HEAD_GROUP) * LANES + h % HEAD_GROUP


def _scatter_stats(cols, t):
    lane = lax.broadcasted_iota(jnp.int32, (t, N_GROUPS * LANES), 1)
    out = jnp.zeros((t, N_GROUPS * LANES), F32)
    for h, col in enumerate(cols):
        out = jnp.where(lane == _stat_lane(h), col, out)
    return out


def _resident(shape, index_map):
    return pl.BlockSpec(shape, index_map, pipeline_mode=pl.Buffered(1))


def _attn_fwd(q, kv, kvt, kr, shards=()):
    L = q.shape[0]
    t = TM
    nq = L // t
    n = len(shards)
    pass_step = (5 * nq) // 6

    def body(q_ref, kv_ref, kvt_ref, kr_ref, *refs):
        gather_refs = refs[:n] + refs[n + 2:2 * n + 2] + refs[2 * n + 6:]
        o_ref, lse_ref = refs[n:n + 2]
        qt_scr, m_scr, l_scr, acc_scr = refs[2 * n + 2:2 * n + 6]
        i = pl.program_id(0)
        if n:
            @pl.when(i == 0)
            def _():
                for cp in _gather_copies(gather_refs[:n], gather_refs[n:2 * n], *gather_refs[2 * n:])[0]:
                    cp().start()

        lane = lax.broadcasted_iota(jnp.int32, (t, HB), 1)
        heads = range(N_HEADS)
        cols = [slice(h * HB, (h + 1) * HB) for h in heads]
        for h in heads:
            qt_scr[cols[h], :] = q_ref[:, cols[h]].astype(F32).T.astype(BF16)
        m_scr[...] = jnp.full_like(m_scr, NEG)
        l_scr[...] = jnp.zeros_like(l_scr)
        acc_scr[...] = jnp.zeros_like(acc_scr)

        def tile(keys, vis, whole=True):
            krj = kr_ref[keys, :]
            kvj = [kv_ref[keys, cols[h]] for h in heads]
            lane_k = lane[:krj.shape[0]]
            s = [_dot(jnp.where(lane_k < QK_NOPE, kvj[h], krj), qt_scr[cols[h], :], NN) for h in heads]
            p, alpha = [], []
            for h in heads:
                sh = s[h] if vis is None else jnp.where(vis, s[h], NEG)
                m_prev = m_scr[h:h + 1, :]
                m_new = jnp.maximum(m_prev, jnp.max(sh, 0, keepdims=True))
                a = jnp.exp2(m_prev - m_new)
                ph = jnp.exp2(sh - m_new)
                l_scr[h:h + 1, :] = a * l_scr[h:h + 1, :] + jnp.sum(ph, 0, keepdims=True)
                m_scr[h:h + 1, :] = m_new
                p.append(ph.astype(BF16))
                alpha.append(a)
            for h in heads:
                pv = _dot(kvt_ref[cols[h], keys], p[h], NN) if whole else _dot(kvj[h], p[h], TN)
                acc_scr[cols[h], :] = alpha[h] * acc_scr[cols[h], :] + pv

        tile(pl.ds(pl.multiple_of(i * t, t), t), _visible_t(i, i, t))

        @pl.when(i > 0)
        def _():
            tile(pl.ds(DEAD, N_META), None, whole=False)

        def unmasked(j, carry):
            tile(pl.ds(pl.multiple_of(j * t, t), t), None)
            return carry

        lax.fori_loop(1, i, unmasked, 0)
        lse_ref[...] = jnp.zeros_like(lse_ref)
        for h in heads:
            l = l_scr[h:h + 1, :]
            o_ref[:, cols[h]] = jnp.where(lane >= QK_NOPE, (acc_scr[cols[h], :] / l).T, 0.0).astype(BF16)
            lse_ref[h // HEAD_GROUP, h % HEAD_GROUP:h % HEAD_GROUP + 1, :] = m_scr[h:h + 1, :] + jnp.log2(l)
        if n:
            @pl.when(i == pass_step)
            def _():
                _, arrivals, forwards, _ = _gather_copies(gather_refs[:n], gather_refs[n:2 * n], *gather_refs[2 * n:])
                for landed, onward in zip(arrivals, forwards):
                    landed().wait_recv()
                    onward().start()

            @pl.when(i == nq - 1)
            def _():
                sends, _, forwards, finals = _gather_copies(gather_refs[:n], gather_refs[n:2 * n], *gather_refs[2 * n:])
                for cp in finals:
                    cp().wait_recv()
                for cp in sends + forwards:
                    cp().wait_send()

    any_spec = pl.BlockSpec(memory_space=pl.ANY)
    outs = pl.pallas_call(
        body, name="attn_fwd", grid=(nq,),
        in_specs=[_rows(t, D_HEADS), _resident((L, D_HEADS), lambda i: (0, 0)),
                  _resident((D_HEADS, L), lambda i: (0, 0)), _resident((L, HB), lambda i: (0, 0))] + [any_spec] * n,
        out_specs=[_rows(t, D_HEADS), pl.BlockSpec((N_GROUPS, SUBLANES, t), lambda i: (0, 0, i))] + [any_spec] * n,
        out_shape=[_sds((L, D_HEADS), BF16), _sds((N_GROUPS, SUBLANES, L), F32)] + _gather_out_shapes(shards),
        scratch_shapes=[pltpu.VMEM((D_HEADS, t), BF16), pltpu.VMEM((N_HEADS, t), F32), pltpu.VMEM((N_HEADS, t), F32),
                        pltpu.VMEM((D_HEADS, t), F32)] + (_gather_semaphores(n) if n else []),
        compiler_params=_params(1))(q, kv, kvt, kr, *shards)
    return outs[0], outs[1], _gathered(outs[2:], shards)


def _fwd_mix(h0, mixa, o, ga, wout, gffn):
    L = h0.shape[0]

    def body(h0_ref, mixa_ref, o_ref, ga_ref, wout_ref, gffn_ref, mix_ref, h1_ref, n2_ref):
        of = o_ref[...].astype(F32)
        mixb = (of * _rms_r(of, N_HEADS * V_HEAD) * ga_ref[...]).astype(BF16)
        mix = jnp.concatenate([mixa_ref[...], mixb], 1)
        mix_ref[...] = mix
        mo = jnp.where(_row_ids(TM) >= DEAD, _dot(mix, wout_ref[...], NN), 0.0)
        h1 = h0_ref[...] + mo
        h1_ref[...] = h1
        n2_ref[...] = (h1 * _rms_r(h1, D_MODEL) * gffn_ref[...]).astype(BF16)

    return pl.pallas_call(
        body, name="fwd_mix", grid=(L // TM,),
        in_specs=[_rows(TM, D_MODEL), _rows(TM, D_CONV), _rows(TM, D_HEADS), _full(ga.shape), _full(wout.shape),
                  _full(gffn.shape)],
        out_specs=[_rows(TM, D_MIX), _rows(TM, D_MODEL), _rows(TM, D_MODEL)],
        out_shape=[_sds((L, D_MIX), BF16), _sds((L, D_MODEL), F32), _sds((L, D_MODEL), BF16)],
        compiler_params=_params(1))(h0, mixa, o, ga, wout, gffn)


def _ffn_act_chunk(c, upg_ref, upv_ref, hg, hv, fcw_ref, fcb_ref):
    cs = slice(c * FF_CHUNK, (c + 1) * FF_CHUNK)
    out = []
    for part, (up_ref, halo) in enumerate(((upg_ref, hg), (upv_ref, hv))):
        xx = jnp.concatenate([halo[:, cs], up_ref[:, cs].astype(F32)], 0)
        ws = slice(part * D_FF + c * FF_CHUNK, part * D_FF + (c + 1) * FF_CHUNK)
        y = (fcw_ref[0:1, ws] * pltpu.roll(xx, 2, 0)[HALO_FFN:] + fcw_ref[1:2, ws] * pltpu.roll(xx, 1, 0)[HALO_FFN:]
             + fcw_ref[2:3, ws] * xx[HALO_FFN:] + fcb_ref[:, ws])
        out.append(y)
    return out


def _ffn_in_specs(L):
    return [_rows(TM, D_FF, 0), _rows(TM, D_FF, 1), _prev(HALO_FFN, D_FF, TM, 0), _prev(HALO_FFN, D_FF, TM, 1)]


def _ffn_halos(hg_ref, hv_ref):
    first = pl.program_id(0) == 0
    return (jnp.where(first, 0.0, hg_ref[...].astype(F32)), jnp.where(first, 0.0, hv_ref[...].astype(F32)))


def _fwd_ffn_loss(up0, fcw, fcb, wdown, h1, target, gfin):
    L = h1.shape[0]

    def body(upg_ref, upv_ref, hg_ref, hv_ref, fcw_ref, fcb_ref, wd_ref, h1_ref, t_ref, gf_ref,
             dh2_ref, loss_ref, dgf_ref, up_ref, act_ref):
        i = pl.program_id(0)
        hg, hv = _ffn_halos(hg_ref, hv_ref)
        for c in range(D_FF // FF_CHUNK):
            cs = slice(c * FF_CHUNK, (c + 1) * FF_CHUNK)
            g, val = _ffn_act_chunk(c, upg_ref, upv_ref, hg, hv, fcw_ref, fcb_ref)
            up_ref[:, cs] = g.astype(BF16)
            up_ref[:, D_FF + c * FF_CHUNK:D_FF + (c + 1) * FF_CHUNK] = val.astype(BF16)
            act_ref[:, cs] = (g * jax.nn.sigmoid(g) * val).astype(BF16)
        h2 = h1_ref[...] + _dot(act_ref[...], wd_ref[...], NN)
        r = _rms_r(h2, D_MODEL)
        gf = gf_ref[...]
        err = jnp.where(i > 0, h2 * r * gf - t_ref[...], 0.0)
        dy = err * (1.0 / D_MODEL)
        dh2, dgf = _rms_bwd(dy, h2, r, gf, D_MODEL)
        dh2_ref[...] = dh2

        @pl.when(i == 0)
        def _():
            loss_ref[...] = jnp.zeros_like(loss_ref)
            dgf_ref[...] = jnp.zeros_like(dgf_ref)

        loss_ref[...] += jnp.sum(err * err) * (0.5 / D_MODEL)
        dgf_ref[...] += dgf

    return pl.pallas_call(
        body, name="fwd_ffn_loss", grid=(L // TM,),
        in_specs=_ffn_in_specs(L) + [_full(fcw.shape), _full(fcb.shape), _full(wdown.shape), _rows(TM, D_MODEL),
                                     pl.BlockSpec((TM, D_MODEL), lambda i: (jnp.maximum(i - 1, 0), 0)),
                                     _full(gfin.shape)],
        out_specs=[_rows(TM, D_MODEL), _full((1, LANES)), _full((1, D_MODEL)), _rows(TM, 2 * D_FF), _rows(TM, D_FF)],
        out_shape=[_sds((L, D_MODEL), F32), _sds((1, LANES), F32), _sds((1, D_MODEL), F32),
                   _sds((L, 2 * D_FF), BF16), _sds((L, D_FF), BF16)],
        compiler_params=_params(1))(up0, up0, up0, up0, fcw, fcb, wdown, h1, target, gfin)


def _bwd_ffn_act(dh2, up, wdown):
    L = dh2.shape[0]

    def body(dh2_ref, upg_ref, upv_ref, wd_ref, dup_ref, dfcb_ref):
        da = _dot(dh2_ref[...].astype(BF16), wd_ref[...], NT)

        @pl.when(pl.program_id(0) == 0)
        def _():
            dfcb_ref[...] = jnp.zeros_like(dfcb_ref)

        for c in range(D_FF // FF_CHUNK):
            cs = slice(c * FF_CHUNK, (c + 1) * FF_CHUNK)
            vs = slice(D_FF + c * FF_CHUNK, D_FF + (c + 1) * FF_CHUNK)
            g, val = upg_ref[:, cs].astype(F32), upv_ref[:, cs].astype(F32)
            sg = jax.nn.sigmoid(g)
            si = g * sg
            dac = da[:, cs]
            dg = dac * val * (sg * (1.0 + g * (1.0 - sg)))
            dv = dac * si
            dup_ref[:, cs] = dg.astype(BF16)
            dup_ref[:, vs] = dv.astype(BF16)
            dfcb_ref[:, cs] += jnp.sum(dg, 0, keepdims=True)
            dfcb_ref[:, vs] += jnp.sum(dv, 0, keepdims=True)

    return pl.pallas_call(
        body, name="bwd_ffn_act", grid=(L // TM,),
        in_specs=[_rows(TM, D_MODEL), _rows(TM, D_FF, 0), _rows(TM, D_FF, 1), _full(wdown.shape)],
        out_specs=[_rows(TM, 2 * D_FF), _full((1, 2 * D_FF))],
        out_shape=[_sds((L, 2 * D_FF), BF16), _sds((1, 2 * D_FF), F32)],
        compiler_params=_params(1))(dh2, up, up, wdown)


def _bwd_ffn_conv_up(dup, up0, fcw, wup, tl):
    L, C = dup.shape
    tc = FF_TILE
    nt = L // tl
    nhb = L // HALO_FFN
    chunks = [(c0, min(FF_MXU_CHUNK, tc - c0)) for c0 in range(0, tc, FF_MXU_CHUNK)]

    def body(dy_ref, dyn_ref, x_ref, xp_ref, w_ref, wup_ref, dn2_ref, dx_ref, dw_ref, acc_ref):
        i, k = pl.program_id(0), pl.program_id(1)

        @pl.when(k == 0)
        def _():
            acc_ref[...] = jnp.zeros_like(acc_ref)

        last, first = i == nt - 1, i == 0
        for c0, cw in chunks:
            cs = slice(c0, c0 + cw)
            yy = jnp.concatenate([dy_ref[:, cs].astype(F32), jnp.where(last, 0.0, dyn_ref[:, cs].astype(F32))], 0)
            w = w_ref[:, cs]
            dx = (w[0:1] * pltpu.roll(yy, tl + HALO_FFN - 2, 0)[:tl] + w[1:2] * pltpu.roll(yy, tl + HALO_FFN - 1, 0)[:tl]
                  + w[2:3] * yy[:tl]).astype(BF16)
            dx_ref[:, cs] = dx
            acc_ref[...] += _dot(dx, wup_ref[:, cs], NT)
            xx = jnp.concatenate([jnp.where(first, 0.0, xp_ref[:, cs].astype(F32)), x_ref[:, cs].astype(F32)], 0)
            dy = yy[:tl]
            dw_ref[0:1, cs] = jnp.sum(dy * pltpu.roll(xx, 2, 0)[HALO_FFN:], 0, keepdims=True)
            dw_ref[1:2, cs] = jnp.sum(dy * pltpu.roll(xx, 1, 0)[HALO_FFN:], 0, keepdims=True)
            dw_ref[2:3, cs] = jnp.sum(dy * xx[HALO_FFN:], 0, keepdims=True)

        @pl.when(k == C // tc - 1)
        def _():
            dn2_ref[...] = acc_ref[...]

    tile = pl.BlockSpec((tl, tc), lambda i, k: (i, k))
    per = tl // HALO_FFN
    return pl.pallas_call(
        body, name="bwd_ffn_conv_up", grid=(nt, C // tc),
        in_specs=[tile, pl.BlockSpec((HALO_FFN, tc), lambda i, k: (jnp.minimum((i + 1) * per, nhb - 1), k)),
                  tile, pl.BlockSpec((HALO_FFN, tc), lambda i, k: (jnp.maximum(i * per - 1, 0), k)),
                  pl.BlockSpec((FFN_CONV_WIDTH, tc), lambda i, k: (0, k)),
                  pl.BlockSpec((None, D_MODEL, tc), lambda i, k: (k, 0, 0))],
        out_specs=[pl.BlockSpec((tl, D_MODEL), lambda i, k: (i, 0)), tile,
                   pl.BlockSpec((None, FFN_CONV_WIDTH, tc), lambda i, k: (i, 0, k))],
        out_shape=[_sds((L, D_MODEL), F32), _sds((L, C), BF16), _sds((nt, FFN_CONV_WIDTH, C), F32)],
        scratch_shapes=[pltpu.VMEM((tl, D_MODEL), F32)],
        compiler_params=_params(2))(dup, dup, up0, up0, fcw, wup)


def _carrying(core, n_in, n_out, n, copies_fn, steps):
    def body(*refs):
        exchange = (refs[n_in:n_in + n], refs[n_in + n + n_out:n_in + 2 * n + n_out]) + refs[n_in + 2 * n + n_out:]
        if n:
            @pl.when(pl.program_id(0) == 0)
            def _():
                for cp in copies_fn(*exchange):
                    cp().start()

        core(*refs[:n_in], *refs[n_in + n:n_in + n + n_out])
        if n:
            @pl.when(pl.program_id(0) == steps - 1)
            def _():
                for cp in copies_fn(*exchange):
                    cp().wait()

    return body


def _bwd_mix(dh2, dn2, h1, gffn, wout, o, ga, u1, lg, lb, og, parts=()):
    L = h1.shape[0]

    def body(dh2_ref, dn2_ref, h1_ref, gffn_ref, wout_ref, o_ref, ga_ref, u1_ref, lg_ref, lb_ref, og_ref,
             dh1_ref, dh1m_ref, do_ref, delta_ref, du1_ref, dgffn_ref, dga_ref, dog_ref, dlg_ref, dlb_ref):
        h1 = h1_ref[...]
        dn2x, dgffn = _rms_bwd(dn2_ref[...], h1, _rms_r(h1, D_MODEL), gffn_ref[...], D_MODEL)
        dh1 = dh2_ref[...] + dn2x
        dh1_ref[...] = dh1
        dh1m = jnp.where(_row_ids(TM) >= DEAD, dh1, 0.0).astype(BF16)
        dh1m_ref[...] = dh1m
        dmix = _dot(dh1m, wout_ref[...], NT)
        dma, dmb = dmix[:, :D_CONV], dmix[:, D_CONV:]
        of = o_ref[...].astype(F32)
        do, dga = _rms_bwd(dmb, of, _rms_r(of, N_HEADS * V_HEAD), ga_ref[...], N_HEADS * V_HEAD)
        do_ref[...] = do.astype(BF16)
        prod = do * of
        by_lane = _scatter_stats([jnp.sum(prod[:, h * HB:(h + 1) * HB], -1, keepdims=True) for h in range(N_HEADS)], TM)
        by_row = by_lane.T
        for grp in range(N_GROUPS):
            delta_ref[grp] = by_row[grp * LANES:grp * LANES + SUBLANES, :]
        lg = lg_ref[...]
        rs, xh, u2, sg, u = _ln_silu(u1_ref[...], lg, lb_ref[...])
        du, dog = _rms_bwd(dma, u, _rms_r(u, D_CONV), og_ref[...], D_CONV)
        du2 = du * (sg * (1.0 + u2 * (1.0 - sg)))
        dxh = du2 * lg
        du1_ref[...] = rs * (dxh - jnp.mean(dxh, -1, keepdims=True) - xh * jnp.mean(dxh * xh, -1, keepdims=True))

        @pl.when(pl.program_id(0) == 0)
        def _():
            for ref in (dgffn_ref, dga_ref, dog_ref, dlg_ref, dlb_ref):
                ref[...] = jnp.zeros_like(ref)

        dgffn_ref[...] += dgffn
        dga_ref[...] += dga
        dog_ref[...] += dog
        dlg_ref[...] += jnp.sum(du2 * xh, 0, keepdims=True)
        dlb_ref[...] += jnp.sum(du2, 0, keepdims=True)

    n = len(parts)
    any_spec = pl.BlockSpec(memory_space=pl.ANY)
    outs = pl.pallas_call(
        _carrying(body, 11, 10, n, _sibling_exchange_copies, L // TM), name="bwd_mix", grid=(L // TM,),
        in_specs=[_rows(TM, D_MODEL), _rows(TM, D_MODEL), _rows(TM, D_MODEL), _full(gffn.shape), _full(wout.shape),
                  _rows(TM, D_HEADS), _full(ga.shape), _rows(TM, D_CONV), _full(lg.shape), _full(lb.shape),
                  _full(og.shape)] + [any_spec] * n,
        out_specs=[_rows(TM, D_MODEL), _rows(TM, D_MODEL), _rows(TM, D_HEADS),
                   pl.BlockSpec((N_GROUPS, SUBLANES, TM), lambda i: (0, 0, i)),
                   _rows(TM, D_CONV), _full((1, D_MODEL)), _full((1, D_HEADS)), _full((1, D_CONV)),
                   _full((1, D_CONV)), _full((1, D_CONV))] + [any_spec] * n,
        out_shape=[_sds((L, D_MODEL), F32), _sds((L, D_MODEL), BF16), _sds((L, D_HEADS), BF16),
                   _sds((N_GROUPS, SUBLANES, L), F32),
                   _sds((L, D_CONV), F32), _sds((1, D_MODEL), F32), _sds((1, D_HEADS), F32), _sds((1, D_CONV), F32),
                   _sds((1, D_CONV), F32), _sds((1, D_CONV), F32)] + _sibling_exchange_shapes(parts),
        scratch_shapes=_sibling_exchange_semaphores(n) if n else [],
        compiler_params=_params(1))(dh2, dn2, h1, gffn, wout, o, ga, u1, lg, lb, og, *parts)
    return outs[:10], list(outs[10:])


def _attn_bwd(q, kv, kr, do, lse, delta, parts=()):
    L = q.shape[0]
    t = TM
    nt = L // t
    gw = HEAD_GROUP * HB
    n = len(parts)

    def body(q_ref, kv_ref, kr_ref, do_ref, lse_ref, delta_ref, *refs):
        dq_ref, dkv_ref, dkr_ref = refs[n:n + 3]
        dqt_acc, dk_acc, dv_acc, kkt_scr = refs[2 * n + 3:2 * n + 7]
        exchange_refs = (refs[:n], refs[n + 3:2 * n + 3]) + refs[2 * n + 7:]
        g, j = pl.program_id(0), pl.program_id(1)
        if n:
            @pl.when((g == 0) & (j == 0))
            def _():
                for cp in _chip_copies(*exchange_refs)[0]:
                    cp().start()

        lane = lax.broadcasted_iota(jnp.int32, (t, HB), 1)

        @pl.when(j == 0)
        def _():
            dqt_acc[...] = jnp.zeros_like(dqt_acc)

        @pl.when((j == 0) & (g == 0))
        def _():
            dkr_ref[...] = jnp.zeros_like(dkr_ref)

        dk_acc[...] = jnp.zeros_like(dk_acc)
        dv_acc[...] = jnp.zeros_like(dv_acc)
        krj = kr_ref[...]
        heads = range(HEAD_GROUP)
        cols = [slice(h * HB, (h + 1) * HB) for h in heads]
        for hc in cols:
            kkt_scr[hc, :] = jnp.where(lane < QK_NOPE, kv_ref[:, hc], krj).astype(F32).T.astype(BF16)

        def tile(i, vis, whole=True):
            qs = pl.ds(pl.multiple_of(i * t, t), t)
            keys = slice(None) if whole else slice(DEAD, t)
            kvj = [kv_ref[keys, hc] for hc in cols]
            lane_k = lane[:kvj[0].shape[0]]
            kk = [jnp.where(lane_k < QK_NOPE, kvj[h], krj[keys]) for h in heads]
            qi = [q_ref[qs, hc] for hc in cols]
            doi = [do_ref[qs, hc] for hc in cols]
            s = [_dot(kk[h], qi[h].astype(F32).T.astype(BF16), NN) for h in heads]
            dp = [_dot(kvj[h], doi[h].astype(F32).T.astype(BF16), NN) for h in heads]
            p = []
            for h in heads:
                sh = s[h] if vis is None else jnp.where(vis, s[h], NEG)
                p.append(jnp.exp2(sh - lse_ref[h:h + 1, qs]))
            for h in heads:
                dv_acc[keys, cols[h]] += _dot(p[h].astype(BF16), doi[h], NN)
            ds = [(p[h] * (dp[h] - delta_ref[h:h + 1, qs]) * LN2).astype(BF16) for h in heads]
            for h in heads:
                dk_acc[keys, cols[h]] += _dot(ds[h], qi[h], NN)
            for h in heads:
                dqt = _dot(kkt_scr[cols[h], :], ds[h], NN) if whole else _dot(kk[h], ds[h], TN)
                dqt_acc[cols[h], qs] += dqt

        @pl.when(j == 0)
        def _():
            tile(0, _visible_t(0, 0, t)[DEAD:], whole=False)

            def meta_keys(i, carry):
                tile(i, None, whole=False)
                return carry

            lax.fori_loop(1, nt, meta_keys, 0)

        @pl.when(j > 0)
        def _():
            tile(j, _visible_t(j, j, t))

            def unmasked(i, carry):
                tile(i, None)
                return carry

            lax.fori_loop(j + 1, nt, unmasked, 0)

        dkr = jnp.zeros((t, HB), F32)
        for h in range(HEAD_GROUP):
            hc = slice(h * HB, (h + 1) * HB)
            dk = dk_acc[:, hc]
            dkv_ref[:, hc] = jnp.where(lane < QK_NOPE, dk, dv_acc[:, hc]).astype(BF16)
            dkr = dkr + jnp.where(lane >= QK_NOPE, dk, 0.0)
        dkr_ref[pl.ds(pl.multiple_of(j * t, t), t), :] += dkr

        @pl.when(j == nt - 1)
        def _():
            def untranspose(i, carry):
                qs = pl.ds(pl.multiple_of(i * t, t), t)
                dq_ref[qs, :] = (dqt_acc[:, qs].T * Q_SCALE).astype(BF16)
                return carry

            lax.fori_loop(0, nt, untranspose, 0)

        if n:
            @pl.when((g == N_GROUPS - 1) & (j == nt - 1))
            def _():
                sends, arrivals = _chip_copies(*exchange_refs)
                for cp in arrivals:
                    cp().wait_recv()
                for cp in sends:
                    cp().wait_send()

    group = lambda g, j: (0, g)
    stats = _resident((None, SUBLANES, L), lambda g, j: (g, 0, 0))
    any_spec = pl.BlockSpec(memory_space=pl.ANY)
    outs = pl.pallas_call(
        body, name="attn_bwd", grid=(N_GROUPS, nt),
        in_specs=[_resident((L, gw), group), pl.BlockSpec((t, gw), lambda g, j: (j, g)),
                  pl.BlockSpec((t, HB), lambda g, j: (j, 0)), _resident((L, gw), group), stats, stats]
        + [any_spec] * n,
        out_specs=[pl.BlockSpec((L, gw), group), pl.BlockSpec((t, gw), lambda g, j: (j, g)),
                   pl.BlockSpec((L, HB), lambda g, j: (0, 0))] + [any_spec] * n,
        out_shape=[_sds((L, D_HEADS), BF16), _sds((L, D_HEADS), BF16), _sds((L, HB), F32)]
        + [_sds(p.shape, p.dtype) for p in parts],
        scratch_shapes=[pltpu.VMEM((gw, L), F32), pltpu.VMEM((t, gw), F32), pltpu.VMEM((t, gw), F32),
                        pltpu.VMEM((gw, t), BF16)] + (_chip_semaphores(n) if n else []),
        compiler_params=_params(2))(q, kv, kr, do, lse, delta, *parts)
    return outs[0], outs[1], outs[2], list(outs[3:])


def _bwd_conv(du1, u0, cw, zag, halves=()):
    L = du1.shape[0]
    nt = L // TM

    def body(dy_ref, dyn_ref, x_ref, xp_ref, cw_ref, zag_ref, dzag_ref, dcw_ref, dcb_ref):
        i = pl.program_id(0)
        dy = dy_ref[...]
        yy = jnp.concatenate([dy, jnp.where(i < nt - 1, dyn_ref[...], 0.0)], 0)
        du0 = _conv_taps(yy, cw_ref, HALO_CONV, TM, True)
        zag = zag_ref[...].astype(F32)
        a, sg = zag[:, :D_CONV], jax.nn.sigmoid(zag[:, D_CONV:])
        dzag_ref[...] = jnp.concatenate([du0 * sg, du0 * a * sg * (1.0 - sg)], 1).astype(BF16)
        xx = jnp.concatenate([jnp.where(i > 0, xp_ref[...], 0.0), x_ref[...]], 0)

        @pl.when(i == 0)
        def _():
            dcw_ref[...] = jnp.zeros_like(dcw_ref)
            dcb_ref[...] = jnp.zeros_like(dcb_ref)

        rolls = {}
        for k in range(CONV_WIDTH):
            xs = _shifted_rows(xx, CONV_WIDTH - 1 - k, HALO_CONV, TM, False, rolls)
            dcw_ref[k:k + 1, :] += jnp.sum(dy * xs, 0, keepdims=True)
        dcb_ref[...] += jnp.sum(dy, 0, keepdims=True)

    n = len(halves)
    any_spec = pl.BlockSpec(memory_space=pl.ANY)
    outs = pl.pallas_call(
        _carrying(body, 6, 3, n, _sibling_gather_copies, nt), name="bwd_conv", grid=(nt,),
        in_specs=[_rows(TM, D_CONV), _next(HALO_CONV, D_CONV, TM, L // HALO_CONV), _rows(TM, D_CONV),
                  _prev(HALO_CONV, D_CONV, TM), _full(cw.shape), _rows(TM, D_AG)] + [any_spec] * n,
        out_specs=[_rows(TM, D_AG), _full(cw.shape), _full((1, D_CONV))] + [any_spec] * n,
        out_shape=[_sds((L, D_AG), BF16), _sds(cw.shape, F32), _sds((1, D_CONV), F32)]
        + [_sds(p.shape, p.dtype) for p in halves],
        input_output_aliases={6 + a: 3 + a for a in range(n)},
        scratch_shapes=_sibling_gather_semaphores(n) if n else [],
        compiler_params=_params(1))(du1, du1, u0, u0, cw, zag, *halves)
    return outs[:3], list(outs[3:])


def _bwd_in(dzag, dq, dkv, dkr, cq, ckv, gq, gkv, wuq, wukv, win, rc, rs1, rs2, h0, gmix, dh1):
    L = h0.shape[0]

    def body(dzag_ref, dq_ref, dkv_ref, dkr_ref, cq_ref, ckv_ref, gq_ref, gkv_ref, wuq_ref, wukv_ref, win_ref,
             c_ref, s1_ref, s2_ref, h0_ref, gmix_ref, dh1_ref,
             dz_ref, dqr_ref, gx_ref, dfirst_ref, dgq_ref, dgkv_ref, dgmix_ref):
        i = pl.program_id(0)
        c, s1, s2 = c_ref[...], s1_ref[...], s2_ref[...]
        dqr = _rope_bwd(dq_ref[...].astype(F32), jnp.tile(c, (1, N_HEADS)), jnp.tile(s1, (1, N_HEADS)),
                        jnp.tile(s2, (1, N_HEADS))).astype(BF16)
        dqr_ref[...] = dqr
        cq, ckv = cq_ref[...], ckv_ref[...]
        dcq, dgq = _rms_bwd(_dot(dqr, wuq_ref[...], NT), cq, _rms_r(cq, Q_LORA), gq_ref[...], Q_LORA)
        dckv, dgkv = _rms_bwd(_dot(dkv_ref[...], wukv_ref[...], NT), ckv, _rms_r(ckv, KV_LORA), gkv_ref[...], KV_LORA)
        dkrp = _rope_bwd(dkr_ref[...], c, s1, s2)
        dz = jnp.concatenate([dzag_ref[...], dcq.astype(BF16), dckv.astype(BF16), dkrp.astype(BF16)], 1)
        dz_ref[...] = dz
        h0 = h0_ref[...]
        dnx, dgmix = _rms_bwd(_dot(dz, win_ref[...], NT), h0, _rms_r(h0, D_MODEL), gmix_ref[...], D_MODEL)
        dh0 = dh1_ref[...] + dnx

        @pl.when(i == 0)
        def _():
            dfirst_ref[...] = dh0
            for ref in (dgq_ref, dgkv_ref, dgmix_ref):
                ref[...] = jnp.zeros_like(ref)

        @pl.when(i > 0)
        def _():
            gx_ref[...] = dh0

        dgq_ref[...] += dgq
        dgkv_ref[...] += dgkv
        dgmix_ref[...] += dgmix

    return pl.pallas_call(
        body, name="bwd_in", grid=(L // TM,),
        in_specs=[_rows(TM, D_AG), _rows(TM, D_HEADS), _rows(TM, D_HEADS), _rows(TM, HB), _rows(TM, Q_LORA),
                  _rows(TM, KV_LORA), _full(gq.shape), _full(gkv.shape), _full(wuq.shape), _full(wukv.shape),
                  _full(win.shape), _rows(TM, HB), _rows(TM, HB), _rows(TM, HB), _rows(TM, D_MODEL),
                  _full(gmix.shape), _rows(TM, D_MODEL)],
        out_specs=[_rows(TM, D_ZP), _rows(TM, D_HEADS),
                   pl.BlockSpec((TM, D_MODEL), lambda i: (jnp.maximum(i - 1, 0), 0)), _full((TM, D_MODEL)),
                   _full((1, Q_LORA)), _full((1, KV_LORA)), _full((1, D_MODEL))],
        out_shape=[_sds((L, D_ZP), BF16), _sds((L, D_HEADS), BF16), _sds((L - TM, D_MODEL), F32),
                   _sds((TM, D_MODEL), F32), _sds((1, Q_LORA), F32), _sds((1, KV_LORA), F32), _sds((1, D_MODEL), F32)],
        compiler_params=_params(1))(dzag, dq, dkv, dkr, cq, ckv, gq, gkv, wuq, wukv, win, rc, rs1, rs2, h0, gmix, dh1)


def _mesh_pos():
    return lax.axis_index("x"), lax.axis_index("y"), lax.axis_index("c")


def _remote_copy(src, dst, send_sem, recv_sem, to):
    return functools.partial(pltpu.make_async_remote_copy, src, dst, send_sem, recv_sem, device_id=to,
                             device_id_type=MESH)


def _all_gather(shards):
    n = len(shards)

    def body(*refs):
        sends, arrivals, forwards, finals = _gather_copies(refs[:n], refs[n:2 * n], *refs[2 * n:])
        for cp in sends:
            cp().start()
        for landed, onward in zip(arrivals, forwards):
            landed().wait_recv()
            onward().start()
        for cp in finals:
            cp().wait_recv()
        for cp in sends + forwards:
            cp().wait_send()

    any_spec = pl.BlockSpec(memory_space=pl.ANY)
    outs = pl.pallas_call(
        body, name="all_gather_weights", in_specs=[any_spec] * n, out_specs=[any_spec] * n,
        out_shape=_gather_out_shapes(shards), scratch_shapes=_gather_semaphores(n))(*shards)
    return _gathered(outs, shards)


def _gather_out_shapes(shards):
    return [_sds((2 * N_CHIPS, s.shape[0] // 2) + s.shape[1:], s.dtype) for s in shards]


def _gather_semaphores(n):
    return [pltpu.SemaphoreType.DMA((n, 8)), pltpu.SemaphoreType.DMA((n, 8))]


def _gathered(outs, shards):
    return [o.reshape((N_CHIPS, s.shape[0]) + s.shape[1:]) for o, s in zip(outs, shards)]


def _gather_copies(ins, outs, send_sems, recv_sems):
    x, y, c = _mesh_pos()
    chips = [(1 - x, y), (x, 1 - y), (1 - x, 1 - y)]
    sends, arrivals, forwards, finals = [], [], [], []

    def copy(src, dst, a, k, to):
        return _remote_copy(src, dst, send_sems.at[a, k], recv_sems.at[a, k], to)

    for a, (src, out) in enumerate(zip(ins, outs)):
        m = out.shape[1]
        mine = src.at[pl.ds(pl.multiple_of(c * m, 16), m)]
        for hf in range(2):
            own = out.at[4 * x + 2 * y + hf]
            sends.append(copy(src.at[pl.ds(hf * m, m)], own, a, 6 + hf, (x, y, 1 - c)))
            finals.append(copy(own, own, a, 6 + hf, (x, y, 1 - c)))
        for k, chip in enumerate(chips):
            slot = 4 * chip[0] + 2 * chip[1]
            sends.append(copy(mine, out.at[4 * x + 2 * y + c], a, k, (*chip, c)))
            arrivals.append(copy(out.at[slot + c], out.at[slot + c], a, k, (*chip, c)))
            forwards.append(copy(out.at[slot + c], out.at[slot + c], a, 3 + k, (x, y, 1 - c)))
            finals.append(copy(out.at[slot + 1 - c], out.at[slot + 1 - c], a, 3 + k, (x, y, 1 - c)))
    return sends, arrivals, forwards, finals


def _sibling_exchange(parts, name):
    n = len(parts)

    def body(*refs):
        copies = _sibling_exchange_copies(refs[:n], refs[n:2 * n], *refs[2 * n:])
        for cp in copies:
            cp().start()
        for cp in copies:
            cp().wait()

    any_spec = pl.BlockSpec(memory_space=pl.ANY)
    return pl.pallas_call(
        body, name=name, in_specs=[any_spec] * n, out_specs=[any_spec] * n,
        out_shape=_sibling_exchange_shapes(parts), scratch_shapes=_sibling_exchange_semaphores(n))(*parts)


def _sibling_exchange_shapes(parts):
    return [_sds((N_CHIPS, p.shape[1] // 2, p.shape[2]), p.dtype) for p in parts]


def _sibling_exchange_semaphores(n):
    return [pltpu.SemaphoreType.DMA((n, N_CHIPS)), pltpu.SemaphoreType.DMA((n, N_CHIPS))]


def _sibling_exchange_copies(ins, theirs, send_sems, recv_sems):
    x, y, c = _mesh_pos()
    copies = []
    for a, (src, dst) in enumerate(zip(ins, theirs)):
        h = dst.shape[1]
        rows = pl.ds(pl.multiple_of((1 - c) * h, 16), h)
        copies += [_remote_copy(src.at[q, rows], dst.at[q], send_sems.at[a, q], recv_sems.at[a, q], (x, y, 1 - c))
                   for q in range(N_CHIPS)]
    return copies


def _chip_exchange(parts, name):
    n = len(parts)

    def body(*refs):
        sends, arrivals = _chip_copies(refs[:n], refs[n:2 * n], *refs[2 * n:])
        for cp in sends:
            cp().start()
        for cp in arrivals:
            cp().wait_recv()
        for cp in sends:
            cp().wait_send()

    any_spec = pl.BlockSpec(memory_space=pl.ANY)
    return pl.pallas_call(
        body, name=name, in_specs=[any_spec] * n, out_specs=[any_spec] * n,
        out_shape=[_sds(p.shape, p.dtype) for p in parts], scratch_shapes=_chip_semaphores(n))(*parts)


def _chip_semaphores(n):
    return [pltpu.SemaphoreType.DMA((n, 3)), pltpu.SemaphoreType.DMA((n, 3))]


def _chip_copies(ins, outs, send_sems, recv_sems):
    x, y, c = _mesh_pos()
    me = 2 * x + y
    sends, arrivals = [], []
    for a, (src, out) in enumerate(zip(ins, outs)):
        for k, chip in enumerate([(1 - x, y), (x, 1 - y), (1 - x, 1 - y)]):
            slot = 2 * chip[0] + chip[1]
            sems = (send_sems.at[a, k], recv_sems.at[a, k], (*chip, c))
            sends.append(_remote_copy(src.at[slot], out.at[me], *sems))
            arrivals.append(_remote_copy(out.at[slot], out.at[slot], *sems))
    return sends, arrivals


def _sibling_gather(parts, name):
    n = len(parts)

    def body(*refs):
        copies = _sibling_gather_copies(refs[:n], refs[n:2 * n], *refs[2 * n:])
        for cp in copies:
            cp().start()
        for cp in copies:
            cp().wait()

    any_spec = pl.BlockSpec(memory_space=pl.ANY)
    return pl.pallas_call(
        body, name=name, in_specs=[any_spec] * n, out_specs=[any_spec] * n,
        out_shape=[_sds(p.shape, p.dtype) for p in parts], input_output_aliases={a: a for a in range(n)},
        scratch_shapes=_sibling_gather_semaphores(n))(*parts)


def _sibling_gather_semaphores(n):
    return [pltpu.SemaphoreType.DMA((n,)), pltpu.SemaphoreType.DMA((n,))]


def _sibling_gather_copies(ins, outs, send_sems, recv_sems):
    x, y, c = _mesh_pos()
    return [_remote_copy(src.at[c], dst.at[c], send_sems.at[a], recv_sems.at[a], (x, y, 1 - c))
            for a, (src, dst) in enumerate(zip(ins, outs))]


def _row_tile(rows, row_bytes, align, budget=1 << 20):
    best = None
    for t in range(align, rows + 1, align):
        if rows % t == 0 and t * row_bytes <= budget:
            best = t
    return best or rows


def _scalar(v):
    return jnp.reshape(v, (1,)).astype(jnp.int32)


def _add_pair(part, theirs, c, name):
    _, h, cols = theirs.shape
    tr = _row_tile(h, cols * 4, 16, budget=1 << 21)
    nb = h // tr

    def body(c_ref, a_ref, b_ref, o_ref):
        o_ref[...] = (a_ref[...].astype(F32) + b_ref[...].astype(F32)).astype(o_ref.dtype)

    half = pl.BlockSpec((None, tr, cols), lambda q, i, c_ref: (q, i, 0))
    grid_spec = pltpu.PrefetchScalarGridSpec(
        num_scalar_prefetch=1, grid=(N_CHIPS, nb),
        in_specs=[pl.BlockSpec((None, tr, cols), lambda q, i, c_ref: (q, c_ref[0] * nb + i, 0)), half],
        out_specs=half)
    return pl.pallas_call(body, name=name, grid_spec=grid_spec, out_shape=_sds(theirs.shape, part.dtype),
                          compiler_params=_params(2))(_scalar(c), part, theirs)


def _add_chips(got, own, me, c, name):
    _, h, cols = got.shape
    tr = _row_tile(h, cols * 4 * N_CHIPS, 16, budget=1 << 22)

    def body(pos_ref, got_ref, own_ref, o_ref):
        acc = None
        for q in range(N_CHIPS):
            term = jnp.where(pos_ref[0] == q, own_ref[q], got_ref[q]).astype(F32)
            acc = term if acc is None else acc + term
        o_ref[...] = acc

    by_chip = pl.BlockSpec((N_CHIPS, tr, cols), lambda i, pos_ref: (0, i, 0))
    grid_spec = pltpu.PrefetchScalarGridSpec(
        num_scalar_prefetch=1, grid=(h // tr,), in_specs=[by_chip, by_chip],
        out_specs=pl.BlockSpec((None, tr, cols), lambda i, pos_ref: (pos_ref[1], i, 0)))
    return pl.pallas_call(body, name=name, grid_spec=grid_spec, out_shape=_sds((2, h, cols), F32),
                          compiler_params=_params(1))(jnp.stack([me, c]).astype(jnp.int32), got, own)


def _add_pairs(parts, theirs, tag):
    c = lax.axis_index("c")
    return [_add_pair(p, t, c, f"grad_add_pair_{tag}_{a}") for a, (p, t) in enumerate(zip(parts, theirs))]


def _add_all_chips(pair, got, tag):
    x, y, c = _mesh_pos()
    return [_add_chips(g, p, 2 * x + y, c, f"grad_add_chips_{tag}_{a}") for a, (g, p) in enumerate(zip(got, pair))]


def _totals(both):
    return [b.reshape(-1, b.shape[-1]) for b in both]


def _reduce_begin(parts, tag):
    return _add_pairs(parts, _sibling_exchange(parts, f"grad_sibling_exchange_{tag}"), tag)


def _reduce_end(pair, got, tag):
    return _totals(_sibling_gather(_add_all_chips(pair, got, tag), f"grad_sibling_gather_{tag}"))


def _adamw_math(w, g, m, v):
    m = ADAM_B1 * m + (1.0 - ADAM_B1) * g
    v = ADAM_B2 * v + (1.0 - ADAM_B2) * (g * g)
    m_hat = m / (1.0 - ADAM_B1 ** ADAM_STEP)
    v_hat = v / (1.0 - ADAM_B2 ** ADAM_STEP)
    return -ADAM_LR * (m_hat / (jnp.sqrt(v_hat) + ADAM_EPS) + ADAM_WD * w), m, v


def _adamw_big(w, g, m, v, name, parts=()):
    r, c = w.shape
    tr = _row_tile(r, c * 4, 8, budget=1 << 19)
    if r // tr > ADAMW_MAX_STEPS:
        tr = r
    n = len(parts)
    steps = r // tr

    def body(w_ref, g_ref, m_ref, v_ref, *refs):
        go_ref, d_ref, mo_ref, vo_ref = refs[n:n + 4]
        exchange_refs = (refs[:n], refs[n + 4:2 * n + 4]) + refs[2 * n + 4:]
        if n:
            @pl.when(pl.program_id(0) == 0)
            def _():
                for cp in _chip_copies(*exchange_refs)[0]:
                    cp().start()

        g = g_ref[...]
        go_ref[...] = g
        d_ref[...], mo_ref[...], vo_ref[...] = _adamw_math(w_ref[...], g, m_ref[...], v_ref[...])
        if n:
            @pl.when(pl.program_id(0) == steps - 1)
            def _():
                sends, arrivals = _chip_copies(*exchange_refs)
                for cp in arrivals:
                    cp().wait_recv()
                for cp in sends:
                    cp().wait_send()

    any_spec = pl.BlockSpec(memory_space=pl.ANY)
    outs = pl.pallas_call(
        body, name=name, grid=(steps,), in_specs=[_rows(tr, c)] * 4 + [any_spec] * n,
        out_specs=[_rows(tr, c)] * 4 + [any_spec] * n,
        out_shape=[_sds((r, c), F32)] * 4 + [_sds(p.shape, p.dtype) for p in parts],
        scratch_shapes=_chip_semaphores(n) if n else [], compiler_params=_params(1))(w, g, m, v, *parts)
    return outs[:4], list(outs[4:])


def _adamw_small(ws, gs, ms, vs):
    n = len(ws)

    def body(*refs):
        for a in range(n):
            w_ref, g_ref, m_ref, v_ref = (refs[k * n + a] for k in range(4))
            d, m, v = _adamw_math(w_ref[...], g_ref[...], m_ref[...], v_ref[...])
            refs[4 * n + a][...] = d
            refs[5 * n + a][...] = m
            refs[6 * n + a][...] = v

    vm = pl.BlockSpec(memory_space=pltpu.VMEM)
    outs = pl.pallas_call(
        body, name="adamw_small", in_specs=[vm] * (4 * n), out_specs=[vm] * (3 * n),
        out_shape=[_sds(w.shape, F32) for w in ws] * 3)(*ws, *gs, *ms, *vs)
    return outs[:n], outs[n:2 * n], outs[2 * n:]


BIG = ("w_in", "w_uq", "w_ukv", "w_out", "w_ffn_up", "w_ffn_down")
SMALL_SHARDED = ("conv_w", "ffn_conv_w", "meta_tokens")
REPLICATED = ("mix_norm_g", "q_norm_g", "kv_norm_g", "conv_b", "conv_ln_g", "conv_ln_b", "conv_out_g", "attn_out_g",
              "ffn_norm_g", "ffn_conv_b", "final_norm_g")
WEIGHTS = ("meta_tokens", "mix_norm_g", "w_in", "q_norm_g", "w_uq", "kv_norm_g", "w_ukv", "conv_w", "conv_b",
           "conv_ln_g", "conv_ln_b", "conv_out_g", "attn_out_g", "w_out", "ffn_norm_g", "w_ffn_up", "ffn_conv_w",
           "ffn_conv_b", "w_ffn_down", "final_norm_g")


def _lane_rows(a):
    return a.reshape(N_CHIPS, -1, LANES)


def _col_shards(a):
    k = a.shape[0]
    return a.reshape(k, N_CHIPS, -1).transpose(1, 0, 2)


def _from_col_shards(a):
    return a.transpose(1, 0, 2).reshape(a.shape[1], -1)


def _pad_rows_to(a, rows):
    return jnp.pad(a, ((0, 0), (0, rows - a.shape[1]), (0, 0)))


def _rope_tables(L):
    pos = (jnp.arange(L, dtype=jnp.int32) - DEAD).astype(F32)
    inv_freq = 1.0 / (ROPE_THETA ** (jnp.arange(0, QK_ROPE, 2, dtype=F32) / QK_ROPE))
    ang = pos[:, None] * inv_freq[None, :]
    cos, sin = jnp.cos(ang), jnp.sin(ang)
    half = QK_ROPE // 2
    z = lambda n: jnp.zeros((L, n), F32)
    rc = jnp.concatenate([jnp.ones((L, QK_NOPE), F32), cos, cos, z(HB - QK_NOPE - QK_ROPE)], 1)
    rs1 = jnp.concatenate([z(QK_NOPE), -sin, z(HB - QK_NOPE - half)], 1)
    rs2 = jnp.concatenate([z(QK_NOPE + half), sin, z(HB - QK_NOPE - QK_ROPE)], 1)
    return rc, rs1, rs2


def _pad_heads(g):
    return jnp.pad(g.reshape(N_HEADS, V_HEAD), ((0, 0), (HB - V_HEAD, 0))).reshape(1, D_HEADS)


def _unpad_heads(g):
    return g.reshape(N_HEADS, HB)[:, HB - V_HEAD:].reshape(1, N_HEADS * V_HEAD)


def _local_step(x, target, w, late_shards=None, reduce_first=False):
    S = x.shape[0]
    L = TM + S
    tl = L // 4
    d_qk = QK_NOPE + QK_ROPE
    win_n = w["w_in"]
    kr0 = D_AG + Q_LORA + KV_LORA
    win = jnp.concatenate([win_n[:, :kr0], jnp.zeros((D_MODEL, QK_NOPE), BF16), win_n[:, kr0:],
                           jnp.zeros((D_MODEL, HB - d_qk), BF16)], 1)
    wuq = jnp.pad(w["w_uq"].reshape(Q_LORA, N_HEADS, d_qk), ((0, 0), (0, 0), (0, HB - d_qk))).reshape(Q_LORA, D_HEADS)
    wukv = w["w_ukv"]
    ga = _pad_heads(w["attn_out_g"])
    gfin = w["final_norm_g"].reshape(1, D_MODEL)
    rc, rs1, rs2 = _rope_tables(L)
    head = jnp.concatenate([jnp.zeros((DEAD, D_MODEL), F32), w["meta_tokens"]], 0)

    h0, n, zag, u0, cq, ckv, qn, kvn, q, kv, kr, kvt = _fwd_in(x, head, w["mix_norm_g"], win, w["q_norm_g"], wuq,
                                                                w["kv_norm_g"], wukv, rc, rs1, rs2)
    u1, mixa = _fwd_conv(u0, w["conv_w"], w["conv_b"], w["conv_ln_g"], w["conv_ln_b"], w["conv_out_g"])
    if late_shards is None:
        o, lse, _ = _attn_fwd(q, kv, kvt, kr)
        wout_n, wup, wdown = w["w_out"], _col_shards(w["w_ffn_up"]), w["w_ffn_down"]
    else:
        o, lse, late = _attn_fwd(q, kv, kvt, kr, [late_shards[k] for k in LATE])
        wout_n, wup, wdown = _full_weight("w_out", late[0]), late[1], _full_weight("w_ffn_down", late[2])
    wout = jnp.concatenate([wout_n[:D_CONV], jnp.pad(wout_n[D_CONV:].reshape(N_HEADS, V_HEAD, D_MODEL),
                                                     ((0, 0), (HB - V_HEAD, 0), (0, 0))).reshape(D_HEADS, D_MODEL)], 0)
    mix, h1, n2 = _fwd_mix(h0, mixa, o, ga, wout, w["ffn_norm_g"])
    up0 = _mm(n2, wup, NN, BF16, tl, FF_TILE, D_MODEL, "ffn_up", b_slabs=True)
    dh2, loss, g_fin, up, act = _fwd_ffn_loss(up0, w["ffn_conv_w"], w["ffn_conv_b"], wdown, h1, target, gfin)

    dup, g_fcb = _bwd_ffn_act(dh2, up, wdown)
    dn2, dup0, g_fcw_tiles = _bwd_ffn_conv_up(dup, up0, w["ffn_conv_w"], wup, tl)
    g_fcw = jnp.sum(g_fcw_tiles, 0)
    g_wup = _mm(n2, dup0, TN, BF16, D_MODEL, FF_TILE, tl, "ffn_up_dw", by_col_tile=True)
    g_wdown = _mm(act, dh2, TN, BF16, D_FF // 2, D_MODEL, tl, "ffn_down_dw").reshape(N_CHIPS, -1, D_MODEL)
    ffn = [g_wup, g_wdown] if reduce_first else []
    (dh1, dh1m, do, delta, du1, g_gffn, g_ga, g_og, g_lg, g_lb), theirs = _bwd_mix(
        dh2, dn2, h1, w["ffn_norm_g"], wout, o, ga, u1, w["conv_ln_g"], w["conv_ln_b"], w["conv_out_g"], ffn)
    g_wout = _mm(mix, dh1m, TN, BF16, D_MIX // 2, D_MODEL, tl, "out_dw")
    g_wout = jnp.concatenate([g_wout[:D_CONV], g_wout[D_CONV:].reshape(N_HEADS, HB, D_MODEL)[:, HB - V_HEAD:]
                              .reshape(N_HEADS * V_HEAD, D_MODEL)], 0).reshape(N_CHIPS, -1, D_MODEL)
    pair = ()
    if reduce_first:
        theirs += _sibling_exchange([g_wout], "grad_sibling_exchange_out")
        pair = _add_pairs(ffn + [g_wout], theirs, "first")
    dq, dkv, dkr, got = _attn_bwd(q, kv, kr, do, lse, delta, pair)
    halves = _add_all_chips(pair, got, "first") if reduce_first else ()
    (dzag, g_cw, g_cb), both = _bwd_conv(du1, u0, w["conv_w"], zag, halves)
    if reduce_first:
        g_wup, g_wdown, g_wout = _totals(both)
    dz, dqr, gx, dfirst, g_gq, g_gkv, g_gmix = _bwd_in(dzag, dq, dkv, dkr, cq, ckv, w["q_norm_g"], w["kv_norm_g"],
                                                      wuq, wukv, win, rc, rs1, rs2, h0, w["mix_norm_g"], dh1)
    g_win = _mm(n, dz, TN, BF16, D_MODEL, D_ZP // 2, tl, "in_dw")
    g_wuq = _mm(qn, dqr, TN, BF16, Q_LORA, D_HEADS, tl, "uq_dw")
    g_wukv = _col_shards(_mm(kvn, dkv, TN, BF16, KV_LORA, D_HEADS, tl, "ukv_dw"))

    grads = {
        "w_in": _col_shards(jnp.concatenate([g_win[:, :kr0], g_win[:, kr0 + QK_NOPE:kr0 + d_qk]], 1)),
        "w_uq": _col_shards(g_wuq.reshape(Q_LORA, N_HEADS, HB)[:, :, :d_qk].reshape(Q_LORA, N_HEADS * d_qk)),
        "w_ukv": g_wukv,
        "w_out": g_wout,
        "w_ffn_up": g_wup, "w_ffn_down": g_wdown, "conv_w": g_cw, "ffn_conv_w": g_fcw,
        "meta_tokens": dfirst[DEAD:], "mix_norm_g": g_gmix, "q_norm_g": g_gq, "kv_norm_g": g_gkv, "conv_b": g_cb,
        "conv_ln_g": g_lg, "conv_ln_b": g_lb, "conv_out_g": g_og, "attn_out_g": _unpad_heads(g_ga),
        "ffn_norm_g": g_gffn, "ffn_conv_b": g_fcb, "final_norm_g": g_fin,
    }
    return loss, gx, grads


ROW_SHARDED = ("w_out", "w_ffn_down")


TRANSPOSED = ("w_in", "w_uq")
REDUCED_FIRST = ("w_ffn_up", "w_ffn_down", "w_out")
LATE = ("w_out", "w_ffn_up", "w_ffn_down")


def _full_weight(name, by_chip):
    return by_chip.reshape(-1, by_chip.shape[-1]) if name in ROW_SHARDED else _from_col_shards(by_chip)


def kernel(x, meta_tokens, mix_norm_g, w_in, q_norm_g, w_uq, kv_norm_g, w_ukv, conv_w, conv_b, conv_ln_g, conv_ln_b, conv_out_g, attn_out_g, w_out, ffn_norm_g, w_ffn_up, ffn_conv_w, ffn_conv_b, w_ffn_down, final_norm_g, loss_target, m_meta_tokens, m_mix_norm_g, m_w_in, m_q_norm_g, m_w_uq, m_kv_norm_g, m_w_ukv, m_conv_w, m_conv_b, m_conv_ln_g, m_conv_ln_b, m_conv_out_g, m_attn_out_g, m_w_out, m_ffn_norm_g, m_w_ffn_up, m_ffn_conv_w, m_ffn_conv_b, m_w_ffn_down, m_final_norm_g, v_meta_tokens, v_mix_norm_g, v_w_in, v_q_norm_g, v_w_uq, v_kv_norm_g, v_w_ukv, v_conv_w, v_conv_b, v_conv_ln_g, v_conv_ln_b, v_conv_out_g, v_attn_out_g, v_w_out, v_ffn_norm_g, v_w_ffn_up, v_ffn_conv_w, v_ffn_conv_b, v_w_ffn_down, v_final_norm_g):
    local = dict(meta_tokens=meta_tokens, mix_norm_g=mix_norm_g, w_in=w_in[0], q_norm_g=q_norm_g, w_uq=w_uq[0],
                 kv_norm_g=kv_norm_g, w_ukv=w_ukv[0], conv_w=conv_w[0], conv_b=conv_b, conv_ln_g=conv_ln_g,
                 conv_ln_b=conv_ln_b, conv_out_g=conv_out_g, attn_out_g=attn_out_g, w_out=w_out[0],
                 ffn_norm_g=ffn_norm_g, w_ffn_up=w_ffn_up[0], ffn_conv_w=ffn_conv_w[0], ffn_conv_b=ffn_conv_b,
                 w_ffn_down=w_ffn_down[0], final_norm_g=final_norm_g.reshape(1, D_MODEL))
    ms = dict(zip(WEIGHTS, (m_meta_tokens, m_mix_norm_g, m_w_in, m_q_norm_g, m_w_uq, m_kv_norm_g, m_w_ukv, m_conv_w,
                            m_conv_b, m_conv_ln_g, m_conv_ln_b, m_conv_out_g, m_attn_out_g, m_w_out, m_ffn_norm_g,
                            m_w_ffn_up, m_ffn_conv_w, m_ffn_conv_b, m_w_ffn_down, m_final_norm_g)))
    vs = dict(zip(WEIGHTS, (v_meta_tokens, v_mix_norm_g, v_w_in, v_q_norm_g, v_w_uq, v_kv_norm_g, v_w_ukv, v_conv_w,
                            v_conv_b, v_conv_ln_g, v_conv_ln_b, v_conv_out_g, v_attn_out_g, v_w_out, v_ffn_norm_g,
                            v_w_ffn_up, v_ffn_conv_w, v_ffn_conv_b, v_w_ffn_down, v_final_norm_g)))

    small_flat = jnp.concatenate([local[k].reshape(-1) for k in SMALL_SHARDED]).reshape(-1, LANES)
    early = [k for k in BIG if k not in LATE]
    gathered = _all_gather([local[k].astype(BF16) for k in early] + [small_flat])
    full = {k: v for k, v in local.items() if k not in LATE}
    for name, g in zip(early, gathered[:len(early)]):
        full[name] = _full_weight(name, g)
    small = gathered[-1].reshape(N_CHIPS, -1)
    at = 0
    for name in SMALL_SHARDED:
        r, c = local[name].shape
        full[name] = _from_col_shards(small[:, at:at + r * c].reshape(N_CHIPS, r, c))
        at += r * c

    loss_row, grad_x, grads = _local_step(x[0], loss_target[0], full, {k: local[k].astype(BF16) for k in LATE},
                                          reduce_first=True)

    rest_big = [k for k in BIG if k not in REDUCED_FIRST]
    rep = jnp.concatenate([grads[k].reshape(-1) for k in REPLICATED] + [loss_row.reshape(-1)]).reshape(1, -1, LANES)
    small_pieces = [_lane_rows(_col_shards(grads[k])) for k in SMALL_SHARDED]
    small_pieces.append(jnp.broadcast_to(rep, (N_CHIPS,) + rep.shape[1:]))
    small_rows = sum(p.shape[1] for p in small_pieces)
    small_pack = _pad_rows_to(jnp.concatenate(small_pieces, 1), -(-small_rows // 32) * 32)
    pair = _reduce_begin([grads[k] for k in rest_big] + [small_pack], "rest")

    total = {k: grads[k] for k in REDUCED_FIRST}
    delta, new_m, new_v = {}, {}, {}
    shape2 = lambda a, name: a.reshape(local[name].shape)
    turn = lambda a, name: a.T if name in TRANSPOSED else a

    def update(name, parts=()):
        outs, got = _adamw_big(turn(local[name], name), turn(total[name], name), turn(shape2(ms[name], name), name),
                               turn(shape2(vs[name], name), name), "adamw_" + name, parts)
        total[name], delta[name], new_m[name], new_v[name] = (turn(o, name) for o in outs)
        return got

    *rest_tot, small_tot = _reduce_end(pair, update(REDUCED_FIRST[0], pair), "rest")
    total.update(zip(rest_big, rest_tot))
    flat = small_tot.reshape(-1)
    at = 0
    for name in SMALL_SHARDED + REPLICATED:
        shape = local[name].shape
        size = shape[0] * shape[1]
        total[name] = flat[at:at + size].reshape(shape)
        at += -(-size // LANES) * LANES if name in SMALL_SHARDED else size
    loss = flat[at]

    for name in BIG:
        if name != REDUCED_FIRST[0]:
            update(name)
    rest = SMALL_SHARDED + REPLICATED
    ds, nms, nvs = _adamw_small([local[k] for k in rest], [total[k] for k in rest],
                                [shape2(ms[k], k) for k in rest], [shape2(vs[k], k) for k in rest])
    for k, d, nm, nv in zip(rest, ds, nms, nvs):
        delta[k], new_m[k], new_v[k] = d, nm, nv

    out_shape = dict(zip(WEIGHTS, (meta_tokens, mix_norm_g, w_in, q_norm_g, w_uq, kv_norm_g, w_ukv, conv_w, conv_b,
                                   conv_ln_g, conv_ln_b, conv_out_g, attn_out_g, w_out, ffn_norm_g, w_ffn_up,
                                   ffn_conv_w, ffn_conv_b, w_ffn_down, final_norm_g)))
    outs = [loss, grad_x[None]]
    for group in (total, delta, new_m, new_v):
        outs += [group[k].reshape(out_shape[k].shape) for k in WEIGHTS]
    return tuple(outs)
```

```python
import functools

import jax
import jax.numpy as jnp
from jax import lax
from jax.experimental import pallas as pl
from jax.experimental.pallas import tpu as pltpu

F32 = jnp.float32
BF16 = jnp.bfloat16

D_MODEL = 1024
D_CONV = 512
CONV_WIDTH = 31
N_HEADS = 8
QK_NOPE = 64
QK_ROPE = 32
V_HEAD = 64
Q_LORA = 384
KV_LORA = 256
D_FF = 2816
FFN_CONV_WIDTH = 3
CHUNK_SHIFT = 6
N_META = 16
ROPE_THETA = 10000.0
EPS = 1e-6
NEG = -1e30
ADAM_LR = 0.001
ADAM_B1 = 0.9
ADAM_B2 = 0.999
ADAM_EPS = 1e-08
ADAM_WD = 0.01
ADAM_STEP = 10

LANES = 128
SUBLANES = 8
HB = LANES
D_HEADS = N_HEADS * HB
TM = 256
DEAD = TM - N_META
D_AG = 2 * D_CONV
D_ZP = D_AG + Q_LORA + KV_LORA + HB
D_MIX = D_CONV + D_HEADS
LN2 = 0.6931471805599453
Q_SCALE = (QK_NOPE + QK_ROPE) ** -0.5 / LN2
HALO_CONV = 32
HALO_FFN = 16
FF_CHUNK = 256
FF_MXU_CHUNK = 256
FF_TILE = D_FF // 2
VMEM_LIMIT = 56 * 1024 * 1024
ADAMW_MAX_STEPS = 32
N_CHIPS = 4
HEAD_GROUP = 4
N_GROUPS = N_HEADS // HEAD_GROUP
MESH =pl.DeviceIdType.MESH


def _params(n_grid):
    return pltpu.CompilerParams(dimension_semantics=("arbitrary",) * n_grid, vmem_limit_bytes=VMEM_LIMIT)


def _rows(tm, c, off=0):
    return pl.BlockSpec((tm, c), lambda i: (i, off))


def _full(shape):
    return pl.BlockSpec(shape, lambda i: (0,) * len(shape))


def _prev(hb, c, tm, off=0):
    return pl.BlockSpec((hb, c), lambda i: (jnp.maximum(i * (tm // hb) - 1, 0), off))


def _next(hb, c, tm, nblk, off=0):
    return pl.BlockSpec((hb, c), lambda i: (jnp.minimum((i + 1) * (tm // hb), nblk - 1), off))


def _sds(shape, dtype):
    return jax.ShapeDtypeStruct(shape, dtype)


def _rms_r(x, n):
    return lax.rsqrt(jnp.sum(x * x, -1, keepdims=True) * (1.0 / n) + EPS)


def _rms_bwd(dy, x, r, g, n):
    gd = dy * g
    dx = r * gd - x * (r * r * r) * (jnp.sum(x * gd, -1, keepdims=True) * (1.0 / n))
    return dx, jnp.sum(dy * x * r, 0, keepdims=True)


def _dot(a, b, dims):
    return lax.dot_general(a, b, (dims, ((), ())), preferred_element_type=F32)


NN = ((1,), (0,))
NT = ((1,), (1,))
TN = ((0,), (0,))


def _rope(x, c, s1, s2):
    n = x.shape[-1]
    return x * c + pltpu.roll(x, n - QK_ROPE // 2, 1) * s1 + pltpu.roll(x, QK_ROPE // 2, 1) * s2


def _rope_bwd(g, c, s1, s2):
    n = g.shape[-1]
    return g * c + pltpu.roll(g * s1, QK_ROPE // 2, 1) + pltpu.roll(g * s2, n - QK_ROPE // 2, 1)


def _row_ids(tm, cols=1):
    return pl.program_id(0) * tm + lax.broadcasted_iota(jnp.int32, (tm, cols), 0)


def _mm(a, b, dims, out_dtype, tm, tn, tk, name, by_col_tile=False, b_slabs=False):
    if dims == TN:
        (kk, m), (_, n) = a.shape, b.shape
        a_spec = pl.BlockSpec((tk, tm), lambda i, j, k: (k, i))
    else:
        m, kk = a.shape
        a_spec = pl.BlockSpec((tm, tk), lambda i, j, k: (i, k))
    if dims == NT and b_slabs:
        n = b.shape[1]
        assert b.shape[2] == tk and kk == b.shape[0] * tk, (name, b.shape)
        b_spec = pl.BlockSpec((None, tn, tk), lambda i, j, k: (k, j, 0))
    elif dims == NT:
        n = b.shape[0]
        b_spec = pl.BlockSpec((tn, tk), lambda i, j, k: (j, k))
    elif b_slabs:
        n = b.shape[0] * b.shape[2]
        assert b.shape[2] == tn and kk == b.shape[1], (name, b.shape)
        b_spec = pl.BlockSpec((None, tk, tn), lambda i, j, k: (j, k, 0))
    else:
        n = b.shape[1]
        b_spec = pl.BlockSpec((tk, tn), lambda i, j, k: (k, j))
    assert m % tm == 0 and n % tn == 0 and kk % tk == 0, (name, a.shape, b.shape, tm, tn, tk)
    nk = kk // tk

    def body(a_ref, b_ref, o_ref, acc_ref):
        k = pl.program_id(2)

        @pl.when(k == 0)
        def _():
            acc_ref[...] = jnp.zeros_like(acc_ref)

        acc_ref[...] += _dot(a_ref[...].astype(BF16), b_ref[...].astype(BF16), dims)

        @pl.when(k == nk - 1)
        def _():
            o_ref[...] = acc_ref[...].astype(out_dtype)

    if by_col_tile:
        out_spec, out_shape = pl.BlockSpec((None, tm, tn), lambda i, j, k: (j, i, 0)), (n // tn, m, tn)
    else:
        out_spec, out_shape = pl.BlockSpec((tm, tn), lambda i, j, k: (i, j)), (m, n)
    return pl.pallas_call(
        body, name=name, grid=(m // tm, n // tn, nk), in_specs=[a_spec, b_spec], out_specs=out_spec,
        out_shape=_sds(out_shape, out_dtype), scratch_shapes=[pltpu.VMEM((tm, tn), F32)],
        compiler_params=_params(3))(a, b)


def _fwd_in(x, head, gmix, win, gq, wuq, gkv, wukv, rc, rs1, rs2):
    L = TM + x.shape[0]

    def body(x_ref, head_ref, gmix_ref, win_ref, gq_ref, wuq_ref, gkv_ref, wukv_ref, c_ref, s1_ref, s2_ref,
             h0_ref, n_ref, zag_ref, u0_ref, cq_ref, ckv_ref, qn_ref, kvn_ref, q_ref, kv_ref, kr_ref, kvt_ref):
        h = jnp.where(pl.program_id(0) == 0, head_ref[...], x_ref[...])
        h0_ref[...] = h
        n = (h * _rms_r(h, D_MODEL) * gmix_ref[...]).astype(BF16)
        n_ref[...] = n
        z = _dot(n, win_ref[...], NN)
        a, gate = z[:, :D_CONV], z[:, D_CONV:D_AG]
        zag_ref[...] = z[:, :D_AG].astype(BF16)
        u0_ref[...] = a * jax.nn.sigmoid(gate)
        cq = z[:, D_AG:D_AG + Q_LORA]
        ckv = z[:, D_AG + Q_LORA:D_AG + Q_LORA + KV_LORA]
        krp = z[:, D_AG + Q_LORA + KV_LORA:]
        cq_ref[...] = cq
        ckv_ref[...] = ckv
        qn = (cq * _rms_r(cq, Q_LORA) * gq_ref[...]).astype(BF16)
        qn_ref[...] = qn
        kvn = (ckv * _rms_r(ckv, KV_LORA) * gkv_ref[...]).astype(BF16)
        kvn_ref[...] = kvn
        c, s1, s2 = c_ref[...], s1_ref[...], s2_ref[...]
        q = _dot(qn, wuq_ref[...], NN)
        q = _rope(q, jnp.tile(c, (1, N_HEADS)), jnp.tile(s1, (1, N_HEADS)), jnp.tile(s2, (1, N_HEADS)))
        q_ref[...] = (q * Q_SCALE).astype(BF16)
        kv = _dot(kvn, wukv_ref[...], NN)
        kv_ref[...] = kv.astype(BF16)
        kvt_ref[...] = kv.T.astype(BF16)
        kr_ref[...] = _rope(krp, c, s1, s2).astype(BF16)

    outs = [(D_MODEL, F32), (D_MODEL, BF16), (D_AG, BF16), (D_CONV, F32), (Q_LORA, F32), (KV_LORA, F32), (Q_LORA, BF16),
            (KV_LORA, BF16), (D_HEADS, BF16), (D_HEADS, BF16), (HB, BF16)]
    return pl.pallas_call(
        body, name="fwd_in", grid=(L // TM,),
        in_specs=[pl.BlockSpec((TM, D_MODEL), lambda i: (jnp.maximum(i - 1, 0), 0)), _full(head.shape),
                  _full(gmix.shape), _full(win.shape), _full(gq.shape), _full(wuq.shape),
                  _full(gkv.shape), _full(wukv.shape), _rows(TM, HB), _rows(TM, HB), _rows(TM, HB)],
        out_specs=[_rows(TM, c) for c, _ in outs] + [pl.BlockSpec((D_HEADS, TM), lambda i: (0, i))],
        out_shape=[_sds((L, c), d) for c, d in outs] + [_sds((D_HEADS, L), BF16)],
        compiler_params=_params(1))(x, head, gmix, win, gq, wuq, gkv, wukv, rc, rs1, rs2)


def _conv_taps(xx, w_ref, halo, tm, flip):
    kw = w_ref.shape[0]
    acc, rolls = None, {}
    for k in range(kw):
        term = w_ref[k:k + 1, :] * _shifted_rows(xx, kw - 1 - k, halo, tm, flip, rolls)
        acc = term if acc is None else acc + term
    return acc


def _shifted_rows(xx, d, halo, tm, flip, rolls):
    a, b = divmod(d, SUBLANES)
    if b not in rolls:
        rolls[b] = xx if b == 0 else pltpu.roll(xx, (xx.shape[0] - b) if flip else b, 0)
    start = SUBLANES * a if flip else halo - SUBLANES * a
    return rolls[b][start:start + tm]


def _ln_silu(u1, lg, lb):
    mu = jnp.mean(u1, -1, keepdims=True)
    xc = u1 - mu
    rs = lax.rsqrt(jnp.mean(xc * xc, -1, keepdims=True) + EPS)
    xh = xc * rs
    u2 = xh * lg + lb
    sg = jax.nn.sigmoid(u2)
    return rs, xh, u2, sg, u2 * sg


def _fwd_conv(u0, cw, cb, lg, lb, og):
    L = u0.shape[0]

    def body(u0_ref, u0p_ref, cw_ref, cb_ref, lg_ref, lb_ref, og_ref, u1_ref, mixa_ref):
        halo = jnp.where(pl.program_id(0) > 0, u0p_ref[...], 0.0)
        xx = jnp.concatenate([halo, u0_ref[...]], 0)
        u1 = _conv_taps(xx, cw_ref, HALO_CONV, TM, False) + cb_ref[...]
        u1_ref[...] = u1
        u = _ln_silu(u1, lg_ref[...], lb_ref[...])[4]
        mixa_ref[...] = (u * _rms_r(u, D_CONV) * og_ref[...]).astype(BF16)

    return pl.pallas_call(
        body, name="fwd_conv", grid=(L // TM,),
        in_specs=[_rows(TM, D_CONV), _prev(HALO_CONV, D_CONV, TM), _full(cw.shape), _full(cb.shape),
                  _full(lg.shape), _full(lb.shape), _full(og.shape)],
        out_specs=[_rows(TM, D_CONV), _rows(TM, D_CONV)],
        out_shape=[_sds((L, D_CONV), F32), _sds((L, D_CONV), BF16)],
        compiler_params=_params(1))(u0, u0, cw, cb, lg, lb, og)


def _visible_t(i, j, t):
    key = j * t + lax.broadcasted_iota(jnp.int32, (t, t), 0)
    query = i * t + lax.broadcasted_iota(jnp.int32, (t, t), 1)
    return (lax.shift_right_logical(key, CHUNK_SHIFT) <= lax.shift_right_logical(query, CHUNK_SHIFT)) & (key >= DEAD)


def _stat_lane(h):
    return (h // HEAD_GROUP) * LANES + h % HEAD_GROUP


def _scatter_stats(cols, t):
    lane = lax.broadcasted_iota(jnp.int32, (t, N_GROUPS * LANES), 1)
    out = jnp.zeros((t, N_GROUPS * LANES), F32)
    for h, col in enumerate(cols):
        out = jnp.where(lane == _stat_lane(h), col, out)
    return out


def _resident(shape, index_map):
    return pl.BlockSpec(shape, index_map, pipeline_mode=pl.Buffered(1))


def _attn_fwd(q, kv, kvt, kr, shards=()):
    L = q.shape[0]
    t = TM
    nq = L // t
    n = len(shards)
    pass_step = (5 * nq) // 6

    def body(q_ref, kv_ref, kvt_ref, kr_ref, *refs):
        gather_refs = refs[:n] + refs[n + 2:2 * n + 2] + refs[2 * n + 6:]
        o_ref, lse_ref = refs[n:n + 2]
        qt_scr, m_scr, l_scr, acc_scr = refs[2 * n + 2:2 * n + 6]
        i = pl.program_id(0)
        if n:
            @pl.when(i == 0)
            def _():
                for cp in _gather_copies(gather_refs[:n], gather_refs[n:2 * n], *gather_refs[2 * n:])[0]:
                    cp().start()

        lane = lax.broadcasted_iota(jnp.int32, (t, HB), 1)
        heads = range(N_HEADS)
        cols = [slice(h * HB, (h + 1) * HB) for h in heads]
        for h in heads:
            qt_scr[cols[h], :] = q_ref[:, cols[h]].astype(F32).T.astype(BF16)
        m_scr[...] = jnp.full_like(m_scr, NEG)
        l_scr[...] = jnp.zeros_like(l_scr)
        acc_scr[...] = jnp.zeros_like(acc_scr)

        def tile(keys, vis, whole=True):
            krj = kr_ref[keys, :]
            kvj = [kv_ref[keys, cols[h]] for h in heads]
            lane_k = lane[:krj.shape[0]]
            s = [_dot(jnp.where(lane_k < QK_NOPE, kvj[h], krj), qt_scr[cols[h], :], NN) for h in heads]
            p, alpha = [], []
            for h in heads:
                sh = s[h] if vis is None else jnp.where(vis, s[h], NEG)
                m_prev = m_scr[h:h + 1, :]
                m_new = jnp.maximum(m_prev, jnp.max(sh, 0, keepdims=True))
                a = jnp.exp2(m_prev - m_new)
                ph = jnp.exp2(sh - m_new)
                l_scr[h:h + 1, :] = a * l_scr[h:h + 1, :] + jnp.sum(ph, 0, keepdims=True)
                m_scr[h:h + 1, :] = m_new
                p.append(ph.astype(BF16))
                alpha.append(a)
            for h in heads:
                pv = _dot(kvt_ref[cols[h], keys], p[h], NN) if whole else _dot(kvj[h], p[h], TN)
                acc_scr[cols[h], :] = alpha[h] * acc_scr[cols[h], :] + pv

        tile(pl.ds(pl.multiple_of(i * t, t), t), _visible_t(i, i, t))

        @pl.when(i > 0)
        def _():
            tile(pl.ds(DEAD, N_META), None, whole=False)

        def unmasked(j, carry):
            tile(pl.ds(pl.multiple_of(j * t, t), t), None)
            return carry

        lax.fori_loop(1, i, unmasked, 0)
        lse_ref[...] = jnp.zeros_like(lse_ref)
        for h in heads:
            l = l_scr[h:h + 1, :]
            o_ref[:, cols[h]] = jnp.where(lane >= QK_NOPE, (acc_scr[cols[h], :] / l).T, 0.0).astype(BF16)
            lse_ref[h // HEAD_GROUP, h % HEAD_GROUP:h % HEAD_GROUP + 1, :] = m_scr[h:h + 1, :] + jnp.log2(l)
        if n:
            @pl.when(i == pass_step)
            def _():
                _, arrivals, forwards, _ = _gather_copies(gather_refs[:n], gather_refs[n:2 * n], *gather_refs[2 * n:])
                for landed, onward in zip(arrivals, forwards):
                    landed().wait_recv()
                    onward().start()

            @pl.when(i == nq - 1)
            def _():
                sends, _, forwards, finals = _gather_copies(gather_refs[:n], gather_refs[n:2 * n], *gather_refs[2 * n:])
                for cp in finals:
                    cp().wait_recv()
                for cp in sends + forwards:
                    cp().wait_send()

    any_spec = pl.BlockSpec(memory_space=pl.ANY)
    outs = pl.pallas_call(
        body, name="attn_fwd", grid=(nq,),
        in_specs=[_rows(t, D_HEADS), _resident((L, D_HEADS), lambda i: (0, 0)),
                  _resident((D_HEADS, L), lambda i: (0, 0)), _resident((L, HB), lambda i: (0, 0))] + [any_spec] * n,
        out_specs=[_rows(t, D_HEADS), pl.BlockSpec((N_GROUPS, SUBLANES, t), lambda i: (0, 0, i))] + [any_spec] * n,
        out_shape=[_sds((L, D_HEADS), BF16), _sds((N_GROUPS, SUBLANES, L), F32)] + _gather_out_shapes(shards),
        scratch_shapes=[pltpu.VMEM((D_HEADS, t), BF16), pltpu.VMEM((N_HEADS, t), F32), pltpu.VMEM((N_HEADS, t), F32),
                        pltpu.VMEM((D_HEADS, t), F32)] + (_gather_semaphores(n) if n else []),
        compiler_params=_params(1))(q, kv, kvt, kr, *shards)
    return outs[0], outs[1], _gathered(outs[2:], shards)


def _fwd_mix(h0, mixa, o, ga, wout, gffn):
    L = h0.shape[0]

    def body(h0_ref, mixa_ref, o_ref, ga_ref, wout_ref, gffn_ref, mix_ref, h1_ref, n2_ref):
        of = o_ref[...].astype(F32)
        mixb = (of * _rms_r(of, N_HEADS * V_HEAD) * ga_ref[...]).astype(BF16)
        mix = jnp.concatenate([mixa_ref[...], mixb], 1)
        mix_ref[...] = mix
        mo = jnp.where(_row_ids(TM) >= DEAD, _dot(mix, wout_ref[...], NN), 0.0)
        h1 = h0_ref[...] + mo
        h1_ref[...] = h1
        n2_ref[...] = (h1 * _rms_r(h1, D_MODEL) * gffn_ref[...]).astype(BF16)

    return pl.pallas_call(
        body, name="fwd_mix", grid=(L // TM,),
        in_specs=[_rows(TM, D_MODEL), _rows(TM, D_CONV), _rows(TM, D_HEADS), _full(ga.shape), _full(wout.shape),
                  _full(gffn.shape)],
        out_specs=[_rows(TM, D_MIX), _rows(TM, D_MODEL), _rows(TM, D_MODEL)],
        out_shape=[_sds((L, D_MIX), BF16), _sds((L, D_MODEL), F32), _sds((L, D_MODEL), BF16)],
        compiler_params=_params(1))(h0, mixa, o, ga, wout, gffn)


def _ffn_act_chunk(c, upg_ref, upv_ref, hg, hv, fcw_ref, fcb_ref):
    cs = slice(c * FF_CHUNK, (c + 1) * FF_CHUNK)
    out = []
    for part, (up_ref, halo) in enumerate(((upg_ref, hg), (upv_ref, hv))):
        xx = jnp.concatenate([halo[:, cs], up_ref[:, cs].astype(F32)], 0)
        ws = slice(part * D_FF + c * FF_CHUNK, part * D_FF + (c + 1) * FF_CHUNK)
        y = (fcw_ref[0:1, ws] * pltpu.roll(xx, 2, 0)[HALO_FFN:] + fcw_ref[1:2, ws] * pltpu.roll(xx, 1, 0)[HALO_FFN:]
             + fcw_ref[2:3, ws] * xx[HALO_FFN:] + fcb_ref[:, ws])
        out.append(y)
    return out


def _ffn_in_specs(L):
    return [_rows(TM, D_FF, 0), _rows(TM, D_FF, 1), _prev(HALO_FFN, D_FF, TM, 0), _prev(HALO_FFN, D_FF, TM, 1)]


def _ffn_halos(hg_ref, hv_ref):
    first = pl.program_id(0) == 0
    return (jnp.where(first, 0.0, hg_ref[...].astype(F32)), jnp.where(first, 0.0, hv_ref[...].astype(F32)))


def _fwd_ffn_loss(up0, fcw, fcb, wdown, h1, target, gfin):
    L = h1.shape[0]

    def body(upg_ref, upv_ref, hg_ref, hv_ref, fcw_ref, fcb_ref, wd_ref, h1_ref, t_ref, gf_ref,
             dh2_ref, loss_ref, dgf_ref, up_ref, act_ref):
        i = pl.program_id(0)
        hg, hv = _ffn_halos(hg_ref, hv_ref)
        for c in range(D_FF // FF_CHUNK):
            cs = slice(c * FF_CHUNK, (c + 1) * FF_CHUNK)
            g, val = _ffn_act_chunk(c, upg_ref, upv_ref, hg, hv, fcw_ref, fcb_ref)
            up_ref[:, cs] = g.astype(BF16)
            up_ref[:, D_FF + c * FF_CHUNK:D_FF + (c + 1) * FF_CHUNK] = val.astype(BF16)
            act_ref[:, cs] = (g * jax.nn.sigmoid(g) * val).astype(BF16)
        h2 = h1_ref[...] + _dot(act_ref[...], wd_ref[...], NN)
        r = _rms_r(h2, D_MODEL)
        gf = gf_ref[...]
        err = jnp.where(i > 0, h2 * r * gf - t_ref[...], 0.0)
        dy = err * (1.0 / D_MODEL)
        dh2, dgf = _rms_bwd(dy, h2, r, gf, D_MODEL)
        dh2_ref[...] = dh2

        @pl.when(i == 0)
        def _():
            loss_ref[...] = jnp.zeros_like(loss_ref)
            dgf_ref[...] = jnp.zeros_like(dgf_ref)

        loss_ref[...] += jnp.sum(err * err) * (0.5 / D_MODEL)
        dgf_ref[...] += dgf

    return pl.pallas_call(
        body, name="fwd_ffn_loss", grid=(L // TM,),
        in_specs=_ffn_in_specs(L) + [_full(fcw.shape), _full(fcb.shape), _full(wdown.shape), _rows(TM, D_MODEL),
                                     pl.BlockSpec((TM, D_MODEL), lambda i: (jnp.maximum(i - 1, 0), 0)),
                                     _full(gfin.shape)],
        out_specs=[_rows(TM, D_MODEL), _full((1, LANES)), _full((1, D_MODEL)), _rows(TM, 2 * D_FF), _rows(TM, D_FF)],
        out_shape=[_sds((L, D_MODEL), F32), _sds((1, LANES), F32), _sds((1, D_MODEL), F32),
                   _sds((L, 2 * D_FF), BF16), _sds((L, D_FF), BF16)],
        compiler_params=_params(1))(up0, up0, up0, up0, fcw, fcb, wdown, h1, target, gfin)


def _bwd_ffn_act(dh2, up, wdown):
    L = dh2.shape[0]

    def body(dh2_ref, upg_ref, upv_ref, wd_ref, dup_ref, dfcb_ref):
        da = _dot(dh2_ref[...].astype(BF16), wd_ref[...], NT)

        @pl.when(pl.program_id(0) == 0)
        def _():
            dfcb_ref[...] = jnp.zeros_like(dfcb_ref)

        for c in range(D_FF // FF_CHUNK):
            cs = slice(c * FF_CHUNK, (c + 1) * FF_CHUNK)
            vs = slice(D_FF + c * FF_CHUNK, D_FF + (c + 1) * FF_CHUNK)
            g, val = upg_ref[:, cs].astype(F32), upv_ref[:, cs].astype(F32)
            sg = jax.nn.sigmoid(g)
            si = g * sg
            dac = da[:, cs]
            dg = dac * val * (sg * (1.0 + g * (1.0 - sg)))
            dv = dac * si
            dup_ref[:, cs] = dg.astype(BF16)
            dup_ref[:, vs] = dv.astype(BF16)
            dfcb_ref[:, cs] += jnp.sum(dg, 0, keepdims=True)
            dfcb_ref[:, vs] += jnp.sum(dv, 0, keepdims=True)

    return pl.pallas_call(
        body, name="bwd_ffn_act", grid=(L // TM,),
        in_specs=[_rows(TM, D_MODEL), _rows(TM, D_FF, 0), _rows(TM, D_FF, 1), _full(wdown.shape)],
        out_specs=[_rows(TM, 2 * D_FF), _full((1, 2 * D_FF))],
        out_shape=[_sds((L, 2 * D_FF), BF16), _sds((1, 2 * D_FF), F32)],
        compiler_params=_params(1))(dh2, up, up, wdown)


def _bwd_ffn_conv_up(dup, up0, fcw, wup, tl):
    L, C = dup.shape
    tc = FF_TILE
    nt = L // tl
    nhb = L // HALO_FFN
    chunks = [(c0, min(FF_MXU_CHUNK, tc - c0)) for c0 in range(0, tc, FF_MXU_CHUNK)]

    def body(dy_ref, dyn_ref, x_ref, xp_ref, w_ref, wup_ref, dn2_ref, dx_ref, dw_ref, acc_ref):
        i, k = pl.program_id(0), pl.program_id(1)

        @pl.when(k == 0)
        def _():
            acc_ref[...] = jnp.zeros_like(acc_ref)

        last, first = i == nt - 1, i == 0
        for c0, cw in chunks:
            cs = slice(c0, c0 + cw)
            yy = jnp.concatenate([dy_ref[:, cs].astype(F32), jnp.where(last, 0.0, dyn_ref[:, cs].astype(F32))], 0)
            w = w_ref[:, cs]
            dx = (w[0:1] * pltpu.roll(yy, tl + HALO_FFN - 2, 0)[:tl] + w[1:2] * pltpu.roll(yy, tl + HALO_FFN - 1, 0)[:tl]
                  + w[2:3] * yy[:tl]).astype(BF16)
            dx_ref[:, cs] = dx
            acc_ref[...] += _dot(dx, wup_ref[:, cs], NT)
            xx = jnp.concatenate([jnp.where(first, 0.0, xp_ref[:, cs].astype(F32)), x_ref[:, cs].astype(F32)], 0)
            dy = yy[:tl]
            dw_ref[0:1, cs] = jnp.sum(dy * pltpu.roll(xx, 2, 0)[HALO_FFN:], 0, keepdims=True)
            dw_ref[1:2, cs] = jnp.sum(dy * pltpu.roll(xx, 1, 0)[HALO_FFN:], 0, keepdims=True)
            dw_ref[2:3, cs] = jnp.sum(dy * xx[HALO_FFN:], 0, keepdims=True)

        @pl.when(k == C // tc - 1)
        def _():
            dn2_ref[...] = acc_ref[...]

    tile = pl.BlockSpec((tl, tc), lambda i, k: (i, k))
    per = tl // HALO_FFN
    return pl.pallas_call(
        body, name="bwd_ffn_conv_up", grid=(nt, C // tc),
        in_specs=[tile, pl.BlockSpec((HALO_FFN, tc), lambda i, k: (jnp.minimum((i + 1) * per, nhb - 1), k)),
                  tile, pl.BlockSpec((HALO_FFN, tc), lambda i, k: (jnp.maximum(i * per - 1, 0), k)),
                  pl.BlockSpec((FFN_CONV_WIDTH, tc), lambda i, k: (0, k)),
                  pl.BlockSpec((None, D_MODEL, tc), lambda i, k: (k, 0, 0))],
        out_specs=[pl.BlockSpec((tl, D_MODEL), lambda i, k: (i, 0)), tile,
                   pl.BlockSpec((None, FFN_CONV_WIDTH, tc), lambda i, k: (i, 0, k))],
        out_shape=[_sds((L, D_MODEL), F32), _sds((L, C), BF16), _sds((nt, FFN_CONV_WIDTH, C), F32)],
        scratch_shapes=[pltpu.VMEM((tl, D_MODEL), F32)],
        compiler_params=_params(2))(dup, dup, up0, up0, fcw, wup)


def _carrying(core, n_in, n_out, n, copies_fn, steps):
    def body(*refs):
        exchange = (refs[n_in:n_in + n], refs[n_in + n + n_out:n_in + 2 * n + n_out]) + refs[n_in + 2 * n + n_out:]
        if n:
            @pl.when(pl.program_id(0) == 0)
            def _():
                for cp in copies_fn(*exchange):
                    cp().start()

        core(*refs[:n_in], *refs[n_in + n:n_in + n + n_out])
        if n:
            @pl.when(pl.program_id(0) == steps - 1)
            def _():
                for cp in copies_fn(*exchange):
                    cp().wait()

    return body


def _bwd_mix(dh2, dn2, h1, gffn, wout, o, ga, u1, lg, lb, og, parts=()):
    L = h1.shape[0]

    def body(dh2_ref, dn2_ref, h1_ref, gffn_ref, wout_ref, o_ref, ga_ref, u1_ref, lg_ref, lb_ref, og_ref,
             dh1_ref, dh1m_ref, do_ref, delta_ref, du1_ref, dgffn_ref, dga_ref, dog_ref, dlg_ref, dlb_ref):
        h1 = h1_ref[...]
        dn2x, dgffn = _rms_bwd(dn2_ref[...], h1, _rms_r(h1, D_MODEL), gffn_ref[...], D_MODEL)
        dh1 = dh2_ref[...] + dn2x
        dh1_ref[...] = dh1
        dh1m = jnp.where(_row_ids(TM) >= DEAD, dh1, 0.0).astype(BF16)
        dh1m_ref[...] = dh1m
        dmix = _dot(dh1m, wout_ref[...], NT)
        dma, dmb = dmix[:, :D_CONV], dmix[:, D_CONV:]
        of = o_ref[...].astype(F32)
        do, dga = _rms_bwd(dmb, of, _rms_r(of, N_HEADS * V_HEAD), ga_ref[...], N_HEADS * V_HEAD)
        do_ref[...] = do.astype(BF16)
        prod = do * of
        by_lane = _scatter_stats([jnp.sum(prod[:, h * HB:(h + 1) * HB], -1, keepdims=True) for h in range(N_HEADS)], TM)
        by_row = by_lane.T
        for grp in range(N_GROUPS):
            delta_ref[grp] = by_row[grp * LANES:grp * LANES + SUBLANES, :]
        lg = lg_ref[...]
        rs, xh, u2, sg, u = _ln_silu(u1_ref[...], lg, lb_ref[...])
        du, dog = _rms_bwd(dma, u, _rms_r(u, D_CONV), og_ref[...], D_CONV)
        du2 = du * (sg * (1.0 + u2 * (1.0 - sg)))
        dxh = du2 * lg
        du1_ref[...] = rs * (dxh - jnp.mean(dxh, -1, keepdims=True) - xh * jnp.mean(dxh * xh, -1, keepdims=True))

        @pl.when(pl.program_id(0) == 0)
        def _():
            for ref in (dgffn_ref, dga_ref, dog_ref, dlg_ref, dlb_ref):
                ref[...] = jnp.zeros_like(ref)

        dgffn_ref[...] += dgffn
        dga_ref[...] += dga
        dog_ref[...] += dog
        dlg_ref[...] += jnp.sum(du2 * xh, 0, keepdims=True)
        dlb_ref[...] += jnp.sum(du2, 0, keepdims=True)

    n = len(parts)
    any_spec = pl.BlockSpec(memory_space=pl.ANY)
    outs = pl.pallas_call(
        _carrying(body, 11, 10, n, _sibling_exchange_copies, L // TM), name="bwd_mix", grid=(L // TM,),
        in_specs=[_rows(TM, D_MODEL), _rows(TM, D_MODEL), _rows(TM, D_MODEL), _full(gffn.shape), _full(wout.shape),
                  _rows(TM, D_HEADS), _full(ga.shape), _rows(TM, D_CONV), _full(lg.shape), _full(lb.shape),
                  _full(og.shape)] + [any_spec] * n,
        out_specs=[_rows(TM, D_MODEL), _rows(TM, D_MODEL), _rows(TM, D_HEADS),
                   pl.BlockSpec((N_GROUPS, SUBLANES, TM), lambda i: (0, 0, i)),
                   _rows(TM, D_CONV), _full((1, D_MODEL)), _full((1, D_HEADS)), _full((1, D_CONV)),
                   _full((1, D_CONV)), _full((1, D_CONV))] + [any_spec] * n,
        out_shape=[_sds((L, D_MODEL), F32), _sds((L, D_MODEL), BF16), _sds((L, D_HEADS), BF16),
                   _sds((N_GROUPS, SUBLANES, L), F32),
                   _sds((L, D_CONV), F32), _sds((1, D_MODEL), F32), _sds((1, D_HEADS), F32), _sds((1, D_CONV), F32),
                   _sds((1, D_CONV), F32), _sds((1, D_CONV), F32)] + _sibling_exchange_shapes(parts),
        scratch_shapes=_sibling_exchange_semaphores(n) if n else [],
        compiler_params=_params(1))(dh2, dn2, h1, gffn, wout, o, ga, u1, lg, lb, og, *parts)
    return outs[:10], list(outs[10:])


def _attn_bwd(q, kv, kr, do, lse, delta, parts=()):
    L = q.shape[0]
    t = TM
    nt = L // t
    gw = HEAD_GROUP * HB
    n = len(parts)

    def body(q_ref, kv_ref, kr_ref, do_ref, lse_ref, delta_ref, *refs):
        dq_ref, dkv_ref, dkr_ref = refs[n:n + 3]
        dqt_acc, dk_acc, dv_acc, kkt_scr = refs[2 * n + 3:2 * n + 7]
        exchange_refs = (refs[:n], refs[n + 3:2 * n + 3]) + refs[2 * n + 7:]
        g, j = pl.program_id(0), pl.program_id(1)
        if n:
            @pl.when((g == 0) & (j == 0))
            def _():
                for cp in _chip_copies(*exchange_refs)[0]:
                    cp().start()

        lane = lax.broadcasted_iota(jnp.int32, (t, HB), 1)

        @pl.when(j == 0)
        def _():
            dqt_acc[...] = jnp.zeros_like(dqt_acc)

        @pl.when((j == 0) & (g == 0))
        def _():
            dkr_ref[...] = jnp.zeros_like(dkr_ref)

        dk_acc[...] = jnp.zeros_like(dk_acc)
        dv_acc[...] = jnp.zeros_like(dv_acc)
        krj = kr_ref[...]
        heads = range(HEAD_GROUP)
        cols = [slice(h * HB, (h + 1) * HB) for h in heads]
        for hc in cols:
            kkt_scr[hc, :] = jnp.where(lane < QK_NOPE, kv_ref[:, hc], krj).astype(F32).T.astype(BF16)

        def tile(i, vis, whole=True):
            qs = pl.ds(pl.multiple_of(i * t, t), t)
            keys = slice(None) if whole else slice(DEAD, t)
            kvj = [kv_ref[keys, hc] for hc in cols]
            lane_k = lane[:kvj[0].shape[0]]
            kk = [jnp.where(lane_k < QK_NOPE, kvj[h], krj[keys]) for h in heads]
            qi = [q_ref[qs, hc] for hc in cols]
            doi = [do_ref[qs, hc] for hc in cols]
            s = [_dot(kk[h], qi[h].astype(F32).T.astype(BF16), NN) for h in heads]
            dp = [_dot(kvj[h], doi[h].astype(F32).T.astype(BF16), NN) for h in heads]
            p = []
            for h in heads:
                sh = s[h] if vis is None else jnp.where(vis, s[h], NEG)
                p.append(jnp.exp2(sh - lse_ref[h:h + 1, qs]))
            for h in heads:
                dv_acc[keys, cols[h]] += _dot(p[h].astype(BF16), doi[h], NN)
            ds = [(p[h] * (dp[h] - delta_ref[h:h + 1, qs]) * LN2).astype(BF16) for h in heads]
            for h in heads:
                dk_acc[keys, cols[h]] += _dot(ds[h], qi[h], NN)
            for h in heads:
                dqt = _dot(kkt_scr[cols[h], :], ds[h], NN) if whole else _dot(kk[h], ds[h], TN)
                dqt_acc[cols[h], qs] += dqt

        @pl.when(j == 0)
        def _():
            tile(0, _visible_t(0, 0, t)[DEAD:], whole=False)

            def meta_keys(i, carry):
                tile(i, None, whole=False)
                return carry

            lax.fori_loop(1, nt, meta_keys, 0)

        @pl.when(j > 0)
        def _():
            tile(j, _visible_t(j, j, t))

            def unmasked(i, carry):
                tile(i, None)
                return carry

            lax.fori_loop(j + 1, nt, unmasked, 0)

        dkr = jnp.zeros((t, HB), F32)
        for h in range(HEAD_GROUP):
            hc = slice(h * HB, (h + 1) * HB)
            dk = dk_acc[:, hc]
            dkv_ref[:, hc] = jnp.where(lane < QK_NOPE, dk, dv_acc[:, hc]).astype(BF16)
            dkr = dkr + jnp.where(lane >= QK_NOPE, dk, 0.0)
        dkr_ref[pl.ds(pl.multiple_of(j * t, t), t), :] += dkr

        @pl.when(j == nt - 1)
        def _():
            def untranspose(i, carry):
                qs = pl.ds(pl.multiple_of(i * t, t), t)
                dq_ref[qs, :] = (dqt_acc[:, qs].T * Q_SCALE).astype(BF16)
                return carry

            lax.fori_loop(0, nt, untranspose, 0)

        if n:
            @pl.when((g == N_GROUPS - 1) & (j == nt - 1))
            def _():
                sends, arrivals = _chip_copies(*exchange_refs)
                for cp in arrivals:
                    cp().wait_recv()
                for cp in sends:
                    cp().wait_send()

    group = lambda g, j: (0, g)
    stats = _resident((None, SUBLANES, L), lambda g, j: (g, 0, 0))
    any_spec = pl.BlockSpec(memory_space=pl.ANY)
    outs = pl.pallas_call(
        body, name="attn_bwd", grid=(N_GROUPS, nt),
        in_specs=[_resident((L, gw), group), pl.BlockSpec((t, gw), lambda g, j: (j, g)),
                  pl.BlockSpec((t, HB), lambda g, j: (j, 0)), _resident((L, gw), group), stats, stats]
        + [any_spec] * n,
        out_specs=[pl.BlockSpec((L, gw), group), pl.BlockSpec((t, gw), lambda g, j: (j, g)),
                   pl.BlockSpec((L, HB), lambda g, j: (0, 0))] + [any_spec] * n,
        out_shape=[_sds((L, D_HEADS), BF16), _sds((L, D_HEADS), BF16), _sds((L, HB), F32)]
        + [_sds(p.shape, p.dtype) for p in parts],
        scratch_shapes=[pltpu.VMEM((gw, L), F32), pltpu.VMEM((t, gw), F32), pltpu.VMEM((t, gw), F32),
                        pltpu.VMEM((gw, t), BF16)] + (_chip_semaphores(n) if n else []),
        compiler_params=_params(2))(q, kv, kr, do, lse, delta, *parts)
    return outs[0], outs[1], outs[2], list(outs[3:])


def _bwd_conv(du1, u0, cw, zag, halves=()):
    L = du1.shape[0]
    nt = L // TM

    def body(dy_ref, dyn_ref, x_ref, xp_ref, cw_ref, zag_ref, dzag_ref, dcw_ref, dcb_ref):
        i = pl.program_id(0)
        dy = dy_ref[...]
        yy = jnp.concatenate([dy, jnp.where(i < nt - 1, dyn_ref[...], 0.0)], 0)
        du0 = _conv_taps(yy, cw_ref, HALO_CONV, TM, True)
        zag = zag_ref[...].astype(F32)
        a, sg = zag[:, :D_CONV], jax.nn.sigmoid(zag[:, D_CONV:])
        dzag_ref[...] = jnp.concatenate([du0 * sg, du0 * a * sg * (1.0 - sg)], 1).astype(BF16)
        xx = jnp.concatenate([jnp.where(i > 0, xp_ref[...], 0.0), x_ref[...]], 0)

        @pl.when(i == 0)
        def _():
            dcw_ref[...] = jnp.zeros_like(dcw_ref)
            dcb_ref[...] = jnp.zeros_like(dcb_ref)

        rolls = {}
        for k in range(CONV_WIDTH):
            xs = _shifted_rows(xx, CONV_WIDTH - 1 - k, HALO_CONV, TM, False, rolls)
            dcw_ref[k:k + 1, :] += jnp.sum(dy * xs, 0, keepdims=True)
        dcb_ref[...] += jnp.sum(dy, 0, keepdims=True)

    n = len(halves)
    any_spec = pl.BlockSpec(memory_space=pl.ANY)
    outs = pl.pallas_call(
        _carrying(body, 6, 3, n, _sibling_gather_copies, nt), name="bwd_conv", grid=(nt,),
        in_specs=[_rows(TM, D_CONV), _next(HALO_CONV, D_CONV, TM, L // HALO_CONV), _rows(TM, D_CONV),
                  _prev(HALO_CONV, D_CONV, TM), _full(cw.shape), _rows(TM, D_AG)] + [any_spec] * n,
        out_specs=[_rows(TM, D_AG), _full(cw.shape), _full((1, D_CONV))] + [any_spec] * n,
        out_shape=[_sds((L, D_AG), BF16), _sds(cw.shape, F32), _sds((1, D_CONV), F32)]
        + [_sds(p.shape, p.dtype) for p in halves],
        input_output_aliases={6 + a: 3 + a for a in range(n)},
        scratch_shapes=_sibling_gather_semaphores(n) if n else [],
        compiler_params=_params(1))(du1, du1, u0, u0, cw, zag, *halves)
    return outs[:3], list(outs[3:])


def _bwd_in(dzag, dq, dkv, dkr, cq, ckv, gq, gkv, wuq, wukv, win, rc, rs1, rs2, h0, gmix, dh1):
    L = h0.shape[0]

    def body(dzag_ref, dq_ref, dkv_ref, dkr_ref, cq_ref, ckv_ref, gq_ref, gkv_ref, wuq_ref, wukv_ref, win_ref,
             c_ref, s1_ref, s2_ref, h0_ref, gmix_ref, dh1_ref,
             dz_ref, dqr_ref, gx_ref, dfirst_ref, dgq_ref, dgkv_ref, dgmix_ref):
        i = pl.program_id(0)
        c, s1, s2 = c_ref[...], s1_ref[...], s2_ref[...]
        dqr = _rope_bwd(dq_ref[...].astype(F32), jnp.tile(c, (1, N_HEADS)), jnp.tile(s1, (1, N_HEADS)),
                        jnp.tile(s2, (1, N_HEADS))).astype(BF16)
        dqr_ref[...] = dqr
        cq, ckv = cq_ref[...], ckv_ref[...]
        dcq, dgq = _rms_bwd(_dot(dqr, wuq_ref[...], NT), cq, _rms_r(cq, Q_LORA), gq_ref[...], Q_LORA)
        dckv, dgkv = _rms_bwd(_dot(dkv_ref[...], wukv_ref[...], NT), ckv, _rms_r(ckv, KV_LORA), gkv_ref[...], KV_LORA)
        dkrp = _rope_bwd(dkr_ref[...], c, s1, s2)
        dz = jnp.concatenate([dzag_ref[...], dcq.astype(BF16), dckv.astype(BF16), dkrp.astype(BF16)], 1)
        dz_ref[...] = dz
        h0 = h0_ref[...]
        dnx, dgmix = _rms_bwd(_dot(dz, win_ref[...], NT), h0, _rms_r(h0, D_MODEL), gmix_ref[...], D_MODEL)
        dh0 = dh1_ref[...] + dnx

        @pl.when(i == 0)
        def _():
            dfirst_ref[...] = dh0
            for ref in (dgq_ref, dgkv_ref, dgmix_ref):
                ref[...] = jnp.zeros_like(ref)

        @pl.when(i > 0)
        def _():
            gx_ref[...] = dh0

        dgq_ref[...] += dgq
        dgkv_ref[...] += dgkv
        dgmix_ref[...] += dgmix

    return pl.pallas_call(
        body, name="bwd_in", grid=(L // TM,),
        in_specs=[_rows(TM, D_AG), _rows(TM, D_HEADS), _rows(TM, D_HEADS), _rows(TM, HB), _rows(TM, Q_LORA),
                  _rows(TM, KV_LORA), _full(gq.shape), _full(gkv.shape), _full(wuq.shape), _full(wukv.shape),
                  _full(win.shape), _rows(TM, HB), _rows(TM, HB), _rows(TM, HB), _rows(TM, D_MODEL),
                  _full(gmix.shape), _rows(TM, D_MODEL)],
        out_specs=[_rows(TM, D_ZP), _rows(TM, D_HEADS),
                   pl.BlockSpec((TM, D_MODEL), lambda i: (jnp.maximum(i - 1, 0), 0)), _full((TM, D_MODEL)),
                   _full((1, Q_LORA)), _full((1, KV_LORA)), _full((1, D_MODEL))],
        out_shape=[_sds((L, D_ZP), BF16), _sds((L, D_HEADS), BF16), _sds((L - TM, D_MODEL), F32),
                   _sds((TM, D_MODEL), F32), _sds((1, Q_LORA), F32), _sds((1, KV_LORA), F32), _sds((1, D_MODEL), F32)],
        compiler_params=_params(1))(dzag, dq, dkv, dkr, cq, ckv, gq, gkv, wuq, wukv, win, rc, rs1, rs2, h0, gmix, dh1)


def _mesh_pos():
    return lax.axis_index("x"), lax.axis_index("y"), lax.axis_index("c")


def _remote_copy(src, dst, send_sem, recv_sem, to):
    return functools.partial(pltpu.make_async_remote_copy, src, dst, send_sem, recv_sem, device_id=to,
                             device_id_type=MESH)


def _all_gather(shards):
    n = len(shards)

    def body(*refs):
        sends, arrivals, forwards, finals = _gather_copies(refs[:n], refs[n:2 * n], *refs[2 * n:])
        for cp in sends:
            cp().start()
        for landed, onward in zip(arrivals, forwards):
            landed().wait_recv()
            onward().start()
        for cp in finals:
            cp().wait_recv()
        for cp in sends + forwards:
            cp().wait_send()

    any_spec = pl.BlockSpec(memory_space=pl.ANY)
    outs = pl.pallas_call(
        body, name="all_gather_weights", in_specs=[any_spec] * n, out_specs=[any_spec] * n,
        out_shape=_gather_out_shapes(shards), scratch_shapes=_gather_semaphores(n))(*shards)
    return _gathered(outs, shards)


def _gather_out_shapes(shards):
    return [_sds((2 * N_CHIPS, s.shape[0] // 2) + s.shape[1:], s.dtype) for s in shards]


def _gather_semaphores(n):
    return [pltpu.SemaphoreType.DMA((n, 8)), pltpu.SemaphoreType.DMA((n, 8))]


def _gathered(outs, shards):
    return [o.reshape((N_CHIPS, s.shape[0]) + s.shape[1:]) for o, s in zip(outs, shards)]


def _gather_copies(ins, outs, send_sems, recv_sems):
    x, y, c = _mesh_pos()
    chips = [(1 - x, y), (x, 1 - y), (1 - x, 1 - y)]
    sends, arrivals, forwards, finals = [], [], [], []

    def copy(src, dst, a, k, to):
        return _remote_copy(src, dst, send_sems.at[a, k], recv_sems.at[a, k], to)

    for a, (src, out) in enumerate(zip(ins, outs)):
        m = out.shape[1]
        mine = src.at[pl.ds(pl.multiple_of(c * m, 16), m)]
        for hf in range(2):
            own = out.at[4 * x + 2 * y + hf]
            sends.append(copy(src.at[pl.ds(hf * m, m)], own, a, 6 + hf, (x, y, 1 - c)))
            finals.append(copy(own, own, a, 6 + hf, (x, y, 1 - c)))
        for k, chip in enumerate(chips):
            slot = 4 * chip[0] + 2 * chip[1]
            sends.append(copy(mine, out.at[4 * x + 2 * y + c], a, k, (*chip, c)))
            arrivals.append(copy(out.at[slot + c], out.at[slot + c], a, k, (*chip, c)))
            forwards.append(copy(out.at[slot + c], out.at[slot + c], a, 3 + k, (x, y, 1 - c)))
            finals.append(copy(out.at[slot + 1 - c], out.at[slot + 1 - c], a, 3 + k, (x, y, 1 - c)))
    return sends, arrivals, forwards, finals


def _sibling_exchange(parts, name):
    n = len(parts)

    def body(*refs):
        copies = _sibling_exchange_copies(refs[:n], refs[n:2 * n], *refs[2 * n:])
        for cp in copies:
            cp().start()
        for cp in copies:
            cp().wait()

    any_spec = pl.BlockSpec(memory_space=pl.ANY)
    return pl.pallas_call(
        body, name=name, in_specs=[any_spec] * n, out_specs=[any_spec] * n,
        out_shape=_sibling_exchange_shapes(parts), scratch_shapes=_sibling_exchange_semaphores(n))(*parts)


def _sibling_exchange_shapes(parts):
    return [_sds((N_CHIPS, p.shape[1] // 2, p.shape[2]), p.dtype) for p in parts]


def _sibling_exchange_semaphores(n):
    return [pltpu.SemaphoreType.DMA((n, N_CHIPS)), pltpu.SemaphoreType.DMA((n, N_CHIPS))]


def _sibling_exchange_copies(ins, theirs, send_sems, recv_sems):
    x, y, c = _mesh_pos()
    copies = []
    for a, (src, dst) in enumerate(zip(ins, theirs)):
        h = dst.shape[1]
        rows = pl.ds(pl.multiple_of((1 - c) * h, 16), h)
        copies += [_remote_copy(src.at[q, rows], dst.at[q], send_sems.at[a, q], recv_sems.at[a, q], (x, y, 1 - c))
                   for q in range(N_CHIPS)]
    return copies


def _chip_semaphores(n):
    return [pltpu.SemaphoreType.DMA((n, 3)), pltpu.SemaphoreType.DMA((n, 3))]


def _chip_copies(ins, outs, send_sems, recv_sems):
    x, y, c = _mesh_pos()
    me = 2 * x + y
    sends, arrivals = [], []
    for a, (src, out) in enumerate(zip(ins, outs)):
        for k, chip in enumerate([(1 - x, y), (x, 1 - y), (1 - x, 1 - y)]):
            slot = 2 * chip[0] + chip[1]
            sems = (send_sems.at[a, k], recv_sems.at[a, k], (*chip, c))
            sends.append(_remote_copy(src.at[slot], out.at[me], *sems))
            arrivals.append(_remote_copy(out.at[slot], out.at[slot], *sems))
    return sends, arrivals


def _sibling_gather(parts, name):
    n = len(parts)

    def body(*refs):
        copies = _sibling_gather_copies(refs[:n], refs[n:2 * n], *refs[2 * n:])
        for cp in copies:
            cp().start()
        for cp in copies:
            cp().wait()

    any_spec = pl.BlockSpec(memory_space=pl.ANY)
    return pl.pallas_call(
        body, name=name, in_specs=[any_spec] * n, out_specs=[any_spec] * n,
        out_shape=[_sds(p.shape, p.dtype) for p in parts], input_output_aliases={a: a for a in range(n)},
        scratch_shapes=_sibling_gather_semaphores(n))(*parts)


def _sibling_gather_semaphores(n):
    return [pltpu.SemaphoreType.DMA((n,)), pltpu.SemaphoreType.DMA((n,))]


def _sibling_gather_copies(ins, outs, send_sems, recv_sems):
    x, y, c = _mesh_pos()
    return [_remote_copy(src.at[c], dst.at[c], send_sems.at[a], recv_sems.at[a], (x, y, 1 - c))
            for a, (src, dst) in enumerate(zip(ins, outs))]


def _row_tile(rows, row_bytes, align, budget=1 << 20):
    best = None
    for t in range(align, rows + 1, align):
        if rows % t == 0 and t * row_bytes <= budget:
            best = t
    return best or rows


def _scalar(v):
    return jnp.reshape(v, (1,)).astype(jnp.int32)


def _add_pair(part, theirs, c, name):
    _, h, cols = theirs.shape
    tr = _row_tile(h, cols * 4, 16, budget=1 << 21)
    nb = h // tr

    def body(c_ref, a_ref, b_ref, o_ref):
        o_ref[...] = (a_ref[...].astype(F32) + b_ref[...].astype(F32)).astype(o_ref.dtype)

    half = pl.BlockSpec((None, tr, cols), lambda q, i, c_ref: (q, i, 0))
    grid_spec = pltpu.PrefetchScalarGridSpec(
        num_scalar_prefetch=1, grid=(N_CHIPS, nb),
        in_specs=[pl.BlockSpec((None, tr, cols), lambda q, i, c_ref: (q, c_ref[0] * nb + i, 0)), half],
        out_specs=half)
    return pl.pallas_call(body, name=name, grid_spec=grid_spec, out_shape=_sds(theirs.shape, part.dtype),
                          compiler_params=_params(2))(_scalar(c), part, theirs)


def _add_chips(got, own, me, c, name):
    _, h, cols = got.shape
    tr = _row_tile(h, cols * 4 * N_CHIPS, 16, budget=1 << 22)

    def body(pos_ref, got_ref, own_ref, o_ref):
        acc = None
        for q in range(N_CHIPS):
            term = jnp.where(pos_ref[0] == q, own_ref[q], got_ref[q]).astype(F32)
            acc = term if acc is None else acc + term
        o_ref[...] = acc

    by_chip = pl.BlockSpec((N_CHIPS, tr, cols), lambda i, pos_ref: (0, i, 0))
    grid_spec = pltpu.PrefetchScalarGridSpec(
        num_scalar_prefetch=1, grid=(h // tr,), in_specs=[by_chip, by_chip],
        out_specs=pl.BlockSpec((None, tr, cols), lambda i, pos_ref: (pos_ref[1], i, 0)))
    return pl.pallas_call(body, name=name, grid_spec=grid_spec, out_shape=_sds((2, h, cols), F32),
                          compiler_params=_params(1))(jnp.stack([me, c]).astype(jnp.int32), got, own)


def _add_pairs(parts, theirs, tag):
    c = lax.axis_index("c")
    return [_add_pair(p, t, c, f"grad_add_pair_{tag}_{a}") for a, (p, t) in enumerate(zip(parts, theirs))]


def _add_all_chips(pair, got, tag):
    x, y, c = _mesh_pos()
    return [_add_chips(g, p, 2 * x + y, c, f"grad_add_chips_{tag}_{a}") for a, (g, p) in enumerate(zip(got, pair))]


def _totals(both):
    return [b.reshape(-1, b.shape[-1]) for b in both]


def _reduce_begin(parts, tag):
    return _add_pairs(parts, _sibling_exchange(parts, f"grad_sibling_exchange_{tag}"), tag)


def _reduce_end(pair, got, tag):
    return _totals(_sibling_gather(_add_all_chips(pair, got, tag), f"grad_sibling_gather_{tag}"))


def _adamw_math(w, g, m, v):
    m = ADAM_B1 * m + (1.0 - ADAM_B1) * g
    v = ADAM_B2 * v + (1.0 - ADAM_B2) * (g * g)
    m_hat = m / (1.0 - ADAM_B1 ** ADAM_STEP)
    v_hat = v / (1.0 - ADAM_B2 ** ADAM_STEP)
    return -ADAM_LR * (m_hat / (jnp.sqrt(v_hat) + ADAM_EPS) + ADAM_WD * w), m, v


def _adamw_big(w, g, m, v, name, parts=()):
    r, c = w.shape
    tr = _row_tile(r, c * 4, 8, budget=1 << 19)
    if r // tr > ADAMW_MAX_STEPS:
        tr = r
    n = len(parts)
    steps = r // tr

    def body(w_ref, g_ref, m_ref, v_ref, *refs):
        go_ref, d_ref, mo_ref, vo_ref = refs[n:n + 4]
        exchange_refs = (refs[:n], refs[n + 4:2 * n + 4]) + refs[2 * n + 4:]
        if n:
            @pl.when(pl.program_id(0) == 0)
            def _():
                for cp in _chip_copies(*exchange_refs)[0]:
                    cp().start()

        g = g_ref[...]
        go_ref[...] = g
        d_ref[...], mo_ref[...], vo_ref[...] = _adamw_math(w_ref[...], g, m_ref[...], v_ref[...])
        if n:
            @pl.when(pl.program_id(0) == steps - 1)
            def _():
                sends, arrivals = _chip_copies(*exchange_refs)
                for cp in arrivals:
                    cp().wait_recv()
                for cp in sends:
                    cp().wait_send()

    any_spec = pl.BlockSpec(memory_space=pl.ANY)
    outs = pl.pallas_call(
        body, name=name, grid=(steps,), in_specs=[_rows(tr, c)] * 4 + [any_spec] * n,
        out_specs=[_rows(tr, c)] * 4 + [any_spec] * n,
        out_shape=[_sds((r, c), F32)] * 4 + [_sds(p.shape, p.dtype) for p in parts],
        scratch_shapes=_chip_semaphores(n) if n else [], compiler_params=_params(1))(w, g, m, v, *parts)
    return outs[:4], list(outs[4:])


def _adamw_small(ws, gs, ms, vs):
    n = len(ws)

    def body(*refs):
        for a in range(n):
            w_ref, g_ref, m_ref, v_ref = (refs[k * n + a] for k in range(4))
            d, m, v = _adamw_math(w_ref[...], g_ref[...], m_ref[...], v_ref[...])
            refs[4 * n + a][...] = d
            refs[5 * n + a][...] = m
            refs[6 * n + a][...] = v

    vm = pl.BlockSpec(memory_space=pltpu.VMEM)
    outs = pl.pallas_call(
        body, name="adamw_small", in_specs=[vm] * (4 * n), out_specs=[vm] * (3 * n),
        out_shape=[_sds(w.shape, F32) for w in ws] * 3)(*ws, *gs, *ms, *vs)
    return outs[:n], outs[n:2 * n], outs[2 * n:]


BIG = ("w_in", "w_uq", "w_ukv", "w_out", "w_ffn_up", "w_ffn_down")
SMALL_SHARDED = ("conv_w", "ffn_conv_w", "meta_tokens")
REPLICATED = ("mix_norm_g", "q_norm_g", "kv_norm_g", "conv_b", "conv_ln_g", "conv_ln_b", "conv_out_g", "attn_out_g",
              "ffn_norm_g", "ffn_conv_b", "final_norm_g")
WEIGHTS = ("meta_tokens", "mix_norm_g", "w_in", "q_norm_g", "w_uq", "kv_norm_g", "w_ukv", "conv_w", "conv_b",
           "conv_ln_g", "conv_ln_b", "conv_out_g", "attn_out_g", "w_out", "ffn_norm_g", "w_ffn_up", "ffn_conv_w",
           "ffn_conv_b", "w_ffn_down", "final_norm_g")


def _lane_rows(a):
    return a.reshape(N_CHIPS, -1, LANES)


def _col_shards(a):
    k = a.shape[0]
    return a.reshape(k, N_CHIPS, -1).transpose(1, 0, 2)


def _from_col_shards(a):
    return a.transpose(1, 0, 2).reshape(a.shape[1], -1)


def _pad_rows_to(a, rows):
    return jnp.pad(a, ((0, 0), (0, rows - a.shape[1]), (0, 0)))


def _rope_tables(L):
    pos = (jnp.arange(L, dtype=jnp.int32) - DEAD).astype(F32)
    inv_freq = 1.0 / (ROPE_THETA ** (jnp.arange(0, QK_ROPE, 2, dtype=F32) / QK_ROPE))
    ang = pos[:, None] * inv_freq[None, :]
    cos, sin = jnp.cos(ang), jnp.sin(ang)
    half = QK_ROPE // 2
    z = lambda n: jnp.zeros((L, n), F32)
    rc = jnp.concatenate([jnp.ones((L, QK_NOPE), F32), cos, cos, z(HB - QK_NOPE - QK_ROPE)], 1)
    rs1 = jnp.concatenate([z(QK_NOPE), -sin, z(HB - QK_NOPE - half)], 1)
    rs2 = jnp.concatenate([z(QK_NOPE + half), sin, z(HB - QK_NOPE - QK_ROPE)], 1)
    return rc, rs1, rs2


def _pad_heads(g):
    return jnp.pad(g.reshape(N_HEADS, V_HEAD), ((0, 0), (HB - V_HEAD, 0))).reshape(1, D_HEADS)


def _unpad_heads(g):
    return g.reshape(N_HEADS, HB)[:, HB - V_HEAD:].reshape(1, N_HEADS * V_HEAD)


def _local_step(x, target, w, late_shards=None, reduce_first=False):
    S = x.shape[0]
    L = TM + S
    tl = L // 4
    d_qk = QK_NOPE + QK_ROPE
    win_n = w["w_in"]
    kr0 = D_AG + Q_LORA + KV_LORA
    win = jnp.concatenate([win_n[:, :kr0], jnp.zeros((D_MODEL, QK_NOPE), BF16), win_n[:, kr0:],
                           jnp.zeros((D_MODEL, HB - d_qk), BF16)], 1)
    wuq = jnp.pad(w["w_uq"].reshape(Q_LORA, N_HEADS, d_qk), ((0, 0), (0, 0), (0, HB - d_qk))).reshape(Q_LORA, D_HEADS)
    wukv = w["w_ukv"]
    ga = _pad_heads(w["attn_out_g"])
    gfin = w["final_norm_g"].reshape(1, D_MODEL)
    rc, rs1, rs2 = _rope_tables(L)
    head = jnp.concatenate([jnp.zeros((DEAD, D_MODEL), F32), w["meta_tokens"]], 0)

    h0, n, zag, u0, cq, ckv, qn, kvn, q, kv, kr, kvt = _fwd_in(x, head, w["mix_norm_g"], win, w["q_norm_g"], wuq,
                                                                w["kv_norm_g"], wukv, rc, rs1, rs2)
    u1, mixa = _fwd_conv(u0, w["conv_w"], w["conv_b"], w["conv_ln_g"], w["conv_ln_b"], w["conv_out_g"])
    if late_shards is None:
        o, lse, _ = _attn_fwd(q, kv, kvt, kr)
        wout_n, wup, wdown = w["w_out"], _col_shards(w["w_ffn_up"]), w["w_ffn_down"]
    else:
        o, lse, late = _attn_fwd(q, kv, kvt, kr, [late_shards[k] for k in LATE])
        wout_n, wup, wdown = _full_weight("w_out", late[0]), late[1], _full_weight("w_ffn_down", late[2])
    wout = jnp.concatenate([wout_n[:D_CONV], jnp.pad(wout_n[D_CONV:].reshape(N_HEADS, V_HEAD, D_MODEL),
                                                     ((0, 0), (HB - V_HEAD, 0), (0, 0))).reshape(D_HEADS, D_MODEL)], 0)
    mix, h1, n2 = _fwd_mix(h0, mixa, o, ga, wout, w["ffn_norm_g"])
    up0 = _mm(n2, wup, NN, BF16, tl, FF_TILE, D_MODEL, "ffn_up", b_slabs=True)
    dh2, loss, g_fin, up, act = _fwd_ffn_loss(up0, w["ffn_conv_w"], w["ffn_conv_b"], wdown, h1, target, gfin)

    dup, g_fcb = _bwd_ffn_act(dh2, up, wdown)
    dn2, dup0, g_fcw_tiles = _bwd_ffn_conv_up(dup, up0, w["ffn_conv_w"], wup, tl)
    g_fcw = jnp.sum(g_fcw_tiles, 0)
    g_wup = _mm(n2, dup0, TN, BF16, D_MODEL, FF_TILE, tl, "ffn_up_dw", by_col_tile=True)
    g_wdown = _mm(act, dh2, TN, BF16, D_FF // 2, D_MODEL, tl, "ffn_down_dw").reshape(N_CHIPS, -1, D_MODEL)
    ffn = [g_wup, g_wdown] if reduce_first else []
    (dh1, dh1m, do, delta, du1, g_gffn, g_ga, g_og, g_lg, g_lb), theirs = _bwd_mix(
        dh2, dn2, h1, w["ffn_norm_g"], wout, o, ga, u1, w["conv_ln_g"], w["conv_ln_b"], w["conv_out_g"], ffn)
    g_wout = _mm(mix, dh1m, TN, BF16, D_MIX, D_MODEL, tl, "out_dw")
    g_wout = jnp.concatenate([g_wout[:D_CONV], g_wout[D_CONV:].reshape(N_HEADS, HB, D_MODEL)[:, HB - V_HEAD:]
                              .reshape(N_HEADS * V_HEAD, D_MODEL)], 0).reshape(N_CHIPS, -1, D_MODEL)
    pair = ()
    if reduce_first:
        theirs += _sibling_exchange([g_wout], "grad_sibling_exchange_out")
        pair = _add_pairs(ffn + [g_wout], theirs, "first")
    dq, dkv, dkr, got = _attn_bwd(q, kv, kr, do, lse, delta, pair)
    halves = _add_all_chips(pair, got, "first") if reduce_first else ()
    (dzag, g_cw, g_cb), both = _bwd_conv(du1, u0, w["conv_w"], zag, halves)
    if reduce_first:
        g_wup, g_wdown, g_wout = _totals(both)
    dz, dqr, gx, dfirst, g_gq, g_gkv, g_gmix = _bwd_in(dzag, dq, dkv, dkr, cq, ckv, w["q_norm_g"], w["kv_norm_g"],
                                                      wuq, wukv, win, rc, rs1, rs2, h0, w["mix_norm_g"], dh1)
    g_win = _mm(n, dz, TN, BF16, D_MODEL, D_ZP, tl, "in_dw")
    g_wuq = _mm(qn, dqr, TN, BF16, Q_LORA, D_HEADS, tl, "uq_dw")
    g_wukv = _col_shards(_mm(kvn, dkv, TN, BF16, KV_LORA, D_HEADS, tl, "ukv_dw"))

    grads = {
        "w_in": _col_shards(jnp.concatenate([g_win[:, :kr0], g_win[:, kr0 + QK_NOPE:kr0 + d_qk]], 1)),
        "w_uq": _col_shards(g_wuq.reshape(Q_LORA, N_HEADS, HB)[:, :, :d_qk].reshape(Q_LORA, N_HEADS * d_qk)),
        "w_ukv": g_wukv,
        "w_out": g_wout,
        "w_ffn_up": g_wup, "w_ffn_down": g_wdown, "conv_w": g_cw, "ffn_conv_w": g_fcw,
        "meta_tokens": dfirst[DEAD:], "mix_norm_g": g_gmix, "q_norm_g": g_gq, "kv_norm_g": g_gkv, "conv_b": g_cb,
        "conv_ln_g": g_lg, "conv_ln_b": g_lb, "conv_out_g": g_og, "attn_out_g": _unpad_heads(g_ga),
        "ffn_norm_g": g_gffn, "ffn_conv_b": g_fcb, "final_norm_g": g_fin,
    }
    return loss, gx, grads


ROW_SHARDED = ("w_out", "w_ffn_down")


TRANSPOSED = ("w_in", "w_uq")
REDUCED_FIRST = ("w_ffn_up", "w_ffn_down", "w_out")
LATE = ("w_out", "w_ffn_up", "w_ffn_down")


def _full_weight(name, by_chip):
    return by_chip.reshape(-1, by_chip.shape[-1]) if name in ROW_SHARDED else _from_col_shards(by_chip)


def kernel(x, meta_tokens, mix_norm_g, w_in, q_norm_g, w_uq, kv_norm_g, w_ukv, conv_w, conv_b, conv_ln_g, conv_ln_b, conv_out_g, attn_out_g, w_out, ffn_norm_g, w_ffn_up, ffn_conv_w, ffn_conv_b, w_ffn_down, final_norm_g, loss_target, m_meta_tokens, m_mix_norm_g, m_w_in, m_q_norm_g, m_w_uq, m_kv_norm_g, m_w_ukv, m_conv_w, m_conv_b, m_conv_ln_g, m_conv_ln_b, m_conv_out_g, m_attn_out_g, m_w_out, m_ffn_norm_g, m_w_ffn_up, m_ffn_conv_w, m_ffn_conv_b, m_w_ffn_down, m_final_norm_g, v_meta_tokens, v_mix_norm_g, v_w_in, v_q_norm_g, v_w_uq, v_kv_norm_g, v_w_ukv, v_conv_w, v_conv_b, v_conv_ln_g, v_conv_ln_b, v_conv_out_g, v_attn_out_g, v_w_out, v_ffn_norm_g, v_w_ffn_up, v_ffn_conv_w, v_ffn_conv_b, v_w_ffn_down, v_final_norm_g):
    local = dict(meta_tokens=meta_tokens, mix_norm_g=mix_norm_g, w_in=w_in[0], q_norm_g=q_norm_g, w_uq=w_uq[0],
                 kv_norm_g=kv_norm_g, w_ukv=w_ukv[0], conv_w=conv_w[0], conv_b=conv_b, conv_ln_g=conv_ln_g,
                 conv_ln_b=conv_ln_b, conv_out_g=conv_out_g, attn_out_g=attn_out_g, w_out=w_out[0],
                 ffn_norm_g=ffn_norm_g, w_ffn_up=w_ffn_up[0], ffn_conv_w=ffn_conv_w[0], ffn_conv_b=ffn_conv_b,
                 w_ffn_down=w_ffn_down[0], final_norm_g=final_norm_g.reshape(1, D_MODEL))
    ms = dict(zip(WEIGHTS, (m_meta_tokens, m_mix_norm_g, m_w_in, m_q_norm_g, m_w_uq, m_kv_norm_g, m_w_ukv, m_conv_w,
                            m_conv_b, m_conv_ln_g, m_conv_ln_b, m_conv_out_g, m_attn_out_g, m_w_out, m_ffn_norm_g,
                            m_w_ffn_up, m_ffn_conv_w, m_ffn_conv_b, m_w_ffn_down, m_final_norm_g)))
    vs = dict(zip(WEIGHTS, (v_meta_tokens, v_mix_norm_g, v_w_in, v_q_norm_g, v_w_uq, v_kv_norm_g, v_w_ukv, v_conv_w,
                            v_conv_b, v_conv_ln_g, v_conv_ln_b, v_conv_out_g, v_attn_out_g, v_w_out, v_ffn_norm_g,
                            v_w_ffn_up, v_ffn_conv_w, v_ffn_conv_b, v_w_ffn_down, v_final_norm_g)))

    small_flat = jnp.concatenate([local[k].reshape(-1) for k in SMALL_SHARDED]).reshape(-1, LANES)
    early = [k for k in BIG if k not in LATE]
    gathered = _all_gather([local[k].astype(BF16) for k in early] + [small_flat])
    full = {k: v for k, v in local.items() if k not in LATE}
    for name, g in zip(early, gathered[:len(early)]):
        full[name] = _full_weight(name, g)
    small = gathered[-1].reshape(N_CHIPS, -1)
    at = 0
    for name in SMALL_SHARDED:
        r, c = local[name].shape
        full[name] = _from_col_shards(small[:, at:at + r * c].reshape(N_CHIPS, r, c))
        at += r * c

    loss_row, grad_x, grads = _local_step(x[0], loss_target[0], full, {k: local[k].astype(BF16) for k in LATE},
                                          reduce_first=True)

    rest_big = [k for k in BIG if k not in REDUCED_FIRST]
    rep = jnp.concatenate([grads[k].reshape(-1) for k in REPLICATED] + [loss_row.reshape(-1)]).reshape(1, -1, LANES)
    small_pieces = [_lane_rows(_col_shards(grads[k])) for k in SMALL_SHARDED]
    small_pieces.append(jnp.broadcast_to(rep, (N_CHIPS,) + rep.shape[1:]))
    small_rows = sum(p.shape[1] for p in small_pieces)
    small_pack = _pad_rows_to(jnp.concatenate(small_pieces, 1), -(-small_rows // 32) * 32)
    pair = _reduce_begin([grads[k] for k in rest_big] + [small_pack], "rest")

    total = {k: grads[k] for k in REDUCED_FIRST}
    delta, new_m, new_v = {}, {}, {}
    shape2 = lambda a, name: a.reshape(local[name].shape)
    turn = lambda a, name: a.T if name in TRANSPOSED else a

    def update(name, parts=()):
        outs, got = _adamw_big(turn(local[name], name), turn(total[name], name), turn(shape2(ms[name], name), name),
                               turn(shape2(vs[name], name), name), "adamw_" + name, parts)
        total[name], delta[name], new_m[name], new_v[name] = (turn(o, name) for o in outs)
        return got

    *rest_tot, small_tot = _reduce_end(pair, update(REDUCED_FIRST[0], pair), "rest")
    total.update(zip(rest_big, rest_tot))
    flat = small_tot.reshape(-1)
    at = 0
    for name in SMALL_SHARDED + REPLICATED:
        shape = local[name].shape
        size = shape[0] * shape[1]
        total[name] = flat[at:at + size].reshape(shape)
        at += -(-size // LANES) * LANES if name in SMALL_SHARDED else size
    loss = flat[at]

    for name in BIG:
        if name != REDUCED_FIRST[0]:
            update(name)
    rest = SMALL_SHARDED + REPLICATED
    ds, nms, nvs = _adamw_small([local[k] for k in rest], [total[k] for k in rest],
                                [shape2(ms[k], k) for k in rest], [shape2(vs[k], k) for k in rest])
    for k, d, nm, nv in zip(rest, ds, nms, nvs):
        delta[k], new_m[k], new_v[k] = d, nm, nv

    out_shape = dict(zip(WEIGHTS, (meta_tokens, mix_norm_g, w_in, q_norm_g, w_uq, kv_norm_g, w_ukv, conv_w, conv_b,
                                   conv_ln_g, conv_ln_b, conv_out_g, attn_out_g, w_out, ffn_norm_g, w_ffn_up,
                                   ffn_conv_w, ffn_conv_b, w_ffn_down, final_norm_g)))
    outs = [loss, grad_x[None]]
    for group in (total, delta, new_m, new_v):
        outs += [group[k].reshape(out_shape[k].shape) for k in WEIGHTS]
    return tuple(outs)
```

```python
import functools

import jax
import jax.numpy as jnp
from jax import lax
from jax.experimental import pallas as pl
from jax.experimental.pallas import tpu as pltpu

F32 = jnp.float32
BF16 = jnp.bfloat16

D_MODEL = 1024
D_CONV = 512
CONV_WIDTH = 31
N_HEADS = 8
QK_NOPE = 64
QK_ROPE = 32
V_HEAD = 64
Q_LORA = 384
KV_LORA = 256
D_FF = 2816
FFN_CONV_WIDTH = 3
CHUNK_SHIFT = 6
N_META = 16
ROPE_THETA = 10000.0
EPS = 1e-6
NEG = -1e30
ADAM_LR = 0.001
ADAM_B1 = 0.9
ADAM_B2 = 0.999
ADAM_EPS = 1e-08
ADAM_WD = 0.01
ADAM_STEP = 10

LANES = 128
SUBLANES = 8
HB = LANES
D_HEADS = N_HEADS * HB
TM = 256
DEAD = TM - N_META
D_AG = 2 * D_CONV
D_ZP = D_AG + Q_LORA + KV_LORA + HB
D_MIX = D_CONV + D_HEADS
LN2 = 0.6931471805599453
Q_SCALE = (QK_NOPE + QK_ROPE) ** -0.5 / LN2
HALO_CONV = 32
HALO_FFN = 16
FF_CHUNK = 256
FF_MXU_CHUNK = 256
FF_TILE = D_FF // 2
VMEM_LIMIT = 56 * 1024 * 1024
ADAMW_MAX_STEPS = 32
N_CHIPS = 4
HEAD_GROUP = 4
N_GROUPS = N_HEADS // HEAD_GROUP
MESH =pl.DeviceIdType.MESH


def _params(n_grid):
    return pltpu.CompilerParams(dimension_semantics=("arbitrary",) * n_grid, vmem_limit_bytes=VMEM_LIMIT)


def _rows(tm, c, off=0):
    return pl.BlockSpec((tm, c), lambda i: (i, off))


def _full(shape):
    return pl.BlockSpec(shape, lambda i: (0,) * len(shape))


def _prev(hb, c, tm, off=0):
    return pl.BlockSpec((hb, c), lambda i: (jnp.maximum(i * (tm // hb) - 1, 0), off))


def _next(hb, c, tm, nblk, off=0):
    return pl.BlockSpec((hb, c), lambda i: (jnp.minimum((i + 1) * (tm // hb), nblk - 1), off))


def _sds(shape, dtype):
    return jax.ShapeDtypeStruct(shape, dtype)


def _rms_r(x, n):
    return lax.rsqrt(jnp.sum(x * x, -1, keepdims=True) * (1.0 / n) + EPS)


def _rms_bwd(dy, x, r, g, n):
    gd = dy * g
    dx = r * gd - x * (r * r * r) * (jnp.sum(x * gd, -1, keepdims=True) * (1.0 / n))
    return dx, jnp.sum(dy * x * r, 0, keepdims=True)


def _dot(a, b, dims):
    return lax.dot_general(a, b, (dims, ((), ())), preferred_element_type=F32)


NN = ((1,), (0,))
NT = ((1,), (1,))
TN = ((0,), (0,))


def _rope(x, c, s1, s2):
    n = x.shape[-1]
    return x * c + pltpu.roll(x, n - QK_ROPE // 2, 1) * s1 + pltpu.roll(x, QK_ROPE // 2, 1) * s2


def _rope_bwd(g, c, s1, s2):
    n = g.shape[-1]
    return g * c + pltpu.roll(g * s1, QK_ROPE // 2, 1) + pltpu.roll(g * s2, n - QK_ROPE // 2, 1)


def _row_ids(tm, cols=1):
    return pl.program_id(0) * tm + lax.broadcasted_iota(jnp.int32, (tm, cols), 0)


def _mm(a, b, dims, out_dtype, tm, tn, tk, name, by_col_tile=False, b_slabs=False):
    if dims == TN:
        (kk, m), (_, n) = a.shape, b.shape
        a_spec = pl.BlockSpec((tk, tm), lambda i, j, k: (k, i))
    else:
        m, kk = a.shape
        a_spec = pl.BlockSpec((tm, tk), lambda i, j, k: (i, k))
    if dims == NT and b_slabs:
        n = b.shape[1]
        assert b.shape[2] == tk and kk == b.shape[0] * tk, (name, b.shape)
        b_spec = pl.BlockSpec((None, tn, tk), lambda i, j, k: (k, j, 0))
    elif dims == NT:
        n = b.shape[0]
        b_spec = pl.BlockSpec((tn, tk), lambda i, j, k: (j, k))
    elif b_slabs:
        n = b.shape[0] * b.shape[2]
        assert b.shape[2] == tn and kk == b.shape[1], (name, b.shape)
        b_spec = pl.BlockSpec((None, tk, tn), lambda i, j, k: (j, k, 0))
    else:
        n = b.shape[1]
        b_spec = pl.BlockSpec((tk, tn), lambda i, j, k: (k, j))
    assert m % tm == 0 and n % tn == 0 and kk % tk == 0, (name, a.shape, b.shape, tm, tn, tk)
    nk = kk // tk

    def body(a_ref, b_ref, o_ref, acc_ref):
        k = pl.program_id(2)

        @pl.when(k == 0)
        def _():
            acc_ref[...] = jnp.zeros_like(acc_ref)

        acc_ref[...] += _dot(a_ref[...].astype(BF16), b_ref[...].astype(BF16), dims)

        @pl.when(k == nk - 1)
        def _():
            o_ref[...] = acc_ref[...].astype(out_dtype)

    if by_col_tile:
        out_spec, out_shape = pl.BlockSpec((None, tm, tn), lambda i, j, k: (j, i, 0)), (n // tn, m, tn)
    else:
        out_spec, out_shape = pl.BlockSpec((tm, tn), lambda i, j, k: (i, j)), (m, n)
    return pl.pallas_call(
        body, name=name, grid=(m // tm, n // tn, nk), in_specs=[a_spec, b_spec], out_specs=out_spec,
        out_shape=_sds(out_shape, out_dtype), scratch_shapes=[pltpu.VMEM((tm, tn), F32)],
        compiler_params=_params(3))(a, b)


def _fwd_in(x, head, gmix, win, gq, wuq, gkv, wukv, rc, rs1, rs2):
    L = TM + x.shape[0]

    def body(x_ref, head_ref, gmix_ref, win_ref, gq_ref, wuq_ref, gkv_ref, wukv_ref, c_ref, s1_ref, s2_ref,
             h0_ref, n_ref, zag_ref, u0_ref, cq_ref, ckv_ref, qn_ref, kvn_ref, q_ref, kv_ref, kr_ref, kvt_ref):
        h = jnp.where(pl.program_id(0) == 0, head_ref[...], x_ref[...])
        h0_ref[...] = h
        n = (h * _rms_r(h, D_MODEL) * gmix_ref[...]).astype(BF16)
        n_ref[...] = n
        z = _dot(n, win_ref[...], NN)
        a, gate = z[:, :D_CONV], z[:, D_CONV:D_AG]
        zag_ref[...] = z[:, :D_AG].astype(BF16)
        u0_ref[...] = a * jax.nn.sigmoid(gate)
        cq = z[:, D_AG:D_AG + Q_LORA]
        ckv = z[:, D_AG + Q_LORA:D_AG + Q_LORA + KV_LORA]
        krp = z[:, D_AG + Q_LORA + KV_LORA:]
        cq_ref[...] = cq
        ckv_ref[...] = ckv
        qn = (cq * _rms_r(cq, Q_LORA) * gq_ref[...]).astype(BF16)
        qn_ref[...] = qn
        kvn = (ckv * _rms_r(ckv, KV_LORA) * gkv_ref[...]).astype(BF16)
        kvn_ref[...] = kvn
        c, s1, s2 = c_ref[...], s1_ref[...], s2_ref[...]
        q = _dot(qn, wuq_ref[...], NN)
        q = _rope(q, jnp.tile(c, (1, N_HEADS)), jnp.tile(s1, (1, N_HEADS)), jnp.tile(s2, (1, N_HEADS)))
        q_ref[...] = (q * Q_SCALE).astype(BF16)
        kv = _dot(kvn, wukv_ref[...], NN)
        kv_ref[...] = kv.astype(BF16)
        kvt_ref[...] = kv.T.astype(BF16)
        kr_ref[...] = _rope(krp, c, s1, s2).astype(BF16)

    outs = [(D_MODEL, F32), (D_MODEL, BF16), (D_AG, BF16), (D_CONV, F32), (Q_LORA, F32), (KV_LORA, F32), (Q_LORA, BF16),
            (KV_LORA, BF16), (D_HEADS, BF16), (D_HEADS, BF16), (HB, BF16)]
    return pl.pallas_call(
        body, name="fwd_in", grid=(L // TM,),
        in_specs=[pl.BlockSpec((TM, D_MODEL), lambda i: (jnp.maximum(i - 1, 0), 0)), _full(head.shape),
                  _full(gmix.shape), _full(win.shape), _full(gq.shape), _full(wuq.shape),
                  _full(gkv.shape), _full(wukv.shape), _rows(TM, HB), _rows(TM, HB), _rows(TM, HB)],
        out_specs=[_rows(TM, c) for c, _ in outs] + [pl.BlockSpec((D_HEADS, TM), lambda i: (0, i))],
        out_shape=[_sds((L, c), d) for c, d in outs] + [_sds((D_HEADS, L), BF16)],
        compiler_params=_params(1))(x, head, gmix, win, gq, wuq, gkv, wukv, rc, rs1, rs2)


def _conv_taps(xx, w_ref, halo, tm, flip):
    kw = w_ref.shape[0]
    acc, rolls = None, {}
    for k in range(kw):
        term = w_ref[k:k + 1, :] * _shifted_rows(xx, kw - 1 - k, halo, tm, flip, rolls)
        acc = term if acc is None else acc + term
    return acc


def _shifted_rows(xx, d, halo, tm, flip, rolls):
    a, b = divmod(d, SUBLANES)
    if b not in rolls:
        rolls[b] = xx if b == 0 else pltpu.roll(xx, (xx.shape[0] - b) if flip else b, 0)
    start = SUBLANES * a if flip else halo - SUBLANES * a
    return rolls[b][start:start + tm]


def _ln_silu(u1, lg, lb):
    mu = jnp.mean(u1, -1, keepdims=True)
    xc = u1 - mu
    rs = lax.rsqrt(jnp.mean(xc * xc, -1, keepdims=True) + EPS)
    xh = xc * rs
    u2 = xh * lg + lb
    sg = jax.nn.sigmoid(u2)
    return rs, xh, u2, sg, u2 * sg


def _fwd_conv(u0, cw, cb, lg, lb, og):
    L = u0.shape[0]

    def body(u0_ref, u0p_ref, cw_ref, cb_ref, lg_ref, lb_ref, og_ref, u1_ref, mixa_ref):
        halo = jnp.where(pl.program_id(0) > 0, u0p_ref[...], 0.0)
        xx = jnp.concatenate([halo, u0_ref[...]], 0)
        u1 = _conv_taps(xx, cw_ref, HALO_CONV, TM, False) + cb_ref[...]
        u1_ref[...] = u1
        u = _ln_silu(u1, lg_ref[...], lb_ref[...])[4]
        mixa_ref[...] = (u * _rms_r(u, D_CONV) * og_ref[...]).astype(BF16)

    return pl.pallas_call(
        body, name="fwd_conv", grid=(L // TM,),
        in_specs=[_rows(TM, D_CONV), _prev(HALO_CONV, D_CONV, TM), _full(cw.shape), _full(cb.shape),
                  _full(lg.shape), _full(lb.shape), _full(og.shape)],
        out_specs=[_rows(TM, D_CONV), _rows(TM, D_CONV)],
        out_shape=[_sds((L, D_CONV), F32), _sds((L, D_CONV), BF16)],
        compiler_params=_params(1))(u0, u0, cw, cb, lg, lb, og)


def _visible_t(i, j, t):
    key = j * t + lax.broadcasted_iota(jnp.int32, (t, t), 0)
    query = i * t + lax.broadcasted_iota(jnp.int32, (t, t), 1)
    return (lax.shift_right_logical(key, CHUNK_SHIFT) <= lax.shift_right_logical(query, CHUNK_SHIFT)) & (key >= DEAD)


def _stat_lane(h):
    return (h // HEAD_GROUP) * LANES + h % HEAD_GROUP


def _scatter_stats(cols, t):
    lane = lax.broadcasted_iota(jnp.int32, (t, N_GROUPS * LANES), 1)
    out = jnp.zeros((t, N_GROUPS * LANES), F32)
    for h, col in enumerate(cols):
        out = jnp.where(lane == _stat_lane(h), col, out)
    return out


def _resident(shape, index_map):
    return pl.BlockSpec(shape, index_map, pipeline_mode=pl.Buffered(1))


def _attn_fwd(q, kv, kvt, kr, shards=()):
    L = q.shape[0]
    t = TM
    nq = L // t
    n = len(shards)
    pass_step = (5 * nq) // 6

    def body(q_ref, kv_ref, kvt_ref, kr_ref, *refs):
        gather_refs = refs[:n] + refs[n + 2:2 * n + 2] + refs[2 * n + 6:]
        o_ref, lse_ref = refs[n:n + 2]
        qt_scr, m_scr, l_scr, acc_scr = refs[2 * n + 2:2 * n + 6]
        i = pl.program_id(0)
        if n:
            @pl.when(i == 0)
            def _():
                for cp in _gather_copies(gather_refs[:n], gather_refs[n:2 * n], *gather_refs[2 * n:])[0]:
                    cp().start()

        lane = lax.broadcasted_iota(jnp.int32, (t, HB), 1)
        heads = range(N_HEADS)
        cols = [slice(h * HB, (h + 1) * HB) for h in heads]
        for h in heads:
            qt_scr[cols[h], :] = q_ref[:, cols[h]].astype(F32).T.astype(BF16)
        m_scr[...] = jnp.full_like(m_scr, NEG)
        l_scr[...] = jnp.zeros_like(l_scr)
        acc_scr[...] = jnp.zeros_like(acc_scr)

        def tile(keys, vis, whole=True):
            krj = kr_ref[keys, :]
            kvj = [kv_ref[keys, cols[h]] for h in heads]
            lane_k = lane[:krj.shape[0]]
            s = [_dot(jnp.where(lane_k < QK_NOPE, kvj[h], krj), qt_scr[cols[h], :], NN) for h in heads]
            p, alpha = [], []
            for h in heads:
                sh = s[h] if vis is None else jnp.where(vis, s[h], NEG)
                m_prev = m_scr[h:h + 1, :]
                m_new = jnp.maximum(m_prev, jnp.max(sh, 0, keepdims=True))
                a = jnp.exp2(m_prev - m_new)
                ph = jnp.exp2(sh - m_new)
                l_scr[h:h + 1, :] = a * l_scr[h:h + 1, :] + jnp.sum(ph, 0, keepdims=True)
                m_scr[h:h + 1, :] = m_new
                p.append(ph.astype(BF16))
                alpha.append(a)
            for h in heads:
                pv = _dot(kvt_ref[cols[h], keys], p[h], NN) if whole else _dot(kvj[h], p[h], TN)
                acc_scr[cols[h], :] = alpha[h] * acc_scr[cols[h], :] + pv

        tile(pl.ds(pl.multiple_of(i * t, t), t), _visible_t(i, i, t))

        @pl.when(i > 0)
        def _():
            tile(pl.ds(DEAD, N_META), None, whole=False)

        def unmasked(j, carry):
            tile(pl.ds(pl.multiple_of(j * t, t), t), None)
            return carry

        lax.fori_loop(1, i, unmasked, 0)
        lse_ref[...] = jnp.zeros_like(lse_ref)
        for h in heads:
            l = l_scr[h:h + 1, :]
            o_ref[:, cols[h]] = jnp.where(lane >= QK_NOPE, (acc_scr[cols[h], :] / l).T, 0.0).astype(BF16)
            lse_ref[h // HEAD_GROUP, h % HEAD_GROUP:h % HEAD_GROUP + 1, :] = m_scr[h:h + 1, :] + jnp.log2(l)
        if n:
            @pl.when(i == pass_step)
            def _():
                _, arrivals, forwards, _ = _gather_copies(gather_refs[:n], gather_refs[n:2 * n], *gather_refs[2 * n:])
                for landed, onward in zip(arrivals, forwards):
                    landed().wait_recv()
                    onward().start()

            @pl.when(i == nq - 1)
            def _():
                sends, _, forwards, finals = _gather_copies(gather_refs[:n], gather_refs[n:2 * n], *gather_refs[2 * n:])
                for cp in finals:
                    cp().wait_recv()
                for cp in sends + forwards:
                    cp().wait_send()

    any_spec = pl.BlockSpec(memory_space=pl.ANY)
    outs = pl.pallas_call(
        body, name="attn_fwd", grid=(nq,),
        in_specs=[_rows(t, D_HEADS), _resident((L, D_HEADS), lambda i: (0, 0)),
                  _resident((D_HEADS, L), lambda i: (0, 0)), _resident((L, HB), lambda i: (0, 0))] + [any_spec] * n,
        out_specs=[_rows(t, D_HEADS), pl.BlockSpec((N_GROUPS, SUBLANES, t), lambda i: (0, 0, i))] + [any_spec] * n,
        out_shape=[_sds((L, D_HEADS), BF16), _sds((N_GROUPS, SUBLANES, L), F32)] + _gather_out_shapes(shards),
        scratch_shapes=[pltpu.VMEM((D_HEADS, t), BF16), pltpu.VMEM((N_HEADS, t), F32), pltpu.VMEM((N_HEADS, t), F32),
                        pltpu.VMEM((D_HEADS, t), F32)] + (_gather_semaphores(n) if n else []),
        compiler_params=_params(1))(q, kv, kvt, kr, *shards)
    return outs[0], outs[1], _gathered(outs[2:], shards)


def _fwd_mix(h0, mixa, o, ga, wout, gffn):
    L = h0.shape[0]

    def body(h0_ref, mixa_ref, o_ref, ga_ref, wout_ref, gffn_ref, mix_ref, h1_ref, n2_ref):
        of = o_ref[...].astype(F32)
        mixb = (of * _rms_r(of, N_HEADS * V_HEAD) * ga_ref[...]).astype(BF16)
        mix = jnp.concatenate([mixa_ref[...], mixb], 1)
        mix_ref[...] = mix
        mo = jnp.where(_row_ids(TM) >= DEAD, _dot(mix, wout_ref[...], NN), 0.0)
        h1 = h0_ref[...] + mo
        h1_ref[...] = h1
        n2_ref[...] = (h1 * _rms_r(h1, D_MODEL) * gffn_ref[...]).astype(BF16)

    return pl.pallas_call(
        body, name="fwd_mix", grid=(L // TM,),
        in_specs=[_rows(TM, D_MODEL), _rows(TM, D_CONV), _rows(TM, D_HEADS), _full(ga.shape), _full(wout.shape),
                  _full(gffn.shape)],
        out_specs=[_rows(TM, D_MIX), _rows(TM, D_MODEL), _rows(TM, D_MODEL)],
        out_shape=[_sds((L, D_MIX), BF16), _sds((L, D_MODEL), F32), _sds((L, D_MODEL), BF16)],
        compiler_params=_params(1))(h0, mixa, o, ga, wout, gffn)


def _ffn_act_chunk(c, upg_ref, upv_ref, hg, hv, fcw_ref, fcb_ref):
    cs = slice(c * FF_CHUNK, (c + 1) * FF_CHUNK)
    out = []
    for part, (up_ref, halo) in enumerate(((upg_ref, hg), (upv_ref, hv))):
        xx = jnp.concatenate([halo[:, cs], up_ref[:, cs].astype(F32)], 0)
        ws = slice(part * D_FF + c * FF_CHUNK, part * D_FF + (c + 1) * FF_CHUNK)
        y = (fcw_ref[0:1, ws] * pltpu.roll(xx, 2, 0)[HALO_FFN:] + fcw_ref[1:2, ws] * pltpu.roll(xx, 1, 0)[HALO_FFN:]
             + fcw_ref[2:3, ws] * xx[HALO_FFN:] + fcb_ref[:, ws])
        out.append(y)
    return out


def _ffn_in_specs(L):
    return [_rows(TM, D_FF, 0), _rows(TM, D_FF, 1), _prev(HALO_FFN, D_FF, TM, 0), _prev(HALO_FFN, D_FF, TM, 1)]


def _ffn_halos(hg_ref, hv_ref):
    first = pl.program_id(0) == 0
    return (jnp.where(first, 0.0, hg_ref[...].astype(F32)), jnp.where(first, 0.0, hv_ref[...].astype(F32)))


def _fwd_ffn_loss(up0, fcw, fcb, wdown, h1, target, gfin):
    L = h1.shape[0]

    def body(upg_ref, upv_ref, hg_ref, hv_ref, fcw_ref, fcb_ref, wd_ref, h1_ref, t_ref, gf_ref,
             dh2_ref, loss_ref, dgf_ref, up_ref, act_ref):
        i = pl.program_id(0)
        hg, hv = _ffn_halos(hg_ref, hv_ref)
        for c in range(D_FF // FF_CHUNK):
            cs = slice(c * FF_CHUNK, (c + 1) * FF_CHUNK)
            g, val = _ffn_act_chunk(c, upg_ref, upv_ref, hg, hv, fcw_ref, fcb_ref)
            up_ref[:, cs] = g.astype(BF16)
            up_ref[:, D_FF + c * FF_CHUNK:D_FF + (c + 1) * FF_CHUNK] = val.astype(BF16)
            act_ref[:, cs] = (g * jax.nn.sigmoid(g) * val).astype(BF16)
        h2 = h1_ref[...] + _dot(act_ref[...], wd_ref[...], NN)
        r = _rms_r(h2, D_MODEL)
        gf = gf_ref[...]
        err = jnp.where(i > 0, h2 * r * gf - t_ref[...], 0.0)
        dy = err * (1.0 / D_MODEL)
        dh2, dgf = _rms_bwd(dy, h2, r, gf, D_MODEL)
        dh2_ref[...] = dh2

        @pl.when(i == 0)
        def _():
            loss_ref[...] = jnp.zeros_like(loss_ref)
            dgf_ref[...] = jnp.zeros_like(dgf_ref)

        loss_ref[...] += jnp.sum(err * err) * (0.5 / D_MODEL)
        dgf_ref[...] += dgf

    return pl.pallas_call(
        body, name="fwd_ffn_loss", grid=(L // TM,),
        in_specs=_ffn_in_specs(L) + [_full(fcw.shape), _full(fcb.shape), _full(wdown.shape), _rows(TM, D_MODEL),
                                     pl.BlockSpec((TM, D_MODEL), lambda i: (jnp.maximum(i - 1, 0), 0)),
                                     _full(gfin.shape)],
        out_specs=[_rows(TM, D_MODEL), _full((1, LANES)), _full((1, D_MODEL)), _rows(TM, 2 * D_FF), _rows(TM, D_FF)],
        out_shape=[_sds((L, D_MODEL), F32), _sds((1, LANES), F32), _sds((1, D_MODEL), F32),
                   _sds((L, 2 * D_FF), BF16), _sds((L, D_FF), BF16)],
        compiler_params=_params(1))(up0, up0, up0, up0, fcw, fcb, wdown, h1, target, gfin)


def _bwd_ffn_act(dh2, up, wdown):
    L = dh2.shape[0]

    def body(dh2_ref, upg_ref, upv_ref, wd_ref, dup_ref, dfcb_ref):
        da = _dot(dh2_ref[...].astype(BF16), wd_ref[...], NT)

        @pl.when(pl.program_id(0) == 0)
        def _():
            dfcb_ref[...] = jnp.zeros_like(dfcb_ref)

        for c in range(D_FF // FF_CHUNK):
            cs = slice(c * FF_CHUNK, (c + 1) * FF_CHUNK)
            vs = slice(D_FF + c * FF_CHUNK, D_FF + (c + 1) * FF_CHUNK)
            g, val = upg_ref[:, cs].astype(F32), upv_ref[:, cs].astype(F32)
            sg = jax.nn.sigmoid(g)
            si = g * sg
            dac = da[:, cs]
            dg = dac * val * (sg * (1.0 + g * (1.0 - sg)))
            dv = dac * si
            dup_ref[:, cs] = dg.astype(BF16)
            dup_ref[:, vs] = dv.astype(BF16)
            dfcb_ref[:, cs] += jnp.sum(dg, 0, keepdims=True)
            dfcb_ref[:, vs] += jnp.sum(dv, 0, keepdims=True)

    return pl.pallas_call(
        body, name="bwd_ffn_act", grid=(L // TM,),
        in_specs=[_rows(TM, D_MODEL), _rows(TM, D_FF, 0), _rows(TM, D_FF, 1), _full(wdown.shape)],
        out_specs=[_rows(TM, 2 * D_FF), _full((1, 2 * D_FF))],
        out_shape=[_sds((L, 2 * D_FF), BF16), _sds((1, 2 * D_FF), F32)],
        compiler_params=_params(1))(dh2, up, up, wdown)


def _bwd_ffn_conv_up(dup, up0, fcw, wup, tl):
    L, C = dup.shape
    tc = FF_TILE
    nt = L // tl
    nhb = L // HALO_FFN
    chunks = [(c0, min(FF_MXU_CHUNK, tc - c0)) for c0 in range(0, tc, FF_MXU_CHUNK)]

    def body(dy_ref, dyn_ref, x_ref, xp_ref, w_ref, wup_ref, dn2_ref, dx_ref, dw_ref, acc_ref):
        i, k = pl.program_id(0), pl.program_id(1)

        @pl.when(k == 0)
        def _():
            acc_ref[...] = jnp.zeros_like(acc_ref)

        last, first = i == nt - 1, i == 0
        for c0, cw in chunks:
            cs = slice(c0, c0 + cw)
            yy = jnp.concatenate([dy_ref[:, cs].astype(F32), jnp.where(last, 0.0, dyn_ref[:, cs].astype(F32))], 0)
            w = w_ref[:, cs]
            dx = (w[0:1] * pltpu.roll(yy, tl + HALO_FFN - 2, 0)[:tl] + w[1:2] * pltpu.roll(yy, tl + HALO_FFN - 1, 0)[:tl]
                  + w[2:3] * yy[:tl]).astype(BF16)
            dx_ref[:, cs] = dx
            acc_ref[...] += _dot(dx, wup_ref[:, cs], NT)
            xx = jnp.concatenate([jnp.where(first, 0.0, xp_ref[:, cs].astype(F32)), x_ref[:, cs].astype(F32)], 0)
            dy = yy[:tl]
            dw_ref[0:1, cs] = jnp.sum(dy * pltpu.roll(xx, 2, 0)[HALO_FFN:], 0, keepdims=True)
            dw_ref[1:2, cs] = jnp.sum(dy * pltpu.roll(xx, 1, 0)[HALO_FFN:], 0, keepdims=True)
            dw_ref[2:3, cs] = jnp.sum(dy * xx[HALO_FFN:], 0, keepdims=True)

        @pl.when(k == C // tc - 1)
        def _():
            dn2_ref[...] = acc_ref[...]

    tile = pl.BlockSpec((tl, tc), lambda i, k: (i, k))
    per = tl // HALO_FFN
    return pl.pallas_call(
        body, name="bwd_ffn_conv_up", grid=(nt, C // tc),
        in_specs=[tile, pl.BlockSpec((HALO_FFN, tc), lambda i, k: (jnp.minimum((i + 1) * per, nhb - 1), k)),
                  tile, pl.BlockSpec((HALO_FFN, tc), lambda i, k: (jnp.maximum(i * per - 1, 0), k)),
                  pl.BlockSpec((FFN_CONV_WIDTH, tc), lambda i, k: (0, k)),
                  pl.BlockSpec((None, D_MODEL, tc), lambda i, k: (k, 0, 0))],
        out_specs=[pl.BlockSpec((tl, D_MODEL), lambda i, k: (i, 0)), tile,
                   pl.BlockSpec((None, FFN_CONV_WIDTH, tc), lambda i, k: (i, 0, k))],
        out_shape=[_sds((L, D_MODEL), F32), _sds((L, C), BF16), _sds((nt, FFN_CONV_WIDTH, C), F32)],
        scratch_shapes=[pltpu.VMEM((tl, D_MODEL), F32)],
        compiler_params=_params(2))(dup, dup, up0, up0, fcw, wup)


def _carrying(core, n_in, n_out, n, copies_fn, steps):
    def body(*refs):
        exchange = (refs[n_in:n_in + n], refs[n_in + n + n_out:n_in + 2 * n + n_out]) + refs[n_in + 2 * n + n_out:]
        if n:
            @pl.when(pl.program_id(0) == 0)
            def _():
                for cp in copies_fn(*exchange):
                    cp().start()

        core(*refs[:n_in], *refs[n_in + n:n_in + n + n_out])
        if n:
            @pl.when(pl.program_id(0) == steps - 1)
            def _():
                for cp in copies_fn(*exchange):
                    cp().wait()

    return body


def _bwd_mix(dh2, dn2, h1, gffn, wout, o, ga, u1, lg, lb, og, parts=()):
    L = h1.shape[0]

    def body(dh2_ref, dn2_ref, h1_ref, gffn_ref, wout_ref, o_ref, ga_ref, u1_ref, lg_ref, lb_ref, og_ref,
             dh1_ref, dh1m_ref, do_ref, delta_ref, du1_ref, dgffn_ref, dga_ref, dog_ref, dlg_ref, dlb_ref):
        h1 = h1_ref[...]
        dn2x, dgffn = _rms_bwd(dn2_ref[...], h1, _rms_r(h1, D_MODEL), gffn_ref[...], D_MODEL)
        dh1 = dh2_ref[...] + dn2x
        dh1_ref[...] = dh1
        dh1m = jnp.where(_row_ids(TM) >= DEAD, dh1, 0.0).astype(BF16)
        dh1m_ref[...] = dh1m
        dmix = _dot(dh1m, wout_ref[...], NT)
        dma, dmb = dmix[:, :D_CONV], dmix[:, D_CONV:]
        of = o_ref[...].astype(F32)
        do, dga = _rms_bwd(dmb, of, _rms_r(of, N_HEADS * V_HEAD), ga_ref[...], N_HEADS * V_HEAD)
        do_ref[...] = do.astype(BF16)
        prod = do * of
        by_lane = _scatter_stats([jnp.sum(prod[:, h * HB:(h + 1) * HB], -1, keepdims=True) for h in range(N_HEADS)], TM)
        by_row = by_lane.T
        for grp in range(N_GROUPS):
            delta_ref[grp] = by_row[grp * LANES:grp * LANES + SUBLANES, :]
        lg = lg_ref[...]
        rs, xh, u2, sg, u = _ln_silu(u1_ref[...], lg, lb_ref[...])
        du, dog = _rms_bwd(dma, u, _rms_r(u, D_CONV), og_ref[...], D_CONV)
        du2 = du * (sg * (1.0 + u2 * (1.0 - sg)))
        dxh = du2 * lg
        du1_ref[...] = rs * (dxh - jnp.mean(dxh, -1, keepdims=True) - xh * jnp.mean(dxh * xh, -1, keepdims=True))

        @pl.when(pl.program_id(0) == 0)
        def _():
            for ref in (dgffn_ref, dga_ref, dog_ref, dlg_ref, dlb_ref):
                ref[...] = jnp.zeros_like(ref)

        dgffn_ref[...] += dgffn
        dga_ref[...] += dga
        dog_ref[...] += dog
        dlg_ref[...] += jnp.sum(du2 * xh, 0, keepdims=True)
        dlb_ref[...] += jnp.sum(du2, 0, keepdims=True)

    n = len(parts)
    any_spec = pl.BlockSpec(memory_space=pl.ANY)
    outs = pl.pallas_call(
        _carrying(body, 11, 10, n, _sibling_exchange_copies, L // TM), name="bwd_mix", grid=(L // TM,),
        in_specs=[_rows(TM, D_MODEL), _rows(TM, D_MODEL), _rows(TM, D_MODEL), _full(gffn.shape), _full(wout.shape),
                  _rows(TM, D_HEADS), _full(ga.shape), _rows(TM, D_CONV), _full(lg.shape), _full(lb.shape),
                  _full(og.shape)] + [any_spec] * n,
        out_specs=[_rows(TM, D_MODEL), _rows(TM, D_MODEL), _rows(TM, D_HEADS),
                   pl.BlockSpec((N_GROUPS, SUBLANES, TM), lambda i: (0, 0, i)),
                   _rows(TM, D_CONV), _full((1, D_MODEL)), _full((1, D_HEADS)), _full((1, D_CONV)),
                   _full((1, D_CONV)), _full((1, D_CONV))] + [any_spec] * n,
        out_shape=[_sds((L, D_MODEL), F32), _sds((L, D_MODEL), BF16), _sds((L, D_HEADS), BF16),
                   _sds((N_GROUPS, SUBLANES, L), F32),
                   _sds((L, D_CONV), F32), _sds((1, D_MODEL), F32), _sds((1, D_HEADS), F32), _sds((1, D_CONV), F32),
                   _sds((1, D_CONV), F32), _sds((1, D_CONV), F32)] + _sibling_exchange_shapes(parts),
        scratch_shapes=_sibling_exchange_semaphores(n) if n else [],
        compiler_params=_params(1))(dh2, dn2, h1, gffn, wout, o, ga, u1, lg, lb, og, *parts)
    return outs[:10], list(outs[10:])


def _attn_bwd(q, kv, kr, do, lse, delta, parts=()):
    L = q.shape[0]
    t = TM
    nt = L // t
    gw = HEAD_GROUP * HB
    n = len(parts)

    def body(q_ref, kv_ref, kr_ref, do_ref, lse_ref, delta_ref, *refs):
        dq_ref, dkv_ref, dkr_ref = refs[n:n + 3]
        dqt_acc, dk_acc, dv_acc, kkt_scr = refs[2 * n + 3:2 * n + 7]
        exchange_refs = (refs[:n], refs[n + 3:2 * n + 3]) + refs[2 * n + 7:]
        g, j = pl.program_id(0), pl.program_id(1)
        if n:
            @pl.when((g == 0) & (j == 0))
            def _():
                for cp in _chip_copies(*exchange_refs)[0]:
                    cp().start()

        lane = lax.broadcasted_iota(jnp.int32, (t, HB), 1)

        @pl.when(j == 0)
        def _():
            dqt_acc[...] = jnp.zeros_like(dqt_acc)

        @pl.when((j == 0) & (g == 0))
        def _():
            dkr_ref[...] = jnp.zeros_like(dkr_ref)

        dk_acc[...] = jnp.zeros_like(dk_acc)
        dv_acc[...] = jnp.zeros_like(dv_acc)
        krj = kr_ref[...]
        heads = range(HEAD_GROUP)
        cols = [slice(h * HB, (h + 1) * HB) for h in heads]
        for hc in cols:
            kkt_scr[hc, :] = jnp.where(lane < QK_NOPE, kv_ref[:, hc], krj).astype(F32).T.astype(BF16)

        def tile(i, vis, whole=True):
            qs = pl.ds(pl.multiple_of(i * t, t), t)
            keys = slice(None) if whole else slice(DEAD, t)
            kvj = [kv_ref[keys, hc] for hc in cols]
            lane_k = lane[:kvj[0].shape[0]]
            kk = [jnp.where(lane_k < QK_NOPE, kvj[h], krj[keys]) for h in heads]
            qi = [q_ref[qs, hc] for hc in cols]
            doi = [do_ref[qs, hc] for hc in cols]
            s = [_dot(kk[h], qi[h].astype(F32).T.astype(BF16), NN) for h in heads]
            dp = [_dot(kvj[h], doi[h].astype(F32).T.astype(BF16), NN) for h in heads]
            p = []
            for h in heads:
                sh = s[h] if vis is None else jnp.where(vis, s[h], NEG)
                p.append(jnp.exp2(sh - lse_ref[h:h + 1, qs]))
            for h in heads:
                dv_acc[keys, cols[h]] += _dot(p[h].astype(BF16), doi[h], NN)
            ds = [(p[h] * (dp[h] - delta_ref[h:h + 1, qs]) * LN2).astype(BF16) for h in heads]
            for h in heads:
                dk_acc[keys, cols[h]] += _dot(ds[h], qi[h], NN)
            for h in heads:
                dqt = _dot(kkt_scr[cols[h], :], ds[h], NN) if whole else _dot(kk[h], ds[h], TN)
                dqt_acc[cols[h], qs] += dqt

        @pl.when(j == 0)
        def _():
            tile(0, _visible_t(0, 0, t)[DEAD:], whole=False)

            def meta_keys(i, carry):
                tile(i, None, whole=False)
                return carry

            lax.fori_loop(1, nt, meta_keys, 0)

        @pl.when(j > 0)
        def _():
            tile(j, _visible_t(j, j, t))

            def unmasked(i, carry):
                tile(i, None)
                return carry

            lax.fori_loop(j + 1, nt, unmasked, 0)

        dkr = jnp.zeros((t, HB), F32)
        for h in range(HEAD_GROUP):
            hc = slice(h * HB, (h + 1) * HB)
            dk = dk_acc[:, hc]
            dkv_ref[:, hc] = jnp.where(lane < QK_NOPE, dk, dv_acc[:, hc]).astype(BF16)
            dkr = dkr + jnp.where(lane >= QK_NOPE, dk, 0.0)
        dkr_ref[pl.ds(pl.multiple_of(j * t, t), t), :] += dkr

        @pl.when(j == nt - 1)
        def _():
            def untranspose(i, carry):
                qs = pl.ds(pl.multiple_of(i * t, t), t)
                dq_ref[qs, :] = (dqt_acc[:, qs].T * Q_SCALE).astype(BF16)
                return carry

            lax.fori_loop(0, nt, untranspose, 0)

        if n:
            @pl.when((g == N_GROUPS - 1) & (j == nt - 1))
            def _():
                sends, arrivals = _chip_copies(*exchange_refs)
                for cp in arrivals:
                    cp().wait_recv()
                for cp in sends:
                    cp().wait_send()

    group = lambda g, j: (0, g)
    stats = _resident((None, SUBLANES, L), lambda g, j: (g, 0, 0))
    any_spec = pl.BlockSpec(memory_space=pl.ANY)
    outs = pl.pallas_call(
        body, name="attn_bwd", grid=(N_GROUPS, nt),
        in_specs=[_resident((L, gw), group), pl.BlockSpec((t, gw), lambda g, j: (j, g)),
                  pl.BlockSpec((t, HB), lambda g, j: (j, 0)), _resident((L, gw), group), stats, stats]
        + [any_spec] * n,
        out_specs=[pl.BlockSpec((L, gw), group), pl.BlockSpec((t, gw), lambda g, j: (j, g)),
                   pl.BlockSpec((L, HB), lambda g, j: (0, 0))] + [any_spec] * n,
        out_shape=[_sds((L, D_HEADS), BF16), _sds((L, D_HEADS), BF16), _sds((L, HB), F32)]
        + [_sds(p.shape, p.dtype) for p in parts],
        scratch_shapes=[pltpu.VMEM((gw, L), F32), pltpu.VMEM((t, gw), F32), pltpu.VMEM((t, gw), F32),
                        pltpu.VMEM((gw, t), BF16)] + (_chip_semaphores(n) if n else []),
        compiler_params=_params(2))(q, kv, kr, do, lse, delta, *parts)
    return outs[0], outs[1], outs[2], list(outs[3:])


def _bwd_conv(du1, u0, cw, zag, halves=()):
    L = du1.shape[0]
    nt = L // TM

    def body(dy_ref, dyn_ref, x_ref, cw_ref, zag_ref, dzag_ref, dcw_ref, dcb_ref):
        i = pl.program_id(0)
        dy = dy_ref[...]
        yy = jnp.concatenate([dy, jnp.where(i < nt - 1, dyn_ref[...], 0.0)], 0)
        x = x_ref[...]

        @pl.when(i == 0)
        def _():
            dcw_ref[...] = jnp.zeros_like(dcw_ref)
            dcb_ref[...] = jnp.zeros_like(dcb_ref)

        du0, rolls = None, {}
        for k in range(CONV_WIDTH):
            ahead = _shifted_rows(yy, CONV_WIDTH - 1 - k, HALO_CONV, TM, True, rolls)
            term = cw_ref[k:k + 1, :] * ahead
            du0 = term if du0 is None else du0 + term
            dcw_ref[k:k + 1, :] += jnp.sum(ahead * x, 0, keepdims=True)
        dcb_ref[...] += jnp.sum(dy, 0, keepdims=True)
        zag = zag_ref[...].astype(F32)
        a, sg = zag[:, :D_CONV], jax.nn.sigmoid(zag[:, D_CONV:])
        dzag_ref[...] = jnp.concatenate([du0 * sg, du0 * a * sg * (1.0 - sg)], 1).astype(BF16)

    n = len(halves)
    any_spec = pl.BlockSpec(memory_space=pl.ANY)
    outs = pl.pallas_call(
        _carrying(body, 5, 3, n, _sibling_gather_copies, nt), name="bwd_conv", grid=(nt,),
        in_specs=[_rows(TM, D_CONV), _next(HALO_CONV, D_CONV, TM, L // HALO_CONV), _rows(TM, D_CONV),
                  _full(cw.shape), _rows(TM, D_AG)] + [any_spec] * n,
        out_specs=[_rows(TM, D_AG), _full(cw.shape), _full((1, D_CONV))] + [any_spec] * n,
        out_shape=[_sds((L, D_AG), BF16), _sds(cw.shape, F32), _sds((1, D_CONV), F32)]
        + [_sds(p.shape, p.dtype) for p in halves],
        input_output_aliases={5 + a: 3 + a for a in range(n)},
        scratch_shapes=_sibling_gather_semaphores(n) if n else [],
        compiler_params=_params(1))(du1, du1, u0, cw, zag, *halves)
    return outs[:3], list(outs[3:])


def _bwd_in(dzag, dq, dkv, dkr, cq, ckv, gq, gkv, wuq, wukv, win, rc, rs1, rs2, h0, gmix, dh1):
    L = h0.shape[0]

    def body(dzag_ref, dq_ref, dkv_ref, dkr_ref, cq_ref, ckv_ref, gq_ref, gkv_ref, wuq_ref, wukv_ref, win_ref,
             c_ref, s1_ref, s2_ref, h0_ref, gmix_ref, dh1_ref,
             dz_ref, dqr_ref, gx_ref, dfirst_ref, dgq_ref, dgkv_ref, dgmix_ref):
        i = pl.program_id(0)
        c, s1, s2 = c_ref[...], s1_ref[...], s2_ref[...]
        dqr = _rope_bwd(dq_ref[...].astype(F32), jnp.tile(c, (1, N_HEADS)), jnp.tile(s1, (1, N_HEADS)),
                        jnp.tile(s2, (1, N_HEADS))).astype(BF16)
        dqr_ref[...] = dqr
        cq, ckv = cq_ref[...], ckv_ref[...]
        dcq, dgq = _rms_bwd(_dot(dqr, wuq_ref[...], NT), cq, _rms_r(cq, Q_LORA), gq_ref[...], Q_LORA)
        dckv, dgkv = _rms_bwd(_dot(dkv_ref[...], wukv_ref[...], NT), ckv, _rms_r(ckv, KV_LORA), gkv_ref[...], KV_LORA)
        dkrp = _rope_bwd(dkr_ref[...], c, s1, s2)
        dz = jnp.concatenate([dzag_ref[...], dcq.astype(BF16), dckv.astype(BF16), dkrp.astype(BF16)], 1)
        dz_ref[...] = dz
        h0 = h0_ref[...]
        dnx, dgmix = _rms_bwd(_dot(dz, win_ref[...], NT), h0, _rms_r(h0, D_MODEL), gmix_ref[...], D_MODEL)
        dh0 = dh1_ref[...] + dnx

        @pl.when(i == 0)
        def _():
            dfirst_ref[...] = dh0
            for ref in (dgq_ref, dgkv_ref, dgmix_ref):
                ref[...] = jnp.zeros_like(ref)

        @pl.when(i > 0)
        def _():
            gx_ref[...] = dh0

        dgq_ref[...] += dgq
        dgkv_ref[...] += dgkv
        dgmix_ref[...] += dgmix

    return pl.pallas_call(
        body, name="bwd_in", grid=(L // TM,),
        in_specs=[_rows(TM, D_AG), _rows(TM, D_HEADS), _rows(TM, D_HEADS), _rows(TM, HB), _rows(TM, Q_LORA),
                  _rows(TM, KV_LORA), _full(gq.shape), _full(gkv.shape), _full(wuq.shape), _full(wukv.shape),
                  _full(win.shape), _rows(TM, HB), _rows(TM, HB), _rows(TM, HB), _rows(TM, D_MODEL),
                  _full(gmix.shape), _rows(TM, D_MODEL)],
        out_specs=[_rows(TM, D_ZP), _rows(TM, D_HEADS),
                   pl.BlockSpec((TM, D_MODEL), lambda i: (jnp.maximum(i - 1, 0), 0)), _full((TM, D_MODEL)),
                   _full((1, Q_LORA)), _full((1, KV_LORA)), _full((1, D_MODEL))],
        out_shape=[_sds((L, D_ZP), BF16), _sds((L, D_HEADS), BF16), _sds((L - TM, D_MODEL), F32),
                   _sds((TM, D_MODEL), F32), _sds((1, Q_LORA), F32), _sds((1, KV_LORA), F32), _sds((1, D_MODEL), F32)],
        compiler_params=_params(1))(dzag, dq, dkv, dkr, cq, ckv, gq, gkv, wuq, wukv, win, rc, rs1, rs2, h0, gmix, dh1)


def _mesh_pos():
    return lax.axis_index("x"), lax.axis_index("y"), lax.axis_index("c")


def _remote_copy(src, dst, send_sem, recv_sem, to):
    return functools.partial(pltpu.make_async_remote_copy, src, dst, send_sem, recv_sem, device_id=to,
                             device_id_type=MESH)


def _all_gather(shards):
    n = len(shards)

    def body(*refs):
        sends, arrivals, forwards, finals = _gather_copies(refs[:n], refs[n:2 * n], *refs[2 * n:])
        for cp in sends:
            cp().start()
        for landed, onward in zip(arrivals, forwards):
            landed().wait_recv()
            onward().start()
        for cp in finals:
            cp().wait_recv()
        for cp in sends + forwards:
            cp().wait_send()

    any_spec = pl.BlockSpec(memory_space=pl.ANY)
    outs = pl.pallas_call(
        body, name="all_gather_weights", in_specs=[any_spec] * n, out_specs=[any_spec] * n,
        out_shape=_gather_out_shapes(shards), scratch_shapes=_gather_semaphores(n))(*shards)
    return _gathered(outs, shards)


def _gather_out_shapes(shards):
    return [_sds((2 * N_CHIPS, s.shape[0] // 2) + s.shape[1:], s.dtype) for s in shards]


def _gather_semaphores(n):
    return [pltpu.SemaphoreType.DMA((n, 8)), pltpu.SemaphoreType.DMA((n, 8))]


def _gathered(outs, shards):
    return [o.reshape((N_CHIPS, s.shape[0]) + s.shape[1:]) for o, s in zip(outs, shards)]


def _gather_copies(ins, outs, send_sems, recv_sems):
    x, y, c = _mesh_pos()
    chips = [(1 - x, y), (x, 1 - y), (1 - x, 1 - y)]
    sends, arrivals, forwards, finals = [], [], [], []

    def copy(src, dst, a, k, to):
        return _remote_copy(src, dst, send_sems.at[a, k], recv_sems.at[a, k], to)

    for a, (src, out) in enumerate(zip(ins, outs)):
        m = out.shape[1]
        mine = src.at[pl.ds(pl.multiple_of(c * m, 16), m)]
        for hf in range(2):
            own = out.at[4 * x + 2 * y + hf]
            sends.append(copy(src.at[pl.ds(hf * m, m)], own, a, 6 + hf, (x, y, 1 - c)))
            finals.append(copy(own, own, a, 6 + hf, (x, y, 1 - c)))
        for k, chip in enumerate(chips):
            slot = 4 * chip[0] + 2 * chip[1]
            sends.append(copy(mine, out.at[4 * x + 2 * y + c], a, k, (*chip, c)))
            arrivals.append(copy(out.at[slot + c], out.at[slot + c], a, k, (*chip, c)))
            forwards.append(copy(out.at[slot + c], out.at[slot + c], a, 3 + k, (x, y, 1 - c)))
            finals.append(copy(out.at[slot + 1 - c], out.at[slot + 1 - c], a, 3 + k, (x, y, 1 - c)))
    return sends, arrivals, forwards, finals


def _sibling_exchange(parts, name):
    n = len(parts)

    def body(*refs):
        copies = _sibling_exchange_copies(refs[:n], refs[n:2 * n], *refs[2 * n:])
        for cp in copies:
            cp().start()
        for cp in copies:
            cp().wait()

    any_spec = pl.BlockSpec(memory_space=pl.ANY)
    return pl.pallas_call(
        body, name=name, in_specs=[any_spec] * n, out_specs=[any_spec] * n,
        out_shape=_sibling_exchange_shapes(parts), scratch_shapes=_sibling_exchange_semaphores(n))(*parts)


def _sibling_exchange_shapes(parts):
    return [_sds((N_CHIPS, p.shape[1] // 2, p.shape[2]), p.dtype) for p in parts]


def _sibling_exchange_semaphores(n):
    return [pltpu.SemaphoreType.DMA((n, N_CHIPS)), pltpu.SemaphoreType.DMA((n, N_CHIPS))]


def _sibling_exchange_copies(ins, theirs, send_sems, recv_sems):
    x, y, c = _mesh_pos()
    copies = []
    for a, (src, dst) in enumerate(zip(ins, theirs)):
        h = dst.shape[1]
        rows = pl.ds(pl.multiple_of((1 - c) * h, 16), h)
        copies += [_remote_copy(src.at[q, rows], dst.at[q], send_sems.at[a, q], recv_sems.at[a, q], (x, y, 1 - c))
                   for q in range(N_CHIPS)]
    return copies


def _chip_semaphores(n):
    return [pltpu.SemaphoreType.DMA((n, 3)), pltpu.SemaphoreType.DMA((n, 3))]


def _chip_copies(ins, outs, send_sems, recv_sems):
    x, y, c = _mesh_pos()
    me = 2 * x + y
    sends, arrivals = [], []
    for a, (src, out) in enumerate(zip(ins, outs)):
        for k, chip in enumerate([(1 - x, y), (x, 1 - y), (1 - x, 1 - y)]):
            slot = 2 * chip[0] + chip[1]
            sems = (send_sems.at[a, k], recv_sems.at[a, k], (*chip, c))
            sends.append(_remote_copy(src.at[slot], out.at[me], *sems))
            arrivals.append(_remote_copy(out.at[slot], out.at[slot], *sems))
    return sends, arrivals


def _sibling_gather(parts, name):
    n = len(parts)

    def body(*refs):
        copies = _sibling_gather_copies(refs[:n], refs[n:2 * n], *refs[2 * n:])
        for cp in copies:
            cp().start()
        for cp in copies:
            cp().wait()

    any_spec = pl.BlockSpec(memory_space=pl.ANY)
    return pl.pallas_call(
        body, name=name, in_specs=[any_spec] * n, out_specs=[any_spec] * n,
        out_shape=[_sds(p.shape, p.dtype) for p in parts], input_output_aliases={a: a for a in range(n)},
        scratch_shapes=_sibling_gather_semaphores(n))(*parts)


def _sibling_gather_semaphores(n):
    return [pltpu.SemaphoreType.DMA((n,)), pltpu.SemaphoreType.DMA((n,))]


def _sibling_gather_copies(ins, outs, send_sems, recv_sems):
    x, y, c = _mesh_pos()
    return [_remote_copy(src.at[c], dst.at[c], send_sems.at[a], recv_sems.at[a], (x, y, 1 - c))
            for a, (src, dst) in enumerate(zip(ins, outs))]


def _row_tile(rows, row_bytes, align, budget=1 << 20):
    best = None
    for t in range(align, rows + 1, align):
        if rows % t == 0 and t * row_bytes <= budget:
            best = t
    return best or rows


def _scalar(v):
    return jnp.reshape(v, (1,)).astype(jnp.int32)


def _add_pair(part, theirs, c, name):
    _, h, cols = theirs.shape
    tr = _row_tile(h, cols * 4, 16, budget=1 << 21)
    nb = h // tr

    def body(c_ref, a_ref, b_ref, o_ref):
        o_ref[...] = (a_ref[...].astype(F32) + b_ref[...].astype(F32)).astype(o_ref.dtype)

    half = pl.BlockSpec((None, tr, cols), lambda q, i, c_ref: (q, i, 0))
    grid_spec = pltpu.PrefetchScalarGridSpec(
        num_scalar_prefetch=1, grid=(N_CHIPS, nb),
        in_specs=[pl.BlockSpec((None, tr, cols), lambda q, i, c_ref: (q, c_ref[0] * nb + i, 0)), half],
        out_specs=half)
    return pl.pallas_call(body, name=name, grid_spec=grid_spec, out_shape=_sds(theirs.shape, part.dtype),
                          compiler_params=_params(2))(_scalar(c), part, theirs)


def _add_chips(got, own, me, c, name):
    _, h, cols = got.shape
    tr = _row_tile(h, cols * 4 * N_CHIPS, 16, budget=1 << 22)

    def body(pos_ref, got_ref, own_ref, o_ref):
        acc = None
        for q in range(N_CHIPS):
            term = jnp.where(pos_ref[0] == q, own_ref[q], got_ref[q]).astype(F32)
            acc = term if acc is None else acc + term
        o_ref[...] = acc

    by_chip = pl.BlockSpec((N_CHIPS, tr, cols), lambda i, pos_ref: (0, i, 0))
    grid_spec = pltpu.PrefetchScalarGridSpec(
        num_scalar_prefetch=1, grid=(h // tr,), in_specs=[by_chip, by_chip],
        out_specs=pl.BlockSpec((None, tr, cols), lambda i, pos_ref: (pos_ref[1], i, 0)))
    return pl.pallas_call(body, name=name, grid_spec=grid_spec, out_shape=_sds((2, h, cols), F32),
                          compiler_params=_params(1))(jnp.stack([me, c]).astype(jnp.int32), got, own)


def _add_pairs(parts, theirs, tag):
    c = lax.axis_index("c")
    return [_add_pair(p, t, c, f"grad_add_pair_{tag}_{a}") for a, (p, t) in enumerate(zip(parts, theirs))]


def _add_all_chips(pair, got, tag):
    x, y, c = _mesh_pos()
    return [_add_chips(g, p, 2 * x + y, c, f"grad_add_chips_{tag}_{a}") for a, (g, p) in enumerate(zip(got, pair))]


def _totals(both):
    return [b.reshape(-1, b.shape[-1]) for b in both]


def _reduce_begin(parts, tag):
    return _add_pairs(parts, _sibling_exchange(parts, f"grad_sibling_exchange_{tag}"), tag)


def _reduce_end(pair, got, tag):
    return _totals(_sibling_gather(_add_all_chips(pair, got, tag), f"grad_sibling_gather_{tag}"))


def _adamw_math(w, g, m, v):
    m = ADAM_B1 * m + (1.0 - ADAM_B1) * g
    v = ADAM_B2 * v + (1.0 - ADAM_B2) * (g * g)
    m_hat = m / (1.0 - ADAM_B1 ** ADAM_STEP)
    v_hat = v / (1.0 - ADAM_B2 ** ADAM_STEP)
    return -ADAM_LR * (m_hat / (jnp.sqrt(v_hat) + ADAM_EPS) + ADAM_WD * w), m, v


def _adamw_big(w, g, m, v, name, parts=()):
    r, c = w.shape
    tr = _row_tile(r, c * 4, 8, budget=1 << 19)
    if r // tr > ADAMW_MAX_STEPS:
        tr = r
    n = len(parts)
    steps = r // tr

    def body(w_ref, g_ref, m_ref, v_ref, *refs):
        go_ref, d_ref, mo_ref, vo_ref = refs[n:n + 4]
        exchange_refs = (refs[:n], refs[n + 4:2 * n + 4]) + refs[2 * n + 4:]
        if n:
            @pl.when(pl.program_id(0) == 0)
            def _():
                for cp in _chip_copies(*exchange_refs)[0]:
                    cp().start()

        g = g_ref[...]
        go_ref[...] = g
        d_ref[...], mo_ref[...], vo_ref[...] = _adamw_math(w_ref[...], g, m_ref[...], v_ref[...])
        if n:
            @pl.when(pl.program_id(0) == steps - 1)
            def _():
                sends, arrivals = _chip_copies(*exchange_refs)
                for cp in arrivals:
                    cp().wait_recv()
                for cp in sends:
                    cp().wait_send()

    any_spec = pl.BlockSpec(memory_space=pl.ANY)
    outs = pl.pallas_call(
        body, name=name, grid=(steps,), in_specs=[_rows(tr, c)] * 4 + [any_spec] * n,
        out_specs=[_rows(tr, c)] * 4 + [any_spec] * n,
        out_shape=[_sds((r, c), F32)] * 4 + [_sds(p.shape, p.dtype) for p in parts],
        scratch_shapes=_chip_semaphores(n) if n else [], compiler_params=_params(1))(w, g, m, v, *parts)
    return outs[:4], list(outs[4:])


def _adamw_small(ws, gs, ms, vs):
    n = len(ws)

    def body(*refs):
        for a in range(n):
            w_ref, g_ref, m_ref, v_ref = (refs[k * n + a] for k in range(4))
            d, m, v = _adamw_math(w_ref[...], g_ref[...], m_ref[...], v_ref[...])
            refs[4 * n + a][...] = d
            refs[5 * n + a][...] = m
            refs[6 * n + a][...] = v

    vm = pl.BlockSpec(memory_space=pltpu.VMEM)
    outs = pl.pallas_call(
        body, name="adamw_small", in_specs=[vm] * (4 * n), out_specs=[vm] * (3 * n),
        out_shape=[_sds(w.shape, F32) for w in ws] * 3)(*ws, *gs, *ms, *vs)
    return outs[:n], outs[n:2 * n], outs[2 * n:]


BIG = ("w_in", "w_uq", "w_ukv", "w_out", "w_ffn_up", "w_ffn_down")
SMALL_SHARDED = ("conv_w", "ffn_conv_w", "meta_tokens")
REPLICATED = ("mix_norm_g", "q_norm_g", "kv_norm_g", "conv_b", "conv_ln_g", "conv_ln_b", "conv_out_g", "attn_out_g",
              "ffn_norm_g", "ffn_conv_b", "final_norm_g")
WEIGHTS = ("meta_tokens", "mix_norm_g", "w_in", "q_norm_g", "w_uq", "kv_norm_g", "w_ukv", "conv_w", "conv_b",
           "conv_ln_g", "conv_ln_b", "conv_out_g", "attn_out_g", "w_out", "ffn_norm_g", "w_ffn_up", "ffn_conv_w",
           "ffn_conv_b", "w_ffn_down", "final_norm_g")


def _lane_rows(a):
    return a.reshape(N_CHIPS, -1, LANES)


def _col_shards(a):
    k = a.shape[0]
    return a.reshape(k, N_CHIPS, -1).transpose(1, 0, 2)


def _from_col_shards(a):
    return a.transpose(1, 0, 2).reshape(a.shape[1], -1)


def _pad_rows_to(a, rows):
    return jnp.pad(a, ((0, 0), (0, rows - a.shape[1]), (0, 0)))


def _rope_tables(L):
    pos = (jnp.arange(L, dtype=jnp.int32) - DEAD).astype(F32)
    inv_freq = 1.0 / (ROPE_THETA ** (jnp.arange(0, QK_ROPE, 2, dtype=F32) / QK_ROPE))
    ang = pos[:, None] * inv_freq[None, :]
    cos, sin = jnp.cos(ang), jnp.sin(ang)
    half = QK_ROPE // 2
    z = lambda n: jnp.zeros((L, n), F32)
    rc = jnp.concatenate([jnp.ones((L, QK_NOPE), F32), cos, cos, z(HB - QK_NOPE - QK_ROPE)], 1)
    rs1 = jnp.concatenate([z(QK_NOPE), -sin, z(HB - QK_NOPE - half)], 1)
    rs2 = jnp.concatenate([z(QK_NOPE + half), sin, z(HB - QK_NOPE - QK_ROPE)], 1)
    return rc, rs1, rs2


def _pad_heads(g):
    return jnp.pad(g.reshape(N_HEADS, V_HEAD), ((0, 0), (HB - V_HEAD, 0))).reshape(1, D_HEADS)


def _unpad_heads(g):
    return g.reshape(N_HEADS, HB)[:, HB - V_HEAD:].reshape(1, N_HEADS * V_HEAD)


def _local_step(x, target, w, late_shards=None, reduce_first=False):
    S = x.shape[0]
    L = TM + S
    tl = L // 4
    d_qk = QK_NOPE + QK_ROPE
    win_n = w["w_in"]
    kr0 = D_AG + Q_LORA + KV_LORA
    win = jnp.concatenate([win_n[:, :kr0], jnp.zeros((D_MODEL, QK_NOPE), BF16), win_n[:, kr0:],
                           jnp.zeros((D_MODEL, HB - d_qk), BF16)], 1)
    wuq = jnp.pad(w["w_uq"].reshape(Q_LORA, N_HEADS, d_qk), ((0, 0), (0, 0), (0, HB - d_qk))).reshape(Q_LORA, D_HEADS)
    wukv = w["w_ukv"]
    ga = _pad_heads(w["attn_out_g"])
    gfin = w["final_norm_g"].reshape(1, D_MODEL)
    rc, rs1, rs2 = _rope_tables(L)
    head = jnp.concatenate([jnp.zeros((DEAD, D_MODEL), F32), w["meta_tokens"]], 0)

    h0, n, zag, u0, cq, ckv, qn, kvn, q, kv, kr, kvt = _fwd_in(x, head, w["mix_norm_g"], win, w["q_norm_g"], wuq,
                                                                w["kv_norm_g"], wukv, rc, rs1, rs2)
    u1, mixa = _fwd_conv(u0, w["conv_w"], w["conv_b"], w["conv_ln_g"], w["conv_ln_b"], w["conv_out_g"])
    if late_shards is None:
        o, lse, _ = _attn_fwd(q, kv, kvt, kr)
        wout_n, wup, wdown = w["w_out"], _col_shards(w["w_ffn_up"]), w["w_ffn_down"]
    else:
        o, lse, late = _attn_fwd(q, kv, kvt, kr, [late_shards[k] for k in LATE])
        wout_n, wup, wdown = _full_weight("w_out", late[0]), late[1], _full_weight("w_ffn_down", late[2])
    wout = jnp.concatenate([wout_n[:D_CONV], jnp.pad(wout_n[D_CONV:].reshape(N_HEADS, V_HEAD, D_MODEL),
                                                     ((0, 0), (HB - V_HEAD, 0), (0, 0))).reshape(D_HEADS, D_MODEL)], 0)
    mix, h1, n2 = _fwd_mix(h0, mixa, o, ga, wout, w["ffn_norm_g"])
    up0 = _mm(n2, wup, NN, BF16, tl, FF_TILE, D_MODEL, "ffn_up", b_slabs=True)
    dh2, loss, g_fin, up, act = _fwd_ffn_loss(up0, w["ffn_conv_w"], w["ffn_conv_b"], wdown, h1, target, gfin)

    dup, g_fcb = _bwd_ffn_act(dh2, up, wdown)
    dn2, dup0, g_fcw_tiles = _bwd_ffn_conv_up(dup, up0, w["ffn_conv_w"], wup, tl)
    g_fcw = jnp.sum(g_fcw_tiles, 0)
    g_wup = _mm(n2, dup0, TN, BF16, D_MODEL, FF_TILE, tl, "ffn_up_dw", by_col_tile=True)
    g_wdown = _mm(act, dh2, TN, BF16, D_FF // 2, D_MODEL, tl, "ffn_down_dw").reshape(N_CHIPS, -1, D_MODEL)
    ffn = [g_wup, g_wdown] if reduce_first else []
    (dh1, dh1m, do, delta, du1, g_gffn, g_ga, g_og, g_lg, g_lb), theirs = _bwd_mix(
        dh2, dn2, h1, w["ffn_norm_g"], wout, o, ga, u1, w["conv_ln_g"], w["conv_ln_b"], w["conv_out_g"], ffn)
    g_wout = _mm(mix, dh1m, TN, BF16, D_MIX // 2, D_MODEL, tl, "out_dw")
    g_wout = jnp.concatenate([g_wout[:D_CONV], g_wout[D_CONV:].reshape(N_HEADS, HB, D_MODEL)[:, HB - V_HEAD:]
                              .reshape(N_HEADS * V_HEAD, D_MODEL)], 0).reshape(N_CHIPS, -1, D_MODEL)
    pair = ()
    if reduce_first:
        theirs += _sibling_exchange([g_wout], "grad_sibling_exchange_out")
        pair = _add_pairs(ffn + [g_wout], theirs, "first")
    dq, dkv, dkr, got = _attn_bwd(q, kv, kr, do, lse, delta, pair)
    halves = _add_all_chips(pair, got, "first") if reduce_first else ()
    (dzag, g_cw, g_cb), both = _bwd_conv(du1, u0, w["conv_w"], zag, halves)
    if reduce_first:
        g_wup, g_wdown, g_wout = _totals(both)
    dz, dqr, gx, dfirst, g_gq, g_gkv, g_gmix = _bwd_in(dzag, dq, dkv, dkr, cq, ckv, w["q_norm_g"], w["kv_norm_g"],
                                                      wuq, wukv, win, rc, rs1, rs2, h0, w["mix_norm_g"], dh1)
    g_win = _mm(n, dz, TN, BF16, D_MODEL, D_ZP // 2, tl, "in_dw")
    g_wuq = _mm(qn, dqr, TN, BF16, Q_LORA, D_HEADS, tl, "uq_dw")
    g_wukv = _col_shards(_mm(kvn, dkv, TN, BF16, KV_LORA, D_HEADS, tl, "ukv_dw"))

    grads = {
        "w_in": _col_shards(jnp.concatenate([g_win[:, :kr0], g_win[:, kr0 + QK_NOPE:kr0 + d_qk]], 1)),
        "w_uq": _col_shards(g_wuq.reshape(Q_LORA, N_HEADS, HB)[:, :, :d_qk].reshape(Q_LORA, N_HEADS * d_qk)),
        "w_ukv": g_wukv,
        "w_out": g_wout,
        "w_ffn_up": g_wup, "w_ffn_down": g_wdown, "conv_w": g_cw, "ffn_conv_w": g_fcw,
        "meta_tokens": dfirst[DEAD:], "mix_norm_g": g_gmix, "q_norm_g": g_gq, "kv_norm_g": g_gkv, "conv_b": g_cb,
        "conv_ln_g": g_lg, "conv_ln_b": g_lb, "conv_out_g": g_og, "attn_out_g": _unpad_heads(g_ga),
        "ffn_norm_g": g_gffn, "ffn_conv_b": g_fcb, "final_norm_g": g_fin,
    }
    return loss, gx, grads


ROW_SHARDED = ("w_out", "w_ffn_down")


TRANSPOSED = ("w_in", "w_uq")
REDUCED_FIRST = ("w_ffn_up", "w_ffn_down", "w_out")
LATE = ("w_out", "w_ffn_up", "w_ffn_down")


def _full_weight(name, by_chip):
    return by_chip.reshape(-1, by_chip.shape[-1]) if name in ROW_SHARDED else _from_col_shards(by_chip)


def kernel(x, meta_tokens, mix_norm_g, w_in, q_norm_g, w_uq, kv_norm_g, w_ukv, conv_w, conv_b, conv_ln_g, conv_ln_b, conv_out_g, attn_out_g, w_out, ffn_norm_g, w_ffn_up, ffn_conv_w, ffn_conv_b, w_ffn_down, final_norm_g, loss_target, m_meta_tokens, m_mix_norm_g, m_w_in, m_q_norm_g, m_w_uq, m_kv_norm_g, m_w_ukv, m_conv_w, m_conv_b, m_conv_ln_g, m_conv_ln_b, m_conv_out_g, m_attn_out_g, m_w_out, m_ffn_norm_g, m_w_ffn_up, m_ffn_conv_w, m_ffn_conv_b, m_w_ffn_down, m_final_norm_g, v_meta_tokens, v_mix_norm_g, v_w_in, v_q_norm_g, v_w_uq, v_kv_norm_g, v_w_ukv, v_conv_w, v_conv_b, v_conv_ln_g, v_conv_ln_b, v_conv_out_g, v_attn_out_g, v_w_out, v_ffn_norm_g, v_w_ffn_up, v_ffn_conv_w, v_ffn_conv_b, v_w_ffn_down, v_final_norm_g):
    local = dict(meta_tokens=meta_tokens, mix_norm_g=mix_norm_g, w_in=w_in[0], q_norm_g=q_norm_g, w_uq=w_uq[0],
                 kv_norm_g=kv_norm_g, w_ukv=w_ukv[0], conv_w=conv_w[0], conv_b=conv_b, conv_ln_g=conv_ln_g,
                 conv_ln_b=conv_ln_b, conv_out_g=conv_out_g, attn_out_g=attn_out_g, w_out=w_out[0],
                 ffn_norm_g=ffn_norm_g, w_ffn_up=w_ffn_up[0], ffn_conv_w=ffn_conv_w[0], ffn_conv_b=ffn_conv_b,
                 w_ffn_down=w_ffn_down[0], final_norm_g=final_norm_g.reshape(1, D_MODEL))
    ms = dict(zip(WEIGHTS, (m_meta_tokens, m_mix_norm_g, m_w_in, m_q_norm_g, m_w_uq, m_kv_norm_g, m_w_ukv, m_conv_w,
                            m_conv_b, m_conv_ln_g, m_conv_ln_b, m_conv_out_g, m_attn_out_g, m_w_out, m_ffn_norm_g,
                            m_w_ffn_up, m_ffn_conv_w, m_ffn_conv_b, m_w_ffn_down, m_final_norm_g)))
    vs = dict(zip(WEIGHTS, (v_meta_tokens, v_mix_norm_g, v_w_in, v_q_norm_g, v_w_uq, v_kv_norm_g, v_w_ukv, v_conv_w,
                            v_conv_b, v_conv_ln_g, v_conv_ln_b, v_conv_out_g, v_attn_out_g, v_w_out, v_ffn_norm_g,
                            v_w_ffn_up, v_ffn_conv_w, v_ffn_conv_b, v_w_ffn_down, v_final_norm_g)))

    small_flat = jnp.concatenate([local[k].reshape(-1) for k in SMALL_SHARDED]).reshape(-1, LANES)
    early = [k for k in BIG if k not in LATE]
    gathered = _all_gather([local[k].astype(BF16) for k in early] + [small_flat])
    full = {k: v for k, v in local.items() if k not in LATE}
    for name, g in zip(early, gathered[:len(early)]):
        full[name] = _full_weight(name, g)
    small = gathered[-1].reshape(N_CHIPS, -1)
    at = 0
    for name in SMALL_SHARDED:
        r, c = local[name].shape
        full[name] = _from_col_shards(small[:, at:at + r * c].reshape(N_CHIPS, r, c))
        at += r * c

    loss_row, grad_x, grads = _local_step(x[0], loss_target[0], full, {k: local[k].astype(BF16) for k in LATE},
                                          reduce_first=True)

    rest_big = [k for k in BIG if k not in REDUCED_FIRST]
    rep = jnp.concatenate([grads[k].reshape(-1) for k in REPLICATED] + [loss_row.reshape(-1)]).reshape(1, -1, LANES)
    small_pieces = [_lane_rows(_col_shards(grads[k])) for k in SMALL_SHARDED]
    small_pieces.append(jnp.broadcast_to(rep, (N_CHIPS,) + rep.shape[1:]))
    small_rows = sum(p.shape[1] for p in small_pieces)
    small_pack = _pad_rows_to(jnp.concatenate(small_pieces, 1), -(-small_rows // 32) * 32)
    pair = _reduce_begin([grads[k] for k in rest_big] + [small_pack], "rest")

    total = {k: grads[k] for k in REDUCED_FIRST}
    delta, new_m, new_v = {}, {}, {}
    shape2 = lambda a, name: a.reshape(local[name].shape)
    turn = lambda a, name: a.T if name in TRANSPOSED else a

    def update(name, parts=()):
        outs, got = _adamw_big(turn(local[name], name), turn(total[name], name), turn(shape2(ms[name], name), name),
                               turn(shape2(vs[name], name), name), "adamw_" + name, parts)
        total[name], delta[name], new_m[name], new_v[name] = (turn(o, name) for o in outs)
        return got

    *rest_tot, small_tot = _reduce_end(pair, update(REDUCED_FIRST[0], pair), "rest")
    total.update(zip(rest_big, rest_tot))
    flat = small_tot.reshape(-1)
    at = 0
    for name in SMALL_SHARDED + REPLICATED:
        shape = local[name].shape
        size = shape[0] * shape[1]
        total[name] = flat[at:at + size].reshape(shape)
        at += -(-size // LANES) * LANES if name in SMALL_SHARDED else size
    loss = flat[at]

    for name in BIG:
        if name != REDUCED_FIRST[0]:
            update(name)
    rest = SMALL_SHARDED + REPLICATED
    ds, nms, nvs = _adamw_small([local[k] for k in rest], [total[k] for k in rest],
                                [shape2(ms[k], k) for k in rest], [shape2(vs[k], k) for k in rest])
    for k, d, nm, nv in zip(rest, ds, nms, nvs):
        delta[k], new_m[k], new_v[k] = d, nm, nv

    out_shape = dict(zip(WEIGHTS, (meta_tokens, mix_norm_g, w_in, q_norm_g, w_uq, kv_norm_g, w_ukv, conv_w, conv_b,
                                   conv_ln_g, conv_ln_b, conv_out_g, attn_out_g, w_out, ffn_norm_g, w_ffn_up,
                                   ffn_conv_w, ffn_conv_b, w_ffn_down, final_norm_g)))
    outs = [loss, grad_x[None]]
    for group in (total, delta, new_m, new_v):
        outs += [group[k].reshape(out_shape[k].shape) for k in WEIGHTS]
    return tuple(outs)
```

```python
import functools

import jax
import jax.numpy as jnp
from jax import lax
from jax.experimental import pallas as pl
from jax.experimental.pallas import tpu as pltpu

F32 = jnp.float32
BF16 = jnp.bfloat16

D_MODEL = 1024
D_CONV = 512
CONV_WIDTH = 31
N_HEADS = 8
QK_NOPE = 64
QK_ROPE = 32
V_HEAD = 64
Q_LORA = 384
KV_LORA = 256
D_FF = 2816
FFN_CONV_WIDTH = 3
CHUNK_SHIFT = 6
N_META = 16
ROPE_THETA = 10000.0
EPS = 1e-6
NEG = -1e30
ADAM_LR = 0.001
ADAM_B1 = 0.9
ADAM_B2 = 0.999
ADAM_EPS = 1e-08
ADAM_WD = 0.01
ADAM_STEP = 10

LANES = 128
SUBLANES = 8
HB = LANES
D_HEADS = N_HEADS * HB
TM = 256
DEAD = TM - N_META
D_AG = 2 * D_CONV
D_ZP = D_AG + Q_LORA + KV_LORA + HB
D_MIX = D_CONV + D_HEADS
LN2 = 0.6931471805599453
Q_SCALE = (QK_NOPE + QK_ROPE) ** -0.5 / LN2
HALO_CONV = 32
HALO_FFN = 16
FF_CHUNK = 256
FF_MXU_CHUNK = 256
FF_TILE = D_FF // 2
VMEM_LIMIT = 56 * 1024 * 1024
ADAMW_MAX_STEPS = 32
N_CHIPS = 4
HEAD_GROUP = 4
N_GROUPS = N_HEADS // HEAD_GROUP
MESH =pl.DeviceIdType.MESH


def _params(n_grid):
    return pltpu.CompilerParams(dimension_semantics=("arbitrary",) * n_grid, vmem_limit_bytes=VMEM_LIMIT)


def _rows(tm, c, off=0):
    return pl.BlockSpec((tm, c), lambda i: (i, off))


def _full(shape):
    return pl.BlockSpec(shape, lambda i: (0,) * len(shape))


def _prev(hb, c, tm, off=0):
    return pl.BlockSpec((hb, c), lambda i: (jnp.maximum(i * (tm // hb) - 1, 0), off))


def _next(hb, c, tm, nblk, off=0):
    return pl.BlockSpec((hb, c), lambda i: (jnp.minimum((i + 1) * (tm // hb), nblk - 1), off))


def _sds(shape, dtype):
    return jax.ShapeDtypeStruct(shape, dtype)


def _rms_r(x, n):
    return lax.rsqrt(jnp.sum(x * x, -1, keepdims=True) * (1.0 / n) + EPS)


def _rms_bwd(dy, x, r, g, n):
    gd = dy * g
    dx = r * gd - x * (r * r * r) * (jnp.sum(x * gd, -1, keepdims=True) * (1.0 / n))
    return dx, jnp.sum(dy * x * r, 0, keepdims=True)


def _dot(a, b, dims):
    return lax.dot_general(a, b, (dims, ((), ())), preferred_element_type=F32)


NN = ((1,), (0,))
NT = ((1,), (1,))
TN = ((0,), (0,))


def _rope(x, c, s1, s2):
    n = x.shape[-1]
    return x * c + pltpu.roll(x, n - QK_ROPE // 2, 1) * s1 + pltpu.roll(x, QK_ROPE // 2, 1) * s2


def _rope_bwd(g, c, s1, s2):
    n = g.shape[-1]
    return g * c + pltpu.roll(g * s1, QK_ROPE // 2, 1) + pltpu.roll(g * s2, n - QK_ROPE // 2, 1)


def _row_ids(tm, cols=1):
    return pl.program_id(0) * tm + lax.broadcasted_iota(jnp.int32, (tm, cols), 0)


def _mm(a, b, dims, out_dtype, tm, tn, tk, name, by_col_tile=False, b_slabs=False):
    if dims == TN:
        (kk, m), (_, n) = a.shape, b.shape
        a_spec = pl.BlockSpec((tk, tm), lambda i, j, k: (k, i))
    else:
        m, kk = a.shape
        a_spec = pl.BlockSpec((tm, tk), lambda i, j, k: (i, k))
    if dims == NT and b_slabs:
        n = b.shape[1]
        assert b.shape[2] == tk and kk == b.shape[0] * tk, (name, b.shape)
        b_spec = pl.BlockSpec((None, tn, tk), lambda i, j, k: (k, j, 0))
    elif dims == NT:
        n = b.shape[0]
        b_spec = pl.BlockSpec((tn, tk), lambda i, j, k: (j, k))
    elif b_slabs:
        n = b.shape[0] * b.shape[2]
        assert b.shape[2] == tn and kk == b.shape[1], (name, b.shape)
        b_spec = pl.BlockSpec((None, tk, tn), lambda i, j, k: (j, k, 0))
    else:
        n = b.shape[1]
        b_spec = pl.BlockSpec((tk, tn), lambda i, j, k: (k, j))
    assert m % tm == 0 and n % tn == 0 and kk % tk == 0, (name, a.shape, b.shape, tm, tn, tk)
    nk = kk // tk

    def body(a_ref, b_ref, o_ref, acc_ref):
        k = pl.program_id(2)

        @pl.when(k == 0)
        def _():
            acc_ref[...] = jnp.zeros_like(acc_ref)

        acc_ref[...] += _dot(a_ref[...].astype(BF16), b_ref[...].astype(BF16), dims)

        @pl.when(k == nk - 1)
        def _():
            o_ref[...] = acc_ref[...].astype(out_dtype)

    if by_col_tile:
        out_spec, out_shape = pl.BlockSpec((None, tm, tn), lambda i, j, k: (j, i, 0)), (n // tn, m, tn)
    else:
        out_spec, out_shape = pl.BlockSpec((tm, tn), lambda i, j, k: (i, j)), (m, n)
    return pl.pallas_call(
        body, name=name, grid=(m // tm, n // tn, nk), in_specs=[a_spec, b_spec], out_specs=out_spec,
        out_shape=_sds(out_shape, out_dtype), scratch_shapes=[pltpu.VMEM((tm, tn), F32)],
        compiler_params=_params(3))(a, b)


def _fwd_in(x, head, gmix, win, gq, wuq, gkv, wukv, rc, rs1, rs2):
    L = TM + x.shape[0]

    def body(x_ref, head_ref, gmix_ref, win_ref, gq_ref, wuq_ref, gkv_ref, wukv_ref, c_ref, s1_ref, s2_ref,
             h0_ref, n_ref, zag_ref, u0_ref, cq_ref, ckv_ref, qn_ref, kvn_ref, q_ref, kv_ref, kr_ref, kvt_ref):
        h = jnp.where(pl.program_id(0) == 0, head_ref[...], x_ref[...])
        h0_ref[...] = h
        n = (h * _rms_r(h, D_MODEL) * gmix_ref[...]).astype(BF16)
        n_ref[...] = n
        z = _dot(n, win_ref[...], NN)
        a, gate = z[:, :D_CONV], z[:, D_CONV:D_AG]
        zag_ref[...] = z[:, :D_AG].astype(BF16)
        u0_ref[...] = a * jax.nn.sigmoid(gate)
        cq = z[:, D_AG:D_AG + Q_LORA]
        ckv = z[:, D_AG + Q_LORA:D_AG + Q_LORA + KV_LORA]
        krp = z[:, D_AG + Q_LORA + KV_LORA:]
        cq_ref[...] = cq
        ckv_ref[...] = ckv
        qn = (cq * _rms_r(cq, Q_LORA) * gq_ref[...]).astype(BF16)
        qn_ref[...] = qn
        kvn = (ckv * _rms_r(ckv, KV_LORA) * gkv_ref[...]).astype(BF16)
        kvn_ref[...] = kvn
        c, s1, s2 = c_ref[...], s1_ref[...], s2_ref[...]
        q = _dot(qn, wuq_ref[...], NN)
        q = _rope(q, jnp.tile(c, (1, N_HEADS)), jnp.tile(s1, (1, N_HEADS)), jnp.tile(s2, (1, N_HEADS)))
        q_ref[...] = (q * Q_SCALE).astype(BF16)
        kv = _dot(kvn, wukv_ref[...], NN)
        kv_ref[...] = kv.astype(BF16)
        kvt_ref[...] = kv.T.astype(BF16)
        kr_ref[...] = _rope(krp, c, s1, s2).astype(BF16)

    outs = [(D_MODEL, F32), (D_MODEL, BF16), (D_AG, BF16), (D_CONV, F32), (Q_LORA, F32), (KV_LORA, F32), (Q_LORA, BF16),
            (KV_LORA, BF16), (D_HEADS, BF16), (D_HEADS, BF16), (HB, BF16)]
    return pl.pallas_call(
        body, name="fwd_in", grid=(L // TM,),
        in_specs=[pl.BlockSpec((TM, D_MODEL), lambda i: (jnp.maximum(i - 1, 0), 0)), _full(head.shape),
                  _full(gmix.shape), _full(win.shape), _full(gq.shape), _full(wuq.shape),
                  _full(gkv.shape), _full(wukv.shape), _rows(TM, HB), _rows(TM, HB), _rows(TM, HB)],
        out_specs=[_rows(TM, c) for c, _ in outs] + [pl.BlockSpec((D_HEADS, TM), lambda i: (0, i))],
        out_shape=[_sds((L, c), d) for c, d in outs] + [_sds((D_HEADS, L), BF16)],
        compiler_params=_params(1))(x, head, gmix, win, gq, wuq, gkv, wukv, rc, rs1, rs2)


def _conv_taps(xx, w_ref, halo, tm, flip):
    kw = w_ref.shape[0]
    acc, rolls = None, {}
    for k in range(kw):
        term = w_ref[k:k + 1, :] * _shifted_rows(xx, kw - 1 - k, halo, tm, flip, rolls)
        acc = term if acc is None else acc + term
    return acc


def _shifted_rows(xx, d, halo, tm, flip, rolls):
    a, b = divmod(d, SUBLANES)
    if b not in rolls:
        rolls[b] = xx if b == 0 else pltpu.roll(xx, (xx.shape[0] - b) if flip else b, 0)
    start = SUBLANES * a if flip else halo - SUBLANES * a
    return rolls[b][start:start + tm]


def _ln_silu(u1, lg, lb):
    mu = jnp.mean(u1, -1, keepdims=True)
    xc = u1 - mu
    rs = lax.rsqrt(jnp.mean(xc * xc, -1, keepdims=True) + EPS)
    xh = xc * rs
    u2 = xh * lg + lb
    sg = jax.nn.sigmoid(u2)
    return rs, xh, u2, sg, u2 * sg


def _fwd_conv(u0, cw, cb, lg, lb, og):
    L = u0.shape[0]

    def body(u0_ref, u0p_ref, cw_ref, cb_ref, lg_ref, lb_ref, og_ref, u1_ref, mixa_ref):
        halo = jnp.where(pl.program_id(0) > 0, u0p_ref[...], 0.0)
        xx = jnp.concatenate([halo, u0_ref[...]], 0)
        u1 = _conv_taps(xx, cw_ref, HALO_CONV, TM, False) + cb_ref[...]
        u1_ref[...] = u1
        u = _ln_silu(u1, lg_ref[...], lb_ref[...])[4]
        mixa_ref[...] = (u * _rms_r(u, D_CONV) * og_ref[...]).astype(BF16)

    return pl.pallas_call(
        body, name="fwd_conv", grid=(L // TM,),
        in_specs=[_rows(TM, D_CONV), _prev(HALO_CONV, D_CONV, TM), _full(cw.shape), _full(cb.shape),
                  _full(lg.shape), _full(lb.shape), _full(og.shape)],
        out_specs=[_rows(TM, D_CONV), _rows(TM, D_CONV)],
        out_shape=[_sds((L, D_CONV), F32), _sds((L, D_CONV), BF16)],
        compiler_params=_params(1))(u0, u0, cw, cb, lg, lb, og)


def _visible_t(i, j, t):
    key = j * t + lax.broadcasted_iota(jnp.int32, (t, t), 0)
    query = i * t + lax.broadcasted_iota(jnp.int32, (t, t), 1)
    return (lax.shift_right_logical(key, CHUNK_SHIFT) <= lax.shift_right_logical(query, CHUNK_SHIFT)) & (key >= DEAD)


def _stat_lane(h):
    return (h // HEAD_GROUP) * LANES + h % HEAD_GROUP


def _scatter_stats(cols, t):
    lane = lax.broadcasted_iota(jnp.int32, (t, N_GROUPS * LANES), 1)
    out = jnp.zeros((t, N_GROUPS * LANES), F32)
    for h, col in enumerate(cols):
        out = jnp.where(lane == _stat_lane(h), col, out)
    return out


def _resident(shape, index_map):
    return pl.BlockSpec(shape, index_map, pipeline_mode=pl.Buffered(1))


def _attn_fwd(q, kv, kvt, kr, shards=()):
    L = q.shape[0]
    t = TM
    nq = L // t
    n = len(shards)
    pass_step = (5 * nq) // 6

    def body(q_ref, kv_ref, kvt_ref, kr_ref, *refs):
        gather_refs = refs[:n] + refs[n + 2:2 * n + 2] + refs[2 * n + 6:]
        o_ref, lse_ref = refs[n:n + 2]
        qt_scr, m_scr, l_scr, acc_scr = refs[2 * n + 2:2 * n + 6]
        i = pl.program_id(0)
        if n:
            @pl.when(i == 0)
            def _():
                for cp in _gather_copies(gather_refs[:n], gather_refs[n:2 * n], *gather_refs[2 * n:])[0]:
                    cp().start()

        lane = lax.broadcasted_iota(jnp.int32, (t, HB), 1)
        heads = range(N_HEADS)
        cols = [slice(h * HB, (h + 1) * HB) for h in heads]
        for h in heads:
            qt_scr[cols[h], :] = q_ref[:, cols[h]].astype(F32).T.astype(BF16)
        m_scr[...] = jnp.full_like(m_scr, NEG)
        l_scr[...] = jnp.zeros_like(l_scr)
        acc_scr[...] = jnp.zeros_like(acc_scr)

        def tile(keys, vis, whole=True):
            krj = kr_ref[keys, :]
            kvj = [kv_ref[keys, cols[h]] for h in heads]
            lane_k = lane[:krj.shape[0]]
            s = [_dot(jnp.where(lane_k < QK_NOPE, kvj[h], krj), qt_scr[cols[h], :], NN) for h in heads]
            p, alpha = [], []
            for h in heads:
                sh = s[h] if vis is None else jnp.where(vis, s[h], NEG)
                m_prev = m_scr[h:h + 1, :]
                m_new = jnp.maximum(m_prev, jnp.max(sh, 0, keepdims=True))
                a = jnp.exp2(m_prev - m_new)
                ph = jnp.exp2(sh - m_new)
                l_scr[h:h + 1, :] = a * l_scr[h:h + 1, :] + jnp.sum(ph, 0, keepdims=True)
                m_scr[h:h + 1, :] = m_new
                p.append(ph.astype(BF16))
                alpha.append(a)
            for h in heads:
                pv = _dot(kvt_ref[cols[h], keys], p[h], NN) if whole else _dot(kvj[h], p[h], TN)
                acc_scr[cols[h], :] = alpha[h] * acc_scr[cols[h], :] + pv

        tile(pl.ds(pl.multiple_of(i * t, t), t), _visible_t(i, i, t))

        @pl.when(i > 0)
        def _():
            tile(pl.ds(DEAD, N_META), None, whole=False)

        def unmasked(j, carry):
            tile(pl.ds(pl.multiple_of(j * t, t), t), None)
            return carry

        lax.fori_loop(1, i, unmasked, 0)
        lse_ref[...] = jnp.zeros_like(lse_ref)
        for h in heads:
            l = l_scr[h:h + 1, :]
            o_ref[:, cols[h]] = jnp.where(lane >= QK_NOPE, (acc_scr[cols[h], :] / l).T, 0.0).astype(BF16)
            lse_ref[h // HEAD_GROUP, h % HEAD_GROUP:h % HEAD_GROUP + 1, :] = m_scr[h:h + 1, :] + jnp.log2(l)
        if n:
            @pl.when(i == pass_step)
            def _():
                _, arrivals, forwards, _ = _gather_copies(gather_refs[:n], gather_refs[n:2 * n], *gather_refs[2 * n:])
                for landed, onward in zip(arrivals, forwards):
                    landed().wait_recv()
                    onward().start()

            @pl.when(i == nq - 1)
            def _():
                sends, _, forwards, finals = _gather_copies(gather_refs[:n], gather_refs[n:2 * n], *gather_refs[2 * n:])
                for cp in finals:
                    cp().wait_recv()
                for cp in sends + forwards:
                    cp().wait_send()

    any_spec = pl.BlockSpec(memory_space=pl.ANY)
    outs = pl.pallas_call(
        body, name="attn_fwd", grid=(nq,),
        in_specs=[_rows(t, D_HEADS), _resident((L, D_HEADS), lambda i: (0, 0)),
                  _resident((D_HEADS, L), lambda i: (0, 0)), _resident((L, HB), lambda i: (0, 0))] + [any_spec] * n,
        out_specs=[_rows(t, D_HEADS), pl.BlockSpec((N_GROUPS, SUBLANES, t), lambda i: (0, 0, i))] + [any_spec] * n,
        out_shape=[_sds((L, D_HEADS), BF16), _sds((N_GROUPS, SUBLANES, L), F32)] + _gather_out_shapes(shards),
        scratch_shapes=[pltpu.VMEM((D_HEADS, t), BF16), pltpu.VMEM((N_HEADS, t), F32), pltpu.VMEM((N_HEADS, t), F32),
                        pltpu.VMEM((D_HEADS, t), F32)] + (_gather_semaphores(n) if n else []),
        compiler_params=_params(1))(q, kv, kvt, kr, *shards)
    return outs[0], outs[1], _gathered(outs[2:], shards)


def _fwd_mix(h0, mixa, o, ga, wout, gffn):
    L = h0.shape[0]

    def body(h0_ref, mixa_ref, o_ref, ga_ref, wout_ref, gffn_ref, mix_ref, h1_ref, n2_ref):
        of = o_ref[...].astype(F32)
        mixb = (of * _rms_r(of, N_HEADS * V_HEAD) * ga_ref[...]).astype(BF16)
        mix = jnp.concatenate([mixa_ref[...], mixb], 1)
        mix_ref[...] = mix
        mo = jnp.where(_row_ids(TM) >= DEAD, _dot(mix, wout_ref[...], NN), 0.0)
        h1 = h0_ref[...] + mo
        h1_ref[...] = h1
        n2_ref[...] = (h1 * _rms_r(h1, D_MODEL) * gffn_ref[...]).astype(BF16)

    return pl.pallas_call(
        body, name="fwd_mix", grid=(L // TM,),
        in_specs=[_rows(TM, D_MODEL), _rows(TM, D_CONV), _rows(TM, D_HEADS), _full(ga.shape), _full(wout.shape),
                  _full(gffn.shape)],
        out_specs=[_rows(TM, D_MIX), _rows(TM, D_MODEL), _rows(TM, D_MODEL)],
        out_shape=[_sds((L, D_MIX), BF16), _sds((L, D_MODEL), F32), _sds((L, D_MODEL), BF16)],
        compiler_params=_params(1))(h0, mixa, o, ga, wout, gffn)


def _ffn_act_chunk(c, upg_ref, upv_ref, hg, hv, fcw_ref, fcb_ref):
    cs = slice(c * FF_CHUNK, (c + 1) * FF_CHUNK)
    out = []
    for part, (up_ref, halo) in enumerate(((upg_ref, hg), (upv_ref, hv))):
        xx = jnp.concatenate([halo[:, cs], up_ref[:, cs].astype(F32)], 0)
        ws = slice(part * D_FF + c * FF_CHUNK, part * D_FF + (c + 1) * FF_CHUNK)
        y = (fcw_ref[0:1, ws] * pltpu.roll(xx, 2, 0)[HALO_FFN:] + fcw_ref[1:2, ws] * pltpu.roll(xx, 1, 0)[HALO_FFN:]
             + fcw_ref[2:3, ws] * xx[HALO_FFN:] + fcb_ref[:, ws])
        out.append(y)
    return out


def _ffn_in_specs(L):
    return [_rows(TM, D_FF, 0), _rows(TM, D_FF, 1), _prev(HALO_FFN, D_FF, TM, 0), _prev(HALO_FFN, D_FF, TM, 1)]


def _ffn_halos(hg_ref, hv_ref):
    first = pl.program_id(0) == 0
    return (jnp.where(first, 0.0, hg_ref[...].astype(F32)), jnp.where(first, 0.0, hv_ref[...].astype(F32)))


def _fwd_ffn_loss(up0, fcw, fcb, wdown, h1, target, gfin):
    L = h1.shape[0]

    def body(upg_ref, upv_ref, hg_ref, hv_ref, fcw_ref, fcb_ref, wd_ref, h1_ref, t_ref, gf_ref,
             dh2_ref, loss_ref, dgf_ref, up_ref, act_ref):
        i = pl.program_id(0)
        hg, hv = _ffn_halos(hg_ref, hv_ref)
        for c in range(D_FF // FF_CHUNK):
            cs = slice(c * FF_CHUNK, (c + 1) * FF_CHUNK)
            g, val = _ffn_act_chunk(c, upg_ref, upv_ref, hg, hv, fcw_ref, fcb_ref)
            up_ref[:, cs] = g.astype(BF16)
            up_ref[:, D_FF + c * FF_CHUNK:D_FF + (c + 1) * FF_CHUNK] = val.astype(BF16)
            act_ref[:, cs] = (g * jax.nn.sigmoid(g) * val).astype(BF16)
        h2 = h1_ref[...] + _dot(act_ref[...], wd_ref[...], NN)
        r = _rms_r(h2, D_MODEL)
        gf = gf_ref[...]
        err = jnp.where(i > 0, h2 * r * gf - t_ref[...], 0.0)
        dy = err * (1.0 / D_MODEL)
        dh2, dgf = _rms_bwd(dy, h2, r, gf, D_MODEL)
        dh2_ref[...] = dh2

        @pl.when(i == 0)
        def _():
            loss_ref[...] = jnp.zeros_like(loss_ref)
            dgf_ref[...] = jnp.zeros_like(dgf_ref)

        loss_ref[...] += jnp.sum(err * err) * (0.5 / D_MODEL)
        dgf_ref[...] += dgf

    return pl.pallas_call(
        body, name="fwd_ffn_loss", grid=(L // TM,),
        in_specs=_ffn_in_specs(L) + [_full(fcw.shape), _full(fcb.shape), _full(wdown.shape), _rows(TM, D_MODEL),
                                     pl.BlockSpec((TM, D_MODEL), lambda i: (jnp.maximum(i - 1, 0), 0)),
                                     _full(gfin.shape)],
        out_specs=[_rows(TM, D_MODEL), _full((1, LANES)), _full((1, D_MODEL)), _rows(TM, 2 * D_FF), _rows(TM, D_FF)],
        out_shape=[_sds((L, D_MODEL), F32), _sds((1, LANES), F32), _sds((1, D_MODEL), F32),
                   _sds((L, 2 * D_FF), BF16), _sds((L, D_FF), BF16)],
        compiler_params=_params(1))(up0, up0, up0, up0, fcw, fcb, wdown, h1, target, gfin)


def _bwd_ffn_act(dh2, up, wdown):
    L = dh2.shape[0]

    def body(dh2_ref, upg_ref, upv_ref, wd_ref, dup_ref, dfcb_ref):
        da = _dot(dh2_ref[...].astype(BF16), wd_ref[...], NT)

        @pl.when(pl.program_id(0) == 0)
        def _():
            dfcb_ref[...] = jnp.zeros_like(dfcb_ref)

        for c in range(D_FF // FF_CHUNK):
            cs = slice(c * FF_CHUNK, (c + 1) * FF_CHUNK)
            vs = slice(D_FF + c * FF_CHUNK, D_FF + (c + 1) * FF_CHUNK)
            g, val = upg_ref[:, cs].astype(F32), upv_ref[:, cs].astype(F32)
            sg = jax.nn.sigmoid(g)
            si = g * sg
            dac = da[:, cs]
            dg = dac * val * (sg * (1.0 + g * (1.0 - sg)))
            dv = dac * si
            dup_ref[:, cs] = dg.astype(BF16)
            dup_ref[:, vs] = dv.astype(BF16)
            dfcb_ref[:, cs] += jnp.sum(dg, 0, keepdims=True)
            dfcb_ref[:, vs] += jnp.sum(dv, 0, keepdims=True)

    return pl.pallas_call(
        body, name="bwd_ffn_act", grid=(L // TM,),
        in_specs=[_rows(TM, D_MODEL), _rows(TM, D_FF, 0), _rows(TM, D_FF, 1), _full(wdown.shape)],
        out_specs=[_rows(TM, 2 * D_FF), _full((1, 2 * D_FF))],
        out_shape=[_sds((L, 2 * D_FF), BF16), _sds((1, 2 * D_FF), F32)],
        compiler_params=_params(1))(dh2, up, up, wdown)


def _bwd_ffn_conv_up(dup, up0, fcw, wup, tl):
    L, C = dup.shape
    tc = FF_TILE
    nt = L // tl
    nhb = L // HALO_FFN
    chunks = [(c0, min(FF_MXU_CHUNK, tc - c0)) for c0 in range(0, tc, FF_MXU_CHUNK)]

    def body(dy_ref, dyn_ref, x_ref, xp_ref, w_ref, wup_ref, dn2_ref, dx_ref, dw_ref, acc_ref):
        i, k = pl.program_id(0), pl.program_id(1)

        @pl.when(k == 0)
        def _():
            acc_ref[...] = jnp.zeros_like(acc_ref)

        last, first = i == nt - 1, i == 0
        for c0, cw in chunks:
            cs = slice(c0, c0 + cw)
            yy = jnp.concatenate([dy_ref[:, cs].astype(F32), jnp.where(last, 0.0, dyn_ref[:, cs].astype(F32))], 0)
            w = w_ref[:, cs]
            dx = (w[0:1] * pltpu.roll(yy, tl + HALO_FFN - 2, 0)[:tl] + w[1:2] * pltpu.roll(yy, tl + HALO_FFN - 1, 0)[:tl]
                  + w[2:3] * yy[:tl]).astype(BF16)
            dx_ref[:, cs] = dx
            acc_ref[...] += _dot(dx, wup_ref[:, cs], NT)
            xx = jnp.concatenate([jnp.where(first, 0.0, xp_ref[:, cs].astype(F32)), x_ref[:, cs].astype(F32)], 0)
            dy = yy[:tl]
            dw_ref[0:1, cs] = jnp.sum(dy * pltpu.roll(xx, 2, 0)[HALO_FFN:], 0, keepdims=True)
            dw_ref[1:2, cs] = jnp.sum(dy * pltpu.roll(xx, 1, 0)[HALO_FFN:], 0, keepdims=True)
            dw_ref[2:3, cs] = jnp.sum(dy * xx[HALO_FFN:], 0, keepdims=True)

        @pl.when(k == C // tc - 1)
        def _():
            dn2_ref[...] = acc_ref[...]

    tile = pl.BlockSpec((tl, tc), lambda i, k: (i, k))
    per = tl // HALO_FFN
    return pl.pallas_call(
        body, name="bwd_ffn_conv_up", grid=(nt, C // tc),
        in_specs=[tile, pl.BlockSpec((HALO_FFN, tc), lambda i, k: (jnp.minimum((i + 1) * per, nhb - 1), k)),
                  tile, pl.BlockSpec((HALO_FFN, tc), lambda i, k: (jnp.maximum(i * per - 1, 0), k)),
                  pl.BlockSpec((FFN_CONV_WIDTH, tc), lambda i, k: (0, k)),
                  pl.BlockSpec((None, D_MODEL, tc), lambda i, k: (k, 0, 0))],
        out_specs=[pl.BlockSpec((tl, D_MODEL), lambda i, k: (i, 0)), tile,
                   pl.BlockSpec((None, FFN_CONV_WIDTH, tc), lambda i, k: (i, 0, k))],
        out_shape=[_sds((L, D_MODEL), F32), _sds((L, C), BF16), _sds((nt, FFN_CONV_WIDTH, C), F32)],
        scratch_shapes=[pltpu.VMEM((tl, D_MODEL), F32)],
        compiler_params=_params(2))(dup, dup, up0, up0, fcw, wup)


def _carrying(core, n_in, n_out, n, copies_fn, steps):
    def body(*refs):
        exchange = (refs[n_in:n_in + n], refs[n_in + n + n_out:n_in + 2 * n + n_out]) + refs[n_in + 2 * n + n_out:]
        if n:
            @pl.when(pl.program_id(0) == 0)
            def _():
                for cp in copies_fn(*exchange):
                    cp().start()

        core(*refs[:n_in], *refs[n_in + n:n_in + n + n_out])
        if n:
            @pl.when(pl.program_id(0) == steps - 1)
            def _():
                for cp in copies_fn(*exchange):
                    cp().wait()

    return body


def _bwd_mix(dh2, dn2, h1, gffn, wout, o, ga, u1, lg, lb, og, parts=()):
    L = h1.shape[0]

    def body(dh2_ref, dn2_ref, h1_ref, gffn_ref, wout_ref, o_ref, ga_ref, u1_ref, lg_ref, lb_ref, og_ref,
             dh1_ref, dh1m_ref, do_ref, delta_ref, du1_ref, dgffn_ref, dga_ref, dog_ref, dlg_ref, dlb_ref):
        h1 = h1_ref[...]
        dn2x, dgffn = _rms_bwd(dn2_ref[...], h1, _rms_r(h1, D_MODEL), gffn_ref[...], D_MODEL)
        dh1 = dh2_ref[...] + dn2x
        dh1_ref[...] = dh1
        dh1m = jnp.where(_row_ids(TM) >= DEAD, dh1, 0.0).astype(BF16)
        dh1m_ref[...] = dh1m
        dmix = _dot(dh1m, wout_ref[...], NT)
        dma, dmb = dmix[:, :D_CONV], dmix[:, D_CONV:]
        of = o_ref[...].astype(F32)
        do, dga = _rms_bwd(dmb, of, _rms_r(of, N_HEADS * V_HEAD), ga_ref[...], N_HEADS * V_HEAD)
        do_ref[...] = do.astype(BF16)
        prod = do * of
        by_lane = _scatter_stats([jnp.sum(prod[:, h * HB:(h + 1) * HB], -1, keepdims=True) for h in range(N_HEADS)], TM)
        by_row = by_lane.T
        for grp in range(N_GROUPS):
            delta_ref[grp] = by_row[grp * LANES:grp * LANES + SUBLANES, :]
        lg = lg_ref[...]
        rs, xh, u2, sg, u = _ln_silu(u1_ref[...], lg, lb_ref[...])
        du, dog = _rms_bwd(dma, u, _rms_r(u, D_CONV), og_ref[...], D_CONV)
        du2 = du * (sg * (1.0 + u2 * (1.0 - sg)))
        dxh = du2 * lg
        du1_ref[...] = rs * (dxh - jnp.mean(dxh, -1, keepdims=True) - xh * jnp.mean(dxh * xh, -1, keepdims=True))

        @pl.when(pl.program_id(0) == 0)
        def _():
            for ref in (dgffn_ref, dga_ref, dog_ref, dlg_ref, dlb_ref):
                ref[...] = jnp.zeros_like(ref)

        dgffn_ref[...] += dgffn
        dga_ref[...] += dga
        dog_ref[...] += dog
        dlg_ref[...] += jnp.sum(du2 * xh, 0, keepdims=True)
        dlb_ref[...] += jnp.sum(du2, 0, keepdims=True)

    n = len(parts)
    any_spec = pl.BlockSpec(memory_space=pl.ANY)
    outs = pl.pallas_call(
        _carrying(body, 11, 10, n, _sibling_exchange_copies, L // TM), name="bwd_mix", grid=(L // TM,),
        in_specs=[_rows(TM, D_MODEL), _rows(TM, D_MODEL), _rows(TM, D_MODEL), _full(gffn.shape), _full(wout.shape),
                  _rows(TM, D_HEADS), _full(ga.shape), _rows(TM, D_CONV), _full(lg.shape), _full(lb.shape),
                  _full(og.shape)] + [any_spec] * n,
        out_specs=[_rows(TM, D_MODEL), _rows(TM, D_MODEL), _rows(TM, D_HEADS),
                   pl.BlockSpec((N_GROUPS, SUBLANES, TM), lambda i: (0, 0, i)),
                   _rows(TM, D_CONV), _full((1, D_MODEL)), _full((1, D_HEADS)), _full((1, D_CONV)),
                   _full((1, D_CONV)), _full((1, D_CONV))] + [any_spec] * n,
        out_shape=[_sds((L, D_MODEL), F32), _sds((L, D_MODEL), BF16), _sds((L, D_HEADS), BF16),
                   _sds((N_GROUPS, SUBLANES, L), F32),
                   _sds((L, D_CONV), F32), _sds((1, D_MODEL), F32), _sds((1, D_HEADS), F32), _sds((1, D_CONV), F32),
                   _sds((1, D_CONV), F32), _sds((1, D_CONV), F32)] + _sibling_exchange_shapes(parts),
        scratch_shapes=_sibling_exchange_semaphores(n) if n else [],
        compiler_params=_params(1))(dh2, dn2, h1, gffn, wout, o, ga, u1, lg, lb, og, *parts)
    return outs[:10], list(outs[10:])


def _attn_bwd(q, kv, kr, do, lse, delta, parts=()):
    L = q.shape[0]
    t = TM
    nt = L // t
    gw = HEAD_GROUP * HB
    n = len(parts)

    def body(q_ref, kv_ref, kr_ref, do_ref, lse_ref, delta_ref, *refs):
        dq_ref, dkv_ref, dkr_ref = refs[n:n + 3]
        dqt_acc, dk_acc, dv_acc, kkt_scr = refs[2 * n + 3:2 * n + 7]
        exchange_refs = (refs[:n], refs[n + 3:2 * n + 3]) + refs[2 * n + 7:]
        g, j = pl.program_id(0), pl.program_id(1)
        if n:
            @pl.when((g == 0) & (j == 0))
            def _():
                for cp in _chip_copies(*exchange_refs)[0]:
                    cp().start()

        lane = lax.broadcasted_iota(jnp.int32, (t, HB), 1)

        @pl.when(j == 0)
        def _():
            dqt_acc[...] = jnp.zeros_like(dqt_acc)

        @pl.when((j == 0) & (g == 0))
        def _():
            dkr_ref[...] = jnp.zeros_like(dkr_ref)

        dk_acc[...] = jnp.zeros_like(dk_acc)
        dv_acc[...] = jnp.zeros_like(dv_acc)
        krj = kr_ref[...]
        heads = range(HEAD_GROUP)
        cols = [slice(h * HB, (h + 1) * HB) for h in heads]
        for hc in cols:
            kkt_scr[hc, :] = jnp.where(lane < QK_NOPE, kv_ref[:, hc], krj).astype(F32).T.astype(BF16)

        def tile(i, vis, whole=True):
            qs = pl.ds(pl.multiple_of(i * t, t), t)
            keys = slice(None) if whole else slice(DEAD, t)
            kvj = [kv_ref[keys, hc] for hc in cols]
            lane_k = lane[:kvj[0].shape[0]]
            kk = [jnp.where(lane_k < QK_NOPE, kvj[h], krj[keys]) for h in heads]
            qi = [q_ref[qs, hc] for hc in cols]
            doi = [do_ref[qs, hc] for hc in cols]
            s = [_dot(kk[h], qi[h].astype(F32).T.astype(BF16), NN) for h in heads]
            dp = [_dot(kvj[h], doi[h].astype(F32).T.astype(BF16), NN) for h in heads]
            p = []
            for h in heads:
                sh = s[h] if vis is None else jnp.where(vis, s[h], NEG)
                p.append(jnp.exp2(sh - lse_ref[h:h + 1, qs]))
            for h in heads:
                dv_acc[keys, cols[h]] += _dot(p[h].astype(BF16), doi[h], NN)
            ds = [(p[h] * (dp[h] - delta_ref[h:h + 1, qs]) * LN2).astype(BF16) for h in heads]
            for h in heads:
                dk_acc[keys, cols[h]] += _dot(ds[h], qi[h], NN)
            for h in heads:
                dqt = _dot(kkt_scr[cols[h], :], ds[h], NN) if whole else _dot(kk[h], ds[h], TN)
                dqt_acc[cols[h], qs] += dqt

        @pl.when(j == 0)
        def _():
            tile(0, _visible_t(0, 0, t)[DEAD:], whole=False)

            def meta_keys(i, carry):
                tile(i, None, whole=False)
                return carry

            lax.fori_loop(1, nt, meta_keys, 0)

        @pl.when(j > 0)
        def _():
            tile(j, _visible_t(j, j, t))

            count = nt - 1 - j

            def unmasked_pair(r, carry):
                tile(j + 1 + 2 * r, None)
                tile(j + 2 + 2 * r, None)
                return carry

            lax.fori_loop(0, count // 2, unmasked_pair, 0)

            @pl.when(count % 2 == 1)
            def _():
                tile(nt - 1, None)

        dkr = jnp.zeros((t, HB), F32)
        for h in range(HEAD_GROUP):
            hc = slice(h * HB, (h + 1) * HB)
            dk = dk_acc[:, hc]
            dkv_ref[:, hc] = jnp.where(lane < QK_NOPE, dk, dv_acc[:, hc]).astype(BF16)
            dkr = dkr + jnp.where(lane >= QK_NOPE, dk, 0.0)
        dkr_ref[pl.ds(pl.multiple_of(j * t, t), t), :] += dkr

        @pl.when(j == nt - 1)
        def _():
            def untranspose(i, carry):
                qs = pl.ds(pl.multiple_of(i * t, t), t)
                dq_ref[qs, :] = (dqt_acc[:, qs].T * Q_SCALE).astype(BF16)
                return carry

            lax.fori_loop(0, nt, untranspose, 0)

        if n:
            @pl.when((g == N_GROUPS - 1) & (j == nt - 1))
            def _():
                sends, arrivals = _chip_copies(*exchange_refs)
                for cp in arrivals:
                    cp().wait_recv()
                for cp in sends:
                    cp().wait_send()

    group = lambda g, j: (0, g)
    stats = _resident((None, SUBLANES, L), lambda g, j: (g, 0, 0))
    any_spec = pl.BlockSpec(memory_space=pl.ANY)
    outs = pl.pallas_call(
        body, name="attn_bwd", grid=(N_GROUPS, nt),
        in_specs=[_resident((L, gw), group), pl.BlockSpec((t, gw), lambda g, j: (j, g)),
                  pl.BlockSpec((t, HB), lambda g, j: (j, 0)), _resident((L, gw), group), stats, stats]
        + [any_spec] * n,
        out_specs=[pl.BlockSpec((L, gw), group), pl.BlockSpec((t, gw), lambda g, j: (j, g)),
                   pl.BlockSpec((L, HB), lambda g, j: (0, 0))] + [any_spec] * n,
        out_shape=[_sds((L, D_HEADS), BF16), _sds((L, D_HEADS), BF16), _sds((L, HB), F32)]
        + [_sds(p.shape, p.dtype) for p in parts],
        scratch_shapes=[pltpu.VMEM((gw, L), F32), pltpu.VMEM((t, gw), F32), pltpu.VMEM((t, gw), F32),
                        pltpu.VMEM((gw, t), BF16)] + (_chip_semaphores(n) if n else []),
        compiler_params=_params(2))(q, kv, kr, do, lse, delta, *parts)
    return outs[0], outs[1], outs[2], list(outs[3:])


def _bwd_conv(du1, u0, cw, zag, halves=()):
    L = du1.shape[0]
    nt = L // TM

    def body(dy_ref, dyn_ref, x_ref, cw_ref, zag_ref, dzag_ref, dcw_ref, dcb_ref):
        i = pl.program_id(0)
        dy = dy_ref[...]
        yy = jnp.concatenate([dy, jnp.where(i < nt - 1, dyn_ref[...], 0.0)], 0)
        x = x_ref[...]

        @pl.when(i == 0)
        def _():
            dcw_ref[...] = jnp.zeros_like(dcw_ref)
            dcb_ref[...] = jnp.zeros_like(dcb_ref)

        du0, rolls = None, {}
        for k in range(CONV_WIDTH):
            ahead = _shifted_rows(yy, CONV_WIDTH - 1 - k, HALO_CONV, TM, True, rolls)
            term = cw_ref[k:k + 1, :] * ahead
            du0 = term if du0 is None else du0 + term
            dcw_ref[k:k + 1, :] += jnp.sum(ahead * x, 0, keepdims=True)
        dcb_ref[...] += jnp.sum(dy, 0, keepdims=True)
        zag = zag_ref[...].astype(F32)
        a, sg = zag[:, :D_CONV], jax.nn.sigmoid(zag[:, D_CONV:])
        dzag_ref[...] = jnp.concatenate([du0 * sg, du0 * a * sg * (1.0 - sg)], 1).astype(BF16)

    n = len(halves)
    any_spec = pl.BlockSpec(memory_space=pl.ANY)
    outs = pl.pallas_call(
        _carrying(body, 5, 3, n, _sibling_gather_copies, nt), name="bwd_conv", grid=(nt,),
        in_specs=[_rows(TM, D_CONV), _next(HALO_CONV, D_CONV, TM, L // HALO_CONV), _rows(TM, D_CONV),
                  _full(cw.shape), _rows(TM, D_AG)] + [any_spec] * n,
        out_specs=[_rows(TM, D_AG), _full(cw.shape), _full((1, D_CONV))] + [any_spec] * n,
        out_shape=[_sds((L, D_AG), BF16), _sds(cw.shape, F32), _sds((1, D_CONV), F32)]
        + [_sds(p.shape, p.dtype) for p in halves],
        input_output_aliases={5 + a: 3 + a for a in range(n)},
        scratch_shapes=_sibling_gather_semaphores(n) if n else [],
        compiler_params=_params(1))(du1, du1, u0, cw, zag, *halves)
    return outs[:3], list(outs[3:])


def _bwd_in(dzag, dq, dkv, dkr, cq, ckv, gq, gkv, wuq, wukv, win, rc, rs1, rs2, h0, gmix, dh1):
    L = h0.shape[0]

    def body(dzag_ref, dq_ref, dkv_ref, dkr_ref, cq_ref, ckv_ref, gq_ref, gkv_ref, wuq_ref, wukv_ref, win_ref,
             c_ref, s1_ref, s2_ref, h0_ref, gmix_ref, dh1_ref,
             dz_ref, dqr_ref, gx_ref, dfirst_ref, dgq_ref, dgkv_ref, dgmix_ref):
        i = pl.program_id(0)
        c, s1, s2 = c_ref[...], s1_ref[...], s2_ref[...]
        dqr = _rope_bwd(dq_ref[...].astype(F32), jnp.tile(c, (1, N_HEADS)), jnp.tile(s1, (1, N_HEADS)),
                        jnp.tile(s2, (1, N_HEADS))).astype(BF16)
        dqr_ref[...] = dqr
        cq, ckv = cq_ref[...], ckv_ref[...]
        dcq, dgq = _rms_bwd(_dot(dqr, wuq_ref[...], NT), cq, _rms_r(cq, Q_LORA), gq_ref[...], Q_LORA)
        dckv, dgkv = _rms_bwd(_dot(dkv_ref[...], wukv_ref[...], NT), ckv, _rms_r(ckv, KV_LORA), gkv_ref[...], KV_LORA)
        dkrp = _rope_bwd(dkr_ref[...], c, s1, s2)
        dz = jnp.concatenate([dzag_ref[...], dcq.astype(BF16), dckv.astype(BF16), dkrp.astype(BF16)], 1)
        dz_ref[...] = dz
        h0 = h0_ref[...]
        dnx, dgmix = _rms_bwd(_dot(dz, win_ref[...], NT), h0, _rms_r(h0, D_MODEL), gmix_ref[...], D_MODEL)
        dh0 = dh1_ref[...] + dnx

        @pl.when(i == 0)
        def _():
            dfirst_ref[...] = dh0
            for ref in (dgq_ref, dgkv_ref, dgmix_ref):
                ref[...] = jnp.zeros_like(ref)

        @pl.when(i > 0)
        def _():
            gx_ref[...] = dh0

        dgq_ref[...] += dgq
        dgkv_ref[...] += dgkv
        dgmix_ref[...] += dgmix

    return pl.pallas_call(
        body, name="bwd_in", grid=(L // TM,),
        in_specs=[_rows(TM, D_AG), _rows(TM, D_HEADS), _rows(TM, D_HEADS), _rows(TM, HB), _rows(TM, Q_LORA),
                  _rows(TM, KV_LORA), _full(gq.shape), _full(gkv.shape), _full(wuq.shape), _full(wukv.shape),
                  _full(win.shape), _rows(TM, HB), _rows(TM, HB), _rows(TM, HB), _rows(TM, D_MODEL),
                  _full(gmix.shape), _rows(TM, D_MODEL)],
        out_specs=[_rows(TM, D_ZP), _rows(TM, D_HEADS),
                   pl.BlockSpec((TM, D_MODEL), lambda i: (jnp.maximum(i - 1, 0), 0)), _full((TM, D_MODEL)),
                   _full((1, Q_LORA)), _full((1, KV_LORA)), _full((1, D_MODEL))],
        out_shape=[_sds((L, D_ZP), BF16), _sds((L, D_HEADS), BF16), _sds((L - TM, D_MODEL), F32),
                   _sds((TM, D_MODEL), F32), _sds((1, Q_LORA), F32), _sds((1, KV_LORA), F32), _sds((1, D_MODEL), F32)],
        compiler_params=_params(1))(dzag, dq, dkv, dkr, cq, ckv, gq, gkv, wuq, wukv, win, rc, rs1, rs2, h0, gmix, dh1)


def _mesh_pos():
    return lax.axis_index("x"), lax.axis_index("y"), lax.axis_index("c")


def _remote_copy(src, dst, send_sem, recv_sem, to):
    return functools.partial(pltpu.make_async_remote_copy, src, dst, send_sem, recv_sem, device_id=to,
                             device_id_type=MESH)


def _all_gather(shards):
    n = len(shards)

    def body(*refs):
        sends, arrivals, forwards, finals = _gather_copies(refs[:n], refs[n:2 * n], *refs[2 * n:])
        for cp in sends:
            cp().start()
        for landed, onward in zip(arrivals, forwards):
            landed().wait_recv()
            onward().start()
        for cp in finals:
            cp().wait_recv()
        for cp in sends + forwards:
            cp().wait_send()

    any_spec = pl.BlockSpec(memory_space=pl.ANY)
    outs = pl.pallas_call(
        body, name="all_gather_weights", in_specs=[any_spec] * n, out_specs=[any_spec] * n,
        out_shape=_gather_out_shapes(shards), scratch_shapes=_gather_semaphores(n))(*shards)
    return _gathered(outs, shards)


def _gather_out_shapes(shards):
    return [_sds((2 * N_CHIPS, s.shape[0] // 2) + s.shape[1:], s.dtype) for s in shards]


def _gather_semaphores(n):
    return [pltpu.SemaphoreType.DMA((n, 8)), pltpu.SemaphoreType.DMA((n, 8))]


def _gathered(outs, shards):
    return [o.reshape((N_CHIPS, s.shape[0]) + s.shape[1:]) for o, s in zip(outs, shards)]


def _gather_copies(ins, outs, send_sems, recv_sems):
    x, y, c = _mesh_pos()
    chips = [(1 - x, y), (x, 1 - y), (1 - x, 1 - y)]
    sends, arrivals, forwards, finals = [], [], [], []

    def copy(src, dst, a, k, to):
        return _remote_copy(src, dst, send_sems.at[a, k], recv_sems.at[a, k], to)

    for a, (src, out) in enumerate(zip(ins, outs)):
        m = out.shape[1]
        mine = src.at[pl.ds(pl.multiple_of(c * m, 16), m)]
        for hf in range(2):
            own = out.at[4 * x + 2 * y + hf]
            sends.append(copy(src.at[pl.ds(hf * m, m)], own, a, 6 + hf, (x, y, 1 - c)))
            finals.append(copy(own, own, a, 6 + hf, (x, y, 1 - c)))
        for k, chip in enumerate(chips):
            slot = 4 * chip[0] + 2 * chip[1]
            sends.append(copy(mine, out.at[4 * x + 2 * y + c], a, k, (*chip, c)))
            arrivals.append(copy(out.at[slot + c], out.at[slot + c], a, k, (*chip, c)))
            forwards.append(copy(out.at[slot + c], out.at[slot + c], a, 3 + k, (x, y, 1 - c)))
            finals.append(copy(out.at[slot + 1 - c], out.at[slot + 1 - c], a, 3 + k, (x, y, 1 - c)))
    return sends, arrivals, forwards, finals


def _sibling_exchange(parts, name):
    n = len(parts)

    def body(*refs):
        copies = _sibling_exchange_copies(refs[:n], refs[n:2 * n], *refs[2 * n:])
        for cp in copies:
            cp().start()
        for cp in copies:
            cp().wait()

    any_spec = pl.BlockSpec(memory_space=pl.ANY)
    return pl.pallas_call(
        body, name=name, in_specs=[any_spec] * n, out_specs=[any_spec] * n,
        out_shape=_sibling_exchange_shapes(parts), scratch_shapes=_sibling_exchange_semaphores(n))(*parts)


def _sibling_exchange_shapes(parts):
    return [_sds((N_CHIPS, p.shape[1] // 2, p.shape[2]), p.dtype) for p in parts]


def _sibling_exchange_semaphores(n):
    return [pltpu.SemaphoreType.DMA((n, N_CHIPS)), pltpu.SemaphoreType.DMA((n, N_CHIPS))]


def _sibling_exchange_copies(ins, theirs, send_sems, recv_sems):
    x, y, c = _mesh_pos()
    copies = []
    for a, (src, dst) in enumerate(zip(ins, theirs)):
        h = dst.shape[1]
        rows = pl.ds(pl.multiple_of((1 - c) * h, 16), h)
        copies += [_remote_copy(src.at[q, rows], dst.at[q], send_sems.at[a, q], recv_sems.at[a, q], (x, y, 1 - c))
                   for q in range(N_CHIPS)]
    return copies


def _chip_semaphores(n):
    return [pltpu.SemaphoreType.DMA((n, 3)), pltpu.SemaphoreType.DMA((n, 3))]


def _chip_copies(ins, outs, send_sems, recv_sems):
    x, y, c = _mesh_pos()
    me = 2 * x + y
    sends, arrivals = [], []
    for a, (src, out) in enumerate(zip(ins, outs)):
        for k, chip in enumerate([(1 - x, y), (x, 1 - y), (1 - x, 1 - y)]):
            slot = 2 * chip[0] + chip[1]
            sems = (send_sems.at[a, k], recv_sems.at[a, k], (*chip, c))
            sends.append(_remote_copy(src.at[slot], out.at[me], *sems))
            arrivals.append(_remote_copy(out.at[slot], out.at[slot], *sems))
    return sends, arrivals


def _sibling_gather(parts, name):
    n = len(parts)

    def body(*refs):
        copies = _sibling_gather_copies(refs[:n], refs[n:2 * n], *refs[2 * n:])
        for cp in copies:
            cp().start()
        for cp in copies:
            cp().wait()

    any_spec = pl.BlockSpec(memory_space=pl.ANY)
    return pl.pallas_call(
        body, name=name, in_specs=[any_spec] * n, out_specs=[any_spec] * n,
        out_shape=[_sds(p.shape, p.dtype) for p in parts], input_output_aliases={a: a for a in range(n)},
        scratch_shapes=_sibling_gather_semaphores(n))(*parts)


def _sibling_gather_semaphores(n):
    return [pltpu.SemaphoreType.DMA((n,)), pltpu.SemaphoreType.DMA((n,))]


def _sibling_gather_copies(ins, outs, send_sems, recv_sems):
    x, y, c = _mesh_pos()
    return [_remote_copy(src.at[c], dst.at[c], send_sems.at[a], recv_sems.at[a], (x, y, 1 - c))
            for a, (src, dst) in enumerate(zip(ins, outs))]


def _row_tile(rows, row_bytes, align, budget=1 << 20):
    best = None
    for t in range(align, rows + 1, align):
        if rows % t == 0 and t * row_bytes <= budget:
            best = t
    return best or rows


def _scalar(v):
    return jnp.reshape(v, (1,)).astype(jnp.int32)


def _add_pair(part, theirs, c, name):
    _, h, cols = theirs.shape
    tr = _row_tile(h, cols * 4, 16, budget=1 << 21)
    nb = h // tr

    def body(c_ref, a_ref, b_ref, o_ref):
        o_ref[...] = (a_ref[...].astype(F32) + b_ref[...].astype(F32)).astype(o_ref.dtype)

    half = pl.BlockSpec((None, tr, cols), lambda q, i, c_ref: (q, i, 0))
    grid_spec = pltpu.PrefetchScalarGridSpec(
        num_scalar_prefetch=1, grid=(N_CHIPS, nb),
        in_specs=[pl.BlockSpec((None, tr, cols), lambda q, i, c_ref: (q, c_ref[0] * nb + i, 0)), half],
        out_specs=half)
    return pl.pallas_call(body, name=name, grid_spec=grid_spec, out_shape=_sds(theirs.shape, part.dtype),
                          compiler_params=_params(2))(_scalar(c), part, theirs)


def _add_chips(got, own, me, c, name):
    _, h, cols = got.shape
    tr = _row_tile(h, cols * 4 * N_CHIPS, 16, budget=1 << 22)

    def body(pos_ref, got_ref, own_ref, o_ref):
        acc = None
        for q in range(N_CHIPS):
            term = jnp.where(pos_ref[0] == q, own_ref[q], got_ref[q]).astype(F32)
            acc = term if acc is None else acc + term
        o_ref[...] = acc

    by_chip = pl.BlockSpec((N_CHIPS, tr, cols), lambda i, pos_ref: (0, i, 0))
    grid_spec = pltpu.PrefetchScalarGridSpec(
        num_scalar_prefetch=1, grid=(h // tr,), in_specs=[by_chip, by_chip],
        out_specs=pl.BlockSpec((None, tr, cols), lambda i, pos_ref: (pos_ref[1], i, 0)))
    return pl.pallas_call(body, name=name, grid_spec=grid_spec, out_shape=_sds((2, h, cols), F32),
                          compiler_params=_params(1))(jnp.stack([me, c]).astype(jnp.int32), got, own)


def _add_pairs(parts, theirs, tag):
    c = lax.axis_index("c")
    return [_add_pair(p, t, c, f"grad_add_pair_{tag}_{a}") for a, (p, t) in enumerate(zip(parts, theirs))]


def _add_all_chips(pair, got, tag):
    x, y, c = _mesh_pos()
    return [_add_chips(g, p, 2 * x + y, c, f"grad_add_chips_{tag}_{a}") for a, (g, p) in enumerate(zip(got, pair))]


def _totals(both):
    return [b.reshape(-1, b.shape[-1]) for b in both]


def _reduce_begin(parts, tag):
    return _add_pairs(parts, _sibling_exchange(parts, f"grad_sibling_exchange_{tag}"), tag)


def _reduce_end(pair, got, tag):
    return _totals(_sibling_gather(_add_all_chips(pair, got, tag), f"grad_sibling_gather_{tag}"))


def _adamw_math(w, g, m, v):
    m = ADAM_B1 * m + (1.0 - ADAM_B1) * g
    v = ADAM_B2 * v + (1.0 - ADAM_B2) * (g * g)
    m_hat = m / (1.0 - ADAM_B1 ** ADAM_STEP)
    v_hat = v / (1.0 - ADAM_B2 ** ADAM_STEP)
    return -ADAM_LR * (m_hat / (jnp.sqrt(v_hat) + ADAM_EPS) + ADAM_WD * w), m, v


def _adamw_big(w, g, m, v, name, parts=()):
    r, c = w.shape
    tr = _row_tile(r, c * 4, 8, budget=1 << 19)
    if r // tr > ADAMW_MAX_STEPS:
        tr = r
    n = len(parts)
    steps = r // tr

    def body(w_ref, g_ref, m_ref, v_ref, *refs):
        go_ref, d_ref, mo_ref, vo_ref = refs[n:n + 4]
        exchange_refs = (refs[:n], refs[n + 4:2 * n + 4]) + refs[2 * n + 4:]
        if n:
            @pl.when(pl.program_id(0) == 0)
            def _():
                for cp in _chip_copies(*exchange_refs)[0]:
                    cp().start()

        g = g_ref[...]
        go_ref[...] = g
        d_ref[...], mo_ref[...], vo_ref[...] = _adamw_math(w_ref[...], g, m_ref[...], v_ref[...])
        if n:
            @pl.when(pl.program_id(0) == steps - 1)
            def _():
                sends, arrivals = _chip_copies(*exchange_refs)
                for cp in arrivals:
                    cp().wait_recv()
                for cp in sends:
                    cp().wait_send()

    any_spec = pl.BlockSpec(memory_space=pl.ANY)
    outs = pl.pallas_call(
        body, name=name, grid=(steps,), in_specs=[_rows(tr, c)] * 4 + [any_spec] * n,
        out_specs=[_rows(tr, c)] * 4 + [any_spec] * n,
        out_shape=[_sds((r, c), F32)] * 4 + [_sds(p.shape, p.dtype) for p in parts],
        scratch_shapes=_chip_semaphores(n) if n else [], compiler_params=_params(1))(w, g, m, v, *parts)
    return outs[:4], list(outs[4:])


def _adamw_small(ws, gs, ms, vs):
    n = len(ws)

    def body(*refs):
        for a in range(n):
            w_ref, g_ref, m_ref, v_ref = (refs[k * n + a] for k in range(4))
            d, m, v = _adamw_math(w_ref[...], g_ref[...], m_ref[...], v_ref[...])
            refs[4 * n + a][...] = d
            refs[5 * n + a][...] = m
            refs[6 * n + a][...] = v

    vm = pl.BlockSpec(memory_space=pltpu.VMEM)
    outs = pl.pallas_call(
        body, name="adamw_small", in_specs=[vm] * (4 * n), out_specs=[vm] * (3 * n),
        out_shape=[_sds(w.shape, F32) for w in ws] * 3)(*ws, *gs, *ms, *vs)
    return outs[:n], outs[n:2 * n], outs[2 * n:]


BIG = ("w_in", "w_uq", "w_ukv", "w_out", "w_ffn_up", "w_ffn_down")
SMALL_SHARDED = ("conv_w", "ffn_conv_w", "meta_tokens")
REPLICATED = ("mix_norm_g", "q_norm_g", "kv_norm_g", "conv_b", "conv_ln_g", "conv_ln_b", "conv_out_g", "attn_out_g",
              "ffn_norm_g", "ffn_conv_b", "final_norm_g")
WEIGHTS = ("meta_tokens", "mix_norm_g", "w_in", "q_norm_g", "w_uq", "kv_norm_g", "w_ukv", "conv_w", "conv_b",
           "conv_ln_g", "conv_ln_b", "conv_out_g", "attn_out_g", "w_out", "ffn_norm_g", "w_ffn_up", "ffn_conv_w",
           "ffn_conv_b", "w_ffn_down", "final_norm_g")


def _lane_rows(a):
    return a.reshape(N_CHIPS, -1, LANES)


def _col_shards(a):
    k = a.shape[0]
    return a.reshape(k, N_CHIPS, -1).transpose(1, 0, 2)


def _from_col_shards(a):
    return a.transpose(1, 0, 2).reshape(a.shape[1], -1)


def _pad_rows_to(a, rows):
    return jnp.pad(a, ((0, 0), (0, rows - a.shape[1]), (0, 0)))


def _rope_tables(L):
    pos = (jnp.arange(L, dtype=jnp.int32) - DEAD).astype(F32)
    inv_freq = 1.0 / (ROPE_THETA ** (jnp.arange(0, QK_ROPE, 2, dtype=F32) / QK_ROPE))
    ang = pos[:, None] * inv_freq[None, :]
    cos, sin = jnp.cos(ang), jnp.sin(ang)
    half = QK_ROPE // 2
    z = lambda n: jnp.zeros((L, n), F32)
    rc = jnp.concatenate([jnp.ones((L, QK_NOPE), F32), cos, cos, z(HB - QK_NOPE - QK_ROPE)], 1)
    rs1 = jnp.concatenate([z(QK_NOPE), -sin, z(HB - QK_NOPE - half)], 1)
    rs2 = jnp.concatenate([z(QK_NOPE + half), sin, z(HB - QK_NOPE - QK_ROPE)], 1)
    return rc, rs1, rs2


def _pad_heads(g):
    return jnp.pad(g.reshape(N_HEADS, V_HEAD), ((0, 0), (HB - V_HEAD, 0))).reshape(1, D_HEADS)


def _unpad_heads(g):
    return g.reshape(N_HEADS, HB)[:, HB - V_HEAD:].reshape(1, N_HEADS * V_HEAD)


def _local_step(x, target, w, late_shards=None, reduce_first=False):
    S = x.shape[0]
    L = TM + S
    tl = L // 4
    d_qk = QK_NOPE + QK_ROPE
    win_n = w["w_in"]
    kr0 = D_AG + Q_LORA + KV_LORA
    win = jnp.concatenate([win_n[:, :kr0], jnp.zeros((D_MODEL, QK_NOPE), BF16), win_n[:, kr0:],
                           jnp.zeros((D_MODEL, HB - d_qk), BF16)], 1)
    wuq = jnp.pad(w["w_uq"].reshape(Q_LORA, N_HEADS, d_qk), ((0, 0), (0, 0), (0, HB - d_qk))).reshape(Q_LORA, D_HEADS)
    wukv = w["w_ukv"]
    ga = _pad_heads(w["attn_out_g"])
    gfin = w["final_norm_g"].reshape(1, D_MODEL)
    rc, rs1, rs2 = _rope_tables(L)
    head = jnp.concatenate([jnp.zeros((DEAD, D_MODEL), F32), w["meta_tokens"]], 0)

    h0, n, zag, u0, cq, ckv, qn, kvn, q, kv, kr, kvt = _fwd_in(x, head, w["mix_norm_g"], win, w["q_norm_g"], wuq,
                                                                w["kv_norm_g"], wukv, rc, rs1, rs2)
    u1, mixa = _fwd_conv(u0, w["conv_w"], w["conv_b"], w["conv_ln_g"], w["conv_ln_b"], w["conv_out_g"])
    if late_shards is None:
        o, lse, _ = _attn_fwd(q, kv, kvt, kr)
        wout_n, wup, wdown = w["w_out"], _col_shards(w["w_ffn_up"]), w["w_ffn_down"]
    else:
        o, lse, late = _attn_fwd(q, kv, kvt, kr, [late_shards[k] for k in LATE])
        wout_n, wup, wdown = _full_weight("w_out", late[0]), late[1], _full_weight("w_ffn_down", late[2])
    wout = jnp.concatenate([wout_n[:D_CONV], jnp.pad(wout_n[D_CONV:].reshape(N_HEADS, V_HEAD, D_MODEL),
                                                     ((0, 0), (HB - V_HEAD, 0), (0, 0))).reshape(D_HEADS, D_MODEL)], 0)
    mix, h1, n2 = _fwd_mix(h0, mixa, o, ga, wout, w["ffn_norm_g"])
    up0 = _mm(n2, wup, NN, BF16, tl, FF_TILE, D_MODEL, "ffn_up", b_slabs=True)
    dh2, loss, g_fin, up, act = _fwd_ffn_loss(up0, w["ffn_conv_w"], w["ffn_conv_b"], wdown, h1, target, gfin)

    dup, g_fcb = _bwd_ffn_act(dh2, up, wdown)
    dn2, dup0, g_fcw_tiles = _bwd_ffn_conv_up(dup, up0, w["ffn_conv_w"], wup, tl)
    g_fcw = jnp.sum(g_fcw_tiles, 0)
    g_wup = _mm(n2, dup0, TN, BF16, D_MODEL, FF_TILE, tl, "ffn_up_dw", by_col_tile=True)
    g_wdown = _mm(act, dh2, TN, BF16, D_FF // 2, D_MODEL, tl, "ffn_down_dw").reshape(N_CHIPS, -1, D_MODEL)
    ffn = [g_wup, g_wdown] if reduce_first else []
    (dh1, dh1m, do, delta, du1, g_gffn, g_ga, g_og, g_lg, g_lb), theirs = _bwd_mix(
        dh2, dn2, h1, w["ffn_norm_g"], wout, o, ga, u1, w["conv_ln_g"], w["conv_ln_b"], w["conv_out_g"], ffn)
    g_wout = _mm(mix, dh1m, TN, BF16, D_MIX // 2, D_MODEL, tl, "out_dw")
    g_wout = jnp.concatenate([g_wout[:D_CONV], g_wout[D_CONV:].reshape(N_HEADS, HB, D_MODEL)[:, HB - V_HEAD:]
                              .reshape(N_HEADS * V_HEAD, D_MODEL)], 0).reshape(N_CHIPS, -1, D_MODEL)
    pair = ()
    if reduce_first:
        theirs += _sibling_exchange([g_wout], "grad_sibling_exchange_out")
        pair = _add_pairs(ffn + [g_wout], theirs, "first")
    dq, dkv, dkr, got = _attn_bwd(q, kv, kr, do, lse, delta, pair)
    halves = _add_all_chips(pair, got, "first") if reduce_first else ()
    (dzag, g_cw, g_cb), both = _bwd_conv(du1, u0, w["conv_w"], zag, halves)
    if reduce_first:
        g_wup, g_wdown, g_wout = _totals(both)
    dz, dqr, gx, dfirst, g_gq, g_gkv, g_gmix = _bwd_in(dzag, dq, dkv, dkr, cq, ckv, w["q_norm_g"], w["kv_norm_g"],
                                                      wuq, wukv, win, rc, rs1, rs2, h0, w["mix_norm_g"], dh1)
    g_win = _mm(n, dz, TN, BF16, D_MODEL, D_ZP // 2, tl, "in_dw")
    g_wuq = _mm(qn, dqr, TN, BF16, Q_LORA, D_HEADS, tl, "uq_dw")
    g_wukv = _col_shards(_mm(kvn, dkv, TN, BF16, KV_LORA, D_HEADS, tl, "ukv_dw"))

    grads = {
        "w_in": _col_shards(jnp.concatenate([g_win[:, :kr0], g_win[:, kr0 + QK_NOPE:kr0 + d_qk]], 1)),
        "w_uq": _col_shards(g_wuq.reshape(Q_LORA, N_HEADS, HB)[:, :, :d_qk].reshape(Q_LORA, N_HEADS * d_qk)),
        "w_ukv": g_wukv,
        "w_out": g_wout,
        "w_ffn_up": g_wup, "w_ffn_down": g_wdown, "conv_w": g_cw, "ffn_conv_w": g_fcw,
        "meta_tokens": dfirst[DEAD:], "mix_norm_g": g_gmix, "q_norm_g": g_gq, "kv_norm_g": g_gkv, "conv_b": g_cb,
        "conv_ln_g": g_lg, "conv_ln_b": g_lb, "conv_out_g": g_og, "attn_out_g": _unpad_heads(g_ga),
        "ffn_norm_g": g_gffn, "ffn_conv_b": g_fcb, "final_norm_g": g_fin,
    }
    return loss, gx, grads


ROW_SHARDED = ("w_out", "w_ffn_down")


TRANSPOSED = ("w_in", "w_uq")
REDUCED_FIRST = ("w_ffn_up", "w_ffn_down", "w_out")
LATE = ("w_out", "w_ffn_up", "w_ffn_down")


def _full_weight(name, by_chip):
    return by_chip.reshape(-1, by_chip.shape[-1]) if name in ROW_SHARDED else _from_col_shards(by_chip)


def kernel(x, meta_tokens, mix_norm_g, w_in, q_norm_g, w_uq, kv_norm_g, w_ukv, conv_w, conv_b, conv_ln_g, conv_ln_b, conv_out_g, attn_out_g, w_out, ffn_norm_g, w_ffn_up, ffn_conv_w, ffn_conv_b, w_ffn_down, final_norm_g, loss_target, m_meta_tokens, m_mix_norm_g, m_w_in, m_q_norm_g, m_w_uq, m_kv_norm_g, m_w_ukv, m_conv_w, m_conv_b, m_conv_ln_g, m_conv_ln_b, m_conv_out_g, m_attn_out_g, m_w_out, m_ffn_norm_g, m_w_ffn_up, m_ffn_conv_w, m_ffn_conv_b, m_w_ffn_down, m_final_norm_g, v_meta_tokens, v_mix_norm_g, v_w_in, v_q_norm_g, v_w_uq, v_kv_norm_g, v_w_ukv, v_conv_w, v_conv_b, v_conv_ln_g, v_conv_ln_b, v_conv_out_g, v_attn_out_g, v_w_out, v_ffn_norm_g, v_w_ffn_up, v_ffn_conv_w, v_ffn_conv_b, v_w_ffn_down, v_final_norm_g):
    local = dict(meta_tokens=meta_tokens, mix_norm_g=mix_norm_g, w_in=w_in[0], q_norm_g=q_norm_g, w_uq=w_uq[0],
                 kv_norm_g=kv_norm_g, w_ukv=w_ukv[0], conv_w=conv_w[0], conv_b=conv_b, conv_ln_g=conv_ln_g,
                 conv_ln_b=conv_ln_b, conv_out_g=conv_out_g, attn_out_g=attn_out_g, w_out=w_out[0],
                 ffn_norm_g=ffn_norm_g, w_ffn_up=w_ffn_up[0], ffn_conv_w=ffn_conv_w[0], ffn_conv_b=ffn_conv_b,
                 w_ffn_down=w_ffn_down[0], final_norm_g=final_norm_g.reshape(1, D_MODEL))
    ms = dict(zip(WEIGHTS, (m_meta_tokens, m_mix_norm_g, m_w_in, m_q_norm_g, m_w_uq, m_kv_norm_g, m_w_ukv, m_conv_w,
                            m_conv_b, m_conv_ln_g, m_conv_ln_b, m_conv_out_g, m_attn_out_g, m_w_out, m_ffn_norm_g,
                            m_w_ffn_up, m_ffn_conv_w, m_ffn_conv_b, m_w_ffn_down, m_final_norm_g)))
    vs = dict(zip(WEIGHTS, (v_meta_tokens, v_mix_norm_g, v_w_in, v_q_norm_g, v_w_uq, v_kv_norm_g, v_w_ukv, v_conv_w,
                            v_conv_b, v_conv_ln_g, v_conv_ln_b, v_conv_out_g, v_attn_out_g, v_w_out, v_ffn_norm_g,
                            v_w_ffn_up, v_ffn_conv_w, v_ffn_conv_b, v_w_ffn_down, v_final_norm_g)))

    small_flat = jnp.concatenate([local[k].reshape(-1) for k in SMALL_SHARDED]).reshape(-1, LANES)
    early = [k for k in BIG if k not in LATE]
    gathered = _all_gather([local[k].astype(BF16) for k in early] + [small_flat])
    full = {k: v for k, v in local.items() if k not in LATE}
    for name, g in zip(early, gathered[:len(early)]):
        full[name] = _full_weight(name, g)
    small = gathered[-1].reshape(N_CHIPS, -1)
    at = 0
    for name in SMALL_SHARDED:
        r, c = local[name].shape
        full[name] = _from_col_shards(small[:, at:at + r * c].reshape(N_CHIPS, r, c))
        at += r * c

    loss_row, grad_x, grads = _local_step(x[0], loss_target[0], full, {k: local[k].astype(BF16) for k in LATE},
                                          reduce_first=True)

    rest_big = [k for k in BIG if k not in REDUCED_FIRST]
    rep = jnp.concatenate([grads[k].reshape(-1) for k in REPLICATED] + [loss_row.reshape(-1)]).reshape(1, -1, LANES)
    small_pieces = [_lane_rows(_col_shards(grads[k])) for k in SMALL_SHARDED]
    small_pieces.append(jnp.broadcast_to(rep, (N_CHIPS,) + rep.shape[1:]))
    small_rows = sum(p.shape[1] for p in small_pieces)
    small_pack = _pad_rows_to(jnp.concatenate(small_pieces, 1), -(-small_rows // 32) * 32)
    pair = _reduce_begin([grads[k] for k in rest_big] + [small_pack], "rest")

    total = {k: grads[k] for k in REDUCED_FIRST}
    delta, new_m, new_v = {}, {}, {}
    shape2 = lambda a, name: a.reshape(local[name].shape)
    turn = lambda a, name: a.T if name in TRANSPOSED else a

    def update(name, parts=()):
        outs, got = _adamw_big(turn(local[name], name), turn(total[name], name), turn(shape2(ms[name], name), name),
                               turn(shape2(vs[name], name), name), "adamw_" + name, parts)
        total[name], delta[name], new_m[name], new_v[name] = (turn(o, name) for o in outs)
        return got

    *rest_tot, small_tot = _reduce_end(pair, update(REDUCED_FIRST[0], pair), "rest")
    total.update(zip(rest_big, rest_tot))
    flat = small_tot.reshape(-1)
    at = 0
    for name in SMALL_SHARDED + REPLICATED:
        shape = local[name].shape
        size = shape[0] * shape[1]
        total[name] = flat[at:at + size].reshape(shape)
        at += -(-size // LANES) * LANES if name in SMALL_SHARDED else size
    loss = flat[at]

    for name in BIG:
        if name != REDUCED_FIRST[0]:
            update(name)
    rest = SMALL_SHARDED + REPLICATED
    ds, nms, nvs = _adamw_small([local[k] for k in rest], [total[k] for k in rest],
                                [shape2(ms[k], k) for k in rest], [shape2(vs[k], k) for k in rest])
    for k, d, nm, nv in zip(rest, ds, nms, nvs):
        delta[k], new_m[k], new_v[k] = d, nm, nv

    out_shape = dict(zip(WEIGHTS, (meta_tokens, mix_norm_g, w_in, q_norm_g, w_uq, kv_norm_g, w_ukv, conv_w, conv_b,
                                   conv_ln_g, conv_ln_b, conv_out_g, attn_out_g, w_out, ffn_norm_g, w_ffn_up,
                                   ffn_conv_w, ffn_conv_b, w_ffn_down, final_norm_g)))
    outs = [loss, grad_x[None]]
    for group in (total, delta, new_m, new_v):
        outs += [group[k].reshape(out_shape[k].shape) for k in WEIGHTS]
    return tuple(outs)
```

```python
import functools

import jax
import jax.numpy as jnp
from jax import lax
from jax.experimental import pallas as pl
from jax.experimental.pallas import tpu as pltpu

F32 = jnp.float32
BF16 = jnp.bfloat16

D_MODEL = 1024
D_CONV = 512
CONV_WIDTH = 31
N_HEADS = 8
QK_NOPE = 64
QK_ROPE = 32
V_HEAD = 64
Q_LORA = 384
KV_LORA = 256
D_FF = 2816
FFN_CONV_WIDTH = 3
CHUNK_SHIFT = 6
N_META = 16
ROPE_THETA = 10000.0
EPS = 1e-6
NEG = -1e30
ADAM_LR = 0.001
ADAM_B1 = 0.9
ADAM_B2 = 0.999
ADAM_EPS = 1e-08
ADAM_WD = 0.01
ADAM_STEP = 10

LANES = 128
SUBLANES = 8
HB = LANES
D_HEADS = N_HEADS * HB
TM = 256
DEAD = TM - N_META
D_AG = 2 * D_CONV
D_ZP = D_AG + Q_LORA + KV_LORA + HB
D_MIX = D_CONV + D_HEADS
LN2 = 0.6931471805599453
Q_SCALE = (QK_NOPE + QK_ROPE) ** -0.5 / LN2
HALO_CONV = 32
HALO_FFN = 16
FF_CHUNK = 256
FF_MXU_CHUNK = 256
FF_TILE = D_FF // 2
VMEM_LIMIT = 56 * 1024 * 1024
ADAMW_MAX_STEPS = 32
N_CHIPS = 4
HEAD_GROUP = 4
N_GROUPS = N_HEADS // HEAD_GROUP
MESH =pl.DeviceIdType.MESH


def _params(n_grid):
    return pltpu.CompilerParams(dimension_semantics=("arbitrary",) * n_grid, vmem_limit_bytes=VMEM_LIMIT)


def _rows(tm, c, off=0):
    return pl.BlockSpec((tm, c), lambda i: (i, off))


def _full(shape):
    return pl.BlockSpec(shape, lambda i: (0,) * len(shape))


def _prev(hb, c, tm, off=0):
    return pl.BlockSpec((hb, c), lambda i: (jnp.maximum(i * (tm // hb) - 1, 0), off))


def _next(hb, c, tm, nblk, off=0):
    return pl.BlockSpec((hb, c), lambda i: (jnp.minimum((i + 1) * (tm // hb), nblk - 1), off))


def _sds(shape, dtype):
    return jax.ShapeDtypeStruct(shape, dtype)


def _rms_r(x, n):
    return lax.rsqrt(jnp.sum(x * x, -1, keepdims=True) * (1.0 / n) + EPS)


def _rms_bwd(dy, x, r, g, n):
    gd = dy * g
    dx = r * gd - x * (r * r * r) * (jnp.sum(x * gd, -1, keepdims=True) * (1.0 / n))
    return dx, jnp.sum(dy * x * r, 0, keepdims=True)


def _dot(a, b, dims):
    return lax.dot_general(a, b, (dims, ((), ())), preferred_element_type=F32)


NN = ((1,), (0,))
NT = ((1,), (1,))
TN = ((0,), (0,))


def _rope(x, c, s1, s2):
    n = x.shape[-1]
    return x * c + pltpu.roll(x, n - QK_ROPE // 2, 1) * s1 + pltpu.roll(x, QK_ROPE // 2, 1) * s2


def _rope_bwd(g, c, s1, s2):
    n = g.shape[-1]
    return g * c + pltpu.roll(g * s1, QK_ROPE // 2, 1) + pltpu.roll(g * s2, n - QK_ROPE // 2, 1)


def _row_ids(tm, cols=1):
    return pl.program_id(0) * tm + lax.broadcasted_iota(jnp.int32, (tm, cols), 0)


def _mm(a, b, dims, out_dtype, tm, tn, tk, name, by_col_tile=False, b_slabs=False):
    if dims == TN:
        (kk, m), (_, n) = a.shape, b.shape
        a_spec = pl.BlockSpec((tk, tm), lambda i, j, k: (k, i))
    else:
        m, kk = a.shape
        a_spec = pl.BlockSpec((tm, tk), lambda i, j, k: (i, k))
    if dims == NT and b_slabs:
        n = b.shape[1]
        assert b.shape[2] == tk and kk == b.shape[0] * tk, (name, b.shape)
        b_spec = pl.BlockSpec((None, tn, tk), lambda i, j, k: (k, j, 0))
    elif dims == NT:
        n = b.shape[0]
        b_spec = pl.BlockSpec((tn, tk), lambda i, j, k: (j, k))
    elif b_slabs:
        n = b.shape[0] * b.shape[2]
        assert b.shape[2] == tn and kk == b.shape[1], (name, b.shape)
        b_spec = pl.BlockSpec((None, tk, tn), lambda i, j, k: (j, k, 0))
    else:
        n = b.shape[1]
        b_spec = pl.BlockSpec((tk, tn), lambda i, j, k: (k, j))
    assert m % tm == 0 and n % tn == 0 and kk % tk == 0, (name, a.shape, b.shape, tm, tn, tk)
    nk = kk // tk

    def body(a_ref, b_ref, o_ref, acc_ref):
        k = pl.program_id(2)

        @pl.when(k == 0)
        def _():
            acc_ref[...] = jnp.zeros_like(acc_ref)

        acc_ref[...] += _dot(a_ref[...].astype(BF16), b_ref[...].astype(BF16), dims)

        @pl.when(k == nk - 1)
        def _():
            o_ref[...] = acc_ref[...].astype(out_dtype)

    if by_col_tile:
        out_spec, out_shape = pl.BlockSpec((None, tm, tn), lambda i, j, k: (j, i, 0)), (n // tn, m, tn)
    else:
        out_spec, out_shape = pl.BlockSpec((tm, tn), lambda i, j, k: (i, j)), (m, n)
    return pl.pallas_call(
        body, name=name, grid=(m // tm, n // tn, nk), in_specs=[a_spec, b_spec], out_specs=out_spec,
        out_shape=_sds(out_shape, out_dtype), scratch_shapes=[pltpu.VMEM((tm, tn), F32)],
        compiler_params=_params(3))(a, b)


def _fwd_in(x, head, gmix, win, gq, wuq, gkv, wukv, rc, rs1, rs2):
    L = TM + x.shape[0]

    def body(x_ref, head_ref, gmix_ref, win_ref, gq_ref, wuq_ref, gkv_ref, wukv_ref, c_ref, s1_ref, s2_ref,
             h0_ref, n_ref, zag_ref, u0_ref, cq_ref, ckv_ref, qn_ref, kvn_ref, q_ref, kv_ref, kr_ref, kvt_ref):
        h = jnp.where(pl.program_id(0) == 0, head_ref[...], x_ref[...])
        h0_ref[...] = h
        n = (h * _rms_r(h, D_MODEL) * gmix_ref[...]).astype(BF16)
        n_ref[...] = n
        z = _dot(n, win_ref[...], NN)
        a, gate = z[:, :D_CONV], z[:, D_CONV:D_AG]
        zag_ref[...] = z[:, :D_AG].astype(BF16)
        u0_ref[...] = a * jax.nn.sigmoid(gate)
        cq = z[:, D_AG:D_AG + Q_LORA]
        ckv = z[:, D_AG + Q_LORA:D_AG + Q_LORA + KV_LORA]
        krp = z[:, D_AG + Q_LORA + KV_LORA:]
        cq_ref[...] = cq
        ckv_ref[...] = ckv
        qn = (cq * _rms_r(cq, Q_LORA) * gq_ref[...]).astype(BF16)
        qn_ref[...] = qn
        kvn = (ckv * _rms_r(ckv, KV_LORA) * gkv_ref[...]).astype(BF16)
        kvn_ref[...] = kvn
        c, s1, s2 = c_ref[...], s1_ref[...], s2_ref[...]
        q = _dot(qn, wuq_ref[...], NN)
        q = _rope(q, jnp.tile(c, (1, N_HEADS)), jnp.tile(s1, (1, N_HEADS)), jnp.tile(s2, (1, N_HEADS)))
        q_ref[...] = (q * Q_SCALE).astype(BF16)
        kv = _dot(kvn, wukv_ref[...], NN)
        kv_ref[...] = kv.astype(BF16)
        kvt_ref[...] = kv.T.astype(BF16)
        kr_ref[...] = _rope(krp, c, s1, s2).astype(BF16)

    outs = [(D_MODEL, F32), (D_MODEL, BF16), (D_AG, BF16), (D_CONV, F32), (Q_LORA, F32), (KV_LORA, F32), (Q_LORA, BF16),
            (KV_LORA, BF16), (D_HEADS, BF16), (D_HEADS, BF16), (HB, BF16)]
    return pl.pallas_call(
        body, name="fwd_in", grid=(L // TM,),
        in_specs=[pl.BlockSpec((TM, D_MODEL), lambda i: (jnp.maximum(i - 1, 0), 0)), _full(head.shape),
                  _full(gmix.shape), _full(win.shape), _full(gq.shape), _full(wuq.shape),
                  _full(gkv.shape), _full(wukv.shape), _rows(TM, HB), _rows(TM, HB), _rows(TM, HB)],
        out_specs=[_rows(TM, c) for c, _ in outs] + [pl.BlockSpec((D_HEADS, TM), lambda i: (0, i))],
        out_shape=[_sds((L, c), d) for c, d in outs] + [_sds((D_HEADS, L), BF16)],
        compiler_params=_params(1))(x, head, gmix, win, gq, wuq, gkv, wukv, rc, rs1, rs2)


def _conv_taps(xx, w_ref, halo, tm, flip):
    kw = w_ref.shape[0]
    acc, rolls = None, {}
    for k in range(kw):
        term = w_ref[k:k + 1, :] * _shifted_rows(xx, kw - 1 - k, halo, tm, flip, rolls)
        acc = term if acc is None else acc + term
    return acc


def _shifted_rows(xx, d, halo, tm, flip, rolls):
    a, b = divmod(d, SUBLANES)
    if b not in rolls:
        rolls[b] = xx if b == 0 else pltpu.roll(xx, (xx.shape[0] - b) if flip else b, 0)
    start = SUBLANES * a if flip else halo - SUBLANES * a
    return rolls[b][start:start + tm]


def _ln_silu(u1, lg, lb):
    mu = jnp.mean(u1, -1, keepdims=True)
    xc = u1 - mu
    rs = lax.rsqrt(jnp.mean(xc * xc, -1, keepdims=True) + EPS)
    xh = xc * rs
    u2 = xh * lg + lb
    sg = jax.nn.sigmoid(u2)
    return rs, xh, u2, sg, u2 * sg


def _fwd_conv(u0, cw, cb, lg, lb, og):
    L = u0.shape[0]

    def body(u0_ref, u0p_ref, cw_ref, cb_ref, lg_ref, lb_ref, og_ref, u1_ref, mixa_ref):
        halo = jnp.where(pl.program_id(0) > 0, u0p_ref[...], 0.0)
        xx = jnp.concatenate([halo, u0_ref[...]], 0)
        u1 = _conv_taps(xx, cw_ref, HALO_CONV, TM, False) + cb_ref[...]
        u1_ref[...] = u1
        u = _ln_silu(u1, lg_ref[...], lb_ref[...])[4]
        mixa_ref[...] = (u * _rms_r(u, D_CONV) * og_ref[...]).astype(BF16)

    return pl.pallas_call(
        body, name="fwd_conv", grid=(L // TM,),
        in_specs=[_rows(TM, D_CONV), _prev(HALO_CONV, D_CONV, TM), _full(cw.shape), _full(cb.shape),
                  _full(lg.shape), _full(lb.shape), _full(og.shape)],
        out_specs=[_rows(TM, D_CONV), _rows(TM, D_CONV)],
        out_shape=[_sds((L, D_CONV), F32), _sds((L, D_CONV), BF16)],
        compiler_params=_params(1))(u0, u0, cw, cb, lg, lb, og)


def _visible_t(i, j, t):
    key = j * t + lax.broadcasted_iota(jnp.int32, (t, t), 0)
    query = i * t + lax.broadcasted_iota(jnp.int32, (t, t), 1)
    return (lax.shift_right_logical(key, CHUNK_SHIFT) <= lax.shift_right_logical(query, CHUNK_SHIFT)) & (key >= DEAD)


def _stat_lane(h):
    return (h // HEAD_GROUP) * LANES + h % HEAD_GROUP


def _scatter_stats(cols, t):
    lane = lax.broadcasted_iota(jnp.int32, (t, N_GROUPS * LANES), 1)
    out = jnp.zeros((t, N_GROUPS * LANES), F32)
    for h, col in enumerate(cols):
        out = jnp.where(lane == _stat_lane(h), col, out)
    return out


def _resident(shape, index_map):
    return pl.BlockSpec(shape, index_map, pipeline_mode=pl.Buffered(1))


def _attn_fwd(q, kv, kvt, kr, shards=()):
    L = q.shape[0]
    t = TM
    nq = L // t
    n = len(shards)
    pass_step = (5 * nq) // 6

    def body(q_ref, kv_ref, kvt_ref, kr_ref, *refs):
        gather_refs = refs[:n] + refs[n + 2:2 * n + 2] + refs[2 * n + 6:]
        o_ref, lse_ref = refs[n:n + 2]
        qt_scr, m_scr, l_scr, acc_scr = refs[2 * n + 2:2 * n + 6]
        i = pl.program_id(0)
        if n:
            @pl.when(i == 0)
            def _():
                for cp in _gather_copies(gather_refs[:n], gather_refs[n:2 * n], *gather_refs[2 * n:])[0]:
                    cp().start()

        lane = lax.broadcasted_iota(jnp.int32, (t, HB), 1)
        heads = range(N_HEADS)
        cols = [slice(h * HB, (h + 1) * HB) for h in heads]
        for h in heads:
            qt_scr[cols[h], :] = q_ref[:, cols[h]].astype(F32).T.astype(BF16)
        m_scr[...] = jnp.full_like(m_scr, NEG)
        l_scr[...] = jnp.zeros_like(l_scr)
        acc_scr[...] = jnp.zeros_like(acc_scr)

        def tile(keys, vis, whole=True):
            krj = kr_ref[keys, :]
            kvj = [kv_ref[keys, cols[h]] for h in heads]
            lane_k = lane[:krj.shape[0]]
            s = [_dot(jnp.where(lane_k < QK_NOPE, kvj[h], krj), qt_scr[cols[h], :], NN) for h in heads]
            p, alpha = [], []
            for h in heads:
                sh = s[h] if vis is None else jnp.where(vis, s[h], NEG)
                m_prev = m_scr[h:h + 1, :]
                m_new = jnp.maximum(m_prev, jnp.max(sh, 0, keepdims=True))
                a = jnp.exp2(m_prev - m_new)
                ph = jnp.exp2(sh - m_new)
                l_scr[h:h + 1, :] = a * l_scr[h:h + 1, :] + jnp.sum(ph, 0, keepdims=True)
                m_scr[h:h + 1, :] = m_new
                p.append(ph.astype(BF16))
                alpha.append(a)
            for h in heads:
                pv = _dot(kvt_ref[cols[h], keys], p[h], NN) if whole else _dot(kvj[h], p[h], TN)
                acc_scr[cols[h], :] = alpha[h] * acc_scr[cols[h], :] + pv

        tile(pl.ds(pl.multiple_of(i * t, t), t), _visible_t(i, i, t))

        @pl.when(i > 0)
        def _():
            tile(pl.ds(DEAD, N_META), None, whole=False)

        count = jnp.maximum(i - 1, 0)

        def unmasked_pair(r, carry):
            tile(pl.ds(pl.multiple_of((1 + 2 * r) * t, t), t), None)
            tile(pl.ds(pl.multiple_of((2 + 2 * r) * t, t), t), None)
            return carry

        lax.fori_loop(0, count // 2, unmasked_pair, 0)

        @pl.when(count % 2 == 1)
        def _():
            tile(pl.ds(pl.multiple_of((i - 1) * t, t), t), None)
        lse_ref[...] = jnp.zeros_like(lse_ref)
        for h in heads:
            l = l_scr[h:h + 1, :]
            o_ref[:, cols[h]] = jnp.where(lane >= QK_NOPE, (acc_scr[cols[h], :] / l).T, 0.0).astype(BF16)
            lse_ref[h // HEAD_GROUP, h % HEAD_GROUP:h % HEAD_GROUP + 1, :] = m_scr[h:h + 1, :] + jnp.log2(l)
        if n:
            @pl.when(i == pass_step)
            def _():
                _, arrivals, forwards, _ = _gather_copies(gather_refs[:n], gather_refs[n:2 * n], *gather_refs[2 * n:])
                for landed, onward in zip(arrivals, forwards):
                    landed().wait_recv()
                    onward().start()

            @pl.when(i == nq - 1)
            def _():
                sends, _, forwards, finals = _gather_copies(gather_refs[:n], gather_refs[n:2 * n], *gather_refs[2 * n:])
                for cp in finals:
                    cp().wait_recv()
                for cp in sends + forwards:
                    cp().wait_send()

    any_spec = pl.BlockSpec(memory_space=pl.ANY)
    outs = pl.pallas_call(
        body, name="attn_fwd", grid=(nq,),
        in_specs=[_rows(t, D_HEADS), _resident((L, D_HEADS), lambda i: (0, 0)),
                  _resident((D_HEADS, L), lambda i: (0, 0)), _resident((L, HB), lambda i: (0, 0))] + [any_spec] * n,
        out_specs=[_rows(t, D_HEADS), pl.BlockSpec((N_GROUPS, SUBLANES, t), lambda i: (0, 0, i))] + [any_spec] * n,
        out_shape=[_sds((L, D_HEADS), BF16), _sds((N_GROUPS, SUBLANES, L), F32)] + _gather_out_shapes(shards),
        scratch_shapes=[pltpu.VMEM((D_HEADS, t), BF16), pltpu.VMEM((N_HEADS, t), F32), pltpu.VMEM((N_HEADS, t), F32),
                        pltpu.VMEM((D_HEADS, t), F32)] + (_gather_semaphores(n) if n else []),
        compiler_params=_params(1))(q, kv, kvt, kr, *shards)
    return outs[0], outs[1], _gathered(outs[2:], shards)


def _fwd_mix(h0, mixa, o, ga, wout, gffn):
    L = h0.shape[0]

    def body(h0_ref, mixa_ref, o_ref, ga_ref, wout_ref, gffn_ref, mix_ref, h1_ref, n2_ref):
        of = o_ref[...].astype(F32)
        mixb = (of * _rms_r(of, N_HEADS * V_HEAD) * ga_ref[...]).astype(BF16)
        mix = jnp.concatenate([mixa_ref[...], mixb], 1)
        mix_ref[...] = mix
        mo = jnp.where(_row_ids(TM) >= DEAD, _dot(mix, wout_ref[...], NN), 0.0)
        h1 = h0_ref[...] + mo
        h1_ref[...] = h1
        n2_ref[...] = (h1 * _rms_r(h1, D_MODEL) * gffn_ref[...]).astype(BF16)

    return pl.pallas_call(
        body, name="fwd_mix", grid=(L // TM,),
        in_specs=[_rows(TM, D_MODEL), _rows(TM, D_CONV), _rows(TM, D_HEADS), _full(ga.shape), _full(wout.shape),
                  _full(gffn.shape)],
        out_specs=[_rows(TM, D_MIX), _rows(TM, D_MODEL), _rows(TM, D_MODEL)],
        out_shape=[_sds((L, D_MIX), BF16), _sds((L, D_MODEL), F32), _sds((L, D_MODEL), BF16)],
        compiler_params=_params(1))(h0, mixa, o, ga, wout, gffn)


def _ffn_act_chunk(c, upg_ref, upv_ref, hg, hv, fcw_ref, fcb_ref):
    cs = slice(c * FF_CHUNK, (c + 1) * FF_CHUNK)
    out = []
    for part, (up_ref, halo) in enumerate(((upg_ref, hg), (upv_ref, hv))):
        xx = jnp.concatenate([halo[:, cs], up_ref[:, cs].astype(F32)], 0)
        ws = slice(part * D_FF + c * FF_CHUNK, part * D_FF + (c + 1) * FF_CHUNK)
        y = (fcw_ref[0:1, ws] * pltpu.roll(xx, 2, 0)[HALO_FFN:] + fcw_ref[1:2, ws] * pltpu.roll(xx, 1, 0)[HALO_FFN:]
             + fcw_ref[2:3, ws] * xx[HALO_FFN:] + fcb_ref[:, ws])
        out.append(y)
    return out


def _ffn_in_specs(L):
    return [_rows(TM, D_FF, 0), _rows(TM, D_FF, 1), _prev(HALO_FFN, D_FF, TM, 0), _prev(HALO_FFN, D_FF, TM, 1)]


def _ffn_halos(hg_ref, hv_ref):
    first = pl.program_id(0) == 0
    return (jnp.where(first, 0.0, hg_ref[...].astype(F32)), jnp.where(first, 0.0, hv_ref[...].astype(F32)))


def _fwd_ffn_loss(up0, fcw, fcb, wdown, h1, target, gfin):
    L = h1.shape[0]

    def body(upg_ref, upv_ref, hg_ref, hv_ref, fcw_ref, fcb_ref, wd_ref, h1_ref, t_ref, gf_ref,
             dh2_ref, loss_ref, dgf_ref, up_ref, act_ref):
        i = pl.program_id(0)
        hg, hv = _ffn_halos(hg_ref, hv_ref)
        for c in range(D_FF // FF_CHUNK):
            cs = slice(c * FF_CHUNK, (c + 1) * FF_CHUNK)
            g, val = _ffn_act_chunk(c, upg_ref, upv_ref, hg, hv, fcw_ref, fcb_ref)
            up_ref[:, cs] = g.astype(BF16)
            up_ref[:, D_FF + c * FF_CHUNK:D_FF + (c + 1) * FF_CHUNK] = val.astype(BF16)
            act_ref[:, cs] = (g * jax.nn.sigmoid(g) * val).astype(BF16)
        h2 = h1_ref[...] + _dot(act_ref[...], wd_ref[...], NN)
        r = _rms_r(h2, D_MODEL)
        gf = gf_ref[...]
        err = jnp.where(i > 0, h2 * r * gf - t_ref[...], 0.0)
        dy = err * (1.0 / D_MODEL)
        dh2, dgf = _rms_bwd(dy, h2, r, gf, D_MODEL)
        dh2_ref[...] = dh2

        @pl.when(i == 0)
        def _():
            loss_ref[...] = jnp.zeros_like(loss_ref)
            dgf_ref[...] = jnp.zeros_like(dgf_ref)

        loss_ref[...] += jnp.sum(err * err) * (0.5 / D_MODEL)
        dgf_ref[...] += dgf

    return pl.pallas_call(
        body, name="fwd_ffn_loss", grid=(L // TM,),
        in_specs=_ffn_in_specs(L) + [_full(fcw.shape), _full(fcb.shape), _full(wdown.shape), _rows(TM, D_MODEL),
                                     pl.BlockSpec((TM, D_MODEL), lambda i: (jnp.maximum(i - 1, 0), 0)),
                                     _full(gfin.shape)],
        out_specs=[_rows(TM, D_MODEL), _full((1, LANES)), _full((1, D_MODEL)), _rows(TM, 2 * D_FF), _rows(TM, D_FF)],
        out_shape=[_sds((L, D_MODEL), F32), _sds((1, LANES), F32), _sds((1, D_MODEL), F32),
                   _sds((L, 2 * D_FF), BF16), _sds((L, D_FF), BF16)],
        compiler_params=_params(1))(up0, up0, up0, up0, fcw, fcb, wdown, h1, target, gfin)


def _bwd_ffn_act(dh2, up, wdown):
    L = dh2.shape[0]

    def body(dh2_ref, upg_ref, upv_ref, wd_ref, dup_ref, dfcb_ref):
        da = _dot(dh2_ref[...].astype(BF16), wd_ref[...], NT)

        @pl.when(pl.program_id(0) == 0)
        def _():
            dfcb_ref[...] = jnp.zeros_like(dfcb_ref)

        for c in range(D_FF // FF_CHUNK):
            cs = slice(c * FF_CHUNK, (c + 1) * FF_CHUNK)
            vs = slice(D_FF + c * FF_CHUNK, D_FF + (c + 1) * FF_CHUNK)
            g, val = upg_ref[:, cs].astype(F32), upv_ref[:, cs].astype(F32)
            sg = jax.nn.sigmoid(g)
            si = g * sg
            dac = da[:, cs]
            dg = dac * val * (sg * (1.0 + g * (1.0 - sg)))
            dv = dac * si
            dup_ref[:, cs] = dg.astype(BF16)
            dup_ref[:, vs] = dv.astype(BF16)
            dfcb_ref[:, cs] += jnp.sum(dg, 0, keepdims=True)
            dfcb_ref[:, vs] += jnp.sum(dv, 0, keepdims=True)

    return pl.pallas_call(
        body, name="bwd_ffn_act", grid=(L // TM,),
        in_specs=[_rows(TM, D_MODEL), _rows(TM, D_FF, 0), _rows(TM, D_FF, 1), _full(wdown.shape)],
        out_specs=[_rows(TM, 2 * D_FF), _full((1, 2 * D_FF))],
        out_shape=[_sds((L, 2 * D_FF), BF16), _sds((1, 2 * D_FF), F32)],
        compiler_params=_params(1))(dh2, up, up, wdown)


def _bwd_ffn_conv_up(dup, up0, fcw, wup, tl):
    L, C = dup.shape
    tc = FF_TILE
    nt = L // tl
    nhb = L // HALO_FFN
    chunks = [(c0, min(FF_MXU_CHUNK, tc - c0)) for c0 in range(0, tc, FF_MXU_CHUNK)]

    def body(dy_ref, dyn_ref, x_ref, xp_ref, w_ref, wup_ref, dn2_ref, dx_ref, dw_ref, acc_ref):
        i, k = pl.program_id(0), pl.program_id(1)

        @pl.when(k == 0)
        def _():
            acc_ref[...] = jnp.zeros_like(acc_ref)

        last, first = i == nt - 1, i == 0
        for c0, cw in chunks:
            cs = slice(c0, c0 + cw)
            yy = jnp.concatenate([dy_ref[:, cs].astype(F32), jnp.where(last, 0.0, dyn_ref[:, cs].astype(F32))], 0)
            w = w_ref[:, cs]
            dx = (w[0:1] * pltpu.roll(yy, tl + HALO_FFN - 2, 0)[:tl] + w[1:2] * pltpu.roll(yy, tl + HALO_FFN - 1, 0)[:tl]
                  + w[2:3] * yy[:tl]).astype(BF16)
            dx_ref[:, cs] = dx
            acc_ref[...] += _dot(dx, wup_ref[:, cs], NT)
            xx = jnp.concatenate([jnp.where(first, 0.0, xp_ref[:, cs].astype(F32)), x_ref[:, cs].astype(F32)], 0)
            dy = yy[:tl]
            dw_ref[0:1, cs] = jnp.sum(dy * pltpu.roll(xx, 2, 0)[HALO_FFN:], 0, keepdims=True)
            dw_ref[1:2, cs] = jnp.sum(dy * pltpu.roll(xx, 1, 0)[HALO_FFN:], 0, keepdims=True)
            dw_ref[2:3, cs] = jnp.sum(dy * xx[HALO_FFN:], 0, keepdims=True)

        @pl.when(k == C // tc - 1)
        def _():
            dn2_ref[...] = acc_ref[...]

    tile = pl.BlockSpec((tl, tc), lambda i, k: (i, k))
    per = tl // HALO_FFN
    return pl.pallas_call(
        body, name="bwd_ffn_conv_up", grid=(nt, C // tc),
        in_specs=[tile, pl.BlockSpec((HALO_FFN, tc), lambda i, k: (jnp.minimum((i + 1) * per, nhb - 1), k)),
                  tile, pl.BlockSpec((HALO_FFN, tc), lambda i, k: (jnp.maximum(i * per - 1, 0), k)),
                  pl.BlockSpec((FFN_CONV_WIDTH, tc), lambda i, k: (0, k)),
                  pl.BlockSpec((None, D_MODEL, tc), lambda i, k: (k, 0, 0))],
        out_specs=[pl.BlockSpec((tl, D_MODEL), lambda i, k: (i, 0)), tile,
                   pl.BlockSpec((None, FFN_CONV_WIDTH, tc), lambda i, k: (i, 0, k))],
        out_shape=[_sds((L, D_MODEL), F32), _sds((L, C), BF16), _sds((nt, FFN_CONV_WIDTH, C), F32)],
        scratch_shapes=[pltpu.VMEM((tl, D_MODEL), F32)],
        compiler_params=_params(2))(dup, dup, up0, up0, fcw, wup)


def _carrying(core, n_in, n_out, n, copies_fn, steps):
    def body(*refs):
        exchange = (refs[n_in:n_in + n], refs[n_in + n + n_out:n_in + 2 * n + n_out]) + refs[n_in + 2 * n + n_out:]
        if n:
            @pl.when(pl.program_id(0) == 0)
            def _():
                for cp in copies_fn(*exchange):
                    cp().start()

        core(*refs[:n_in], *refs[n_in + n:n_in + n + n_out])
        if n:
            @pl.when(pl.program_id(0) == steps - 1)
            def _():
                for cp in copies_fn(*exchange):
                    cp().wait()

    return body


def _bwd_mix(dh2, dn2, h1, gffn, wout, o, ga, u1, lg, lb, og, parts=()):
    L = h1.shape[0]

    def body(dh2_ref, dn2_ref, h1_ref, gffn_ref, wout_ref, o_ref, ga_ref, u1_ref, lg_ref, lb_ref, og_ref,
             dh1_ref, dh1m_ref, do_ref, delta_ref, du1_ref, dgffn_ref, dga_ref, dog_ref, dlg_ref, dlb_ref):
        h1 = h1_ref[...]
        dn2x, dgffn = _rms_bwd(dn2_ref[...], h1, _rms_r(h1, D_MODEL), gffn_ref[...], D_MODEL)
        dh1 = dh2_ref[...] + dn2x
        dh1_ref[...] = dh1
        dh1m = jnp.where(_row_ids(TM) >= DEAD, dh1, 0.0).astype(BF16)
        dh1m_ref[...] = dh1m
        dmix = _dot(dh1m, wout_ref[...], NT)
        dma, dmb = dmix[:, :D_CONV], dmix[:, D_CONV:]
        of = o_ref[...].astype(F32)
        do, dga = _rms_bwd(dmb, of, _rms_r(of, N_HEADS * V_HEAD), ga_ref[...], N_HEADS * V_HEAD)
        do_ref[...] = do.astype(BF16)
        prod = do * of
        by_lane = _scatter_stats([jnp.sum(prod[:, h * HB:(h + 1) * HB], -1, keepdims=True) for h in range(N_HEADS)], TM)
        by_row = by_lane.T
        for grp in range(N_GROUPS):
            delta_ref[grp] = by_row[grp * LANES:grp * LANES + SUBLANES, :]
        lg = lg_ref[...]
        rs, xh, u2, sg, u = _ln_silu(u1_ref[...], lg, lb_ref[...])
        du, dog = _rms_bwd(dma, u, _rms_r(u, D_CONV), og_ref[...], D_CONV)
        du2 = du * (sg * (1.0 + u2 * (1.0 - sg)))
        dxh = du2 * lg
        du1_ref[...] = rs * (dxh - jnp.mean(dxh, -1, keepdims=True) - xh * jnp.mean(dxh * xh, -1, keepdims=True))

        @pl.when(pl.program_id(0) == 0)
        def _():
            for ref in (dgffn_ref, dga_ref, dog_ref, dlg_ref, dlb_ref):
                ref[...] = jnp.zeros_like(ref)

        dgffn_ref[...] += dgffn
        dga_ref[...] += dga
        dog_ref[...] += dog
        dlg_ref[...] += jnp.sum(du2 * xh, 0, keepdims=True)
        dlb_ref[...] += jnp.sum(du2, 0, keepdims=True)

    n = len(parts)
    any_spec = pl.BlockSpec(memory_space=pl.ANY)
    outs = pl.pallas_call(
        _carrying(body, 11, 10, n, _sibling_exchange_copies, L // TM), name="bwd_mix", grid=(L // TM,),
        in_specs=[_rows(TM, D_MODEL), _rows(TM, D_MODEL), _rows(TM, D_MODEL), _full(gffn.shape), _full(wout.shape),
                  _rows(TM, D_HEADS), _full(ga.shape), _rows(TM, D_CONV), _full(lg.shape), _full(lb.shape),
                  _full(og.shape)] + [any_spec] * n,
        out_specs=[_rows(TM, D_MODEL), _rows(TM, D_MODEL), _rows(TM, D_HEADS),
                   pl.BlockSpec((N_GROUPS, SUBLANES, TM), lambda i: (0, 0, i)),
                   _rows(TM, D_CONV), _full((1, D_MODEL)), _full((1, D_HEADS)), _full((1, D_CONV)),
                   _full((1, D_CONV)), _full((1, D_CONV))] + [any_spec] * n,
        out_shape=[_sds((L, D_MODEL), F32), _sds((L, D_MODEL), BF16), _sds((L, D_HEADS), BF16),
                   _sds((N_GROUPS, SUBLANES, L), F32),
                   _sds((L, D_CONV), F32), _sds((1, D_MODEL), F32), _sds((1, D_HEADS), F32), _sds((1, D_CONV), F32),
                   _sds((1, D_CONV), F32), _sds((1, D_CONV), F32)] + _sibling_exchange_shapes(parts),
        scratch_shapes=_sibling_exchange_semaphores(n) if n else [],
        compiler_params=_params(1))(dh2, dn2, h1, gffn, wout, o, ga, u1, lg, lb, og, *parts)
    return outs[:10], list(outs[10:])


def _attn_bwd(q, kv, kr, do, lse, delta, parts=()):
    L = q.shape[0]
    t = TM
    nt = L // t
    gw = HEAD_GROUP * HB
    n = len(parts)

    def body(q_ref, kv_ref, kr_ref, do_ref, lse_ref, delta_ref, *refs):
        dq_ref, dkv_ref, dkr_ref = refs[n:n + 3]
        dqt_acc, dk_acc, dv_acc, kkt_scr = refs[2 * n + 3:2 * n + 7]
        exchange_refs = (refs[:n], refs[n + 3:2 * n + 3]) + refs[2 * n + 7:]
        g, j = pl.program_id(0), pl.program_id(1)
        if n:
            @pl.when((g == 0) & (j == 0))
            def _():
                for cp in _chip_copies(*exchange_refs)[0]:
                    cp().start()

        lane = lax.broadcasted_iota(jnp.int32, (t, HB), 1)

        @pl.when(j == 0)
        def _():
            dqt_acc[...] = jnp.zeros_like(dqt_acc)

        @pl.when((j == 0) & (g == 0))
        def _():
            dkr_ref[...] = jnp.zeros_like(dkr_ref)

        dk_acc[...] = jnp.zeros_like(dk_acc)
        dv_acc[...] = jnp.zeros_like(dv_acc)
        krj = kr_ref[...]
        heads = range(HEAD_GROUP)
        cols = [slice(h * HB, (h + 1) * HB) for h in heads]
        for hc in cols:
            kkt_scr[hc, :] = jnp.where(lane < QK_NOPE, kv_ref[:, hc], krj).astype(F32).T.astype(BF16)

        def tile(i, vis, whole=True):
            qs = pl.ds(pl.multiple_of(i * t, t), t)
            keys = slice(None) if whole else slice(DEAD, t)
            kvj = [kv_ref[keys, hc] for hc in cols]
            lane_k = lane[:kvj[0].shape[0]]
            kk = [jnp.where(lane_k < QK_NOPE, kvj[h], krj[keys]) for h in heads]
            qi = [q_ref[qs, hc] for hc in cols]
            doi = [do_ref[qs, hc] for hc in cols]
            s = [_dot(kk[h], qi[h].astype(F32).T.astype(BF16), NN) for h in heads]
            dp = [_dot(kvj[h], doi[h].astype(F32).T.astype(BF16), NN) for h in heads]
            p = []
            for h in heads:
                sh = s[h] if vis is None else jnp.where(vis, s[h], NEG)
                p.append(jnp.exp2(sh - lse_ref[h:h + 1, qs]))
            for h in heads:
                dv_acc[keys, cols[h]] += _dot(p[h].astype(BF16), doi[h], NN)
            ds = [(p[h] * (dp[h] - delta_ref[h:h + 1, qs]) * LN2).astype(BF16) for h in heads]
            for h in heads:
                dk_acc[keys, cols[h]] += _dot(ds[h], qi[h], NN)
            for h in heads:
                dqt = _dot(kkt_scr[cols[h], :], ds[h], NN) if whole else _dot(kk[h], ds[h], TN)
                dqt_acc[cols[h], qs] += dqt

        @pl.when(j == 0)
        def _():
            tile(0, _visible_t(0, 0, t)[DEAD:], whole=False)

            def meta_keys(i, carry):
                tile(i, None, whole=False)
                return carry

            lax.fori_loop(1, nt, meta_keys, 0)

        @pl.when(j > 0)
        def _():
            tile(j, _visible_t(j, j, t))

            count = nt - 1 - j

            def unmasked_pair(r, carry):
                tile(j + 1 + 2 * r, None)
                tile(j + 2 + 2 * r, None)
                return carry

            lax.fori_loop(0, count // 2, unmasked_pair, 0)

            @pl.when(count % 2 == 1)
            def _():
                tile(nt - 1, None)

        dkr = jnp.zeros((t, HB), F32)
        for h in range(HEAD_GROUP):
            hc = slice(h * HB, (h + 1) * HB)
            dk = dk_acc[:, hc]
            dkv_ref[:, hc] = jnp.where(lane < QK_NOPE, dk, dv_acc[:, hc]).astype(BF16)
            dkr = dkr + jnp.where(lane >= QK_NOPE, dk, 0.0)
        dkr_ref[pl.ds(pl.multiple_of(j * t, t), t), :] += dkr

        @pl.when(j == nt - 1)
        def _():
            def untranspose(i, carry):
                qs = pl.ds(pl.multiple_of(i * t, t), t)
                dq_ref[qs, :] = (dqt_acc[:, qs].T * Q_SCALE).astype(BF16)
                return carry

            lax.fori_loop(0, nt, untranspose, 0)

        if n:
            @pl.when((g == N_GROUPS - 1) & (j == nt - 1))
            def _():
                sends, arrivals = _chip_copies(*exchange_refs)
                for cp in arrivals:
                    cp().wait_recv()
                for cp in sends:
                    cp().wait_send()

    group = lambda g, j: (0, g)
    stats = _resident((None, SUBLANES, L), lambda g, j: (g, 0, 0))
    any_spec = pl.BlockSpec(memory_space=pl.ANY)
    outs = pl.pallas_call(
        body, name="attn_bwd", grid=(N_GROUPS, nt),
        in_specs=[_resident((L, gw), group), pl.BlockSpec((t, gw), lambda g, j: (j, g)),
                  pl.BlockSpec((t, HB), lambda g, j: (j, 0)), _resident((L, gw), group), stats, stats]
        + [any_spec] * n,
        out_specs=[pl.BlockSpec((L, gw), group), pl.BlockSpec((t, gw), lambda g, j: (j, g)),
                   pl.BlockSpec((L, HB), lambda g, j: (0, 0))] + [any_spec] * n,
        out_shape=[_sds((L, D_HEADS), BF16), _sds((L, D_HEADS), BF16), _sds((L, HB), F32)]
        + [_sds(p.shape, p.dtype) for p in parts],
        scratch_shapes=[pltpu.VMEM((gw, L), F32), pltpu.VMEM((t, gw), F32), pltpu.VMEM((t, gw), F32),
                        pltpu.VMEM((gw, t), BF16)] + (_chip_semaphores(n) if n else []),
        compiler_params=_params(2))(q, kv, kr, do, lse, delta, *parts)
    return outs[0], outs[1], outs[2], list(outs[3:])


def _bwd_conv(du1, u0, cw, zag, halves=()):
    L = du1.shape[0]
    nt = L // TM

    def body(dy_ref, dyn_ref, x_ref, cw_ref, zag_ref, dzag_ref, dcw_ref, dcb_ref):
        i = pl.program_id(0)
        dy = dy_ref[...]
        yy = jnp.concatenate([dy, jnp.where(i < nt - 1, dyn_ref[...], 0.0)], 0)
        x = x_ref[...]

        @pl.when(i == 0)
        def _():
            dcw_ref[...] = jnp.zeros_like(dcw_ref)
            dcb_ref[...] = jnp.zeros_like(dcb_ref)

        du0, rolls = None, {}
        for k in range(CONV_WIDTH):
            ahead = _shifted_rows(yy, CONV_WIDTH - 1 - k, HALO_CONV, TM, True, rolls)
            term = cw_ref[k:k + 1, :] * ahead
            du0 = term if du0 is None else du0 + term
            dcw_ref[k:k + 1, :] += jnp.sum(ahead * x, 0, keepdims=True)
        dcb_ref[...] += jnp.sum(dy, 0, keepdims=True)
        zag = zag_ref[...].astype(F32)
        a, sg = zag[:, :D_CONV], jax.nn.sigmoid(zag[:, D_CONV:])
        dzag_ref[...] = jnp.concatenate([du0 * sg, du0 * a * sg * (1.0 - sg)], 1).astype(BF16)

    n = len(halves)
    any_spec = pl.BlockSpec(memory_space=pl.ANY)
    outs = pl.pallas_call(
        _carrying(body, 5, 3, n, _sibling_gather_copies, nt), name="bwd_conv", grid=(nt,),
        in_specs=[_rows(TM, D_CONV), _next(HALO_CONV, D_CONV, TM, L // HALO_CONV), _rows(TM, D_CONV),
                  _full(cw.shape), _rows(TM, D_AG)] + [any_spec] * n,
        out_specs=[_rows(TM, D_AG), _full(cw.shape), _full((1, D_CONV))] + [any_spec] * n,
        out_shape=[_sds((L, D_AG), BF16), _sds(cw.shape, F32), _sds((1, D_CONV), F32)]
        + [_sds(p.shape, p.dtype) for p in halves],
        input_output_aliases={5 + a: 3 + a for a in range(n)},
        scratch_shapes=_sibling_gather_semaphores(n) if n else [],
        compiler_params=_params(1))(du1, du1, u0, cw, zag, *halves)
    return outs[:3], list(outs[3:])


def _bwd_in(dzag, dq, dkv, dkr, cq, ckv, gq, gkv, wuq, wukv, win, rc, rs1, rs2, h0, gmix, dh1):
    L = h0.shape[0]

    def body(dzag_ref, dq_ref, dkv_ref, dkr_ref, cq_ref, ckv_ref, gq_ref, gkv_ref, wuq_ref, wukv_ref, win_ref,
             c_ref, s1_ref, s2_ref, h0_ref, gmix_ref, dh1_ref,
             dz_ref, dqr_ref, gx_ref, dfirst_ref, dgq_ref, dgkv_ref, dgmix_ref):
        i = pl.program_id(0)
        c, s1, s2 = c_ref[...], s1_ref[...], s2_ref[...]
        dqr = _rope_bwd(dq_ref[...].astype(F32), jnp.tile(c, (1, N_HEADS)), jnp.tile(s1, (1, N_HEADS)),
                        jnp.tile(s2, (1, N_HEADS))).astype(BF16)
        dqr_ref[...] = dqr
        cq, ckv = cq_ref[...], ckv_ref[...]
        dcq, dgq = _rms_bwd(_dot(dqr, wuq_ref[...], NT), cq, _rms_r(cq, Q_LORA), gq_ref[...], Q_LORA)
        dckv, dgkv = _rms_bwd(_dot(dkv_ref[...], wukv_ref[...], NT), ckv, _rms_r(ckv, KV_LORA), gkv_ref[...], KV_LORA)
        dkrp = _rope_bwd(dkr_ref[...], c, s1, s2)
        dz = jnp.concatenate([dzag_ref[...], dcq.astype(BF16), dckv.astype(BF16), dkrp.astype(BF16)], 1)
        dz_ref[...] = dz
        h0 = h0_ref[...]
        dnx, dgmix = _rms_bwd(_dot(dz, win_ref[...], NT), h0, _rms_r(h0, D_MODEL), gmix_ref[...], D_MODEL)
        dh0 = dh1_ref[...] + dnx

        @pl.when(i == 0)
        def _():
            dfirst_ref[...] = dh0
            for ref in (dgq_ref, dgkv_ref, dgmix_ref):
                ref[...] = jnp.zeros_like(ref)

        @pl.when(i > 0)
        def _():
            gx_ref[...] = dh0

        dgq_ref[...] += dgq
        dgkv_ref[...] += dgkv
        dgmix_ref[...] += dgmix

    return pl.pallas_call(
        body, name="bwd_in", grid=(L // TM,),
        in_specs=[_rows(TM, D_AG), _rows(TM, D_HEADS), _rows(TM, D_HEADS), _rows(TM, HB), _rows(TM, Q_LORA),
                  _rows(TM, KV_LORA), _full(gq.shape), _full(gkv.shape), _full(wuq.shape), _full(wukv.shape),
                  _full(win.shape), _rows(TM, HB), _rows(TM, HB), _rows(TM, HB), _rows(TM, D_MODEL),
                  _full(gmix.shape), _rows(TM, D_MODEL)],
        out_specs=[_rows(TM, D_ZP), _rows(TM, D_HEADS),
                   pl.BlockSpec((TM, D_MODEL), lambda i: (jnp.maximum(i - 1, 0), 0)), _full((TM, D_MODEL)),
                   _full((1, Q_LORA)), _full((1, KV_LORA)), _full((1, D_MODEL))],
        out_shape=[_sds((L, D_ZP), BF16), _sds((L, D_HEADS), BF16), _sds((L - TM, D_MODEL), F32),
                   _sds((TM, D_MODEL), F32), _sds((1, Q_LORA), F32), _sds((1, KV_LORA), F32), _sds((1, D_MODEL), F32)],
        compiler_params=_params(1))(dzag, dq, dkv, dkr, cq, ckv, gq, gkv, wuq, wukv, win, rc, rs1, rs2, h0, gmix, dh1)


def _mesh_pos():
    return lax.axis_index("x"), lax.axis_index("y"), lax.axis_index("c")


def _remote_copy(src, dst, send_sem, recv_sem, to):
    return functools.partial(pltpu.make_async_remote_copy, src, dst, send_sem, recv_sem, device_id=to,
                             device_id_type=MESH)


def _all_gather(shards):
    n = len(shards)

    def body(*refs):
        sends, arrivals, forwards, finals = _gather_copies(refs[:n], refs[n:2 * n], *refs[2 * n:])
        for cp in sends:
            cp().start()
        for landed, onward in zip(arrivals, forwards):
            landed().wait_recv()
            onward().start()
        for cp in finals:
            cp().wait_recv()
        for cp in sends + forwards:
            cp().wait_send()

    any_spec = pl.BlockSpec(memory_space=pl.ANY)
    outs = pl.pallas_call(
        body, name="all_gather_weights", in_specs=[any_spec] * n, out_specs=[any_spec] * n,
        out_shape=_gather_out_shapes(shards), scratch_shapes=_gather_semaphores(n))(*shards)
    return _gathered(outs, shards)


def _gather_out_shapes(shards):
    return [_sds((2 * N_CHIPS, s.shape[0] // 2) + s.shape[1:], s.dtype) for s in shards]


def _gather_semaphores(n):
    return [pltpu.SemaphoreType.DMA((n, 8)), pltpu.SemaphoreType.DMA((n, 8))]


def _gathered(outs, shards):
    return [o.reshape((N_CHIPS, s.shape[0]) + s.shape[1:]) for o, s in zip(outs, shards)]


def _gather_copies(ins, outs, send_sems, recv_sems):
    x, y, c = _mesh_pos()
    chips = [(1 - x, y), (x, 1 - y), (1 - x, 1 - y)]
    sends, arrivals, forwards, finals = [], [], [], []

    def copy(src, dst, a, k, to):
        return _remote_copy(src, dst, send_sems.at[a, k], recv_sems.at[a, k], to)

    for a, (src, out) in enumerate(zip(ins, outs)):
        m = out.shape[1]
        mine = src.at[pl.ds(pl.multiple_of(c * m, 16), m)]
        for hf in range(2):
            own = out.at[4 * x + 2 * y + hf]
            sends.append(copy(src.at[pl.ds(hf * m, m)], own, a, 6 + hf, (x, y, 1 - c)))
            finals.append(copy(own, own, a, 6 + hf, (x, y, 1 - c)))
        for k, chip in enumerate(chips):
            slot = 4 * chip[0] + 2 * chip[1]
            sends.append(copy(mine, out.at[4 * x + 2 * y + c], a, k, (*chip, c)))
            arrivals.append(copy(out.at[slot + c], out.at[slot + c], a, k, (*chip, c)))
            forwards.append(copy(out.at[slot + c], out.at[slot + c], a, 3 + k, (x, y, 1 - c)))
            finals.append(copy(out.at[slot + 1 - c], out.at[slot + 1 - c], a, 3 + k, (x, y, 1 - c)))
    return sends, arrivals, forwards, finals


def _sibling_exchange(parts, name):
    n = len(parts)

    def body(*refs):
        copies = _sibling_exchange_copies(refs[:n], refs[n:2 * n], *refs[2 * n:])
        for cp in copies:
            cp().start()
        for cp in copies:
            cp().wait()

    any_spec = pl.BlockSpec(memory_space=pl.ANY)
    return pl.pallas_call(
        body, name=name, in_specs=[any_spec] * n, out_specs=[any_spec] * n,
        out_shape=_sibling_exchange_shapes(parts), scratch_shapes=_sibling_exchange_semaphores(n))(*parts)


def _sibling_exchange_shapes(parts):
    return [_sds((N_CHIPS, p.shape[1] // 2, p.shape[2]), p.dtype) for p in parts]


def _sibling_exchange_semaphores(n):
    return [pltpu.SemaphoreType.DMA((n, N_CHIPS)), pltpu.SemaphoreType.DMA((n, N_CHIPS))]


def _sibling_exchange_copies(ins, theirs, send_sems, recv_sems):
    x, y, c = _mesh_pos()
    copies = []
    for a, (src, dst) in enumerate(zip(ins, theirs)):
        h = dst.shape[1]
        rows = pl.ds(pl.multiple_of((1 - c) * h, 16), h)
        copies += [_remote_copy(src.at[q, rows], dst.at[q], send_sems.at[a, q], recv_sems.at[a, q], (x, y, 1 - c))
                   for q in range(N_CHIPS)]
    return copies


def _chip_semaphores(n):
    return [pltpu.SemaphoreType.DMA((n, 3)), pltpu.SemaphoreType.DMA((n, 3))]


def _chip_copies(ins, outs, send_sems, recv_sems):
    x, y, c = _mesh_pos()
    me = 2 * x + y
    sends, arrivals = [], []
    for a, (src, out) in enumerate(zip(ins, outs)):
        for k, chip in enumerate([(1 - x, y), (x, 1 - y), (1 - x, 1 - y)]):
            slot = 2 * chip[0] + chip[1]
            sems = (send_sems.at[a, k], recv_sems.at[a, k], (*chip, c))
            sends.append(_remote_copy(src.at[slot], out.at[me], *sems))
            arrivals.append(_remote_copy(out.at[slot], out.at[slot], *sems))
    return sends, arrivals


def _sibling_gather(parts, name):
    n = len(parts)

    def body(*refs):
        copies = _sibling_gather_copies(refs[:n], refs[n:2 * n], *refs[2 * n:])
        for cp in copies:
            cp().start()
        for cp in copies:
            cp().wait()

    any_spec = pl.BlockSpec(memory_space=pl.ANY)
    return pl.pallas_call(
        body, name=name, in_specs=[any_spec] * n, out_specs=[any_spec] * n,
        out_shape=[_sds(p.shape, p.dtype) for p in parts], input_output_aliases={a: a for a in range(n)},
        scratch_shapes=_sibling_gather_semaphores(n))(*parts)


def _sibling_gather_semaphores(n):
    return [pltpu.SemaphoreType.DMA((n,)), pltpu.SemaphoreType.DMA((n,))]


def _sibling_gather_copies(ins, outs, send_sems, recv_sems):
    x, y, c = _mesh_pos()
    return [_remote_copy(src.at[c], dst.at[c], send_sems.at[a], recv_sems.at[a], (x, y, 1 - c))
            for a, (src, dst) in enumerate(zip(ins, outs))]


def _row_tile(rows, row_bytes, align, budget=1 << 20):
    best = None
    for t in range(align, rows + 1, align):
        if rows % t == 0 and t * row_bytes <= budget:
            best = t
    return best or rows


def _scalar(v):
    return jnp.reshape(v, (1,)).astype(jnp.int32)


def _add_pair(part, theirs, c, name):
    _, h, cols = theirs.shape
    tr = _row_tile(h, cols * 4, 16, budget=1 << 21)
    nb = h // tr

    def body(c_ref, a_ref, b_ref, o_ref):
        o_ref[...] = (a_ref[...].astype(F32) + b_ref[...].astype(F32)).astype(o_ref.dtype)

    half = pl.BlockSpec((None, tr, cols), lambda q, i, c_ref: (q, i, 0))
    grid_spec = pltpu.PrefetchScalarGridSpec(
        num_scalar_prefetch=1, grid=(N_CHIPS, nb),
        in_specs=[pl.BlockSpec((None, tr, cols), lambda q, i, c_ref: (q, c_ref[0] * nb + i, 0)), half],
        out_specs=half)
    return pl.pallas_call(body, name=name, grid_spec=grid_spec, out_shape=_sds(theirs.shape, part.dtype),
                          compiler_params=_params(2))(_scalar(c), part, theirs)


def _add_chips(got, own, me, c, name):
    _, h, cols = got.shape
    tr = _row_tile(h, cols * 4 * N_CHIPS, 16, budget=1 << 22)

    def body(pos_ref, got_ref, own_ref, o_ref):
        acc = None
        for q in range(N_CHIPS):
            term = jnp.where(pos_ref[0] == q, own_ref[q], got_ref[q]).astype(F32)
            acc = term if acc is None else acc + term
        o_ref[...] = acc

    by_chip = pl.BlockSpec((N_CHIPS, tr, cols), lambda i, pos_ref: (0, i, 0))
    grid_spec = pltpu.PrefetchScalarGridSpec(
        num_scalar_prefetch=1, grid=(h // tr,), in_specs=[by_chip, by_chip],
        out_specs=pl.BlockSpec((None, tr, cols), lambda i, pos_ref: (pos_ref[1], i, 0)))
    return pl.pallas_call(body, name=name, grid_spec=grid_spec, out_shape=_sds((2, h, cols), F32),
                          compiler_params=_params(1))(jnp.stack([me, c]).astype(jnp.int32), got, own)


def _add_pairs(parts, theirs, tag):
    c = lax.axis_index("c")
    return [_add_pair(p, t, c, f"grad_add_pair_{tag}_{a}") for a, (p, t) in enumerate(zip(parts, theirs))]


def _add_all_chips(pair, got, tag):
    x, y, c = _mesh_pos()
    return [_add_chips(g, p, 2 * x + y, c, f"grad_add_chips_{tag}_{a}") for a, (g, p) in enumerate(zip(got, pair))]


def _totals(both):
    return [b.reshape(-1, b.shape[-1]) for b in both]


def _reduce_begin(parts, tag):
    return _add_pairs(parts, _sibling_exchange(parts, f"grad_sibling_exchange_{tag}"), tag)


def _reduce_end(pair, got, tag):
    return _totals(_sibling_gather(_add_all_chips(pair, got, tag), f"grad_sibling_gather_{tag}"))


def _adamw_math(w, g, m, v):
    m = ADAM_B1 * m + (1.0 - ADAM_B1) * g
    v = ADAM_B2 * v + (1.0 - ADAM_B2) * (g * g)
    m_hat = m / (1.0 - ADAM_B1 ** ADAM_STEP)
    v_hat = v / (1.0 - ADAM_B2 ** ADAM_STEP)
    return -ADAM_LR * (m_hat / (jnp.sqrt(v_hat) + ADAM_EPS) + ADAM_WD * w), m, v


def _adamw_big(w, g, m, v, name, parts=()):
    r, c = w.shape
    tr = _row_tile(r, c * 4, 8, budget=1 << 19)
    if r // tr > ADAMW_MAX_STEPS:
        tr = r
    n = len(parts)
    steps = r // tr

    def body(w_ref, g_ref, m_ref, v_ref, *refs):
        go_ref, d_ref, mo_ref, vo_ref = refs[n:n + 4]
        exchange_refs = (refs[:n], refs[n + 4:2 * n + 4]) + refs[2 * n + 4:]
        if n:
            @pl.when(pl.program_id(0) == 0)
            def _():
                for cp in _chip_copies(*exchange_refs)[0]:
                    cp().start()

        g = g_ref[...]
        go_ref[...] = g
        d_ref[...], mo_ref[...], vo_ref[...] = _adamw_math(w_ref[...], g, m_ref[...], v_ref[...])
        if n:
            @pl.when(pl.program_id(0) == steps - 1)
            def _():
                sends, arrivals = _chip_copies(*exchange_refs)
                for cp in arrivals:
                    cp().wait_recv()
                for cp in sends:
                    cp().wait_send()

    any_spec = pl.BlockSpec(memory_space=pl.ANY)
    outs = pl.pallas_call(
        body, name=name, grid=(steps,), in_specs=[_rows(tr, c)] * 4 + [any_spec] * n,
        out_specs=[_rows(tr, c)] * 4 + [any_spec] * n,
        out_shape=[_sds((r, c), F32)] * 4 + [_sds(p.shape, p.dtype) for p in parts],
        scratch_shapes=_chip_semaphores(n) if n else [], compiler_params=_params(1))(w, g, m, v, *parts)
    return outs[:4], list(outs[4:])


def _adamw_small(ws, gs, ms, vs):
    n = len(ws)

    def body(*refs):
        for a in range(n):
            w_ref, g_ref, m_ref, v_ref = (refs[k * n + a] for k in range(4))
            d, m, v = _adamw_math(w_ref[...], g_ref[...], m_ref[...], v_ref[...])
            refs[4 * n + a][...] = d
            refs[5 * n + a][...] = m
            refs[6 * n + a][...] = v

    vm = pl.BlockSpec(memory_space=pltpu.VMEM)
    outs = pl.pallas_call(
        body, name="adamw_small", in_specs=[vm] * (4 * n), out_specs=[vm] * (3 * n),
        out_shape=[_sds(w.shape, F32) for w in ws] * 3)(*ws, *gs, *ms, *vs)
    return outs[:n], outs[n:2 * n], outs[2 * n:]


BIG = ("w_in", "w_uq", "w_ukv", "w_out", "w_ffn_up", "w_ffn_down")
SMALL_SHARDED = ("conv_w", "ffn_conv_w", "meta_tokens")
REPLICATED = ("mix_norm_g", "q_norm_g", "kv_norm_g", "conv_b", "conv_ln_g", "conv_ln_b", "conv_out_g", "attn_out_g",
              "ffn_norm_g", "ffn_conv_b", "final_norm_g")
WEIGHTS = ("meta_tokens", "mix_norm_g", "w_in", "q_norm_g", "w_uq", "kv_norm_g", "w_ukv", "conv_w", "conv_b",
           "conv_ln_g", "conv_ln_b", "conv_out_g", "attn_out_g", "w_out", "ffn_norm_g", "w_ffn_up", "ffn_conv_w",
           "ffn_conv_b", "w_ffn_down", "final_norm_g")


def _lane_rows(a):
    return a.reshape(N_CHIPS, -1, LANES)


def _col_shards(a):
    k = a.shape[0]
    return a.reshape(k, N_CHIPS, -1).transpose(1, 0, 2)


def _from_col_shards(a):
    return a.transpose(1, 0, 2).reshape(a.shape[1], -1)


def _pad_rows_to(a, rows):
    return jnp.pad(a, ((0, 0), (0, rows - a.shape[1]), (0, 0)))


def _rope_tables(L):
    pos = (jnp.arange(L, dtype=jnp.int32) - DEAD).astype(F32)
    inv_freq = 1.0 / (ROPE_THETA ** (jnp.arange(0, QK_ROPE, 2, dtype=F32) / QK_ROPE))
    ang = pos[:, None] * inv_freq[None, :]
    cos, sin = jnp.cos(ang), jnp.sin(ang)
    half = QK_ROPE // 2
    z = lambda n: jnp.zeros((L, n), F32)
    rc = jnp.concatenate([jnp.ones((L, QK_NOPE), F32), cos, cos, z(HB - QK_NOPE - QK_ROPE)], 1)
    rs1 = jnp.concatenate([z(QK_NOPE), -sin, z(HB - QK_NOPE - half)], 1)
    rs2 = jnp.concatenate([z(QK_NOPE + half), sin, z(HB - QK_NOPE - QK_ROPE)], 1)
    return rc, rs1, rs2


def _pad_heads(g):
    return jnp.pad(g.reshape(N_HEADS, V_HEAD), ((0, 0), (HB - V_HEAD, 0))).reshape(1, D_HEADS)


def _unpad_heads(g):
    return g.reshape(N_HEADS, HB)[:, HB - V_HEAD:].reshape(1, N_HEADS * V_HEAD)


def _local_step(x, target, w, late_shards=None, reduce_first=False):
    S = x.shape[0]
    L = TM + S
    tl = L // 4
    d_qk = QK_NOPE + QK_ROPE
    win_n = w["w_in"]
    kr0 = D_AG + Q_LORA + KV_LORA
    win = jnp.concatenate([win_n[:, :kr0], jnp.zeros((D_MODEL, QK_NOPE), BF16), win_n[:, kr0:],
                           jnp.zeros((D_MODEL, HB - d_qk), BF16)], 1)
    wuq = jnp.pad(w["w_uq"].reshape(Q_LORA, N_HEADS, d_qk), ((0, 0), (0, 0), (0, HB - d_qk))).reshape(Q_LORA, D_HEADS)
    wukv = w["w_ukv"]
    ga = _pad_heads(w["attn_out_g"])
    gfin = w["final_norm_g"].reshape(1, D_MODEL)
    rc, rs1, rs2 = _rope_tables(L)
    head = jnp.concatenate([jnp.zeros((DEAD, D_MODEL), F32), w["meta_tokens"]], 0)

    h0, n, zag, u0, cq, ckv, qn, kvn, q, kv, kr, kvt = _fwd_in(x, head, w["mix_norm_g"], win, w["q_norm_g"], wuq,
                                                                w["kv_norm_g"], wukv, rc, rs1, rs2)
    u1, mixa = _fwd_conv(u0, w["conv_w"], w["conv_b"], w["conv_ln_g"], w["conv_ln_b"], w["conv_out_g"])
    if late_shards is None:
        o, lse, _ = _attn_fwd(q, kv, kvt, kr)
        wout_n, wup, wdown = w["w_out"], _col_shards(w["w_ffn_up"]), w["w_ffn_down"]
    else:
        o, lse, late = _attn_fwd(q, kv, kvt, kr, [late_shards[k] for k in LATE])
        wout_n, wup, wdown = _full_weight("w_out", late[0]), late[1], _full_weight("w_ffn_down", late[2])
    wout = jnp.concatenate([wout_n[:D_CONV], jnp.pad(wout_n[D_CONV:].reshape(N_HEADS, V_HEAD, D_MODEL),
                                                     ((0, 0), (HB - V_HEAD, 0), (0, 0))).reshape(D_HEADS, D_MODEL)], 0)
    mix, h1, n2 = _fwd_mix(h0, mixa, o, ga, wout, w["ffn_norm_g"])
    up0 = _mm(n2, wup, NN, BF16, tl, FF_TILE, D_MODEL, "ffn_up", b_slabs=True)
    dh2, loss, g_fin, up, act = _fwd_ffn_loss(up0, w["ffn_conv_w"], w["ffn_conv_b"], wdown, h1, target, gfin)

    dup, g_fcb = _bwd_ffn_act(dh2, up, wdown)
    dn2, dup0, g_fcw_tiles = _bwd_ffn_conv_up(dup, up0, w["ffn_conv_w"], wup, tl)
    g_fcw = jnp.sum(g_fcw_tiles, 0)
    g_wup = _mm(n2, dup0, TN, BF16, D_MODEL, FF_TILE, tl, "ffn_up_dw", by_col_tile=True)
    g_wdown = _mm(act, dh2, TN, BF16, D_FF // 2, D_MODEL, tl, "ffn_down_dw").reshape(N_CHIPS, -1, D_MODEL)
    ffn = [g_wup, g_wdown] if reduce_first else []
    (dh1, dh1m, do, delta, du1, g_gffn, g_ga, g_og, g_lg, g_lb), theirs = _bwd_mix(
        dh2, dn2, h1, w["ffn_norm_g"], wout, o, ga, u1, w["conv_ln_g"], w["conv_ln_b"], w["conv_out_g"], ffn)
    g_wout = _mm(mix, dh1m, TN, BF16, D_MIX // 2, D_MODEL, tl, "out_dw")
    g_wout = jnp.concatenate([g_wout[:D_CONV], g_wout[D_CONV:].reshape(N_HEADS, HB, D_MODEL)[:, HB - V_HEAD:]
                              .reshape(N_HEADS * V_HEAD, D_MODEL)], 0).reshape(N_CHIPS, -1, D_MODEL)
    pair = ()
    if reduce_first:
        theirs += _sibling_exchange([g_wout], "grad_sibling_exchange_out")
        pair = _add_pairs(ffn + [g_wout], theirs, "first")
    dq, dkv, dkr, got = _attn_bwd(q, kv, kr, do, lse, delta, pair)
    halves = _add_all_chips(pair, got, "first") if reduce_first else ()
    (dzag, g_cw, g_cb), both = _bwd_conv(du1, u0, w["conv_w"], zag, halves)
    if reduce_first:
        g_wup, g_wdown, g_wout = _totals(both)
    dz, dqr, gx, dfirst, g_gq, g_gkv, g_gmix = _bwd_in(dzag, dq, dkv, dkr, cq, ckv, w["q_norm_g"], w["kv_norm_g"],
                                                      wuq, wukv, win, rc, rs1, rs2, h0, w["mix_norm_g"], dh1)
    g_win = _mm(n, dz, TN, BF16, D_MODEL, D_ZP // 2, tl, "in_dw")
    g_wuq = _mm(qn, dqr, TN, BF16, Q_LORA, D_HEADS, tl, "uq_dw")
    g_wukv = _col_shards(_mm(kvn, dkv, TN, BF16, KV_LORA, D_HEADS, tl, "ukv_dw"))

    grads = {
        "w_in": _col_shards(jnp.concatenate([g_win[:, :kr0], g_win[:, kr0 + QK_NOPE:kr0 + d_qk]], 1)),
        "w_uq": _col_shards(g_wuq.reshape(Q_LORA, N_HEADS, HB)[:, :, :d_qk].reshape(Q_LORA, N_HEADS * d_qk)),
        "w_ukv": g_wukv,
        "w_out": g_wout,
        "w_ffn_up": g_wup, "w_ffn_down": g_wdown, "conv_w": g_cw, "ffn_conv_w": g_fcw,
        "meta_tokens": dfirst[DEAD:], "mix_norm_g": g_gmix, "q_norm_g": g_gq, "kv_norm_g": g_gkv, "conv_b": g_cb,
        "conv_ln_g": g_lg, "conv_ln_b": g_lb, "conv_out_g": g_og, "attn_out_g": _unpad_heads(g_ga),
        "ffn_norm_g": g_gffn, "ffn_conv_b": g_fcb, "final_norm_g": g_fin,
    }
    return loss, gx, grads


ROW_SHARDED = ("w_out", "w_ffn_down")


TRANSPOSED = ("w_in", "w_uq")
REDUCED_FIRST = ("w_ffn_up", "w_ffn_down", "w_out")
LATE = ("w_out", "w_ffn_up", "w_ffn_down")


def _full_weight(name, by_chip):
    return by_chip.reshape(-1, by_chip.shape[-1]) if name in ROW_SHARDED else _from_col_shards(by_chip)


def kernel(x, meta_tokens, mix_norm_g, w_in, q_norm_g, w_uq, kv_norm_g, w_ukv, conv_w, conv_b, conv_ln_g, conv_ln_b, conv_out_g, attn_out_g, w_out, ffn_norm_g, w_ffn_up, ffn_conv_w, ffn_conv_b, w_ffn_down, final_norm_g, loss_target, m_meta_tokens, m_mix_norm_g, m_w_in, m_q_norm_g, m_w_uq, m_kv_norm_g, m_w_ukv, m_conv_w, m_conv_b, m_conv_ln_g, m_conv_ln_b, m_conv_out_g, m_attn_out_g, m_w_out, m_ffn_norm_g, m_w_ffn_up, m_ffn_conv_w, m_ffn_conv_b, m_w_ffn_down, m_final_norm_g, v_meta_tokens, v_mix_norm_g, v_w_in, v_q_norm_g, v_w_uq, v_kv_norm_g, v_w_ukv, v_conv_w, v_conv_b, v_conv_ln_g, v_conv_ln_b, v_conv_out_g, v_attn_out_g, v_w_out, v_ffn_norm_g, v_w_ffn_up, v_ffn_conv_w, v_ffn_conv_b, v_w_ffn_down, v_final_norm_g):
    local = dict(meta_tokens=meta_tokens, mix_norm_g=mix_norm_g, w_in=w_in[0], q_norm_g=q_norm_g, w_uq=w_uq[0],
                 kv_norm_g=kv_norm_g, w_ukv=w_ukv[0], conv_w=conv_w[0], conv_b=conv_b, conv_ln_g=conv_ln_g,
                 conv_ln_b=conv_ln_b, conv_out_g=conv_out_g, attn_out_g=attn_out_g, w_out=w_out[0],
                 ffn_norm_g=ffn_norm_g, w_ffn_up=w_ffn_up[0], ffn_conv_w=ffn_conv_w[0], ffn_conv_b=ffn_conv_b,
                 w_ffn_down=w_ffn_down[0], final_norm_g=final_norm_g.reshape(1, D_MODEL))
    ms = dict(zip(WEIGHTS, (m_meta_tokens, m_mix_norm_g, m_w_in, m_q_norm_g, m_w_uq, m_kv_norm_g, m_w_ukv, m_conv_w,
                            m_conv_b, m_conv_ln_g, m_conv_ln_b, m_conv_out_g, m_attn_out_g, m_w_out, m_ffn_norm_g,
                            m_w_ffn_up, m_ffn_conv_w, m_ffn_conv_b, m_w_ffn_down, m_final_norm_g)))
    vs = dict(zip(WEIGHTS, (v_meta_tokens, v_mix_norm_g, v_w_in, v_q_norm_g, v_w_uq, v_kv_norm_g, v_w_ukv, v_conv_w,
                            v_conv_b, v_conv_ln_g, v_conv_ln_b, v_conv_out_g, v_attn_out_g, v_w_out, v_ffn_norm_g,
                            v_w_ffn_up, v_ffn_conv_w, v_ffn_conv_b, v_w_ffn_down, v_final_norm_g)))

    small_flat = jnp.concatenate([local[k].reshape(-1) for k in SMALL_SHARDED]).reshape(-1, LANES)
    early = [k for k in BIG if k not in LATE]
    gathered = _all_gather([local[k].astype(BF16) for k in early] + [small_flat])
    full = {k: v for k, v in local.items() if k not in LATE}
    for name, g in zip(early, gathered[:len(early)]):
        full[name] = _full_weight(name, g)
    small = gathered[-1].reshape(N_CHIPS, -1)
    at = 0
    for name in SMALL_SHARDED:
        r, c = local[name].shape
        full[name] = _from_col_shards(small[:, at:at + r * c].reshape(N_CHIPS, r, c))
        at += r * c

    loss_row, grad_x, grads = _local_step(x[0], loss_target[0], full, {k: local[k].astype(BF16) for k in LATE},
                                          reduce_first=True)

    rest_big = [k for k in BIG if k not in REDUCED_FIRST]
    rep = jnp.concatenate([grads[k].reshape(-1) for k in REPLICATED] + [loss_row.reshape(-1)]).reshape(1, -1, LANES)
    small_pieces = [_lane_rows(_col_shards(grads[k])) for k in SMALL_SHARDED]
    small_pieces.append(jnp.broadcast_to(rep, (N_CHIPS,) + rep.shape[1:]))
    small_rows = sum(p.shape[1] for p in small_pieces)
    small_pack = _pad_rows_to(jnp.concatenate(small_pieces, 1), -(-small_rows // 32) * 32)
    pair = _reduce_begin([grads[k] for k in rest_big] + [small_pack], "rest")

    total = {k: grads[k] for k in REDUCED_FIRST}
    delta, new_m, new_v = {}, {}, {}
    shape2 = lambda a, name: a.reshape(local[name].shape)
    turn = lambda a, name: a.T if name in TRANSPOSED else a

    def update(name, parts=()):
        outs, got = _adamw_big(turn(local[name], name), turn(total[name], name), turn(shape2(ms[name], name), name),
                               turn(shape2(vs[name], name), name), "adamw_" + name, parts)
        total[name], delta[name], new_m[name], new_v[name] = (turn(o, name) for o in outs)
        return got

    *rest_tot, small_tot = _reduce_end(pair, update(REDUCED_FIRST[0], pair), "rest")
    total.update(zip(rest_big, rest_tot))
    flat = small_tot.reshape(-1)
    at = 0
    for name in SMALL_SHARDED + REPLICATED:
        shape = local[name].shape
        size = shape[0] * shape[1]
        total[name] = flat[at:at + size].reshape(shape)
        at += -(-size // LANES) * LANES if name in SMALL_SHARDED else size
    loss = flat[at]

    for name in BIG:
        if name != REDUCED_FIRST[0]:
            update(name)
    rest = SMALL_SHARDED + REPLICATED
    ds, nms, nvs = _adamw_small([local[k] for k in rest], [total[k] for k in rest],
                                [shape2(ms[k], k) for k in rest], [shape2(vs[k], k) for k in rest])
    for k, d, nm, nv in zip(rest, ds, nms, nvs):
        delta[k], new_m[k], new_v[k] = d, nm, nv

    out_shape = dict(zip(WEIGHTS, (meta_tokens, mix_norm_g, w_in, q_norm_g, w_uq, kv_norm_g, w_ukv, conv_w, conv_b,
                                   conv_ln_g, conv_ln_b, conv_out_g, attn_out_g, w_out, ffn_norm_g, w_ffn_up,
                                   ffn_conv_w, ffn_conv_b, w_ffn_down, final_norm_g)))
    outs = [loss, grad_x[None]]
    for group in (total, delta, new_m, new_v):
        outs += [group[k].reshape(out_shape[k].shape) for k in WEIGHTS]
    return tuple(outs)
```

```python
import functools

import jax
import jax.numpy as jnp
from jax import lax
from jax.experimental import pallas as pl
from jax.experimental.pallas import tpu as pltpu

F32 = jnp.float32
BF16 = jnp.bfloat16

D_MODEL = 1024
D_CONV = 512
CONV_WIDTH = 31
N_HEADS = 8
QK_NOPE = 64
QK_ROPE = 32
V_HEAD = 64
Q_LORA = 384
KV_LORA = 256
D_FF = 2816
FFN_CONV_WIDTH = 3
CHUNK_SHIFT = 6
N_META = 16
ROPE_THETA = 10000.0
EPS = 1e-6
NEG = -1e30
ADAM_LR = 0.001
ADAM_B1 = 0.9
ADAM_B2 = 0.999
ADAM_EPS = 1e-08
ADAM_WD = 0.01
ADAM_STEP = 10

LANES = 128
SUBLANES = 8
HB = LANES
D_HEADS = N_HEADS * HB
TM = 256
DEAD = TM - N_META
D_AG = 2 * D_CONV
D_ZP = D_AG + Q_LORA + KV_LORA + HB
D_MIX = D_CONV + D_HEADS
LN2 = 0.6931471805599453
Q_SCALE = (QK_NOPE + QK_ROPE) ** -0.5 / LN2
HALO_CONV = 32
HALO_FFN = 16
FF_CHUNK = 256
FF_MXU_CHUNK = 256
FF_TILE = D_FF // 2
VMEM_LIMIT = 56 * 1024 * 1024
ADAMW_MAX_STEPS = 32
N_CHIPS = 4
HEAD_GROUP = 4
N_GROUPS = N_HEADS // HEAD_GROUP
MESH =pl.DeviceIdType.MESH


def _params(n_grid):
    return pltpu.CompilerParams(dimension_semantics=("arbitrary",) * n_grid, vmem_limit_bytes=VMEM_LIMIT)


def _rows(tm, c, off=0):
    return pl.BlockSpec((tm, c), lambda i: (i, off))


def _full(shape):
    return pl.BlockSpec(shape, lambda i: (0,) * len(shape))


def _prev(hb, c, tm, off=0):
    return pl.BlockSpec((hb, c), lambda i: (jnp.maximum(i * (tm // hb) - 1, 0), off))


def _next(hb, c, tm, nblk, off=0):
    return pl.BlockSpec((hb, c), lambda i: (jnp.minimum((i + 1) * (tm // hb), nblk - 1), off))


def _sds(shape, dtype):
    return jax.ShapeDtypeStruct(shape, dtype)


def _rms_r(x, n):
    return lax.rsqrt(jnp.sum(x * x, -1, keepdims=True) * (1.0 / n) + EPS)


def _rms_bwd(dy, x, r, g, n):
    gd = dy * g
    dx = r * gd - x * (r * r * r) * (jnp.sum(x * gd, -1, keepdims=True) * (1.0 / n))
    return dx, jnp.sum(dy * x * r, 0, keepdims=True)


def _dot(a, b, dims):
    return lax.dot_general(a, b, (dims, ((), ())), preferred_element_type=F32)


NN = ((1,), (0,))
NT = ((1,), (1,))
TN = ((0,), (0,))


def _rope(x, c, s1, s2):
    n = x.shape[-1]
    return x * c + pltpu.roll(x, n - QK_ROPE // 2, 1) * s1 + pltpu.roll(x, QK_ROPE // 2, 1) * s2


def _rope_bwd(g, c, s1, s2):
    n = g.shape[-1]
    return g * c + pltpu.roll(g * s1, QK_ROPE // 2, 1) + pltpu.roll(g * s2, n - QK_ROPE // 2, 1)


def _row_ids(tm, cols=1):
    return pl.program_id(0) * tm + lax.broadcasted_iota(jnp.int32, (tm, cols), 0)


def _mm(a, b, dims, out_dtype, tm, tn, tk, name, by_col_tile=False, b_slabs=False):
    if dims == TN:
        (kk, m), (_, n) = a.shape, b.shape
        a_spec = pl.BlockSpec((tk, tm), lambda i, j, k: (k, i))
    else:
        m, kk = a.shape
        a_spec = pl.BlockSpec((tm, tk), lambda i, j, k: (i, k))
    if dims == NT and b_slabs:
        n = b.shape[1]
        assert b.shape[2] == tk and kk == b.shape[0] * tk, (name, b.shape)
        b_spec = pl.BlockSpec((None, tn, tk), lambda i, j, k: (k, j, 0))
    elif dims == NT:
        n = b.shape[0]
        b_spec = pl.BlockSpec((tn, tk), lambda i, j, k: (j, k))
    elif b_slabs:
        n = b.shape[0] * b.shape[2]
        assert b.shape[2] == tn and kk == b.shape[1], (name, b.shape)
        b_spec = pl.BlockSpec((None, tk, tn), lambda i, j, k: (j, k, 0))
    else:
        n = b.shape[1]
        b_spec = pl.BlockSpec((tk, tn), lambda i, j, k: (k, j))
    assert m % tm == 0 and n % tn == 0 and kk % tk == 0, (name, a.shape, b.shape, tm, tn, tk)
    nk = kk // tk

    def body(a_ref, b_ref, o_ref, acc_ref):
        k = pl.program_id(2)

        @pl.when(k == 0)
        def _():
            acc_ref[...] = jnp.zeros_like(acc_ref)

        acc_ref[...] += _dot(a_ref[...].astype(BF16), b_ref[...].astype(BF16), dims)

        @pl.when(k == nk - 1)
        def _():
            o_ref[...] = acc_ref[...].astype(out_dtype)

    if by_col_tile:
        out_spec, out_shape = pl.BlockSpec((None, tm, tn), lambda i, j, k: (j, i, 0)), (n // tn, m, tn)
    else:
        out_spec, out_shape = pl.BlockSpec((tm, tn), lambda i, j, k: (i, j)), (m, n)
    return pl.pallas_call(
        body, name=name, grid=(m // tm, n // tn, nk), in_specs=[a_spec, b_spec], out_specs=out_spec,
        out_shape=_sds(out_shape, out_dtype), scratch_shapes=[pltpu.VMEM((tm, tn), F32)],
        compiler_params=_params(3))(a, b)


def _fwd_in(x, head, gmix, win, gq, wuq, gkv, wukv, rc, rs1, rs2):
    L = TM + x.shape[0]

    def body(x_ref, head_ref, gmix_ref, win_ref, gq_ref, wuq_ref, gkv_ref, wukv_ref, c_ref, s1_ref, s2_ref,
             h0_ref, n_ref, zag_ref, u0_ref, cq_ref, ckv_ref, qn_ref, kvn_ref, q_ref, kv_ref, kr_ref, kvt_ref):
        h = jnp.where(pl.program_id(0) == 0, head_ref[...], x_ref[...])
        h0_ref[...] = h
        n = (h * _rms_r(h, D_MODEL) * gmix_ref[...]).astype(BF16)
        n_ref[...] = n
        z = _dot(n, win_ref[...], NN)
        a, gate = z[:, :D_CONV], z[:, D_CONV:D_AG]
        zag_ref[...] = z[:, :D_AG].astype(BF16)
        u0_ref[...] = a * jax.nn.sigmoid(gate)
        cq = z[:, D_AG:D_AG + Q_LORA]
        ckv = z[:, D_AG + Q_LORA:D_AG + Q_LORA + KV_LORA]
        krp = z[:, D_AG + Q_LORA + KV_LORA:]
        cq_ref[...] = cq
        ckv_ref[...] = ckv
        qn = (cq * _rms_r(cq, Q_LORA) * gq_ref[...]).astype(BF16)
        qn_ref[...] = qn
        kvn = (ckv * _rms_r(ckv, KV_LORA) * gkv_ref[...]).astype(BF16)
        kvn_ref[...] = kvn
        c, s1, s2 = c_ref[...], s1_ref[...], s2_ref[...]
        q = _dot(qn, wuq_ref[...], NN)
        q = _rope(q, jnp.tile(c, (1, N_HEADS)), jnp.tile(s1, (1, N_HEADS)), jnp.tile(s2, (1, N_HEADS)))
        q_ref[...] = (q * Q_SCALE).astype(BF16)
        kv = _dot(kvn, wukv_ref[...], NN)
        kv_ref[...] = kv.astype(BF16)
        kvt_ref[...] = kv.T.astype(BF16)
        kr_ref[...] = _rope(krp, c, s1, s2).astype(BF16)

    outs = [(D_MODEL, F32), (D_MODEL, BF16), (D_AG, BF16), (D_CONV, F32), (Q_LORA, F32), (KV_LORA, F32), (Q_LORA, BF16),
            (KV_LORA, BF16), (D_HEADS, BF16), (D_HEADS, BF16), (HB, BF16)]
    return pl.pallas_call(
        body, name="fwd_in", grid=(L // TM,),
        in_specs=[pl.BlockSpec((TM, D_MODEL), lambda i: (jnp.maximum(i - 1, 0), 0)), _full(head.shape),
                  _full(gmix.shape), _full(win.shape), _full(gq.shape), _full(wuq.shape),
                  _full(gkv.shape), _full(wukv.shape), _rows(TM, HB), _rows(TM, HB), _rows(TM, HB)],
        out_specs=[_rows(TM, c) for c, _ in outs] + [pl.BlockSpec((D_HEADS, TM), lambda i: (0, i))],
        out_shape=[_sds((L, c), d) for c, d in outs] + [_sds((D_HEADS, L), BF16)],
        compiler_params=_params(1))(x, head, gmix, win, gq, wuq, gkv, wukv, rc, rs1, rs2)


def _conv_taps(xx, w_ref, halo, tm, flip):
    kw = w_ref.shape[0]
    acc, rolls = None, {}
    for k in range(kw):
        term = w_ref[k:k + 1, :] * _shifted_rows(xx, kw - 1 - k, halo, tm, flip, rolls)
        acc = term if acc is None else acc + term
    return acc


def _shifted_rows(xx, d, halo, tm, flip, rolls):
    a, b = divmod(d, SUBLANES)
    if b not in rolls:
        rolls[b] = xx if b == 0 else pltpu.roll(xx, (xx.shape[0] - b) if flip else b, 0)
    start = SUBLANES * a if flip else halo - SUBLANES * a
    return rolls[b][start:start + tm]


def _ln_silu(u1, lg, lb):
    mu = jnp.mean(u1, -1, keepdims=True)
    xc = u1 - mu
    rs = lax.rsqrt(jnp.mean(xc * xc, -1, keepdims=True) + EPS)
    xh = xc * rs
    u2 = xh * lg + lb
    sg = jax.nn.sigmoid(u2)
    return rs, xh, u2, sg, u2 * sg


def _fwd_conv(u0, cw, cb, lg, lb, og):
    L = u0.shape[0]

    def body(u0_ref, u0p_ref, cw_ref, cb_ref, lg_ref, lb_ref, og_ref, u1_ref, mixa_ref):
        halo = jnp.where(pl.program_id(0) > 0, u0p_ref[...], 0.0)
        xx = jnp.concatenate([halo, u0_ref[...]], 0)
        u1 = _conv_taps(xx, cw_ref, HALO_CONV, TM, False) + cb_ref[...]
        u1_ref[...] = u1
        u = _ln_silu(u1, lg_ref[...], lb_ref[...])[4]
        mixa_ref[...] = (u * _rms_r(u, D_CONV) * og_ref[...]).astype(BF16)

    return pl.pallas_call(
        body, name="fwd_conv", grid=(L // TM,),
        in_specs=[_rows(TM, D_CONV), _prev(HALO_CONV, D_CONV, TM), _full(cw.shape), _full(cb.shape),
                  _full(lg.shape), _full(lb.shape), _full(og.shape)],
        out_specs=[_rows(TM, D_CONV), _rows(TM, D_CONV)],
        out_shape=[_sds((L, D_CONV), F32), _sds((L, D_CONV), BF16)],
        compiler_params=_params(1))(u0, u0, cw, cb, lg, lb, og)


def _visible_t(i, j, t):
    key = j * t + lax.broadcasted_iota(jnp.int32, (t, t), 0)
    query = i * t + lax.broadcasted_iota(jnp.int32, (t, t), 1)
    return (lax.shift_right_logical(key, CHUNK_SHIFT) <= lax.shift_right_logical(query, CHUNK_SHIFT)) & (key >= DEAD)


def _stat_lane(h):
    return (h // HEAD_GROUP) * LANES + h % HEAD_GROUP


def _scatter_stats(cols, t):
    lane = lax.broadcasted_iota(jnp.int32, (t, N_GROUPS * LANES), 1)
    out = jnp.zeros((t, N_GROUPS * LANES), F32)
    for h, col in enumerate(cols):
        out = jnp.where(lane == _stat_lane(h), col, out)
    return out


def _resident(shape, index_map):
    return pl.BlockSpec(shape, index_map, pipeline_mode=pl.Buffered(1))


def _attn_fwd(q, kv, kvt, kr, shards=()):
    L = q.shape[0]
    t = TM
    nq = L // t
    n = len(shards)
    pass_step = (5 * nq) // 6

    def body(q_ref, kv_ref, kvt_ref, kr_ref, *refs):
        gather_refs = refs[:n] + refs[n + 2:2 * n + 2] + refs[2 * n + 6:]
        o_ref, lse_ref = refs[n:n + 2]
        qt_scr, m_scr, l_scr, acc_scr = refs[2 * n + 2:2 * n + 6]
        i = pl.program_id(0)
        if n:
            @pl.when(i == 0)
            def _():
                for cp in _gather_copies(gather_refs[:n], gather_refs[n:2 * n], *gather_refs[2 * n:])[0]:
                    cp().start()

        lane = lax.broadcasted_iota(jnp.int32, (t, HB), 1)
        heads = range(N_HEADS)
        cols = [slice(h * HB, (h + 1) * HB) for h in heads]
        for h in heads:
            qt_scr[cols[h], :] = q_ref[:, cols[h]].astype(F32).T.astype(BF16)
        m_scr[...] = jnp.full_like(m_scr, NEG)
        l_scr[...] = jnp.zeros_like(l_scr)
        acc_scr[...] = jnp.zeros_like(acc_scr)

        def tile(keys, vis, whole=True):
            krj = kr_ref[keys, :]
            kvj = [kv_ref[keys, cols[h]] for h in heads]
            lane_k = lane[:krj.shape[0]]
            s = [_dot(jnp.where(lane_k < QK_NOPE, kvj[h], krj), qt_scr[cols[h], :], NN) for h in heads]
            p, alpha = [], []
            for h in heads:
                sh = s[h] if vis is None else jnp.where(vis, s[h], NEG)
                m_prev = m_scr[h:h + 1, :]
                m_new = jnp.maximum(m_prev, jnp.max(sh, 0, keepdims=True))
                a = jnp.exp2(m_prev - m_new)
                ph = jnp.exp2(sh - m_new)
                l_scr[h:h + 1, :] = a * l_scr[h:h + 1, :] + jnp.sum(ph, 0, keepdims=True)
                m_scr[h:h + 1, :] = m_new
                p.append(ph.astype(BF16))
                alpha.append(a)
            for h in heads:
                pv = _dot(kvt_ref[cols[h], keys], p[h], NN) if whole else _dot(kvj[h], p[h], TN)
                acc_scr[cols[h], :] = alpha[h] * acc_scr[cols[h], :] + pv

        tile(pl.ds(pl.multiple_of(i * t, t), t), _visible_t(i, i, t))

        @pl.when(i > 0)
        def _():
            tile(pl.ds(DEAD, N_META), None, whole=False)

        count = jnp.maximum(i - 1, 0)

        def unmasked_pair(r, carry):
            tile(pl.ds(pl.multiple_of((1 + 2 * r) * t, t), t), None)
            tile(pl.ds(pl.multiple_of((2 + 2 * r) * t, t), t), None)
            return carry

        lax.fori_loop(0, count // 2, unmasked_pair, 0)

        @pl.when(count % 2 == 1)
        def _():
            tile(pl.ds(pl.multiple_of((i - 1) * t, t), t), None)
        lse_ref[...] = jnp.zeros_like(lse_ref)
        for h in heads:
            l = l_scr[h:h + 1, :]
            o_ref[:, cols[h]] = jnp.where(lane >= QK_NOPE, (acc_scr[cols[h], :] / l).T, 0.0).astype(BF16)
            lse_ref[h // HEAD_GROUP, h % HEAD_GROUP:h % HEAD_GROUP + 1, :] = m_scr[h:h + 1, :] + jnp.log2(l)
        if n:
            @pl.when(i == pass_step)
            def _():
                _, arrivals, forwards, _ = _gather_copies(gather_refs[:n], gather_refs[n:2 * n], *gather_refs[2 * n:])
                for landed, onward in zip(arrivals, forwards):
                    landed().wait_recv()
                    onward().start()

            @pl.when(i == nq - 1)
            def _():
                sends, _, forwards, finals = _gather_copies(gather_refs[:n], gather_refs[n:2 * n], *gather_refs[2 * n:])
                for cp in finals:
                    cp().wait_recv()
                for cp in sends + forwards:
                    cp().wait_send()

    any_spec = pl.BlockSpec(memory_space=pl.ANY)
    outs = pl.pallas_call(
        body, name="attn_fwd", grid=(nq,),
        in_specs=[_rows(t, D_HEADS), _resident((L, D_HEADS), lambda i: (0, 0)),
                  _resident((D_HEADS, L), lambda i: (0, 0)), _resident((L, HB), lambda i: (0, 0))] + [any_spec] * n,
        out_specs=[_rows(t, D_HEADS), pl.BlockSpec((N_GROUPS, SUBLANES, t), lambda i: (0, 0, i))] + [any_spec] * n,
        out_shape=[_sds((L, D_HEADS), BF16), _sds((N_GROUPS, SUBLANES, L), F32)] + _gather_out_shapes(shards),
        scratch_shapes=[pltpu.VMEM((D_HEADS, t), BF16), pltpu.VMEM((N_HEADS, t), F32), pltpu.VMEM((N_HEADS, t), F32),
                        pltpu.VMEM((D_HEADS, t), F32)] + (_gather_semaphores(n) if n else []),
        compiler_params=_params(1))(q, kv, kvt, kr, *shards)
    return outs[0], outs[1], _gathered(outs[2:], shards)


def _fwd_mix(h0, mixa, o, ga, wout, gffn):
    L = h0.shape[0]

    def body(h0_ref, mixa_ref, o_ref, ga_ref, wout_ref, gffn_ref, mix_ref, h1_ref, n2_ref):
        of = o_ref[...].astype(F32)
        mixb = (of * _rms_r(of, N_HEADS * V_HEAD) * ga_ref[...]).astype(BF16)
        mix = jnp.concatenate([mixa_ref[...], mixb], 1)
        mix_ref[...] = mix
        mo = jnp.where(_row_ids(TM) >= DEAD, _dot(mix, wout_ref[...], NN), 0.0)
        h1 = h0_ref[...] + mo
        h1_ref[...] = h1
        n2_ref[...] = (h1 * _rms_r(h1, D_MODEL) * gffn_ref[...]).astype(BF16)

    return pl.pallas_call(
        body, name="fwd_mix", grid=(L // TM,),
        in_specs=[_rows(TM, D_MODEL), _rows(TM, D_CONV), _rows(TM, D_HEADS), _full(ga.shape), _full(wout.shape),
                  _full(gffn.shape)],
        out_specs=[_rows(TM, D_MIX), _rows(TM, D_MODEL), _rows(TM, D_MODEL)],
        out_shape=[_sds((L, D_MIX), BF16), _sds((L, D_MODEL), F32), _sds((L, D_MODEL), BF16)],
        compiler_params=_params(1))(h0, mixa, o, ga, wout, gffn)


def _ffn_act_chunk(c, upg_ref, upv_ref, hg, hv, fcw_ref, fcb_ref):
    cs = slice(c * FF_CHUNK, (c + 1) * FF_CHUNK)
    out = []
    for part, (up_ref, halo) in enumerate(((upg_ref, hg), (upv_ref, hv))):
        xx = jnp.concatenate([halo[:, cs], up_ref[:, cs].astype(F32)], 0)
        ws = slice(part * D_FF + c * FF_CHUNK, part * D_FF + (c + 1) * FF_CHUNK)
        y = (fcw_ref[0:1, ws] * pltpu.roll(xx, 2, 0)[HALO_FFN:] + fcw_ref[1:2, ws] * pltpu.roll(xx, 1, 0)[HALO_FFN:]
             + fcw_ref[2:3, ws] * xx[HALO_FFN:] + fcb_ref[:, ws])
        out.append(y)
    return out


def _ffn_in_specs(L):
    return [_rows(TM, D_FF, 0), _rows(TM, D_FF, 1), _prev(HALO_FFN, D_FF, TM, 0), _prev(HALO_FFN, D_FF, TM, 1)]


def _ffn_halos(hg_ref, hv_ref):
    first = pl.program_id(0) == 0
    return (jnp.where(first, 0.0, hg_ref[...].astype(F32)), jnp.where(first, 0.0, hv_ref[...].astype(F32)))


def _fwd_ffn_loss(up0, fcw, fcb, wdown, h1, target, gfin):
    L = h1.shape[0]

    def body(upg_ref, upv_ref, hg_ref, hv_ref, fcw_ref, fcb_ref, wd_ref, h1_ref, t_ref, gf_ref,
             dh2_ref, loss_ref, dgf_ref, up_ref, act_ref, dh2b_ref):
        i = pl.program_id(0)
        hg, hv = _ffn_halos(hg_ref, hv_ref)
        for c in range(D_FF // FF_CHUNK):
            cs = slice(c * FF_CHUNK, (c + 1) * FF_CHUNK)
            g, val = _ffn_act_chunk(c, upg_ref, upv_ref, hg, hv, fcw_ref, fcb_ref)
            up_ref[:, cs] = g.astype(BF16)
            up_ref[:, D_FF + c * FF_CHUNK:D_FF + (c + 1) * FF_CHUNK] = val.astype(BF16)
            act_ref[:, cs] = (g * jax.nn.sigmoid(g) * val).astype(BF16)
        h2 = h1_ref[...] + _dot(act_ref[...], wd_ref[...], NN)
        r = _rms_r(h2, D_MODEL)
        gf = gf_ref[...]
        err = jnp.where(i > 0, h2 * r * gf - t_ref[...], 0.0)
        dy = err * (1.0 / D_MODEL)
        dh2, dgf = _rms_bwd(dy, h2, r, gf, D_MODEL)
        dh2_ref[...] = dh2
        dh2b_ref[...] = dh2.astype(BF16)

        @pl.when(i == 0)
        def _():
            loss_ref[...] = jnp.zeros_like(loss_ref)
            dgf_ref[...] = jnp.zeros_like(dgf_ref)

        loss_ref[...] += jnp.sum(err * err) * (0.5 / D_MODEL)
        dgf_ref[...] += dgf

    return pl.pallas_call(
        body, name="fwd_ffn_loss", grid=(L // TM,),
        in_specs=_ffn_in_specs(L) + [_full(fcw.shape), _full(fcb.shape), _full(wdown.shape), _rows(TM, D_MODEL),
                                     pl.BlockSpec((TM, D_MODEL), lambda i: (jnp.maximum(i - 1, 0), 0)),
                                     _full(gfin.shape)],
        out_specs=[_rows(TM, D_MODEL), _full((1, LANES)), _full((1, D_MODEL)), _rows(TM, 2 * D_FF), _rows(TM, D_FF),
                   _rows(TM, D_MODEL)],
        out_shape=[_sds((L, D_MODEL), F32), _sds((1, LANES), F32), _sds((1, D_MODEL), F32),
                   _sds((L, 2 * D_FF), BF16), _sds((L, D_FF), BF16), _sds((L, D_MODEL), BF16)],
        compiler_params=_params(1))(up0, up0, up0, up0, fcw, fcb, wdown, h1, target, gfin)


def _bwd_ffn_act(dh2, up, wdown):
    L = dh2.shape[0]

    def body(dh2_ref, upg_ref, upv_ref, wd_ref, dup_ref, dfcb_ref):
        da = _dot(dh2_ref[...], wd_ref[...], NT)

        @pl.when(pl.program_id(0) == 0)
        def _():
            dfcb_ref[...] = jnp.zeros_like(dfcb_ref)

        for c in range(D_FF // FF_CHUNK):
            cs = slice(c * FF_CHUNK, (c + 1) * FF_CHUNK)
            vs = slice(D_FF + c * FF_CHUNK, D_FF + (c + 1) * FF_CHUNK)
            g, val = upg_ref[:, cs].astype(F32), upv_ref[:, cs].astype(F32)
            sg = jax.nn.sigmoid(g)
            si = g * sg
            dac = da[:, cs]
            dg = dac * val * (sg * (1.0 + g * (1.0 - sg)))
            dv = dac * si
            dup_ref[:, cs] = dg.astype(BF16)
            dup_ref[:, vs] = dv.astype(BF16)
            dfcb_ref[:, cs] += jnp.sum(dg, 0, keepdims=True)
            dfcb_ref[:, vs] += jnp.sum(dv, 0, keepdims=True)

    return pl.pallas_call(
        body, name="bwd_ffn_act", grid=(L // TM,),
        in_specs=[_rows(TM, D_MODEL), _rows(TM, D_FF, 0), _rows(TM, D_FF, 1), _full(wdown.shape)],
        out_specs=[_rows(TM, 2 * D_FF), _full((1, 2 * D_FF))],
        out_shape=[_sds((L, 2 * D_FF), BF16), _sds((1, 2 * D_FF), F32)],
        compiler_params=_params(1))(dh2, up, up, wdown)


def _bwd_ffn_conv_up(dup, up0, fcw, wup, tl):
    L, C = dup.shape
    tc = FF_TILE
    nt = L // tl
    nhb = L // HALO_FFN
    chunks = [(c0, min(FF_MXU_CHUNK, tc - c0)) for c0 in range(0, tc, FF_MXU_CHUNK)]

    def body(dy_ref, dyn_ref, x_ref, xp_ref, w_ref, wup_ref, dn2_ref, dx_ref, dw_ref, acc_ref):
        i, k = pl.program_id(0), pl.program_id(1)

        @pl.when(k == 0)
        def _():
            acc_ref[...] = jnp.zeros_like(acc_ref)

        last, first = i == nt - 1, i == 0
        for c0, cw in chunks:
            cs = slice(c0, c0 + cw)
            yy = jnp.concatenate([dy_ref[:, cs].astype(F32), jnp.where(last, 0.0, dyn_ref[:, cs].astype(F32))], 0)
            w = w_ref[:, cs]
            dx = (w[0:1] * pltpu.roll(yy, tl + HALO_FFN - 2, 0)[:tl] + w[1:2] * pltpu.roll(yy, tl + HALO_FFN - 1, 0)[:tl]
                  + w[2:3] * yy[:tl]).astype(BF16)
            dx_ref[:, cs] = dx
            acc_ref[...] += _dot(dx, wup_ref[:, cs], NT)
            xx = jnp.concatenate([jnp.where(first, 0.0, xp_ref[:, cs].astype(F32)), x_ref[:, cs].astype(F32)], 0)
            dy = yy[:tl]
            dw_ref[0:1, cs] = jnp.sum(dy * pltpu.roll(xx, 2, 0)[HALO_FFN:], 0, keepdims=True)
            dw_ref[1:2, cs] = jnp.sum(dy * pltpu.roll(xx, 1, 0)[HALO_FFN:], 0, keepdims=True)
            dw_ref[2:3, cs] = jnp.sum(dy * xx[HALO_FFN:], 0, keepdims=True)

        @pl.when(k == C // tc - 1)
        def _():
            dn2_ref[...] = acc_ref[...]

    tile = pl.BlockSpec((tl, tc), lambda i, k: (i, k))
    per = tl // HALO_FFN
    return pl.pallas_call(
        body, name="bwd_ffn_conv_up", grid=(nt, C // tc),
        in_specs=[tile, pl.BlockSpec((HALO_FFN, tc), lambda i, k: (jnp.minimum((i + 1) * per, nhb - 1), k)),
                  tile, pl.BlockSpec((HALO_FFN, tc), lambda i, k: (jnp.maximum(i * per - 1, 0), k)),
                  pl.BlockSpec((FFN_CONV_WIDTH, tc), lambda i, k: (0, k)),
                  pl.BlockSpec((None, D_MODEL, tc), lambda i, k: (k, 0, 0))],
        out_specs=[pl.BlockSpec((tl, D_MODEL), lambda i, k: (i, 0)), tile,
                   pl.BlockSpec((None, FFN_CONV_WIDTH, tc), lambda i, k: (i, 0, k))],
        out_shape=[_sds((L, D_MODEL), F32), _sds((L, C), BF16), _sds((nt, FFN_CONV_WIDTH, C), F32)],
        scratch_shapes=[pltpu.VMEM((tl, D_MODEL), F32)],
        compiler_params=_params(2))(dup, dup, up0, up0, fcw, wup)


def _carrying(core, n_in, n_out, n, copies_fn, steps):
    def body(*refs):
        exchange = (refs[n_in:n_in + n], refs[n_in + n + n_out:n_in + 2 * n + n_out]) + refs[n_in + 2 * n + n_out:]
        if n:
            @pl.when(pl.program_id(0) == 0)
            def _():
                for cp in copies_fn(*exchange):
                    cp().start()

        core(*refs[:n_in], *refs[n_in + n:n_in + n + n_out])
        if n:
            @pl.when(pl.program_id(0) == steps - 1)
            def _():
                for cp in copies_fn(*exchange):
                    cp().wait()

    return body


def _bwd_mix(dh2, dn2, h1, gffn, wout, o, ga, u1, lg, lb, og, parts=()):
    L = h1.shape[0]

    def body(dh2_ref, dn2_ref, h1_ref, gffn_ref, wout_ref, o_ref, ga_ref, u1_ref, lg_ref, lb_ref, og_ref,
             dh1_ref, dh1m_ref, do_ref, delta_ref, du1_ref, dgffn_ref, dga_ref, dog_ref, dlg_ref, dlb_ref):
        h1 = h1_ref[...]
        dn2x, dgffn = _rms_bwd(dn2_ref[...], h1, _rms_r(h1, D_MODEL), gffn_ref[...], D_MODEL)
        dh1 = dh2_ref[...] + dn2x
        dh1_ref[...] = dh1
        dh1m = jnp.where(_row_ids(TM) >= DEAD, dh1, 0.0).astype(BF16)
        dh1m_ref[...] = dh1m
        dmix = _dot(dh1m, wout_ref[...], NT)
        dma, dmb = dmix[:, :D_CONV], dmix[:, D_CONV:]
        of = o_ref[...].astype(F32)
        do, dga = _rms_bwd(dmb, of, _rms_r(of, N_HEADS * V_HEAD), ga_ref[...], N_HEADS * V_HEAD)
        do_ref[...] = do.astype(BF16)
        prod = do * of
        by_lane = _scatter_stats([jnp.sum(prod[:, h * HB:(h + 1) * HB], -1, keepdims=True) for h in range(N_HEADS)], TM)
        by_row = by_lane.T
        for grp in range(N_GROUPS):
            delta_ref[grp] = by_row[grp * LANES:grp * LANES + SUBLANES, :]
        lg = lg_ref[...]
        rs, xh, u2, sg, u = _ln_silu(u1_ref[...], lg, lb_ref[...])
        du, dog = _rms_bwd(dma, u, _rms_r(u, D_CONV), og_ref[...], D_CONV)
        du2 = du * (sg * (1.0 + u2 * (1.0 - sg)))
        dxh = du2 * lg
        du1_ref[...] = rs * (dxh - jnp.mean(dxh, -1, keepdims=True) - xh * jnp.mean(dxh * xh, -1, keepdims=True))

        @pl.when(pl.program_id(0) == 0)
        def _():
            for ref in (dgffn_ref, dga_ref, dog_ref, dlg_ref, dlb_ref):
                ref[...] = jnp.zeros_like(ref)

        dgffn_ref[...] += dgffn
        dga_ref[...] += dga
        dog_ref[...] += dog
        dlg_ref[...] += jnp.sum(du2 * xh, 0, keepdims=True)
        dlb_ref[...] += jnp.sum(du2, 0, keepdims=True)

    n = len(parts)
    any_spec = pl.BlockSpec(memory_space=pl.ANY)
    outs = pl.pallas_call(
        _carrying(body, 11, 10, n, _sibling_exchange_copies, L // TM), name="bwd_mix", grid=(L // TM,),
        in_specs=[_rows(TM, D_MODEL), _rows(TM, D_MODEL), _rows(TM, D_MODEL), _full(gffn.shape), _full(wout.shape),
                  _rows(TM, D_HEADS), _full(ga.shape), _rows(TM, D_CONV), _full(lg.shape), _full(lb.shape),
                  _full(og.shape)] + [any_spec] * n,
        out_specs=[_rows(TM, D_MODEL), _rows(TM, D_MODEL), _rows(TM, D_HEADS),
                   pl.BlockSpec((N_GROUPS, SUBLANES, TM), lambda i: (0, 0, i)),
                   _rows(TM, D_CONV), _full((1, D_MODEL)), _full((1, D_HEADS)), _full((1, D_CONV)),
                   _full((1, D_CONV)), _full((1, D_CONV))] + [any_spec] * n,
        out_shape=[_sds((L, D_MODEL), F32), _sds((L, D_MODEL), BF16), _sds((L, D_HEADS), BF16),
                   _sds((N_GROUPS, SUBLANES, L), F32),
                   _sds((L, D_CONV), F32), _sds((1, D_MODEL), F32), _sds((1, D_HEADS), F32), _sds((1, D_CONV), F32),
                   _sds((1, D_CONV), F32), _sds((1, D_CONV), F32)] + _sibling_exchange_shapes(parts),
        scratch_shapes=_sibling_exchange_semaphores(n) if n else [],
        compiler_params=_params(1))(dh2, dn2, h1, gffn, wout, o, ga, u1, lg, lb, og, *parts)
    return outs[:10], list(outs[10:])


def _attn_bwd(q, kv, kr, do, lse, delta, parts=()):
    L = q.shape[0]
    t = TM
    nt = L // t
    gw = HEAD_GROUP * HB
    n = len(parts)

    def body(q_ref, kv_ref, kr_ref, do_ref, lse_ref, delta_ref, *refs):
        dq_ref, dkv_ref, dkr_ref = refs[n:n + 3]
        dqt_acc, dk_acc, dv_acc, kkt_scr = refs[2 * n + 3:2 * n + 7]
        exchange_refs = (refs[:n], refs[n + 3:2 * n + 3]) + refs[2 * n + 7:]
        g, j = pl.program_id(0), pl.program_id(1)
        if n:
            @pl.when((g == 0) & (j == 0))
            def _():
                for cp in _chip_copies(*exchange_refs)[0]:
                    cp().start()

        lane = lax.broadcasted_iota(jnp.int32, (t, HB), 1)

        @pl.when(j == 0)
        def _():
            dqt_acc[...] = jnp.zeros_like(dqt_acc)

        @pl.when((j == 0) & (g == 0))
        def _():
            dkr_ref[...] = jnp.zeros_like(dkr_ref)

        dk_acc[...] = jnp.zeros_like(dk_acc)
        dv_acc[...] = jnp.zeros_like(dv_acc)
        krj = kr_ref[...]
        heads = range(HEAD_GROUP)
        cols = [slice(h * HB, (h + 1) * HB) for h in heads]
        for hc in cols:
            kkt_scr[hc, :] = jnp.where(lane < QK_NOPE, kv_ref[:, hc], krj).astype(F32).T.astype(BF16)

        def tile(i, vis, whole=True):
            qs = pl.ds(pl.multiple_of(i * t, t), t)
            keys = slice(None) if whole else slice(DEAD, t)
            kvj = [kv_ref[keys, hc] for hc in cols]
            lane_k = lane[:kvj[0].shape[0]]
            kk = [jnp.where(lane_k < QK_NOPE, kvj[h], krj[keys]) for h in heads]
            qi = [q_ref[qs, hc] for hc in cols]
            doi = [do_ref[qs, hc] for hc in cols]
            s = [_dot(kk[h], qi[h].astype(F32).T.astype(BF16), NN) for h in heads]
            dp = [_dot(kvj[h], doi[h].astype(F32).T.astype(BF16), NN) for h in heads]
            p = []
            for h in heads:
                sh = s[h] if vis is None else jnp.where(vis, s[h], NEG)
                p.append(jnp.exp2(sh - lse_ref[h:h + 1, qs]))
            for h in heads:
                dv_acc[keys, cols[h]] += _dot(p[h].astype(BF16), doi[h], NN)
            ds = [(p[h] * (dp[h] - delta_ref[h:h + 1, qs]) * LN2).astype(BF16) for h in heads]
            for h in heads:
                dk_acc[keys, cols[h]] += _dot(ds[h], qi[h], NN)
            for h in heads:
                dqt = _dot(kkt_scr[cols[h], :], ds[h], NN) if whole else _dot(kk[h], ds[h], TN)
                dqt_acc[cols[h], qs] += dqt

        @pl.when(j == 0)
        def _():
            tile(0, _visible_t(0, 0, t)[DEAD:], whole=False)

            def meta_keys(i, carry):
                tile(i, None, whole=False)
                return carry

            lax.fori_loop(1, nt, meta_keys, 0)

        @pl.when(j > 0)
        def _():
            tile(j, _visible_t(j, j, t))

            count = nt - 1 - j

            def unmasked_pair(r, carry):
                tile(j + 1 + 2 * r, None)
                tile(j + 2 + 2 * r, None)
                return carry

            lax.fori_loop(0, count // 2, unmasked_pair, 0)

            @pl.when(count % 2 == 1)
            def _():
                tile(nt - 1, None)

        dkr = jnp.zeros((t, HB), F32)
        for h in range(HEAD_GROUP):
            hc = slice(h * HB, (h + 1) * HB)
            dk = dk_acc[:, hc]
            dkv_ref[:, hc] = jnp.where(lane < QK_NOPE, dk, dv_acc[:, hc]).astype(BF16)
            dkr = dkr + jnp.where(lane >= QK_NOPE, dk, 0.0)
        dkr_ref[pl.ds(pl.multiple_of(j * t, t), t), :] += dkr

        @pl.when(j == nt - 1)
        def _():
            def untranspose(i, carry):
                qs = pl.ds(pl.multiple_of(i * t, t), t)
                dq_ref[qs, :] = (dqt_acc[:, qs].T * Q_SCALE).astype(BF16)
                return carry

            lax.fori_loop(0, nt, untranspose, 0)

        if n:
            @pl.when((g == N_GROUPS - 1) & (j == nt - 1))
            def _():
                sends, arrivals = _chip_copies(*exchange_refs)
                for cp in arrivals:
                    cp().wait_recv()
                for cp in sends:
                    cp().wait_send()

    group = lambda g, j: (0, g)
    stats = _resident((None, SUBLANES, L), lambda g, j: (g, 0, 0))
    any_spec = pl.BlockSpec(memory_space=pl.ANY)
    outs = pl.pallas_call(
        body, name="attn_bwd", grid=(N_GROUPS, nt),
        in_specs=[_resident((L, gw), group), pl.BlockSpec((t, gw), lambda g, j: (j, g)),
                  pl.BlockSpec((t, HB), lambda g, j: (j, 0)), _resident((L, gw), group), stats, stats]
        + [any_spec] * n,
        out_specs=[pl.BlockSpec((L, gw), group), pl.BlockSpec((t, gw), lambda g, j: (j, g)),
                   pl.BlockSpec((L, HB), lambda g, j: (0, 0))] + [any_spec] * n,
        out_shape=[_sds((L, D_HEADS), BF16), _sds((L, D_HEADS), BF16), _sds((L, HB), F32)]
        + [_sds(p.shape, p.dtype) for p in parts],
        scratch_shapes=[pltpu.VMEM((gw, L), F32), pltpu.VMEM((t, gw), F32), pltpu.VMEM((t, gw), F32),
                        pltpu.VMEM((gw, t), BF16)] + (_chip_semaphores(n) if n else []),
        compiler_params=_params(2))(q, kv, kr, do, lse, delta, *parts)
    return outs[0], outs[1], outs[2], list(outs[3:])


def _bwd_conv(du1, u0, cw, zag, halves=()):
    L = du1.shape[0]
    nt = L // TM

    def body(dy_ref, dyn_ref, x_ref, cw_ref, zag_ref, dzag_ref, dcw_ref, dcb_ref):
        i = pl.program_id(0)
        dy = dy_ref[...]
        yy = jnp.concatenate([dy, jnp.where(i < nt - 1, dyn_ref[...], 0.0)], 0)
        x = x_ref[...]

        @pl.when(i == 0)
        def _():
            dcw_ref[...] = jnp.zeros_like(dcw_ref)
            dcb_ref[...] = jnp.zeros_like(dcb_ref)

        du0, rolls = None, {}
        for k in range(CONV_WIDTH):
            ahead = _shifted_rows(yy, CONV_WIDTH - 1 - k, HALO_CONV, TM, True, rolls)
            term = cw_ref[k:k + 1, :] * ahead
            du0 = term if du0 is None else du0 + term
            dcw_ref[k:k + 1, :] += jnp.sum(ahead * x, 0, keepdims=True)
        dcb_ref[...] += jnp.sum(dy, 0, keepdims=True)
        zag = zag_ref[...].astype(F32)
        a, sg = zag[:, :D_CONV], jax.nn.sigmoid(zag[:, D_CONV:])
        dzag_ref[...] = jnp.concatenate([du0 * sg, du0 * a * sg * (1.0 - sg)], 1).astype(BF16)

    n = len(halves)
    any_spec = pl.BlockSpec(memory_space=pl.ANY)
    outs = pl.pallas_call(
        _carrying(body, 5, 3, n, _sibling_gather_copies, nt), name="bwd_conv", grid=(nt,),
        in_specs=[_rows(TM, D_CONV), _next(HALO_CONV, D_CONV, TM, L // HALO_CONV), _rows(TM, D_CONV),
                  _full(cw.shape), _rows(TM, D_AG)] + [any_spec] * n,
        out_specs=[_rows(TM, D_AG), _full(cw.shape), _full((1, D_CONV))] + [any_spec] * n,
        out_shape=[_sds((L, D_AG), BF16), _sds(cw.shape, F32), _sds((1, D_CONV), F32)]
        + [_sds(p.shape, p.dtype) for p in halves],
        input_output_aliases={5 + a: 3 + a for a in range(n)},
        scratch_shapes=_sibling_gather_semaphores(n) if n else [],
        compiler_params=_params(1))(du1, du1, u0, cw, zag, *halves)
    return outs[:3], list(outs[3:])


def _bwd_in(dzag, dq, dkv, dkr, cq, ckv, gq, gkv, wuq, wukv, win, rc, rs1, rs2, h0, gmix, dh1):
    L = h0.shape[0]

    def body(dzag_ref, dq_ref, dkv_ref, dkr_ref, cq_ref, ckv_ref, gq_ref, gkv_ref, wuq_ref, wukv_ref, win_ref,
             c_ref, s1_ref, s2_ref, h0_ref, gmix_ref, dh1_ref,
             dz_ref, dqr_ref, gx_ref, dfirst_ref, dgq_ref, dgkv_ref, dgmix_ref):
        i = pl.program_id(0)
        c, s1, s2 = c_ref[...], s1_ref[...], s2_ref[...]
        dqr = _rope_bwd(dq_ref[...].astype(F32), jnp.tile(c, (1, N_HEADS)), jnp.tile(s1, (1, N_HEADS)),
                        jnp.tile(s2, (1, N_HEADS))).astype(BF16)
        dqr_ref[...] = dqr
        cq, ckv = cq_ref[...], ckv_ref[...]
        dcq, dgq = _rms_bwd(_dot(dqr, wuq_ref[...], NT), cq, _rms_r(cq, Q_LORA), gq_ref[...], Q_LORA)
        dckv, dgkv = _rms_bwd(_dot(dkv_ref[...], wukv_ref[...], NT), ckv, _rms_r(ckv, KV_LORA), gkv_ref[...], KV_LORA)
        dkrp = _rope_bwd(dkr_ref[...], c, s1, s2)
        dz = jnp.concatenate([dzag_ref[...], dcq.astype(BF16), dckv.astype(BF16), dkrp.astype(BF16)], 1)
        dz_ref[...] = dz
        h0 = h0_ref[...]
        dnx, dgmix = _rms_bwd(_dot(dz, win_ref[...], NT), h0, _rms_r(h0, D_MODEL), gmix_ref[...], D_MODEL)
        dh0 = dh1_ref[...] + dnx

        @pl.when(i == 0)
        def _():
            dfirst_ref[...] = dh0
            for ref in (dgq_ref, dgkv_ref, dgmix_ref):
                ref[...] = jnp.zeros_like(ref)

        @pl.when(i > 0)
        def _():
            gx_ref[...] = dh0

        dgq_ref[...] += dgq
        dgkv_ref[...] += dgkv
        dgmix_ref[...] += dgmix

    return pl.pallas_call(
        body, name="bwd_in", grid=(L // TM,),
        in_specs=[_rows(TM, D_AG), _rows(TM, D_HEADS), _rows(TM, D_HEADS), _rows(TM, HB), _rows(TM, Q_LORA),
                  _rows(TM, KV_LORA), _full(gq.shape), _full(gkv.shape), _full(wuq.shape), _full(wukv.shape),
                  _full(win.shape), _rows(TM, HB), _rows(TM, HB), _rows(TM, HB), _rows(TM, D_MODEL),
                  _full(gmix.shape), _rows(TM, D_MODEL)],
        out_specs=[_rows(TM, D_ZP), _rows(TM, D_HEADS),
                   pl.BlockSpec((TM, D_MODEL), lambda i: (jnp.maximum(i - 1, 0), 0)), _full((TM, D_MODEL)),
                   _full((1, Q_LORA)), _full((1, KV_LORA)), _full((1, D_MODEL))],
        out_shape=[_sds((L, D_ZP), BF16), _sds((L, D_HEADS), BF16), _sds((L - TM, D_MODEL), F32),
                   _sds((TM, D_MODEL), F32), _sds((1, Q_LORA), F32), _sds((1, KV_LORA), F32), _sds((1, D_MODEL), F32)],
        compiler_params=_params(1))(dzag, dq, dkv, dkr, cq, ckv, gq, gkv, wuq, wukv, win, rc, rs1, rs2, h0, gmix, dh1)


def _mesh_pos():
    return lax.axis_index("x"), lax.axis_index("y"), lax.axis_index("c")


def _remote_copy(src, dst, send_sem, recv_sem, to):
    return functools.partial(pltpu.make_async_remote_copy, src, dst, send_sem, recv_sem, device_id=to,
                             device_id_type=MESH)


def _all_gather(shards):
    n = len(shards)

    def body(*refs):
        sends, arrivals, forwards, finals = _gather_copies(refs[:n], refs[n:2 * n], *refs[2 * n:])
        for cp in sends:
            cp().start()
        for landed, onward in zip(arrivals, forwards):
            landed().wait_recv()
            onward().start()
        for cp in finals:
            cp().wait_recv()
        for cp in sends + forwards:
            cp().wait_send()

    any_spec = pl.BlockSpec(memory_space=pl.ANY)
    outs = pl.pallas_call(
        body, name="all_gather_weights", in_specs=[any_spec] * n, out_specs=[any_spec] * n,
        out_shape=_gather_out_shapes(shards), scratch_shapes=_gather_semaphores(n))(*shards)
    return _gathered(outs, shards)


def _gather_out_shapes(shards):
    return [_sds((2 * N_CHIPS, s.shape[0] // 2) + s.shape[1:], s.dtype) for s in shards]


def _gather_semaphores(n):
    return [pltpu.SemaphoreType.DMA((n, 8)), pltpu.SemaphoreType.DMA((n, 8))]


def _gathered(outs, shards):
    return [o.reshape((N_CHIPS, s.shape[0]) + s.shape[1:]) for o, s in zip(outs, shards)]


def _gather_copies(ins, outs, send_sems, recv_sems):
    x, y, c = _mesh_pos()
    chips = [(1 - x, y), (x, 1 - y), (1 - x, 1 - y)]
    sends, arrivals, forwards, finals = [], [], [], []

    def copy(src, dst, a, k, to):
        return _remote_copy(src, dst, send_sems.at[a, k], recv_sems.at[a, k], to)

    for a, (src, out) in enumerate(zip(ins, outs)):
        m = out.shape[1]
        mine = src.at[pl.ds(pl.multiple_of(c * m, 16), m)]
        for hf in range(2):
            own = out.at[4 * x + 2 * y + hf]
            sends.append(copy(src.at[pl.ds(hf * m, m)], own, a, 6 + hf, (x, y, 1 - c)))
            finals.append(copy(own, own, a, 6 + hf, (x, y, 1 - c)))
        for k, chip in enumerate(chips):
            slot = 4 * chip[0] + 2 * chip[1]
            sends.append(copy(mine, out.at[4 * x + 2 * y + c], a, k, (*chip, c)))
            arrivals.append(copy(out.at[slot + c], out.at[slot + c], a, k, (*chip, c)))
            forwards.append(copy(out.at[slot + c], out.at[slot + c], a, 3 + k, (x, y, 1 - c)))
            finals.append(copy(out.at[slot + 1 - c], out.at[slot + 1 - c], a, 3 + k, (x, y, 1 - c)))
    return sends, arrivals, forwards, finals


def _sibling_exchange(parts, name):
    n = len(parts)

    def body(*refs):
        copies = _sibling_exchange_copies(refs[:n], refs[n:2 * n], *refs[2 * n:])
        for cp in copies:
            cp().start()
        for cp in copies:
            cp().wait()

    any_spec = pl.BlockSpec(memory_space=pl.ANY)
    return pl.pallas_call(
        body, name=name, in_specs=[any_spec] * n, out_specs=[any_spec] * n,
        out_shape=_sibling_exchange_shapes(parts), scratch_shapes=_sibling_exchange_semaphores(n))(*parts)


def _sibling_exchange_shapes(parts):
    return [_sds((N_CHIPS, p.shape[1] // 2, p.shape[2]), p.dtype) for p in parts]


def _sibling_exchange_semaphores(n):
    return [pltpu.SemaphoreType.DMA((n, N_CHIPS)), pltpu.SemaphoreType.DMA((n, N_CHIPS))]


def _sibling_exchange_copies(ins, theirs, send_sems, recv_sems):
    x, y, c = _mesh_pos()
    copies = []
    for a, (src, dst) in enumerate(zip(ins, theirs)):
        h = dst.shape[1]
        rows = pl.ds(pl.multiple_of((1 - c) * h, 16), h)
        copies += [_remote_copy(src.at[q, rows], dst.at[q], send_sems.at[a, q], recv_sems.at[a, q], (x, y, 1 - c))
                   for q in range(N_CHIPS)]
    return copies


def _chip_semaphores(n):
    return [pltpu.SemaphoreType.DMA((n, 3)), pltpu.SemaphoreType.DMA((n, 3))]


def _chip_copies(ins, outs, send_sems, recv_sems):
    x, y, c = _mesh_pos()
    me = 2 * x + y
    sends, arrivals = [], []
    for a, (src, out) in enumerate(zip(ins, outs)):
        for k, chip in enumerate([(1 - x, y), (x, 1 - y), (1 - x, 1 - y)]):
            slot = 2 * chip[0] + chip[1]
            sems = (send_sems.at[a, k], recv_sems.at[a, k], (*chip, c))
            sends.append(_remote_copy(src.at[slot], out.at[me], *sems))
            arrivals.append(_remote_copy(out.at[slot], out.at[slot], *sems))
    return sends, arrivals


def _sibling_gather(parts, name):
    n = len(parts)

    def body(*refs):
        copies = _sibling_gather_copies(refs[:n], refs[n:2 * n], *refs[2 * n:])
        for cp in copies:
            cp().start()
        for cp in copies:
            cp().wait()

    any_spec = pl.BlockSpec(memory_space=pl.ANY)
    return pl.pallas_call(
        body, name=name, in_specs=[any_spec] * n, out_specs=[any_spec] * n,
        out_shape=[_sds(p.shape, p.dtype) for p in parts], input_output_aliases={a: a for a in range(n)},
        scratch_shapes=_sibling_gather_semaphores(n))(*parts)


def _sibling_gather_semaphores(n):
    return [pltpu.SemaphoreType.DMA((n,)), pltpu.SemaphoreType.DMA((n,))]


def _sibling_gather_copies(ins, outs, send_sems, recv_sems):
    x, y, c = _mesh_pos()
    return [_remote_copy(src.at[c], dst.at[c], send_sems.at[a], recv_sems.at[a], (x, y, 1 - c))
            for a, (src, dst) in enumerate(zip(ins, outs))]


def _row_tile(rows, row_bytes, align, budget=1 << 20):
    best = None
    for t in range(align, rows + 1, align):
        if rows % t == 0 and t * row_bytes <= budget:
            best = t
    return best or rows


def _scalar(v):
    return jnp.reshape(v, (1,)).astype(jnp.int32)


def _add_pair(part, theirs, c, name):
    _, h, cols = theirs.shape
    tr = _row_tile(h, cols * 4, 16, budget=1 << 21)
    nb = h // tr

    def body(c_ref, a_ref, b_ref, o_ref):
        o_ref[...] = (a_ref[...].astype(F32) + b_ref[...].astype(F32)).astype(o_ref.dtype)

    half = pl.BlockSpec((None, tr, cols), lambda q, i, c_ref: (q, i, 0))
    grid_spec = pltpu.PrefetchScalarGridSpec(
        num_scalar_prefetch=1, grid=(N_CHIPS, nb),
        in_specs=[pl.BlockSpec((None, tr, cols), lambda q, i, c_ref: (q, c_ref[0] * nb + i, 0)), half],
        out_specs=half)
    return pl.pallas_call(body, name=name, grid_spec=grid_spec, out_shape=_sds(theirs.shape, part.dtype),
                          compiler_params=_params(2))(_scalar(c), part, theirs)


def _add_chips(got, own, me, c, name):
    _, h, cols = got.shape
    tr = _row_tile(h, cols * 4 * N_CHIPS, 16, budget=1 << 22)

    def body(pos_ref, got_ref, own_ref, o_ref):
        acc = None
        for q in range(N_CHIPS):
            term = jnp.where(pos_ref[0] == q, own_ref[q], got_ref[q]).astype(F32)
            acc = term if acc is None else acc + term
        o_ref[...] = acc

    by_chip = pl.BlockSpec((N_CHIPS, tr, cols), lambda i, pos_ref: (0, i, 0))
    grid_spec = pltpu.PrefetchScalarGridSpec(
        num_scalar_prefetch=1, grid=(h // tr,), in_specs=[by_chip, by_chip],
        out_specs=pl.BlockSpec((None, tr, cols), lambda i, pos_ref: (pos_ref[1], i, 0)))
    return pl.pallas_call(body, name=name, grid_spec=grid_spec, out_shape=_sds((2, h, cols), F32),
                          compiler_params=_params(1))(jnp.stack([me, c]).astype(jnp.int32), got, own)


def _add_pairs(parts, theirs, tag):
    c = lax.axis_index("c")
    return [_add_pair(p, t, c, f"grad_add_pair_{tag}_{a}") for a, (p, t) in enumerate(zip(parts, theirs))]


def _add_all_chips(pair, got, tag):
    x, y, c = _mesh_pos()
    return [_add_chips(g, p, 2 * x + y, c, f"grad_add_chips_{tag}_{a}") for a, (g, p) in enumerate(zip(got, pair))]


def _totals(both):
    return [b.reshape(-1, b.shape[-1]) for b in both]


def _reduce_begin(parts, tag):
    return _add_pairs(parts, _sibling_exchange(parts, f"grad_sibling_exchange_{tag}"), tag)


def _reduce_end(pair, got, tag):
    return _totals(_sibling_gather(_add_all_chips(pair, got, tag), f"grad_sibling_gather_{tag}"))


def _adamw_math(w, g, m, v):
    m = ADAM_B1 * m + (1.0 - ADAM_B1) * g
    v = ADAM_B2 * v + (1.0 - ADAM_B2) * (g * g)
    m_hat = m / (1.0 - ADAM_B1 ** ADAM_STEP)
    v_hat = v / (1.0 - ADAM_B2 ** ADAM_STEP)
    return -ADAM_LR * (m_hat / (jnp.sqrt(v_hat) + ADAM_EPS) + ADAM_WD * w), m, v


def _adamw_big(w, g, m, v, name, parts=()):
    r, c = w.shape
    tr = _row_tile(r, c * 4, 8, budget=1 << 19)
    if r // tr > ADAMW_MAX_STEPS:
        tr = r
    n = len(parts)
    steps = r // tr

    def body(w_ref, g_ref, m_ref, v_ref, *refs):
        go_ref, d_ref, mo_ref, vo_ref = refs[n:n + 4]
        exchange_refs = (refs[:n], refs[n + 4:2 * n + 4]) + refs[2 * n + 4:]
        if n:
            @pl.when(pl.program_id(0) == 0)
            def _():
                for cp in _chip_copies(*exchange_refs)[0]:
                    cp().start()

        g = g_ref[...]
        go_ref[...] = g
        d_ref[...], mo_ref[...], vo_ref[...] = _adamw_math(w_ref[...], g, m_ref[...], v_ref[...])
        if n:
            @pl.when(pl.program_id(0) == steps - 1)
            def _():
                sends, arrivals = _chip_copies(*exchange_refs)
                for cp in arrivals:
                    cp().wait_recv()
                for cp in sends:
                    cp().wait_send()

    any_spec = pl.BlockSpec(memory_space=pl.ANY)
    outs = pl.pallas_call(
        body, name=name, grid=(steps,), in_specs=[_rows(tr, c)] * 4 + [any_spec] * n,
        out_specs=[_rows(tr, c)] * 4 + [any_spec] * n,
        out_shape=[_sds((r, c), F32)] * 4 + [_sds(p.shape, p.dtype) for p in parts],
        scratch_shapes=_chip_semaphores(n) if n else [], compiler_params=_params(1))(w, g, m, v, *parts)
    return outs[:4], list(outs[4:])


def _adamw_small(ws, gs, ms, vs):
    n = len(ws)

    def body(*refs):
        for a in range(n):
            w_ref, g_ref, m_ref, v_ref = (refs[k * n + a] for k in range(4))
            d, m, v = _adamw_math(w_ref[...], g_ref[...], m_ref[...], v_ref[...])
            refs[4 * n + a][...] = d
            refs[5 * n + a][...] = m
            refs[6 * n + a][...] = v

    vm = pl.BlockSpec(memory_space=pltpu.VMEM)
    outs = pl.pallas_call(
        body, name="adamw_small", in_specs=[vm] * (4 * n), out_specs=[vm] * (3 * n),
        out_shape=[_sds(w.shape, F32) for w in ws] * 3)(*ws, *gs, *ms, *vs)
    return outs[:n], outs[n:2 * n], outs[2 * n:]


BIG = ("w_in", "w_uq", "w_ukv", "w_out", "w_ffn_up", "w_ffn_down")
SMALL_SHARDED = ("conv_w", "ffn_conv_w", "meta_tokens")
REPLICATED = ("mix_norm_g", "q_norm_g", "kv_norm_g", "conv_b", "conv_ln_g", "conv_ln_b", "conv_out_g", "attn_out_g",
              "ffn_norm_g", "ffn_conv_b", "final_norm_g")
WEIGHTS = ("meta_tokens", "mix_norm_g", "w_in", "q_norm_g", "w_uq", "kv_norm_g", "w_ukv", "conv_w", "conv_b",
           "conv_ln_g", "conv_ln_b", "conv_out_g", "attn_out_g", "w_out", "ffn_norm_g", "w_ffn_up", "ffn_conv_w",
           "ffn_conv_b", "w_ffn_down", "final_norm_g")


def _lane_rows(a):
    return a.reshape(N_CHIPS, -1, LANES)


def _col_shards(a):
    k = a.shape[0]
    return a.reshape(k, N_CHIPS, -1).transpose(1, 0, 2)


def _from_col_shards(a):
    return a.transpose(1, 0, 2).reshape(a.shape[1], -1)


def _pad_rows_to(a, rows):
    return jnp.pad(a, ((0, 0), (0, rows - a.shape[1]), (0, 0)))


def _rope_tables(L):
    pos = (jnp.arange(L, dtype=jnp.int32) - DEAD).astype(F32)
    inv_freq = 1.0 / (ROPE_THETA ** (jnp.arange(0, QK_ROPE, 2, dtype=F32) / QK_ROPE))
    ang = pos[:, None] * inv_freq[None, :]
    cos, sin = jnp.cos(ang), jnp.sin(ang)
    half = QK_ROPE // 2
    z = lambda n: jnp.zeros((L, n), F32)
    rc = jnp.concatenate([jnp.ones((L, QK_NOPE), F32), cos, cos, z(HB - QK_NOPE - QK_ROPE)], 1)
    rs1 = jnp.concatenate([z(QK_NOPE), -sin, z(HB - QK_NOPE - half)], 1)
    rs2 = jnp.concatenate([z(QK_NOPE + half), sin, z(HB - QK_NOPE - QK_ROPE)], 1)
    return rc, rs1, rs2


def _pad_heads(g):
    return jnp.pad(g.reshape(N_HEADS, V_HEAD), ((0, 0), (HB - V_HEAD, 0))).reshape(1, D_HEADS)


def _unpad_heads(g):
    return g.reshape(N_HEADS, HB)[:, HB - V_HEAD:].reshape(1, N_HEADS * V_HEAD)


def _local_step(x, target, w, late_shards=None, reduce_first=False):
    S = x.shape[0]
    L = TM + S
    tl = L // 4
    d_qk = QK_NOPE + QK_ROPE
    win_n = w["w_in"]
    kr0 = D_AG + Q_LORA + KV_LORA
    win = jnp.concatenate([win_n[:, :kr0], jnp.zeros((D_MODEL, QK_NOPE), BF16), win_n[:, kr0:],
                           jnp.zeros((D_MODEL, HB - d_qk), BF16)], 1)
    wuq = jnp.pad(w["w_uq"].reshape(Q_LORA, N_HEADS, d_qk), ((0, 0), (0, 0), (0, HB - d_qk))).reshape(Q_LORA, D_HEADS)
    wukv = w["w_ukv"]
    ga = _pad_heads(w["attn_out_g"])
    gfin = w["final_norm_g"].reshape(1, D_MODEL)
    rc, rs1, rs2 = _rope_tables(L)
    head = jnp.concatenate([jnp.zeros((DEAD, D_MODEL), F32), w["meta_tokens"]], 0)

    h0, n, zag, u0, cq, ckv, qn, kvn, q, kv, kr, kvt = _fwd_in(x, head, w["mix_norm_g"], win, w["q_norm_g"], wuq,
                                                                w["kv_norm_g"], wukv, rc, rs1, rs2)
    u1, mixa = _fwd_conv(u0, w["conv_w"], w["conv_b"], w["conv_ln_g"], w["conv_ln_b"], w["conv_out_g"])
    if late_shards is None:
        o, lse, _ = _attn_fwd(q, kv, kvt, kr)
        wout_n, wup, wdown = w["w_out"], _col_shards(w["w_ffn_up"]), w["w_ffn_down"]
    else:
        o, lse, late = _attn_fwd(q, kv, kvt, kr, [late_shards[k] for k in LATE])
        wout_n, wup, wdown = _full_weight("w_out", late[0]), late[1], _full_weight("w_ffn_down", late[2])
    wout = jnp.concatenate([wout_n[:D_CONV], jnp.pad(wout_n[D_CONV:].reshape(N_HEADS, V_HEAD, D_MODEL),
                                                     ((0, 0), (HB - V_HEAD, 0), (0, 0))).reshape(D_HEADS, D_MODEL)], 0)
    mix, h1, n2 = _fwd_mix(h0, mixa, o, ga, wout, w["ffn_norm_g"])
    up0 = _mm(n2, wup, NN, BF16, tl, FF_TILE, D_MODEL, "ffn_up", b_slabs=True)
    dh2, loss, g_fin, up, act, dh2b = _fwd_ffn_loss(up0, w["ffn_conv_w"], w["ffn_conv_b"], wdown, h1, target, gfin)

    dup, g_fcb = _bwd_ffn_act(dh2b, up, wdown)
    dn2, dup0, g_fcw_tiles = _bwd_ffn_conv_up(dup, up0, w["ffn_conv_w"], wup, tl)
    g_fcw = jnp.sum(g_fcw_tiles, 0)
    g_wup = _mm(n2, dup0, TN, BF16, D_MODEL, FF_TILE, tl, "ffn_up_dw", by_col_tile=True)
    g_wdown = _mm(act, dh2b, TN, BF16, D_FF // 2, D_MODEL, tl, "ffn_down_dw").reshape(N_CHIPS, -1, D_MODEL)
    ffn = [g_wup, g_wdown] if reduce_first else []
    (dh1, dh1m, do, delta, du1, g_gffn, g_ga, g_og, g_lg, g_lb), theirs = _bwd_mix(
        dh2, dn2, h1, w["ffn_norm_g"], wout, o, ga, u1, w["conv_ln_g"], w["conv_ln_b"], w["conv_out_g"], ffn)
    g_wout = _mm(mix, dh1m, TN, BF16, D_MIX // 2, D_MODEL, tl, "out_dw")
    g_wout = jnp.concatenate([g_wout[:D_CONV], g_wout[D_CONV:].reshape(N_HEADS, HB, D_MODEL)[:, HB - V_HEAD:]
                              .reshape(N_HEADS * V_HEAD, D_MODEL)], 0).reshape(N_CHIPS, -1, D_MODEL)
    pair = ()
    if reduce_first:
        theirs += _sibling_exchange([g_wout], "grad_sibling_exchange_out")
        pair = _add_pairs(ffn + [g_wout], theirs, "first")
    dq, dkv, dkr, got = _attn_bwd(q, kv, kr, do, lse, delta, pair)
    halves = _add_all_chips(pair, got, "first") if reduce_first else ()
    (dzag, g_cw, g_cb), both = _bwd_conv(du1, u0, w["conv_w"], zag, halves)
    if reduce_first:
        g_wup, g_wdown, g_wout = _totals(both)
    dz, dqr, gx, dfirst, g_gq, g_gkv, g_gmix = _bwd_in(dzag, dq, dkv, dkr, cq, ckv, w["q_norm_g"], w["kv_norm_g"],
                                                      wuq, wukv, win, rc, rs1, rs2, h0, w["mix_norm_g"], dh1)
    g_win = _mm(n, dz, TN, BF16, D_MODEL, D_ZP // 2, tl, "in_dw")
    g_wuq = _mm(qn, dqr, TN, BF16, Q_LORA, D_HEADS, tl, "uq_dw")
    g_wukv = _col_shards(_mm(kvn, dkv, TN, BF16, KV_LORA, D_HEADS, tl, "ukv_dw"))

    grads = {
        "w_in": _col_shards(jnp.concatenate([g_win[:, :kr0], g_win[:, kr0 + QK_NOPE:kr0 + d_qk]], 1)),
        "w_uq": _col_shards(g_wuq.reshape(Q_LORA, N_HEADS, HB)[:, :, :d_qk].reshape(Q_LORA, N_HEADS * d_qk)),
        "w_ukv": g_wukv,
        "w_out": g_wout,
        "w_ffn_up": g_wup, "w_ffn_down": g_wdown, "conv_w": g_cw, "ffn_conv_w": g_fcw,
        "meta_tokens": dfirst[DEAD:], "mix_norm_g": g_gmix, "q_norm_g": g_gq, "kv_norm_g": g_gkv, "conv_b": g_cb,
        "conv_ln_g": g_lg, "conv_ln_b": g_lb, "conv_out_g": g_og, "attn_out_g": _unpad_heads(g_ga),
        "ffn_norm_g": g_gffn, "ffn_conv_b": g_fcb, "final_norm_g": g_fin,
    }
    return loss, gx, grads


ROW_SHARDED = ("w_out", "w_ffn_down")


TRANSPOSED = ("w_in", "w_uq")
REDUCED_FIRST = ("w_ffn_up", "w_ffn_down", "w_out")
LATE = ("w_out", "w_ffn_up", "w_ffn_down")


def _full_weight(name, by_chip):
    return by_chip.reshape(-1, by_chip.shape[-1]) if name in ROW_SHARDED else _from_col_shards(by_chip)


def kernel(x, meta_tokens, mix_norm_g, w_in, q_norm_g, w_uq, kv_norm_g, w_ukv, conv_w, conv_b, conv_ln_g, conv_ln_b, conv_out_g, attn_out_g, w_out, ffn_norm_g, w_ffn_up, ffn_conv_w, ffn_conv_b, w_ffn_down, final_norm_g, loss_target, m_meta_tokens, m_mix_norm_g, m_w_in, m_q_norm_g, m_w_uq, m_kv_norm_g, m_w_ukv, m_conv_w, m_conv_b, m_conv_ln_g, m_conv_ln_b, m_conv_out_g, m_attn_out_g, m_w_out, m_ffn_norm_g, m_w_ffn_up, m_ffn_conv_w, m_ffn_conv_b, m_w_ffn_down, m_final_norm_g, v_meta_tokens, v_mix_norm_g, v_w_in, v_q_norm_g, v_w_uq, v_kv_norm_g, v_w_ukv, v_conv_w, v_conv_b, v_conv_ln_g, v_conv_ln_b, v_conv_out_g, v_attn_out_g, v_w_out, v_ffn_norm_g, v_w_ffn_up, v_ffn_conv_w, v_ffn_conv_b, v_w_ffn_down, v_final_norm_g):
    local = dict(meta_tokens=meta_tokens, mix_norm_g=mix_norm_g, w_in=w_in[0], q_norm_g=q_norm_g, w_uq=w_uq[0],
                 kv_norm_g=kv_norm_g, w_ukv=w_ukv[0], conv_w=conv_w[0], conv_b=conv_b, conv_ln_g=conv_ln_g,
                 conv_ln_b=conv_ln_b, conv_out_g=conv_out_g, attn_out_g=attn_out_g, w_out=w_out[0],
                 ffn_norm_g=ffn_norm_g, w_ffn_up=w_ffn_up[0], ffn_conv_w=ffn_conv_w[0], ffn_conv_b=ffn_conv_b,
                 w_ffn_down=w_ffn_down[0], final_norm_g=final_norm_g.reshape(1, D_MODEL))
    ms = dict(zip(WEIGHTS, (m_meta_tokens, m_mix_norm_g, m_w_in, m_q_norm_g, m_w_uq, m_kv_norm_g, m_w_ukv, m_conv_w,
                            m_conv_b, m_conv_ln_g, m_conv_ln_b, m_conv_out_g, m_attn_out_g, m_w_out, m_ffn_norm_g,
                            m_w_ffn_up, m_ffn_conv_w, m_ffn_conv_b, m_w_ffn_down, m_final_norm_g)))
    vs = dict(zip(WEIGHTS, (v_meta_tokens, v_mix_norm_g, v_w_in, v_q_norm_g, v_w_uq, v_kv_norm_g, v_w_ukv, v_conv_w,
                            v_conv_b, v_conv_ln_g, v_conv_ln_b, v_conv_out_g, v_attn_out_g, v_w_out, v_ffn_norm_g,
                            v_w_ffn_up, v_ffn_conv_w, v_ffn_conv_b, v_w_ffn_down, v_final_norm_g)))

    small_flat = jnp.concatenate([local[k].reshape(-1) for k in SMALL_SHARDED]).reshape(-1, LANES)
    early = [k for k in BIG if k not in LATE]
    gathered = _all_gather([local[k].astype(BF16) for k in early] + [small_flat])
    full = {k: v for k, v in local.items() if k not in LATE}
    for name, g in zip(early, gathered[:len(early)]):
        full[name] = _full_weight(name, g)
    small = gathered[-1].reshape(N_CHIPS, -1)
    at = 0
    for name in SMALL_SHARDED:
        r, c = local[name].shape
        full[name] = _from_col_shards(small[:, at:at + r * c].reshape(N_CHIPS, r, c))
        at += r * c

    loss_row, grad_x, grads = _local_step(x[0], loss_target[0], full, {k: local[k].astype(BF16) for k in LATE},
                                          reduce_first=True)

    rest_big = [k for k in BIG if k not in REDUCED_FIRST]
    rep = jnp.concatenate([grads[k].reshape(-1) for k in REPLICATED] + [loss_row.reshape(-1)]).reshape(1, -1, LANES)
    small_pieces = [_lane_rows(_col_shards(grads[k])) for k in SMALL_SHARDED]
    small_pieces.append(jnp.broadcast_to(rep, (N_CHIPS,) + rep.shape[1:]))
    small_rows = sum(p.shape[1] for p in small_pieces)
    small_pack = _pad_rows_to(jnp.concatenate(small_pieces, 1), -(-small_rows // 32) * 32)
    pair = _reduce_begin([grads[k] for k in rest_big] + [small_pack], "rest")

    total = {k: grads[k] for k in REDUCED_FIRST}
    delta, new_m, new_v = {}, {}, {}
    shape2 = lambda a, name: a.reshape(local[name].shape)
    turn = lambda a, name: a.T if name in TRANSPOSED else a

    def update(name, parts=()):
        outs, got = _adamw_big(turn(local[name], name), turn(total[name], name), turn(shape2(ms[name], name), name),
                               turn(shape2(vs[name], name), name), "adamw_" + name, parts)
        total[name], delta[name], new_m[name], new_v[name] = (turn(o, name) for o in outs)
        return got

    *rest_tot, small_tot = _reduce_end(pair, update(REDUCED_FIRST[0], pair), "rest")
    total.update(zip(rest_big, rest_tot))
    flat = small_tot.reshape(-1)
    at = 0
    for name in SMALL_SHARDED + REPLICATED:
        shape = local[name].shape
        size = shape[0] * shape[1]
        total[name] = flat[at:at + size].reshape(shape)
        at += -(-size // LANES) * LANES if name in SMALL_SHARDED else size
    loss = flat[at]

    for name in BIG:
        if name != REDUCED_FIRST[0]:
            update(name)
    rest = SMALL_SHARDED + REPLICATED
    ds, nms, nvs = _adamw_small([local[k] for k in rest], [total[k] for k in rest],
                                [shape2(ms[k], k) for k in rest], [shape2(vs[k], k) for k in rest])
    for k, d, nm, nv in zip(rest, ds, nms, nvs):
        delta[k], new_m[k], new_v[k] = d, nm, nv

    out_shape = dict(zip(WEIGHTS, (meta_tokens, mix_norm_g, w_in, q_norm_g, w_uq, kv_norm_g, w_ukv, conv_w, conv_b,
                                   conv_ln_g, conv_ln_b, conv_out_g, attn_out_g, w_out, ffn_norm_g, w_ffn_up,
                                   ffn_conv_w, ffn_conv_b, w_ffn_down, final_norm_g)))
    outs = [loss, grad_x[None]]
    for group in (total, delta, new_m, new_v):
        outs += [group[k].reshape(out_shape[k].shape) for k in WEIGHTS]
    return tuple(outs)
```
